```python
import jax, jax.numpy as jnp
from jax import lax
import numpy as np

D_MODEL = 1024
BATCH = 8
SEQ = 8192
DEPTH = 4

D_MIX = D_MODEL
A_WIDTH = D_MIX // 2
B_WIDTH = D_MIX - A_WIDTH
A_HEADS = 4
A_HEAD_DIM = A_WIDTH // A_HEADS
CHUNK = 128
B_HEADS = 4
B_VAL_DIM = B_WIDTH // B_HEADS
B_KEY_DIM = B_VAL_DIM // 2
QK_WIDTH = B_HEADS * B_KEY_DIM
GATE_RANK = 16
GATE_NORMALIZER = 16.0
N_IN = 2 * A_WIDTH + 2 * QK_WIDTH + 2 * B_WIDTH + 2 * GATE_RANK
D_FF = 2816
CONV_WIDTH = 3
EPS = 1e-6

kernel_name = "hybrid_gmlp_gla_convffn_encoder"


def _split_points():
    sizes = [A_WIDTH, A_WIDTH, QK_WIDTH, QK_WIDTH, B_WIDTH, B_WIDTH, GATE_RANK, GATE_RANK]
    pts, acc = [], 0
    for s in sizes[:-1]:
        acc += s
        pts.append(acc)
    return pts


def _rmsnorm(x, g):
    xf = x.astype(jnp.float32)
    y = xf * lax.rsqrt(jnp.mean(xf * xf, axis=-1, keepdims=True) + EPS)
    return (y * g.astype(jnp.float32)).astype(x.dtype)


def _layernorm(x, g, b):
    xf = x.astype(jnp.float32)
    mu = jnp.mean(xf, axis=-1, keepdims=True)
    var = jnp.mean(jnp.square(xf - mu), axis=-1, keepdims=True)
    y = (xf - mu) * lax.rsqrt(var + EPS)
    return (y * g.astype(jnp.float32) + b.astype(jnp.float32)).astype(x.dtype)


def _spatial_gating(u, v, w_s, b_s, ln_g, ln_b):
    bsz, s = u.shape[0], u.shape[1]
    n = s // CHUNK
    vn = _layernorm(v, ln_g, ln_b).reshape(bsz, n, CHUNK, A_HEADS, A_HEAD_DIM)
    mixed = jnp.einsum('hij,bnjhd->bnihd', w_s, vn) + b_s.T[None, None, :, :, None]
    return u * mixed.reshape(bsz, s, A_HEADS, A_HEAD_DIM)


def _gla_chunked(q, k, v, log_a, include_diag):
    bsz, s, h, dk = q.shape
    dv = v.shape[-1]
    n = s // CHUNK
    f32 = jnp.float32
    qc = q.astype(f32).reshape(bsz, n, CHUNK, h, dk)
    kc = k.astype(f32).reshape(bsz, n, CHUNK, h, dk)
    vc = v.astype(f32).reshape(bsz, n, CHUNK, h, dv)
    cum = jnp.cumsum(log_a.astype(f32).reshape(bsz, n, CHUNK, h, dk), axis=2)
    cum_last = cum[:, :, -1:]
    q_dec = qc * jnp.exp(cum)
    k_inv = kc * jnp.exp(-cum)
    k_to_end = kc * jnp.exp(cum_last - cum)
    scores = jnp.einsum('bnihd,bnjhd->bnhij', q_dec, k_inv)
    mask = jnp.tril(jnp.ones((CHUNK, CHUNK), dtype=bool), k=0 if include_diag else -1)
    scores = jnp.where(mask, scores, 0.0)
    o_intra = jnp.einsum('bnhij,bnjhe->bnihe', scores, vc)
    d_state = jnp.einsum('bnjhd,bnjhe->bnhde', k_to_end, vc)
    chunk_decay = jnp.exp(cum_last[:, :, 0])

    def step(state, inp):
        ds, dec = inp
        return state * dec[..., None] + ds, state

    s0 = jnp.zeros((bsz, h, dk, dv), f32)
    _, states = lax.scan(step, s0, (jnp.swapaxes(d_state, 0, 1), jnp.swapaxes(chunk_decay, 0, 1)))
    states = jnp.swapaxes(states, 0, 1)
    o_inter = jnp.einsum('bnihd,bnhde->bnihe', q_dec, states)
    return (o_intra + o_inter).reshape(bsz, s, h, dv)


def _dwconv_centred(z, w, b):
    s = z.shape[1]
    half = CONV_WIDTH // 2
    zp = jnp.pad(z, ((0, 0), (half, half), (0, 0)))
    out = b
    for t in range(CONV_WIDTH):
        out = out + zp[:, t:t + s] * w[t]
    return out


def _fwd_setup_inputs(seed: int = 0) -> dict:
    key = jax.random.key(seed)
    ks = jax.random.split(key, 20)
    f32 = jnp.float32
    nrm = lambda k, shape, scale: jax.random.normal(k, shape, f32) * scale
    return {
        "x": nrm(ks[0], (BATCH, SEQ, D_MODEL), 1.0),
        "g_mix": 1.0 + nrm(ks[1], (DEPTH, D_MODEL), 0.02),
        "w_in": nrm(ks[2], (DEPTH, D_MODEL, N_IN), D_MODEL ** -0.5),
        "w_s": nrm(ks[3], (DEPTH, A_HEADS, CHUNK, CHUNK), CHUNK ** -0.5),
        "b_s": 1.0 + nrm(ks[4], (DEPTH, A_HEADS, CHUNK), 0.1),
        "ln_g": 1.0 + nrm(ks[5], (DEPTH, A_HEADS, A_HEAD_DIM), 0.02),
        "ln_b": nrm(ks[6], (DEPTH, A_HEADS, A_HEAD_DIM), 0.02),
        "w_gate_f": nrm(ks[7], (DEPTH, GATE_RANK, QK_WIDTH), GATE_RANK ** -0.5),
        "b_gate_f": nrm(ks[8], (DEPTH, QK_WIDTH), 0.1),
        "w_gate_b": nrm(ks[9], (DEPTH, GATE_RANK, QK_WIDTH), GATE_RANK ** -0.5),
        "b_gate_b": nrm(ks[10], (DEPTH, QK_WIDTH), 0.1),
        "g_gla": 1.0 + nrm(ks[11], (DEPTH, B_HEADS, B_VAL_DIM), 0.02),
        "w_out": nrm(ks[12], (DEPTH, D_MIX, D_MODEL), D_MIX ** -0.5),
        "g_ffn": 1.0 + nrm(ks[13], (DEPTH, D_MODEL), 0.02),
        "w_up": nrm(ks[14], (DEPTH, D_MODEL, 2 * D_FF), D_MODEL ** -0.5),
        "conv_w": nrm(ks[15], (DEPTH, CONV_WIDTH, 2 * D_FF), CONV_WIDTH ** -0.5),
        "conv_b": nrm(ks[16], (DEPTH, 2 * D_FF), 0.02),
        "w_down": nrm(ks[17], (DEPTH, D_FF, D_MODEL), D_FF ** -0.5),
        "g_final": 1.0 + nrm(ks[18], (D_MODEL,), 0.02),
    }


def _fwd_reference(x, g_mix, w_in, w_s, b_s, ln_g, ln_b, w_gate_f, b_gate_f, w_gate_b, b_gate_b,
              g_gla, w_out, g_ffn, w_up, conv_w, conv_b, w_down, g_final):
    bsz, s, _ = x.shape
    pts = _split_points()
    q_scale = B_KEY_DIM ** -0.5
    for l in range(DEPTH):
        h = _rmsnorm(x, g_mix[l])
        p = h @ w_in[l]
        pa_u, pa_v, pq, pk, pv, pg, r_f, r_b = jnp.split(p, pts, axis=-1)
        u = jax.nn.gelu(pa_u, approximate=False).reshape(bsz, s, A_HEADS, A_HEAD_DIM)
        va = jax.nn.gelu(pa_v, approximate=False).reshape(bsz, s, A_HEADS, A_HEAD_DIM)
        out_a = _spatial_gating(u, va, w_s[l], b_s[l], ln_g[l], ln_b[l])
        q = pq.reshape(bsz, s, B_HEADS, B_KEY_DIM) * q_scale
        k = pk.reshape(bsz, s, B_HEADS, B_KEY_DIM)
        vb = pv.reshape(bsz, s, B_HEADS, B_VAL_DIM)
        la_f = (jax.nn.log_sigmoid((r_f @ w_gate_f[l] + b_gate_f[l]).astype(jnp.float32))
                / GATE_NORMALIZER).reshape(bsz, s, B_HEADS, B_KEY_DIM)
        la_b = (jax.nn.log_sigmoid((r_b @ w_gate_b[l] + b_gate_b[l]).astype(jnp.float32))
                / GATE_NORMALIZER).reshape(bsz, s, B_HEADS, B_KEY_DIM)
        o_fwd = _gla_chunked(q, k, vb, la_f, True)
        o_bwd = jnp.flip(_gla_chunked(jnp.flip(q, 1), jnp.flip(k, 1), jnp.flip(vb, 1),
                                      jnp.flip(la_b, 1), False), 1)
        o = (o_fwd + o_bwd).astype(x.dtype)
        out_b = _rmsnorm(o, g_gla[l]) * jax.nn.silu(pg.reshape(bsz, s, B_HEADS, B_VAL_DIM))
        mixed = jnp.concatenate([out_a.reshape(bsz, s, A_WIDTH),
                                 out_b.reshape(bsz, s, B_WIDTH)], axis=-1)
        x = x + mixed @ w_out[l]
        h = _rmsnorm(x, g_ffn[l])
        z = _dwconv_centred(h @ w_up[l], conv_w[l], conv_b[l])
        z_gate, z_val = jnp.split(z, [D_FF], axis=-1)
        x = x + (jax.nn.silu(z_gate) * z_val) @ w_down[l]
    return _rmsnorm(x, g_final)


import jax as _jax
import jax.numpy as _jnp

TWIN_FORMAT = 'train_step'
FWD_PARAMS = ['x', 'g_mix', 'w_in', 'w_s', 'b_s', 'ln_g', 'ln_b', 'w_gate_f', 'b_gate_f', 'w_gate_b', 'b_gate_b', 'g_gla', 'w_out', 'g_ffn', 'w_up', 'conv_w', 'conv_b', 'w_down', 'g_final']
TWIN_WEIGHTS = ['g_mix', 'w_in', 'w_s', 'b_s', 'ln_g', 'ln_b', 'w_gate_f', 'b_gate_f', 'w_gate_b', 'b_gate_b', 'g_gla', 'w_out', 'g_ffn', 'w_up', 'conv_w', 'conv_b', 'w_down', 'g_final']
TWIN_DIFF_INPUT = 'x'
TWIN_INPUTS = ['x', 'g_mix', 'w_in', 'w_s', 'b_s', 'ln_g', 'ln_b', 'w_gate_f', 'b_gate_f', 'w_gate_b', 'b_gate_b', 'g_gla', 'w_out', 'g_ffn', 'w_up', 'conv_w', 'conv_b', 'w_down', 'g_final', 'loss_target', 'm_g_mix', 'm_w_in', 'm_w_s', 'm_b_s', 'm_ln_g', 'm_ln_b', 'm_w_gate_f', 'm_b_gate_f', 'm_w_gate_b', 'm_b_gate_b', 'm_g_gla', 'm_w_out', 'm_g_ffn', 'm_w_up', 'm_conv_w', 'm_conv_b', 'm_w_down', 'm_g_final', 'v_g_mix', 'v_w_in', 'v_w_s', 'v_b_s', 'v_ln_g', 'v_ln_b', 'v_w_gate_f', 'v_b_gate_f', 'v_w_gate_b', 'v_b_gate_b', 'v_g_gla', 'v_w_out', 'v_g_ffn', 'v_w_up', 'v_conv_w', 'v_conv_b', 'v_w_down', 'v_g_final']
TWIN_OUTPUTS = ['loss', 'grad_x', 'grad_g_mix', 'grad_w_in', 'grad_w_s', 'grad_b_s', 'grad_ln_g', 'grad_ln_b', 'grad_w_gate_f', 'grad_b_gate_f', 'grad_w_gate_b', 'grad_b_gate_b', 'grad_g_gla', 'grad_w_out', 'grad_g_ffn', 'grad_w_up', 'grad_conv_w', 'grad_conv_b', 'grad_w_down', 'grad_g_final', 'delta_g_mix', 'delta_w_in', 'delta_w_s', 'delta_b_s', 'delta_ln_g', 'delta_ln_b', 'delta_w_gate_f', 'delta_b_gate_f', 'delta_w_gate_b', 'delta_b_gate_b', 'delta_g_gla', 'delta_w_out', 'delta_g_ffn', 'delta_w_up', 'delta_conv_w', 'delta_conv_b', 'delta_w_down', 'delta_g_final', 'new_m_g_mix', 'new_m_w_in', 'new_m_w_s', 'new_m_b_s', 'new_m_ln_g', 'new_m_ln_b', 'new_m_w_gate_f', 'new_m_b_gate_f', 'new_m_w_gate_b', 'new_m_b_gate_b', 'new_m_g_gla', 'new_m_w_out', 'new_m_g_ffn', 'new_m_w_up', 'new_m_conv_w', 'new_m_conv_b', 'new_m_w_down', 'new_m_g_final', 'new_v_g_mix', 'new_v_w_in', 'new_v_w_s', 'new_v_b_s', 'new_v_ln_g', 'new_v_ln_b', 'new_v_w_gate_f', 'new_v_b_gate_f', 'new_v_w_gate_b', 'new_v_b_gate_b', 'new_v_g_gla', 'new_v_w_out', 'new_v_g_ffn', 'new_v_w_up', 'new_v_conv_w', 'new_v_conv_b', 'new_v_w_down', 'new_v_g_final']
TWIN_LEAF_KINDS = {'loss': 'loss', 'grad_x': 'grad_x', 'grad_g_mix': 'grad_w', 'grad_w_in': 'grad_w', 'grad_w_s': 'grad_w', 'grad_b_s': 'grad_w', 'grad_ln_g': 'grad_w', 'grad_ln_b': 'grad_w', 'grad_w_gate_f': 'grad_w', 'grad_b_gate_f': 'grad_w', 'grad_w_gate_b': 'grad_w', 'grad_b_gate_b': 'grad_w', 'grad_g_gla': 'grad_w', 'grad_w_out': 'grad_w', 'grad_g_ffn': 'grad_w', 'grad_w_up': 'grad_w', 'grad_conv_w': 'grad_w', 'grad_conv_b': 'grad_w', 'grad_w_down': 'grad_w', 'grad_g_final': 'grad_w', 'delta_g_mix': 'delta_w', 'delta_w_in': 'delta_w', 'delta_w_s': 'delta_w', 'delta_b_s': 'delta_w', 'delta_ln_g': 'delta_w', 'delta_ln_b': 'delta_w', 'delta_w_gate_f': 'delta_w', 'delta_b_gate_f': 'delta_w', 'delta_w_gate_b': 'delta_w', 'delta_b_gate_b': 'delta_w', 'delta_g_gla': 'delta_w', 'delta_w_out': 'delta_w', 'delta_g_ffn': 'delta_w', 'delta_w_up': 'delta_w', 'delta_conv_w': 'delta_w', 'delta_conv_b': 'delta_w', 'delta_w_down': 'delta_w', 'delta_g_final': 'delta_w', 'new_m_g_mix': 'new_m', 'new_m_w_in': 'new_m', 'new_m_w_s': 'new_m', 'new_m_b_s': 'new_m', 'new_m_ln_g': 'new_m', 'new_m_ln_b': 'new_m', 'new_m_w_gate_f': 'new_m', 'new_m_b_gate_f': 'new_m', 'new_m_w_gate_b': 'new_m', 'new_m_b_gate_b': 'new_m', 'new_m_g_gla': 'new_m', 'new_m_w_out': 'new_m', 'new_m_g_ffn': 'new_m', 'new_m_w_up': 'new_m', 'new_m_conv_w': 'new_m', 'new_m_conv_b': 'new_m', 'new_m_w_down': 'new_m', 'new_m_g_final': 'new_m', 'new_v_g_mix': 'new_v', 'new_v_w_in': 'new_v', 'new_v_w_s': 'new_v', 'new_v_b_s': 'new_v', 'new_v_ln_g': 'new_v', 'new_v_ln_b': 'new_v', 'new_v_w_gate_f': 'new_v', 'new_v_b_gate_f': 'new_v', 'new_v_w_gate_b': 'new_v', 'new_v_b_gate_b': 'new_v', 'new_v_g_gla': 'new_v', 'new_v_w_out': 'new_v', 'new_v_g_ffn': 'new_v', 'new_v_w_up': 'new_v', 'new_v_conv_w': 'new_v', 'new_v_conv_b': 'new_v', 'new_v_w_down': 'new_v', 'new_v_g_final': 'new_v'}


def _forward(args):
    return _fwd_reference(*[args[k] for k in FWD_PARAMS])


def _output_shape():
    def fwd():
        inp = _fwd_setup_inputs(0)
        return _fwd_reference(*[inp[k] for k in FWD_PARAMS])
    out = _jax.eval_shape(fwd)
    return out.shape, out.dtype

N_MICROBATCH = 1
ADAM_LR = 0.001
ADAM_B1 = 0.9
ADAM_B2 = 0.999
ADAM_EPS = 1e-08
ADAM_WD = 0.01
ADAM_STEP = 10
PER_EXAMPLE_BATCH_AXIS = {'x': 0, 'loss_target': 0}
SHARED_INPUTS = []
_WEIGHT_DTYPES = {'g_mix': _jnp.float32, 'w_in': _jnp.float32, 'w_s': _jnp.float32, 'b_s': _jnp.float32, 'ln_g': _jnp.float32, 'ln_b': _jnp.float32, 'w_gate_f': _jnp.float32, 'b_gate_f': _jnp.float32, 'w_gate_b': _jnp.float32, 'b_gate_b': _jnp.float32, 'g_gla': _jnp.float32, 'w_out': _jnp.float32, 'g_ffn': _jnp.float32, 'w_up': _jnp.float32, 'conv_w': _jnp.float32, 'conv_b': _jnp.float32, 'w_down': _jnp.float32, 'g_final': _jnp.float32}
MOMENT_SCALE = {'g_mix': 2.886027e-01, 'w_in': 1.751605e-01, 'w_s': 1.489962e-01, 'b_s': 1.507616e-01, 'ln_g': 1.482219e-01, 'ln_b': 1.516486e-01, 'w_gate_f': 1.691864e-02, 'b_gate_f': 6.125499e-02, 'w_gate_b': 1.645344e-02, 'b_gate_b': 6.145525e-02, 'g_gla': 1.542668e-01, 'w_out': 1.825856e-01, 'g_ffn': 1.682777e-01, 'w_up': 7.347199e-02, 'conv_w': 7.361783e-02, 'conv_b': 7.668650e-02, 'w_down': 1.196938e-01, 'g_final': 6.406546e+01}


def _to_microbatches(a, axis):
    t = _jnp.moveaxis(a, axis, 0)
    t = t.reshape((N_MICROBATCH, t.shape[0] // N_MICROBATCH) + t.shape[1:])
    return _jnp.moveaxis(t, 1, axis + 1)


def setup_inputs(seed: int = 0) -> dict:
    inp = _fwd_setup_inputs(seed)
    key = _jax.random.fold_in(_jax.random.key(seed), 7919)
    shape, _ = _output_shape()
    out = dict(inp)
    out["loss_target"] = _jax.random.normal(_jax.random.fold_in(key, 0), shape, _jnp.float32)
    for i, name in enumerate(TWIN_WEIGHTS):
        w = inp[name].astype(_jnp.float32)
        if MOMENT_SCALE is None:
            s = _jnp.sqrt(_jnp.mean(_jnp.square(w)) + 1e-30)
        else:
            s = MOMENT_SCALE[name]
        km, kv = _jax.random.split(_jax.random.fold_in(key, i + 1))
        out[name] = w
        out["m_" + name] = s * _jax.random.normal(km, w.shape, _jnp.float32)
        out["v_" + name] = (s * s) * _jax.random.uniform(kv, w.shape, _jnp.float32, 0.5, 1.5)
    if N_MICROBATCH > 1:
        for name, axis in PER_EXAMPLE_BATCH_AXIS.items():
            out[name] = _to_microbatches(out[name], axis)
    return {'x': out['x'], 'g_mix': out['g_mix'], 'w_in': out['w_in'], 'w_s': out['w_s'], 'b_s': out['b_s'], 'ln_g': out['ln_g'], 'ln_b': out['ln_b'], 'w_gate_f': out['w_gate_f'], 'b_gate_f': out['b_gate_f'], 'w_gate_b': out['w_gate_b'], 'b_gate_b': out['b_gate_b'], 'g_gla': out['g_gla'], 'w_out': out['w_out'], 'g_ffn': out['g_ffn'], 'w_up': out['w_up'], 'conv_w': out['conv_w'], 'conv_b': out['conv_b'], 'w_down': out['w_down'], 'g_final': out['g_final'], 'loss_target': out['loss_target'], 'm_g_mix': out['m_g_mix'], 'm_w_in': out['m_w_in'], 'm_w_s': out['m_w_s'], 'm_b_s': out['m_b_s'], 'm_ln_g': out['m_ln_g'], 'm_ln_b': out['m_ln_b'], 'm_w_gate_f': out['m_w_gate_f'], 'm_b_gate_f': out['m_b_gate_f'], 'm_w_gate_b': out['m_w_gate_b'], 'm_b_gate_b': out['m_b_gate_b'], 'm_g_gla': out['m_g_gla'], 'm_w_out': out['m_w_out'], 'm_g_ffn': out['m_g_ffn'], 'm_w_up': out['m_w_up'], 'm_conv_w': out['m_conv_w'], 'm_conv_b': out['m_conv_b'], 'm_w_down': out['m_w_down'], 'm_g_final': out['m_g_final'], 'v_g_mix': out['v_g_mix'], 'v_w_in': out['v_w_in'], 'v_w_s': out['v_w_s'], 'v_b_s': out['v_b_s'], 'v_ln_g': out['v_ln_g'], 'v_ln_b': out['v_ln_b'], 'v_w_gate_f': out['v_w_gate_f'], 'v_b_gate_f': out['v_b_gate_f'], 'v_w_gate_b': out['v_w_gate_b'], 'v_b_gate_b': out['v_b_gate_b'], 'v_g_gla': out['v_g_gla'], 'v_w_out': out['v_w_out'], 'v_g_ffn': out['v_g_ffn'], 'v_w_up': out['v_w_up'], 'v_conv_w': out['v_conv_w'], 'v_conv_b': out['v_conv_b'], 'v_w_down': out['v_w_down'], 'v_g_final': out['v_g_final']}


def _loss(weights, diff, rest, loss_target):
    with _jax.named_scope("forward"):
        args = {**rest, TWIN_DIFF_INPUT: diff, **{k: w.astype(_WEIGHT_DTYPES[k]) for k, w in weights.items()}}
        y = _forward(args)
    with _jax.named_scope("loss_head"):
        err = _jnp.square(y.astype(_jnp.float32) - loss_target)
        return 0.5 * _jnp.sum(_jnp.mean(err, axis=-1)) if err.ndim else 0.5 * err


def _adamw(w, g, m, v):
    m = ADAM_B1 * m + (1.0 - ADAM_B1) * g
    v = ADAM_B2 * v + (1.0 - ADAM_B2) * _jnp.square(g)
    m_hat = m / (1.0 - ADAM_B1 ** ADAM_STEP)
    v_hat = v / (1.0 - ADAM_B2 ** ADAM_STEP)
    delta = -ADAM_LR * (m_hat / (_jnp.sqrt(v_hat) + ADAM_EPS) + ADAM_WD * w)
    return delta, m, v


def reference(x, g_mix, w_in, w_s, b_s, ln_g, ln_b, w_gate_f, b_gate_f, w_gate_b, b_gate_b, g_gla, w_out, g_ffn, w_up, conv_w, conv_b, w_down, g_final, loss_target, m_g_mix, m_w_in, m_w_s, m_b_s, m_ln_g, m_ln_b, m_w_gate_f, m_b_gate_f, m_w_gate_b, m_b_gate_b, m_g_gla, m_w_out, m_g_ffn, m_w_up, m_conv_w, m_conv_b, m_w_down, m_g_final, v_g_mix, v_w_in, v_w_s, v_b_s, v_ln_g, v_ln_b, v_w_gate_f, v_b_gate_f, v_w_gate_b, v_b_gate_b, v_g_gla, v_w_out, v_g_ffn, v_w_up, v_conv_w, v_conv_b, v_w_down, v_g_final):
    given = dict(x=x, g_mix=g_mix, w_in=w_in, w_s=w_s, b_s=b_s, ln_g=ln_g, ln_b=ln_b, w_gate_f=w_gate_f, b_gate_f=b_gate_f, w_gate_b=w_gate_b, b_gate_b=b_gate_b, g_gla=g_gla, w_out=w_out, g_ffn=g_ffn, w_up=w_up, conv_w=conv_w, conv_b=conv_b, w_down=w_down, g_final=g_final, loss_target=loss_target, m_g_mix=m_g_mix, m_w_in=m_w_in, m_w_s=m_w_s, m_b_s=m_b_s, m_ln_g=m_ln_g, m_ln_b=m_ln_b, m_w_gate_f=m_w_gate_f, m_b_gate_f=m_b_gate_f, m_w_gate_b=m_w_gate_b, m_b_gate_b=m_b_gate_b, m_g_gla=m_g_gla, m_w_out=m_w_out, m_g_ffn=m_g_ffn, m_w_up=m_w_up, m_conv_w=m_conv_w, m_conv_b=m_conv_b, m_w_down=m_w_down, m_g_final=m_g_final, v_g_mix=v_g_mix, v_w_in=v_w_in, v_w_s=v_w_s, v_b_s=v_b_s, v_ln_g=v_ln_g, v_ln_b=v_ln_b, v_w_gate_f=v_w_gate_f, v_b_gate_f=v_b_gate_f, v_w_gate_b=v_w_gate_b, v_b_gate_b=v_b_gate_b, v_g_gla=v_g_gla, v_w_out=v_w_out, v_g_ffn=v_g_ffn, v_w_up=v_w_up, v_conv_w=v_conv_w, v_conv_b=v_conv_b, v_w_down=v_w_down, v_g_final=v_g_final)
    weights = {n: given[n] for n in TWIN_WEIGHTS}
    shared = {n: given[n] for n in SHARED_INPUTS}
    per_example = {n: given[n] for n in ['x']}
    grad_fn = _jax.value_and_grad(_loss, argnums=(0, 1))

    def one_microbatch(ex, loss_target):
        ex = dict(ex)
        diff = ex.pop(TWIN_DIFF_INPUT)
        return grad_fn(weights, diff, {**shared, **ex}, loss_target)

    if N_MICROBATCH == 1:
        loss, (grad_w, grad_x) = one_microbatch(per_example, given["loss_target"])
    else:
        def body(carry, xs):
            loss_sum, grad_sum = carry
            l_k, (gw_k, gx_k) = one_microbatch(xs[0], xs[1])
            with _jax.named_scope("update"):
                return (loss_sum + l_k, _jax.tree.map(_jnp.add, grad_sum, gw_k)), gx_k

        init = (_jnp.zeros((), _jnp.float32), _jax.tree.map(_jnp.zeros_like, weights))
        (loss, grad_w), grad_x = _jax.lax.scan(body, init, (per_example, given["loss_target"]))
    with _jax.named_scope("update"):
        delta_w, new_m, new_v = {}, {}, {}
        for n in TWIN_WEIGHTS:
            delta_w[n], new_m[n], new_v[n] = _adamw(weights[n], grad_w[n], given["m_" + n], given["v_" + n])
    return (loss, grad_x, *[grad_w[n] for n in TWIN_WEIGHTS], *[delta_w[n] for n in TWIN_WEIGHTS],
            *[new_m[n] for n in TWIN_WEIGHTS], *[new_v[n] for n in TWIN_WEIGHTS])
```

```python
import functools

import jax
import jax.numpy as jnp
from jax import lax
from jax.experimental import pallas as pl
from jax.experimental.pallas import tpu as pltpu

F32 = jnp.float32
BF16 = jnp.bfloat16
MX = BF16

D = 1024
CH = 128
NL = 4
N_IN = 2592
N_INP = 2688
NUP = 5632
DFF = 2816
EPS = 1e-6
VMEM_LIMIT = 56 * 1024 * 1024

ADAM_LR, ADAM_B1, ADAM_B2, ADAM_EPS, ADAM_WD, ADAM_STEP = 0.001, 0.9, 0.999, 1e-08, 0.01, 10


def _dg(a, b, ca, cb):
    return lax.dot_general(a.astype(MX), b.astype(MX), (((ca,), (cb,)), ((), ())), preferred_element_type=F32)


@jax.custom_vjp
def mm(a, b):
    return _dg(a, b, 1, 0)


mm.defvjp(lambda a, b: (_dg(a, b, 1, 0), (a, b)),
          lambda r, g: (_dg(g, r[1], 1, 1), _dg(r[0], g, 0, 0)))


@jax.custom_vjp
def mm_nt(a, b):
    return _dg(a, b, 1, 1)


mm_nt.defvjp(lambda a, b: (_dg(a, b, 1, 1), (a, b)),
             lambda r, g: (_dg(g, r[1], 1, 0), _dg(g, r[0], 0, 0)))


@jax.custom_vjp
def mm_tn(a, b):
    return _dg(a, b, 0, 0)


mm_tn.defvjp(lambda a, b: (_dg(a, b, 0, 0), (a, b)),
             lambda r, g: (_dg(r[1], g, 1, 1), _dg(r[0], g, 1, 0)))


def _split3(x):
    hi = x.astype(BF16)
    r1 = x - hi.astype(F32)
    mid = r1.astype(BF16)
    lo = (r1 - mid.astype(F32)).astype(BF16)
    return hi, mid, lo


def _dot3(m, x):
    hi, mid, lo = _split3(x)
    d = lambda p: lax.dot_general(m, p, (((1,), (0,)), ((), ())), preferred_element_type=F32)
    return d(hi) + d(mid) + d(lo)


@jax.custom_vjp
def cumdot(m, mt, x):
    return _dot3(m, x)


cumdot.defvjp(lambda m, mt, x: (_dot3(m, x), (m, mt)),
              lambda r, g: (jnp.zeros_like(r[0]), jnp.zeros_like(r[1]), _dot3(r[1], g)))


def rmsnorm(x, g):
    return x * lax.rsqrt(jnp.mean(x * x, axis=-1, keepdims=True) + EPS) * g


def gelu(x):
    return 0.5 * x * (1.0 + lax.erf(x * 0.7071067811865476))


def sigmoid(x):
    return 1.0 / (1.0 + jnp.exp(-x))


def log_sigmoid(x):
    return jnp.minimum(x, 0.0) - jnp.log(1.0 + jnp.exp(-jnp.abs(x)))


def gmlp_head(u_pre, v_pre, w, bcol, g, b):
    u = gelu(u_pre)
    v = gelu(v_pre)
    mu = jnp.mean(v, axis=-1, keepdims=True)
    var = jnp.mean(jnp.square(v - mu), axis=-1, keepdims=True)
    vn = (v - mu) * lax.rsqrt(var + EPS) * g + b
    return u * (mm(w, vn) + bcol)


def outb_head(o, pg, g):
    return rmsnorm(o, g) * (pg * sigmoid(pg))


def ffn_act(zg, zv):
    return zg * sigmoid(zg) * zv


def _tri(reverse):
    r = lax.broadcasted_iota(jnp.int32, (CH, CH), 0)
    c = lax.broadcasted_iota(jnp.int32, (CH, CH), 1)
    if reverse:
        cm, sm = c >= r, c > r
    else:
        cm, sm = c <= r, c <= r
    one = jnp.ones((), BF16)
    zero = jnp.zeros((), BF16)
    return jnp.where(cm, one, zero), jnp.where(cm.T, one, zero), sm


def gla_pair(consts, pr, wg, bg, qp, kp, v0, v1, st0, st1):
    m, mt, smask, lm0, lm1 = consts
    la = log_sigmoid(mm(pr, wg) + bg) * (1.0 / 16.0)
    cum = cumdot(m, mt, la)
    tot = jnp.sum(la, axis=0, keepdims=True)
    q_dec = (qp * 0.125) * jnp.exp(cum)
    k_inv = kp * jnp.exp(-cum)
    k_end = kp * jnp.exp(tot - cum)
    dec = jnp.exp(tot)
    outs = []
    for lm, v, st in ((lm0, v0, st0), (lm1, v1, st1)):
        s = jnp.where(smask, mm_nt(q_dec * lm, k_inv), 0.0)
        o = mm(s, v) + mm_nt(q_dec, st)
        st_new = st * dec + mm_tn(v, k_end * lm)
        outs += [o, st_new]
    return outs[0], outs[2], outs[1], outs[3]


def _lane_masks():
    lane = lax.broadcasted_iota(jnp.int32, (1, 128), 1)
    return (lane < 64).astype(F32), (lane >= 64).astype(F32)


def _cparams(n_axes=1):
    return pltpu.CompilerParams(dimension_semantics=("arbitrary",) * n_axes, vmem_limit_bytes=VMEM_LIMIT)


def _full(a):
    nd = a.ndim
    return pl.BlockSpec(a.shape, lambda *_: (0,) * nd)


def _rows(tm, w, cb=0, rev_n=None):
    if rev_n is None:
        return pl.BlockSpec((tm, w), lambda i: (i, cb))
    return pl.BlockSpec((tm, w), lambda i: (rev_n - 1 - i, cb))


def _call(body, name, grid, in_specs, out_specs, out_shape, scratch=(), n_axes=1):
    return pl.pallas_call(body, name=name, grid=grid, in_specs=in_specs, out_specs=out_specs, out_shape=out_shape,
                          scratch_shapes=list(scratch), compiler_params=_cparams(n_axes))


def _sds(shape, dt=F32):
    return jax.ShapeDtypeStruct(shape, dt)


def norm_matmul(x, g, w, tm, name):
    T, n = x.shape[0], w.shape[1]

    def body(x_ref, g_ref, w_ref, y_ref, h_ref):
        hb = rmsnorm(x_ref[...], g_ref[...]).astype(MX)
        h_ref[...] = hb
        y_ref[...] = jnp.dot(hb, w_ref[...], preferred_element_type=F32)

    return _call(body, name, (T // tm,), [_rows(tm, D), _full(g), _full(w)],
                 [_rows(tm, n), _rows(tm, D)], [_sds((T, n)), _sds((T, D), MX)])(x, g, w)


def gmlp_fwd(p, ws, bs, lg, lb):
    T = p.shape[0]

    def body(pa_ref, ws_ref, bs_ref, lg_ref, lb_ref, o_ref):
        for h in range(4):
            o_ref[:, h * 128:(h + 1) * 128] = gmlp_head(
                pa_ref[:, h * 128:(h + 1) * 128], pa_ref[:, 512 + h * 128:512 + (h + 1) * 128],
                ws_ref[h], bs_ref[h], lg_ref[h], lb_ref[h]).astype(MX)

    return _call(body, "gmlp_fwd", (T // CH,), [_rows(CH, 1024), _full(ws), _full(bs), _full(lg), _full(lb)],
                 _rows(CH, 512), _sds((T, 512), MX))(p, ws, bs, lg, lb)


def _gla_in_specs(n, rev):
    r = n if rev else None
    return [_rows(CH, 256, 4, r), _rows(CH, 256, 5, r), _rows(CH, 512, 3, r), _rows(CH, 128, 20, r)]


def gla_fwd(p, wg, bg, reverse):
    T = p.shape[0]
    n = T // CH
    rev = n if reverse else None

    def body(q_ref, k_ref, v_ref, r_ref, wg_ref, bg_ref, o_ref, ss_ref, st_ref):
        @pl.when(pl.program_id(0) == 0)
        def _():
            st_ref[...] = jnp.zeros_like(st_ref)

        consts = _tri(reverse) + _lane_masks()
        for j in range(2):
            sl = slice(j * 128, (j + 1) * 128)
            st0, st1 = st_ref[2 * j], st_ref[2 * j + 1]
            ss_ref[0, 2 * j] = st0
            ss_ref[0, 2 * j + 1] = st1
            o0, o1, n0, n1 = gla_pair(consts, r_ref[...], wg_ref[:, sl], bg_ref[:, sl], q_ref[:, sl], k_ref[:, sl],
                                      v_ref[:, 256 * j:256 * j + 128], v_ref[:, 256 * j + 128:256 * j + 256], st0, st1)
            o_ref[:, 256 * j:256 * j + 128] = o0
            o_ref[:, 256 * j + 128:256 * j + 256] = o1
            st_ref[2 * j] = n0
            st_ref[2 * j + 1] = n1

    ss_spec = pl.BlockSpec((1, 4, 128, 128), (lambda i: (n - 1 - i, 0, 0, 0)) if reverse else (lambda i: (i, 0, 0, 0)))
    return _call(body, "gla_fwd_r" if reverse else "gla_fwd_f", (n,),
                 _gla_in_specs(n, reverse) + [_full(wg), _full(bg)],
                 [_rows(CH, 512, 0, rev), ss_spec], [_sds((T, 512)), _sds((n, 4, 128, 128))],
                 scratch=[pltpu.VMEM((4, 128, 128), F32)])(p, p, p, p, wg, bg)


def mix_out(x, of, ob, p, outa, gg, w_out, tm):
    T = x.shape[0]

    def body(x_ref, of_ref, ob_ref, pg_ref, oa_ref, gg_ref, w_ref, x1_ref, mx_ref):
        mx_ref[:, 0:512] = oa_ref[...]
        for h in range(4):
            sl = slice(h * 128, (h + 1) * 128)
            mx_ref[:, 512 + h * 128:512 + (h + 1) * 128] = outb_head(
                of_ref[:, sl] + ob_ref[:, sl], pg_ref[:, sl], gg_ref[h]).astype(MX)
        x1_ref[...] = x_ref[...] + jnp.dot(mx_ref[...], w_ref[...], preferred_element_type=F32)

    return _call(body, "mix_out", (T // tm,),
                 [_rows(tm, D), _rows(tm, 512), _rows(tm, 512), _rows(tm, 512, 4), _rows(tm, 512), _full(gg), _full(w_out)],
                 [_rows(tm, D), _rows(tm, 1024)], [_sds((T, D)), _sds((T, 1024), MX)])(x, of, ob, p, outa, gg, w_out)


def _halo_specs(T, tm, w):
    nb = T // 8
    r = tm // 8
    return [pl.BlockSpec((tm, w), lambda i: (i, 0)),
            pl.BlockSpec((8, w), lambda i: (jnp.maximum(i * r - 1, 0), 0)),
            pl.BlockSpec((8, w), lambda i: (jnp.minimum((i + 1) * r, nb - 1), 0))]


def _shifted(main, prev8, next8, i, nsteps):
    tm = main.shape[0]
    row = lax.broadcasted_iota(jnp.int32, (tm, 1), 0)
    pr = jnp.where(i > 0, prev8[7:8, :], 0.0)
    nx = jnp.where(i < nsteps - 1, next8[0:1, :], 0.0)
    dn = jnp.where(row == 0, pr, pltpu.roll(main, 1, 0))
    up = jnp.where(row == tm - 1, nx, pltpu.roll(main, tm - 1, 0))
    return dn, up


def _conv(zu, zp8, zn8, cw_ref, cb_ref, i, nsteps):
    dn, up = _shifted(zu, zp8, zn8, i, nsteps)
    return cb_ref[...] + dn * cw_ref[0:1, :] + zu * cw_ref[1:2, :] + up * cw_ref[2:3, :]


def ffn_down(x1, zu, cw, cb, w_down, tm):
    T = x1.shape[0]
    ns = T // tm

    def body(x_ref, zu_ref, zp_ref, zn_ref, cw_ref, cb_ref, w_ref, x2_ref, a_ref):
        z = _conv(zu_ref[...], zp_ref[...], zn_ref[...], cw_ref, cb_ref, pl.program_id(0), ns)
        a = ffn_act(z[:, :DFF], z[:, DFF:]).astype(MX)
        a_ref[...] = a
        x2_ref[...] = x_ref[...] + jnp.dot(a, w_ref[...], preferred_element_type=F32)

    return _call(body, "ffn_down", (ns,), [_rows(tm, D)] + _halo_specs(T, tm, NUP) + [_full(cw), _full(cb), _full(w_down)],
                 [_rows(tm, D), _rows(tm, DFF)], [_sds((T, D)), _sds((T, DFF), MX)])(x1, zu, zu, zu, cw, cb, w_down)


def loss_head(x, g, tgt, tm):
    T = x.shape[0]

    def body(x_ref, g_ref, t_ref, l_ref, dx_ref, dg_ref):
        @pl.when(pl.program_id(0) == 0)
        def _():
            l_ref[...] = jnp.zeros_like(l_ref)
            dg_ref[...] = jnp.zeros_like(dg_ref)

        y, vjp = jax.vjp(rmsnorm, x_ref[...], g_ref[...])
        err = y - t_ref[...]
        l_ref[...] += jnp.sum(err * err, axis=0, keepdims=True)
        dx, dg = vjp(err * (1.0 / D))
        dx_ref[...] = dx
        dg_ref[...] += dg

    return _call(body, "loss_head", (T // tm,), [_rows(tm, D), _full(g), _rows(tm, D)],
                 [_full(g), _rows(tm, D), _full(g)], [_sds((1, D)), _sds((T, D)), _sds((1, D))])(x, g, tgt)


def ffn_down_bwd(dx2, zu, cw, cb, w_down, tm):
    T = dx2.shape[0]
    ns = T // tm

    def body(dx_ref, zu_ref, zp_ref, zn_ref, cw_ref, cb_ref, w_ref, dz_ref):
        z = _conv(zu_ref[...], zp_ref[...], zn_ref[...], cw_ref, cb_ref, pl.program_id(0), ns)
        da = _dg(dx_ref[...], w_ref[...], 1, 1)
        _, vjp = jax.vjp(ffn_act, z[:, :DFF], z[:, DFF:])
        dzg, dzv = vjp(da)
        dz_ref[:, :DFF] = dzg
        dz_ref[:, DFF:] = dzv

    return _call(body, "ffn_down_bwd", (ns,), [_rows(tm, D)] + _halo_specs(T, tm, NUP) + [_full(cw), _full(cb), _full(w_down)],
                 _rows(tm, NUP), _sds((T, NUP)))(dx2, zu, zu, zu, cw, cb, w_down)


def ffn_conv_bwd(dz, zu, cw, tm):
    T = dz.shape[0]
    ns = T // tm

    def body(dz_ref, dp_ref, dn_ref, zu_ref, zp_ref, zn_ref, cw_ref, dzu_ref, dcw_ref, dcb_ref):
        i = pl.program_id(0)

        @pl.when(i == 0)
        def _():
            dcw_ref[...] = jnp.zeros_like(dcw_ref)
            dcb_ref[...] = jnp.zeros_like(dcb_ref)

        dz = dz_ref[...]
        zu = zu_ref[...]
        ddn, dup = _shifted(dz, dp_ref[...], dn_ref[...], i, ns)
        zdn, zup = _shifted(zu, zp_ref[...], zn_ref[...], i, ns)
        dzu_ref[...] = (dup * cw_ref[0:1, :] + dz * cw_ref[1:2, :] + ddn * cw_ref[2:3, :]).astype(MX)
        dcw_ref[0:1, :] += jnp.sum(zdn * dz, axis=0, keepdims=True)
        dcw_ref[1:2, :] += jnp.sum(zu * dz, axis=0, keepdims=True)
        dcw_ref[2:3, :] += jnp.sum(zup * dz, axis=0, keepdims=True)
        dcb_ref[...] += jnp.sum(dz, axis=0, keepdims=True)

    return _call(body, "ffn_conv_bwd", (ns,), _halo_specs(T, tm, NUP) + _halo_specs(T, tm, NUP) + [_full(cw)],
                 [_rows(tm, NUP), _full(cw), pl.BlockSpec((1, NUP), lambda i: (0, 0))],
                 [_sds((T, NUP), MX), _sds((3, NUP)), _sds((1, NUP))])(dz, dz, dz, zu, zu, zu, cw)


def nt_normbwd(dy, w, x, g, dres, tm, name):
    T, k = dy.shape

    def body(dy_ref, w_ref, x_ref, g_ref, dr_ref, dx_ref, dg_ref):
        @pl.when(pl.program_id(0) == 0)
        def _():
            dg_ref[...] = jnp.zeros_like(dg_ref)

        dh = _dg(dy_ref[...], w_ref[...], 1, 1)
        _, vjp = jax.vjp(rmsnorm, x_ref[...], g_ref[...])
        dx, dg = vjp(dh)
        dx_ref[...] = dr_ref[...] + dx
        dg_ref[...] += dg

    return _call(body, name, (T // tm,), [_rows(tm, k), _full(w), _rows(tm, D), _full(g), _rows(tm, D)],
                 [_rows(tm, D), _full(g)], [_sds((T, D)), _sds((1, D))])(dy, w, x, g, dres)


def matmul_tn(a, b, tt, tn, name):
    T, k = a.shape
    n = b.shape[1]

    def body(a_ref, b_ref, o_ref):
        @pl.when(pl.program_id(1) == 0)
        def _():
            o_ref[...] = jnp.zeros_like(o_ref)

        o_ref[...] += _dg(a_ref[...], b_ref[...], 0, 0)

    return _call(body, name, (n // tn, T // tt),
                 [pl.BlockSpec((tt, k), lambda j, i: (i, 0)), pl.BlockSpec((tt, tn), lambda j, i: (i, j))],
                 pl.BlockSpec((k, tn), lambda j, i: (0, j)), _sds((k, n)), n_axes=2)(a, b)


def mix_out_bwd(dx1, w_out, of, ob, p, gg, tm):
    T = dx1.shape[0]

    def body(dx_ref, w_ref, of_ref, ob_ref, pg_ref, gg_ref, da_ref, do_ref, dpg_ref, dgg_ref):
        @pl.when(pl.program_id(0) == 0)
        def _():
            dgg_ref[...] = jnp.zeros_like(dgg_ref)

        dxb = dx_ref[...].astype(MX)
        da_ref[...] = _dg(dxb, w_ref[0:512, :], 1, 1)
        for h in range(4):
            sl = slice(h * 128, (h + 1) * 128)
            dm = _dg(dxb, w_ref[512 + h * 128:512 + (h + 1) * 128, :], 1, 1)
            _, vjp = jax.vjp(outb_head, of_ref[:, sl] + ob_ref[:, sl], pg_ref[:, sl], gg_ref[h])
            do, dpg, dg = vjp(dm)
            do_ref[:, sl] = do
            dpg_ref[:, sl] = dpg
            dgg_ref[h] += dg

    return _call(body, "mix_out_bwd", (T // tm,),
                 [_rows(tm, D), _full(w_out), _rows(tm, 512), _rows(tm, 512), _rows(tm, 512, 4), _full(gg)],
                 [_rows(tm, 512), _rows(tm, 512), _rows(tm, 512), _full(gg)],
                 [_sds((T, 512)), _sds((T, 512)), _sds((T, 512)), _sds(gg.shape)])(dx1, w_out, of, ob, p, gg)


def gla_bwd(p, wg, bg, ss, do, reverse):
    T = p.shape[0]
    n = T // CH
    rev = not reverse
    rn = n if rev else None

    def body(q_ref, k_ref, v_ref, r_ref, wg_ref, bg_ref, ss_ref, do_ref,
             dq_ref, dk_ref, dv_ref, dr_ref, dwg_ref, dbg_ref, dst_ref):
        @pl.when(pl.program_id(0) == 0)
        def _():
            dst_ref[...] = jnp.zeros_like(dst_ref)
            dwg_ref[...] = jnp.zeros_like(dwg_ref)
            dbg_ref[...] = jnp.zeros_like(dbg_ref)

        consts = _tri(reverse) + _lane_masks()
        dr = jnp.zeros((CH, 128), F32)
        for j in range(2):
            sl = slice(j * 128, (j + 1) * 128)
            v0s, v1s = slice(256 * j, 256 * j + 128), slice(256 * j + 128, 256 * j + 256)
            _, vjp = jax.vjp(functools.partial(gla_pair, consts), r_ref[...], wg_ref[:, sl], bg_ref[:, sl],
                             q_ref[:, sl], k_ref[:, sl], v_ref[:, v0s], v_ref[:, v1s], ss_ref[0, 2 * j], ss_ref[0, 2 * j + 1])
            g = vjp((do_ref[:, v0s], do_ref[:, v1s], dst_ref[2 * j], dst_ref[2 * j + 1]))
            dr = dr + g[0]
            dwg_ref[:, sl] += g[1]
            dbg_ref[:, sl] += g[2]
            dq_ref[:, sl] = g[3]
            dk_ref[:, sl] = g[4]
            dv_ref[:, v0s] = g[5]
            dv_ref[:, v1s] = g[6]
            dst_ref[2 * j] = g[7]
            dst_ref[2 * j + 1] = g[8]
        dr_ref[...] = dr

    ss_spec = pl.BlockSpec((1, 4, 128, 128), (lambda i: (n - 1 - i, 0, 0, 0)) if rev else (lambda i: (i, 0, 0, 0)))
    return _call(body, "gla_bwd_r" if reverse else "gla_bwd_f", (n,),
                 _gla_in_specs(n, rev) + [_full(wg), _full(bg), ss_spec, _rows(CH, 512, 0, rn)],
                 [_rows(CH, 256, 0, rn), _rows(CH, 256, 0, rn), _rows(CH, 512, 0, rn), _rows(CH, 128, 0, rn),
                  _full(wg), _full(bg)],
                 [_sds((T, 256)), _sds((T, 256)), _sds((T, 512)), _sds((T, 128)), _sds(wg.shape), _sds(bg.shape)],
                 scratch=[pltpu.VMEM((4, 128, 128), F32)])(p, p, p, p, wg, bg, ss, do)


def gmlp_bwd(p, douta, ws, bs, lg, lb):
    T = p.shape[0]

    def body(pa_ref, do_ref, ws_ref, bs_ref, lg_ref, lb_ref, dpa_ref, dws_ref, dbs_ref, dlg_ref, dlb_ref):
        @pl.when(pl.program_id(0) == 0)
        def _():
            for r in (dws_ref, dbs_ref, dlg_ref, dlb_ref):
                r[...] = jnp.zeros_like(r)

        for h in range(4):
            us, vs = slice(h * 128, (h + 1) * 128), slice(512 + h * 128, 512 + (h + 1) * 128)
            _, vjp = jax.vjp(gmlp_head, pa_ref[:, us], pa_ref[:, vs], ws_ref[h], bs_ref[h], lg_ref[h], lb_ref[h])
            du, dv, dw, dbc, dg, db = vjp(do_ref[:, us])
            dpa_ref[:, us] = du
            dpa_ref[:, vs] = dv
            dws_ref[h] += dw
            dbs_ref[h] += dbc
            dlg_ref[h] += dg
            dlb_ref[h] += db

    return _call(body, "gmlp_bwd", (T // CH,),
                 [_rows(CH, 1024), _rows(CH, 512), _full(ws), _full(bs), _full(lg), _full(lb)],
                 [_rows(CH, 1024), _full(ws), _full(bs), _full(lg), _full(lb)],
                 [_sds((T, 1024)), _sds(ws.shape), _sds(bs.shape), _sds(lg.shape), _sds(lb.shape)])(p, douta, ws, bs, lg, lb)


def assemble_dp(dpa, dqf, dqb, dkf, dkb, dvf, dvb, dpg, drf, drb, tm):
    T = dpa.shape[0]

    def body(a_ref, qf, qb, kf, kb, vf, vb, g_ref, rf, rb, o_ref):
        o_ref[:, 0:1024] = a_ref[...].astype(MX)
        o_ref[:, 1024:1280] = (qf[...] + qb[...]).astype(MX)
        o_ref[:, 1280:1536] = (kf[...] + kb[...]).astype(MX)
        o_ref[:, 1536:2048] = (vf[...] + vb[...]).astype(MX)
        o_ref[:, 2048:2560] = g_ref[...].astype(MX)
        o_ref[:, 2560:2688] = (rf[...] + rb[...]).astype(MX)

    ins = [dpa, dqf, dqb, dkf, dkb, dvf, dvb, dpg, drf, drb]
    return _call(body, "assemble_dp", (T // tm,), [_rows(tm, a.shape[1]) for a in ins],
                 _rows(tm, N_INP), _sds((T, N_INP), MX))(*ins)


def _gate_pad(w, row0):
    return jnp.zeros((128, 256), F32).at[row0:row0 + 16].set(w)


def local_step(x, tgt, W, tm=256):
    saved = []
    for l in range(NL):
        s = {"x": x}
        p, s["h"] = norm_matmul(x, W["g_mix"][l][None], W["w_in"][l], tm, "mix_in")
        s["p"] = p
        ws, bs = W["w_s"][l], W["b_s"][l][:, :, None]
        lg, lb = W["ln_g"][l][:, None, :], W["ln_b"][l][:, None, :]
        outa = gmlp_fwd(p, ws, bs, lg, lb)
        wgf, wgb = _gate_pad(W["w_gate_f"][l], 0), _gate_pad(W["w_gate_b"][l], 16)
        bgf, bgb = W["b_gate_f"][l][None], W["b_gate_b"][l][None]
        s["of"], s["ssf"] = gla_fwd(p, wgf, bgf, False)
        s["ob"], s["ssb"] = gla_fwd(p, wgb, bgb, True)
        gg = W["g_gla"][l][:, None, :]
        x1, s["mixed"] = mix_out(x, s["of"], s["ob"], p, outa, gg, W["w_out"][l], tm)
        s["x1"] = x1
        s["zu"], s["h2"] = norm_matmul(x1, W["g_ffn"][l][None], W["w_up"][l], tm, "ffn_up")
        x, s["a"] = ffn_down(x1, s["zu"], W["conv_w"][l], W["conv_b"][l][None], W["w_down"][l], tm)
        saved.append(s)

    lsum, dx, dgf = loss_head(x, W["g_final"][None], tgt, tm)
    G = {k: [None] * NL for k in ("g_mix", "w_in", "w_s", "b_s", "ln_g", "ln_b", "w_gate_f", "b_gate_f", "w_gate_b",
                                  "b_gate_b", "g_gla", "w_out", "g_ffn", "w_up", "conv_w", "conv_b", "w_down")}
    for l in reversed(range(NL)):
        s = saved[l]
        cw, cb = W["conv_w"][l], W["conv_b"][l][None]
        G["w_down"][l] = matmul_tn(s["a"], dx, 512, 512, "dw_down")
        dz = ffn_down_bwd(dx, s["zu"], cw, cb, W["w_down"][l], tm)
        dzu, G["conv_w"][l], dcb = ffn_conv_bwd(dz, s["zu"], cw, tm)
        G["conv_b"][l] = dcb[0]
        G["w_up"][l] = matmul_tn(s["h2"], dzu, 512, 1408, "dw_up")
        dx1, dg = nt_normbwd(dzu, W["w_up"][l], s["x1"], W["g_ffn"][l][None], dx, tm, "ffn_up_bwd")
        G["g_ffn"][l] = dg[0]
        G["w_out"][l] = matmul_tn(s["mixed"], dx1, 512, 512, "dw_out")
        gg = W["g_gla"][l][:, None, :]
        douta, do, dpg, dgg = mix_out_bwd(dx1, W["w_out"][l], s["of"], s["ob"], s["p"], gg, tm)
        G["g_gla"][l] = dgg[:, 0, :]
        wgf, wgb = _gate_pad(W["w_gate_f"][l], 0), _gate_pad(W["w_gate_b"][l], 16)
        bgf, bgb = W["b_gate_f"][l][None], W["b_gate_b"][l][None]
        dqf, dkf, dvf, drf, dwgf, dbgf = gla_bwd(s["p"], wgf, bgf, s["ssf"], do, False)
        dqb, dkb, dvb, drb, dwgb, dbgb = gla_bwd(s["p"], wgb, bgb, s["ssb"], do, True)
        G["w_gate_f"][l], G["b_gate_f"][l] = dwgf[0:16], dbgf[0]
        G["w_gate_b"][l], G["b_gate_b"][l] = dwgb[16:32], dbgb[0]
        ws, bs = W["w_s"][l], W["b_s"][l][:, :, None]
        lg, lb = W["ln_g"][l][:, None, :], W["ln_b"][l][:, None, :]
        dpa, G["w_s"][l], dbs, dlg, dlb = gmlp_bwd(s["p"], douta, ws, bs, lg, lb)
        G["b_s"][l], G["ln_g"][l], G["ln_b"][l] = dbs[:, :, 0], dlg[:, 0, :], dlb[:, 0, :]
        dpc = assemble_dp(dpa, dqf, dqb, dkf, dkb, dvf, dvb, dpg, drf, drb, tm)
        G["w_in"][l] = matmul_tn(s["h"], dpc, 512, 896, "dw_in")[:, :N_IN]
        dx, dg = nt_normbwd(dpc, W["w_in"][l], s["x"], W["g_mix"][l][None], dx1, tm, "mix_in_bwd")
        G["g_mix"][l] = dg[0]
    G = {k: jnp.stack(v) for k, v in G.items()}
    G["g_final"] = dgf[0]
    return lsum, dx, G


def cast_bf16(a, tr):
    r, c = a.shape

    def body(a_ref, o_ref):
        o_ref[...] = a_ref[...].astype(BF16)

    return _call(body, "cast_bf16", (r // tr,), [_rows(tr, c)], _rows(tr, c), _sds((r, c), BF16))(a)


def add_half(g, r, cvec, tr):
    _, rr, cc = g.shape

    def body(c_ref, g_ref, r_ref, o_ref):
        o_ref[...] = g_ref[...] + r_ref[...]

    gs = pltpu.PrefetchScalarGridSpec(
        num_scalar_prefetch=1, grid=(2, rr // tr),
        in_specs=[pl.BlockSpec((1, tr, cc), lambda l, i, c: (2 * c[0] + l, i, 0)),
                  pl.BlockSpec((1, tr, cc), lambda l, i, c: (l, i, 0))],
        out_specs=pl.BlockSpec((1, tr, cc), lambda l, i, c: (l, i, 0)))
    return pl.pallas_call(body, name="add_half", grid_spec=gs, out_shape=_sds((2, rr, cc)),
                          compiler_params=_cparams(2))(cvec, g, r)


def sum_lead(y, tr):
    n, rr, cc = y.shape

    def body(y_ref, o_ref):
        acc = y_ref[0]
        for k in range(1, n):
            acc = acc + y_ref[k]
        o_ref[...] = acc

    return _call(body, "sum_lead", (rr // tr,), [pl.BlockSpec((n, tr, cc), lambda i: (0, i, 0))],
                 _rows(tr, cc), _sds((rr, cc)))(y)


def adamw(w, g, m, v, tr):
    r, c = w.shape

    def body(w_ref, g_ref, m_ref, v_ref, d_ref, nm_ref, nv_ref):
        gr = g_ref[...]
        nm = ADAM_B1 * m_ref[...] + (1.0 - ADAM_B1) * gr
        nv = ADAM_B2 * v_ref[...] + (1.0 - ADAM_B2) * jnp.square(gr)
        m_hat = nm / (1.0 - ADAM_B1 ** ADAM_STEP)
        v_hat = nv / (1.0 - ADAM_B2 ** ADAM_STEP)
        d_ref[...] = -ADAM_LR * (m_hat / (jnp.sqrt(v_hat) + ADAM_EPS) + ADAM_WD * w_ref[...])
        nm_ref[...] = nm
        nv_ref[...] = nv

    sp = _rows(tr, c)
    return _call(body, "adamw", (r // tr,), [sp] * 4, [sp] * 3, [_sds((r, c))] * 3)(w, g, m, v)


MESH = pl.DeviceIdType.MESH
ANY = pl.BlockSpec(memory_space=pl.ANY)
N_BIG = 4


def _pos():
    return lax.axis_index("x"), lax.axis_index("y"), lax.axis_index("c")


def _other_chips(x, y):
    return [(1 - x, y), (x, 1 - y), (1 - x, 1 - y)]


def _rcopy(src, dst, send_sems, recv_sems, k, to):
    return pltpu.make_async_remote_copy(src_ref=src, dst_ref=dst, send_sem=send_sems.at[k], recv_sem=recv_sems.at[k],
                                        device_id=to, device_id_type=MESH)


def allgather8(xs):
    m, n = xs.shape

    def body(x_ref, out_ref, send_sems, recv_sems, local_sem):
        x, y, c = _pos()
        me, sibling = (x, y, c), (x, y, 1 - c)
        chips = _other_chips(x, y)

        def rows(px, py, pc):
            return out_ref.at[pl.ds((4 * px + 2 * py + pc) * m, m), :]

        def copy(k, block, to, src=None):
            return _rcopy(rows(*block) if src is None else src, rows(*block), send_sems, recv_sems, k, to)

        mine = pltpu.make_async_copy(x_ref, rows(*me), local_sem)
        mine.start()
        first = [copy(0, me, sibling, src=x_ref)]
        first += [copy(1 + j, me, (*chip, c), src=x_ref) for j, chip in enumerate(chips)]
        for cp in first:
            cp.start()
        passed = [copy(4 + j, (*chip, c), sibling) for j, chip in enumerate(chips)]
        for j, chip in enumerate(chips):
            copy(1 + j, (*chip, c), me).wait_recv()
            passed[j].start()
        copy(0, sibling, me).wait_recv()
        for j, chip in enumerate(chips):
            copy(4 + j, (*chip, 1 - c), me).wait_recv()
        for cp in first + passed:
            cp.wait_send()
        mine.wait()

    vm = pl.BlockSpec(memory_space=pltpu.VMEM)
    return pl.pallas_call(
        body, name="allgather8", out_shape=_sds((8 * m, n), xs.dtype), in_specs=[vm], out_specs=vm,
        scratch_shapes=[pltpu.SemaphoreType.DMA((7,)), pltpu.SemaphoreType.DMA((7,)), pltpu.SemaphoreType.DMA],
        compiler_params=pltpu.CompilerParams(vmem_limit_bytes=VMEM_LIMIT))(xs)


def _full_slab(k, ref, j, lay):
    if k == 0:
        return ref.at[lay, j]
    if k == 1:
        return ref.at[lay, pl.ds(256 * j, 256), :]
    if k == 2:
        return ref.at[lay, :, pl.ds(1408 * j, 1408)]
    return ref.at[lay, pl.ds(704 * j, 704), :]


_FULL_SHAPES = [(NL, 4, 1024, 648), (NL, 1024, 1024), (NL, 1024, NUP), (NL, DFF, 1024)]
_SHARD_SHAPES = [(NL, 1024, 648), (NL, 256, 1024), (NL, 1024, 1408), (NL, 704, 1024)]


def gather_weights(shards):
    def body(*refs):
        S, F = refs[:N_BIG], refs[N_BIG:2 * N_BIG]
        send_sems, recv_sems, local_sems = refs[2 * N_BIG:]
        x, y, c = _pos()
        me = 2 * x + y
        sibling = (x, y, 1 - c)
        chips = _other_chips(x, y)
        mine_l, other_l, all_l = pl.ds(2 * c, 2), pl.ds(2 * (1 - c), 2), pl.ds(0, NL)
        local = [pltpu.make_async_copy(S[k], _full_slab(k, F[k], me, all_l), local_sems.at[k]) for k in range(N_BIG)]
        for cp in local:
            cp.start()
        first = []
        for k in range(N_BIG):
            for j, (px, py) in enumerate(chips):
                first.append(_rcopy(S[k].at[mine_l], _full_slab(k, F[k], me, mine_l), send_sems, recv_sems,
                                    6 * k + j, (px, py, c)))
        for cp in first:
            cp.start()
        passed = []
        for k in range(N_BIG):
            for j, (px, py) in enumerate(chips):
                region = _full_slab(k, F[k], 2 * px + py, mine_l)
                _rcopy(region, region, send_sems, recv_sems, 6 * k + j, sibling).wait_recv()
                cp = _rcopy(region, region, send_sems, recv_sems, 6 * k + 3 + j, sibling)
                cp.start()
                passed.append(cp)
        for k in range(N_BIG):
            for j, (px, py) in enumerate(chips):
                region = _full_slab(k, F[k], 2 * px + py, other_l)
                _rcopy(region, region, send_sems, recv_sems, 6 * k + 3 + j, sibling).wait_recv()
        for cp in first + passed:
            cp.wait_send()
        for cp in local:
            cp.wait()

    ns = 6 * N_BIG
    return pl.pallas_call(
        body, name="gather_weights", out_shape=[_sds(s, BF16) for s in _FULL_SHAPES],
        in_specs=[ANY] * N_BIG, out_specs=[ANY] * N_BIG,
        scratch_shapes=[pltpu.SemaphoreType.DMA((ns,)), pltpu.SemaphoreType.DMA((ns,)), pltpu.SemaphoreType.DMA((N_BIG,))],
    )(*shards)


def rs_swap(grads):
    def body(*refs):
        G, R = refs[:N_BIG], refs[N_BIG:2 * N_BIG]
        send_sems, recv_sems = refs[2 * N_BIG:]
        x, y, c = _pos()
        cps = [_rcopy(G[k].at[pl.ds(2 * (1 - c), 2)], R[k], send_sems, recv_sems, k, (x, y, 1 - c)) for k in range(N_BIG)]
        for cp in cps:
            cp.start()
        for cp in cps:
            cp.wait()

    return pl.pallas_call(
        body, name="rs_swap", out_shape=[_sds((2,) + s[1:]) for s in _FULL_SHAPES],
        in_specs=[ANY] * N_BIG, out_specs=[ANY] * N_BIG,
        scratch_shapes=[pltpu.SemaphoreType.DMA((N_BIG,)), pltpu.SemaphoreType.DMA((N_BIG,))])(*grads)


def rs_chips(parts):
    def body(*refs):
        P, Y = refs[:N_BIG], refs[N_BIG:2 * N_BIG]
        send_sems, recv_sems, local_sems = refs[2 * N_BIG:]
        x, y, c = _pos()
        me = 2 * x + y
        chips = _other_chips(x, y)
        lay = pl.ds(0, 2)
        local = [pltpu.make_async_copy(_full_slab(k, P[k], me, lay), Y[k].at[me], local_sems.at[k]) for k in range(N_BIG)]
        for cp in local:
            cp.start()
        cps = []
        for k in range(N_BIG):
            for j, (px, py) in enumerate(chips):
                cps.append(_rcopy(_full_slab(k, P[k], 2 * px + py, lay), Y[k].at[me], send_sems, recv_sems,
                                  3 * k + j, (px, py, c)))
        for cp in cps:
            cp.start()
        for k in range(N_BIG):
            for j, (px, py) in enumerate(chips):
                dst = Y[k].at[2 * px + py]
                _rcopy(dst, dst, send_sems, recv_sems, 3 * k + j, (px, py, c)).wait_recv()
        for cp in cps:
            cp.wait_send()
        for cp in local:
            cp.wait()

    ns = 3 * N_BIG
    return pl.pallas_call(
        body, name="rs_chips", out_shape=[_sds((4, 2) + s[1:]) for s in _SHARD_SHAPES],
        in_specs=[ANY] * N_BIG, out_specs=[ANY] * N_BIG,
        scratch_shapes=[pltpu.SemaphoreType.DMA((ns,)), pltpu.SemaphoreType.DMA((ns,)), pltpu.SemaphoreType.DMA((N_BIG,))],
    )(*parts)


def rs_final(halves):
    def body(*refs):
        Fh, O = refs[:N_BIG], refs[N_BIG:2 * N_BIG]
        send_sems, recv_sems, local_sems = refs[2 * N_BIG:]
        x, y, c = _pos()
        mine_l = pl.ds(2 * c, 2)
        local = [pltpu.make_async_copy(Fh[k], O[k].at[mine_l], local_sems.at[k]) for k in range(N_BIG)]
        cps = [_rcopy(Fh[k], O[k].at[mine_l], send_sems, recv_sems, k, (x, y, 1 - c)) for k in range(N_BIG)]
        for cp in local + cps:
            cp.start()
        for k in range(N_BIG):
            dst = O[k].at[pl.ds(2 * (1 - c), 2)]
            _rcopy(dst, dst, send_sems, recv_sems, k, (x, y, 1 - c)).wait_recv()
        for cp in cps:
            cp.wait_send()
        for cp in local:
            cp.wait()

    return pl.pallas_call(
        body, name="rs_final", out_shape=[_sds(s) for s in _SHARD_SHAPES],
        in_specs=[ANY] * N_BIG, out_specs=[ANY] * N_BIG,
        scratch_shapes=[pltpu.SemaphoreType.DMA((N_BIG,)), pltpu.SemaphoreType.DMA((N_BIG,)), pltpu.SemaphoreType.DMA((N_BIG,))],
    )(*halves)


_WEIGHTS = ['g_mix', 'w_in', 'w_s', 'b_s', 'ln_g', 'ln_b', 'w_gate_f', 'b_gate_f', 'w_gate_b', 'b_gate_b', 'g_gla',
            'w_out', 'g_ffn', 'w_up', 'conv_w', 'conv_b', 'w_down', 'g_final']
_BIG = ['w_in', 'w_out', 'w_up', 'w_down']
_SMALL = [n for n in _WEIGHTS if n not in _BIG]
_SMALL_SHARDED = {'w_gate_f': 64, 'w_gate_b': 64, 'conv_w': 1408}
_BIG_TR = {'w_in': 512, 'w_out': 256, 'w_up': 256, 'w_down': 352}


def _pack(arrs):
    flat = jnp.concatenate([a.reshape(-1) for a in arrs])
    pad = (-flat.shape[0]) % 1024
    return jnp.pad(flat, (0, pad)).reshape(-1, 128)


def _unpack(buf, shapes):
    flat = buf.reshape(-1)
    out, o = [], 0
    for s in shapes:
        n = 1
        for d in s:
            n *= d
        out.append(flat[o:o + n].reshape(s))
        o += n
    return out


def _rows3(a):
    return a.reshape(a.shape[0], -1, a.shape[-1])


def kernel(x, g_mix, w_in, w_s, b_s, ln_g, ln_b, w_gate_f, b_gate_f, w_gate_b, b_gate_b, g_gla, w_out, g_ffn, w_up, conv_w, conv_b, w_down, g_final, loss_target, m_g_mix, m_w_in, m_w_s, m_b_s, m_ln_g, m_ln_b, m_w_gate_f, m_b_gate_f, m_w_gate_b, m_b_gate_b, m_g_gla, m_w_out, m_g_ffn, m_w_up, m_conv_w, m_conv_b, m_w_down, m_g_final, v_g_mix, v_w_in, v_w_s, v_b_s, v_ln_g, v_ln_b, v_w_gate_f, v_b_gate_f, v_w_gate_b, v_b_gate_b, v_g_gla, v_w_out, v_g_ffn, v_w_up, v_conv_w, v_conv_b, v_w_down, v_g_final):
    loc = locals()
    w = {n: loc[n] for n in _WEIGHTS}
    m = {n: loc["m_" + n] for n in _WEIGHTS}
    v = {n: loc["v_" + n] for n in _WEIGHTS}
    xi, yi, ci = _pos()
    chip = 2 * xi + yi
    cvec = jnp.reshape(ci, (1,)).astype(jnp.int32)

    sh_names = list(_SMALL_SHARDED)
    g8 = allgather8(_pack([w[n] for n in sh_names]))
    rows = g8.shape[0] // 8
    per_chip = [_unpack(g8[2 * j * rows:(2 * j + 1) * rows], [w[n].shape for n in sh_names]) for j in range(4)]
    W = dict(w)
    for k, n in enumerate(sh_names):
        W[n] = jnp.concatenate([per_chip[j][k] for j in range(4)], axis=-1)

    shards = [cast_bf16(w[n].reshape(-1, w[n].shape[-1]), _BIG_TR[n]).reshape(w[n].shape) for n in _BIG]
    f_in, f_out, f_up, f_down = gather_weights(shards)
    f_in = jnp.transpose(f_in, (0, 2, 1, 3)).reshape(NL, D, N_IN)
    W["w_in"] = jnp.pad(f_in, ((0, 0), (0, 0), (0, N_INP - N_IN)))
    W["w_out"], W["w_up"], W["w_down"] = f_out, f_up, f_down

    lsum, grad_x, G = local_step(x[0], loss_target[0], W)

    g_in = jnp.transpose(G["w_in"].reshape(NL, D, 4, 648), (0, 2, 1, 3))
    full = [g_in, G["w_out"], G["w_up"], G["w_down"]]
    recv = rs_swap(full)
    parts = [add_half(_rows3(a), _rows3(b), cvec, 256).reshape(b.shape) for a, b in zip(full, recv)]
    ys = rs_chips(parts)
    halves = [sum_lead(a.reshape(4, -1, a.shape[-1]), _BIG_TR[n]).reshape(a.shape[1:]) for n, a in zip(_BIG, ys)]
    big_grads = dict(zip(_BIG, rs_final(halves)))

    small_shapes = [G[n].shape for n in _SMALL] + [(D,)]
    pk = _pack([G[n] for n in _SMALL] + [lsum])
    srows = pk.shape[0]
    red = sum_lead(allgather8(pk).reshape(8, srows, 128), srows)
    small = dict(zip(_SMALL + ["lsum"], _unpack(red, small_shapes)))
    loss = 0.5 * jnp.sum(small.pop("lsum")) / D
    for n, wd in _SMALL_SHARDED.items():
        small[n] = lax.dynamic_slice_in_dim(small[n], chip * wd, wd, axis=small[n].ndim - 1)

    grads, delta, new_m, new_v = dict(small), {}, {}, {}
    for n in _BIG:
        grads[n] = big_grads[n]
        two = lambda a: a.reshape(-1, a.shape[-1])
        d_, m_, v_ = adamw(two(w[n]), two(grads[n]), two(m[n]), two(v[n]), _BIG_TR[n])
        delta[n], new_m[n], new_v[n] = d_.reshape(w[n].shape), m_.reshape(w[n].shape), v_.reshape(w[n].shape)
    shapes = [w[n].shape for n in _SMALL]
    pw, pg, pm, pv = (_pack([t[n] for n in _SMALL]) for t in (w, grads, m, v))
    d_, m_, v_ = adamw(pw, pg, pm, pv, pw.shape[0])
    for t, buf in ((delta, d_), (new_m, m_), (new_v, v_)):
        t.update(zip(_SMALL, _unpack(buf, shapes)))

    return (loss, grad_x[None], *[grads[n] for n in _WEIGHTS], *[delta[n] for n in _WEIGHTS],
            *[new_m[n] for n in _WEIGHTS], *[new_v[n] for n in _WEIGHTS])
```

```python
import functools

import jax
import jax.numpy as jnp
from jax import lax
from jax.experimental import pallas as pl
from jax.experimental.pallas import tpu as pltpu

F32 = jnp.float32
BF16 = jnp.bfloat16
MX = BF16

D = 1024
CH = 128
NL = 4
N_IN = 2592
N_INP = 2688
NUP = 5632
DFF = 2816
EPS = 1e-6
VMEM_LIMIT = 56 * 1024 * 1024

ADAM_LR, ADAM_B1, ADAM_B2, ADAM_EPS, ADAM_WD, ADAM_STEP = 0.001, 0.9, 0.999, 1e-08, 0.01, 10


def _dg(a, b, ca, cb):
    return lax.dot_general(a.astype(MX), b.astype(MX), (((ca,), (cb,)), ((), ())), preferred_element_type=F32)


@jax.custom_vjp
def mm(a, b):
    return _dg(a, b, 1, 0)


mm.defvjp(lambda a, b: (_dg(a, b, 1, 0), (a, b)),
          lambda r, g: (_dg(g, r[1], 1, 1), _dg(r[0], g, 0, 0)))


@jax.custom_vjp
def mm_nt(a, b):
    return _dg(a, b, 1, 1)


mm_nt.defvjp(lambda a, b: (_dg(a, b, 1, 1), (a, b)),
             lambda r, g: (_dg(g, r[1], 1, 0), _dg(g, r[0], 0, 0)))


@jax.custom_vjp
def mm_tn(a, b):
    return _dg(a, b, 0, 0)


mm_tn.defvjp(lambda a, b: (_dg(a, b, 0, 0), (a, b)),
             lambda r, g: (_dg(r[1], g, 1, 1), _dg(r[0], g, 1, 0)))


def _split3(x):
    hi = x.astype(BF16)
    r1 = x - hi.astype(F32)
    mid = r1.astype(BF16)
    lo = (r1 - mid.astype(F32)).astype(BF16)
    return hi, mid, lo


def _dot3(m, x):
    hi, mid, lo = _split3(x)
    d = lambda p: lax.dot_general(m, p, (((1,), (0,)), ((), ())), preferred_element_type=F32)
    return d(hi) + d(mid) + d(lo)


@jax.custom_vjp
def cumdot(m, mt, x):
    return _dot3(m, x)


cumdot.defvjp(lambda m, mt, x: (_dot3(m, x), (m, mt)),
              lambda r, g: (jnp.zeros_like(r[0]), jnp.zeros_like(r[1]), _dot3(r[1], g)))


def rmsnorm(x, g):
    return x * lax.rsqrt(jnp.mean(x * x, axis=-1, keepdims=True) + EPS) * g


def gelu(x):
    return 0.5 * x * (1.0 + lax.erf(x * 0.7071067811865476))


def sigmoid(x):
    return 1.0 / (1.0 + jnp.exp(-x))


def log_sigmoid(x):
    return jnp.minimum(x, 0.0) - jnp.log(1.0 + jnp.exp(-jnp.abs(x)))


def gmlp_head(u_pre, v_pre, w, bcol, g, b):
    u = gelu(u_pre)
    v = gelu(v_pre)
    mu = jnp.mean(v, axis=-1, keepdims=True)
    var = jnp.mean(jnp.square(v - mu), axis=-1, keepdims=True)
    vn = (v - mu) * lax.rsqrt(var + EPS) * g + b
    return u * (mm(w, vn) + bcol)


def outb_head(o, pg, g):
    return rmsnorm(o, g) * (pg * sigmoid(pg))


def ffn_act(zg, zv):
    return zg * sigmoid(zg) * zv


def _tri(reverse):
    r = lax.broadcasted_iota(jnp.int32, (CH, CH), 0)
    c = lax.broadcasted_iota(jnp.int32, (CH, CH), 1)
    if reverse:
        cm, sm = c >= r, c > r
    else:
        cm, sm = c <= r, c <= r
    one = jnp.ones((), BF16)
    zero = jnp.zeros((), BF16)
    return jnp.where(cm, one, zero), jnp.where(cm.T, one, zero), sm


def gla_pair(consts, pr, wg, bg, qp, kp, v0, v1, st0, st1):
    m, mt, smask, lm0, lm1 = consts
    la = log_sigmoid(mm(pr, wg) + bg) * (1.0 / 16.0)
    cum = cumdot(m, mt, la)
    tot = jnp.sum(la, axis=0, keepdims=True)
    q_dec = (qp * 0.125) * jnp.exp(cum)
    k_inv = kp * jnp.exp(-cum)
    k_end = kp * jnp.exp(tot - cum)
    dec = jnp.exp(tot)
    outs = []
    for lm, v, st in ((lm0, v0, st0), (lm1, v1, st1)):
        s = jnp.where(smask, mm_nt(q_dec * lm, k_inv), 0.0)
        o = mm(s, v) + mm_nt(q_dec, st)
        st_new = st * dec + mm_tn(v, k_end * lm)
        outs += [o, st_new]
    return outs[0], outs[2], outs[1], outs[3]


def _lane_masks():
    lane = lax.broadcasted_iota(jnp.int32, (1, 128), 1)
    return (lane < 64).astype(F32), (lane >= 64).astype(F32)


def _cparams(n_axes=1):
    return pltpu.CompilerParams(dimension_semantics=("arbitrary",) * n_axes, vmem_limit_bytes=VMEM_LIMIT)


def _full(a):
    nd = a.ndim
    return pl.BlockSpec(a.shape, lambda *_: (0,) * nd)


def _rows(tm, w, cb=0, rev_n=None):
    if rev_n is None:
        return pl.BlockSpec((tm, w), lambda i: (i, cb))
    return pl.BlockSpec((tm, w), lambda i: (rev_n - 1 - i, cb))


def _call(body, name, grid, in_specs, out_specs, out_shape, scratch=(), n_axes=1):
    return pl.pallas_call(body, name=name, grid=grid, in_specs=in_specs, out_specs=out_specs, out_shape=out_shape,
                          scratch_shapes=list(scratch), compiler_params=_cparams(n_axes))


def _sds(shape, dt=F32):
    return jax.ShapeDtypeStruct(shape, dt)


def norm_matmul(x, g, w, tm, name):
    T, n = x.shape[0], w.shape[1]

    def body(x_ref, g_ref, w_ref, y_ref, h_ref):
        hb = rmsnorm(x_ref[...], g_ref[...]).astype(MX)
        h_ref[...] = hb
        y_ref[...] = jnp.dot(hb, w_ref[...], preferred_element_type=F32)

    return _call(body, name, (T // tm,), [_rows(tm, D), _full(g), _full(w)],
                 [_rows(tm, n), _rows(tm, D)], [_sds((T, n)), _sds((T, D), MX)])(x, g, w)


def gmlp_fwd(p, ws, bs, lg, lb):
    T = p.shape[0]

    def body(pa_ref, ws_ref, bs_ref, lg_ref, lb_ref, o_ref):
        for h in range(4):
            o_ref[:, h * 128:(h + 1) * 128] = gmlp_head(
                pa_ref[:, h * 128:(h + 1) * 128], pa_ref[:, 512 + h * 128:512 + (h + 1) * 128],
                ws_ref[h], bs_ref[h], lg_ref[h], lb_ref[h]).astype(MX)

    return _call(body, "gmlp_fwd", (T // CH,), [_rows(CH, 1024), _full(ws), _full(bs), _full(lg), _full(lb)],
                 _rows(CH, 512), _sds((T, 512), MX))(p, ws, bs, lg, lb)


def _gla_in_specs(n, rev):
    r = n if rev else None
    return [_rows(CH, 256, 4, r), _rows(CH, 256, 5, r), _rows(CH, 512, 3, r), _rows(CH, 128, 20, r)]


def gla_fwd(p, wg, bg, reverse):
    T = p.shape[0]
    n = T // CH
    rev = n if reverse else None

    def body(q_ref, k_ref, v_ref, r_ref, wg_ref, bg_ref, o_ref, ss_ref, st_ref):
        @pl.when(pl.program_id(0) == 0)
        def _():
            st_ref[...] = jnp.zeros_like(st_ref)

        consts = _tri(reverse) + _lane_masks()
        for j in range(2):
            sl = slice(j * 128, (j + 1) * 128)
            st0, st1 = st_ref[2 * j], st_ref[2 * j + 1]
            ss_ref[0, 2 * j] = st0
            ss_ref[0, 2 * j + 1] = st1
            o0, o1, n0, n1 = gla_pair(consts, r_ref[...], wg_ref[:, sl], bg_ref[:, sl], q_ref[:, sl], k_ref[:, sl],
                                      v_ref[:, 256 * j:256 * j + 128], v_ref[:, 256 * j + 128:256 * j + 256], st0, st1)
            o_ref[:, 256 * j:256 * j + 128] = o0
            o_ref[:, 256 * j + 128:256 * j + 256] = o1
            st_ref[2 * j] = n0
            st_ref[2 * j + 1] = n1

    ss_spec = pl.BlockSpec((1, 4, 128, 128), (lambda i: (n - 1 - i, 0, 0, 0)) if reverse else (lambda i: (i, 0, 0, 0)))
    return _call(body, "gla_fwd_r" if reverse else "gla_fwd_f", (n,),
                 _gla_in_specs(n, reverse) + [_full(wg), _full(bg)],
                 [_rows(CH, 512, 0, rev), ss_spec], [_sds((T, 512)), _sds((n, 4, 128, 128))],
                 scratch=[pltpu.VMEM((4, 128, 128), F32)])(p, p, p, p, wg, bg)


def mix_out(x, of, ob, p, outa, gg, w_out, tm):
    T = x.shape[0]

    def body(x_ref, of_ref, ob_ref, pg_ref, oa_ref, gg_ref, w_ref, x1_ref, mx_ref):
        mx_ref[:, 0:512] = oa_ref[...]
        for h in range(4):
            sl = slice(h * 128, (h + 1) * 128)
            mx_ref[:, 512 + h * 128:512 + (h + 1) * 128] = outb_head(
                of_ref[:, sl] + ob_ref[:, sl], pg_ref[:, sl], gg_ref[h]).astype(MX)
        x1_ref[...] = x_ref[...] + jnp.dot(mx_ref[...], w_ref[...], preferred_element_type=F32)

    return _call(body, "mix_out", (T // tm,),
                 [_rows(tm, D), _rows(tm, 512), _rows(tm, 512), _rows(tm, 512, 4), _rows(tm, 512), _full(gg), _full(w_out)],
                 [_rows(tm, D), _rows(tm, 1024)], [_sds((T, D)), _sds((T, 1024), MX)])(x, of, ob, p, outa, gg, w_out)


def _halo_specs(T, tm, w):
    nb = T // 8
    r = tm // 8
    return [pl.BlockSpec((tm, w), lambda i: (i, 0)),
            pl.BlockSpec((8, w), lambda i: (jnp.maximum(i * r - 1, 0), 0)),
            pl.BlockSpec((8, w), lambda i: (jnp.minimum((i + 1) * r, nb - 1), 0))]


def _shifted(main, prev8, next8, i, nsteps):
    tm = main.shape[0]
    row = lax.broadcasted_iota(jnp.int32, (tm, 1), 0)
    pr = jnp.where(i > 0, prev8[7:8, :], 0.0)
    nx = jnp.where(i < nsteps - 1, next8[0:1, :], 0.0)
    dn = jnp.where(row == 0, pr, pltpu.roll(main, 1, 0))
    up = jnp.where(row == tm - 1, nx, pltpu.roll(main, tm - 1, 0))
    return dn, up


def _conv(zu, zp8, zn8, cw_ref, cb_ref, i, nsteps):
    dn, up = _shifted(zu, zp8, zn8, i, nsteps)
    return cb_ref[...] + dn * cw_ref[0:1, :] + zu * cw_ref[1:2, :] + up * cw_ref[2:3, :]


def ffn_down(x1, zu, cw, cb, w_down, tm):
    T = x1.shape[0]
    ns = T // tm

    def body(x_ref, zu_ref, zp_ref, zn_ref, cw_ref, cb_ref, w_ref, x2_ref, a_ref):
        z = _conv(zu_ref[...], zp_ref[...], zn_ref[...], cw_ref, cb_ref, pl.program_id(0), ns)
        a = ffn_act(z[:, :DFF], z[:, DFF:]).astype(MX)
        a_ref[...] = a
        x2_ref[...] = x_ref[...] + jnp.dot(a, w_ref[...], preferred_element_type=F32)

    return _call(body, "ffn_down", (ns,), [_rows(tm, D)] + _halo_specs(T, tm, NUP) + [_full(cw), _full(cb), _full(w_down)],
                 [_rows(tm, D), _rows(tm, DFF)], [_sds((T, D)), _sds((T, DFF), MX)])(x1, zu, zu, zu, cw, cb, w_down)


def loss_head(x, g, tgt, tm):
    T = x.shape[0]

    def body(x_ref, g_ref, t_ref, l_ref, dx_ref, dg_ref):
        @pl.when(pl.program_id(0) == 0)
        def _():
            l_ref[...] = jnp.zeros_like(l_ref)
            dg_ref[...] = jnp.zeros_like(dg_ref)

        y, vjp = jax.vjp(rmsnorm, x_ref[...], g_ref[...])
        err = y - t_ref[...]
        l_ref[...] += jnp.sum(err * err, axis=0, keepdims=True)
        dx, dg = vjp(err * (1.0 / D))
        dx_ref[...] = dx
        dg_ref[...] += dg

    return _call(body, "loss_head", (T // tm,), [_rows(tm, D), _full(g), _rows(tm, D)],
                 [_full(g), _rows(tm, D), _full(g)], [_sds((1, D)), _sds((T, D)), _sds((1, D))])(x, g, tgt)


def ffn_down_bwd(dx2, zu, cw, cb, w_down, tm):
    T = dx2.shape[0]
    ns = T // tm

    def body(dx_ref, zu_ref, zp_ref, zn_ref, cw_ref, cb_ref, w_ref, dz_ref):
        z = _conv(zu_ref[...], zp_ref[...], zn_ref[...], cw_ref, cb_ref, pl.program_id(0), ns)
        da = _dg(dx_ref[...], w_ref[...], 1, 1)
        _, vjp = jax.vjp(ffn_act, z[:, :DFF], z[:, DFF:])
        dzg, dzv = vjp(da)
        dz_ref[:, :DFF] = dzg
        dz_ref[:, DFF:] = dzv

    return _call(body, "ffn_down_bwd", (ns,), [_rows(tm, D)] + _halo_specs(T, tm, NUP) + [_full(cw), _full(cb), _full(w_down)],
                 _rows(tm, NUP), _sds((T, NUP)))(dx2, zu, zu, zu, cw, cb, w_down)


def ffn_conv_bwd(dz, zu, cw, tm):
    T = dz.shape[0]
    ns = T // tm

    def body(dz_ref, dp_ref, dn_ref, zu_ref, zp_ref, zn_ref, cw_ref, dzu_ref, dcw_ref, dcb_ref):
        i = pl.program_id(0)

        @pl.when(i == 0)
        def _():
            dcw_ref[...] = jnp.zeros_like(dcw_ref)
            dcb_ref[...] = jnp.zeros_like(dcb_ref)

        dz = dz_ref[...]
        zu = zu_ref[...]
        ddn, dup = _shifted(dz, dp_ref[...], dn_ref[...], i, ns)
        zdn, zup = _shifted(zu, zp_ref[...], zn_ref[...], i, ns)
        dzu_ref[...] = (dup * cw_ref[0:1, :] + dz * cw_ref[1:2, :] + ddn * cw_ref[2:3, :]).astype(MX)
        dcw_ref[0:1, :] += jnp.sum(zdn * dz, axis=0, keepdims=True)
        dcw_ref[1:2, :] += jnp.sum(zu * dz, axis=0, keepdims=True)
        dcw_ref[2:3, :] += jnp.sum(zup * dz, axis=0, keepdims=True)
        dcb_ref[...] += jnp.sum(dz, axis=0, keepdims=True)

    return _call(body, "ffn_conv_bwd", (ns,), _halo_specs(T, tm, NUP) + _halo_specs(T, tm, NUP) + [_full(cw)],
                 [_rows(tm, NUP), _full(cw), pl.BlockSpec((1, NUP), lambda i: (0, 0))],
                 [_sds((T, NUP), MX), _sds((3, NUP)), _sds((1, NUP))])(dz, dz, dz, zu, zu, zu, cw)


def nt_normbwd(dy, w, x, g, dres, tm, name):
    T, k = dy.shape

    def body(dy_ref, w_ref, x_ref, g_ref, dr_ref, dx_ref, dg_ref):
        @pl.when(pl.program_id(0) == 0)
        def _():
            dg_ref[...] = jnp.zeros_like(dg_ref)

        dh = _dg(dy_ref[...], w_ref[...], 1, 1)
        _, vjp = jax.vjp(rmsnorm, x_ref[...], g_ref[...])
        dx, dg = vjp(dh)
        dx_ref[...] = dr_ref[...] + dx
        dg_ref[...] += dg

    return _call(body, name, (T // tm,), [_rows(tm, k), _full(w), _rows(tm, D), _full(g), _rows(tm, D)],
                 [_rows(tm, D), _full(g)], [_sds((T, D)), _sds((1, D))])(dy, w, x, g, dres)


def matmul_tn(a, b, tt, tn, name):
    T, k = a.shape
    n = b.shape[1]
    last = T // tt - 1

    def body(a_ref, b_ref, o_ref, acc_ref):
        @pl.when(pl.program_id(1) == 0)
        def _():
            acc_ref[...] = jnp.zeros_like(acc_ref)

        acc_ref[...] += _dg(a_ref[...], b_ref[...], 0, 0)

        @pl.when(pl.program_id(1) == last)
        def _():
            o_ref[...] = acc_ref[...].astype(MX)

    return _call(body, name, (n // tn, T // tt),
                 [pl.BlockSpec((tt, k), lambda j, i: (i, 0)), pl.BlockSpec((tt, tn), lambda j, i: (i, j))],
                 pl.BlockSpec((k, tn), lambda j, i: (0, j)), _sds((k, n), MX), scratch=[pltpu.VMEM((k, tn), F32)],
                 n_axes=2)(a, b)


def mix_out_bwd(dx1, w_out, of, ob, p, gg, tm):
    T = dx1.shape[0]

    def body(dx_ref, w_ref, of_ref, ob_ref, pg_ref, gg_ref, da_ref, do_ref, dpg_ref, dgg_ref):
        @pl.when(pl.program_id(0) == 0)
        def _():
            dgg_ref[...] = jnp.zeros_like(dgg_ref)

        dxb = dx_ref[...].astype(MX)
        da_ref[...] = _dg(dxb, w_ref[0:512, :], 1, 1)
        for h in range(4):
            sl = slice(h * 128, (h + 1) * 128)
            dm = _dg(dxb, w_ref[512 + h * 128:512 + (h + 1) * 128, :], 1, 1)
            _, vjp = jax.vjp(outb_head, of_ref[:, sl] + ob_ref[:, sl], pg_ref[:, sl], gg_ref[h])
            do, dpg, dg = vjp(dm)
            do_ref[:, sl] = do
            dpg_ref[:, sl] = dpg
            dgg_ref[h] += dg

    return _call(body, "mix_out_bwd", (T // tm,),
                 [_rows(tm, D), _full(w_out), _rows(tm, 512), _rows(tm, 512), _rows(tm, 512, 4), _full(gg)],
                 [_rows(tm, 512), _rows(tm, 512), _rows(tm, 512), _full(gg)],
                 [_sds((T, 512)), _sds((T, 512)), _sds((T, 512)), _sds(gg.shape)])(dx1, w_out, of, ob, p, gg)


def gla_bwd(p, wg, bg, ss, do, reverse):
    T = p.shape[0]
    n = T // CH
    rev = not reverse
    rn = n if rev else None

    def body(q_ref, k_ref, v_ref, r_ref, wg_ref, bg_ref, ss_ref, do_ref,
             dq_ref, dk_ref, dv_ref, dr_ref, dwg_ref, dbg_ref, dst_ref):
        @pl.when(pl.program_id(0) == 0)
        def _():
            dst_ref[...] = jnp.zeros_like(dst_ref)
            dwg_ref[...] = jnp.zeros_like(dwg_ref)
            dbg_ref[...] = jnp.zeros_like(dbg_ref)

        consts = _tri(reverse) + _lane_masks()
        dr = jnp.zeros((CH, 128), F32)
        for j in range(2):
            sl = slice(j * 128, (j + 1) * 128)
            v0s, v1s = slice(256 * j, 256 * j + 128), slice(256 * j + 128, 256 * j + 256)
            _, vjp = jax.vjp(functools.partial(gla_pair, consts), r_ref[...], wg_ref[:, sl], bg_ref[:, sl],
                             q_ref[:, sl], k_ref[:, sl], v_ref[:, v0s], v_ref[:, v1s], ss_ref[0, 2 * j], ss_ref[0, 2 * j + 1])
            g = vjp((do_ref[:, v0s], do_ref[:, v1s], dst_ref[2 * j], dst_ref[2 * j + 1]))
            dr = dr + g[0]
            dwg_ref[:, sl] += g[1]
            dbg_ref[:, sl] += g[2]
            dq_ref[:, sl] = g[3]
            dk_ref[:, sl] = g[4]
            dv_ref[:, v0s] = g[5]
            dv_ref[:, v1s] = g[6]
            dst_ref[2 * j] = g[7]
            dst_ref[2 * j + 1] = g[8]
        dr_ref[...] = dr

    ss_spec = pl.BlockSpec((1, 4, 128, 128), (lambda i: (n - 1 - i, 0, 0, 0)) if rev else (lambda i: (i, 0, 0, 0)))
    return _call(body, "gla_bwd_r" if reverse else "gla_bwd_f", (n,),
                 _gla_in_specs(n, rev) + [_full(wg), _full(bg), ss_spec, _rows(CH, 512, 0, rn)],
                 [_rows(CH, 256, 0, rn), _rows(CH, 256, 0, rn), _rows(CH, 512, 0, rn), _rows(CH, 128, 0, rn),
                  _full(wg), _full(bg)],
                 [_sds((T, 256)), _sds((T, 256)), _sds((T, 512)), _sds((T, 128)), _sds(wg.shape), _sds(bg.shape)],
                 scratch=[pltpu.VMEM((4, 128, 128), F32)])(p, p, p, p, wg, bg, ss, do)


def gmlp_bwd(p, douta, ws, bs, lg, lb):
    T = p.shape[0]

    def body(pa_ref, do_ref, ws_ref, bs_ref, lg_ref, lb_ref, dpa_ref, dws_ref, dbs_ref, dlg_ref, dlb_ref):
        @pl.when(pl.program_id(0) == 0)
        def _():
            for r in (dws_ref, dbs_ref, dlg_ref, dlb_ref):
                r[...] = jnp.zeros_like(r)

        for h in range(4):
            us, vs = slice(h * 128, (h + 1) * 128), slice(512 + h * 128, 512 + (h + 1) * 128)
            _, vjp = jax.vjp(gmlp_head, pa_ref[:, us], pa_ref[:, vs], ws_ref[h], bs_ref[h], lg_ref[h], lb_ref[h])
            du, dv, dw, dbc, dg, db = vjp(do_ref[:, us])
            dpa_ref[:, us] = du
            dpa_ref[:, vs] = dv
            dws_ref[h] += dw
            dbs_ref[h] += dbc
            dlg_ref[h] += dg
            dlb_ref[h] += db

    return _call(body, "gmlp_bwd", (T // CH,),
                 [_rows(CH, 1024), _rows(CH, 512), _full(ws), _full(bs), _full(lg), _full(lb)],
                 [_rows(CH, 1024), _full(ws), _full(bs), _full(lg), _full(lb)],
                 [_sds((T, 1024)), _sds(ws.shape), _sds(bs.shape), _sds(lg.shape), _sds(lb.shape)])(p, douta, ws, bs, lg, lb)


def assemble_dp(dpa, dqf, dqb, dkf, dkb, dvf, dvb, dpg, drf, drb, tm):
    T = dpa.shape[0]

    def body(a_ref, qf, qb, kf, kb, vf, vb, g_ref, rf, rb, o_ref):
        o_ref[:, 0:1024] = a_ref[...].astype(MX)
        o_ref[:, 1024:1280] = (qf[...] + qb[...]).astype(MX)
        o_ref[:, 1280:1536] = (kf[...] + kb[...]).astype(MX)
        o_ref[:, 1536:2048] = (vf[...] + vb[...]).astype(MX)
        o_ref[:, 2048:2560] = g_ref[...].astype(MX)
        o_ref[:, 2560:2688] = (rf[...] + rb[...]).astype(MX)

    ins = [dpa, dqf, dqb, dkf, dkb, dvf, dvb, dpg, drf, drb]
    return _call(body, "assemble_dp", (T // tm,), [_rows(tm, a.shape[1]) for a in ins],
                 _rows(tm, N_INP), _sds((T, N_INP), MX))(*ins)


def _gate_pad(w, row0):
    return jnp.zeros((128, 256), F32).at[row0:row0 + 16].set(w)


def local_step(x, tgt, W, get_big, emit, tm=256):
    saved = []
    for l in range(NL):
        s = {"x": x}
        s.update(get_big(l, "in", x))
        p, s["h"] = norm_matmul(x, W["g_mix"][l][None], s["w_in"], tm, "mix_in")
        s["p"] = p
        ws, bs = W["w_s"][l], W["b_s"][l][:, :, None]
        lg, lb = W["ln_g"][l][:, None, :], W["ln_b"][l][:, None, :]
        outa = gmlp_fwd(p, ws, bs, lg, lb)
        wgf, wgb = _gate_pad(W["w_gate_f"][l], 0), _gate_pad(W["w_gate_b"][l], 16)
        bgf, bgb = W["b_gate_f"][l][None], W["b_gate_b"][l][None]
        s["of"], s["ssf"] = gla_fwd(p, wgf, bgf, False)
        s["ob"], s["ssb"] = gla_fwd(p, wgb, bgb, True)
        s.update(get_big(l, "rest", s["ob"]))
        gg = W["g_gla"][l][:, None, :]
        x1, s["mixed"] = mix_out(x, s["of"], s["ob"], p, outa, gg, s["w_out"], tm)
        s["x1"] = x1
        s["zu"], s["h2"] = norm_matmul(x1, W["g_ffn"][l][None], s["w_up"], tm, "ffn_up")
        x, s["a"] = ffn_down(x1, s["zu"], W["conv_w"][l], W["conv_b"][l][None], s["w_down"], tm)
        saved.append(s)

    lsum, dx, dgf = loss_head(x, W["g_final"][None], tgt, tm)
    G = {k: [None] * NL for k in _SMALL if k != "g_final"}
    tok = jnp.zeros((1, 1), F32)
    for l in reversed(range(NL)):
        s = saved[l]
        cw, cb = W["conv_w"][l], W["conv_b"][l][None] + tok
        g_down = matmul_tn(s["a"], dx, 512, 512, "dw_down")
        dz = ffn_down_bwd(dx, s["zu"], cw, cb, s["w_down"], tm)
        dzu, G["conv_w"][l], dcb = ffn_conv_bwd(dz, s["zu"], cw, tm)
        G["conv_b"][l] = dcb[0]
        g_up = matmul_tn(s["h2"], dzu, 512, 1408, "dw_up")
        tok = emit(l, "A", {"w_down": g_down, "w_up": g_up})
        dx1, dg = nt_normbwd(dzu, s["w_up"], s["x1"], W["g_ffn"][l][None] + tok, dx, tm, "ffn_up_bwd")
        G["g_ffn"][l] = dg[0]
        g_out = matmul_tn(s["mixed"], dx1, 512, 512, "dw_out")
        gg = W["g_gla"][l][:, None, :]
        douta, do, dpg, dgg = mix_out_bwd(dx1, s["w_out"], s["of"], s["ob"], s["p"], gg, tm)
        G["g_gla"][l] = dgg[:, 0, :]
        wgf, wgb = _gate_pad(W["w_gate_f"][l], 0), _gate_pad(W["w_gate_b"][l], 16)
        bgf, bgb = W["b_gate_f"][l][None], W["b_gate_b"][l][None]
        dqf, dkf, dvf, drf, dwgf, dbgf = gla_bwd(s["p"], wgf, bgf, s["ssf"], do, False)
        dqb, dkb, dvb, drb, dwgb, dbgb = gla_bwd(s["p"], wgb, bgb, s["ssb"], do, True)
        G["w_gate_f"][l], G["b_gate_f"][l] = dwgf[0:16], dbgf[0]
        G["w_gate_b"][l], G["b_gate_b"][l] = dwgb[16:32], dbgb[0]
        ws, bs = W["w_s"][l], W["b_s"][l][:, :, None]
        lg, lb = W["ln_g"][l][:, None, :], W["ln_b"][l][:, None, :]
        dpa, G["w_s"][l], dbs, dlg, dlb = gmlp_bwd(s["p"], douta, ws, bs, lg, lb)
        G["b_s"][l], G["ln_g"][l], G["ln_b"][l] = dbs[:, :, 0], dlg[:, 0, :], dlb[:, 0, :]
        dpc = assemble_dp(dpa, dqf, dqb, dkf, dkb, dvf, dvb, dpg, drf, drb, tm)
        g_in = matmul_tn(s["h"], dpc, 512, 896, "dw_in")
        tok = emit(l, "B", {"w_out": g_out, "w_in": g_in})
        dx, dg = nt_normbwd(dpc, s["w_in"], s["x"], W["g_mix"][l][None] + tok, dx1, tm, "mix_in_bwd")
        G["g_mix"][l] = dg[0]
    G = {k: jnp.stack(v) for k, v in G.items()}
    G["g_final"] = dgf[0]
    return lsum, dx, G


def cast_bf16(a, tr):
    r, c = a.shape

    def body(a_ref, o_ref):
        o_ref[...] = a_ref[...].astype(BF16)

    return _call(body, "cast_bf16", (r // tr,), [_rows(tr, c)], _rows(tr, c), _sds((r, c), BF16))(a)


def sum_lead(y, tr):
    n, rr, cc = y.shape

    def body(y_ref, o_ref):
        acc = y_ref[0].astype(F32)
        for k in range(1, n):
            acc = acc + y_ref[k].astype(F32)
        o_ref[...] = acc

    return _call(body, "sum_lead", (rr // tr,), [pl.BlockSpec((n, tr, cc), lambda i: (0, i, 0))],
                 _rows(tr, cc), _sds((rr, cc)))(y)


def adamw(w, ga, gb, m, v, tr):
    r, c = w.shape

    def body(w_ref, ga_ref, gb_ref, m_ref, v_ref, g_ref, d_ref, nm_ref, nv_ref):
        gr = ga_ref[...] + gb_ref[...]
        g_ref[...] = gr
        nm = ADAM_B1 * m_ref[...] + (1.0 - ADAM_B1) * gr
        nv = ADAM_B2 * v_ref[...] + (1.0 - ADAM_B2) * jnp.square(gr)
        m_hat = nm / (1.0 - ADAM_B1 ** ADAM_STEP)
        v_hat = nv / (1.0 - ADAM_B2 ** ADAM_STEP)
        d_ref[...] = -ADAM_LR * (m_hat / (jnp.sqrt(v_hat) + ADAM_EPS) + ADAM_WD * w_ref[...])
        nm_ref[...] = nm
        nv_ref[...] = nv

    sp = _rows(tr, c)
    return _call(body, "adamw", (r // tr,), [sp] * 5, [sp] * 4, [_sds((r, c))] * 4)(w, ga, gb, m, v)


MESH = pl.DeviceIdType.MESH
ANY = pl.BlockSpec(memory_space=pl.ANY)
N_BIG = 4


def _pos():
    return lax.axis_index("x"), lax.axis_index("y"), lax.axis_index("c")


def _other_chips(x, y):
    return [(1 - x, y), (x, 1 - y), (1 - x, 1 - y)]


def _rcopy(src, dst, send_sems, recv_sems, k, to):
    return pltpu.make_async_remote_copy(src_ref=src, dst_ref=dst, send_sem=send_sems.at[k], recv_sem=recv_sems.at[k],
                                        device_id=to, device_id_type=MESH)


def allgather8(xs):
    m, n = xs.shape

    def body(x_ref, out_ref, send_sems, recv_sems, local_sem):
        x, y, c = _pos()
        me, sibling = (x, y, c), (x, y, 1 - c)
        chips = _other_chips(x, y)

        def rows(px, py, pc):
            return out_ref.at[pl.ds((4 * px + 2 * py + pc) * m, m), :]

        def copy(k, block, to, src=None):
            return _rcopy(rows(*block) if src is None else src, rows(*block), send_sems, recv_sems, k, to)

        mine = pltpu.make_async_copy(x_ref, rows(*me), local_sem)
        mine.start()
        first = [copy(0, me, sibling, src=x_ref)]
        first += [copy(1 + j, me, (*chip, c), src=x_ref) for j, chip in enumerate(chips)]
        for cp in first:
            cp.start()
        passed = [copy(4 + j, (*chip, c), sibling) for j, chip in enumerate(chips)]
        for j, chip in enumerate(chips):
            copy(1 + j, (*chip, c), me).wait_recv()
            passed[j].start()
        copy(0, sibling, me).wait_recv()
        for j, chip in enumerate(chips):
            copy(4 + j, (*chip, 1 - c), me).wait_recv()
        for cp in first + passed:
            cp.wait_send()
        mine.wait()

    vm = pl.BlockSpec(memory_space=pltpu.VMEM)
    return pl.pallas_call(
        body, name="allgather8", out_shape=_sds((8 * m, n), xs.dtype), in_specs=[vm], out_specs=vm,
        scratch_shapes=[pltpu.SemaphoreType.DMA((7,)), pltpu.SemaphoreType.DMA((7,)), pltpu.SemaphoreType.DMA],
        compiler_params=pltpu.CompilerParams(vmem_limit_bytes=VMEM_LIMIT))(xs)


def _slab(k, ref, j):
    if k == 0:
        return ref.at[j]
    if k == 1:
        return ref.at[pl.ds(256 * j, 256), :]
    if k == 2:
        return ref.at[:, pl.ds(1408 * j, 1408)]
    return ref.at[pl.ds(704 * j, 704), :]


_LAYER_FULL = [(4, 1024, 648), (1024, 1024), (1024, NUP), (DFF, 1024)]
_LAYER_SHARD = [(1024, 648), (256, 1024), (1024, 1408), (704, 1024)]
_SHARD_SHAPES = [(NL,) + s for s in _LAYER_SHARD]

HBM = pl.BlockSpec(memory_space=pltpu.HBM)
SEM = pl.BlockSpec(memory_space=pltpu.SEMAPHORE)
VM = pl.BlockSpec(memory_space=pltpu.VMEM)
EFFECT = pltpu.SideEffectType.DATAFLOW_SIDE_EFFECTING
_GW_GROUPS = [[(0, 0)], [(0, 1), (0, 2), (0, 3)]] + [[(l, k) for k in range(N_BIG)] for l in range(1, NL)]
_GW_ORDER = [lk for g in _GW_GROUPS for lk in g]


def _hbm(a):
    return pltpu.with_memory_space_constraint(a, pltpu.HBM)


def _hbm_like(a):
    return pltpu.HBM(a.shape, a.dtype)


def gw_start(shards, landings):
    n = len(_GW_ORDER)

    def body(*refs):
        S, Ld = refs[:N_BIG], refs[N_BIG:N_BIG + n]
        outs = refs[N_BIG + n:]
        send_sems, recv, token = outs[0], outs[1:1 + len(_GW_GROUPS)], outs[-1]
        x, y, c = _pos()
        me = 2 * x + y
        ci = 0
        for gi, grp in enumerate(_GW_GROUPS):
            for t, (l, k) in enumerate(grp):
                land = Ld[_GW_ORDER.index((l, k))]
                for j, (px, py) in enumerate(_other_chips(x, y)):
                    pltpu.make_async_remote_copy(
                        src_ref=S[k].at[l], dst_ref=_slab(k, land, me), send_sem=send_sems.at[ci],
                        recv_sem=recv[gi].at[3 * t + j], device_id=(px, py, c), device_id_type=MESH).start()
                    ci += 1
        token[...] = jnp.zeros_like(token)

    ins = list(shards) + list(landings)
    sems = [pltpu.SemaphoreType.DMA((3 * n,))] + [pltpu.SemaphoreType.DMA((3 * len(g),)) for g in _GW_GROUPS]
    outs = pl.pallas_call(
        body, name="gw_start", out_shape=sems + [_hbm_like(a) for a in ins] + [_sds((8, 128))],
        in_specs=[HBM] * len(ins), out_specs=[SEM] * len(sems) + [HBM] * len(ins) + [VM],
        input_output_aliases={i: len(sems) + i for i in range(len(ins))},
        compiler_params=pltpu.CompilerParams(has_side_effects=EFFECT))(*[_hbm(a) for a in ins])
    ns = len(sems)
    return outs[0], outs[1:ns], outs[ns:ns + N_BIG], outs[ns + N_BIG:ns + len(ins)], outs[-1]


def gw_wait(gi, landings, recv_sems, after, shards=None, send_sems=None):
    grp = _GW_GROUPS[gi]
    n = len(grp)
    last = shards is not None

    def body(*refs):
        Ld, rs = refs[:n], refs[n]
        x, y, c = _pos()
        for t, (l, k) in enumerate(grp):
            for j, (px, py) in enumerate(_other_chips(x, y)):
                region = _slab(k, Ld[t], 2 * px + py)
                pltpu.make_async_remote_copy(src_ref=region, dst_ref=region, send_sem=rs.at[3 * t + j],
                                             recv_sem=rs.at[3 * t + j], device_id=(px, py, c),
                                             device_id_type=MESH).wait_recv()
        if last:
            S, ss = refs[n + 2:n + 2 + N_BIG], refs[n + 2 + N_BIG]
            me = 2 * x + y
            for ci, (l, k) in enumerate(lk for lk in _GW_ORDER for _ in range(3)):
                pltpu.make_async_remote_copy(src_ref=S[k].at[l], dst_ref=_slab(k, Ld[k], me), send_sem=ss.at[ci],
                                             recv_sem=ss.at[ci], device_id=(x, y, c), device_id_type=MESH).wait_send()

    ins = list(landings) + [recv_sems, after]
    specs = [HBM] * n + [SEM, pl.BlockSpec(memory_space=pl.ANY)]
    outs = [_hbm_like(a) for a in landings]
    alias = {i: i for i in range(n)}
    if last:
        ins += list(shards) + [send_sems]
        specs += [HBM] * N_BIG + [SEM]
        outs += [_hbm_like(a) for a in shards]
        alias.update({n + 2 + i: n + i for i in range(N_BIG)})
    res = pl.pallas_call(body, name="gw_wait_%d" % gi, out_shape=outs, in_specs=specs, out_specs=[HBM] * len(outs),
                         input_output_aliases=alias,
                         compiler_params=pltpu.CompilerParams(has_side_effects=EFFECT))(*ins)
    return res[:n]


def ga_start(tag, ks, grads, landings):
    n = len(ks)

    def body(*refs):
        G, Ld = refs[:n], refs[n:2 * n]
        send_sems, recv_sems, token = refs[2 * n], refs[2 * n + 1], refs[-1]
        x, y, c = _pos()
        me = 2 * x + y
        for t, k in enumerate(ks):
            for j, (px, py) in enumerate(_other_chips(x, y)):
                pltpu.make_async_remote_copy(
                    src_ref=_slab(k, G[t], 2 * px + py), dst_ref=Ld[t].at[me], send_sem=send_sems.at[3 * t + j],
                    recv_sem=recv_sems.at[3 * t + j], device_id=(px, py, c), device_id_type=MESH).start()
        token[...] = jnp.zeros_like(token)

    ins = list(grads) + list(landings)
    sems = [pltpu.SemaphoreType.DMA((3 * n,))] * 2
    outs = pl.pallas_call(
        body, name="ga_start_" + tag, out_shape=sems + [_hbm_like(a) for a in ins] + [_sds((8, 128))],
        in_specs=[HBM] * len(ins), out_specs=[SEM, SEM] + [HBM] * len(ins) + [VM],
        input_output_aliases={i: 2 + i for i in range(len(ins))},
        compiler_params=pltpu.CompilerParams(has_side_effects=EFFECT))(*[_hbm(a) for a in ins])
    return outs[0], outs[1], outs[2:2 + n], outs[2 + n:2 + 2 * n], outs[-1]


def ga_wait(tag, ks, send_sems, recv_sems, grads, landings, after):
    n = len(ks)

    def body(*refs):
        G, Ld, ss, rs = refs[:n], refs[n:2 * n], refs[2 * n], refs[2 * n + 1]
        x, y, c = _pos()
        me = 2 * x + y
        for t, k in enumerate(ks):
            for j, (px, py) in enumerate(_other_chips(x, y)):
                pj = 2 * px + py
                cp = pltpu.make_async_remote_copy(
                    src_ref=_slab(k, G[t], pj), dst_ref=Ld[t].at[pj], send_sem=ss.at[3 * t + j],
                    recv_sem=rs.at[3 * t + j], device_id=(px, py, c), device_id_type=MESH)
                cp.wait_send()
                cp.wait_recv()

    ins = list(grads) + list(landings) + [send_sems, recv_sems, after]
    res = pl.pallas_call(
        body, name="ga_wait_" + tag, out_shape=[_hbm_like(a) for a in list(grads) + list(landings)],
        in_specs=[HBM] * (2 * n) + [SEM, SEM, pl.BlockSpec(memory_space=pl.ANY)], out_specs=[HBM] * (2 * n),
        input_output_aliases={i: i for i in range(2 * n)},
        compiler_params=pltpu.CompilerParams(has_side_effects=EFFECT))(*ins)
    return res[n:]


def swap4(parts):
    def body(*refs):
        Q, R = refs[:N_BIG], refs[N_BIG:2 * N_BIG]
        send_sems, recv_sems = refs[2 * N_BIG:]
        x, y, c = _pos()
        cps = [_rcopy(Q[k], R[k], send_sems, recv_sems, k, (x, y, 1 - c)) for k in range(N_BIG)]
        for cp in cps:
            cp.start()
        for cp in cps:
            cp.wait()

    return pl.pallas_call(
        body, name="swap4", out_shape=[_sds(s) for s in _SHARD_SHAPES],
        in_specs=[ANY] * N_BIG, out_specs=[ANY] * N_BIG,
        scratch_shapes=[pltpu.SemaphoreType.DMA((N_BIG,)), pltpu.SemaphoreType.DMA((N_BIG,))])(*parts)


_WEIGHTS = ['g_mix', 'w_in', 'w_s', 'b_s', 'ln_g', 'ln_b', 'w_gate_f', 'b_gate_f', 'w_gate_b', 'b_gate_b', 'g_gla',
            'w_out', 'g_ffn', 'w_up', 'conv_w', 'conv_b', 'w_down', 'g_final']
_BIG = ['w_in', 'w_out', 'w_up', 'w_down']
_SMALL = [n for n in _WEIGHTS if n not in _BIG]
_SMALL_SHARDED = {'w_gate_f': 64, 'w_gate_b': 64, 'conv_w': 1408}
_BIG_TR = {'w_in': 512, 'w_out': 256, 'w_up': 256, 'w_down': 352}


def _pack(arrs):
    flat = jnp.concatenate([a.reshape(-1) for a in arrs])
    pad = (-flat.shape[0]) % 1024
    return jnp.pad(flat, (0, pad)).reshape(-1, 128)


def _unpack(buf, shapes):
    flat = buf.reshape(-1)
    out, o = [], 0
    for s in shapes:
        n = 1
        for d in s:
            n *= d
        out.append(flat[o:o + n].reshape(s))
        o += n
    return out


def _take(k, full, chip):
    if k == 0:
        return lax.dynamic_index_in_dim(full, chip, 0, keepdims=False)
    if k == 2:
        return lax.dynamic_slice_in_dim(full, chip * 1408, 1408, axis=1)
    return lax.dynamic_slice_in_dim(full, chip * _LAYER_SHARD[k][0], _LAYER_SHARD[k][0], axis=0)


def _place(k, full, part, chip):
    if k == 0:
        return lax.dynamic_update_index_in_dim(full, part, chip, 0)
    if k == 2:
        return lax.dynamic_update_slice_in_dim(full, part, chip * 1408, axis=1)
    return lax.dynamic_update_slice_in_dim(full, part, chip * _LAYER_SHARD[k][0], axis=0)


def _place_lead(buf, part, chip):
    return lax.dynamic_update_index_in_dim(buf, part, chip, 0)


def kernel(x, g_mix, w_in, w_s, b_s, ln_g, ln_b, w_gate_f, b_gate_f, w_gate_b, b_gate_b, g_gla, w_out, g_ffn, w_up, conv_w, conv_b, w_down, g_final, loss_target, m_g_mix, m_w_in, m_w_s, m_b_s, m_ln_g, m_ln_b, m_w_gate_f, m_b_gate_f, m_w_gate_b, m_b_gate_b, m_g_gla, m_w_out, m_g_ffn, m_w_up, m_conv_w, m_conv_b, m_w_down, m_g_final, v_g_mix, v_w_in, v_w_s, v_b_s, v_ln_g, v_ln_b, v_w_gate_f, v_b_gate_f, v_w_gate_b, v_b_gate_b, v_g_gla, v_w_out, v_g_ffn, v_w_up, v_conv_w, v_conv_b, v_w_down, v_g_final):
    loc = locals()
    w = {n: loc[n] for n in _WEIGHTS}
    m = {n: loc["m_" + n] for n in _WEIGHTS}
    v = {n: loc["v_" + n] for n in _WEIGHTS}
    xi, yi, _ = _pos()
    chip = 2 * xi + yi

    sh_names = list(_SMALL_SHARDED)
    g8 = allgather8(_pack([w[n] for n in sh_names]))
    rows = g8.shape[0] // 8
    per_chip = [_unpack(g8[2 * j * rows:(2 * j + 1) * rows], [w[n].shape for n in sh_names]) for j in range(4)]
    W = dict(w)
    for k, n in enumerate(sh_names):
        W[n] = jnp.concatenate([per_chip[j][k] for j in range(4)], axis=-1)

    shards = [cast_bf16(w[n].reshape(-1, w[n].shape[-1]), _BIG_TR[n]).reshape(w[n].shape) for n in _BIG]
    landings = [_place(k, lax.empty(_LAYER_FULL[k], BF16), shards[k][l], chip) for l, k in _GW_ORDER]
    send_sems, recv_sems, shards_fly, landings_fly, _ = gw_start(shards, landings)
    arrived = {}

    def get_big(l, stage, after):
        gi = {(0, "in"): 0, (0, "rest"): 1}.get((l, stage), l + 1 if stage == "in" else None)
        if gi is not None:
            lo = sum(len(g) for g in _GW_GROUPS[:gi])
            lands = landings_fly[lo:lo + len(_GW_GROUPS[gi])]
            if gi == len(_GW_GROUPS) - 1:
                full = gw_wait(gi, lands, recv_sems[gi], after, shards_fly, send_sems)
            else:
                full = gw_wait(gi, lands, recv_sems[gi], after)
            arrived.update(zip(_GW_GROUPS[gi], full))
        if stage == "in":
            f_in = jnp.transpose(arrived[(l, 0)], (1, 0, 2)).reshape(D, N_IN)
            return {"w_in": jnp.pad(f_in, ((0, 0), (0, N_INP - N_IN)))}
        return {"w_out": arrived[(l, 1)], "w_up": arrived[(l, 2)], "w_down": arrived[(l, 3)]}

    flying = []

    def emit(l, group, grads):
        ks = [3, 2] if group == "A" else [1, 0]
        gs = [grads[_BIG[k]] for k in ks]
        if group == "B":
            gs[1] = jnp.transpose(gs[1][:, :N_IN].reshape(D, 4, 648), (1, 0, 2))
        lands = [_place_lead(lax.empty((4,) + _LAYER_SHARD[k], BF16), _take(k, g, chip), chip) for k, g in zip(ks, gs)]
        tag = "%d%s" % (l, group)
        ss, rs, gs_fly, lands_fly, tok = ga_start(tag, ks, gs, lands)
        flying.append((tag, l, ks, ss, rs, gs_fly, lands_fly))
        return tok[0:1, 0:1]

    lsum, grad_x, G = local_step(x[0], loss_target[0], W, get_big, emit)

    plane = [[None] * NL for _ in range(N_BIG)]
    for tag, l, ks, ss, rs, gs_fly, lands_fly in flying:
        for k, a in zip(ks, ga_wait(tag, ks, ss, rs, gs_fly, lands_fly, grad_x)):
            plane[k][l] = sum_lead(a.reshape(4, -1, a.shape[-1]), _BIG_TR[_BIG[k]]).reshape(_LAYER_SHARD[k])
    plane = [jnp.stack(p) for p in plane]
    other = swap4(plane)

    small_shapes = [G[n].shape for n in _SMALL] + [(D,)]
    pk = _pack([G[n] for n in _SMALL] + [lsum])
    srows = pk.shape[0]
    red = sum_lead(allgather8(pk).reshape(8, srows, 128), srows)
    small = dict(zip(_SMALL + ["lsum"], _unpack(red, small_shapes)))
    loss = 0.5 * jnp.sum(small.pop("lsum")) / D
    for n, wd in _SMALL_SHARDED.items():
        small[n] = lax.dynamic_slice_in_dim(small[n], chip * wd, wd, axis=small[n].ndim - 1)

    grads, delta, new_m, new_v = dict(small), {}, {}, {}
    two = lambda a: a.reshape(-1, a.shape[-1])
    for k, n in enumerate(_BIG):
        res = adamw(two(w[n]), two(plane[k]), two(other[k]), two(m[n]), two(v[n]), _BIG_TR[n])
        grads[n], delta[n], new_m[n], new_v[n] = (r.reshape(w[n].shape) for r in res)
    shapes = [w[n].shape for n in _SMALL]
    pw, pg, pm, pv = (_pack([t[n] for n in _SMALL]) for t in (w, grads, m, v))
    _, d_, m_, v_ = adamw(pw, pg, jnp.zeros_like(pg), pm, pv, pw.shape[0])
    for t, buf in ((delta, d_), (new_m, m_), (new_v, v_)):
        t.update(zip(_SMALL, _unpack(buf, shapes)))

    return (loss, grad_x[None], *[grads[n] for n in _WEIGHTS], *[delta[n] for n in _WEIGHTS],
            *[new_m[n] for n in _WEIGHTS], *[new_v[n] for n in _WEIGHTS])
```

```python
import functools

import jax
import jax.numpy as jnp
from jax import lax
from jax.experimental import pallas as pl
from jax.experimental.pallas import tpu as pltpu

F32 = jnp.float32
BF16 = jnp.bfloat16
MX = BF16

D = 1024
CH = 128
NL = 4
N_IN = 2592
N_INP = 2688
NUP = 5632
DFF = 2816
EPS = 1e-6
VMEM_LIMIT = 56 * 1024 * 1024

ADAM_LR, ADAM_B1, ADAM_B2, ADAM_EPS, ADAM_WD, ADAM_STEP = 0.001, 0.9, 0.999, 1e-08, 0.01, 10


def _dg(a, b, ca, cb):
    return lax.dot_general(a.astype(MX), b.astype(MX), (((ca,), (cb,)), ((), ())), preferred_element_type=F32)


@jax.custom_vjp
def mm(a, b):
    return _dg(a, b, 1, 0)


mm.defvjp(lambda a, b: (_dg(a, b, 1, 0), (a, b)),
          lambda r, g: (_dg(g, r[1], 1, 1), _dg(r[0], g, 0, 0)))


@jax.custom_vjp
def mm_nt(a, b):
    return _dg(a, b, 1, 1)


mm_nt.defvjp(lambda a, b: (_dg(a, b, 1, 1), (a, b)),
             lambda r, g: (_dg(g, r[1], 1, 0), _dg(g, r[0], 0, 0)))


@jax.custom_vjp
def mm_tn(a, b):
    return _dg(a, b, 0, 0)


mm_tn.defvjp(lambda a, b: (_dg(a, b, 0, 0), (a, b)),
             lambda r, g: (_dg(r[1], g, 1, 1), _dg(r[0], g, 1, 0)))


def _split3(x):
    hi = x.astype(BF16)
    r1 = x - hi.astype(F32)
    mid = r1.astype(BF16)
    lo = (r1 - mid.astype(F32)).astype(BF16)
    return hi, mid, lo


def _dot3(m, x):
    hi, mid, lo = _split3(x)
    d = lambda p: lax.dot_general(m, p, (((1,), (0,)), ((), ())), preferred_element_type=F32)
    return d(hi) + d(mid) + d(lo)


@jax.custom_vjp
def cumdot(m, mt, x):
    return _dot3(m, x)


cumdot.defvjp(lambda m, mt, x: (_dot3(m, x), (m, mt)),
              lambda r, g: (jnp.zeros_like(r[0]), jnp.zeros_like(r[1]), _dot3(r[1], g)))


def rmsnorm(x, g):
    return x * lax.rsqrt(jnp.mean(x * x, axis=-1, keepdims=True) + EPS) * g


def gelu(x):
    return 0.5 * x * (1.0 + lax.erf(x * 0.7071067811865476))


def sigmoid(x):
    return 1.0 / (1.0 + jnp.exp(-x))


def log_sigmoid(x):
    return jnp.minimum(x, 0.0) - jnp.log(1.0 + jnp.exp(-jnp.abs(x)))


def gmlp_head(u_pre, v_pre, w, bcol, g, b):
    u = gelu(u_pre)
    v = gelu(v_pre)
    mu = jnp.mean(v, axis=-1, keepdims=True)
    var = jnp.mean(jnp.square(v - mu), axis=-1, keepdims=True)
    vn = (v - mu) * lax.rsqrt(var + EPS) * g + b
    return u * (mm(w, vn) + bcol)


def outb_head(o, pg, g):
    return rmsnorm(o, g) * (pg * sigmoid(pg))


def ffn_act(zg, zv):
    return zg * sigmoid(zg) * zv


def _tri(reverse):
    r = lax.broadcasted_iota(jnp.int32, (CH, CH), 0)
    c = lax.broadcasted_iota(jnp.int32, (CH, CH), 1)
    if reverse:
        cm, sm = c >= r, c > r
    else:
        cm, sm = c <= r, c <= r
    one = jnp.ones((), BF16)
    zero = jnp.zeros((), BF16)
    return jnp.where(cm, one, zero), jnp.where(cm.T, one, zero), sm


def gla_pair(consts, pr, wg, bg, qp, kp, v0, v1, st0, st1):
    m, mt, smask, lm0, lm1 = consts
    la = log_sigmoid(mm(pr, wg) + bg) * (1.0 / 16.0)
    cum = cumdot(m, mt, la)
    tot = jnp.sum(la, axis=0, keepdims=True)
    q_dec = (qp * 0.125) * jnp.exp(cum)
    k_inv = kp * jnp.exp(-cum)
    k_end = kp * jnp.exp(tot - cum)
    dec = jnp.exp(tot)
    outs = []
    for lm, v, st in ((lm0, v0, st0), (lm1, v1, st1)):
        s = jnp.where(smask, mm_nt(q_dec * lm, k_inv), 0.0)
        o = mm(s, v) + mm_nt(q_dec, st)
        st_new = st * dec + mm_tn(v, k_end * lm)
        outs += [o, st_new]
    return outs[0], outs[2], outs[1], outs[3]


def _lane_masks():
    lane = lax.broadcasted_iota(jnp.int32, (1, 128), 1)
    return (lane < 64).astype(F32), (lane >= 64).astype(F32)


def _cparams(n_axes=1):
    return pltpu.CompilerParams(dimension_semantics=("arbitrary",) * n_axes, vmem_limit_bytes=VMEM_LIMIT)


def _full(a):
    nd = a.ndim
    return pl.BlockSpec(a.shape, lambda *_: (0,) * nd)


def _rows(tm, w, cb=0, rev_n=None):
    if rev_n is None:
        return pl.BlockSpec((tm, w), lambda i: (i, cb))
    return pl.BlockSpec((tm, w), lambda i: (rev_n - 1 - i, cb))


def _call(body, name, grid, in_specs, out_specs, out_shape, scratch=(), n_axes=1):
    return pl.pallas_call(body, name=name, grid=grid, in_specs=in_specs, out_specs=out_specs, out_shape=out_shape,
                          scratch_shapes=list(scratch), compiler_params=_cparams(n_axes))


def _sds(shape, dt=F32):
    return jax.ShapeDtypeStruct(shape, dt)


def norm_matmul(x, g, w, tm, name, ydt=F32):
    T, n = x.shape[0], w.shape[1]

    def body(x_ref, g_ref, w_ref, y_ref, h_ref):
        hb = rmsnorm(x_ref[...], g_ref[...]).astype(MX)
        h_ref[...] = hb
        y_ref[...] = jnp.dot(hb, w_ref[...], preferred_element_type=F32).astype(ydt)

    return _call(body, name, (T // tm,), [_rows(tm, D), _full(g), _full(w)],
                 [_rows(tm, n), _rows(tm, D)], [_sds((T, n), ydt), _sds((T, D), MX)])(x, g, w)


CPB = 4


def _chunk(c):
    return slice(c * CH, (c + 1) * CH)


def gmlp_fwd(p, ws, bs, lg, lb):
    T = p.shape[0]
    tm = CPB * CH

    def body(pa_ref, ws_ref, bs_ref, lg_ref, lb_ref, o_ref):
        for c in range(CPB):
            for h in range(4):
                o_ref[_chunk(c), h * 128:(h + 1) * 128] = gmlp_head(
                    pa_ref[_chunk(c), h * 128:(h + 1) * 128], pa_ref[_chunk(c), 512 + h * 128:512 + (h + 1) * 128],
                    ws_ref[h], bs_ref[h], lg_ref[h], lb_ref[h]).astype(MX)

    return _call(body, "gmlp_fwd", (T // tm,), [_rows(tm, 1024), _full(ws), _full(bs), _full(lg), _full(lb)],
                 _rows(tm, 512), _sds((T, 512), MX))(p, ws, bs, lg, lb)


def _gla_in_specs(tm, n, rev):
    r = n if rev else None
    return [_rows(tm, 256, 4, r), _rows(tm, 256, 5, r), _rows(tm, 512, 3, r), _rows(tm, 128, 20, r)]


def gla_fwd(p, wg, bg, reverse):
    T = p.shape[0]
    tm = CPB * CH
    n = T // tm
    rev = n if reverse else None

    def body(q_ref, k_ref, v_ref, r_ref, wg_ref, bg_ref, o_ref, ss_ref, st_ref):
        @pl.when(pl.program_id(0) == 0)
        def _():
            st_ref[...] = jnp.zeros_like(st_ref)

        consts = _tri(reverse) + _lane_masks()
        for j in range(2):
            sl = slice(j * 128, (j + 1) * 128)
            st0, st1 = st_ref[2 * j], st_ref[2 * j + 1]
            for c in (reversed(range(CPB)) if reverse else range(CPB)):
                rows = _chunk(c)
                ss_ref[c, 2 * j] = st0
                ss_ref[c, 2 * j + 1] = st1
                o0, o1, st0, st1 = gla_pair(
                    consts, r_ref[rows, :], wg_ref[:, sl], bg_ref[:, sl], q_ref[rows, sl], k_ref[rows, sl],
                    v_ref[rows, 256 * j:256 * j + 128], v_ref[rows, 256 * j + 128:256 * j + 256], st0, st1)
                o_ref[rows, 256 * j:256 * j + 128] = o0
                o_ref[rows, 256 * j + 128:256 * j + 256] = o1
            st_ref[2 * j] = st0
            st_ref[2 * j + 1] = st1

    ss_spec = pl.BlockSpec((CPB, 4, 128, 128), (lambda i: (n - 1 - i, 0, 0, 0)) if reverse else (lambda i: (i, 0, 0, 0)))
    return _call(body, "gla_fwd_r" if reverse else "gla_fwd_f", (n,),
                 _gla_in_specs(tm, n, reverse) + [_full(wg), _full(bg)],
                 [_rows(tm, 512, 0, rev), ss_spec], [_sds((T, 512)), _sds((T // CH, 4, 128, 128))],
                 scratch=[pltpu.VMEM((4, 128, 128), F32)])(p, p, p, p, wg, bg)


def mix_out(x, of, ob, p, outa, gg, w_out, tm):
    T = x.shape[0]

    def body(x_ref, of_ref, ob_ref, pg_ref, oa_ref, gg_ref, w_ref, x1_ref, mx_ref):
        mx_ref[:, 0:512] = oa_ref[...]
        for h in range(4):
            sl = slice(h * 128, (h + 1) * 128)
            mx_ref[:, 512 + h * 128:512 + (h + 1) * 128] = outb_head(
                of_ref[:, sl] + ob_ref[:, sl], pg_ref[:, sl], gg_ref[h]).astype(MX)
        x1_ref[...] = x_ref[...] + jnp.dot(mx_ref[...], w_ref[...], preferred_element_type=F32)

    return _call(body, "mix_out", (T // tm,),
                 [_rows(tm, D), _rows(tm, 512), _rows(tm, 512), _rows(tm, 512, 4), _rows(tm, 512), _full(gg), _full(w_out)],
                 [_rows(tm, D), _rows(tm, 1024)], [_sds((T, D)), _sds((T, 1024), MX)])(x, of, ob, p, outa, gg, w_out)


HALO = 16


def _halo_specs(T, tm, w):
    nb = T // HALO
    r = tm // HALO
    return [pl.BlockSpec((tm, w), lambda i: (i, 0)),
            pl.BlockSpec((HALO, w), lambda i: (jnp.maximum(i * r - 1, 0), 0)),
            pl.BlockSpec((HALO, w), lambda i: (jnp.minimum((i + 1) * r, nb - 1), 0))]


def _shifted(main, prev, nxt, i, nsteps):
    tm = main.shape[0]
    row = lax.broadcasted_iota(jnp.int32, (tm, 1), 0)
    pr = jnp.where(i > 0, prev[HALO - 1:HALO, :].astype(F32), 0.0)
    nx = jnp.where(i < nsteps - 1, nxt[0:1, :].astype(F32), 0.0)
    dn = jnp.where(row == 0, pr, pltpu.roll(main, 1, 0))
    up = jnp.where(row == tm - 1, nx, pltpu.roll(main, tm - 1, 0))
    return dn, up


def ffn_down(x1, zu, cw, cb, w_down, tm):
    T = x1.shape[0]
    ns = T // tm

    def body(x_ref, zu_ref, zp_ref, zn_ref, cw_ref, cb_ref, w_ref, x2_ref, z_ref, a_ref):
        zu = zu_ref[...].astype(F32)
        dn, up = _shifted(zu, zp_ref[...], zn_ref[...], pl.program_id(0), ns)
        z = cb_ref[...] + dn * cw_ref[0:1, :] + zu * cw_ref[1:2, :] + up * cw_ref[2:3, :]
        z_ref[...] = z.astype(MX)
        a = ffn_act(z[:, :DFF], z[:, DFF:]).astype(MX)
        a_ref[...] = a
        x2_ref[...] = x_ref[...] + jnp.dot(a, w_ref[...], preferred_element_type=F32)

    return _call(body, "ffn_down", (ns,), [_rows(tm, D)] + _halo_specs(T, tm, NUP) + [_full(cw), _full(cb), _full(w_down)],
                 [_rows(tm, D), _rows(tm, NUP), _rows(tm, DFF)],
                 [_sds((T, D)), _sds((T, NUP), MX), _sds((T, DFF), MX)])(x1, zu, zu, zu, cw, cb, w_down)


def loss_head(x, g, tgt, tm):
    T = x.shape[0]

    def body(x_ref, g_ref, t_ref, l_ref, dx_ref, dg_ref):
        @pl.when(pl.program_id(0) == 0)
        def _():
            l_ref[...] = jnp.zeros_like(l_ref)
            dg_ref[...] = jnp.zeros_like(dg_ref)

        y, vjp = jax.vjp(rmsnorm, x_ref[...], g_ref[...])
        err = y - t_ref[...]
        l_ref[...] += jnp.sum(err * err, axis=0, keepdims=True)
        dx, dg = vjp(err * (1.0 / D))
        dx_ref[...] = dx
        dg_ref[...] += dg

    return _call(body, "loss_head", (T // tm,), [_rows(tm, D), _full(g), _rows(tm, D)],
                 [_full(g), _rows(tm, D), _full(g)], [_sds((1, D)), _sds((T, D)), _sds((1, D))])(x, g, tgt)


def ffn_down_bwd(dx2, z, w_down, tm):
    T = dx2.shape[0]

    def body(dx_ref, z_ref, w_ref, dz_ref):
        da = _dg(dx_ref[...], w_ref[...], 1, 1)
        _, vjp = jax.vjp(ffn_act, z_ref[:, :DFF].astype(F32), z_ref[:, DFF:].astype(F32))
        dzg, dzv = vjp(da)
        dz_ref[:, :DFF] = dzg.astype(MX)
        dz_ref[:, DFF:] = dzv.astype(MX)

    return _call(body, "ffn_down_bwd", (T // tm,), [_rows(tm, D), _rows(tm, NUP), _full(w_down)],
                 _rows(tm, NUP), _sds((T, NUP), MX))(dx2, z, w_down)


def ffn_conv_bwd(dz, zu, cw, tm):
    T = dz.shape[0]
    ns = T // tm

    def body(dz_ref, dp_ref, dn_ref, zu_ref, cw_ref, dzu_ref, dcw_ref, dcb_ref):
        i = pl.program_id(0)

        @pl.when(i == 0)
        def _():
            dcw_ref[...] = jnp.zeros_like(dcw_ref)
            dcb_ref[...] = jnp.zeros_like(dcb_ref)

        dz = dz_ref[...].astype(F32)
        zu = zu_ref[...].astype(F32)
        ddn, dup = _shifted(dz, dp_ref[...], dn_ref[...], i, ns)
        dzu_ref[...] = (dup * cw_ref[0:1, :] + dz * cw_ref[1:2, :] + ddn * cw_ref[2:3, :]).astype(MX)
        dcw_ref[0:1, :] += jnp.sum(zu * dup, axis=0, keepdims=True)
        dcw_ref[1:2, :] += jnp.sum(zu * dz, axis=0, keepdims=True)
        dcw_ref[2:3, :] += jnp.sum(zu * ddn, axis=0, keepdims=True)
        dcb_ref[...] += jnp.sum(dz, axis=0, keepdims=True)

    return _call(body, "ffn_conv_bwd", (ns,), _halo_specs(T, tm, NUP) + [_rows(tm, NUP), _full(cw)],
                 [_rows(tm, NUP), _full(cw), pl.BlockSpec((1, NUP), lambda i: (0, 0))],
                 [_sds((T, NUP), MX), _sds((3, NUP)), _sds((1, NUP))])(dz, dz, dz, zu, cw)


def nt_normbwd(dy, w, x, g, dres, tm, name):
    T, k = dy.shape

    def body(dy_ref, w_ref, x_ref, g_ref, dr_ref, dx_ref, dg_ref):
        @pl.when(pl.program_id(0) == 0)
        def _():
            dg_ref[...] = jnp.zeros_like(dg_ref)

        dh = _dg(dy_ref[...], w_ref[...], 1, 1)
        _, vjp = jax.vjp(rmsnorm, x_ref[...], g_ref[...])
        dx, dg = vjp(dh)
        dx_ref[...] = dr_ref[...] + dx
        dg_ref[...] += dg

    return _call(body, name, (T // tm,), [_rows(tm, k), _full(w), _rows(tm, D), _full(g), _rows(tm, D)],
                 [_rows(tm, D), _full(g)], [_sds((T, D)), _sds((1, D))])(dy, w, x, g, dres)


def matmul_tn(a, b, tt, tn, name):
    T, k = a.shape
    n = b.shape[1]
    last = T // tt - 1

    def body(a_ref, b_ref, o_ref, acc_ref):
        @pl.when(pl.program_id(1) == 0)
        def _():
            acc_ref[...] = jnp.zeros_like(acc_ref)

        acc_ref[...] += _dg(a_ref[...], b_ref[...], 0, 0)

        @pl.when(pl.program_id(1) == last)
        def _():
            o_ref[...] = acc_ref[...].astype(MX)

    return _call(body, name, (n // tn, T // tt),
                 [pl.BlockSpec((tt, k), lambda j, i: (i, 0)), pl.BlockSpec((tt, tn), lambda j, i: (i, j))],
                 pl.BlockSpec((k, tn), lambda j, i: (0, j)), _sds((k, n), MX), scratch=[pltpu.VMEM((k, tn), F32)],
                 n_axes=2)(a, b)


def mix_out_bwd(dx1, w_out, of, ob, p, gg, tm):
    T = dx1.shape[0]

    def body(dx_ref, w_ref, of_ref, ob_ref, pg_ref, gg_ref, da_ref, do_ref, dpg_ref, dgg_ref):
        @pl.when(pl.program_id(0) == 0)
        def _():
            dgg_ref[...] = jnp.zeros_like(dgg_ref)

        dxb = dx_ref[...].astype(MX)
        da_ref[...] = _dg(dxb, w_ref[0:512, :], 1, 1)
        for h in range(4):
            sl = slice(h * 128, (h + 1) * 128)
            dm = _dg(dxb, w_ref[512 + h * 128:512 + (h + 1) * 128, :], 1, 1)
            _, vjp = jax.vjp(outb_head, of_ref[:, sl] + ob_ref[:, sl], pg_ref[:, sl], gg_ref[h])
            do, dpg, dg = vjp(dm)
            do_ref[:, sl] = do
            dpg_ref[:, sl] = dpg
            dgg_ref[h] += dg

    return _call(body, "mix_out_bwd", (T // tm,),
                 [_rows(tm, D), _full(w_out), _rows(tm, 512), _rows(tm, 512), _rows(tm, 512, 4), _full(gg)],
                 [_rows(tm, 512), _rows(tm, 512), _rows(tm, 512), _full(gg)],
                 [_sds((T, 512)), _sds((T, 512)), _sds((T, 512)), _sds(gg.shape)])(dx1, w_out, of, ob, p, gg)


def gla_bwd(p, wg, bg, ss, do, reverse):
    T = p.shape[0]
    tm = CPB * CH
    n = T // tm
    rev = not reverse
    rn = n if rev else None

    def body(q_ref, k_ref, v_ref, r_ref, wg_ref, bg_ref, ss_ref, do_ref,
             dq_ref, dk_ref, dv_ref, dr_ref, dwg_ref, dbg_ref, dst_ref):
        @pl.when(pl.program_id(0) == 0)
        def _():
            dst_ref[...] = jnp.zeros_like(dst_ref)
            dwg_ref[...] = jnp.zeros_like(dwg_ref)
            dbg_ref[...] = jnp.zeros_like(dbg_ref)

        consts = _tri(reverse) + _lane_masks()
        for j in range(2):
            sl = slice(j * 128, (j + 1) * 128)
            v0s, v1s = slice(256 * j, 256 * j + 128), slice(256 * j + 128, 256 * j + 256)
            d0, d1 = dst_ref[2 * j], dst_ref[2 * j + 1]
            dwg, dbg = jnp.zeros((128, 128), F32), jnp.zeros((1, 128), F32)
            for c in (reversed(range(CPB)) if rev else range(CPB)):
                rows = _chunk(c)
                _, vjp = jax.vjp(functools.partial(gla_pair, consts), r_ref[rows, :], wg_ref[:, sl], bg_ref[:, sl],
                                 q_ref[rows, sl], k_ref[rows, sl], v_ref[rows, v0s], v_ref[rows, v1s],
                                 ss_ref[c, 2 * j], ss_ref[c, 2 * j + 1])
                g = vjp((do_ref[rows, v0s], do_ref[rows, v1s], d0, d1))
                if j == 0:
                    dr_ref[rows, :] = g[0]
                else:
                    dr_ref[rows, :] += g[0]
                dwg, dbg = dwg + g[1], dbg + g[2]
                dq_ref[rows, sl] = g[3]
                dk_ref[rows, sl] = g[4]
                dv_ref[rows, v0s] = g[5]
                dv_ref[rows, v1s] = g[6]
                d0, d1 = g[7], g[8]
            dst_ref[2 * j] = d0
            dst_ref[2 * j + 1] = d1
            dwg_ref[:, sl] += dwg
            dbg_ref[:, sl] += dbg

    ss_spec = pl.BlockSpec((CPB, 4, 128, 128), (lambda i: (n - 1 - i, 0, 0, 0)) if rev else (lambda i: (i, 0, 0, 0)))
    return _call(body, "gla_bwd_r" if reverse else "gla_bwd_f", (n,),
                 _gla_in_specs(tm, n, rev) + [_full(wg), _full(bg), ss_spec, _rows(tm, 512, 0, rn)],
                 [_rows(tm, 256, 0, rn), _rows(tm, 256, 0, rn), _rows(tm, 512, 0, rn), _rows(tm, 128, 0, rn),
                  _full(wg), _full(bg)],
                 [_sds((T, 256)), _sds((T, 256)), _sds((T, 512)), _sds((T, 128)), _sds(wg.shape), _sds(bg.shape)],
                 scratch=[pltpu.VMEM((4, 128, 128), F32)])(p, p, p, p, wg, bg, ss, do)


def gmlp_bwd(p, douta, ws, bs, lg, lb):
    T = p.shape[0]
    tm = CPB * CH

    def body(pa_ref, do_ref, ws_ref, bs_ref, lg_ref, lb_ref, dpa_ref, dws_ref, dbs_ref, dlg_ref, dlb_ref):
        @pl.when(pl.program_id(0) == 0)
        def _():
            for r in (dws_ref, dbs_ref, dlg_ref, dlb_ref):
                r[...] = jnp.zeros_like(r)

        for h in range(4):
            us, vs = slice(h * 128, (h + 1) * 128), slice(512 + h * 128, 512 + (h + 1) * 128)
            acc = None
            for c in range(CPB):
                rows = _chunk(c)
                _, vjp = jax.vjp(gmlp_head, pa_ref[rows, us], pa_ref[rows, vs], ws_ref[h], bs_ref[h], lg_ref[h], lb_ref[h])
                du, dv, *dparams = vjp(do_ref[rows, us])
                dpa_ref[rows, us] = du
                dpa_ref[rows, vs] = dv
                acc = dparams if acc is None else [a + b for a, b in zip(acc, dparams)]
            for r, a in zip((dws_ref, dbs_ref, dlg_ref, dlb_ref), acc):
                r[h] += a

    return _call(body, "gmlp_bwd", (T // tm,),
                 [_rows(tm, 1024), _rows(tm, 512), _full(ws), _full(bs), _full(lg), _full(lb)],
                 [_rows(tm, 1024), _full(ws), _full(bs), _full(lg), _full(lb)],
                 [_sds((T, 1024)), _sds(ws.shape), _sds(bs.shape), _sds(lg.shape), _sds(lb.shape)])(p, douta, ws, bs, lg, lb)


def assemble_dp(dpa, dqf, dqb, dkf, dkb, dvf, dvb, dpg, drf, drb, tm):
    T = dpa.shape[0]

    def body(a_ref, qf, qb, kf, kb, vf, vb, g_ref, rf, rb, o_ref):
        o_ref[:, 0:1024] = a_ref[...].astype(MX)
        o_ref[:, 1024:1280] = (qf[...] + qb[...]).astype(MX)
        o_ref[:, 1280:1536] = (kf[...] + kb[...]).astype(MX)
        o_ref[:, 1536:2048] = (vf[...] + vb[...]).astype(MX)
        o_ref[:, 2048:2560] = g_ref[...].astype(MX)
        o_ref[:, 2560:2688] = (rf[...] + rb[...]).astype(MX)

    ins = [dpa, dqf, dqb, dkf, dkb, dvf, dvb, dpg, drf, drb]
    return _call(body, "assemble_dp", (T // tm,), [_rows(tm, a.shape[1]) for a in ins],
                 _rows(tm, N_INP), _sds((T, N_INP), MX))(*ins)


def _gate_pad(w, row0):
    return jnp.zeros((128, 256), F32).at[row0:row0 + 16].set(w)


def local_step(x, tgt, W, get_big, emit, tm=256):
    saved = []
    for l in range(NL):
        s = {"x": x}
        s.update(get_big(l, "in", x))
        p, s["h"] = norm_matmul(x, W["g_mix"][l][None], s["w_in"], tm, "mix_in")
        s["p"] = p
        ws, bs = W["w_s"][l], W["b_s"][l][:, :, None]
        lg, lb = W["ln_g"][l][:, None, :], W["ln_b"][l][:, None, :]
        outa = gmlp_fwd(p, ws, bs, lg, lb)
        wgf, wgb = _gate_pad(W["w_gate_f"][l], 0), _gate_pad(W["w_gate_b"][l], 16)
        bgf, bgb = W["b_gate_f"][l][None], W["b_gate_b"][l][None]
        s["of"], s["ssf"] = gla_fwd(p, wgf, bgf, False)
        s["ob"], s["ssb"] = gla_fwd(p, wgb, bgb, True)
        s.update(get_big(l, "rest", s["ob"]))
        gg = W["g_gla"][l][:, None, :]
        x1, s["mixed"] = mix_out(x, s["of"], s["ob"], p, outa, gg, s["w_out"], tm)
        s["x1"] = x1
        s["zu"], s["h2"] = norm_matmul(x1, W["g_ffn"][l][None], s["w_up"], tm, "ffn_up", MX)
        x, s["z"], s["a"] = ffn_down(x1, s["zu"], W["conv_w"][l], W["conv_b"][l][None], s["w_down"], tm)
        saved.append(s)

    lsum, dx, dgf = loss_head(x, W["g_final"][None], tgt, tm)
    G = {k: [None] * NL for k in _SMALL if k != "g_final"}
    tok = jnp.zeros((1, 1), F32)
    for l in reversed(range(NL)):
        s = saved[l]
        g_down = matmul_tn(s["a"], dx, 512, 512, "dw_down")
        dz = ffn_down_bwd(dx, s["z"], s["w_down"], tm)
        dzu, G["conv_w"][l], dcb = ffn_conv_bwd(dz, s["zu"], W["conv_w"][l] + tok, tm)
        G["conv_b"][l] = dcb[0]
        g_up = matmul_tn(s["h2"], dzu, 512, 1408, "dw_up")
        tok = emit(l, "A", {"w_down": g_down, "w_up": g_up})
        dx1, dg = nt_normbwd(dzu, s["w_up"], s["x1"], W["g_ffn"][l][None] + tok, dx, tm, "ffn_up_bwd")
        G["g_ffn"][l] = dg[0]
        g_out = matmul_tn(s["mixed"], dx1, 512, 512, "dw_out")
        gg = W["g_gla"][l][:, None, :]
        douta, do, dpg, dgg = mix_out_bwd(dx1, s["w_out"], s["of"], s["ob"], s["p"], gg, tm)
        G["g_gla"][l] = dgg[:, 0, :]
        wgf, wgb = _gate_pad(W["w_gate_f"][l], 0), _gate_pad(W["w_gate_b"][l], 16)
        bgf, bgb = W["b_gate_f"][l][None], W["b_gate_b"][l][None]
        dqf, dkf, dvf, drf, dwgf, dbgf = gla_bwd(s["p"], wgf, bgf, s["ssf"], do, False)
        dqb, dkb, dvb, drb, dwgb, dbgb = gla_bwd(s["p"], wgb, bgb, s["ssb"], do, True)
        G["w_gate_f"][l], G["b_gate_f"][l] = dwgf[0:16], dbgf[0]
        G["w_gate_b"][l], G["b_gate_b"][l] = dwgb[16:32], dbgb[0]
        ws, bs = W["w_s"][l], W["b_s"][l][:, :, None]
        lg, lb = W["ln_g"][l][:, None, :], W["ln_b"][l][:, None, :]
        dpa, G["w_s"][l], dbs, dlg, dlb = gmlp_bwd(s["p"], douta, ws, bs, lg, lb)
        G["b_s"][l], G["ln_g"][l], G["ln_b"][l] = dbs[:, :, 0], dlg[:, 0, :], dlb[:, 0, :]
        dpc = assemble_dp(dpa, dqf, dqb, dkf, dkb, dvf, dvb, dpg, drf, drb, tm)
        g_in = matmul_tn(s["h"], dpc, 512, 896, "dw_in")
        tok = emit(l, "B", {"w_out": g_out, "w_in": g_in})
        dx, dg = nt_normbwd(dpc, s["w_in"], s["x"], W["g_mix"][l][None] + tok, dx1, tm, "mix_in_bwd")
        G["g_mix"][l] = dg[0]
    G = {k: jnp.stack(v) for k, v in G.items()}
    G["g_final"] = dgf[0]
    return lsum, dx, G


def cast_bf16(a, tr):
    r, c = a.shape

    def body(a_ref, o_ref):
        o_ref[...] = a_ref[...].astype(BF16)

    return _call(body, "cast_bf16", (r // tr,), [_rows(tr, c)], _rows(tr, c), _sds((r, c), BF16))(a)


def sum_lead(y, tr):
    n, rr, cc = y.shape

    def body(y_ref, o_ref):
        acc = y_ref[0].astype(F32)
        for k in range(1, n):
            acc = acc + y_ref[k].astype(F32)
        o_ref[...] = acc

    return _call(body, "sum_lead", (rr // tr,), [pl.BlockSpec((n, tr, cc), lambda i: (0, i, 0))],
                 _rows(tr, cc), _sds((rr, cc)))(y)


def adamw(w, ga, gb, m, v, tr):
    r, c = w.shape

    def body(w_ref, ga_ref, gb_ref, m_ref, v_ref, g_ref, d_ref, nm_ref, nv_ref):
        gr = ga_ref[...] + gb_ref[...]
        g_ref[...] = gr
        nm = ADAM_B1 * m_ref[...] + (1.0 - ADAM_B1) * gr
        nv = ADAM_B2 * v_ref[...] + (1.0 - ADAM_B2) * jnp.square(gr)
        m_hat = nm / (1.0 - ADAM_B1 ** ADAM_STEP)
        v_hat = nv / (1.0 - ADAM_B2 ** ADAM_STEP)
        d_ref[...] = -ADAM_LR * (m_hat / (jnp.sqrt(v_hat) + ADAM_EPS) + ADAM_WD * w_ref[...])
        nm_ref[...] = nm
        nv_ref[...] = nv

    sp = _rows(tr, c)
    return _call(body, "adamw", (r // tr,), [sp] * 5, [sp] * 4, [_sds((r, c))] * 4)(w, ga, gb, m, v)


MESH = pl.DeviceIdType.MESH
ANY = pl.BlockSpec(memory_space=pl.ANY)
N_BIG = 4


def _pos():
    return lax.axis_index("x"), lax.axis_index("y"), lax.axis_index("c")


def _other_chips(x, y):
    return [(1 - x, y), (x, 1 - y), (1 - x, 1 - y)]


def _rcopy(src, dst, send_sems, recv_sems, k, to):
    return pltpu.make_async_remote_copy(src_ref=src, dst_ref=dst, send_sem=send_sems.at[k], recv_sem=recv_sems.at[k],
                                        device_id=to, device_id_type=MESH)


def allgather8(xs):
    m, n = xs.shape

    def body(x_ref, out_ref, send_sems, recv_sems, local_sem):
        x, y, c = _pos()
        me, sibling = (x, y, c), (x, y, 1 - c)
        chips = _other_chips(x, y)

        def rows(px, py, pc):
            return out_ref.at[pl.ds((4 * px + 2 * py + pc) * m, m), :]

        def copy(k, block, to, src=None):
            return _rcopy(rows(*block) if src is None else src, rows(*block), send_sems, recv_sems, k, to)

        mine = pltpu.make_async_copy(x_ref, rows(*me), local_sem)
        mine.start()
        first = [copy(0, me, sibling, src=x_ref)]
        first += [copy(1 + j, me, (*chip, c), src=x_ref) for j, chip in enumerate(chips)]
        for cp in first:
            cp.start()
        passed = [copy(4 + j, (*chip, c), sibling) for j, chip in enumerate(chips)]
        for j, chip in enumerate(chips):
            copy(1 + j, (*chip, c), me).wait_recv()
            passed[j].start()
        copy(0, sibling, me).wait_recv()
        for j, chip in enumerate(chips):
            copy(4 + j, (*chip, 1 - c), me).wait_recv()
        for cp in first + passed:
            cp.wait_send()
        mine.wait()

    vm = pl.BlockSpec(memory_space=pltpu.VMEM)
    return pl.pallas_call(
        body, name="allgather8", out_shape=_sds((8 * m, n), xs.dtype), in_specs=[vm], out_specs=vm,
        scratch_shapes=[pltpu.SemaphoreType.DMA((7,)), pltpu.SemaphoreType.DMA((7,)), pltpu.SemaphoreType.DMA],
        compiler_params=pltpu.CompilerParams(vmem_limit_bytes=VMEM_LIMIT))(xs)


def _slab(k, ref, j):
    if k == 0:
        return ref.at[j]
    if k == 1:
        return ref.at[pl.ds(256 * j, 256), :]
    if k == 2:
        return ref.at[:, pl.ds(1408 * j, 1408)]
    return ref.at[pl.ds(704 * j, 704), :]


_LAYER_FULL = [(4, 1024, 648), (1024, 1024), (1024, NUP), (DFF, 1024)]
_LAYER_SHARD = [(1024, 648), (256, 1024), (1024, 1408), (704, 1024)]
_SHARD_SHAPES = [(NL,) + s for s in _LAYER_SHARD]

HBM = pl.BlockSpec(memory_space=pltpu.HBM)
SEM = pl.BlockSpec(memory_space=pltpu.SEMAPHORE)
VM = pl.BlockSpec(memory_space=pltpu.VMEM)
EFFECT = pltpu.SideEffectType.DATAFLOW_SIDE_EFFECTING
_GW_GROUPS = [[(0, 0)], [(0, 1), (0, 2), (0, 3)]] + [[(l, k) for k in range(N_BIG)] for l in range(1, NL)]
_GW_ORDER = [lk for g in _GW_GROUPS for lk in g]


def _hbm(a):
    return pltpu.with_memory_space_constraint(a, pltpu.HBM)


def _hbm_like(a):
    return pltpu.HBM(a.shape, a.dtype)


def gw_start(shards, landings):
    n = len(_GW_ORDER)

    def body(*refs):
        S, Ld = refs[:N_BIG], refs[N_BIG:N_BIG + n]
        outs = refs[N_BIG + n:]
        send_sems, recv, token = outs[0], outs[1:1 + len(_GW_GROUPS)], outs[-1]
        x, y, c = _pos()
        me = 2 * x + y
        ci = 0
        for gi, grp in enumerate(_GW_GROUPS):
            for t, (l, k) in enumerate(grp):
                land = Ld[_GW_ORDER.index((l, k))]
                for j, (px, py) in enumerate(_other_chips(x, y)):
                    pltpu.make_async_remote_copy(
                        src_ref=S[k].at[l], dst_ref=_slab(k, land, me), send_sem=send_sems.at[ci],
                        recv_sem=recv[gi].at[3 * t + j], device_id=(px, py, c), device_id_type=MESH).start()
                    ci += 1
        token[...] = jnp.zeros_like(token)

    ins = list(shards) + list(landings)
    sems = [pltpu.SemaphoreType.DMA((3 * n,))] + [pltpu.SemaphoreType.DMA((3 * len(g),)) for g in _GW_GROUPS]
    outs = pl.pallas_call(
        body, name="gw_start", out_shape=sems + [_hbm_like(a) for a in ins] + [_sds((8, 128))],
        in_specs=[HBM] * len(ins), out_specs=[SEM] * len(sems) + [HBM] * len(ins) + [VM],
        input_output_aliases={i: len(sems) + i for i in range(len(ins))},
        compiler_params=pltpu.CompilerParams(has_side_effects=EFFECT))(*[_hbm(a) for a in ins])
    ns = len(sems)
    return outs[0], outs[1:ns], outs[ns:ns + N_BIG], outs[ns + N_BIG:ns + len(ins)], outs[-1]


def gw_wait(gi, landings, recv_sems, after, shards=None, send_sems=None):
    grp = _GW_GROUPS[gi]
    n = len(grp)
    last = shards is not None

    def body(*refs):
        Ld, rs = refs[:n], refs[n]
        x, y, c = _pos()
        for t, (l, k) in enumerate(grp):
            for j, (px, py) in enumerate(_other_chips(x, y)):
                region = _slab(k, Ld[t], 2 * px + py)
                pltpu.make_async_remote_copy(src_ref=region, dst_ref=region, send_sem=rs.at[3 * t + j],
                                             recv_sem=rs.at[3 * t + j], device_id=(px, py, c),
                                             device_id_type=MESH).wait_recv()
        if last:
            S, ss = refs[n + 2:n + 2 + N_BIG], refs[n + 2 + N_BIG]
            me = 2 * x + y
            for ci, (l, k) in enumerate(lk for lk in _GW_ORDER for _ in range(3)):
                pltpu.make_async_remote_copy(src_ref=S[k].at[l], dst_ref=_slab(k, Ld[k], me), send_sem=ss.at[ci],
                                             recv_sem=ss.at[ci], device_id=(x, y, c), device_id_type=MESH).wait_send()

    ins = list(landings) + [recv_sems, after]
    specs = [HBM] * n + [SEM, pl.BlockSpec(memory_space=pl.ANY)]
    outs = [_hbm_like(a) for a in landings]
    alias = {i: i for i in range(n)}
    if last:
        ins += list(shards) + [send_sems]
        specs += [HBM] * N_BIG + [SEM]
        outs += [_hbm_like(a) for a in shards]
        alias.update({n + 2 + i: n + i for i in range(N_BIG)})
    res = pl.pallas_call(body, name="gw_wait_%d" % gi, out_shape=outs, in_specs=specs, out_specs=[HBM] * len(outs),
                         input_output_aliases=alias,
                         compiler_params=pltpu.CompilerParams(has_side_effects=EFFECT))(*ins)
    return res[:n]


def ga_start(tag, ks, grads, landings):
    n = len(ks)

    def body(*refs):
        G, Ld = refs[:n], refs[n:2 * n]
        send_sems, recv_sems, token = refs[2 * n], refs[2 * n + 1], refs[-1]
        x, y, c = _pos()
        me = 2 * x + y
        for t, k in enumerate(ks):
            for j, (px, py) in enumerate(_other_chips(x, y)):
                pltpu.make_async_remote_copy(
                    src_ref=_slab(k, G[t], 2 * px + py), dst_ref=Ld[t].at[me], send_sem=send_sems.at[3 * t + j],
                    recv_sem=recv_sems.at[3 * t + j], device_id=(px, py, c), device_id_type=MESH).start()
        token[...] = jnp.zeros_like(token)

    ins = list(grads) + list(landings)
    sems = [pltpu.SemaphoreType.DMA((3 * n,))] * 2
    outs = pl.pallas_call(
        body, name="ga_start_" + tag, out_shape=sems + [_hbm_like(a) for a in ins] + [_sds((8, 128))],
        in_specs=[HBM] * len(ins), out_specs=[SEM, SEM] + [HBM] * len(ins) + [VM],
        input_output_aliases={i: 2 + i for i in range(len(ins))},
        compiler_params=pltpu.CompilerParams(has_side_effects=EFFECT))(*[_hbm(a) for a in ins])
    return outs[0], outs[1], outs[2:2 + n], outs[2 + n:2 + 2 * n], outs[-1]


def ga_wait(tag, ks, send_sems, recv_sems, grads, landings, after):
    n = len(ks)

    def body(*refs):
        G, Ld, ss, rs = refs[:n], refs[n:2 * n], refs[2 * n], refs[2 * n + 1]
        x, y, c = _pos()
        me = 2 * x + y
        for t, k in enumerate(ks):
            for j, (px, py) in enumerate(_other_chips(x, y)):
                pj = 2 * px + py
                cp = pltpu.make_async_remote_copy(
                    src_ref=_slab(k, G[t], pj), dst_ref=Ld[t].at[pj], send_sem=ss.at[3 * t + j],
                    recv_sem=rs.at[3 * t + j], device_id=(px, py, c), device_id_type=MESH)
                cp.wait_send()
                cp.wait_recv()

    ins = list(grads) + list(landings) + [send_sems, recv_sems, after]
    res = pl.pallas_call(
        body, name="ga_wait_" + tag, out_shape=[_hbm_like(a) for a in list(grads) + list(landings)],
        in_specs=[HBM] * (2 * n) + [SEM, SEM, pl.BlockSpec(memory_space=pl.ANY)], out_specs=[HBM] * (2 * n),
        input_output_aliases={i: i for i in range(2 * n)},
        compiler_params=pltpu.CompilerParams(has_side_effects=EFFECT))(*ins)
    return res[n:]


def swap4(parts):
    def body(*refs):
        Q, R = refs[:N_BIG], refs[N_BIG:2 * N_BIG]
        send_sems, recv_sems = refs[2 * N_BIG:]
        x, y, c = _pos()
        cps = [_rcopy(Q[k], R[k], send_sems, recv_sems, k, (x, y, 1 - c)) for k in range(N_BIG)]
        for cp in cps:
            cp.start()
        for cp in cps:
            cp.wait()

    return pl.pallas_call(
        body, name="swap4", out_shape=[_sds(s) for s in _SHARD_SHAPES],
        in_specs=[ANY] * N_BIG, out_specs=[ANY] * N_BIG,
        scratch_shapes=[pltpu.SemaphoreType.DMA((N_BIG,)), pltpu.SemaphoreType.DMA((N_BIG,))])(*parts)


_WEIGHTS = ['g_mix', 'w_in', 'w_s', 'b_s', 'ln_g', 'ln_b', 'w_gate_f', 'b_gate_f', 'w_gate_b', 'b_gate_b', 'g_gla',
            'w_out', 'g_ffn', 'w_up', 'conv_w', 'conv_b', 'w_down', 'g_final']
_BIG = ['w_in', 'w_out', 'w_up', 'w_down']
_SMALL = [n for n in _WEIGHTS if n not in _BIG]
_SMALL_SHARDED = {'w_gate_f': 64, 'w_gate_b': 64, 'conv_w': 1408}
_BIG_TR = {'w_in': 512, 'w_out': 256, 'w_up': 256, 'w_down': 352}


def _pack(arrs):
    flat = jnp.concatenate([a.reshape(-1) for a in arrs])
    pad = (-flat.shape[0]) % 1024
    return jnp.pad(flat, (0, pad)).reshape(-1, 128)


def _unpack(buf, shapes):
    flat = buf.reshape(-1)
    out, o = [], 0
    for s in shapes:
        n = 1
        for d in s:
            n *= d
        out.append(flat[o:o + n].reshape(s))
        o += n
    return out


def _take(k, full, chip):
    if k == 0:
        return lax.dynamic_index_in_dim(full, chip, 0, keepdims=False)
    if k == 2:
        return lax.dynamic_slice_in_dim(full, chip * 1408, 1408, axis=1)
    return lax.dynamic_slice_in_dim(full, chip * _LAYER_SHARD[k][0], _LAYER_SHARD[k][0], axis=0)


def _place(k, full, part, chip):
    if k == 0:
        return lax.dynamic_update_index_in_dim(full, part, chip, 0)
    if k == 2:
        return lax.dynamic_update_slice_in_dim(full, part, chip * 1408, axis=1)
    return lax.dynamic_update_slice_in_dim(full, part, chip * _LAYER_SHARD[k][0], axis=0)


def _place_lead(buf, part, chip):
    return lax.dynamic_update_index_in_dim(buf, part, chip, 0)


def kernel(x, g_mix, w_in, w_s, b_s, ln_g, ln_b, w_gate_f, b_gate_f, w_gate_b, b_gate_b, g_gla, w_out, g_ffn, w_up, conv_w, conv_b, w_down, g_final, loss_target, m_g_mix, m_w_in, m_w_s, m_b_s, m_ln_g, m_ln_b, m_w_gate_f, m_b_gate_f, m_w_gate_b, m_b_gate_b, m_g_gla, m_w_out, m_g_ffn, m_w_up, m_conv_w, m_conv_b, m_w_down, m_g_final, v_g_mix, v_w_in, v_w_s, v_b_s, v_ln_g, v_ln_b, v_w_gate_f, v_b_gate_f, v_w_gate_b, v_b_gate_b, v_g_gla, v_w_out, v_g_ffn, v_w_up, v_conv_w, v_conv_b, v_w_down, v_g_final):
    loc = locals()
    w = {n: loc[n] for n in _WEIGHTS}
    m = {n: loc["m_" + n] for n in _WEIGHTS}
    v = {n: loc["v_" + n] for n in _WEIGHTS}
    xi, yi, _ = _pos()
    chip = 2 * xi + yi

    sh_names = list(_SMALL_SHARDED)
    g8 = allgather8(_pack([w[n] for n in sh_names]))
    rows = g8.shape[0] // 8
    per_chip = [_unpack(g8[2 * j * rows:(2 * j + 1) * rows], [w[n].shape for n in sh_names]) for j in range(4)]
    W = dict(w)
    for k, n in enumerate(sh_names):
        W[n] = jnp.concatenate([per_chip[j][k] for j in range(4)], axis=-1)

    shards = [cast_bf16(w[n].reshape(-1, w[n].shape[-1]), _BIG_TR[n]).reshape(w[n].shape) for n in _BIG]
    landings = [_place(k, lax.empty(_LAYER_FULL[k], BF16), shards[k][l], chip) for l, k in _GW_ORDER]
    send_sems, recv_sems, shards_fly, landings_fly, _ = gw_start(shards, landings)
    arrived = {}

    def get_big(l, stage, after):
        gi = {(0, "in"): 0, (0, "rest"): 1}.get((l, stage), l + 1 if stage == "in" else None)
        if gi is not None:
            lo = sum(len(g) for g in _GW_GROUPS[:gi])
            lands = landings_fly[lo:lo + len(_GW_GROUPS[gi])]
            if gi == len(_GW_GROUPS) - 1:
                full = gw_wait(gi, lands, recv_sems[gi], after, shards_fly, send_sems)
            else:
                full = gw_wait(gi, lands, recv_sems[gi], after)
            arrived.update(zip(_GW_GROUPS[gi], full))
        if stage == "in":
            f_in = jnp.transpose(arrived[(l, 0)], (1, 0, 2)).reshape(D, N_IN)
            return {"w_in": jnp.pad(f_in, ((0, 0), (0, N_INP - N_IN)))}
        return {"w_out": arrived[(l, 1)], "w_up": arrived[(l, 2)], "w_down": arrived[(l, 3)]}

    flying = []

    def emit(l, group, grads):
        ks = [3, 2] if group == "A" else [1, 0]
        gs = [grads[_BIG[k]] for k in ks]
        if group == "B":
            gs[1] = jnp.transpose(gs[1][:, :N_IN].reshape(D, 4, 648), (1, 0, 2))
        lands = [_place_lead(lax.empty((4,) + _LAYER_SHARD[k], BF16), _take(k, g, chip), chip) for k, g in zip(ks, gs)]
        tag = "%d%s" % (l, group)
        ss, rs, gs_fly, lands_fly, tok = ga_start(tag, ks, gs, lands)
        flying.append((tag, l, ks, ss, rs, gs_fly, lands_fly))
        return tok[0:1, 0:1]

    lsum, grad_x, G = local_step(x[0], loss_target[0], W, get_big, emit)

    plane = [[None] * NL for _ in range(N_BIG)]
    for tag, l, ks, ss, rs, gs_fly, lands_fly in flying:
        for k, a in zip(ks, ga_wait(tag, ks, ss, rs, gs_fly, lands_fly, grad_x)):
            plane[k][l] = sum_lead(a.reshape(4, -1, a.shape[-1]), _BIG_TR[_BIG[k]]).reshape(_LAYER_SHARD[k])
    plane = [jnp.stack(p) for p in plane]
    other = swap4(plane)

    small_shapes = [G[n].shape for n in _SMALL] + [(D,)]
    pk = _pack([G[n] for n in _SMALL] + [lsum])
    srows = pk.shape[0]
    red = sum_lead(allgather8(pk).reshape(8, srows, 128), srows)
    small = dict(zip(_SMALL + ["lsum"], _unpack(red, small_shapes)))
    loss = 0.5 * jnp.sum(small.pop("lsum")) / D
    for n, wd in _SMALL_SHARDED.items():
        small[n] = lax.dynamic_slice_in_dim(small[n], chip * wd, wd, axis=small[n].ndim - 1)

    grads, delta, new_m, new_v = dict(small), {}, {}, {}
    two = lambda a: a.reshape(-1, a.shape[-1])
    for k, n in enumerate(_BIG):
        res = adamw(two(w[n]), two(plane[k]), two(other[k]), two(m[n]), two(v[n]), _BIG_TR[n])
        grads[n], delta[n], new_m[n], new_v[n] = (r.reshape(w[n].shape) for r in res)
    shapes = [w[n].shape for n in _SMALL]
    pw, pg, pm, pv = (_pack([t[n] for n in _SMALL]) for t in (w, grads, m, v))
    _, d_, m_, v_ = adamw(pw, pg, jnp.zeros_like(pg), pm, pv, pw.shape[0])
    for t, buf in ((delta, d_), (new_m, m_), (new_v, v_)):
        t.update(zip(_SMALL, _unpack(buf, shapes)))

    return (loss, grad_x[None], *[grads[n] for n in _WEIGHTS], *[delta[n] for n in _WEIGHTS],
            *[new_m[n] for n in _WEIGHTS], *[new_v[n] for n in _WEIGHTS])
```

```python
import functools

import jax
import jax.numpy as jnp
from jax import lax
from jax.experimental import pallas as pl
from jax.experimental.pallas import tpu as pltpu

F32 = jnp.float32
BF16 = jnp.bfloat16
MX = BF16

D = 1024
CH = 128
NL = 4
N_IN = 2592
N_INP = 2688
NUP = 5632
DFF = 2816
EPS = 1e-6
VMEM_LIMIT = 56 * 1024 * 1024

ADAM_LR, ADAM_B1, ADAM_B2, ADAM_EPS, ADAM_WD, ADAM_STEP = 0.001, 0.9, 0.999, 1e-08, 0.01, 10


def _dg(a, b, ca, cb):
    return lax.dot_general(a.astype(MX), b.astype(MX), (((ca,), (cb,)), ((), ())), preferred_element_type=F32)


@jax.custom_vjp
def mm(a, b):
    return _dg(a, b, 1, 0)


mm.defvjp(lambda a, b: (_dg(a, b, 1, 0), (a, b)),
          lambda r, g: (_dg(g, r[1], 1, 1), _dg(r[0], g, 0, 0)))


@jax.custom_vjp
def mm_nt(a, b):
    return _dg(a, b, 1, 1)


mm_nt.defvjp(lambda a, b: (_dg(a, b, 1, 1), (a, b)),
             lambda r, g: (_dg(g, r[1], 1, 0), _dg(g, r[0], 0, 0)))


@jax.custom_vjp
def mm_tn(a, b):
    return _dg(a, b, 0, 0)


mm_tn.defvjp(lambda a, b: (_dg(a, b, 0, 0), (a, b)),
             lambda r, g: (_dg(r[1], g, 1, 1), _dg(r[0], g, 1, 0)))


def _split3(x):
    hi = x.astype(BF16)
    r1 = x - hi.astype(F32)
    mid = r1.astype(BF16)
    lo = (r1 - mid.astype(F32)).astype(BF16)
    return hi, mid, lo


def _dot3(m, x):
    hi, mid, lo = _split3(x)
    d = lambda p: lax.dot_general(m, p, (((1,), (0,)), ((), ())), preferred_element_type=F32)
    return d(hi) + d(mid) + d(lo)


@jax.custom_vjp
def cumdot(m, mt, x):
    return _dot3(m, x)


cumdot.defvjp(lambda m, mt, x: (_dot3(m, x), (m, mt)),
              lambda r, g: (jnp.zeros_like(r[0]), jnp.zeros_like(r[1]), _dot3(r[1], g)))


def rmsnorm(x, g):
    return x * lax.rsqrt(jnp.mean(x * x, axis=-1, keepdims=True) + EPS) * g


def gelu(x):
    return 0.5 * x * (1.0 + lax.erf(x * 0.7071067811865476))


def sigmoid(x):
    return 1.0 / (1.0 + jnp.exp(-x))


def log_sigmoid(x):
    return jnp.minimum(x, 0.0) - jnp.log(1.0 + jnp.exp(-jnp.abs(x)))


def gmlp_head(u_pre, v_pre, w, bcol, g, b):
    u = gelu(u_pre)
    v = gelu(v_pre)
    mu = jnp.mean(v, axis=-1, keepdims=True)
    var = jnp.mean(jnp.square(v - mu), axis=-1, keepdims=True)
    vn = (v - mu) * lax.rsqrt(var + EPS) * g + b
    return u * (mm(w, vn) + bcol)


def outb_head(o, pg, g):
    return rmsnorm(o, g) * (pg * sigmoid(pg))


def ffn_act(zg, zv):
    return zg * sigmoid(zg) * zv


def _tri(reverse):
    r = lax.broadcasted_iota(jnp.int32, (CH, CH), 0)
    c = lax.broadcasted_iota(jnp.int32, (CH, CH), 1)
    if reverse:
        cm, sm = c >= r, c > r
    else:
        cm, sm = c <= r, c <= r
    one = jnp.ones((), BF16)
    zero = jnp.zeros((), BF16)
    return jnp.where(cm, one, zero), jnp.where(cm.T, one, zero), sm


def gla_pair(consts, pr, wg, bg, qp, kp, v0, v1, st0, st1):
    m, mt, smask, lm0, lm1 = consts
    la = log_sigmoid(mm(pr, wg) + bg) * (1.0 / 16.0)
    cum = cumdot(m, mt, la)
    tot = jnp.sum(la, axis=0, keepdims=True)
    q_dec = (qp * 0.125) * jnp.exp(cum)
    k_inv = kp * jnp.exp(-cum)
    k_end = kp * jnp.exp(tot - cum)
    dec = jnp.exp(tot)
    outs = []
    for lm, v, st in ((lm0, v0, st0), (lm1, v1, st1)):
        s = jnp.where(smask, mm_nt(q_dec * lm, k_inv), 0.0)
        o = mm(s, v) + mm_nt(q_dec, st)
        st_new = st * dec + mm_tn(v, k_end * lm)
        outs += [o, st_new]
    return outs[0], outs[2], outs[1], outs[3]


def _lane_masks():
    lane = lax.broadcasted_iota(jnp.int32, (1, 128), 1)
    return (lane < 64).astype(F32), (lane >= 64).astype(F32)


def _cparams(n_axes=1):
    return pltpu.CompilerParams(dimension_semantics=("arbitrary",) * n_axes, vmem_limit_bytes=VMEM_LIMIT)


def _full(a):
    nd = a.ndim
    return pl.BlockSpec(a.shape, lambda *_: (0,) * nd)


def _rows(tm, w, cb=0, rev_n=None):
    if rev_n is None:
        return pl.BlockSpec((tm, w), lambda i: (i, cb))
    return pl.BlockSpec((tm, w), lambda i: (rev_n - 1 - i, cb))


def _call(body, name, grid, in_specs, out_specs, out_shape, scratch=(), n_axes=1):
    return pl.pallas_call(body, name=name, grid=grid, in_specs=in_specs, out_specs=out_specs, out_shape=out_shape,
                          scratch_shapes=list(scratch), compiler_params=_cparams(n_axes))


def _sds(shape, dt=F32):
    return jax.ShapeDtypeStruct(shape, dt)


def norm_matmul(x, g, w, tm, name, ydt=F32):
    T, n = x.shape[0], w.shape[1]

    def body(x_ref, g_ref, w_ref, y_ref, h_ref):
        hb = rmsnorm(x_ref[...], g_ref[...]).astype(MX)
        h_ref[...] = hb
        y_ref[...] = jnp.dot(hb, w_ref[...], preferred_element_type=F32).astype(ydt)

    return _call(body, name, (T // tm,), [_rows(tm, D), _full(g), _full(w)],
                 [_rows(tm, n), _rows(tm, D)], [_sds((T, n), ydt), _sds((T, D), MX)])(x, g, w)


CPB = 4


def _chunk(c):
    return slice(c * CH, (c + 1) * CH)


def gmlp_fwd(p, ws, bs, lg, lb):
    T = p.shape[0]
    tm = CPB * CH

    def body(pa_ref, ws_ref, bs_ref, lg_ref, lb_ref, o_ref):
        for c in range(CPB):
            for h in range(4):
                o_ref[_chunk(c), h * 128:(h + 1) * 128] = gmlp_head(
                    pa_ref[_chunk(c), h * 128:(h + 1) * 128], pa_ref[_chunk(c), 512 + h * 128:512 + (h + 1) * 128],
                    ws_ref[h], bs_ref[h], lg_ref[h], lb_ref[h]).astype(MX)

    return _call(body, "gmlp_fwd", (T // tm,), [_rows(tm, 1024), _full(ws), _full(bs), _full(lg), _full(lb)],
                 _rows(tm, 512), _sds((T, 512), MX))(p, ws, bs, lg, lb)


def _gla_in_specs(tm, n, rev):
    r = n if rev else None
    return [_rows(tm, 256, 4, r), _rows(tm, 256, 5, r), _rows(tm, 512, 3, r), _rows(tm, 128, 20, r)]


def gla_fwd(p, wg, bg, reverse):
    T = p.shape[0]
    tm = CPB * CH
    n = T // tm
    rev = n if reverse else None

    def body(q_ref, k_ref, v_ref, r_ref, wg_ref, bg_ref, o_ref, ss_ref, st_ref):
        @pl.when(pl.program_id(0) == 0)
        def _():
            st_ref[...] = jnp.zeros_like(st_ref)

        consts = _tri(reverse) + _lane_masks()
        for j in range(2):
            sl = slice(j * 128, (j + 1) * 128)
            st0, st1 = st_ref[2 * j], st_ref[2 * j + 1]
            for c in (reversed(range(CPB)) if reverse else range(CPB)):
                rows = _chunk(c)
                ss_ref[c, 2 * j] = st0
                ss_ref[c, 2 * j + 1] = st1
                o0, o1, st0, st1 = gla_pair(
                    consts, r_ref[rows, :], wg_ref[:, sl], bg_ref[:, sl], q_ref[rows, sl], k_ref[rows, sl],
                    v_ref[rows, 256 * j:256 * j + 128], v_ref[rows, 256 * j + 128:256 * j + 256], st0, st1)
                o_ref[rows, 256 * j:256 * j + 128] = o0
                o_ref[rows, 256 * j + 128:256 * j + 256] = o1
            st_ref[2 * j] = st0
            st_ref[2 * j + 1] = st1

    ss_spec = pl.BlockSpec((CPB, 4, 128, 128), (lambda i: (n - 1 - i, 0, 0, 0)) if reverse else (lambda i: (i, 0, 0, 0)))
    return _call(body, "gla_fwd_r" if reverse else "gla_fwd_f", (n,),
                 _gla_in_specs(tm, n, reverse) + [_full(wg), _full(bg)],
                 [_rows(tm, 512, 0, rev), ss_spec], [_sds((T, 512)), _sds((T // CH, 4, 128, 128))],
                 scratch=[pltpu.VMEM((4, 128, 128), F32)])(p, p, p, p, wg, bg)


def mix_out(x, of, ob, p, outa, gg, w_out, tm):
    T = x.shape[0]

    def body(x_ref, of_ref, ob_ref, pg_ref, oa_ref, gg_ref, w_ref, x1_ref, mx_ref):
        mx_ref[:, 0:512] = oa_ref[...]
        for h in range(4):
            sl = slice(h * 128, (h + 1) * 128)
            mx_ref[:, 512 + h * 128:512 + (h + 1) * 128] = outb_head(
                of_ref[:, sl] + ob_ref[:, sl], pg_ref[:, sl], gg_ref[h]).astype(MX)
        x1_ref[...] = x_ref[...] + jnp.dot(mx_ref[...], w_ref[...], preferred_element_type=F32)

    return _call(body, "mix_out", (T // tm,),
                 [_rows(tm, D), _rows(tm, 512), _rows(tm, 512), _rows(tm, 512, 4), _rows(tm, 512), _full(gg), _full(w_out)],
                 [_rows(tm, D), _rows(tm, 1024)], [_sds((T, D)), _sds((T, 1024), MX)])(x, of, ob, p, outa, gg, w_out)


HALO = 16


def _halo_specs(T, tm, w):
    nb = T // HALO
    r = tm // HALO
    return [pl.BlockSpec((tm, w), lambda i: (i, 0)),
            pl.BlockSpec((HALO, w), lambda i: (jnp.maximum(i * r - 1, 0), 0)),
            pl.BlockSpec((HALO, w), lambda i: (jnp.minimum((i + 1) * r, nb - 1), 0))]


def _shifted(main, prev, nxt, i, nsteps):
    tm = main.shape[0]
    row = lax.broadcasted_iota(jnp.int32, (tm, 1), 0)
    pr = jnp.where(i > 0, prev[HALO - 1:HALO, :].astype(F32), 0.0)
    nx = jnp.where(i < nsteps - 1, nxt[0:1, :].astype(F32), 0.0)
    dn = jnp.where(row == 0, pr, pltpu.roll(main, 1, 0))
    up = jnp.where(row == tm - 1, nx, pltpu.roll(main, tm - 1, 0))
    return dn, up


def ffn_down(x1, zu, cw, cb, w_down, tm):
    T = x1.shape[0]
    ns = T // tm

    def body(x_ref, zu_ref, zp_ref, zn_ref, cw_ref, cb_ref, w_ref, x2_ref, z_ref, a_ref):
        zu = zu_ref[...].astype(F32)
        dn, up = _shifted(zu, zp_ref[...], zn_ref[...], pl.program_id(0), ns)
        z = cb_ref[...] + dn * cw_ref[0:1, :] + zu * cw_ref[1:2, :] + up * cw_ref[2:3, :]
        z_ref[...] = z.astype(MX)
        a = ffn_act(z[:, :DFF], z[:, DFF:]).astype(MX)
        a_ref[...] = a
        x2_ref[...] = x_ref[...] + jnp.dot(a, w_ref[...], preferred_element_type=F32)

    return _call(body, "ffn_down", (ns,), [_rows(tm, D)] + _halo_specs(T, tm, NUP) + [_full(cw), _full(cb), _full(w_down)],
                 [_rows(tm, D), _rows(tm, NUP), _rows(tm, DFF)],
                 [_sds((T, D)), _sds((T, NUP), MX), _sds((T, DFF), MX)])(x1, zu, zu, zu, cw, cb, w_down)


def loss_head(x, g, tgt, tm):
    T = x.shape[0]

    def body(x_ref, g_ref, t_ref, l_ref, dx_ref, dg_ref):
        @pl.when(pl.program_id(0) == 0)
        def _():
            l_ref[...] = jnp.zeros_like(l_ref)
            dg_ref[...] = jnp.zeros_like(dg_ref)

        y, vjp = jax.vjp(rmsnorm, x_ref[...], g_ref[...])
        err = y - t_ref[...]
        l_ref[...] += jnp.sum(err * err, axis=0, keepdims=True)
        dx, dg = vjp(err * (1.0 / D))
        dx_ref[...] = dx
        dg_ref[...] += dg

    return _call(body, "loss_head", (T // tm,), [_rows(tm, D), _full(g), _rows(tm, D)],
                 [_full(g), _rows(tm, D), _full(g)], [_sds((1, D)), _sds((T, D)), _sds((1, D))])(x, g, tgt)


def ffn_down_bwd(dx2, z, w_down, tm):
    T = dx2.shape[0]

    def body(dx_ref, z_ref, w_ref, dz_ref):
        da = _dg(dx_ref[...], w_ref[...], 1, 1)
        _, vjp = jax.vjp(ffn_act, z_ref[:, :DFF].astype(F32), z_ref[:, DFF:].astype(F32))
        dzg, dzv = vjp(da)
        dz_ref[:, :DFF] = dzg.astype(MX)
        dz_ref[:, DFF:] = dzv.astype(MX)

    return _call(body, "ffn_down_bwd", (T // tm,), [_rows(tm, D), _rows(tm, NUP), _full(w_down)],
                 _rows(tm, NUP), _sds((T, NUP), MX))(dx2, z, w_down)


def ffn_conv_bwd(dz, zu, cw, tm):
    T = dz.shape[0]
    ns = T // tm

    def body(dz_ref, dp_ref, dn_ref, zu_ref, cw_ref, dzu_ref, dcw_ref, dcb_ref):
        i = pl.program_id(0)

        @pl.when(i == 0)
        def _():
            dcw_ref[...] = jnp.zeros_like(dcw_ref)
            dcb_ref[...] = jnp.zeros_like(dcb_ref)

        dzb = dz_ref[...]
        dz = dzb.astype(F32)
        zu = zu_ref[...].astype(F32)
        r = lax.broadcasted_iota(jnp.int32, (tm, tm), 0)
        c = lax.broadcasted_iota(jnp.int32, (tm, tm), 1)
        row = lax.broadcasted_iota(jnp.int32, (tm, 1), 0)
        pr = jnp.where(i > 0, dp_ref[HALO - 1:HALO, :].astype(F32), 0.0)
        nx = jnp.where(i < ns - 1, dn_ref[0:1, :].astype(F32), 0.0)
        ddn = jnp.where(row == 0, pr, _dg(jnp.where(c == r - 1, 1.0, 0.0), dzb, 1, 0))
        dup = jnp.where(row == tm - 1, nx, _dg(jnp.where(c == r + 1, 1.0, 0.0), dzb, 1, 0))
        dzu_ref[...] = (dup * cw_ref[0:1, :] + dz * cw_ref[1:2, :] + ddn * cw_ref[2:3, :]).astype(MX)
        dcw_ref[0:1, :] += jnp.sum(zu * dup, axis=0, keepdims=True)
        dcw_ref[1:2, :] += jnp.sum(zu * dz, axis=0, keepdims=True)
        dcw_ref[2:3, :] += jnp.sum(zu * ddn, axis=0, keepdims=True)
        dcb_ref[...] += jnp.sum(dz, axis=0, keepdims=True)

    return _call(body, "ffn_conv_bwd", (ns,), _halo_specs(T, tm, NUP) + [_rows(tm, NUP), _full(cw)],
                 [_rows(tm, NUP), _full(cw), pl.BlockSpec((1, NUP), lambda i: (0, 0))],
                 [_sds((T, NUP), MX), _sds((3, NUP)), _sds((1, NUP))])(dz, dz, dz, zu, cw)


def nt_normbwd(dy, w, x, g, dres, tm, name):
    T, k = dy.shape

    def body(dy_ref, w_ref, x_ref, g_ref, dr_ref, dx_ref, dg_ref):
        @pl.when(pl.program_id(0) == 0)
        def _():
            dg_ref[...] = jnp.zeros_like(dg_ref)

        dh = _dg(dy_ref[...], w_ref[...], 1, 1)
        _, vjp = jax.vjp(rmsnorm, x_ref[...], g_ref[...])
        dx, dg = vjp(dh)
        dx_ref[...] = dr_ref[...] + dx
        dg_ref[...] += dg

    return _call(body, name, (T // tm,), [_rows(tm, k), _full(w), _rows(tm, D), _full(g), _rows(tm, D)],
                 [_rows(tm, D), _full(g)], [_sds((T, D)), _sds((1, D))])(dy, w, x, g, dres)


def matmul_tn(a, b, tt, tn, name):
    T, k = a.shape
    n = b.shape[1]
    last = T // tt - 1

    def body(a_ref, b_ref, o_ref, acc_ref):
        @pl.when(pl.program_id(1) == 0)
        def _():
            acc_ref[...] = jnp.zeros_like(acc_ref)

        acc_ref[...] += _dg(a_ref[...], b_ref[...], 0, 0)

        @pl.when(pl.program_id(1) == last)
        def _():
            o_ref[...] = acc_ref[...].astype(MX)

    return _call(body, name, (n // tn, T // tt),
                 [pl.BlockSpec((tt, k), lambda j, i: (i, 0)), pl.BlockSpec((tt, tn), lambda j, i: (i, j))],
                 pl.BlockSpec((k, tn), lambda j, i: (0, j)), _sds((k, n), MX), scratch=[pltpu.VMEM((k, tn), F32)],
                 n_axes=2)(a, b)


def mix_out_bwd(dx1, w_out, of, ob, p, gg, tm):
    T = dx1.shape[0]

    def body(dx_ref, w_ref, of_ref, ob_ref, pg_ref, gg_ref, da_ref, do_ref, dpg_ref, dgg_ref):
        @pl.when(pl.program_id(0) == 0)
        def _():
            dgg_ref[...] = jnp.zeros_like(dgg_ref)

        dxb = dx_ref[...].astype(MX)
        da_ref[...] = _dg(dxb, w_ref[0:512, :], 1, 1)
        for h in range(4):
            sl = slice(h * 128, (h + 1) * 128)
            dm = _dg(dxb, w_ref[512 + h * 128:512 + (h + 1) * 128, :], 1, 1)
            _, vjp = jax.vjp(outb_head, of_ref[:, sl] + ob_ref[:, sl], pg_ref[:, sl], gg_ref[h])
            do, dpg, dg = vjp(dm)
            do_ref[:, sl] = do
            dpg_ref[:, sl] = dpg
            dgg_ref[h] += dg

    return _call(body, "mix_out_bwd", (T // tm,),
                 [_rows(tm, D), _full(w_out), _rows(tm, 512), _rows(tm, 512), _rows(tm, 512, 4), _full(gg)],
                 [_rows(tm, 512), _rows(tm, 512), _rows(tm, 512), _full(gg)],
                 [_sds((T, 512)), _sds((T, 512)), _sds((T, 512)), _sds(gg.shape)])(dx1, w_out, of, ob, p, gg)


def gla_bwd(p, wg, bg, ss, do, reverse):
    T = p.shape[0]
    tm = CPB * CH
    n = T // tm
    rev = not reverse
    rn = n if rev else None

    def body(q_ref, k_ref, v_ref, r_ref, wg_ref, bg_ref, ss_ref, do_ref,
             dq_ref, dk_ref, dv_ref, dr_ref, dwg_ref, dbg_ref, dst_ref):
        @pl.when(pl.program_id(0) == 0)
        def _():
            dst_ref[...] = jnp.zeros_like(dst_ref)
            dwg_ref[...] = jnp.zeros_like(dwg_ref)
            dbg_ref[...] = jnp.zeros_like(dbg_ref)

        consts = _tri(reverse) + _lane_masks()
        for j in range(2):
            sl = slice(j * 128, (j + 1) * 128)
            v0s, v1s = slice(256 * j, 256 * j + 128), slice(256 * j + 128, 256 * j + 256)
            d0, d1 = dst_ref[2 * j], dst_ref[2 * j + 1]
            dwg, dbg = jnp.zeros((128, 128), F32), jnp.zeros((1, 128), F32)
            for c in (reversed(range(CPB)) if rev else range(CPB)):
                rows = _chunk(c)
                _, vjp = jax.vjp(functools.partial(gla_pair, consts), r_ref[rows, :], wg_ref[:, sl], bg_ref[:, sl],
                                 q_ref[rows, sl], k_ref[rows, sl], v_ref[rows, v0s], v_ref[rows, v1s],
                                 ss_ref[c, 2 * j], ss_ref[c, 2 * j + 1])
                g = vjp((do_ref[rows, v0s], do_ref[rows, v1s], d0, d1))
                if j == 0:
                    dr_ref[rows, :] = g[0]
                else:
                    dr_ref[rows, :] += g[0]
                dwg, dbg = dwg + g[1], dbg + g[2]
                dq_ref[rows, sl] = g[3]
                dk_ref[rows, sl] = g[4]
                dv_ref[rows, v0s] = g[5]
                dv_ref[rows, v1s] = g[6]
                d0, d1 = g[7], g[8]
            dst_ref[2 * j] = d0
            dst_ref[2 * j + 1] = d1
            dwg_ref[:, sl] += dwg
            dbg_ref[:, sl] += dbg

    ss_spec = pl.BlockSpec((CPB, 4, 128, 128), (lambda i: (n - 1 - i, 0, 0, 0)) if rev else (lambda i: (i, 0, 0, 0)))
    return _call(body, "gla_bwd_r" if reverse else "gla_bwd_f", (n,),
                 _gla_in_specs(tm, n, rev) + [_full(wg), _full(bg), ss_spec, _rows(tm, 512, 0, rn)],
                 [_rows(tm, 256, 0, rn), _rows(tm, 256, 0, rn), _rows(tm, 512, 0, rn), _rows(tm, 128, 0, rn),
                  _full(wg), _full(bg)],
                 [_sds((T, 256)), _sds((T, 256)), _sds((T, 512)), _sds((T, 128)), _sds(wg.shape), _sds(bg.shape)],
                 scratch=[pltpu.VMEM((4, 128, 128), F32)])(p, p, p, p, wg, bg, ss, do)


def gmlp_bwd(p, douta, ws, bs, lg, lb):
    T = p.shape[0]
    cpb = 1
    tm = cpb * CH

    def body(pa_ref, do_ref, ws_ref, bs_ref, lg_ref, lb_ref, dpa_ref, dws_ref, dbs_ref, dlg_ref, dlb_ref):
        @pl.when(pl.program_id(0) == 0)
        def _():
            for r in (dws_ref, dbs_ref, dlg_ref, dlb_ref):
                r[...] = jnp.zeros_like(r)

        for c in range(cpb):
            rows = _chunk(c)
            for h in range(4):
                us, vs = slice(h * 128, (h + 1) * 128), slice(512 + h * 128, 512 + (h + 1) * 128)
                _, vjp = jax.vjp(gmlp_head, pa_ref[rows, us], pa_ref[rows, vs], ws_ref[h], bs_ref[h], lg_ref[h], lb_ref[h])
                du, dv, *dparams = vjp(do_ref[rows, us])
                dpa_ref[rows, us] = du
                dpa_ref[rows, vs] = dv
                for r, a in zip((dws_ref, dbs_ref, dlg_ref, dlb_ref), dparams):
                    r[h] += a

    return _call(body, "gmlp_bwd", (T // tm,),
                 [_rows(tm, 1024), _rows(tm, 512), _full(ws), _full(bs), _full(lg), _full(lb)],
                 [_rows(tm, 1024), _full(ws), _full(bs), _full(lg), _full(lb)],
                 [_sds((T, 1024)), _sds(ws.shape), _sds(bs.shape), _sds(lg.shape), _sds(lb.shape)])(p, douta, ws, bs, lg, lb)


def assemble_dp(dpa, dqf, dqb, dkf, dkb, dvf, dvb, dpg, drf, drb, tm):
    T = dpa.shape[0]

    def body(a_ref, qf, qb, kf, kb, vf, vb, g_ref, rf, rb, o_ref):
        o_ref[:, 0:1024] = a_ref[...].astype(MX)
        o_ref[:, 1024:1280] = (qf[...] + qb[...]).astype(MX)
        o_ref[:, 1280:1536] = (kf[...] + kb[...]).astype(MX)
        o_ref[:, 1536:2048] = (vf[...] + vb[...]).astype(MX)
        o_ref[:, 2048:2560] = g_ref[...].astype(MX)
        o_ref[:, 2560:2688] = (rf[...] + rb[...]).astype(MX)

    ins = [dpa, dqf, dqb, dkf, dkb, dvf, dvb, dpg, drf, drb]
    return _call(body, "assemble_dp", (T // tm,), [_rows(tm, a.shape[1]) for a in ins],
                 _rows(tm, N_INP), _sds((T, N_INP), MX))(*ins)


def _gate_pad(w, row0):
    return jnp.zeros((128, 256), F32).at[row0:row0 + 16].set(w)


def local_step(x, tgt, W, get_big, emit, tm=256, tmm=512):
    saved = []
    for l in range(NL):
        s = {"x": x}
        s.update(get_big(l, "in", x))
        p, s["h"] = norm_matmul(x, W["g_mix"][l][None], s["w_in"], tmm, "mix_in")
        s["p"] = p
        ws, bs = W["w_s"][l], W["b_s"][l][:, :, None]
        lg, lb = W["ln_g"][l][:, None, :], W["ln_b"][l][:, None, :]
        outa = gmlp_fwd(p, ws, bs, lg, lb)
        wgf, wgb = _gate_pad(W["w_gate_f"][l], 0), _gate_pad(W["w_gate_b"][l], 16)
        bgf, bgb = W["b_gate_f"][l][None], W["b_gate_b"][l][None]
        s["of"], s["ssf"] = gla_fwd(p, wgf, bgf, False)
        s["ob"], s["ssb"] = gla_fwd(p, wgb, bgb, True)
        s.update(get_big(l, "rest", s["ob"]))
        gg = W["g_gla"][l][:, None, :]
        x1, s["mixed"] = mix_out(x, s["of"], s["ob"], p, outa, gg, s["w_out"], tmm)
        s["x1"] = x1
        s["zu"], s["h2"] = norm_matmul(x1, W["g_ffn"][l][None], s["w_up"], tmm, "ffn_up", MX)
        x, s["z"], s["a"] = ffn_down(x1, s["zu"], W["conv_w"][l], W["conv_b"][l][None], s["w_down"], tm)
        saved.append(s)

    lsum, dx, dgf = loss_head(x, W["g_final"][None], tgt, tmm)
    G = {k: [None] * NL for k in _SMALL if k != "g_final"}
    tok = jnp.zeros((1, 1), F32)
    for l in reversed(range(NL)):
        s = saved[l]
        g_down = matmul_tn(s["a"], dx, 512, 512, "dw_down")
        dz = ffn_down_bwd(dx, s["z"], s["w_down"], tm)
        dzu, G["conv_w"][l], dcb = ffn_conv_bwd(dz, s["zu"], W["conv_w"][l] + tok, tm)
        G["conv_b"][l] = dcb[0]
        g_up = matmul_tn(s["h2"], dzu, 512, 1408, "dw_up")
        tok = emit(l, "A", {"w_down": g_down, "w_up": g_up})
        dx1, dg = nt_normbwd(dzu, s["w_up"], s["x1"], W["g_ffn"][l][None] + tok, dx, tmm, "ffn_up_bwd")
        G["g_ffn"][l] = dg[0]
        g_out = matmul_tn(s["mixed"], dx1, 512, 512, "dw_out")
        gg = W["g_gla"][l][:, None, :]
        douta, do, dpg, dgg = mix_out_bwd(dx1, s["w_out"], s["of"], s["ob"], s["p"], gg, tmm)
        G["g_gla"][l] = dgg[:, 0, :]
        wgf, wgb = _gate_pad(W["w_gate_f"][l], 0), _gate_pad(W["w_gate_b"][l], 16)
        bgf, bgb = W["b_gate_f"][l][None], W["b_gate_b"][l][None]
        dqf, dkf, dvf, drf, dwgf, dbgf = gla_bwd(s["p"], wgf, bgf, s["ssf"], do, False)
        dqb, dkb, dvb, drb, dwgb, dbgb = gla_bwd(s["p"], wgb, bgb, s["ssb"], do, True)
        G["w_gate_f"][l], G["b_gate_f"][l] = dwgf[0:16], dbgf[0]
        G["w_gate_b"][l], G["b_gate_b"][l] = dwgb[16:32], dbgb[0]
        ws, bs = W["w_s"][l], W["b_s"][l][:, :, None]
        lg, lb = W["ln_g"][l][:, None, :], W["ln_b"][l][:, None, :]
        dpa, G["w_s"][l], dbs, dlg, dlb = gmlp_bwd(s["p"], douta, ws, bs, lg, lb)
        G["b_s"][l], G["ln_g"][l], G["ln_b"][l] = dbs[:, :, 0], dlg[:, 0, :], dlb[:, 0, :]
        dpc = assemble_dp(dpa, dqf, dqb, dkf, dkb, dvf, dvb, dpg, drf, drb, tmm)
        g_in = matmul_tn(s["h"], dpc, 512, 896, "dw_in")
        tok = emit(l, "B", {"w_out": g_out, "w_in": g_in})
        dx, dg = nt_normbwd(dpc, s["w_in"], s["x"], W["g_mix"][l][None] + tok, dx1, tmm, "mix_in_bwd")
        G["g_mix"][l] = dg[0]
    G = {k: jnp.stack(v) for k, v in G.items()}
    G["g_final"] = dgf[0]
    return lsum, dx, G


def cast_bf16(a, tr):
    r, c = a.shape

    def body(a_ref, o_ref):
        o_ref[...] = a_ref[...].astype(BF16)

    return _call(body, "cast_bf16", (r // tr,), [_rows(tr, c)], _rows(tr, c), _sds((r, c), BF16))(a)


def sum_lead(y, tr):
    n, rr, cc = y.shape

    def body(y_ref, o_ref):
        acc = y_ref[0].astype(F32)
        for k in range(1, n):
            acc = acc + y_ref[k].astype(F32)
        o_ref[...] = acc

    return _call(body, "sum_lead", (rr // tr,), [pl.BlockSpec((n, tr, cc), lambda i: (0, i, 0))],
                 _rows(tr, cc), _sds((rr, cc)))(y)


def adamw(w, ga, gb, m, v, tr):
    r, c = w.shape

    def body(w_ref, ga_ref, gb_ref, m_ref, v_ref, g_ref, d_ref, nm_ref, nv_ref):
        gr = ga_ref[...] + gb_ref[...]
        g_ref[...] = gr
        nm = ADAM_B1 * m_ref[...] + (1.0 - ADAM_B1) * gr
        nv = ADAM_B2 * v_ref[...] + (1.0 - ADAM_B2) * jnp.square(gr)
        m_hat = nm / (1.0 - ADAM_B1 ** ADAM_STEP)
        v_hat = nv / (1.0 - ADAM_B2 ** ADAM_STEP)
        d_ref[...] = -ADAM_LR * (m_hat / (jnp.sqrt(v_hat) + ADAM_EPS) + ADAM_WD * w_ref[...])
        nm_ref[...] = nm
        nv_ref[...] = nv

    sp = _rows(tr, c)
    return _call(body, "adamw", (r // tr,), [sp] * 5, [sp] * 4, [_sds((r, c))] * 4)(w, ga, gb, m, v)


MESH = pl.DeviceIdType.MESH
ANY = pl.BlockSpec(memory_space=pl.ANY)
N_BIG = 4


def _pos():
    return lax.axis_index("x"), lax.axis_index("y"), lax.axis_index("c")


def _other_chips(x, y):
    return [(1 - x, y), (x, 1 - y), (1 - x, 1 - y)]


def _rcopy(src, dst, send_sems, recv_sems, k, to):
    return pltpu.make_async_remote_copy(src_ref=src, dst_ref=dst, send_sem=send_sems.at[k], recv_sem=recv_sems.at[k],
                                        device_id=to, device_id_type=MESH)


def allgather8(xs):
    m, n = xs.shape

    def body(x_ref, out_ref, send_sems, recv_sems, local_sem):
        x, y, c = _pos()
        me, sibling = (x, y, c), (x, y, 1 - c)
        chips = _other_chips(x, y)

        def rows(px, py, pc):
            return out_ref.at[pl.ds((4 * px + 2 * py + pc) * m, m), :]

        def copy(k, block, to, src=None):
            return _rcopy(rows(*block) if src is None else src, rows(*block), send_sems, recv_sems, k, to)

        mine = pltpu.make_async_copy(x_ref, rows(*me), local_sem)
        mine.start()
        first = [copy(0, me, sibling, src=x_ref)]
        first += [copy(1 + j, me, (*chip, c), src=x_ref) for j, chip in enumerate(chips)]
        for cp in first:
            cp.start()
        passed = [copy(4 + j, (*chip, c), sibling) for j, chip in enumerate(chips)]
        for j, chip in enumerate(chips):
            copy(1 + j, (*chip, c), me).wait_recv()
            passed[j].start()
        copy(0, sibling, me).wait_recv()
        for j, chip in enumerate(chips):
            copy(4 + j, (*chip, 1 - c), me).wait_recv()
        for cp in first + passed:
            cp.wait_send()
        mine.wait()

    vm = pl.BlockSpec(memory_space=pltpu.VMEM)
    return pl.pallas_call(
        body, name="allgather8", out_shape=_sds((8 * m, n), xs.dtype), in_specs=[vm], out_specs=vm,
        scratch_shapes=[pltpu.SemaphoreType.DMA((7,)), pltpu.SemaphoreType.DMA((7,)), pltpu.SemaphoreType.DMA],
        compiler_params=pltpu.CompilerParams(vmem_limit_bytes=VMEM_LIMIT))(xs)


def _slab(k, ref, j):
    if k == 0:
        return ref.at[j]
    if k == 1:
        return ref.at[pl.ds(256 * j, 256), :]
    if k == 2:
        return ref.at[:, pl.ds(1408 * j, 1408)]
    return ref.at[pl.ds(704 * j, 704), :]


_LAYER_FULL = [(4, 1024, 648), (1024, 1024), (1024, NUP), (DFF, 1024)]
_LAYER_SHARD = [(1024, 648), (256, 1024), (1024, 1408), (704, 1024)]
_SHARD_SHAPES = [(NL,) + s for s in _LAYER_SHARD]

HBM = pl.BlockSpec(memory_space=pltpu.HBM)
SEM = pl.BlockSpec(memory_space=pltpu.SEMAPHORE)
VM = pl.BlockSpec(memory_space=pltpu.VMEM)
EFFECT = pltpu.SideEffectType.DATAFLOW_SIDE_EFFECTING
_GW_GROUPS = [[(0, 0)], [(0, 1), (0, 2), (0, 3)]] + [[(l, k) for k in range(N_BIG)] for l in range(1, NL)]
_GW_ORDER = [lk for g in _GW_GROUPS for lk in g]


def _hbm(a):
    return pltpu.with_memory_space_constraint(a, pltpu.HBM)


def _hbm_like(a):
    return pltpu.HBM(a.shape, a.dtype)


def gw_start(shards, landings):
    n = len(_GW_ORDER)

    def body(*refs):
        S, Ld = refs[:N_BIG], refs[N_BIG:N_BIG + n]
        outs = refs[N_BIG + n:]
        send_sems, recv, token = outs[0], outs[1:1 + len(_GW_GROUPS)], outs[-1]
        x, y, c = _pos()
        me = 2 * x + y
        ci = 0
        for gi, grp in enumerate(_GW_GROUPS):
            for t, (l, k) in enumerate(grp):
                land = Ld[_GW_ORDER.index((l, k))]
                for j, (px, py) in enumerate(_other_chips(x, y)):
                    pltpu.make_async_remote_copy(
                        src_ref=S[k].at[l], dst_ref=_slab(k, land, me), send_sem=send_sems.at[ci],
                        recv_sem=recv[gi].at[3 * t + j], device_id=(px, py, c), device_id_type=MESH).start()
                    ci += 1
        token[...] = jnp.zeros_like(token)

    ins = list(shards) + list(landings)
    sems = [pltpu.SemaphoreType.DMA((3 * n,))] + [pltpu.SemaphoreType.DMA((3 * len(g),)) for g in _GW_GROUPS]
    outs = pl.pallas_call(
        body, name="gw_start", out_shape=sems + [_hbm_like(a) for a in ins] + [_sds((8, 128))],
        in_specs=[HBM] * len(ins), out_specs=[SEM] * len(sems) + [HBM] * len(ins) + [VM],
        input_output_aliases={i: len(sems) + i for i in range(len(ins))},
        compiler_params=pltpu.CompilerParams(has_side_effects=EFFECT))(*[_hbm(a) for a in ins])
    ns = len(sems)
    return outs[0], outs[1:ns], outs[ns:ns + N_BIG], outs[ns + N_BIG:ns + len(ins)], outs[-1]


def gw_wait(gi, landings, recv_sems, after, shards=None, send_sems=None):
    grp = _GW_GROUPS[gi]
    n = len(grp)
    last = shards is not None

    def body(*refs):
        Ld, rs = refs[:n], refs[n]
        x, y, c = _pos()
        for t, (l, k) in enumerate(grp):
            for j, (px, py) in enumerate(_other_chips(x, y)):
                region = _slab(k, Ld[t], 2 * px + py)
                pltpu.make_async_remote_copy(src_ref=region, dst_ref=region, send_sem=rs.at[3 * t + j],
                                             recv_sem=rs.at[3 * t + j], device_id=(px, py, c),
                                             device_id_type=MESH).wait_recv()
        if last:
            S, ss = refs[n + 2:n + 2 + N_BIG], refs[n + 2 + N_BIG]
            me = 2 * x + y
            for ci, (l, k) in enumerate(lk for lk in _GW_ORDER for _ in range(3)):
                pltpu.make_async_remote_copy(src_ref=S[k].at[l], dst_ref=_slab(k, Ld[k], me), send_sem=ss.at[ci],
                                             recv_sem=ss.at[ci], device_id=(x, y, c), device_id_type=MESH).wait_send()

    ins = list(landings) + [recv_sems, after]
    specs = [HBM] * n + [SEM, pl.BlockSpec(memory_space=pl.ANY)]
    outs = [_hbm_like(a) for a in landings]
    alias = {i: i for i in range(n)}
    if last:
        ins += list(shards) + [send_sems]
        specs += [HBM] * N_BIG + [SEM]
        outs += [_hbm_like(a) for a in shards]
        alias.update({n + 2 + i: n + i for i in range(N_BIG)})
    res = pl.pallas_call(body, name="gw_wait_%d" % gi, out_shape=outs, in_specs=specs, out_specs=[HBM] * len(outs),
                         input_output_aliases=alias,
                         compiler_params=pltpu.CompilerParams(has_side_effects=EFFECT))(*ins)
    return res[:n]


def ga_start(tag, ks, grads, landings):
    n = len(ks)

    def body(*refs):
        G, Ld = refs[:n], refs[n:2 * n]
        send_sems, recv_sems, token = refs[2 * n], refs[2 * n + 1], refs[-1]
        x, y, c = _pos()
        me = 2 * x + y
        for t, k in enumerate(ks):
            for j, (px, py) in enumerate(_other_chips(x, y)):
                pltpu.make_async_remote_copy(
                    src_ref=_slab(k, G[t], 2 * px + py), dst_ref=Ld[t].at[me], send_sem=send_sems.at[3 * t + j],
                    recv_sem=recv_sems.at[3 * t + j], device_id=(px, py, c), device_id_type=MESH).start()
        token[...] = jnp.zeros_like(token)

    ins = list(grads) + list(landings)
    sems = [pltpu.SemaphoreType.DMA((3 * n,))] * 2
    outs = pl.pallas_call(
        body, name="ga_start_" + tag, out_shape=sems + [_hbm_like(a) for a in ins] + [_sds((8, 128))],
        in_specs=[HBM] * len(ins), out_specs=[SEM, SEM] + [HBM] * len(ins) + [VM],
        input_output_aliases={i: 2 + i for i in range(len(ins))},
        compiler_params=pltpu.CompilerParams(has_side_effects=EFFECT))(*[_hbm(a) for a in ins])
    return outs[0], outs[1], outs[2:2 + n], outs[2 + n:2 + 2 * n], outs[-1]


def ga_wait(tag, ks, send_sems, recv_sems, grads, landings, after):
    n = len(ks)

    def body(*refs):
        G, Ld, ss, rs = refs[:n], refs[n:2 * n], refs[2 * n], refs[2 * n + 1]
        x, y, c = _pos()
        me = 2 * x + y
        for t, k in enumerate(ks):
            for j, (px, py) in enumerate(_other_chips(x, y)):
                pj = 2 * px + py
                cp = pltpu.make_async_remote_copy(
                    src_ref=_slab(k, G[t], pj), dst_ref=Ld[t].at[pj], send_sem=ss.at[3 * t + j],
                    recv_sem=rs.at[3 * t + j], device_id=(px, py, c), device_id_type=MESH)
                cp.wait_send()
                cp.wait_recv()

    ins = list(grads) + list(landings) + [send_sems, recv_sems, after]
    res = pl.pallas_call(
        body, name="ga_wait_" + tag, out_shape=[_hbm_like(a) for a in list(grads) + list(landings)],
        in_specs=[HBM] * (2 * n) + [SEM, SEM, pl.BlockSpec(memory_space=pl.ANY)], out_specs=[HBM] * (2 * n),
        input_output_aliases={i: i for i in range(2 * n)},
        compiler_params=pltpu.CompilerParams(has_side_effects=EFFECT))(*ins)
    return res[n:]


def swap4(parts):
    def body(*refs):
        Q, R = refs[:N_BIG], refs[N_BIG:2 * N_BIG]
        send_sems, recv_sems = refs[2 * N_BIG:]
        x, y, c = _pos()
        cps = [_rcopy(Q[k], R[k], send_sems, recv_sems, k, (x, y, 1 - c)) for k in range(N_BIG)]
        for cp in cps:
            cp.start()
        for cp in cps:
            cp.wait()

    return pl.pallas_call(
        body, name="swap4", out_shape=[_sds(s) for s in _SHARD_SHAPES],
        in_specs=[ANY] * N_BIG, out_specs=[ANY] * N_BIG,
        scratch_shapes=[pltpu.SemaphoreType.DMA((N_BIG,)), pltpu.SemaphoreType.DMA((N_BIG,))])(*parts)


_WEIGHTS = ['g_mix', 'w_in', 'w_s', 'b_s', 'ln_g', 'ln_b', 'w_gate_f', 'b_gate_f', 'w_gate_b', 'b_gate_b', 'g_gla',
            'w_out', 'g_ffn', 'w_up', 'conv_w', 'conv_b', 'w_down', 'g_final']
_BIG = ['w_in', 'w_out', 'w_up', 'w_down']
_SMALL = [n for n in _WEIGHTS if n not in _BIG]
_SMALL_SHARDED = {'w_gate_f': 64, 'w_gate_b': 64, 'conv_w': 1408}
_BIG_TR = {'w_in': 512, 'w_out': 256, 'w_up': 256, 'w_down': 352}


def _pack(arrs):
    flat = jnp.concatenate([a.reshape(-1) for a in arrs])
    pad = (-flat.shape[0]) % 1024
    return jnp.pad(flat, (0, pad)).reshape(-1, 128)


def _unpack(buf, shapes):
    flat = buf.reshape(-1)
    out, o = [], 0
    for s in shapes:
        n = 1
        for d in s:
            n *= d
        out.append(flat[o:o + n].reshape(s))
        o += n
    return out


def _take(k, full, chip):
    if k == 0:
        return lax.dynamic_index_in_dim(full, chip, 0, keepdims=False)
    if k == 2:
        return lax.dynamic_slice_in_dim(full, chip * 1408, 1408, axis=1)
    return lax.dynamic_slice_in_dim(full, chip * _LAYER_SHARD[k][0], _LAYER_SHARD[k][0], axis=0)


def _place(k, full, part, chip):
    if k == 0:
        return lax.dynamic_update_index_in_dim(full, part, chip, 0)
    if k == 2:
        return lax.dynamic_update_slice_in_dim(full, part, chip * 1408, axis=1)
    return lax.dynamic_update_slice_in_dim(full, part, chip * _LAYER_SHARD[k][0], axis=0)


def _place_lead(buf, part, chip):
    return lax.dynamic_update_index_in_dim(buf, part, chip, 0)


def kernel(x, g_mix, w_in, w_s, b_s, ln_g, ln_b, w_gate_f, b_gate_f, w_gate_b, b_gate_b, g_gla, w_out, g_ffn, w_up, conv_w, conv_b, w_down, g_final, loss_target, m_g_mix, m_w_in, m_w_s, m_b_s, m_ln_g, m_ln_b, m_w_gate_f, m_b_gate_f, m_w_gate_b, m_b_gate_b, m_g_gla, m_w_out, m_g_ffn, m_w_up, m_conv_w, m_conv_b, m_w_down, m_g_final, v_g_mix, v_w_in, v_w_s, v_b_s, v_ln_g, v_ln_b, v_w_gate_f, v_b_gate_f, v_w_gate_b, v_b_gate_b, v_g_gla, v_w_out, v_g_ffn, v_w_up, v_conv_w, v_conv_b, v_w_down, v_g_final):
    loc = locals()
    w = {n: loc[n] for n in _WEIGHTS}
    m = {n: loc["m_" + n] for n in _WEIGHTS}
    v = {n: loc["v_" + n] for n in _WEIGHTS}
    xi, yi, _ = _pos()
    chip = 2 * xi + yi

    sh_names = list(_SMALL_SHARDED)
    g8 = allgather8(_pack([w[n] for n in sh_names]))
    rows = g8.shape[0] // 8
    per_chip = [_unpack(g8[2 * j * rows:(2 * j + 1) * rows], [w[n].shape for n in sh_names]) for j in range(4)]
    W = dict(w)
    for k, n in enumerate(sh_names):
        W[n] = jnp.concatenate([per_chip[j][k] for j in range(4)], axis=-1)

    shards = [cast_bf16(w[n].reshape(-1, w[n].shape[-1]), _BIG_TR[n]).reshape(w[n].shape) for n in _BIG]
    landings = [_place(k, lax.empty(_LAYER_FULL[k], BF16), shards[k][l], chip) for l, k in _GW_ORDER]
    send_sems, recv_sems, shards_fly, landings_fly, _ = gw_start(shards, landings)
    arrived = {}

    def get_big(l, stage, after):
        gi = {(0, "in"): 0, (0, "rest"): 1}.get((l, stage), l + 1 if stage == "in" else None)
        if gi is not None:
            lo = sum(len(g) for g in _GW_GROUPS[:gi])
            lands = landings_fly[lo:lo + len(_GW_GROUPS[gi])]
            if gi == len(_GW_GROUPS) - 1:
                full = gw_wait(gi, lands, recv_sems[gi], after, shards_fly, send_sems)
            else:
                full = gw_wait(gi, lands, recv_sems[gi], after)
            arrived.update(zip(_GW_GROUPS[gi], full))
        if stage == "in":
            f_in = jnp.transpose(arrived[(l, 0)], (1, 0, 2)).reshape(D, N_IN)
            return {"w_in": jnp.pad(f_in, ((0, 0), (0, N_INP - N_IN)))}
        return {"w_out": arrived[(l, 1)], "w_up": arrived[(l, 2)], "w_down": arrived[(l, 3)]}

    flying = []

    def emit(l, group, grads):
        ks = [3, 2] if group == "A" else [1, 0]
        gs = [grads[_BIG[k]] for k in ks]
        if group == "B":
            gs[1] = jnp.transpose(gs[1][:, :N_IN].reshape(D, 4, 648), (1, 0, 2))
        lands = [_place_lead(lax.empty((4,) + _LAYER_SHARD[k], BF16), _take(k, g, chip), chip) for k, g in zip(ks, gs)]
        tag = "%d%s" % (l, group)
        ss, rs, gs_fly, lands_fly, tok = ga_start(tag, ks, gs, lands)
        flying.append((tag, l, ks, ss, rs, gs_fly, lands_fly))
        return tok[0:1, 0:1]

    lsum, grad_x, G = local_step(x[0], loss_target[0], W, get_big, emit)

    plane = [[None] * NL for _ in range(N_BIG)]
    for tag, l, ks, ss, rs, gs_fly, lands_fly in flying:
        for k, a in zip(ks, ga_wait(tag, ks, ss, rs, gs_fly, lands_fly, grad_x)):
            plane[k][l] = sum_lead(a.reshape(4, -1, a.shape[-1]), _BIG_TR[_BIG[k]]).reshape(_LAYER_SHARD[k])
    plane = [jnp.stack(p) for p in plane]
    other = swap4(plane)

    small_shapes = [G[n].shape for n in _SMALL] + [(D,)]
    pk = _pack([G[n] for n in _SMALL] + [lsum])
    srows = pk.shape[0]
    red = sum_lead(allgather8(pk).reshape(8, srows, 128), srows)
    small = dict(zip(_SMALL + ["lsum"], _unpack(red, small_shapes)))
    loss = 0.5 * jnp.sum(small.pop("lsum")) / D
    for n, wd in _SMALL_SHARDED.items():
        small[n] = lax.dynamic_slice_in_dim(small[n], chip * wd, wd, axis=small[n].ndim - 1)

    grads, delta, new_m, new_v = dict(small), {}, {}, {}
    two = lambda a: a.reshape(-1, a.shape[-1])
    for k, n in enumerate(_BIG):
        res = adamw(two(w[n]), two(plane[k]), two(other[k]), two(m[n]), two(v[n]), _BIG_TR[n])
        grads[n], delta[n], new_m[n], new_v[n] = (r.reshape(w[n].shape) for r in res)
    shapes = [w[n].shape for n in _SMALL]
    pw, pg, pm, pv = (_pack([t[n] for n in _SMALL]) for t in (w, grads, m, v))
    _, d_, m_, v_ = adamw(pw, pg, jnp.zeros_like(pg), pm, pv, pw.shape[0])
    for t, buf in ((delta, d_), (new_m, m_), (new_v, v_)):
        t.update(zip(_SMALL, _unpack(buf, shapes)))

    return (loss, grad_x[None], *[grads[n] for n in _WEIGHTS], *[delta[n] for n in _WEIGHTS],
            *[new_m[n] for n in _WEIGHTS], *[new_v[n] for n in _WEIGHTS])
```

```python
import functools

import jax
import jax.numpy as jnp
from jax import lax
from jax.experimental import pallas as pl
from jax.experimental.pallas import tpu as pltpu

F32 = jnp.float32
BF16 = jnp.bfloat16
MX = BF16

D = 1024
CH = 128
NL = 4
N_IN = 2592
N_INP = 2688
NUP = 5632
DFF = 2816
EPS = 1e-6
VMEM_LIMIT = 56 * 1024 * 1024

ADAM_LR, ADAM_B1, ADAM_B2, ADAM_EPS, ADAM_WD, ADAM_STEP = 0.001, 0.9, 0.999, 1e-08, 0.01, 10


def _dg(a, b, ca, cb):
    return lax.dot_general(a.astype(MX), b.astype(MX), (((ca,), (cb,)), ((), ())), preferred_element_type=F32)


@jax.custom_vjp
def mm(a, b):
    return _dg(a, b, 1, 0)


mm.defvjp(lambda a, b: (_dg(a, b, 1, 0), (a, b)),
          lambda r, g: (_dg(g, r[1], 1, 1), _dg(r[0], g, 0, 0)))


@jax.custom_vjp
def mm_nt(a, b):
    return _dg(a, b, 1, 1)


mm_nt.defvjp(lambda a, b: (_dg(a, b, 1, 1), (a, b)),
             lambda r, g: (_dg(g, r[1], 1, 0), _dg(g, r[0], 0, 0)))


@jax.custom_vjp
def mm_tn(a, b):
    return _dg(a, b, 0, 0)


mm_tn.defvjp(lambda a, b: (_dg(a, b, 0, 0), (a, b)),
             lambda r, g: (_dg(r[1], g, 1, 1), _dg(r[0], g, 1, 0)))


def _split3(x):
    hi = x.astype(BF16)
    r1 = x - hi.astype(F32)
    mid = r1.astype(BF16)
    lo = (r1 - mid.astype(F32)).astype(BF16)
    return hi, mid, lo


def _dot3(m, x):
    hi, mid, lo = _split3(x)
    d = lambda p: lax.dot_general(m, p, (((1,), (0,)), ((), ())), preferred_element_type=F32)
    return d(hi) + d(mid) + d(lo)


@jax.custom_vjp
def cumdot(m, mt, x):
    return _dot3(m, x)


cumdot.defvjp(lambda m, mt, x: (_dot3(m, x), (m, mt)),
              lambda r, g: (jnp.zeros_like(r[0]), jnp.zeros_like(r[1]), _dot3(r[1], g)))


def rmsnorm(x, g):
    return x * lax.rsqrt(jnp.mean(x * x, axis=-1, keepdims=True) + EPS) * g


def gelu(x):
    return 0.5 * x * (1.0 + lax.erf(x * 0.7071067811865476))


def sigmoid(x):
    return 1.0 / (1.0 + jnp.exp(-x))


def log_sigmoid(x):
    return jnp.minimum(x, 0.0) - jnp.log(1.0 + jnp.exp(-jnp.abs(x)))


def gmlp_head(u_pre, v_pre, w, bcol, g, b):
    u = gelu(u_pre)
    v = gelu(v_pre)
    mu = jnp.mean(v, axis=-1, keepdims=True)
    var = jnp.mean(jnp.square(v - mu), axis=-1, keepdims=True)
    vn = (v - mu) * lax.rsqrt(var + EPS) * g + b
    return u * (mm(w, vn) + bcol)


def outb_head(o, pg, g):
    return rmsnorm(o, g) * (pg * sigmoid(pg))


def ffn_act(zg, zv):
    return zg * sigmoid(zg) * zv


def _tri(reverse):
    r = lax.broadcasted_iota(jnp.int32, (CH, CH), 0)
    c = lax.broadcasted_iota(jnp.int32, (CH, CH), 1)
    if reverse:
        cm, sm = c >= r, c > r
    else:
        cm, sm = c <= r, c <= r
    one = jnp.ones((), BF16)
    zero = jnp.zeros((), BF16)
    return jnp.where(cm, one, zero), jnp.where(cm.T, one, zero), sm


def gla_pair(consts, pr, wg, bg, qp, kp, v0, v1, st0, st1):
    m, mt, smask, lm0, lm1 = consts
    la = log_sigmoid(mm(pr, wg) + bg) * (1.0 / 16.0)
    cum = cumdot(m, mt, la)
    tot = jnp.sum(la, axis=0, keepdims=True)
    q_dec = (qp * 0.125) * jnp.exp(cum)
    k_inv = kp * jnp.exp(-cum)
    k_end = kp * jnp.exp(tot - cum)
    dec = jnp.exp(tot)
    outs = []
    for lm, v, st in ((lm0, v0, st0), (lm1, v1, st1)):
        s = jnp.where(smask, mm_nt(q_dec * lm, k_inv), 0.0)
        o = mm(s, v) + mm_nt(q_dec, st)
        st_new = st * dec + mm_tn(v, k_end * lm)
        outs += [o, st_new]
    return outs[0], outs[2], outs[1], outs[3]


def _lane_masks():
    lane = lax.broadcasted_iota(jnp.int32, (1, 128), 1)
    return (lane < 64).astype(F32), (lane >= 64).astype(F32)


def _cparams(n_axes=1):
    return pltpu.CompilerParams(dimension_semantics=("arbitrary",) * n_axes, vmem_limit_bytes=VMEM_LIMIT)


def _full(a):
    nd = a.ndim
    return pl.BlockSpec(a.shape, lambda *_: (0,) * nd)


def _rows(tm, w, cb=0, rev_n=None):
    if rev_n is None:
        return pl.BlockSpec((tm, w), lambda i: (i, cb))
    return pl.BlockSpec((tm, w), lambda i: (rev_n - 1 - i, cb))


def _call(body, name, grid, in_specs, out_specs, out_shape, scratch=(), n_axes=1):
    return pl.pallas_call(body, name=name, grid=grid, in_specs=in_specs, out_specs=out_specs, out_shape=out_shape,
                          scratch_shapes=list(scratch), compiler_params=_cparams(n_axes))


def _sds(shape, dt=F32):
    return jax.ShapeDtypeStruct(shape, dt)


def norm_matmul(x, g, w, tm, name, ydt=F32):
    T, n = x.shape[0], w.shape[1]

    def body(x_ref, g_ref, w_ref, y_ref, h_ref):
        hb = rmsnorm(x_ref[...], g_ref[...]).astype(MX)
        h_ref[...] = hb
        y_ref[...] = jnp.dot(hb, w_ref[...], preferred_element_type=F32).astype(ydt)

    return _call(body, name, (T // tm,), [_rows(tm, D), _full(g), _full(w)],
                 [_rows(tm, n), _rows(tm, D)], [_sds((T, n), ydt), _sds((T, D), MX)])(x, g, w)


CPB = 4


def _chunk(c):
    return slice(c * CH, (c + 1) * CH)


def gmlp_fwd(p, ws, bs, lg, lb):
    T = p.shape[0]
    tm = CPB * CH

    def body(pa_ref, ws_ref, bs_ref, lg_ref, lb_ref, o_ref):
        for c in range(CPB):
            for h in range(4):
                o_ref[_chunk(c), h * 128:(h + 1) * 128] = gmlp_head(
                    pa_ref[_chunk(c), h * 128:(h + 1) * 128], pa_ref[_chunk(c), 512 + h * 128:512 + (h + 1) * 128],
                    ws_ref[h], bs_ref[h], lg_ref[h], lb_ref[h]).astype(MX)

    return _call(body, "gmlp_fwd", (T // tm,), [_rows(tm, 1024), _full(ws), _full(bs), _full(lg), _full(lb)],
                 _rows(tm, 512), _sds((T, 512), MX))(p, ws, bs, lg, lb)


def _gla_in_specs(tm, n, rev):
    r = n if rev else None
    return [_rows(tm, 256, 4, r), _rows(tm, 256, 5, r), _rows(tm, 512, 3, r), _rows(tm, 128, 20, r)]


def gla_fwd(p, wg, bg, reverse):
    T = p.shape[0]
    tm = CPB * CH
    n = T // tm
    rev = n if reverse else None

    def body(q_ref, k_ref, v_ref, r_ref, wg_ref, bg_ref, o_ref, ss_ref, st_ref):
        @pl.when(pl.program_id(0) == 0)
        def _():
            st_ref[...] = jnp.zeros_like(st_ref)

        consts = _tri(reverse) + _lane_masks()
        for j in range(2):
            sl = slice(j * 128, (j + 1) * 128)
            st0, st1 = st_ref[2 * j], st_ref[2 * j + 1]
            for c in (reversed(range(CPB)) if reverse else range(CPB)):
                rows = _chunk(c)
                ss_ref[c, 2 * j] = st0
                ss_ref[c, 2 * j + 1] = st1
                o0, o1, st0, st1 = gla_pair(
                    consts, r_ref[rows, :], wg_ref[:, sl], bg_ref[:, sl], q_ref[rows, sl], k_ref[rows, sl],
                    v_ref[rows, 256 * j:256 * j + 128], v_ref[rows, 256 * j + 128:256 * j + 256], st0, st1)
                o_ref[rows, 256 * j:256 * j + 128] = o0
                o_ref[rows, 256 * j + 128:256 * j + 256] = o1
            st_ref[2 * j] = st0
            st_ref[2 * j + 1] = st1

    ss_spec = pl.BlockSpec((CPB, 4, 128, 128), (lambda i: (n - 1 - i, 0, 0, 0)) if reverse else (lambda i: (i, 0, 0, 0)))
    return _call(body, "gla_fwd_r" if reverse else "gla_fwd_f", (n,),
                 _gla_in_specs(tm, n, reverse) + [_full(wg), _full(bg)],
                 [_rows(tm, 512, 0, rev), ss_spec], [_sds((T, 512)), _sds((T // CH, 4, 128, 128))],
                 scratch=[pltpu.VMEM((4, 128, 128), F32)])(p, p, p, p, wg, bg)


def mix_out(x, of, ob, p, outa, gg, w_out, tm):
    T = x.shape[0]

    def body(x_ref, of_ref, ob_ref, pg_ref, oa_ref, gg_ref, w_ref, x1_ref, mx_ref):
        mx_ref[:, 0:512] = oa_ref[...]
        for h in range(4):
            sl = slice(h * 128, (h + 1) * 128)
            mx_ref[:, 512 + h * 128:512 + (h + 1) * 128] = outb_head(
                of_ref[:, sl] + ob_ref[:, sl], pg_ref[:, sl], gg_ref[h]).astype(MX)
        x1_ref[...] = x_ref[...] + jnp.dot(mx_ref[...], w_ref[...], preferred_element_type=F32)

    return _call(body, "mix_out", (T // tm,),
                 [_rows(tm, D), _rows(tm, 512), _rows(tm, 512), _rows(tm, 512, 4), _rows(tm, 512), _full(gg), _full(w_out)],
                 [_rows(tm, D), _rows(tm, 1024)], [_sds((T, D)), _sds((T, 1024), MX)])(x, of, ob, p, outa, gg, w_out)


HALO = 16


def _halo_specs(T, tm, w):
    nb = T // HALO
    r = tm // HALO
    return [pl.BlockSpec((tm, w), lambda i: (i, 0)),
            pl.BlockSpec((HALO, w), lambda i: (jnp.maximum(i * r - 1, 0), 0)),
            pl.BlockSpec((HALO, w), lambda i: (jnp.minimum((i + 1) * r, nb - 1), 0))]


def _shifted(main, prev, nxt, i, nsteps):
    tm = main.shape[0]
    row = lax.broadcasted_iota(jnp.int32, (tm, 1), 0)
    pr = jnp.where(i > 0, prev[HALO - 1:HALO, :].astype(F32), 0.0)
    nx = jnp.where(i < nsteps - 1, nxt[0:1, :].astype(F32), 0.0)
    dn = jnp.where(row == 0, pr, pltpu.roll(main, 1, 0))
    up = jnp.where(row == tm - 1, nx, pltpu.roll(main, tm - 1, 0))
    return dn, up


def ffn_down(x1, zu, cw, cb, w_down, tm):
    T = x1.shape[0]
    ns = T // tm

    def body(x_ref, zu_ref, zp_ref, zn_ref, cw_ref, cb_ref, w_ref, x2_ref, z_ref, a_ref):
        zu = zu_ref[...].astype(F32)
        dn, up = _shifted(zu, zp_ref[...], zn_ref[...], pl.program_id(0), ns)
        z = cb_ref[...] + dn * cw_ref[0:1, :] + zu * cw_ref[1:2, :] + up * cw_ref[2:3, :]
        z_ref[...] = z.astype(MX)
        a = ffn_act(z[:, :DFF], z[:, DFF:]).astype(MX)
        a_ref[...] = a
        x2_ref[...] = x_ref[...] + jnp.dot(a, w_ref[...], preferred_element_type=F32)

    return _call(body, "ffn_down", (ns,), [_rows(tm, D)] + _halo_specs(T, tm, NUP) + [_full(cw), _full(cb), _full(w_down)],
                 [_rows(tm, D), _rows(tm, NUP), _rows(tm, DFF)],
                 [_sds((T, D)), _sds((T, NUP), MX), _sds((T, DFF), MX)])(x1, zu, zu, zu, cw, cb, w_down)


def loss_head(x, g, tgt, tm):
    T = x.shape[0]

    def body(x_ref, g_ref, t_ref, l_ref, dx_ref, dg_ref):
        @pl.when(pl.program_id(0) == 0)
        def _():
            l_ref[...] = jnp.zeros_like(l_ref)
            dg_ref[...] = jnp.zeros_like(dg_ref)

        y, vjp = jax.vjp(rmsnorm, x_ref[...], g_ref[...])
        err = y - t_ref[...]
        l_ref[...] += jnp.sum(err * err, axis=0, keepdims=True)
        dx, dg = vjp(err * (1.0 / D))
        dx_ref[...] = dx
        dg_ref[...] += dg

    return _call(body, "loss_head", (T // tm,), [_rows(tm, D), _full(g), _rows(tm, D)],
                 [_full(g), _rows(tm, D), _full(g)], [_sds((1, D)), _sds((T, D)), _sds((1, D))])(x, g, tgt)


def ffn_down_bwd(dx2, z, w_down, tm):
    T = dx2.shape[0]

    def body(dx_ref, z_ref, w_ref, dz_ref):
        da = _dg(dx_ref[...], w_ref[...], 1, 1)
        _, vjp = jax.vjp(ffn_act, z_ref[:, :DFF].astype(F32), z_ref[:, DFF:].astype(F32))
        dzg, dzv = vjp(da)
        dz_ref[:, :DFF] = dzg.astype(MX)
        dz_ref[:, DFF:] = dzv.astype(MX)

    return _call(body, "ffn_down_bwd", (T // tm,), [_rows(tm, D), _rows(tm, NUP), _full(w_down)],
                 _rows(tm, NUP), _sds((T, NUP), MX))(dx2, z, w_down)


def ffn_conv_bwd(dz, zu, cw, tm):
    T = dz.shape[0]
    ns = T // tm

    def body(dz_ref, dp_ref, dn_ref, zu_ref, cw_ref, dzu_ref, dcw_ref, dcb_ref):
        i = pl.program_id(0)

        @pl.when(i == 0)
        def _():
            dcw_ref[...] = jnp.zeros_like(dcw_ref)
            dcb_ref[...] = jnp.zeros_like(dcb_ref)

        dzb = dz_ref[...]
        dz = dzb.astype(F32)
        zu = zu_ref[...].astype(F32)
        r = lax.broadcasted_iota(jnp.int32, (tm, tm), 0)
        c = lax.broadcasted_iota(jnp.int32, (tm, tm), 1)
        row = lax.broadcasted_iota(jnp.int32, (tm, 1), 0)
        pr = jnp.where(i > 0, dp_ref[HALO - 1:HALO, :].astype(F32), 0.0)
        nx = jnp.where(i < ns - 1, dn_ref[0:1, :].astype(F32), 0.0)
        ddn = jnp.where(row == 0, pr, _dg(jnp.where(c == r - 1, 1.0, 0.0), dzb, 1, 0))
        dup = jnp.where(row == tm - 1, nx, _dg(jnp.where(c == r + 1, 1.0, 0.0), dzb, 1, 0))
        dzu_ref[...] = (dup * cw_ref[0:1, :] + dz * cw_ref[1:2, :] + ddn * cw_ref[2:3, :]).astype(MX)
        dcw_ref[0:1, :] += jnp.sum(zu * dup, axis=0, keepdims=True)
        dcw_ref[1:2, :] += jnp.sum(zu * dz, axis=0, keepdims=True)
        dcw_ref[2:3, :] += jnp.sum(zu * ddn, axis=0, keepdims=True)
        dcb_ref[...] += jnp.sum(dz, axis=0, keepdims=True)

    return _call(body, "ffn_conv_bwd", (ns,), _halo_specs(T, tm, NUP) + [_rows(tm, NUP), _full(cw)],
                 [_rows(tm, NUP), _full(cw), pl.BlockSpec((1, NUP), lambda i: (0, 0))],
                 [_sds((T, NUP), MX), _sds((3, NUP)), _sds((1, NUP))])(dz, dz, dz, zu, cw)


def nt_normbwd(dys, w, x, g, dres, tm, name):
    T = x.shape[0]
    n = len(dys)
    offs = [sum(d.shape[1] for d in dys[:i]) for i in range(n + 1)]

    def body(*refs):
        dy_refs, (w_ref, x_ref, g_ref, dr_ref, dx_ref, dg_ref) = refs[:n], refs[n:]

        @pl.when(pl.program_id(0) == 0)
        def _():
            dg_ref[...] = jnp.zeros_like(dg_ref)

        dh = _dg(dy_refs[0][...], w_ref[:, offs[0]:offs[1]], 1, 1)
        for i in range(1, n):
            dh = dh + _dg(dy_refs[i][...], w_ref[:, offs[i]:offs[i + 1]], 1, 1)
        _, vjp = jax.vjp(rmsnorm, x_ref[...], g_ref[...])
        dx, dg = vjp(dh)
        dx_ref[...] = dr_ref[...] + dx
        dg_ref[...] += dg

    return _call(body, name, (T // tm,),
                 [_rows(tm, d.shape[1]) for d in dys] + [_full(w), _rows(tm, D), _full(g), _rows(tm, D)],
                 [_rows(tm, D), _full(g)], [_sds((T, D)), _sds((1, D))])(*dys, w, x, g, dres)


def matmul_tn(a, b, tt, tn, name):
    T, k = a.shape
    n = b.shape[1]
    last = T // tt - 1

    def body(a_ref, b_ref, o_ref, acc_ref):
        @pl.when(pl.program_id(1) == 0)
        def _():
            acc_ref[...] = jnp.zeros_like(acc_ref)

        acc_ref[...] += _dg(a_ref[...], b_ref[...], 0, 0)

        @pl.when(pl.program_id(1) == last)
        def _():
            o_ref[...] = acc_ref[...].astype(MX)

    return _call(body, name, (n // tn, T // tt),
                 [pl.BlockSpec((tt, k), lambda j, i: (i, 0)), pl.BlockSpec((tt, tn), lambda j, i: (i, j))],
                 pl.BlockSpec((k, tn), lambda j, i: (0, j)), _sds((k, n), MX), scratch=[pltpu.VMEM((k, tn), F32)],
                 n_axes=2)(a, b)


def mix_out_bwd(dx1, w_out, of, ob, p, gg, tm):
    T = dx1.shape[0]

    def body(dx_ref, w_ref, of_ref, ob_ref, pg_ref, gg_ref, da_ref, do_ref, dpg_ref, dgg_ref):
        @pl.when(pl.program_id(0) == 0)
        def _():
            dgg_ref[...] = jnp.zeros_like(dgg_ref)

        dxb = dx_ref[...].astype(MX)
        da_ref[...] = _dg(dxb, w_ref[0:512, :], 1, 1)
        for h in range(4):
            sl = slice(h * 128, (h + 1) * 128)
            dm = _dg(dxb, w_ref[512 + h * 128:512 + (h + 1) * 128, :], 1, 1)
            _, vjp = jax.vjp(outb_head, of_ref[:, sl] + ob_ref[:, sl], pg_ref[:, sl], gg_ref[h])
            do, dpg, dg = vjp(dm)
            do_ref[:, sl] = do
            dpg_ref[:, sl] = dpg
            dgg_ref[h] += dg

    return _call(body, "mix_out_bwd", (T // tm,),
                 [_rows(tm, D), _full(w_out), _rows(tm, 512), _rows(tm, 512), _rows(tm, 512, 4), _full(gg)],
                 [_rows(tm, 512), _rows(tm, 512), _rows(tm, 512), _full(gg)],
                 [_sds((T, 512)), _sds((T, 512)), _sds((T, 512)), _sds(gg.shape)])(dx1, w_out, of, ob, p, gg)


def gla_bwd(p, wg, bg, ss, do, reverse, merge=None):
    T = p.shape[0]
    tm = CPB * CH
    n = T // tm
    rev = not reverse
    rn = n if rev else None

    def body(*refs):
        q_ref, k_ref, v_ref, r_ref, wg_ref, bg_ref, ss_ref, do_ref = refs[:8]
        if merge is None:
            dq_ref, dk_ref, dv_ref, dr_ref, dwg_ref, dbg_ref, dst_ref = refs[8:]
        else:
            mq_ref, mk_ref, mv_ref, mr_ref, mg_ref, out_ref, dwg_ref, dbg_ref, dst_ref, drs_ref = refs[8:]
            out_ref[:, 1024:1536] = mg_ref[...].astype(MX)

        @pl.when(pl.program_id(0) == 0)
        def _():
            dst_ref[...] = jnp.zeros_like(dst_ref)
            dwg_ref[...] = jnp.zeros_like(dwg_ref)
            dbg_ref[...] = jnp.zeros_like(dbg_ref)

        consts = _tri(reverse) + _lane_masks()
        for j in range(2):
            sl = slice(j * 128, (j + 1) * 128)
            v0s, v1s = slice(256 * j, 256 * j + 128), slice(256 * j + 128, 256 * j + 256)
            d0, d1 = dst_ref[2 * j], dst_ref[2 * j + 1]
            dwg, dbg = jnp.zeros((128, 128), F32), jnp.zeros((1, 128), F32)
            for c in (reversed(range(CPB)) if rev else range(CPB)):
                rows = _chunk(c)
                _, vjp = jax.vjp(functools.partial(gla_pair, consts), r_ref[rows, :], wg_ref[:, sl], bg_ref[:, sl],
                                 q_ref[rows, sl], k_ref[rows, sl], v_ref[rows, v0s], v_ref[rows, v1s],
                                 ss_ref[c, 2 * j], ss_ref[c, 2 * j + 1])
                g = vjp((do_ref[rows, v0s], do_ref[rows, v1s], d0, d1))
                dwg, dbg = dwg + g[1], dbg + g[2]
                d0, d1 = g[7], g[8]
                if merge is None:
                    if j == 0:
                        dr_ref[rows, :] = g[0]
                    else:
                        dr_ref[rows, :] += g[0]
                    dq_ref[rows, sl] = g[3]
                    dk_ref[rows, sl] = g[4]
                    dv_ref[rows, v0s] = g[5]
                    dv_ref[rows, v1s] = g[6]
                else:
                    if j == 0:
                        drs_ref[rows, :] = mr_ref[rows, :] + g[0]
                    else:
                        out_ref[rows, 1536:1664] = (drs_ref[rows, :] + g[0]).astype(MX)
                    out_ref[rows, sl] = (mq_ref[rows, sl] + g[3]).astype(MX)
                    out_ref[rows, 256 + 128 * j:384 + 128 * j] = (mk_ref[rows, sl] + g[4]).astype(MX)
                    out_ref[rows, 512 + 256 * j:640 + 256 * j] = (mv_ref[rows, v0s] + g[5]).astype(MX)
                    out_ref[rows, 640 + 256 * j:768 + 256 * j] = (mv_ref[rows, v1s] + g[6]).astype(MX)
            dst_ref[2 * j] = d0
            dst_ref[2 * j + 1] = d1
            dwg_ref[:, sl] += dwg
            dbg_ref[:, sl] += dbg

    ss_spec = pl.BlockSpec((CPB, 4, 128, 128), (lambda i: (n - 1 - i, 0, 0, 0)) if rev else (lambda i: (i, 0, 0, 0)))
    ins = [p, p, p, p, wg, bg, ss, do]
    in_specs = _gla_in_specs(tm, n, rev) + [_full(wg), _full(bg), ss_spec, _rows(tm, 512, 0, rn)]
    scratch = [pltpu.VMEM((4, 128, 128), F32)]
    if merge is None:
        out_specs = [_rows(tm, 256, 0, rn), _rows(tm, 256, 0, rn), _rows(tm, 512, 0, rn), _rows(tm, 128, 0, rn)]
        out_shape = [_sds((T, 256)), _sds((T, 256)), _sds((T, 512)), _sds((T, 128))]
    else:
        ins += list(merge)
        in_specs += [_rows(tm, a.shape[1], 0, rn) for a in merge]
        out_specs, out_shape = [_rows(tm, 1664, 0, rn)], [_sds((T, 1664), MX)]
        scratch.append(pltpu.VMEM((tm, 128), F32))
    return _call(body, "gla_bwd_r" if reverse else "gla_bwd_f", (n,), in_specs, out_specs + [_full(wg), _full(bg)],
                 out_shape + [_sds(wg.shape), _sds(bg.shape)], scratch=scratch)(*ins)


def gmlp_bwd(p, douta, ws, bs, lg, lb):
    T = p.shape[0]
    cpb = 1
    tm = cpb * CH

    def body(pa_ref, do_ref, ws_ref, bs_ref, lg_ref, lb_ref, dpa_ref, dws_ref, dbs_ref, dlg_ref, dlb_ref):
        @pl.when(pl.program_id(0) == 0)
        def _():
            for r in (dws_ref, dbs_ref, dlg_ref, dlb_ref):
                r[...] = jnp.zeros_like(r)

        for c in range(cpb):
            rows = _chunk(c)
            for h in range(4):
                us, vs = slice(h * 128, (h + 1) * 128), slice(512 + h * 128, 512 + (h + 1) * 128)
                _, vjp = jax.vjp(gmlp_head, pa_ref[rows, us], pa_ref[rows, vs], ws_ref[h], bs_ref[h], lg_ref[h], lb_ref[h])
                du, dv, *dparams = vjp(do_ref[rows, us])
                dpa_ref[rows, us] = du.astype(MX)
                dpa_ref[rows, vs] = dv.astype(MX)
                for r, a in zip((dws_ref, dbs_ref, dlg_ref, dlb_ref), dparams):
                    r[h] += a

    return _call(body, "gmlp_bwd", (T // tm,),
                 [_rows(tm, 1024), _rows(tm, 512), _full(ws), _full(bs), _full(lg), _full(lb)],
                 [_rows(tm, 1024), _full(ws), _full(bs), _full(lg), _full(lb)],
                 [_sds((T, 1024), MX), _sds(ws.shape), _sds(bs.shape), _sds(lg.shape), _sds(lb.shape)])(p, douta, ws, bs, lg, lb)


def _gate_pad(w, row0):
    return jnp.zeros((128, 256), F32).at[row0:row0 + 16].set(w)


def local_step(x, tgt, W, get_big, emit, tm=256, tmm=512):
    saved = []
    for l in range(NL):
        s = {"x": x}
        s.update(get_big(l, "in", x))
        p, s["h"] = norm_matmul(x, W["g_mix"][l][None], s["w_in"], tmm, "mix_in")
        s["p"] = p
        ws, bs = W["w_s"][l], W["b_s"][l][:, :, None]
        lg, lb = W["ln_g"][l][:, None, :], W["ln_b"][l][:, None, :]
        outa = gmlp_fwd(p, ws, bs, lg, lb)
        wgf, wgb = _gate_pad(W["w_gate_f"][l], 0), _gate_pad(W["w_gate_b"][l], 16)
        bgf, bgb = W["b_gate_f"][l][None], W["b_gate_b"][l][None]
        s["of"], s["ssf"] = gla_fwd(p, wgf, bgf, False)
        s["ob"], s["ssb"] = gla_fwd(p, wgb, bgb, True)
        s.update(get_big(l, "rest", s["ob"]))
        gg = W["g_gla"][l][:, None, :]
        x1, s["mixed"] = mix_out(x, s["of"], s["ob"], p, outa, gg, s["w_out"], tmm)
        s["x1"] = x1
        s["zu"], s["h2"] = norm_matmul(x1, W["g_ffn"][l][None], s["w_up"], tmm, "ffn_up", MX)
        x, s["z"], s["a"] = ffn_down(x1, s["zu"], W["conv_w"][l], W["conv_b"][l][None], s["w_down"], tm)
        saved.append(s)

    lsum, dx, dgf = loss_head(x, W["g_final"][None], tgt, tmm)
    G = {k: [None] * NL for k in _SMALL if k != "g_final"}
    tok = jnp.zeros((1, 1), F32)
    for l in reversed(range(NL)):
        s = saved[l]
        g_down = matmul_tn(s["a"], dx, min(1024, tmm * 2), 512, "dw_down")
        dz = ffn_down_bwd(dx, s["z"], s["w_down"], tm)
        dzu, G["conv_w"][l], dcb = ffn_conv_bwd(dz, s["zu"], W["conv_w"][l] + tok, tm)
        G["conv_b"][l] = dcb[0]
        g_up = matmul_tn(s["h2"], dzu, min(1024, tmm * 2), 1408, "dw_up")
        tok = emit(l, "A", {"w_down": g_down, "w_up": g_up})
        dx1, dg = nt_normbwd([dzu], s["w_up"], s["x1"], W["g_ffn"][l][None] + tok, dx, tmm, "ffn_up_bwd")
        G["g_ffn"][l] = dg[0]
        g_out = matmul_tn(s["mixed"], dx1, min(1024, tmm * 2), 1024, "dw_out")
        gg = W["g_gla"][l][:, None, :]
        douta, do, dpg, dgg = mix_out_bwd(dx1, s["w_out"], s["of"], s["ob"], s["p"], gg, tmm)
        G["g_gla"][l] = dgg[:, 0, :]
        wgf, wgb = _gate_pad(W["w_gate_f"][l], 0), _gate_pad(W["w_gate_b"][l], 16)
        bgf, bgb = W["b_gate_f"][l][None], W["b_gate_b"][l][None]
        dqf, dkf, dvf, drf, dwgf, dbgf = gla_bwd(s["p"], wgf, bgf, s["ssf"], do, False)
        dpb, dwgb, dbgb = gla_bwd(s["p"], wgb, bgb, s["ssb"], do, True, merge=(dqf, dkf, dvf, drf, dpg))
        G["w_gate_f"][l], G["b_gate_f"][l] = dwgf[0:16], dbgf[0]
        G["w_gate_b"][l], G["b_gate_b"][l] = dwgb[16:32], dbgb[0]
        ws, bs = W["w_s"][l], W["b_s"][l][:, :, None]
        lg, lb = W["ln_g"][l][:, None, :], W["ln_b"][l][:, None, :]
        dpa, G["w_s"][l], dbs, dlg, dlb = gmlp_bwd(s["p"], douta, ws, bs, lg, lb)
        G["b_s"][l], G["ln_g"][l], G["ln_b"][l] = dbs[:, :, 0], dlg[:, 0, :], dlb[:, 0, :]
        tt = min(1024, tmm * 2)
        g_in = jnp.concatenate([matmul_tn(s["h"], dpa, tt, 1024, "dw_in_a"),
                                matmul_tn(s["h"], dpb, tt, 1664, "dw_in_b")], axis=1)
        tok = emit(l, "B", {"w_out": g_out, "w_in": g_in})
        dx, dg = nt_normbwd([dpa, dpb], s["w_in"], s["x"], W["g_mix"][l][None] + tok, dx1, tmm, "mix_in_bwd")
        G["g_mix"][l] = dg[0]
    G = {k: jnp.stack(v) for k, v in G.items()}
    G["g_final"] = dgf[0]
    return lsum, dx, G


def cast_bf16(a, tr):
    r, c = a.shape

    def body(a_ref, o_ref):
        o_ref[...] = a_ref[...].astype(BF16)

    return _call(body, "cast_bf16", (r // tr,), [_rows(tr, c)], _rows(tr, c), _sds((r, c), BF16))(a)


def sum_lead(y, tr):
    n, rr, cc = y.shape

    def body(y_ref, o_ref):
        acc = y_ref[0].astype(F32)
        for k in range(1, n):
            acc = acc + y_ref[k].astype(F32)
        o_ref[...] = acc

    return _call(body, "sum_lead", (rr // tr,), [pl.BlockSpec((n, tr, cc), lambda i: (0, i, 0))],
                 _rows(tr, cc), _sds((rr, cc)))(y)


def sum_parts(land, grad, k, chipvec, tr):
    _, rr, cc = land.shape
    nb = rr // tr
    if k == 0:
        own = pl.BlockSpec((None, tr, cc), lambda i, c: (c[0], i, 0))
    elif k == 2:
        own = pl.BlockSpec((tr, cc), lambda i, c: (i, c[0]))
    else:
        own = pl.BlockSpec((tr, cc), lambda i, c: (c[0] * nb + i, 0))

    def body(c_ref, l_ref, g_ref, o_ref):
        mine = g_ref[...].astype(F32)
        acc = None
        for j in range(4):
            part = jnp.where(c_ref[0] == j, mine, l_ref[j].astype(F32))
            acc = part if acc is None else acc + part
        o_ref[...] = acc

    gs = pltpu.PrefetchScalarGridSpec(
        num_scalar_prefetch=1, grid=(nb,),
        in_specs=[pl.BlockSpec((4, tr, cc), lambda i, c: (0, i, 0)), own],
        out_specs=pl.BlockSpec((tr, cc), lambda i, c: (i, 0)))
    return pl.pallas_call(body, name="sum_parts", grid_spec=gs, out_shape=_sds((rr, cc)),
                          compiler_params=_cparams(1))(chipvec, land, grad)


def adamw(w, ga, gb, m, v, tr):
    r, c = w.shape

    def body(w_ref, ga_ref, gb_ref, m_ref, v_ref, g_ref, d_ref, nm_ref, nv_ref):
        gr = ga_ref[...] + gb_ref[...]
        g_ref[...] = gr
        nm = ADAM_B1 * m_ref[...] + (1.0 - ADAM_B1) * gr
        nv = ADAM_B2 * v_ref[...] + (1.0 - ADAM_B2) * jnp.square(gr)
        m_hat = nm / (1.0 - ADAM_B1 ** ADAM_STEP)
        v_hat = nv / (1.0 - ADAM_B2 ** ADAM_STEP)
        d_ref[...] = -ADAM_LR * (m_hat / (jnp.sqrt(v_hat) + ADAM_EPS) + ADAM_WD * w_ref[...])
        nm_ref[...] = nm
        nv_ref[...] = nv

    sp = _rows(tr, c)
    return _call(body, "adamw", (r // tr,), [sp] * 5, [sp] * 4, [_sds((r, c))] * 4)(w, ga, gb, m, v)


MESH = pl.DeviceIdType.MESH
ANY = pl.BlockSpec(memory_space=pl.ANY)
N_BIG = 4


def _pos():
    return lax.axis_index("x"), lax.axis_index("y"), lax.axis_index("c")


def _other_chips(x, y):
    return [(1 - x, y), (x, 1 - y), (1 - x, 1 - y)]


def _rcopy(src, dst, send_sems, recv_sems, k, to):
    return pltpu.make_async_remote_copy(src_ref=src, dst_ref=dst, send_sem=send_sems.at[k], recv_sem=recv_sems.at[k],
                                        device_id=to, device_id_type=MESH)


def allgather8(xs):
    m, n = xs.shape

    def body(x_ref, out_ref, send_sems, recv_sems, local_sem):
        x, y, c = _pos()
        me, sibling = (x, y, c), (x, y, 1 - c)
        chips = _other_chips(x, y)

        def rows(px, py, pc):
            return out_ref.at[pl.ds((4 * px + 2 * py + pc) * m, m), :]

        def copy(k, block, to, src=None):
            return _rcopy(rows(*block) if src is None else src, rows(*block), send_sems, recv_sems, k, to)

        mine = pltpu.make_async_copy(x_ref, rows(*me), local_sem)
        mine.start()
        first = [copy(0, me, sibling, src=x_ref)]
        first += [copy(1 + j, me, (*chip, c), src=x_ref) for j, chip in enumerate(chips)]
        for cp in first:
            cp.start()
        passed = [copy(4 + j, (*chip, c), sibling) for j, chip in enumerate(chips)]
        for j, chip in enumerate(chips):
            copy(1 + j, (*chip, c), me).wait_recv()
            passed[j].start()
        copy(0, sibling, me).wait_recv()
        for j, chip in enumerate(chips):
            copy(4 + j, (*chip, 1 - c), me).wait_recv()
        for cp in first + passed:
            cp.wait_send()
        mine.wait()

    vm = pl.BlockSpec(memory_space=pltpu.VMEM)
    return pl.pallas_call(
        body, name="allgather8", out_shape=_sds((8 * m, n), xs.dtype), in_specs=[vm], out_specs=vm,
        scratch_shapes=[pltpu.SemaphoreType.DMA((7,)), pltpu.SemaphoreType.DMA((7,)), pltpu.SemaphoreType.DMA],
        compiler_params=pltpu.CompilerParams(vmem_limit_bytes=VMEM_LIMIT))(xs)


def _slab(k, ref, j):
    if k == 0:
        return ref.at[j]
    if k == 1:
        return ref.at[pl.ds(256 * j, 256), :]
    if k == 2:
        return ref.at[:, pl.ds(1408 * j, 1408)]
    return ref.at[pl.ds(704 * j, 704), :]


_LAYER_FULL = [(4, 1024, 648), (1024, 1024), (1024, NUP), (DFF, 1024)]
_LAYER_SHARD = [(1024, 648), (256, 1024), (1024, 1408), (704, 1024)]
_SHARD_SHAPES = [(NL,) + s for s in _LAYER_SHARD]

HBM = pl.BlockSpec(memory_space=pltpu.HBM)
SEM = pl.BlockSpec(memory_space=pltpu.SEMAPHORE)
VM = pl.BlockSpec(memory_space=pltpu.VMEM)
EFFECT = pltpu.SideEffectType.DATAFLOW_SIDE_EFFECTING
_GW_GROUPS = [[(0, 0)], [(0, 1), (0, 2), (0, 3)]] + [[(l, k) for k in range(N_BIG)] for l in range(1, NL)]
_GW_ORDER = [lk for g in _GW_GROUPS for lk in g]


def _hbm(a):
    return pltpu.with_memory_space_constraint(a, pltpu.HBM)


def _hbm_like(a):
    return pltpu.HBM(a.shape, a.dtype)


def gw_start(shards, landings):
    n = len(_GW_ORDER)

    def body(*refs):
        S, Ld = refs[:N_BIG], refs[N_BIG:N_BIG + n]
        outs = refs[N_BIG + n:]
        send_sems, recv, token = outs[0], outs[1:1 + len(_GW_GROUPS)], outs[-1]
        x, y, c = _pos()
        me = 2 * x + y
        ci = 0
        for gi, grp in enumerate(_GW_GROUPS):
            for t, (l, k) in enumerate(grp):
                land = Ld[_GW_ORDER.index((l, k))]
                for j, (px, py) in enumerate(_other_chips(x, y)):
                    pltpu.make_async_remote_copy(
                        src_ref=S[k].at[l], dst_ref=_slab(k, land, me), send_sem=send_sems.at[ci],
                        recv_sem=recv[gi].at[3 * t + j], device_id=(px, py, c), device_id_type=MESH).start()
                    ci += 1
        token[...] = jnp.zeros_like(token)

    ins = list(shards) + list(landings)
    sems = [pltpu.SemaphoreType.DMA((3 * n,))] + [pltpu.SemaphoreType.DMA((3 * len(g),)) for g in _GW_GROUPS]
    outs = pl.pallas_call(
        body, name="gw_start", out_shape=sems + [_hbm_like(a) for a in ins] + [_sds((8, 128))],
        in_specs=[HBM] * len(ins), out_specs=[SEM] * len(sems) + [HBM] * len(ins) + [VM],
        input_output_aliases={i: len(sems) + i for i in range(len(ins))},
        compiler_params=pltpu.CompilerParams(has_side_effects=EFFECT))(*[_hbm(a) for a in ins])
    ns = len(sems)
    return outs[0], outs[1:ns], outs[ns:ns + N_BIG], outs[ns + N_BIG:ns + len(ins)], outs[-1]


def gw_wait(gi, landings, recv_sems, after, shards=None, send_sems=None):
    grp = _GW_GROUPS[gi]
    n = len(grp)
    last = shards is not None

    def body(*refs):
        Ld, rs = refs[:n], refs[n]
        x, y, c = _pos()
        for t, (l, k) in enumerate(grp):
            for j, (px, py) in enumerate(_other_chips(x, y)):
                region = _slab(k, Ld[t], 2 * px + py)
                pltpu.make_async_remote_copy(src_ref=region, dst_ref=region, send_sem=rs.at[3 * t + j],
                                             recv_sem=rs.at[3 * t + j], device_id=(px, py, c),
                                             device_id_type=MESH).wait_recv()
        if last:
            S, ss = refs[n + 2:n + 2 + N_BIG], refs[n + 2 + N_BIG]
            me = 2 * x + y
            for ci, (l, k) in enumerate(lk for lk in _GW_ORDER for _ in range(3)):
                pltpu.make_async_remote_copy(src_ref=S[k].at[l], dst_ref=_slab(k, Ld[k], me), send_sem=ss.at[ci],
                                             recv_sem=ss.at[ci], device_id=(x, y, c), device_id_type=MESH).wait_send()

    ins = list(landings) + [recv_sems, after]
    specs = [HBM] * n + [SEM, pl.BlockSpec(memory_space=pl.ANY)]
    outs = [_hbm_like(a) for a in landings]
    alias = {i: i for i in range(n)}
    if last:
        ins += list(shards) + [send_sems]
        specs += [HBM] * N_BIG + [SEM]
        outs += [_hbm_like(a) for a in shards]
        alias.update({n + 2 + i: n + i for i in range(N_BIG)})
    res = pl.pallas_call(body, name="gw_wait_%d" % gi, out_shape=outs, in_specs=specs, out_specs=[HBM] * len(outs),
                         input_output_aliases=alias,
                         compiler_params=pltpu.CompilerParams(has_side_effects=EFFECT))(*ins)
    return res[:n]


def ga_start(tag, ks, grads, landings):
    n = len(ks)

    def body(*refs):
        G, Ld = refs[:n], refs[n:2 * n]
        send_sems, recv_sems, token = refs[2 * n], refs[2 * n + 1], refs[-1]
        x, y, c = _pos()
        me = 2 * x + y
        for t, k in enumerate(ks):
            for j, (px, py) in enumerate(_other_chips(x, y)):
                pltpu.make_async_remote_copy(
                    src_ref=_slab(k, G[t], 2 * px + py), dst_ref=Ld[t].at[me], send_sem=send_sems.at[3 * t + j],
                    recv_sem=recv_sems.at[3 * t + j], device_id=(px, py, c), device_id_type=MESH).start()
        token[...] = jnp.zeros_like(token)

    ins = list(grads) + list(landings)
    sems = [pltpu.SemaphoreType.DMA((3 * n,))] * 2
    outs = pl.pallas_call(
        body, name="ga_start_" + tag, out_shape=sems + [_hbm_like(a) for a in ins] + [_sds((8, 128))],
        in_specs=[HBM] * len(ins), out_specs=[SEM, SEM] + [HBM] * len(ins) + [VM],
        input_output_aliases={i: 2 + i for i in range(len(ins))},
        compiler_params=pltpu.CompilerParams(has_side_effects=EFFECT))(*[_hbm(a) for a in ins])
    return outs[0], outs[1], outs[2:2 + n], outs[2 + n:2 + 2 * n], outs[-1]


def ga_wait(tag, ks, send_sems, recv_sems, grads, landings, after):
    n = len(ks)

    def body(*refs):
        G, Ld, ss, rs = refs[:n], refs[n:2 * n], refs[2 * n], refs[2 * n + 1]
        x, y, c = _pos()
        me = 2 * x + y
        for t, k in enumerate(ks):
            for j, (px, py) in enumerate(_other_chips(x, y)):
                pj = 2 * px + py
                cp = pltpu.make_async_remote_copy(
                    src_ref=_slab(k, G[t], pj), dst_ref=Ld[t].at[pj], send_sem=ss.at[3 * t + j],
                    recv_sem=rs.at[3 * t + j], device_id=(px, py, c), device_id_type=MESH)
                cp.wait_send()
                cp.wait_recv()

    ins = list(grads) + list(landings) + [send_sems, recv_sems, after]
    res = pl.pallas_call(
        body, name="ga_wait_" + tag, out_shape=[_hbm_like(a) for a in list(grads) + list(landings)],
        in_specs=[HBM] * (2 * n) + [SEM, SEM, pl.BlockSpec(memory_space=pl.ANY)], out_specs=[HBM] * (2 * n),
        input_output_aliases={i: i for i in range(2 * n)},
        compiler_params=pltpu.CompilerParams(has_side_effects=EFFECT))(*ins)
    return res[:n], res[n:]


def swap4(parts):
    def body(*refs):
        Q, R = refs[:N_BIG], refs[N_BIG:2 * N_BIG]
        send_sems, recv_sems = refs[2 * N_BIG:]
        x, y, c = _pos()
        cps = [_rcopy(Q[k], R[k], send_sems, recv_sems, k, (x, y, 1 - c)) for k in range(N_BIG)]
        for cp in cps:
            cp.start()
        for cp in cps:
            cp.wait()

    return pl.pallas_call(
        body, name="swap4", out_shape=[_sds(s) for s in _SHARD_SHAPES],
        in_specs=[ANY] * N_BIG, out_specs=[ANY] * N_BIG,
        scratch_shapes=[pltpu.SemaphoreType.DMA((N_BIG,)), pltpu.SemaphoreType.DMA((N_BIG,))])(*parts)


_WEIGHTS = ['g_mix', 'w_in', 'w_s', 'b_s', 'ln_g', 'ln_b', 'w_gate_f', 'b_gate_f', 'w_gate_b', 'b_gate_b', 'g_gla',
            'w_out', 'g_ffn', 'w_up', 'conv_w', 'conv_b', 'w_down', 'g_final']
_BIG = ['w_in', 'w_out', 'w_up', 'w_down']
_SMALL = [n for n in _WEIGHTS if n not in _BIG]
_SMALL_SHARDED = {'w_gate_f': 64, 'w_gate_b': 64, 'conv_w': 1408}
_BIG_TR = {'w_in': 512, 'w_out': 256, 'w_up': 256, 'w_down': 352}


def _pack(arrs):
    flat = jnp.concatenate([a.reshape(-1) for a in arrs])
    pad = (-flat.shape[0]) % 1024
    return jnp.pad(flat, (0, pad)).reshape(-1, 128)


def _unpack(buf, shapes):
    flat = buf.reshape(-1)
    out, o = [], 0
    for s in shapes:
        n = 1
        for d in s:
            n *= d
        out.append(flat[o:o + n].reshape(s))
        o += n
    return out


def _place(k, full, part, chip):
    if k == 0:
        return lax.dynamic_update_index_in_dim(full, part, chip, 0)
    if k == 2:
        return lax.dynamic_update_slice_in_dim(full, part, chip * 1408, axis=1)
    return lax.dynamic_update_slice_in_dim(full, part, chip * _LAYER_SHARD[k][0], axis=0)


def kernel(x, g_mix, w_in, w_s, b_s, ln_g, ln_b, w_gate_f, b_gate_f, w_gate_b, b_gate_b, g_gla, w_out, g_ffn, w_up, conv_w, conv_b, w_down, g_final, loss_target, m_g_mix, m_w_in, m_w_s, m_b_s, m_ln_g, m_ln_b, m_w_gate_f, m_b_gate_f, m_w_gate_b, m_b_gate_b, m_g_gla, m_w_out, m_g_ffn, m_w_up, m_conv_w, m_conv_b, m_w_down, m_g_final, v_g_mix, v_w_in, v_w_s, v_b_s, v_ln_g, v_ln_b, v_w_gate_f, v_b_gate_f, v_w_gate_b, v_b_gate_b, v_g_gla, v_w_out, v_g_ffn, v_w_up, v_conv_w, v_conv_b, v_w_down, v_g_final):
    loc = locals()
    w = {n: loc[n] for n in _WEIGHTS}
    m = {n: loc["m_" + n] for n in _WEIGHTS}
    v = {n: loc["v_" + n] for n in _WEIGHTS}
    xi, yi, _ = _pos()
    chip = 2 * xi + yi

    sh_names = list(_SMALL_SHARDED)
    g8 = allgather8(_pack([w[n] for n in sh_names]))
    rows = g8.shape[0] // 8
    per_chip = [_unpack(g8[2 * j * rows:(2 * j + 1) * rows], [w[n].shape for n in sh_names]) for j in range(4)]
    W = dict(w)
    for k, n in enumerate(sh_names):
        W[n] = jnp.concatenate([per_chip[j][k] for j in range(4)], axis=-1)

    shards = [cast_bf16(w[n].reshape(-1, w[n].shape[-1]), _BIG_TR[n]).reshape(w[n].shape) for n in _BIG]
    landings = [_place(k, lax.empty(_LAYER_FULL[k], BF16), shards[k][l], chip) for l, k in _GW_ORDER]
    send_sems, recv_sems, shards_fly, landings_fly, _ = gw_start(shards, landings)
    arrived = {}

    def get_big(l, stage, after):
        gi = {(0, "in"): 0, (0, "rest"): 1}.get((l, stage), l + 1 if stage == "in" else None)
        if gi is not None:
            lo = sum(len(g) for g in _GW_GROUPS[:gi])
            lands = landings_fly[lo:lo + len(_GW_GROUPS[gi])]
            if gi == len(_GW_GROUPS) - 1:
                full = gw_wait(gi, lands, recv_sems[gi], after, shards_fly, send_sems)
            else:
                full = gw_wait(gi, lands, recv_sems[gi], after)
            arrived.update(zip(_GW_GROUPS[gi], full))
        if stage == "in":
            f_in = jnp.transpose(arrived[(l, 0)], (1, 0, 2)).reshape(D, N_IN)
            return {"w_in": jnp.pad(f_in, ((0, 0), (0, N_INP - N_IN)))}
        return {"w_out": arrived[(l, 1)], "w_up": arrived[(l, 2)], "w_down": arrived[(l, 3)]}

    flying = []

    def emit(l, group, grads):
        ks = [3, 2] if group == "A" else [1, 0]
        gs = [grads[_BIG[k]] for k in ks]
        if group == "B":
            gs[1] = jnp.transpose(gs[1][:, :N_IN].reshape(D, 4, 648), (1, 0, 2))
        lands = [lax.empty((4,) + _LAYER_SHARD[k], BF16) for k in ks]
        tag = "%d%s" % (l, group)
        ss, rs, gs_fly, lands_fly, tok = ga_start(tag, ks, gs, lands)
        flying.append((tag, l, ks, ss, rs, gs_fly, lands_fly))
        return tok[0:1, 0:1]

    lsum, grad_x, G = local_step(x[0], loss_target[0], W, get_big, emit)

    plane = [[None] * NL for _ in range(N_BIG)]
    chipvec = jnp.reshape(chip, (1,)).astype(jnp.int32)
    for tag, l, ks, ss, rs, gs_fly, lands_fly in flying:
        for k, g, a in zip(ks, *ga_wait(tag, ks, ss, rs, gs_fly, lands_fly, grad_x)):
            plane[k][l] = sum_parts(a, g, k, chipvec, _BIG_TR[_BIG[k]])
    plane = [jnp.stack(p) for p in plane]
    other = swap4(plane)

    small_shapes = [G[n].shape for n in _SMALL] + [(D,)]
    pk = _pack([G[n] for n in _SMALL] + [lsum])
    srows = pk.shape[0]
    red = sum_lead(allgather8(pk).reshape(8, srows, 128), srows)
    small = dict(zip(_SMALL + ["lsum"], _unpack(red, small_shapes)))
    loss = 0.5 * jnp.sum(small.pop("lsum")) / D
    for n, wd in _SMALL_SHARDED.items():
        small[n] = lax.dynamic_slice_in_dim(small[n], chip * wd, wd, axis=small[n].ndim - 1)

    grads, delta, new_m, new_v = dict(small), {}, {}, {}
    two = lambda a: a.reshape(-1, a.shape[-1])
    for k, n in enumerate(_BIG):
        res = adamw(two(w[n]), two(plane[k]), two(other[k]), two(m[n]), two(v[n]), _BIG_TR[n])
        grads[n], delta[n], new_m[n], new_v[n] = (r.reshape(w[n].shape) for r in res)
    shapes = [w[n].shape for n in _SMALL]
    pw, pg, pm, pv = (_pack([t[n] for n in _SMALL]) for t in (w, grads, m, v))
    _, d_, m_, v_ = adamw(pw, pg, jnp.zeros_like(pg), pm, pv, pw.shape[0])
    for t, buf in ((delta, d_), (new_m, m_), (new_v, v_)):
        t.update(zip(_SMALL, _unpack(buf, shapes)))

    return (loss, grad_x[None], *[grads[n] for n in _WEIGHTS], *[delta[n] for n in _WEIGHTS],
            *[new_m[n] for n in _WEIGHTS], *[new_v[n] for n in _WEIGHTS])
```

```python
import functools

import jax
import jax.numpy as jnp
from jax import lax
from jax.experimental import pallas as pl
from jax.experimental.pallas import tpu as pltpu

F32 = jnp.float32
BF16 = jnp.bfloat16
MX = BF16

D = 1024
CH = 128
NL = 4
N_IN = 2592
N_INP = 2688
NUP = 5632
DFF = 2816
EPS = 1e-6
VMEM_LIMIT = 56 * 1024 * 1024

ADAM_LR, ADAM_B1, ADAM_B2, ADAM_EPS, ADAM_WD, ADAM_STEP = 0.001, 0.9, 0.999, 1e-08, 0.01, 10


def _dg(a, b, ca, cb):
    return lax.dot_general(a.astype(MX), b.astype(MX), (((ca,), (cb,)), ((), ())), preferred_element_type=F32)


@jax.custom_vjp
def mm(a, b):
    return _dg(a, b, 1, 0)


mm.defvjp(lambda a, b: (_dg(a, b, 1, 0), (a, b)),
          lambda r, g: (_dg(g, r[1], 1, 1), _dg(r[0], g, 0, 0)))


@jax.custom_vjp
def mm_nt(a, b):
    return _dg(a, b, 1, 1)


mm_nt.defvjp(lambda a, b: (_dg(a, b, 1, 1), (a, b)),
             lambda r, g: (_dg(g, r[1], 1, 0), _dg(g, r[0], 0, 0)))


@jax.custom_vjp
def mm_tn(a, b):
    return _dg(a, b, 0, 0)


mm_tn.defvjp(lambda a, b: (_dg(a, b, 0, 0), (a, b)),
             lambda r, g: (_dg(r[1], g, 1, 1), _dg(r[0], g, 1, 0)))


def _split3(x):
    hi = x.astype(BF16)
    r1 = x - hi.astype(F32)
    mid = r1.astype(BF16)
    lo = (r1 - mid.astype(F32)).astype(BF16)
    return hi, mid, lo


def _dot3(m, x):
    hi, mid, lo = _split3(x)
    d = lambda p: lax.dot_general(m, p, (((1,), (0,)), ((), ())), preferred_element_type=F32)
    return d(hi) + d(mid) + d(lo)


@jax.custom_vjp
def cumdot(m, mt, x):
    return _dot3(m, x)


cumdot.defvjp(lambda m, mt, x: (_dot3(m, x), (m, mt)),
              lambda r, g: (jnp.zeros_like(r[0]), jnp.zeros_like(r[1]), _dot3(r[1], g)))


def rmsnorm(x, g):
    return x * lax.rsqrt(jnp.mean(x * x, axis=-1, keepdims=True) + EPS) * g


def gelu(x):
    return 0.5 * x * (1.0 + lax.erf(x * 0.7071067811865476))


def sigmoid(x):
    return 1.0 / (1.0 + jnp.exp(-x))


def log_sigmoid(x):
    return jnp.minimum(x, 0.0) - jnp.log(1.0 + jnp.exp(-jnp.abs(x)))


def gmlp_head(u_pre, v_pre, w, bcol, g, b):
    u = gelu(u_pre)
    v = gelu(v_pre)
    mu = jnp.mean(v, axis=-1, keepdims=True)
    var = jnp.mean(jnp.square(v - mu), axis=-1, keepdims=True)
    vn = (v - mu) * lax.rsqrt(var + EPS) * g + b
    return u * (mm(w, vn) + bcol)


def outb_head(o, pg, g):
    return rmsnorm(o, g) * (pg * sigmoid(pg))


def ffn_act(zg, zv):
    return zg * sigmoid(zg) * zv


def _tri(reverse):
    r = lax.broadcasted_iota(jnp.int32, (CH, CH), 0)
    c = lax.broadcasted_iota(jnp.int32, (CH, CH), 1)
    if reverse:
        cm, sm = c >= r, c > r
    else:
        cm, sm = c <= r, c <= r
    one = jnp.ones((), BF16)
    zero = jnp.zeros((), BF16)
    return jnp.where(cm, one, zero), jnp.where(cm.T, one, zero), sm


def gla_pair(consts, wg, bg, st0, st1, *chunks):
    m, mt, smask, lm0, lm1 = consts
    ch = [chunks[5 * i:5 * i + 5] for i in range(len(chunks) // 5)]
    la = [log_sigmoid(mm(c[0], wg) + bg) * (1.0 / 16.0) for c in ch]
    cum = [cumdot(m, mt, x) for x in la]
    tot = [jnp.sum(x, axis=0, keepdims=True) for x in la]
    q_dec = [(c[1] * 0.125) * jnp.exp(cm) for c, cm in zip(ch, cum)]
    k_inv = [c[2] * jnp.exp(-cm) for c, cm in zip(ch, cum)]
    k_end = [c[2] * jnp.exp(t - cm) for c, t, cm in zip(ch, tot, cum)]
    s = [[jnp.where(smask, mm_nt(qd * lm, ki), 0.0) for lm in (lm0, lm1)] for qd, ki in zip(q_dec, k_inv)]
    o_in = [[mm(si[h], c[3 + h]) for h in (0, 1)] for si, c in zip(s, ch)]
    ds = [[mm_tn(c[3 + h], ke * lm) for h, lm in ((0, lm0), (1, lm1))] for c, ke in zip(ch, k_end)]
    sts = [(st0, st1)]
    for t, d in zip(tot, ds):
        dec = jnp.exp(t)
        sts.append((sts[-1][0] * dec + d[0], sts[-1][1] * dec + d[1]))
    outs = []
    for qd, oi, st in zip(q_dec, o_in, sts):
        outs += [oi[0] + mm_nt(qd, st[0]), oi[1] + mm_nt(qd, st[1])]
    return (*outs, sts[-1][0], sts[-1][1])


def _lane_masks():
    lane = lax.broadcasted_iota(jnp.int32, (1, 128), 1)
    return (lane < 64).astype(F32), (lane >= 64).astype(F32)


def _cparams(n_axes=1):
    return pltpu.CompilerParams(dimension_semantics=("arbitrary",) * n_axes, vmem_limit_bytes=VMEM_LIMIT)


def _full(a):
    nd = a.ndim
    return pl.BlockSpec(a.shape, lambda *_: (0,) * nd)


def _rows(tm, w, cb=0, rev_n=None):
    if rev_n is None:
        return pl.BlockSpec((tm, w), lambda i: (i, cb))
    return pl.BlockSpec((tm, w), lambda i: (rev_n - 1 - i, cb))


def _call(body, name, grid, in_specs, out_specs, out_shape, scratch=(), n_axes=1):
    return pl.pallas_call(body, name=name, grid=grid, in_specs=in_specs, out_specs=out_specs, out_shape=out_shape,
                          scratch_shapes=list(scratch), compiler_params=_cparams(n_axes))


def _sds(shape, dt=F32):
    return jax.ShapeDtypeStruct(shape, dt)


def norm_matmul(x, g, w, tm, name, ydt=F32):
    T, n = x.shape[0], w.shape[1]

    def body(x_ref, g_ref, w_ref, y_ref, h_ref):
        hb = rmsnorm(x_ref[...], g_ref[...]).astype(MX)
        h_ref[...] = hb
        y_ref[...] = jnp.dot(hb, w_ref[...], preferred_element_type=F32).astype(ydt)

    return _call(body, name, (T // tm,), [_rows(tm, D), _full(g), _full(w)],
                 [_rows(tm, n), _rows(tm, D)], [_sds((T, n), ydt), _sds((T, D), MX)])(x, g, w)


CPB = 4


def _chunk(c):
    return slice(c * CH, (c + 1) * CH)


def gmlp_fwd(p, ws, bs, lg, lb):
    T = p.shape[0]
    tm = CPB * CH

    def body(pa_ref, ws_ref, bs_ref, lg_ref, lb_ref, o_ref):
        for c in range(CPB):
            for h in range(4):
                o_ref[_chunk(c), h * 128:(h + 1) * 128] = gmlp_head(
                    pa_ref[_chunk(c), h * 128:(h + 1) * 128], pa_ref[_chunk(c), 512 + h * 128:512 + (h + 1) * 128],
                    ws_ref[h], bs_ref[h], lg_ref[h], lb_ref[h]).astype(MX)

    return _call(body, "gmlp_fwd", (T // tm,), [_rows(tm, 1024), _full(ws), _full(bs), _full(lg), _full(lb)],
                 _rows(tm, 512), _sds((T, 512), MX))(p, ws, bs, lg, lb)


def _gla_in_specs(tm, n, rev):
    r = n if rev else None
    return [_rows(tm, 256, 4, r), _rows(tm, 256, 5, r), _rows(tm, 512, 3, r), _rows(tm, 128, 20, r)]


def gla_fwd(p, wg, bg, reverse):
    T = p.shape[0]
    tm = CPB * CH
    n = T // tm
    rev = n if reverse else None

    def body(q_ref, k_ref, v_ref, r_ref, wg_ref, bg_ref, o_ref, ss_ref, st_ref):
        @pl.when(pl.program_id(0) == 0)
        def _():
            st_ref[...] = jnp.zeros_like(st_ref)

        consts = _tri(reverse) + _lane_masks()
        order = list(reversed(range(CPB))) if reverse else list(range(CPB))
        ss_ref[0] = st_ref[...]
        for j in range(2):
            sl = slice(j * 128, (j + 1) * 128)
            v0s, v1s = slice(256 * j, 256 * j + 128), slice(256 * j + 128, 256 * j + 256)
            chunks = []
            for c in order:
                rows = _chunk(c)
                chunks += [r_ref[rows, :], q_ref[rows, sl], k_ref[rows, sl], v_ref[rows, v0s], v_ref[rows, v1s]]
            res = gla_pair(consts, wg_ref[:, sl], bg_ref[:, sl], st_ref[2 * j], st_ref[2 * j + 1], *chunks)
            for i, c in enumerate(order):
                o_ref[_chunk(c), v0s] = res[2 * i]
                o_ref[_chunk(c), v1s] = res[2 * i + 1]
            st_ref[2 * j] = res[-2]
            st_ref[2 * j + 1] = res[-1]

    ss_spec = pl.BlockSpec((1, 4, 128, 128), (lambda i: (n - 1 - i, 0, 0, 0)) if reverse else (lambda i: (i, 0, 0, 0)))
    return _call(body, "gla_fwd_r" if reverse else "gla_fwd_f", (n,),
                 _gla_in_specs(tm, n, reverse) + [_full(wg), _full(bg)],
                 [_rows(tm, 512, 0, rev), ss_spec], [_sds((T, 512)), _sds((n, 4, 128, 128))],
                 scratch=[pltpu.VMEM((4, 128, 128), F32)])(p, p, p, p, wg, bg)


def mix_out(x, of, ob, p, outa, gg, w_out, tm):
    T = x.shape[0]

    def body(x_ref, of_ref, ob_ref, pg_ref, oa_ref, gg_ref, w_ref, x1_ref, mx_ref):
        mx_ref[:, 0:512] = oa_ref[...]
        for h in range(4):
            sl = slice(h * 128, (h + 1) * 128)
            mx_ref[:, 512 + h * 128:512 + (h + 1) * 128] = outb_head(
                of_ref[:, sl] + ob_ref[:, sl], pg_ref[:, sl], gg_ref[h]).astype(MX)
        x1_ref[...] = x_ref[...] + jnp.dot(mx_ref[...], w_ref[...], preferred_element_type=F32)

    return _call(body, "mix_out", (T // tm,),
                 [_rows(tm, D), _rows(tm, 512), _rows(tm, 512), _rows(tm, 512, 4), _rows(tm, 512), _full(gg), _full(w_out)],
                 [_rows(tm, D), _rows(tm, 1024)], [_sds((T, D)), _sds((T, 1024), MX)])(x, of, ob, p, outa, gg, w_out)


HALO = 16


def _halo_specs(T, tm, w):
    nb = T // HALO
    r = tm // HALO
    return [pl.BlockSpec((tm, w), lambda i: (i, 0)),
            pl.BlockSpec((HALO, w), lambda i: (jnp.maximum(i * r - 1, 0), 0)),
            pl.BlockSpec((HALO, w), lambda i: (jnp.minimum((i + 1) * r, nb - 1), 0))]


def _shifted(main, prev, nxt, i, nsteps):
    tm = main.shape[0]
    row = lax.broadcasted_iota(jnp.int32, (tm, 1), 0)
    pr = jnp.where(i > 0, prev[HALO - 1:HALO, :].astype(F32), 0.0)
    nx = jnp.where(i < nsteps - 1, nxt[0:1, :].astype(F32), 0.0)
    dn = jnp.where(row == 0, pr, pltpu.roll(main, 1, 0))
    up = jnp.where(row == tm - 1, nx, pltpu.roll(main, tm - 1, 0))
    return dn, up


def ffn_down(x1, zu, cw, cb, w_down, tm):
    T = x1.shape[0]
    ns = T // tm

    def body(x_ref, zu_ref, zp_ref, zn_ref, cw_ref, cb_ref, w_ref, x2_ref, z_ref, a_ref):
        zu = zu_ref[...].astype(F32)
        dn, up = _shifted(zu, zp_ref[...], zn_ref[...], pl.program_id(0), ns)
        z = cb_ref[...] + dn * cw_ref[0:1, :] + zu * cw_ref[1:2, :] + up * cw_ref[2:3, :]
        z_ref[...] = z.astype(MX)
        a = ffn_act(z[:, :DFF], z[:, DFF:]).astype(MX)
        a_ref[...] = a
        x2_ref[...] = x_ref[...] + jnp.dot(a, w_ref[...], preferred_element_type=F32)

    return _call(body, "ffn_down", (ns,), [_rows(tm, D)] + _halo_specs(T, tm, NUP) + [_full(cw), _full(cb), _full(w_down)],
                 [_rows(tm, D), _rows(tm, NUP), _rows(tm, DFF)],
                 [_sds((T, D)), _sds((T, NUP), MX), _sds((T, DFF), MX)])(x1, zu, zu, zu, cw, cb, w_down)


def loss_head(x, g, tgt, tm):
    T = x.shape[0]

    def body(x_ref, g_ref, t_ref, l_ref, dx_ref, dg_ref):
        @pl.when(pl.program_id(0) == 0)
        def _():
            l_ref[...] = jnp.zeros_like(l_ref)
            dg_ref[...] = jnp.zeros_like(dg_ref)

        y, vjp = jax.vjp(rmsnorm, x_ref[...], g_ref[...])
        err = y - t_ref[...]
        l_ref[...] += jnp.sum(err * err, axis=0, keepdims=True)
        dx, dg = vjp(err * (1.0 / D))
        dx_ref[...] = dx
        dg_ref[...] += dg

    return _call(body, "loss_head", (T // tm,), [_rows(tm, D), _full(g), _rows(tm, D)],
                 [_full(g), _rows(tm, D), _full(g)], [_sds((1, D)), _sds((T, D)), _sds((1, D))])(x, g, tgt)


def ffn_down_bwd(dx2, z, w_down, tm):
    T = dx2.shape[0]

    def body(dx_ref, z_ref, w_ref, dz_ref):
        da = _dg(dx_ref[...], w_ref[...], 1, 1)
        zg, zv = z_ref[:, :DFF].astype(F32), z_ref[:, DFF:].astype(F32)
        s = sigmoid(zg)
        sz = zg * s
        dz_ref[:, :DFF] = (da * zv * (s + sz * (1.0 - s))).astype(MX)
        dz_ref[:, DFF:] = (da * sz).astype(MX)

    return _call(body, "ffn_down_bwd", (T // tm,), [_rows(tm, D), _rows(tm, NUP), _full(w_down)],
                 _rows(tm, NUP), _sds((T, NUP), MX))(dx2, z, w_down)


def ffn_conv_bwd(dz, zu, cw, tm):
    T = dz.shape[0]
    ns = T // tm

    def body(dz_ref, dp_ref, dn_ref, zu_ref, cw_ref, dzu_ref, dcw_ref, dcb_ref):
        i = pl.program_id(0)

        @pl.when(i == 0)
        def _():
            dcw_ref[...] = jnp.zeros_like(dcw_ref)
            dcb_ref[...] = jnp.zeros_like(dcb_ref)

        dzb = dz_ref[...]
        dz = dzb.astype(F32)
        zu = zu_ref[...].astype(F32)
        r = lax.broadcasted_iota(jnp.int32, (tm, tm), 0)
        c = lax.broadcasted_iota(jnp.int32, (tm, tm), 1)
        row = lax.broadcasted_iota(jnp.int32, (tm, 1), 0)
        pr = jnp.where(i > 0, dp_ref[HALO - 1:HALO, :].astype(F32), 0.0)
        nx = jnp.where(i < ns - 1, dn_ref[0:1, :].astype(F32), 0.0)
        ddn = jnp.where(row == 0, pr, _dg(jnp.where(c == r - 1, 1.0, 0.0), dzb, 1, 0))
        dup = jnp.where(row == tm - 1, nx, _dg(jnp.where(c == r + 1, 1.0, 0.0), dzb, 1, 0))
        dzu_ref[...] = (dup * cw_ref[0:1, :] + dz * cw_ref[1:2, :] + ddn * cw_ref[2:3, :]).astype(MX)
        dcw_ref[0:1, :] += jnp.sum(zu * dup, axis=0, keepdims=True)
        dcw_ref[1:2, :] += jnp.sum(zu * dz, axis=0, keepdims=True)
        dcw_ref[2:3, :] += jnp.sum(zu * ddn, axis=0, keepdims=True)
        dcb_ref[...] += jnp.sum(dz, axis=0, keepdims=True)

    return _call(body, "ffn_conv_bwd", (ns,), _halo_specs(T, tm, NUP) + [_rows(tm, NUP), _full(cw)],
                 [_rows(tm, NUP), _full(cw), pl.BlockSpec((1, NUP), lambda i: (0, 0))],
                 [_sds((T, NUP), MX), _sds((3, NUP)), _sds((1, NUP))])(dz, dz, dz, zu, cw)


def nt_normbwd(dys, w, x, g, dres, tm, name):
    T = x.shape[0]
    n = len(dys)
    offs = [sum(d.shape[1] for d in dys[:i]) for i in range(n + 1)]

    def body(*refs):
        dy_refs, (w_ref, x_ref, g_ref, dr_ref, dx_ref, dg_ref) = refs[:n], refs[n:]

        @pl.when(pl.program_id(0) == 0)
        def _():
            dg_ref[...] = jnp.zeros_like(dg_ref)

        dh = _dg(dy_refs[0][...], w_ref[:, offs[0]:offs[1]], 1, 1)
        for i in range(1, n):
            dh = dh + _dg(dy_refs[i][...], w_ref[:, offs[i]:offs[i + 1]], 1, 1)
        _, vjp = jax.vjp(rmsnorm, x_ref[...], g_ref[...])
        dx, dg = vjp(dh)
        dx_ref[...] = dr_ref[...] + dx
        dg_ref[...] += dg

    return _call(body, name, (T // tm,),
                 [_rows(tm, d.shape[1]) for d in dys] + [_full(w), _rows(tm, D), _full(g), _rows(tm, D)],
                 [_rows(tm, D), _full(g)], [_sds((T, D)), _sds((1, D))])(*dys, w, x, g, dres)


def matmul_tn(a, b, tt, tn, name):
    T, k = a.shape
    n = b.shape[1]
    last = T // tt - 1

    def body(a_ref, b_ref, o_ref, acc_ref):
        @pl.when(pl.program_id(1) == 0)
        def _():
            acc_ref[...] = jnp.zeros_like(acc_ref)

        acc_ref[...] += _dg(a_ref[...], b_ref[...], 0, 0)

        @pl.when(pl.program_id(1) == last)
        def _():
            o_ref[...] = acc_ref[...].astype(MX)

    return _call(body, name, (n // tn, T // tt),
                 [pl.BlockSpec((tt, k), lambda j, i: (i, 0)), pl.BlockSpec((tt, tn), lambda j, i: (i, j))],
                 pl.BlockSpec((k, tn), lambda j, i: (0, j)), _sds((k, n), MX), scratch=[pltpu.VMEM((k, tn), F32)],
                 n_axes=2)(a, b)


def mix_out_bwd(dx1, w_out, of, ob, p, gg, tm):
    T = dx1.shape[0]

    def body(dx_ref, w_ref, of_ref, ob_ref, pg_ref, gg_ref, da_ref, do_ref, dpg_ref, dgg_ref):
        @pl.when(pl.program_id(0) == 0)
        def _():
            dgg_ref[...] = jnp.zeros_like(dgg_ref)

        dxb = dx_ref[...].astype(MX)
        da_ref[...] = _dg(dxb, w_ref[0:512, :], 1, 1)
        for h in range(4):
            sl = slice(h * 128, (h + 1) * 128)
            dm = _dg(dxb, w_ref[512 + h * 128:512 + (h + 1) * 128, :], 1, 1)
            _, vjp = jax.vjp(outb_head, of_ref[:, sl] + ob_ref[:, sl], pg_ref[:, sl], gg_ref[h])
            do, dpg, dg = vjp(dm)
            do_ref[:, sl] = do
            dpg_ref[:, sl] = dpg
            dgg_ref[h] += dg

    return _call(body, "mix_out_bwd", (T // tm,),
                 [_rows(tm, D), _full(w_out), _rows(tm, 512), _rows(tm, 512), _rows(tm, 512, 4), _full(gg)],
                 [_rows(tm, 512), _rows(tm, 512), _rows(tm, 512), _full(gg)],
                 [_sds((T, 512)), _sds((T, 512)), _sds((T, 512)), _sds(gg.shape)])(dx1, w_out, of, ob, p, gg)


def gla_bwd(p, wg, bg, ss, do, reverse, merge=None):
    T = p.shape[0]
    tm = CPB * CH
    n = T // tm
    rev = not reverse
    rn = n if rev else None

    def body(*refs):
        q_ref, k_ref, v_ref, r_ref, wg_ref, bg_ref, ss_ref, do_ref = refs[:8]
        if merge is None:
            dq_ref, dk_ref, dv_ref, dr_ref, dwg_ref, dbg_ref, dst_ref = refs[8:]
        else:
            mq_ref, mk_ref, mv_ref, mr_ref, mg_ref, out_ref, dwg_ref, dbg_ref, dst_ref, drs_ref = refs[8:]
            out_ref[:, 1024:1536] = mg_ref[...].astype(MX)

        @pl.when(pl.program_id(0) == 0)
        def _():
            dst_ref[...] = jnp.zeros_like(dst_ref)
            dwg_ref[...] = jnp.zeros_like(dwg_ref)
            dbg_ref[...] = jnp.zeros_like(dbg_ref)

        consts = _tri(reverse) + _lane_masks()
        order = list(reversed(range(CPB))) if reverse else list(range(CPB))
        for j in range(2):
            sl = slice(j * 128, (j + 1) * 128)
            v0s, v1s = slice(256 * j, 256 * j + 128), slice(256 * j + 128, 256 * j + 256)
            chunks, dout = [], []
            for c in order:
                rows = _chunk(c)
                chunks += [r_ref[rows, :], q_ref[rows, sl], k_ref[rows, sl], v_ref[rows, v0s], v_ref[rows, v1s]]
                dout += [do_ref[rows, v0s], do_ref[rows, v1s]]
            _, vjp = jax.vjp(functools.partial(gla_pair, consts), wg_ref[:, sl], bg_ref[:, sl],
                             ss_ref[0, 2 * j], ss_ref[0, 2 * j + 1], *chunks)
            g = vjp((*dout, dst_ref[2 * j], dst_ref[2 * j + 1]))
            dwg_ref[:, sl] += g[0]
            dbg_ref[:, sl] += g[1]
            dst_ref[2 * j] = g[2]
            dst_ref[2 * j + 1] = g[3]
            for i, c in enumerate(order):
                rows = _chunk(c)
                dr, dq, dk, dv0, dv1 = g[4 + 5 * i:9 + 5 * i]
                if merge is None:
                    if j == 0:
                        dr_ref[rows, :] = dr
                    else:
                        dr_ref[rows, :] += dr
                    dq_ref[rows, sl] = dq
                    dk_ref[rows, sl] = dk
                    dv_ref[rows, v0s] = dv0
                    dv_ref[rows, v1s] = dv1
                else:
                    if j == 0:
                        drs_ref[rows, :] = mr_ref[rows, :] + dr
                    else:
                        out_ref[rows, 1536:1664] = (drs_ref[rows, :] + dr).astype(MX)
                    out_ref[rows, sl] = (mq_ref[rows, sl] + dq).astype(MX)
                    out_ref[rows, 256 + 128 * j:384 + 128 * j] = (mk_ref[rows, sl] + dk).astype(MX)
                    out_ref[rows, 512 + 256 * j:640 + 256 * j] = (mv_ref[rows, v0s] + dv0).astype(MX)
                    out_ref[rows, 640 + 256 * j:768 + 256 * j] = (mv_ref[rows, v1s] + dv1).astype(MX)

    ss_spec = pl.BlockSpec((1, 4, 128, 128), (lambda i: (n - 1 - i, 0, 0, 0)) if rev else (lambda i: (i, 0, 0, 0)))
    ins = [p, p, p, p, wg, bg, ss, do]
    in_specs = _gla_in_specs(tm, n, rev) + [_full(wg), _full(bg), ss_spec, _rows(tm, 512, 0, rn)]
    scratch = [pltpu.VMEM((4, 128, 128), F32)]
    if merge is None:
        out_specs = [_rows(tm, 256, 0, rn), _rows(tm, 256, 0, rn), _rows(tm, 512, 0, rn), _rows(tm, 128, 0, rn)]
        out_shape = [_sds((T, 256)), _sds((T, 256)), _sds((T, 512)), _sds((T, 128))]
    else:
        ins += list(merge)
        in_specs += [_rows(tm, a.shape[1], 0, rn) for a in merge]
        out_specs, out_shape = [_rows(tm, 1664, 0, rn)], [_sds((T, 1664), MX)]
        scratch.append(pltpu.VMEM((tm, 128), F32))
    return _call(body, "gla_bwd_r" if reverse else "gla_bwd_f", (n,), in_specs, out_specs + [_full(wg), _full(bg)],
                 out_shape + [_sds(wg.shape), _sds(bg.shape)], scratch=scratch)(*ins)


def gmlp_bwd(p, douta, ws, bs, lg, lb):
    T = p.shape[0]
    cpb = 1
    tm = cpb * CH

    def body(pa_ref, do_ref, ws_ref, bs_ref, lg_ref, lb_ref, dpa_ref, dws_ref, dbs_ref, dlg_ref, dlb_ref):
        @pl.when(pl.program_id(0) == 0)
        def _():
            for r in (dws_ref, dbs_ref, dlg_ref, dlb_ref):
                r[...] = jnp.zeros_like(r)

        for c in range(cpb):
            rows = _chunk(c)
            for h in range(4):
                us, vs = slice(h * 128, (h + 1) * 128), slice(512 + h * 128, 512 + (h + 1) * 128)
                _, vjp = jax.vjp(gmlp_head, pa_ref[rows, us], pa_ref[rows, vs], ws_ref[h], bs_ref[h], lg_ref[h], lb_ref[h])
                du, dv, *dparams = vjp(do_ref[rows, us])
                dpa_ref[rows, us] = du.astype(MX)
                dpa_ref[rows, vs] = dv.astype(MX)
                for r, a in zip((dws_ref, dbs_ref, dlg_ref, dlb_ref), dparams):
                    r[h] += a

    return _call(body, "gmlp_bwd", (T // tm,),
                 [_rows(tm, 1024), _rows(tm, 512), _full(ws), _full(bs), _full(lg), _full(lb)],
                 [_rows(tm, 1024), _full(ws), _full(bs), _full(lg), _full(lb)],
                 [_sds((T, 1024), MX), _sds(ws.shape), _sds(bs.shape), _sds(lg.shape), _sds(lb.shape)])(p, douta, ws, bs, lg, lb)


def _gate_pad(w, row0):
    return jnp.zeros((128, 256), F32).at[row0:row0 + 16].set(w)


def local_step(x, tgt, W, get_big, emit, tm=256, tmm=512):
    saved = []
    for l in range(NL):
        s = {"x": x}
        s.update(get_big(l, "in", x))
        p, s["h"] = norm_matmul(x, W["g_mix"][l][None], s["w_in"], tmm, "mix_in")
        s["p"] = p
        ws, bs = W["w_s"][l], W["b_s"][l][:, :, None]
        lg, lb = W["ln_g"][l][:, None, :], W["ln_b"][l][:, None, :]
        outa = gmlp_fwd(p, ws, bs, lg, lb)
        wgf, wgb = _gate_pad(W["w_gate_f"][l], 0), _gate_pad(W["w_gate_b"][l], 16)
        bgf, bgb = W["b_gate_f"][l][None], W["b_gate_b"][l][None]
        s["of"], s["ssf"] = gla_fwd(p, wgf, bgf, False)
        s["ob"], s["ssb"] = gla_fwd(p, wgb, bgb, True)
        s.update(get_big(l, "rest", s["ob"]))
        gg = W["g_gla"][l][:, None, :]
        x1, s["mixed"] = mix_out(x, s["of"], s["ob"], p, outa, gg, s["w_out"], tmm)
        s["x1"] = x1
        s["zu"], s["h2"] = norm_matmul(x1, W["g_ffn"][l][None], s["w_up"], tmm, "ffn_up", MX)
        x, s["z"], s["a"] = ffn_down(x1, s["zu"], W["conv_w"][l], W["conv_b"][l][None], s["w_down"], tm)
        saved.append(s)

    lsum, dx, dgf = loss_head(x, W["g_final"][None], tgt, tmm)
    G = {k: [None] * NL for k in _SMALL if k != "g_final"}
    tok = jnp.zeros((1, 1), F32)
    for l in reversed(range(NL)):
        s = saved[l]
        g_down = matmul_tn(s["a"], dx, min(1024, tmm * 2), 512, "dw_down")
        dz = ffn_down_bwd(dx, s["z"], s["w_down"], tm)
        dzu, G["conv_w"][l], dcb = ffn_conv_bwd(dz, s["zu"], W["conv_w"][l] + tok, tm)
        G["conv_b"][l] = dcb[0]
        g_up = matmul_tn(s["h2"], dzu, min(1024, tmm * 2), 1408, "dw_up")
        tok = emit(l, "A", {"w_down": g_down, "w_up": g_up})
        dx1, dg = nt_normbwd([dzu], s["w_up"], s["x1"], W["g_ffn"][l][None] + tok, dx, tmm, "ffn_up_bwd")
        G["g_ffn"][l] = dg[0]
        g_out = matmul_tn(s["mixed"], dx1, min(1024, tmm * 2), 1024, "dw_out")
        gg = W["g_gla"][l][:, None, :]
        douta, do, dpg, dgg = mix_out_bwd(dx1, s["w_out"], s["of"], s["ob"], s["p"], gg, tmm)
        G["g_gla"][l] = dgg[:, 0, :]
        wgf, wgb = _gate_pad(W["w_gate_f"][l], 0), _gate_pad(W["w_gate_b"][l], 16)
        bgf, bgb = W["b_gate_f"][l][None], W["b_gate_b"][l][None]
        dqf, dkf, dvf, drf, dwgf, dbgf = gla_bwd(s["p"], wgf, bgf, s["ssf"], do, False)
        dpb, dwgb, dbgb = gla_bwd(s["p"], wgb, bgb, s["ssb"], do, True, merge=(dqf, dkf, dvf, drf, dpg))
        G["w_gate_f"][l], G["b_gate_f"][l] = dwgf[0:16], dbgf[0]
        G["w_gate_b"][l], G["b_gate_b"][l] = dwgb[16:32], dbgb[0]
        ws, bs = W["w_s"][l], W["b_s"][l][:, :, None]
        lg, lb = W["ln_g"][l][:, None, :], W["ln_b"][l][:, None, :]
        dpa, G["w_s"][l], dbs, dlg, dlb = gmlp_bwd(s["p"], douta, ws, bs, lg, lb)
        G["b_s"][l], G["ln_g"][l], G["ln_b"][l] = dbs[:, :, 0], dlg[:, 0, :], dlb[:, 0, :]
        tt = min(1024, tmm * 2)
        g_in = jnp.concatenate([matmul_tn(s["h"], dpa, tt, 1024, "dw_in_a"),
                                matmul_tn(s["h"], dpb, tt, 1664, "dw_in_b")], axis=1)
        tok = emit(l, "B", {"w_out": g_out, "w_in": g_in})
        dx, dg = nt_normbwd([dpa, dpb], s["w_in"], s["x"], W["g_mix"][l][None] + tok, dx1, tmm, "mix_in_bwd")
        G["g_mix"][l] = dg[0]
    G = {k: jnp.stack(v) for k, v in G.items()}
    G["g_final"] = dgf[0]
    return lsum, dx, G


def cast_bf16(a, tr):
    r, c = a.shape

    def body(a_ref, o_ref):
        o_ref[...] = a_ref[...].astype(BF16)

    return _call(body, "cast_bf16", (r // tr,), [_rows(tr, c)], _rows(tr, c), _sds((r, c), BF16))(a)


def sum_lead(y, tr):
    n, rr, cc = y.shape

    def body(y_ref, o_ref):
        acc = y_ref[0].astype(F32)
        for k in range(1, n):
            acc = acc + y_ref[k].astype(F32)
        o_ref[...] = acc

    return _call(body, "sum_lead", (rr // tr,), [pl.BlockSpec((n, tr, cc), lambda i: (0, i, 0))],
                 _rows(tr, cc), _sds((rr, cc)))(y)


def sum_parts(land, grad, k, chipvec, tr):
    _, rr, cc = land.shape
    nb = rr // tr
    if k == 0:
        own = pl.BlockSpec((None, tr, cc), lambda i, c: (c[0], i, 0))
    elif k == 2:
        own = pl.BlockSpec((tr, cc), lambda i, c: (i, c[0]))
    else:
        own = pl.BlockSpec((tr, cc), lambda i, c: (c[0] * nb + i, 0))

    def body(c_ref, l_ref, g_ref, o_ref):
        mine = g_ref[...].astype(F32)
        acc = None
        for j in range(4):
            part = jnp.where(c_ref[0] == j, mine, l_ref[j].astype(F32))
            acc = part if acc is None else acc + part
        o_ref[...] = acc

    gs = pltpu.PrefetchScalarGridSpec(
        num_scalar_prefetch=1, grid=(nb,),
        in_specs=[pl.BlockSpec((4, tr, cc), lambda i, c: (0, i, 0)), own],
        out_specs=pl.BlockSpec((tr, cc), lambda i, c: (i, 0)))
    return pl.pallas_call(body, name="sum_parts", grid_spec=gs, out_shape=_sds((rr, cc)),
                          compiler_params=_cparams(1))(chipvec, land, grad)


def adamw(w, ga, gb, m, v, tr):
    r, c = w.shape

    def body(w_ref, ga_ref, gb_ref, m_ref, v_ref, g_ref, d_ref, nm_ref, nv_ref):
        gr = ga_ref[...] + gb_ref[...]
        g_ref[...] = gr
        nm = ADAM_B1 * m_ref[...] + (1.0 - ADAM_B1) * gr
        nv = ADAM_B2 * v_ref[...] + (1.0 - ADAM_B2) * jnp.square(gr)
        m_hat = nm / (1.0 - ADAM_B1 ** ADAM_STEP)
        v_hat = nv / (1.0 - ADAM_B2 ** ADAM_STEP)
        d_ref[...] = -ADAM_LR * (m_hat / (jnp.sqrt(v_hat) + ADAM_EPS) + ADAM_WD * w_ref[...])
        nm_ref[...] = nm
        nv_ref[...] = nv

    sp = _rows(tr, c)
    return _call(body, "adamw", (r // tr,), [sp] * 5, [sp] * 4, [_sds((r, c))] * 4)(w, ga, gb, m, v)


MESH = pl.DeviceIdType.MESH
ANY = pl.BlockSpec(memory_space=pl.ANY)
N_BIG = 4


def _pos():
    return lax.axis_index("x"), lax.axis_index("y"), lax.axis_index("c")


def _other_chips(x, y):
    return [(1 - x, y), (x, 1 - y), (1 - x, 1 - y)]


def _rcopy(src, dst, send_sems, recv_sems, k, to):
    return pltpu.make_async_remote_copy(src_ref=src, dst_ref=dst, send_sem=send_sems.at[k], recv_sem=recv_sems.at[k],
                                        device_id=to, device_id_type=MESH)


def allgather8(xs):
    m, n = xs.shape

    def body(x_ref, out_ref, send_sems, recv_sems, local_sem):
        x, y, c = _pos()
        me, sibling = (x, y, c), (x, y, 1 - c)
        chips = _other_chips(x, y)

        def rows(px, py, pc):
            return out_ref.at[pl.ds((4 * px + 2 * py + pc) * m, m), :]

        def copy(k, block, to, src=None):
            return _rcopy(rows(*block) if src is None else src, rows(*block), send_sems, recv_sems, k, to)

        mine = pltpu.make_async_copy(x_ref, rows(*me), local_sem)
        mine.start()
        first = [copy(0, me, sibling, src=x_ref)]
        first += [copy(1 + j, me, (*chip, c), src=x_ref) for j, chip in enumerate(chips)]
        for cp in first:
            cp.start()
        passed = [copy(4 + j, (*chip, c), sibling) for j, chip in enumerate(chips)]
        for j, chip in enumerate(chips):
            copy(1 + j, (*chip, c), me).wait_recv()
            passed[j].start()
        copy(0, sibling, me).wait_recv()
        for j, chip in enumerate(chips):
            copy(4 + j, (*chip, 1 - c), me).wait_recv()
        for cp in first + passed:
            cp.wait_send()
        mine.wait()

    vm = pl.BlockSpec(memory_space=pltpu.VMEM)
    return pl.pallas_call(
        body, name="allgather8", out_shape=_sds((8 * m, n), xs.dtype), in_specs=[vm], out_specs=vm,
        scratch_shapes=[pltpu.SemaphoreType.DMA((7,)), pltpu.SemaphoreType.DMA((7,)), pltpu.SemaphoreType.DMA],
        compiler_params=pltpu.CompilerParams(vmem_limit_bytes=VMEM_LIMIT))(xs)


def _slab(k, ref, j):
    if k == 0:
        return ref.at[j]
    if k == 1:
        return ref.at[pl.ds(256 * j, 256), :]
    if k == 2:
        return ref.at[:, pl.ds(1408 * j, 1408)]
    return ref.at[pl.ds(704 * j, 704), :]


_LAYER_FULL = [(4, 1024, 648), (1024, 1024), (1024, NUP), (DFF, 1024)]
_LAYER_SHARD = [(1024, 648), (256, 1024), (1024, 1408), (704, 1024)]
_SHARD_SHAPES = [(NL,) + s for s in _LAYER_SHARD]

HBM = pl.BlockSpec(memory_space=pltpu.HBM)
SEM = pl.BlockSpec(memory_space=pltpu.SEMAPHORE)
VM = pl.BlockSpec(memory_space=pltpu.VMEM)
EFFECT = pltpu.SideEffectType.DATAFLOW_SIDE_EFFECTING
_GW_GROUPS = [[(0, 0)], [(0, 1), (0, 2), (0, 3)]] + [[(l, k) for k in range(N_BIG)] for l in range(1, NL)]
_GW_ORDER = [lk for g in _GW_GROUPS for lk in g]


def _hbm(a):
    return pltpu.with_memory_space_constraint(a, pltpu.HBM)


def _hbm_like(a):
    return pltpu.HBM(a.shape, a.dtype)


def place_own(shards, landings):
    n = len(_GW_ORDER)

    def body(*refs):
        S, out, sems = refs[:N_BIG], refs[N_BIG + n:N_BIG + 2 * n], refs[-1]
        x, y, _ = _pos()
        cps = [pltpu.make_async_copy(S[k].at[l], _slab(k, out[i], 2 * x + y), sems.at[i])
               for i, (l, k) in enumerate(_GW_ORDER)]
        for cp in cps:
            cp.start()
        for cp in cps:
            cp.wait()

    return pl.pallas_call(
        body, name="place_own", out_shape=[_sds(a.shape, a.dtype) for a in landings],
        in_specs=[ANY] * (N_BIG + n), out_specs=[ANY] * n, input_output_aliases={N_BIG + i: i for i in range(n)},
        scratch_shapes=[pltpu.SemaphoreType.DMA((n,))])(*shards, *landings)


def gw_start(shards, landings):
    n = len(_GW_ORDER)

    def body(*refs):
        S, Ld = refs[:N_BIG], refs[N_BIG:N_BIG + n]
        outs = refs[N_BIG + n:]
        send_sems, recv, token = outs[0], outs[1:1 + len(_GW_GROUPS)], outs[-1]
        x, y, c = _pos()
        me = 2 * x + y
        ci = 0
        for gi, grp in enumerate(_GW_GROUPS):
            for t, (l, k) in enumerate(grp):
                land = Ld[_GW_ORDER.index((l, k))]
                for j, (px, py) in enumerate(_other_chips(x, y)):
                    pltpu.make_async_remote_copy(
                        src_ref=S[k].at[l], dst_ref=_slab(k, land, me), send_sem=send_sems.at[ci],
                        recv_sem=recv[gi].at[3 * t + j], device_id=(px, py, c), device_id_type=MESH).start()
                    ci += 1
        token[...] = jnp.zeros_like(token)

    ins = list(shards) + list(landings)
    sems = [pltpu.SemaphoreType.DMA((3 * n,))] + [pltpu.SemaphoreType.DMA((3 * len(g),)) for g in _GW_GROUPS]
    outs = pl.pallas_call(
        body, name="gw_start", out_shape=sems + [_hbm_like(a) for a in ins] + [_sds((8, 128))],
        in_specs=[HBM] * len(ins), out_specs=[SEM] * len(sems) + [HBM] * len(ins) + [VM],
        input_output_aliases={i: len(sems) + i for i in range(len(ins))},
        compiler_params=pltpu.CompilerParams(has_side_effects=EFFECT))(*[_hbm(a) for a in ins])
    ns = len(sems)
    return outs[0], outs[1:ns], outs[ns:ns + N_BIG], outs[ns + N_BIG:ns + len(ins)], outs[-1]


def gw_wait(gi, landings, recv_sems, after, shards=None, send_sems=None):
    grp = _GW_GROUPS[gi]
    n = len(grp)
    last = shards is not None

    def body(*refs):
        Ld, rs = refs[:n], refs[n]
        x, y, c = _pos()
        for t, (l, k) in enumerate(grp):
            for j, (px, py) in enumerate(_other_chips(x, y)):
                region = _slab(k, Ld[t], 2 * px + py)
                pltpu.make_async_remote_copy(src_ref=region, dst_ref=region, send_sem=rs.at[3 * t + j],
                                             recv_sem=rs.at[3 * t + j], device_id=(px, py, c),
                                             device_id_type=MESH).wait_recv()
        if last:
            S, ss = refs[n + 2:n + 2 + N_BIG], refs[n + 2 + N_BIG]
            me = 2 * x + y
            for ci, (l, k) in enumerate(lk for lk in _GW_ORDER for _ in range(3)):
                pltpu.make_async_remote_copy(src_ref=S[k].at[l], dst_ref=_slab(k, Ld[k], me), send_sem=ss.at[ci],
                                             recv_sem=ss.at[ci], device_id=(x, y, c), device_id_type=MESH).wait_send()

    ins = list(landings) + [recv_sems, after]
    specs = [HBM] * n + [SEM, pl.BlockSpec(memory_space=pl.ANY)]
    outs = [_hbm_like(a) for a in landings]
    alias = {i: i for i in range(n)}
    if last:
        ins += list(shards) + [send_sems]
        specs += [HBM] * N_BIG + [SEM]
        outs += [_hbm_like(a) for a in shards]
        alias.update({n + 2 + i: n + i for i in range(N_BIG)})
    res = pl.pallas_call(body, name="gw_wait_%d" % gi, out_shape=outs, in_specs=specs, out_specs=[HBM] * len(outs),
                         input_output_aliases=alias,
                         compiler_params=pltpu.CompilerParams(has_side_effects=EFFECT))(*ins)
    return res[:n]


def ga_start(tag, ks, grads, landings):
    n = len(ks)

    def body(*refs):
        G, Ld = refs[:n], refs[n:2 * n]
        send_sems, recv_sems, token = refs[2 * n], refs[2 * n + 1], refs[-1]
        x, y, c = _pos()
        me = 2 * x + y
        for t, k in enumerate(ks):
            for j, (px, py) in enumerate(_other_chips(x, y)):
                pltpu.make_async_remote_copy(
                    src_ref=_slab(k, G[t], 2 * px + py), dst_ref=Ld[t].at[me], send_sem=send_sems.at[3 * t + j],
                    recv_sem=recv_sems.at[3 * t + j], device_id=(px, py, c), device_id_type=MESH).start()
        token[...] = jnp.zeros_like(token)

    ins = list(grads) + list(landings)
    sems = [pltpu.SemaphoreType.DMA((3 * n,))] * 2
    outs = pl.pallas_call(
        body, name="ga_start_" + tag, out_shape=sems + [_hbm_like(a) for a in ins] + [_sds((8, 128))],
        in_specs=[HBM] * len(ins), out_specs=[SEM, SEM] + [HBM] * len(ins) + [VM],
        input_output_aliases={i: 2 + i for i in range(len(ins))},
        compiler_params=pltpu.CompilerParams(has_side_effects=EFFECT))(*[_hbm(a) for a in ins])
    return outs[0], outs[1], outs[2:2 + n], outs[2 + n:2 + 2 * n], outs[-1]


def ga_wait(tag, ks, send_sems, recv_sems, grads, landings, after):
    n = len(ks)

    def body(*refs):
        G, Ld, ss, rs = refs[:n], refs[n:2 * n], refs[2 * n], refs[2 * n + 1]
        x, y, c = _pos()
        me = 2 * x + y
        for t, k in enumerate(ks):
            for j, (px, py) in enumerate(_other_chips(x, y)):
                pj = 2 * px + py
                cp = pltpu.make_async_remote_copy(
                    src_ref=_slab(k, G[t], pj), dst_ref=Ld[t].at[pj], send_sem=ss.at[3 * t + j],
                    recv_sem=rs.at[3 * t + j], device_id=(px, py, c), device_id_type=MESH)
                cp.wait_send()
                cp.wait_recv()

    ins = list(grads) + list(landings) + [send_sems, recv_sems, after]
    res = pl.pallas_call(
        body, name="ga_wait_" + tag, out_shape=[_hbm_like(a) for a in list(grads) + list(landings)],
        in_specs=[HBM] * (2 * n) + [SEM, SEM, pl.BlockSpec(memory_space=pl.ANY)], out_specs=[HBM] * (2 * n),
        input_output_aliases={i: i for i in range(2 * n)},
        compiler_params=pltpu.CompilerParams(has_side_effects=EFFECT))(*ins)
    return res[:n], res[n:]


def swap4(parts):
    def body(*refs):
        Q, R = refs[:N_BIG], refs[N_BIG:2 * N_BIG]
        send_sems, recv_sems = refs[2 * N_BIG:]
        x, y, c = _pos()
        cps = [_rcopy(Q[k], R[k], send_sems, recv_sems, k, (x, y, 1 - c)) for k in range(N_BIG)]
        for cp in cps:
            cp.start()
        for cp in cps:
            cp.wait()

    return pl.pallas_call(
        body, name="swap4", out_shape=[_sds(s) for s in _SHARD_SHAPES],
        in_specs=[ANY] * N_BIG, out_specs=[ANY] * N_BIG,
        scratch_shapes=[pltpu.SemaphoreType.DMA((N_BIG,)), pltpu.SemaphoreType.DMA((N_BIG,))])(*parts)


_WEIGHTS = ['g_mix', 'w_in', 'w_s', 'b_s', 'ln_g', 'ln_b', 'w_gate_f', 'b_gate_f', 'w_gate_b', 'b_gate_b', 'g_gla',
            'w_out', 'g_ffn', 'w_up', 'conv_w', 'conv_b', 'w_down', 'g_final']
_BIG = ['w_in', 'w_out', 'w_up', 'w_down']
_SMALL = [n for n in _WEIGHTS if n not in _BIG]
_SMALL_SHARDED = {'w_gate_f': 64, 'w_gate_b': 64, 'conv_w': 1408}
_BIG_TR = {'w_in': 512, 'w_out': 256, 'w_up': 256, 'w_down': 352}


def _pack(arrs):
    flat = jnp.concatenate([a.reshape(-1) for a in arrs])
    pad = (-flat.shape[0]) % 1024
    return jnp.pad(flat, (0, pad)).reshape(-1, 128)


def _unpack(buf, shapes):
    flat = buf.reshape(-1)
    out, o = [], 0
    for s in shapes:
        n = 1
        for d in s:
            n *= d
        out.append(flat[o:o + n].reshape(s))
        o += n
    return out


def kernel(x, g_mix, w_in, w_s, b_s, ln_g, ln_b, w_gate_f, b_gate_f, w_gate_b, b_gate_b, g_gla, w_out, g_ffn, w_up, conv_w, conv_b, w_down, g_final, loss_target, m_g_mix, m_w_in, m_w_s, m_b_s, m_ln_g, m_ln_b, m_w_gate_f, m_b_gate_f, m_w_gate_b, m_b_gate_b, m_g_gla, m_w_out, m_g_ffn, m_w_up, m_conv_w, m_conv_b, m_w_down, m_g_final, v_g_mix, v_w_in, v_w_s, v_b_s, v_ln_g, v_ln_b, v_w_gate_f, v_b_gate_f, v_w_gate_b, v_b_gate_b, v_g_gla, v_w_out, v_g_ffn, v_w_up, v_conv_w, v_conv_b, v_w_down, v_g_final):
    loc = locals()
    w = {n: loc[n] for n in _WEIGHTS}
    m = {n: loc["m_" + n] for n in _WEIGHTS}
    v = {n: loc["v_" + n] for n in _WEIGHTS}
    xi, yi, _ = _pos()
    chip = 2 * xi + yi

    sh_names = list(_SMALL_SHARDED)
    g8 = allgather8(_pack([w[n] for n in sh_names]))
    rows = g8.shape[0] // 8
    per_chip = [_unpack(g8[2 * j * rows:(2 * j + 1) * rows], [w[n].shape for n in sh_names]) for j in range(4)]
    W = dict(w)
    for k, n in enumerate(sh_names):
        W[n] = jnp.concatenate([per_chip[j][k] for j in range(4)], axis=-1)

    shards = [cast_bf16(w[n].reshape(-1, w[n].shape[-1]), _BIG_TR[n]).reshape(w[n].shape) for n in _BIG]
    landings = place_own(shards, [lax.empty(_LAYER_FULL[k], BF16) for _, k in _GW_ORDER])
    send_sems, recv_sems, shards_fly, landings_fly, _ = gw_start(shards, landings)
    arrived = {}

    def get_big(l, stage, after):
        gi = {(0, "in"): 0, (0, "rest"): 1}.get((l, stage), l + 1 if stage == "in" else None)
        if gi is not None:
            lo = sum(len(g) for g in _GW_GROUPS[:gi])
            lands = landings_fly[lo:lo + len(_GW_GROUPS[gi])]
            if gi == len(_GW_GROUPS) - 1:
                full = gw_wait(gi, lands, recv_sems[gi], after, shards_fly, send_sems)
            else:
                full = gw_wait(gi, lands, recv_sems[gi], after)
            arrived.update(zip(_GW_GROUPS[gi], full))
        if stage == "in":
            f_in = jnp.transpose(arrived[(l, 0)], (1, 0, 2)).reshape(D, N_IN)
            return {"w_in": jnp.pad(f_in, ((0, 0), (0, N_INP - N_IN)))}
        return {"w_out": arrived[(l, 1)], "w_up": arrived[(l, 2)], "w_down": arrived[(l, 3)]}

    flying = []

    def emit(l, group, grads):
        ks = [3, 2] if group == "A" else [1, 0]
        gs = [grads[_BIG[k]] for k in ks]
        if group == "B":
            gs[1] = jnp.transpose(gs[1][:, :N_IN].reshape(D, 4, 648), (1, 0, 2))
        lands = [lax.empty((4,) + _LAYER_SHARD[k], BF16) for k in ks]
        tag = "%d%s" % (l, group)
        ss, rs, gs_fly, lands_fly, tok = ga_start(tag, ks, gs, lands)
        flying.append((tag, l, ks, ss, rs, gs_fly, lands_fly))
        return tok[0:1, 0:1]

    lsum, grad_x, G = local_step(x[0], loss_target[0], W, get_big, emit)

    plane = [[None] * NL for _ in range(N_BIG)]
    chipvec = jnp.reshape(chip, (1,)).astype(jnp.int32)
    for tag, l, ks, ss, rs, gs_fly, lands_fly in flying:
        for k, g, a in zip(ks, *ga_wait(tag, ks, ss, rs, gs_fly, lands_fly, grad_x)):
            plane[k][l] = sum_parts(a, g, k, chipvec, _BIG_TR[_BIG[k]])
    plane = [jnp.stack(p) for p in plane]
    other = swap4(plane)

    small_shapes = [G[n].shape for n in _SMALL] + [(D,)]
    pk = _pack([G[n] for n in _SMALL] + [lsum])
    srows = pk.shape[0]
    red = sum_lead(allgather8(pk).reshape(8, srows, 128), srows)
    small = dict(zip(_SMALL + ["lsum"], _unpack(red, small_shapes)))
    loss = 0.5 * jnp.sum(small.pop("lsum")) / D
    for n, wd in _SMALL_SHARDED.items():
        small[n] = lax.dynamic_slice_in_dim(small[n], chip * wd, wd, axis=small[n].ndim - 1)

    grads, delta, new_m, new_v = dict(small), {}, {}, {}
    two = lambda a: a.reshape(-1, a.shape[-1])
    for k, n in enumerate(_BIG):
        res = adamw(two(w[n]), two(plane[k]), two(other[k]), two(m[n]), two(v[n]), _BIG_TR[n])
        grads[n], delta[n], new_m[n], new_v[n] = (r.reshape(w[n].shape) for r in res)
    shapes = [w[n].shape for n in _SMALL]
    pw, pg, pm, pv = (_pack([t[n] for n in _SMALL]) for t in (w, grads, m, v))
    _, d_, m_, v_ = adamw(pw, pg, jnp.zeros_like(pg), pm, pv, pw.shape[0])
    for t, buf in ((delta, d_), (new_m, m_), (new_v, v_)):
        t.update(zip(_SMALL, _unpack(buf, shapes)))

    return (loss, grad_x[None], *[grads[n] for n in _WEIGHTS], *[delta[n] for n in _WEIGHTS],
            *[new_m[n] for n in _WEIGHTS], *[new_v[n] for n in _WEIGHTS])
```

```python
import functools

import jax
import jax.numpy as jnp
from jax import lax
from jax.experimental import pallas as pl
from jax.experimental.pallas import tpu as pltpu

F32 = jnp.float32
BF16 = jnp.bfloat16
MX = BF16

D = 1024
CH = 128
NL = 4
N_IN = 2592
N_INP = 2688
NUP = 5632
DFF = 2816
EPS = 1e-6
VMEM_LIMIT = 56 * 1024 * 1024

ADAM_LR, ADAM_B1, ADAM_B2, ADAM_EPS, ADAM_WD, ADAM_STEP = 0.001, 0.9, 0.999, 1e-08, 0.01, 10


def _dg(a, b, ca, cb):
    return lax.dot_general(a.astype(MX), b.astype(MX), (((ca,), (cb,)), ((), ())), preferred_element_type=F32)


@jax.custom_vjp
def mm(a, b):
    return _dg(a, b, 1, 0)


mm.defvjp(lambda a, b: (_dg(a, b, 1, 0), (a, b)),
          lambda r, g: (_dg(g, r[1], 1, 1), _dg(r[0], g, 0, 0)))


@jax.custom_vjp
def mm_nt(a, b):
    return _dg(a, b, 1, 1)


mm_nt.defvjp(lambda a, b: (_dg(a, b, 1, 1), (a, b)),
             lambda r, g: (_dg(g, r[1], 1, 0), _dg(g, r[0], 0, 0)))


@jax.custom_vjp
def mm_tn(a, b):
    return _dg(a, b, 0, 0)


mm_tn.defvjp(lambda a, b: (_dg(a, b, 0, 0), (a, b)),
             lambda r, g: (_dg(r[1], g, 1, 1), _dg(r[0], g, 1, 0)))


def _split3(x):
    hi = x.astype(BF16)
    r1 = x - hi.astype(F32)
    mid = r1.astype(BF16)
    lo = (r1 - mid.astype(F32)).astype(BF16)
    return hi, mid, lo


def _dot3(m, x):
    hi, mid, lo = _split3(x)
    d = lambda p: lax.dot_general(m, p, (((1,), (0,)), ((), ())), preferred_element_type=F32)
    return d(hi) + d(mid) + d(lo)


@jax.custom_vjp
def cumdot(m, mt, x):
    return _dot3(m, x)


cumdot.defvjp(lambda m, mt, x: (_dot3(m, x), (m, mt)),
              lambda r, g: (jnp.zeros_like(r[0]), jnp.zeros_like(r[1]), _dot3(r[1], g)))


def rmsnorm(x, g):
    return x * lax.rsqrt(jnp.mean(x * x, axis=-1, keepdims=True) + EPS) * g


def gelu(x):
    return 0.5 * x * (1.0 + lax.erf(x * 0.7071067811865476))


def sigmoid(x):
    return 1.0 / (1.0 + jnp.exp(-x))


def log_sigmoid(x):
    return jnp.minimum(x, 0.0) - jnp.log(1.0 + jnp.exp(-jnp.abs(x)))


def gmlp_heads(params, pieces):
    u = [[gelu(p[0]) for p in ch] for ch in pieces]
    v = [[gelu(p[1]) for p in ch] for ch in pieces]
    mu = [[jnp.mean(x, axis=-1, keepdims=True) for x in ch] for ch in v]
    var = [[jnp.mean(jnp.square(x - m), axis=-1, keepdims=True) for x, m in zip(cv, cm)] for cv, cm in zip(v, mu)]
    vn = [[(x - m) * lax.rsqrt(s + EPS) * pr[2] + pr[3] for x, m, s, pr in zip(cv, cm, cs, params)]
          for cv, cm, cs in zip(v, mu, var)]
    mix = [[mm(pr[0], x) + pr[1] for x, pr in zip(ch, params)] for ch in vn]
    return [[a * b for a, b in zip(cu, cx)] for cu, cx in zip(u, mix)]


def outb_head(o, pg, g):
    return rmsnorm(o, g) * (pg * sigmoid(pg))


def ffn_act(zg, zv):
    return zg * sigmoid(zg) * zv


def _tri(reverse):
    r = lax.broadcasted_iota(jnp.int32, (CH, CH), 0)
    c = lax.broadcasted_iota(jnp.int32, (CH, CH), 1)
    if reverse:
        cm, sm = c >= r, c > r
    else:
        cm, sm = c <= r, c <= r
    one = jnp.ones((), BF16)
    zero = jnp.zeros((), BF16)
    return jnp.where(cm, one, zero), jnp.where(cm.T, one, zero), sm


def gla_pair(consts, wg, bg, st0, st1, *chunks):
    m, mt, smask, lm0, lm1 = consts
    ch = [chunks[5 * i:5 * i + 5] for i in range(len(chunks) // 5)]
    la = [log_sigmoid(mm(c[0], wg) + bg) * (1.0 / 16.0) for c in ch]
    cum = [cumdot(m, mt, x) for x in la]
    tot = [jnp.sum(x, axis=0, keepdims=True) for x in la]
    q_dec = [(c[1] * 0.125) * jnp.exp(cm) for c, cm in zip(ch, cum)]
    k_inv = [c[2] * jnp.exp(-cm) for c, cm in zip(ch, cum)]
    k_end = [c[2] * jnp.exp(t - cm) for c, t, cm in zip(ch, tot, cum)]
    s = [[jnp.where(smask, mm_nt(qd * lm, ki), 0.0) for lm in (lm0, lm1)] for qd, ki in zip(q_dec, k_inv)]
    o_in = [[mm(si[h], c[3 + h]) for h in (0, 1)] for si, c in zip(s, ch)]
    ds = [[mm_tn(c[3 + h], ke * lm) for h, lm in ((0, lm0), (1, lm1))] for c, ke in zip(ch, k_end)]
    sts = [(st0, st1)]
    for t, d in zip(tot, ds):
        dec = jnp.exp(t)
        sts.append((sts[-1][0] * dec + d[0], sts[-1][1] * dec + d[1]))
    outs = []
    for qd, oi, st in zip(q_dec, o_in, sts):
        outs += [oi[0] + mm_nt(qd, st[0]), oi[1] + mm_nt(qd, st[1])]
    return (*outs, sts[-1][0], sts[-1][1])


def _lane_masks():
    lane = lax.broadcasted_iota(jnp.int32, (1, 128), 1)
    return (lane < 64).astype(F32), (lane >= 64).astype(F32)


def _cparams(n_axes=1):
    return pltpu.CompilerParams(dimension_semantics=("arbitrary",) * n_axes, vmem_limit_bytes=VMEM_LIMIT)


def _full(a):
    nd = a.ndim
    return pl.BlockSpec(a.shape, lambda *_: (0,) * nd)


def _rows(tm, w, cb=0, rev_n=None):
    if rev_n is None:
        return pl.BlockSpec((tm, w), lambda i: (i, cb))
    return pl.BlockSpec((tm, w), lambda i: (rev_n - 1 - i, cb))


def _call(body, name, grid, in_specs, out_specs, out_shape, scratch=(), n_axes=1):
    return pl.pallas_call(body, name=name, grid=grid, in_specs=in_specs, out_specs=out_specs, out_shape=out_shape,
                          scratch_shapes=list(scratch), compiler_params=_cparams(n_axes))


def _sds(shape, dt=F32):
    return jax.ShapeDtypeStruct(shape, dt)


def norm_matmul(x, g, w, tm, name, ydt=F32):
    T, n = x.shape[0], w.shape[1]

    def body(x_ref, g_ref, w_ref, y_ref, h_ref):
        hb = rmsnorm(x_ref[...], g_ref[...]).astype(MX)
        h_ref[...] = hb
        y_ref[...] = jnp.dot(hb, w_ref[...], preferred_element_type=F32).astype(ydt)

    return _call(body, name, (T // tm,), [_rows(tm, D), _full(g), _full(w)],
                 [_rows(tm, n), _rows(tm, D)], [_sds((T, n), ydt), _sds((T, D), MX)])(x, g, w)


CPB = 8


def _chunk(c):
    return slice(c * CH, (c + 1) * CH)


def gmlp_fwd(p, ws, bs, lg, lb):
    T = p.shape[0]
    tm = CPB * CH

    def body(pa_ref, ws_ref, bs_ref, lg_ref, lb_ref, o_ref):
        params = [(ws_ref[h], bs_ref[h], lg_ref[h], lb_ref[h]) for h in range(4)]
        pieces = [[(pa_ref[_chunk(c), h * 128:(h + 1) * 128], pa_ref[_chunk(c), 512 + h * 128:512 + (h + 1) * 128])
                   for h in range(4)] for c in range(CPB)]
        out = gmlp_heads(params, pieces)
        for c in range(CPB):
            for h in range(4):
                o_ref[_chunk(c), h * 128:(h + 1) * 128] = out[c][h].astype(MX)

    return _call(body, "gmlp_fwd", (T // tm,), [_rows(tm, 1024), _full(ws), _full(bs), _full(lg), _full(lb)],
                 _rows(tm, 512), _sds((T, 512), MX))(p, ws, bs, lg, lb)


def _gla_in_specs(tm, n, rev):
    r = n if rev else None
    return [_rows(tm, 256, 4, r), _rows(tm, 256, 5, r), _rows(tm, 512, 3, r), _rows(tm, 128, 20, r)]


def gla_fwd(p, wg, bg, reverse):
    T = p.shape[0]
    tm = CPB * CH
    n = T // tm
    rev = n if reverse else None

    def body(q_ref, k_ref, v_ref, r_ref, wg_ref, bg_ref, o_ref, ss_ref, st_ref):
        @pl.when(pl.program_id(0) == 0)
        def _():
            st_ref[...] = jnp.zeros_like(st_ref)

        consts = _tri(reverse) + _lane_masks()
        order = list(reversed(range(CPB))) if reverse else list(range(CPB))
        ss_ref[0] = st_ref[...]
        for j in range(2):
            sl = slice(j * 128, (j + 1) * 128)
            v0s, v1s = slice(256 * j, 256 * j + 128), slice(256 * j + 128, 256 * j + 256)
            chunks = []
            for c in order:
                rows = _chunk(c)
                chunks += [r_ref[rows, :], q_ref[rows, sl], k_ref[rows, sl], v_ref[rows, v0s], v_ref[rows, v1s]]
            res = gla_pair(consts, wg_ref[:, sl], bg_ref[:, sl], st_ref[2 * j], st_ref[2 * j + 1], *chunks)
            for i, c in enumerate(order):
                o_ref[_chunk(c), v0s] = res[2 * i]
                o_ref[_chunk(c), v1s] = res[2 * i + 1]
            st_ref[2 * j] = res[-2]
            st_ref[2 * j + 1] = res[-1]

    ss_spec = pl.BlockSpec((1, 4, 128, 128), (lambda i: (n - 1 - i, 0, 0, 0)) if reverse else (lambda i: (i, 0, 0, 0)))
    return _call(body, "gla_fwd_r" if reverse else "gla_fwd_f", (n,),
                 _gla_in_specs(tm, n, reverse) + [_full(wg), _full(bg)],
                 [_rows(tm, 512, 0, rev), ss_spec], [_sds((T, 512)), _sds((n, 4, 128, 128))],
                 scratch=[pltpu.VMEM((4, 128, 128), F32)])(p, p, p, p, wg, bg)


def mix_out(x, of, ob, p, outa, gg, w_out, tm):
    T = x.shape[0]

    def body(x_ref, of_ref, ob_ref, pg_ref, oa_ref, gg_ref, w_ref, x1_ref, mx_ref):
        mx_ref[:, 0:512] = oa_ref[...]
        for h in range(4):
            sl = slice(h * 128, (h + 1) * 128)
            mx_ref[:, 512 + h * 128:512 + (h + 1) * 128] = outb_head(
                of_ref[:, sl] + ob_ref[:, sl], pg_ref[:, sl], gg_ref[h]).astype(MX)
        x1_ref[...] = x_ref[...] + jnp.dot(mx_ref[...], w_ref[...], preferred_element_type=F32)

    return _call(body, "mix_out", (T // tm,),
                 [_rows(tm, D), _rows(tm, 512), _rows(tm, 512), _rows(tm, 512, 4), _rows(tm, 512), _full(gg), _full(w_out)],
                 [_rows(tm, D), _rows(tm, 1024)], [_sds((T, D)), _sds((T, 1024), MX)])(x, of, ob, p, outa, gg, w_out)


HALO = 16


def _halo_specs(T, tm, w):
    nb = T // HALO
    r = tm // HALO
    return [pl.BlockSpec((tm, w), lambda i: (i, 0)),
            pl.BlockSpec((HALO, w), lambda i: (jnp.maximum(i * r - 1, 0), 0)),
            pl.BlockSpec((HALO, w), lambda i: (jnp.minimum((i + 1) * r, nb - 1), 0))]


def _shifted(main, prev, nxt, i, nsteps):
    tm = main.shape[0]
    row = lax.broadcasted_iota(jnp.int32, (tm, 1), 0)
    pr = jnp.where(i > 0, prev[HALO - 1:HALO, :].astype(F32), 0.0)
    nx = jnp.where(i < nsteps - 1, nxt[0:1, :].astype(F32), 0.0)
    dn = jnp.where(row == 0, pr, pltpu.roll(main, 1, 0))
    up = jnp.where(row == tm - 1, nx, pltpu.roll(main, tm - 1, 0))
    return dn, up


def ffn_down(x1, zu, cw, cb, w_down, tm):
    T = x1.shape[0]
    ns = T // tm

    def body(x_ref, zu_ref, zp_ref, zn_ref, cw_ref, cb_ref, w_ref, x2_ref, z_ref, a_ref):
        zu = zu_ref[...].astype(F32)
        dn, up = _shifted(zu, zp_ref[...], zn_ref[...], pl.program_id(0), ns)
        z = cb_ref[...] + dn * cw_ref[0:1, :] + zu * cw_ref[1:2, :] + up * cw_ref[2:3, :]
        z_ref[...] = z.astype(MX)
        a = ffn_act(z[:, :DFF], z[:, DFF:]).astype(MX)
        a_ref[...] = a
        x2_ref[...] = x_ref[...] + jnp.dot(a, w_ref[...], preferred_element_type=F32)

    return _call(body, "ffn_down", (ns,), [_rows(tm, D)] + _halo_specs(T, tm, NUP) + [_full(cw), _full(cb), _full(w_down)],
                 [_rows(tm, D), _rows(tm, NUP), _rows(tm, DFF)],
                 [_sds((T, D)), _sds((T, NUP), MX), _sds((T, DFF), MX)])(x1, zu, zu, zu, cw, cb, w_down)


def loss_head(x, g, tgt, tm):
    T = x.shape[0]

    def body(x_ref, g_ref, t_ref, l_ref, dx_ref, dg_ref):
        @pl.when(pl.program_id(0) == 0)
        def _():
            l_ref[...] = jnp.zeros_like(l_ref)
            dg_ref[...] = jnp.zeros_like(dg_ref)

        y, vjp = jax.vjp(rmsnorm, x_ref[...], g_ref[...])
        err = y - t_ref[...]
        l_ref[...] += jnp.sum(err * err, axis=0, keepdims=True)
        dx, dg = vjp(err * (1.0 / D))
        dx_ref[...] = dx
        dg_ref[...] += dg

    return _call(body, "loss_head", (T // tm,), [_rows(tm, D), _full(g), _rows(tm, D)],
                 [_full(g), _rows(tm, D), _full(g)], [_sds((1, D)), _sds((T, D)), _sds((1, D))])(x, g, tgt)


def ffn_down_bwd(dx2, z, w_down, tm):
    T = dx2.shape[0]

    def body(dx_ref, z_ref, w_ref, dz_ref):
        da = _dg(dx_ref[...], w_ref[...], 1, 1)
        zg, zv = z_ref[:, :DFF].astype(F32), z_ref[:, DFF:].astype(F32)
        s = sigmoid(zg)
        sz = zg * s
        dz_ref[:, :DFF] = (da * zv * (s + sz * (1.0 - s))).astype(MX)
        dz_ref[:, DFF:] = (da * sz).astype(MX)

    return _call(body, "ffn_down_bwd", (T // tm,), [_rows(tm, D), _rows(tm, NUP), _full(w_down)],
                 _rows(tm, NUP), _sds((T, NUP), MX))(dx2, z, w_down)


def ffn_conv_bwd(dz, zu, cw, tm):
    T = dz.shape[0]
    ns = T // tm

    def body(dz_ref, dp_ref, dn_ref, zu_ref, cw_ref, dzu_ref, dcw_ref, dcb_ref):
        i = pl.program_id(0)

        @pl.when(i == 0)
        def _():
            dcw_ref[...] = jnp.zeros_like(dcw_ref)
            dcb_ref[...] = jnp.zeros_like(dcb_ref)

        dzb = dz_ref[...]
        dz = dzb.astype(F32)
        zu = zu_ref[...].astype(F32)
        r = lax.broadcasted_iota(jnp.int32, (tm, tm), 0)
        c = lax.broadcasted_iota(jnp.int32, (tm, tm), 1)
        row = lax.broadcasted_iota(jnp.int32, (tm, 1), 0)
        pr = jnp.where(i > 0, dp_ref[HALO - 1:HALO, :].astype(F32), 0.0)
        nx = jnp.where(i < ns - 1, dn_ref[0:1, :].astype(F32), 0.0)
        ddn = jnp.where(row == 0, pr, _dg(jnp.where(c == r - 1, 1.0, 0.0), dzb, 1, 0))
        dup = jnp.where(row == tm - 1, nx, _dg(jnp.where(c == r + 1, 1.0, 0.0), dzb, 1, 0))
        dzu_ref[...] = (dup * cw_ref[0:1, :] + dz * cw_ref[1:2, :] + ddn * cw_ref[2:3, :]).astype(MX)
        dcw_ref[0:1, :] += jnp.sum(zu * dup, axis=0, keepdims=True)
        dcw_ref[1:2, :] += jnp.sum(zu * dz, axis=0, keepdims=True)
        dcw_ref[2:3, :] += jnp.sum(zu * ddn, axis=0, keepdims=True)
        dcb_ref[...] += jnp.sum(dz, axis=0, keepdims=True)

    return _call(body, "ffn_conv_bwd", (ns,), _halo_specs(T, tm, NUP) + [_rows(tm, NUP), _full(cw)],
                 [_rows(tm, NUP), _full(cw), pl.BlockSpec((1, NUP), lambda i: (0, 0))],
                 [_sds((T, NUP), MX), _sds((3, NUP)), _sds((1, NUP))])(dz, dz, dz, zu, cw)


def nt_normbwd(dys, w, x, g, dres, tm, name):
    T = x.shape[0]
    n = len(dys)
    offs = [sum(d.shape[1] for d in dys[:i]) for i in range(n + 1)]

    def body(*refs):
        dy_refs, (w_ref, x_ref, g_ref, dr_ref, dx_ref, dg_ref) = refs[:n], refs[n:]

        @pl.when(pl.program_id(0) == 0)
        def _():
            dg_ref[...] = jnp.zeros_like(dg_ref)

        dh = _dg(dy_refs[0][...], w_ref[:, offs[0]:offs[1]], 1, 1)
        for i in range(1, n):
            dh = dh + _dg(dy_refs[i][...], w_ref[:, offs[i]:offs[i + 1]], 1, 1)
        _, vjp = jax.vjp(rmsnorm, x_ref[...], g_ref[...])
        dx, dg = vjp(dh)
        dx_ref[...] = dr_ref[...] + dx
        dg_ref[...] += dg

    return _call(body, name, (T // tm,),
                 [_rows(tm, d.shape[1]) for d in dys] + [_full(w), _rows(tm, D), _full(g), _rows(tm, D)],
                 [_rows(tm, D), _full(g)], [_sds((T, D)), _sds((1, D))])(*dys, w, x, g, dres)


def matmul_tn(a, b, tt, tn, name):
    T, k = a.shape
    n = b.shape[1]
    last = T // tt - 1

    def body(a_ref, b_ref, o_ref, acc_ref):
        @pl.when(pl.program_id(1) == 0)
        def _():
            acc_ref[...] = jnp.zeros_like(acc_ref)

        acc_ref[...] += _dg(a_ref[...], b_ref[...], 0, 0)

        @pl.when(pl.program_id(1) == last)
        def _():
            o_ref[...] = acc_ref[...].astype(MX)

    return _call(body, name, (n // tn, T // tt),
                 [pl.BlockSpec((tt, k), lambda j, i: (i, 0)), pl.BlockSpec((tt, tn), lambda j, i: (i, j))],
                 pl.BlockSpec((k, tn), lambda j, i: (0, j)), _sds((k, n), MX), scratch=[pltpu.VMEM((k, tn), F32)],
                 n_axes=2)(a, b)


def mix_out_bwd(dx1, w_out, of, ob, p, gg, tm):
    T = dx1.shape[0]

    def body(dx_ref, w_ref, of_ref, ob_ref, pg_ref, gg_ref, da_ref, do_ref, dpg_ref, dgg_ref):
        @pl.when(pl.program_id(0) == 0)
        def _():
            dgg_ref[...] = jnp.zeros_like(dgg_ref)

        dxb = dx_ref[...].astype(MX)
        da_ref[...] = _dg(dxb, w_ref[0:512, :], 1, 1)
        for h in range(4):
            sl = slice(h * 128, (h + 1) * 128)
            dm = _dg(dxb, w_ref[512 + h * 128:512 + (h + 1) * 128, :], 1, 1)
            _, vjp = jax.vjp(outb_head, of_ref[:, sl] + ob_ref[:, sl], pg_ref[:, sl], gg_ref[h])
            do, dpg, dg = vjp(dm)
            do_ref[:, sl] = do
            dpg_ref[:, sl] = dpg
            dgg_ref[h] += dg

    return _call(body, "mix_out_bwd", (T // tm,),
                 [_rows(tm, D), _full(w_out), _rows(tm, 512), _rows(tm, 512), _rows(tm, 512, 4), _full(gg)],
                 [_rows(tm, 512), _rows(tm, 512), _rows(tm, 512), _full(gg)],
                 [_sds((T, 512)), _sds((T, 512)), _sds((T, 512)), _sds(gg.shape)])(dx1, w_out, of, ob, p, gg)


def gla_bwd(p, wg, bg, ss, do, reverse, merge=None):
    T = p.shape[0]
    tm = CPB * CH
    n = T // tm
    rev = not reverse
    rn = n if rev else None

    def body(*refs):
        q_ref, k_ref, v_ref, r_ref, wg_ref, bg_ref, ss_ref, do_ref = refs[:8]
        if merge is None:
            dq_ref, dk_ref, dv_ref, dr_ref, dwg_ref, dbg_ref, dst_ref = refs[8:]
        else:
            mq_ref, mk_ref, mv_ref, mr_ref, mg_ref, out_ref, dwg_ref, dbg_ref, dst_ref, drs_ref = refs[8:]
            out_ref[:, 1024:1536] = mg_ref[...].astype(MX)

        @pl.when(pl.program_id(0) == 0)
        def _():
            dst_ref[...] = jnp.zeros_like(dst_ref)
            dwg_ref[...] = jnp.zeros_like(dwg_ref)
            dbg_ref[...] = jnp.zeros_like(dbg_ref)

        consts = _tri(reverse) + _lane_masks()
        order = list(reversed(range(CPB))) if reverse else list(range(CPB))
        for j in range(2):
            sl = slice(j * 128, (j + 1) * 128)
            v0s, v1s = slice(256 * j, 256 * j + 128), slice(256 * j + 128, 256 * j + 256)
            chunks, dout = [], []
            for c in order:
                rows = _chunk(c)
                chunks += [r_ref[rows, :], q_ref[rows, sl], k_ref[rows, sl], v_ref[rows, v0s], v_ref[rows, v1s]]
                dout += [do_ref[rows, v0s], do_ref[rows, v1s]]
            _, vjp = jax.vjp(functools.partial(gla_pair, consts), wg_ref[:, sl], bg_ref[:, sl],
                             ss_ref[0, 2 * j], ss_ref[0, 2 * j + 1], *chunks)
            g = vjp((*dout, dst_ref[2 * j], dst_ref[2 * j + 1]))
            dwg_ref[:, sl] += g[0]
            dbg_ref[:, sl] += g[1]
            dst_ref[2 * j] = g[2]
            dst_ref[2 * j + 1] = g[3]
            for i, c in enumerate(order):
                rows = _chunk(c)
                dr, dq, dk, dv0, dv1 = g[4 + 5 * i:9 + 5 * i]
                if merge is None:
                    if j == 0:
                        dr_ref[rows, :] = dr
                    else:
                        dr_ref[rows, :] += dr
                    dq_ref[rows, sl] = dq
                    dk_ref[rows, sl] = dk
                    dv_ref[rows, v0s] = dv0
                    dv_ref[rows, v1s] = dv1
                else:
                    if j == 0:
                        drs_ref[rows, :] = mr_ref[rows, :] + dr
                    else:
                        out_ref[rows, 1536:1664] = (drs_ref[rows, :] + dr).astype(MX)
                    out_ref[rows, sl] = (mq_ref[rows, sl] + dq).astype(MX)
                    out_ref[rows, 256 + 128 * j:384 + 128 * j] = (mk_ref[rows, sl] + dk).astype(MX)
                    out_ref[rows, 512 + 256 * j:640 + 256 * j] = (mv_ref[rows, v0s] + dv0).astype(MX)
                    out_ref[rows, 640 + 256 * j:768 + 256 * j] = (mv_ref[rows, v1s] + dv1).astype(MX)

    ss_spec = pl.BlockSpec((1, 4, 128, 128), (lambda i: (n - 1 - i, 0, 0, 0)) if rev else (lambda i: (i, 0, 0, 0)))
    ins = [p, p, p, p, wg, bg, ss, do]
    in_specs = _gla_in_specs(tm, n, rev) + [_full(wg), _full(bg), ss_spec, _rows(tm, 512, 0, rn)]
    scratch = [pltpu.VMEM((4, 128, 128), F32)]
    if merge is None:
        out_specs = [_rows(tm, 256, 0, rn), _rows(tm, 256, 0, rn), _rows(tm, 512, 0, rn), _rows(tm, 128, 0, rn)]
        out_shape = [_sds((T, 256)), _sds((T, 256)), _sds((T, 512)), _sds((T, 128))]
    else:
        ins += list(merge)
        in_specs += [_rows(tm, a.shape[1], 0, rn) for a in merge]
        out_specs, out_shape = [_rows(tm, 1664, 0, rn)], [_sds((T, 1664), MX)]
        scratch.append(pltpu.VMEM((tm, 128), F32))
    return _call(body, "gla_bwd_r" if reverse else "gla_bwd_f", (n,), in_specs, out_specs + [_full(wg), _full(bg)],
                 out_shape + [_sds(wg.shape), _sds(bg.shape)], scratch=scratch)(*ins)


def gmlp_bwd(p, douta, ws, bs, lg, lb):
    T = p.shape[0]
    cpb = 4
    tm = cpb * CH

    def body(pa_ref, do_ref, ws_ref, bs_ref, lg_ref, lb_ref, dpa_ref, dws_ref, dbs_ref, dlg_ref, dlb_ref):
        @pl.when(pl.program_id(0) == 0)
        def _():
            for r in (dws_ref, dbs_ref, dlg_ref, dlb_ref):
                r[...] = jnp.zeros_like(r)

        us = [slice(h * 128, (h + 1) * 128) for h in range(4)]
        vs = [slice(512 + h * 128, 512 + (h + 1) * 128) for h in range(4)]
        params = [(ws_ref[h], bs_ref[h], lg_ref[h], lb_ref[h]) for h in range(4)]
        pieces = [[(pa_ref[_chunk(c), us[h]], pa_ref[_chunk(c), vs[h]]) for h in range(4)] for c in range(cpb)]
        _, vjp = jax.vjp(gmlp_heads, params, pieces)
        dparams, dpieces = vjp([[do_ref[_chunk(c), us[h]] for h in range(4)] for c in range(cpb)])
        for h in range(4):
            for r, a in zip((dws_ref, dbs_ref, dlg_ref, dlb_ref), dparams[h]):
                r[h] += a
            for c in range(cpb):
                dpa_ref[_chunk(c), us[h]] = dpieces[c][h][0].astype(MX)
                dpa_ref[_chunk(c), vs[h]] = dpieces[c][h][1].astype(MX)

    return _call(body, "gmlp_bwd", (T // tm,),
                 [_rows(tm, 1024), _rows(tm, 512), _full(ws), _full(bs), _full(lg), _full(lb)],
                 [_rows(tm, 1024), _full(ws), _full(bs), _full(lg), _full(lb)],
                 [_sds((T, 1024), MX), _sds(ws.shape), _sds(bs.shape), _sds(lg.shape), _sds(lb.shape)])(p, douta, ws, bs, lg, lb)


def _gate_pad(w, row0):
    return jnp.zeros((128, 256), F32).at[row0:row0 + 16].set(w)


def local_step(x, tgt, W, get_big, emit, tm=256, tmm=512):
    saved = []
    for l in range(NL):
        s = {"x": x}
        s.update(get_big(l, "in", x))
        p, s["h"] = norm_matmul(x, W["g_mix"][l][None], s["w_in"], tmm, "mix_in")
        s["p"] = p
        ws, bs = W["w_s"][l], W["b_s"][l][:, :, None]
        lg, lb = W["ln_g"][l][:, None, :], W["ln_b"][l][:, None, :]
        outa = gmlp_fwd(p, ws, bs, lg, lb)
        wgf, wgb = _gate_pad(W["w_gate_f"][l], 0), _gate_pad(W["w_gate_b"][l], 16)
        bgf, bgb = W["b_gate_f"][l][None], W["b_gate_b"][l][None]
        s["of"], s["ssf"] = gla_fwd(p, wgf, bgf, False)
        s["ob"], s["ssb"] = gla_fwd(p, wgb, bgb, True)
        s.update(get_big(l, "rest", s["ob"]))
        gg = W["g_gla"][l][:, None, :]
        x1, s["mixed"] = mix_out(x, s["of"], s["ob"], p, outa, gg, s["w_out"], tmm)
        s["x1"] = x1
        s["zu"], s["h2"] = norm_matmul(x1, W["g_ffn"][l][None], s["w_up"], tmm, "ffn_up", MX)
        x, s["z"], s["a"] = ffn_down(x1, s["zu"], W["conv_w"][l], W["conv_b"][l][None], s["w_down"], tm)
        saved.append(s)

    lsum, dx, dgf = loss_head(x, W["g_final"][None], tgt, tmm)
    G = {k: [None] * NL for k in _SMALL if k != "g_final"}
    tok = jnp.zeros((1, 1), F32)
    for l in reversed(range(NL)):
        s = saved[l]
        g_down = matmul_tn(s["a"], dx, min(1024, tmm * 2), 512, "dw_down")
        dz = ffn_down_bwd(dx, s["z"], s["w_down"], tm)
        dzu, G["conv_w"][l], dcb = ffn_conv_bwd(dz, s["zu"], W["conv_w"][l] + tok, tm)
        G["conv_b"][l] = dcb[0]
        g_up = matmul_tn(s["h2"], dzu, min(1024, tmm * 2), 1408, "dw_up")
        tok = emit(l, "A", {"w_down": g_down, "w_up": g_up})
        dx1, dg = nt_normbwd([dzu], s["w_up"], s["x1"], W["g_ffn"][l][None] + tok, dx, tmm, "ffn_up_bwd")
        G["g_ffn"][l] = dg[0]
        g_out = matmul_tn(s["mixed"], dx1, min(1024, tmm * 2), 1024, "dw_out")
        gg = W["g_gla"][l][:, None, :]
        douta, do, dpg, dgg = mix_out_bwd(dx1, s["w_out"], s["of"], s["ob"], s["p"], gg, tmm)
        G["g_gla"][l] = dgg[:, 0, :]
        wgf, wgb = _gate_pad(W["w_gate_f"][l], 0), _gate_pad(W["w_gate_b"][l], 16)
        bgf, bgb = W["b_gate_f"][l][None], W["b_gate_b"][l][None]
        dqf, dkf, dvf, drf, dwgf, dbgf = gla_bwd(s["p"], wgf, bgf, s["ssf"], do, False)
        dpb, dwgb, dbgb = gla_bwd(s["p"], wgb, bgb, s["ssb"], do, True, merge=(dqf, dkf, dvf, drf, dpg))
        G["w_gate_f"][l], G["b_gate_f"][l] = dwgf[0:16], dbgf[0]
        G["w_gate_b"][l], G["b_gate_b"][l] = dwgb[16:32], dbgb[0]
        ws, bs = W["w_s"][l], W["b_s"][l][:, :, None]
        lg, lb = W["ln_g"][l][:, None, :], W["ln_b"][l][:, None, :]
        dpa, G["w_s"][l], dbs, dlg, dlb = gmlp_bwd(s["p"], douta, ws, bs, lg, lb)
        G["b_s"][l], G["ln_g"][l], G["ln_b"][l] = dbs[:, :, 0], dlg[:, 0, :], dlb[:, 0, :]
        tt = min(1024, tmm * 2)
        g_in = jnp.concatenate([matmul_tn(s["h"], dpa, tt, 1024, "dw_in_a"),
                                matmul_tn(s["h"], dpb, tt, 1664, "dw_in_b")], axis=1)
        tok = emit(l, "B", {"w_out": g_out, "w_in": g_in})
        dx, dg = nt_normbwd([dpa, dpb], s["w_in"], s["x"], W["g_mix"][l][None] + tok, dx1, tmm, "mix_in_bwd")
        G["g_mix"][l] = dg[0]
    G = {k: jnp.stack(v) for k, v in G.items()}
    G["g_final"] = dgf[0]
    return lsum, dx, G


def cast_bf16(a, tr):
    r, c = a.shape

    def body(a_ref, o_ref):
        o_ref[...] = a_ref[...].astype(BF16)

    return _call(body, "cast_bf16", (r // tr,), [_rows(tr, c)], _rows(tr, c), _sds((r, c), BF16))(a)


def sum_lead(y, tr):
    n, rr, cc = y.shape

    def body(y_ref, o_ref):
        acc = y_ref[0].astype(F32)
        for k in range(1, n):
            acc = acc + y_ref[k].astype(F32)
        o_ref[...] = acc

    return _call(body, "sum_lead", (rr // tr,), [pl.BlockSpec((n, tr, cc), lambda i: (0, i, 0))],
                 _rows(tr, cc), _sds((rr, cc)))(y)


def sum_parts(land, grad, k, chipvec, tr):
    _, rr, cc = land.shape
    nb = rr // tr

    def body(c_ref, l_ref, g_ref, o_ref):
        mine = g_ref[...].astype(F32)
        acc = None
        for j in range(4):
            part = jnp.where(c_ref[0] == j, mine, l_ref[j].astype(F32))
            acc = part if acc is None else acc + part
        o_ref[...] = acc

    gs = pltpu.PrefetchScalarGridSpec(
        num_scalar_prefetch=1, grid=(nb,),
        in_specs=[pl.BlockSpec((4, tr, cc), lambda i, c: (0, i, 0)), _part_spec(k, tr, nb)],
        out_specs=pl.BlockSpec((tr, cc), lambda i, c: (i, 0)))
    return pl.pallas_call(body, name="sum_parts", grid_spec=gs, out_shape=_sds((rr, cc)),
                          compiler_params=_cparams(1))(chipvec, land, grad)


def adamw(w, ga, gb, m, v, tr):
    r, c = w.shape

    def body(w_ref, ga_ref, gb_ref, m_ref, v_ref, g_ref, d_ref, nm_ref, nv_ref):
        gr = ga_ref[...] + gb_ref[...]
        g_ref[...] = gr
        nm = ADAM_B1 * m_ref[...] + (1.0 - ADAM_B1) * gr
        nv = ADAM_B2 * v_ref[...] + (1.0 - ADAM_B2) * jnp.square(gr)
        m_hat = nm / (1.0 - ADAM_B1 ** ADAM_STEP)
        v_hat = nv / (1.0 - ADAM_B2 ** ADAM_STEP)
        d_ref[...] = -ADAM_LR * (m_hat / (jnp.sqrt(v_hat) + ADAM_EPS) + ADAM_WD * w_ref[...])
        nm_ref[...] = nm
        nv_ref[...] = nv

    sp = _rows(tr, c)
    return _call(body, "adamw", (r // tr,), [sp] * 5, [sp] * 4, [_sds((r, c))] * 4)(w, ga, gb, m, v)


MESH = pl.DeviceIdType.MESH
ANY = pl.BlockSpec(memory_space=pl.ANY)
N_BIG = 4


def _pos():
    return lax.axis_index("x"), lax.axis_index("y"), lax.axis_index("c")


def _other_chips(x, y):
    return [(1 - x, y), (x, 1 - y), (1 - x, 1 - y)]


def _rcopy(src, dst, send_sems, recv_sems, k, to):
    return pltpu.make_async_remote_copy(src_ref=src, dst_ref=dst, send_sem=send_sems.at[k], recv_sem=recv_sems.at[k],
                                        device_id=to, device_id_type=MESH)


def allgather8(xs):
    m, n = xs.shape

    def body(x_ref, out_ref, send_sems, recv_sems, local_sem):
        x, y, c = _pos()
        me, sibling = (x, y, c), (x, y, 1 - c)
        chips = _other_chips(x, y)

        def rows(px, py, pc):
            return out_ref.at[pl.ds((4 * px + 2 * py + pc) * m, m), :]

        def copy(k, block, to, src=None):
            return _rcopy(rows(*block) if src is None else src, rows(*block), send_sems, recv_sems, k, to)

        mine = pltpu.make_async_copy(x_ref, rows(*me), local_sem)
        mine.start()
        first = [copy(0, me, sibling, src=x_ref)]
        first += [copy(1 + j, me, (*chip, c), src=x_ref) for j, chip in enumerate(chips)]
        for cp in first:
            cp.start()
        passed = [copy(4 + j, (*chip, c), sibling) for j, chip in enumerate(chips)]
        for j, chip in enumerate(chips):
            copy(1 + j, (*chip, c), me).wait_recv()
            passed[j].start()
        copy(0, sibling, me).wait_recv()
        for j, chip in enumerate(chips):
            copy(4 + j, (*chip, 1 - c), me).wait_recv()
        for cp in first + passed:
            cp.wait_send()
        mine.wait()

    vm = pl.BlockSpec(memory_space=pltpu.VMEM)
    return pl.pallas_call(
        body, name="allgather8", out_shape=_sds((8 * m, n), xs.dtype), in_specs=[vm], out_specs=vm,
        scratch_shapes=[pltpu.SemaphoreType.DMA((7,)), pltpu.SemaphoreType.DMA((7,)), pltpu.SemaphoreType.DMA],
        compiler_params=pltpu.CompilerParams(vmem_limit_bytes=VMEM_LIMIT))(xs)


def _slab(k, ref, j):
    if k == 0:
        return ref.at[j]
    if k == 1:
        return ref.at[pl.ds(256 * j, 256), :]
    if k == 2:
        return ref.at[:, pl.ds(1408 * j, 1408)]
    return ref.at[pl.ds(704 * j, 704), :]


_LAYER_FULL = [(4, 1024, 648), (1024, 1024), (1024, NUP), (DFF, 1024)]
_LAYER_SHARD = [(1024, 648), (256, 1024), (1024, 1408), (704, 1024)]
_SHARD_SHAPES = [(NL,) + s for s in _LAYER_SHARD]

HBM = pl.BlockSpec(memory_space=pltpu.HBM)
SEM = pl.BlockSpec(memory_space=pltpu.SEMAPHORE)
VM = pl.BlockSpec(memory_space=pltpu.VMEM)
EFFECT = pltpu.SideEffectType.DATAFLOW_SIDE_EFFECTING
_GW_GROUPS = [[(0, 0)], [(0, 1), (0, 2), (0, 3)]] + [[(l, k) for k in range(N_BIG)] for l in range(1, NL)]
_GW_ORDER = [lk for g in _GW_GROUPS for lk in g]


def _hbm(a):
    return pltpu.with_memory_space_constraint(a, pltpu.HBM)


def _hbm_like(a):
    return pltpu.HBM(a.shape, a.dtype)


def _part_spec(k, tr, nb):
    cc = _LAYER_SHARD[k][1]
    if k == 0:
        return pl.BlockSpec((None, tr, cc), lambda i, c: (c[0], i, 0))
    if k == 2:
        return pl.BlockSpec((tr, cc), lambda i, c: (i, c[0]))
    return pl.BlockSpec((tr, cc), lambda i, c: (c[0] * nb + i, 0))


def place_own(shard, landing, l, k, chipvec, tr):
    rr, cc = _LAYER_SHARD[k]
    nb = rr // tr

    def body(c_ref, s_ref, l_ref, o_ref):
        o_ref[...] = s_ref[...]

    gs = pltpu.PrefetchScalarGridSpec(
        num_scalar_prefetch=1, grid=(nb,),
        in_specs=[pl.BlockSpec((None, tr, cc), lambda i, c: (l, i, 0)), ANY], out_specs=_part_spec(k, tr, nb))
    return pl.pallas_call(body, name="place_own", grid_spec=gs, out_shape=_sds(landing.shape, landing.dtype),
                          input_output_aliases={2: 0}, compiler_params=_cparams(1))(chipvec, shard, landing)


def gw_start(shards, landings):
    n = len(_GW_ORDER)

    def body(*refs):
        S, Ld = refs[:N_BIG], refs[N_BIG:N_BIG + n]
        outs = refs[N_BIG + n:]
        send_sems, recv, token = outs[0], outs[1:1 + len(_GW_GROUPS)], outs[-1]
        x, y, c = _pos()
        me = 2 * x + y
        ci = 0
        for gi, grp in enumerate(_GW_GROUPS):
            for t, (l, k) in enumerate(grp):
                land = Ld[_GW_ORDER.index((l, k))]
                for j, (px, py) in enumerate(_other_chips(x, y)):
                    pltpu.make_async_remote_copy(
                        src_ref=S[k].at[l], dst_ref=_slab(k, land, me), send_sem=send_sems.at[ci],
                        recv_sem=recv[gi].at[3 * t + j], device_id=(px, py, c), device_id_type=MESH).start()
                    ci += 1
        token[...] = jnp.zeros_like(token)

    ins = list(shards) + list(landings)
    sems = [pltpu.SemaphoreType.DMA((3 * n,))] + [pltpu.SemaphoreType.DMA((3 * len(g),)) for g in _GW_GROUPS]
    outs = pl.pallas_call(
        body, name="gw_start", out_shape=sems + [_hbm_like(a) for a in ins] + [_sds((8, 128))],
        in_specs=[HBM] * len(ins), out_specs=[SEM] * len(sems) + [HBM] * len(ins) + [VM],
        input_output_aliases={i: len(sems) + i for i in range(len(ins))},
        compiler_params=pltpu.CompilerParams(has_side_effects=EFFECT))(*[_hbm(a) for a in ins])
    ns = len(sems)
    return outs[0], outs[1:ns], outs[ns:ns + N_BIG], outs[ns + N_BIG:ns + len(ins)], outs[-1]


def gw_wait(gi, landings, recv_sems, after, shards=None, send_sems=None):
    grp = _GW_GROUPS[gi]
    n = len(grp)
    last = shards is not None

    def body(*refs):
        Ld, rs = refs[:n], refs[n]
        x, y, c = _pos()
        for t, (l, k) in enumerate(grp):
            for j, (px, py) in enumerate(_other_chips(x, y)):
                region = _slab(k, Ld[t], 2 * px + py)
                pltpu.make_async_remote_copy(src_ref=region, dst_ref=region, send_sem=rs.at[3 * t + j],
                                             recv_sem=rs.at[3 * t + j], device_id=(px, py, c),
                                             device_id_type=MESH).wait_recv()
        if last:
            S, ss = refs[n + 2:n + 2 + N_BIG], refs[n + 2 + N_BIG]
            me = 2 * x + y
            for ci, (l, k) in enumerate(lk for lk in _GW_ORDER for _ in range(3)):
                pltpu.make_async_remote_copy(src_ref=S[k].at[l], dst_ref=_slab(k, Ld[k], me), send_sem=ss.at[ci],
                                             recv_sem=ss.at[ci], device_id=(x, y, c), device_id_type=MESH).wait_send()

    ins = list(landings) + [recv_sems, after]
    specs = [HBM] * n + [SEM, pl.BlockSpec(memory_space=pl.ANY)]
    outs = [_hbm_like(a) for a in landings]
    alias = {i: i for i in range(n)}
    if last:
        ins += list(shards) + [send_sems]
        specs += [HBM] * N_BIG + [SEM]
        outs += [_hbm_like(a) for a in shards]
        alias.update({n + 2 + i: n + i for i in range(N_BIG)})
    res = pl.pallas_call(body, name="gw_wait_%d" % gi, out_shape=outs, in_specs=specs, out_specs=[HBM] * len(outs),
                         input_output_aliases=alias,
                         compiler_params=pltpu.CompilerParams(has_side_effects=EFFECT))(*ins)
    return res[:n]


def ga_start(tag, ks, grads, landings):
    n = len(ks)

    def body(*refs):
        G, Ld = refs[:n], refs[n:2 * n]
        send_sems, recv_sems, token = refs[2 * n], refs[2 * n + 1], refs[-1]
        x, y, c = _pos()
        me = 2 * x + y
        for t, k in enumerate(ks):
            for j, (px, py) in enumerate(_other_chips(x, y)):
                pltpu.make_async_remote_copy(
                    src_ref=_slab(k, G[t], 2 * px + py), dst_ref=Ld[t].at[me], send_sem=send_sems.at[3 * t + j],
                    recv_sem=recv_sems.at[3 * t + j], device_id=(px, py, c), device_id_type=MESH).start()
        token[...] = jnp.zeros_like(token)

    ins = list(grads) + list(landings)
    sems = [pltpu.SemaphoreType.DMA((3 * n,))] * 2
    outs = pl.pallas_call(
        body, name="ga_start_" + tag, out_shape=sems + [_hbm_like(a) for a in ins] + [_sds((8, 128))],
        in_specs=[HBM] * len(ins), out_specs=[SEM, SEM] + [HBM] * len(ins) + [VM],
        input_output_aliases={i: 2 + i for i in range(len(ins))},
        compiler_params=pltpu.CompilerParams(has_side_effects=EFFECT))(*[_hbm(a) for a in ins])
    return outs[0], outs[1], outs[2:2 + n], outs[2 + n:2 + 2 * n], outs[-1]


def ga_wait(tag, ks, send_sems, recv_sems, grads, landings, after):
    n = len(ks)

    def body(*refs):
        G, Ld, ss, rs = refs[:n], refs[n:2 * n], refs[2 * n], refs[2 * n + 1]
        x, y, c = _pos()
        me = 2 * x + y
        for t, k in enumerate(ks):
            for j, (px, py) in enumerate(_other_chips(x, y)):
                pj = 2 * px + py
                cp = pltpu.make_async_remote_copy(
                    src_ref=_slab(k, G[t], pj), dst_ref=Ld[t].at[pj], send_sem=ss.at[3 * t + j],
                    recv_sem=rs.at[3 * t + j], device_id=(px, py, c), device_id_type=MESH)
                cp.wait_send()
                cp.wait_recv()

    ins = list(grads) + list(landings) + [send_sems, recv_sems, after]
    res = pl.pallas_call(
        body, name="ga_wait_" + tag, out_shape=[_hbm_like(a) for a in list(grads) + list(landings)],
        in_specs=[HBM] * (2 * n) + [SEM, SEM, pl.BlockSpec(memory_space=pl.ANY)], out_specs=[HBM] * (2 * n),
        input_output_aliases={i: i for i in range(2 * n)},
        compiler_params=pltpu.CompilerParams(has_side_effects=EFFECT))(*ins)
    return res[:n], res[n:]


def swap4(parts):
    def body(*refs):
        Q, R = refs[:N_BIG], refs[N_BIG:2 * N_BIG]
        send_sems, recv_sems = refs[2 * N_BIG:]
        x, y, c = _pos()
        cps = [_rcopy(Q[k], R[k], send_sems, recv_sems, k, (x, y, 1 - c)) for k in range(N_BIG)]
        for cp in cps:
            cp.start()
        for cp in cps:
            cp.wait()

    return pl.pallas_call(
        body, name="swap4", out_shape=[_sds(s) for s in _SHARD_SHAPES],
        in_specs=[ANY] * N_BIG, out_specs=[ANY] * N_BIG,
        scratch_shapes=[pltpu.SemaphoreType.DMA((N_BIG,)), pltpu.SemaphoreType.DMA((N_BIG,))])(*parts)


_WEIGHTS = ['g_mix', 'w_in', 'w_s', 'b_s', 'ln_g', 'ln_b', 'w_gate_f', 'b_gate_f', 'w_gate_b', 'b_gate_b', 'g_gla',
            'w_out', 'g_ffn', 'w_up', 'conv_w', 'conv_b', 'w_down', 'g_final']
_BIG = ['w_in', 'w_out', 'w_up', 'w_down']
_SMALL = [n for n in _WEIGHTS if n not in _BIG]
_SMALL_SHARDED = {'w_gate_f': 64, 'w_gate_b': 64, 'conv_w': 1408}
_BIG_TR = {'w_in': 512, 'w_out': 256, 'w_up': 256, 'w_down': 352}


def _pack(arrs):
    flat = jnp.concatenate([a.reshape(-1) for a in arrs])
    pad = (-flat.shape[0]) % 1024
    return jnp.pad(flat, (0, pad)).reshape(-1, 128)


def _unpack(buf, shapes):
    flat = buf.reshape(-1)
    out, o = [], 0
    for s in shapes:
        n = 1
        for d in s:
            n *= d
        out.append(flat[o:o + n].reshape(s))
        o += n
    return out


def kernel(x, g_mix, w_in, w_s, b_s, ln_g, ln_b, w_gate_f, b_gate_f, w_gate_b, b_gate_b, g_gla, w_out, g_ffn, w_up, conv_w, conv_b, w_down, g_final, loss_target, m_g_mix, m_w_in, m_w_s, m_b_s, m_ln_g, m_ln_b, m_w_gate_f, m_b_gate_f, m_w_gate_b, m_b_gate_b, m_g_gla, m_w_out, m_g_ffn, m_w_up, m_conv_w, m_conv_b, m_w_down, m_g_final, v_g_mix, v_w_in, v_w_s, v_b_s, v_ln_g, v_ln_b, v_w_gate_f, v_b_gate_f, v_w_gate_b, v_b_gate_b, v_g_gla, v_w_out, v_g_ffn, v_w_up, v_conv_w, v_conv_b, v_w_down, v_g_final):
    loc = locals()
    w = {n: loc[n] for n in _WEIGHTS}
    m = {n: loc["m_" + n] for n in _WEIGHTS}
    v = {n: loc["v_" + n] for n in _WEIGHTS}
    xi, yi, _ = _pos()
    chip = 2 * xi + yi

    sh_names = list(_SMALL_SHARDED)
    g8 = allgather8(_pack([w[n] for n in sh_names]))
    rows = g8.shape[0] // 8
    per_chip = [_unpack(g8[2 * j * rows:(2 * j + 1) * rows], [w[n].shape for n in sh_names]) for j in range(4)]
    W = dict(w)
    for k, n in enumerate(sh_names):
        W[n] = jnp.concatenate([per_chip[j][k] for j in range(4)], axis=-1)

    shards = [cast_bf16(w[n].reshape(-1, w[n].shape[-1]), _BIG_TR[n]).reshape(w[n].shape) for n in _BIG]
    chipvec = jnp.reshape(chip, (1,)).astype(jnp.int32)
    landings = [place_own(shards[k], lax.empty(_LAYER_FULL[k], BF16), l, k, chipvec, _BIG_TR[_BIG[k]])
                for l, k in _GW_ORDER]
    send_sems, recv_sems, shards_fly, landings_fly, _ = gw_start(shards, landings)
    arrived = {}

    def get_big(l, stage, after):
        gi = {(0, "in"): 0, (0, "rest"): 1}.get((l, stage), l + 1 if stage == "in" else None)
        if gi is not None:
            lo = sum(len(g) for g in _GW_GROUPS[:gi])
            lands = landings_fly[lo:lo + len(_GW_GROUPS[gi])]
            if gi == len(_GW_GROUPS) - 1:
                full = gw_wait(gi, lands, recv_sems[gi], after, shards_fly, send_sems)
            else:
                full = gw_wait(gi, lands, recv_sems[gi], after)
            arrived.update(zip(_GW_GROUPS[gi], full))
        if stage == "in":
            f_in = jnp.transpose(arrived[(l, 0)], (1, 0, 2)).reshape(D, N_IN)
            return {"w_in": jnp.pad(f_in, ((0, 0), (0, N_INP - N_IN)))}
        return {"w_out": arrived[(l, 1)], "w_up": arrived[(l, 2)], "w_down": arrived[(l, 3)]}

    flying = []

    def emit(l, group, grads):
        ks = [3, 2] if group == "A" else [1, 0]
        gs = [grads[_BIG[k]] for k in ks]
        if group == "B":
            gs[1] = jnp.transpose(gs[1][:, :N_IN].reshape(D, 4, 648), (1, 0, 2))
        lands = [lax.empty((4,) + _LAYER_SHARD[k], BF16) for k in ks]
        tag = "%d%s" % (l, group)
        ss, rs, gs_fly, lands_fly, tok = ga_start(tag, ks, gs, lands)
        flying.append((tag, l, ks, ss, rs, gs_fly, lands_fly))
        return tok[0:1, 0:1]

    lsum, grad_x, G = local_step(x[0], loss_target[0], W, get_big, emit)

    plane = [[None] * NL for _ in range(N_BIG)]
    for tag, l, ks, ss, rs, gs_fly, lands_fly in flying:
        for k, g, a in zip(ks, *ga_wait(tag, ks, ss, rs, gs_fly, lands_fly, grad_x)):
            plane[k][l] = sum_parts(a, g, k, chipvec, _BIG_TR[_BIG[k]])
    plane = [jnp.stack(p) for p in plane]
    other = swap4(plane)

    small_shapes = [G[n].shape for n in _SMALL] + [(D,)]
    pk = _pack([G[n] for n in _SMALL] + [lsum])
    srows = pk.shape[0]
    red = sum_lead(allgather8(pk).reshape(8, srows, 128), srows)
    small = dict(zip(_SMALL + ["lsum"], _unpack(red, small_shapes)))
    loss = 0.5 * jnp.sum(small.pop("lsum")) / D
    for n, wd in _SMALL_SHARDED.items():
        small[n] = lax.dynamic_slice_in_dim(small[n], chip * wd, wd, axis=small[n].ndim - 1)

    grads, delta, new_m, new_v = dict(small), {}, {}, {}
    two = lambda a: a.reshape(-1, a.shape[-1])
    for k, n in enumerate(_BIG):
        res = adamw(two(w[n]), two(plane[k]), two(other[k]), two(m[n]), two(v[n]), _BIG_TR[n])
        grads[n], delta[n], new_m[n], new_v[n] = (r.reshape(w[n].shape) for r in res)
    shapes = [w[n].shape for n in _SMALL]
    pw, pg, pm, pv = (_pack([t[n] for n in _SMALL]) for t in (w, grads, m, v))
    _, d_, m_, v_ = adamw(pw, pg, jnp.zeros_like(pg), pm, pv, pw.shape[0])
    for t, buf in ((delta, d_), (new_m, m_), (new_v, v_)):
        t.update(zip(_SMALL, _unpack(buf, shapes)))

    return (loss, grad_x[None], *[grads[n] for n in _WEIGHTS], *[delta[n] for n in _WEIGHTS],
            *[new_m[n] for n in _WEIGHTS], *[new_v[n] for n in _WEIGHTS])
```

```python
import functools

import jax
import jax.numpy as jnp
from jax import lax
from jax.experimental import pallas as pl
from jax.experimental.pallas import tpu as pltpu

F32 = jnp.float32
BF16 = jnp.bfloat16
MX = BF16

D = 1024
CH = 128
NL = 4
N_IN = 2592
N_INP = 2688
NUP = 5632
DFF = 2816
EPS = 1e-6
VMEM_LIMIT = 56 * 1024 * 1024

ADAM_LR, ADAM_B1, ADAM_B2, ADAM_EPS, ADAM_WD, ADAM_STEP = 0.001, 0.9, 0.999, 1e-08, 0.01, 10


def _dg(a, b, ca, cb):
    return lax.dot_general(a.astype(MX), b.astype(MX), (((ca,), (cb,)), ((), ())), preferred_element_type=F32)


@jax.custom_vjp
def mm(a, b):
    return _dg(a, b, 1, 0)


mm.defvjp(lambda a, b: (_dg(a, b, 1, 0), (a, b)),
          lambda r, g: (_dg(g, r[1], 1, 1), _dg(r[0], g, 0, 0)))


@jax.custom_vjp
def mm_nt(a, b):
    return _dg(a, b, 1, 1)


mm_nt.defvjp(lambda a, b: (_dg(a, b, 1, 1), (a, b)),
             lambda r, g: (_dg(g, r[1], 1, 0), _dg(g, r[0], 0, 0)))


@jax.custom_vjp
def mm_tn(a, b):
    return _dg(a, b, 0, 0)


mm_tn.defvjp(lambda a, b: (_dg(a, b, 0, 0), (a, b)),
             lambda r, g: (_dg(r[1], g, 1, 1), _dg(r[0], g, 1, 0)))


def _split3(x):
    hi = x.astype(BF16)
    r1 = x - hi.astype(F32)
    mid = r1.astype(BF16)
    lo = (r1 - mid.astype(F32)).astype(BF16)
    return hi, mid, lo


def _dot3(m, x):
    hi, mid, lo = _split3(x)
    d = lambda p: lax.dot_general(m, p, (((1,), (0,)), ((), ())), preferred_element_type=F32)
    return d(hi) + d(mid) + d(lo)


@jax.custom_vjp
def cumdot(m, mt, x):
    return _dot3(m, x)


cumdot.defvjp(lambda m, mt, x: (_dot3(m, x), (m, mt)),
              lambda r, g: (jnp.zeros_like(r[0]), jnp.zeros_like(r[1]), _dot3(r[1], g)))


def rmsnorm(x, g):
    return x * lax.rsqrt(jnp.mean(x * x, axis=-1, keepdims=True) + EPS) * g


def gelu(x):
    return 0.5 * x * (1.0 + lax.erf(x * 0.7071067811865476))


def sigmoid(x):
    return 1.0 / (1.0 + jnp.exp(-x))


def log_sigmoid(x):
    return jnp.minimum(x, 0.0) - jnp.log(1.0 + jnp.exp(-jnp.abs(x)))


def gmlp_heads(params, pieces):
    u = [[gelu(p[0]) for p in ch] for ch in pieces]
    v = [[gelu(p[1]) for p in ch] for ch in pieces]
    mu = [[jnp.mean(x, axis=-1, keepdims=True) for x in ch] for ch in v]
    var = [[jnp.mean(jnp.square(x - m), axis=-1, keepdims=True) for x, m in zip(cv, cm)] for cv, cm in zip(v, mu)]
    vn = [[(x - m) * lax.rsqrt(s + EPS) * pr[2] + pr[3] for x, m, s, pr in zip(cv, cm, cs, params)]
          for cv, cm, cs in zip(v, mu, var)]
    mix = [[mm(pr[0], x) + pr[1] for x, pr in zip(ch, params)] for ch in vn]
    return [[a * b for a, b in zip(cu, cx)] for cu, cx in zip(u, mix)]


def outb_head(o, pg, g):
    return rmsnorm(o, g) * (pg * sigmoid(pg))


def ffn_act(zg, zv):
    return zg * sigmoid(zg) * zv


def _tri(reverse):
    r = lax.broadcasted_iota(jnp.int32, (CH, CH), 0)
    c = lax.broadcasted_iota(jnp.int32, (CH, CH), 1)
    if reverse:
        cm, sm = c >= r, c > r
    else:
        cm, sm = c <= r, c <= r
    one = jnp.ones((), BF16)
    zero = jnp.zeros((), BF16)
    return jnp.where(cm, one, zero), jnp.where(cm.T, one, zero), sm


def gla_pair(consts, wg, bg, st0, st1, *chunks):
    m, mt, smask, lm0, lm1 = consts
    ch = [chunks[5 * i:5 * i + 5] for i in range(len(chunks) // 5)]
    la = [log_sigmoid(mm(c[0], wg) + bg) * (1.0 / 16.0) for c in ch]
    cum = [cumdot(m, mt, x) for x in la]
    tot = [jnp.sum(x, axis=0, keepdims=True) for x in la]
    q_dec = [(c[1] * 0.125) * jnp.exp(cm) for c, cm in zip(ch, cum)]
    k_inv = [c[2] * jnp.exp(-cm) for c, cm in zip(ch, cum)]
    k_end = [c[2] * jnp.exp(t - cm) for c, t, cm in zip(ch, tot, cum)]
    s = [[jnp.where(smask, mm_nt(qd * lm, ki), 0.0) for lm in (lm0, lm1)] for qd, ki in zip(q_dec, k_inv)]
    o_in = [[mm(si[h], c[3 + h]) for h in (0, 1)] for si, c in zip(s, ch)]
    ds = [[mm_tn(c[3 + h], ke * lm) for h, lm in ((0, lm0), (1, lm1))] for c, ke in zip(ch, k_end)]
    sts = [(st0, st1)]
    for t, d in zip(tot, ds):
        dec = jnp.exp(t)
        sts.append((sts[-1][0] * dec + d[0], sts[-1][1] * dec + d[1]))
    outs = []
    for qd, oi, st in zip(q_dec, o_in, sts):
        outs += [oi[0] + mm_nt(qd, st[0]), oi[1] + mm_nt(qd, st[1])]
    return (*outs, sts[-1][0], sts[-1][1])


def _lane_masks():
    lane = lax.broadcasted_iota(jnp.int32, (1, 128), 1)
    return (lane < 64).astype(F32), (lane >= 64).astype(F32)


def _cparams(n_axes=1):
    return pltpu.CompilerParams(dimension_semantics=("arbitrary",) * n_axes, vmem_limit_bytes=VMEM_LIMIT)


def _full(a):
    nd = a.ndim
    return pl.BlockSpec(a.shape, lambda *_: (0,) * nd)


def _rows(tm, w, cb=0, rev_n=None):
    if rev_n is None:
        return pl.BlockSpec((tm, w), lambda i: (i, cb))
    return pl.BlockSpec((tm, w), lambda i: (rev_n - 1 - i, cb))


def _call(body, name, grid, in_specs, out_specs, out_shape, scratch=(), n_axes=1):
    return pl.pallas_call(body, name=name, grid=grid, in_specs=in_specs, out_specs=out_specs, out_shape=out_shape,
                          scratch_shapes=list(scratch), compiler_params=_cparams(n_axes))


def _sds(shape, dt=F32):
    return jax.ShapeDtypeStruct(shape, dt)


def norm_matmul(x, g, w, tm, name, ydt=F32):
    T, n = x.shape[0], w.shape[1]

    def body(x_ref, g_ref, w_ref, y_ref, h_ref):
        hb = rmsnorm(x_ref[...], g_ref[...]).astype(MX)
        h_ref[...] = hb
        y_ref[...] = jnp.dot(hb, w_ref[...], preferred_element_type=F32).astype(ydt)

    return _call(body, name, (T // tm,), [_rows(tm, D), _full(g), _full(w)],
                 [_rows(tm, n), _rows(tm, D)], [_sds((T, n), ydt), _sds((T, D), MX)])(x, g, w)


CPB = 8


def _chunk(c):
    return slice(c * CH, (c + 1) * CH)


def gmlp_fwd(p, ws, bs, lg, lb):
    T = p.shape[0]
    tm = CPB * CH

    def body(pa_ref, ws_ref, bs_ref, lg_ref, lb_ref, o_ref):
        params = [(ws_ref[h], bs_ref[h], lg_ref[h], lb_ref[h]) for h in range(4)]
        pieces = [[(pa_ref[_chunk(c), h * 128:(h + 1) * 128], pa_ref[_chunk(c), 512 + h * 128:512 + (h + 1) * 128])
                   for h in range(4)] for c in range(CPB)]
        out = gmlp_heads(params, pieces)
        for c in range(CPB):
            for h in range(4):
                o_ref[_chunk(c), h * 128:(h + 1) * 128] = out[c][h].astype(MX)

    return _call(body, "gmlp_fwd", (T // tm,), [_rows(tm, 1024), _full(ws), _full(bs), _full(lg), _full(lb)],
                 _rows(tm, 512), _sds((T, 512), MX))(p, ws, bs, lg, lb)


def _gla_in_specs(tm, n, rev):
    r = n if rev else None
    return [_rows(tm, 256, 4, r), _rows(tm, 256, 5, r), _rows(tm, 512, 3, r), _rows(tm, 128, 20, r)]


def gla_fwd(p, wg, bg, reverse):
    T = p.shape[0]
    tm = CPB * CH
    n = T // tm
    rev = n if reverse else None

    def body(q_ref, k_ref, v_ref, r_ref, wg_ref, bg_ref, o_ref, ss_ref, st_ref):
        @pl.when(pl.program_id(0) == 0)
        def _():
            st_ref[...] = jnp.zeros_like(st_ref)

        consts = _tri(reverse) + _lane_masks()
        order = list(reversed(range(CPB))) if reverse else list(range(CPB))
        ss_ref[0] = st_ref[...]
        for j in range(2):
            sl = slice(j * 128, (j + 1) * 128)
            v0s, v1s = slice(256 * j, 256 * j + 128), slice(256 * j + 128, 256 * j + 256)
            chunks = []
            for c in order:
                rows = _chunk(c)
                chunks += [r_ref[rows, :], q_ref[rows, sl], k_ref[rows, sl], v_ref[rows, v0s], v_ref[rows, v1s]]
            res = gla_pair(consts, wg_ref[:, sl], bg_ref[:, sl], st_ref[2 * j], st_ref[2 * j + 1], *chunks)
            for i, c in enumerate(order):
                o_ref[_chunk(c), v0s] = res[2 * i]
                o_ref[_chunk(c), v1s] = res[2 * i + 1]
            st_ref[2 * j] = res[-2]
            st_ref[2 * j + 1] = res[-1]

    ss_spec = pl.BlockSpec((1, 4, 128, 128), (lambda i: (n - 1 - i, 0, 0, 0)) if reverse else (lambda i: (i, 0, 0, 0)))
    return _call(body, "gla_fwd_r" if reverse else "gla_fwd_f", (n,),
                 _gla_in_specs(tm, n, reverse) + [_full(wg), _full(bg)],
                 [_rows(tm, 512, 0, rev), ss_spec], [_sds((T, 512)), _sds((n, 4, 128, 128))],
                 scratch=[pltpu.VMEM((4, 128, 128), F32)])(p, p, p, p, wg, bg)


def mix_out(x, of, ob, p, outa, gg, w_out, tm):
    T = x.shape[0]

    def body(x_ref, of_ref, ob_ref, pg_ref, oa_ref, gg_ref, w_ref, x1_ref, mx_ref):
        mx_ref[:, 0:512] = oa_ref[...]
        for h in range(4):
            sl = slice(h * 128, (h + 1) * 128)
            mx_ref[:, 512 + h * 128:512 + (h + 1) * 128] = outb_head(
                of_ref[:, sl] + ob_ref[:, sl], pg_ref[:, sl], gg_ref[h]).astype(MX)
        x1_ref[...] = x_ref[...] + jnp.dot(mx_ref[...], w_ref[...], preferred_element_type=F32)

    return _call(body, "mix_out", (T // tm,),
                 [_rows(tm, D), _rows(tm, 512), _rows(tm, 512), _rows(tm, 512, 4), _rows(tm, 512), _full(gg), _full(w_out)],
                 [_rows(tm, D), _rows(tm, 1024)], [_sds((T, D)), _sds((T, 1024), MX)])(x, of, ob, p, outa, gg, w_out)


HALO = 16


def _halo_specs(T, tm, w):
    nb = T // HALO
    r = tm // HALO
    return [pl.BlockSpec((tm, w), lambda i: (i, 0)),
            pl.BlockSpec((HALO, w), lambda i: (jnp.maximum(i * r - 1, 0), 0)),
            pl.BlockSpec((HALO, w), lambda i: (jnp.minimum((i + 1) * r, nb - 1), 0))]


def _shifted(main, prev, nxt, i, nsteps):
    tm = main.shape[0]
    row = lax.broadcasted_iota(jnp.int32, (tm, 1), 0)
    pr = jnp.where(i > 0, prev[HALO - 1:HALO, :].astype(F32), 0.0)
    nx = jnp.where(i < nsteps - 1, nxt[0:1, :].astype(F32), 0.0)
    dn = jnp.where(row == 0, pr, pltpu.roll(main, 1, 0))
    up = jnp.where(row == tm - 1, nx, pltpu.roll(main, tm - 1, 0))
    return dn, up


def ffn_down(x1, zu, cw, cb, w_down, tm):
    T = x1.shape[0]
    ns = T // tm

    def body(x_ref, zu_ref, zp_ref, zn_ref, cw_ref, cb_ref, w_ref, x2_ref, z_ref, a_ref):
        zu = zu_ref[...].astype(F32)
        dn, up = _shifted(zu, zp_ref[...], zn_ref[...], pl.program_id(0), ns)
        z = cb_ref[...] + dn * cw_ref[0:1, :] + zu * cw_ref[1:2, :] + up * cw_ref[2:3, :]
        z_ref[...] = z.astype(MX)
        a = ffn_act(z[:, :DFF], z[:, DFF:]).astype(MX)
        a_ref[...] = a
        x2_ref[...] = x_ref[...] + jnp.dot(a, w_ref[...], preferred_element_type=F32)

    return _call(body, "ffn_down", (ns,), [_rows(tm, D)] + _halo_specs(T, tm, NUP) + [_full(cw), _full(cb), _full(w_down)],
                 [_rows(tm, D), _rows(tm, NUP), _rows(tm, DFF)],
                 [_sds((T, D)), _sds((T, NUP), MX), _sds((T, DFF), MX)])(x1, zu, zu, zu, cw, cb, w_down)


def loss_head(x, g, tgt, tm):
    T = x.shape[0]

    def body(x_ref, g_ref, t_ref, l_ref, dx_ref, dg_ref):
        @pl.when(pl.program_id(0) == 0)
        def _():
            l_ref[...] = jnp.zeros_like(l_ref)
            dg_ref[...] = jnp.zeros_like(dg_ref)

        y, vjp = jax.vjp(rmsnorm, x_ref[...], g_ref[...])
        err = y - t_ref[...]
        l_ref[...] += jnp.sum(err * err, axis=0, keepdims=True)
        dx, dg = vjp(err * (1.0 / D))
        dx_ref[...] = dx
        dg_ref[...] += dg

    return _call(body, "loss_head", (T // tm,), [_rows(tm, D), _full(g), _rows(tm, D)],
                 [_full(g), _rows(tm, D), _full(g)], [_sds((1, D)), _sds((T, D)), _sds((1, D))])(x, g, tgt)


def ffn_down_bwd(dx2, z, w_down, tm):
    T = dx2.shape[0]

    def body(dx_ref, z_ref, w_ref, dz_ref):
        da = _dg(dx_ref[...], w_ref[...], 1, 1)
        zg, zv = z_ref[:, :DFF].astype(F32), z_ref[:, DFF:].astype(F32)
        s = sigmoid(zg)
        sz = zg * s
        dz_ref[:, :DFF] = (da * zv * (s + sz * (1.0 - s))).astype(MX)
        dz_ref[:, DFF:] = (da * sz).astype(MX)

    return _call(body, "ffn_down_bwd", (T // tm,), [_rows(tm, D), _rows(tm, NUP), _full(w_down)],
                 _rows(tm, NUP), _sds((T, NUP), MX))(dx2, z, w_down)


def ffn_up_bwd(dz, zu, cw, w_up, x1, g, dres, tm):
    T = dz.shape[0]
    ns = T // tm
    cwid = 512

    def body(dz_ref, dp_ref, dn_ref, zu_ref, cw_ref, w_ref, x_ref, g_ref, dr_ref,
             dzu_ref, dx_ref, dg_ref, dcw_ref, dcb_ref):
        i = pl.program_id(0)

        @pl.when(i == 0)
        def _():
            for r in (dg_ref, dcw_ref, dcb_ref):
                r[...] = jnp.zeros_like(r)

        row = lax.broadcasted_iota(jnp.int32, (tm, 1), 0)
        dh = jnp.zeros((tm, D), F32)
        for c0 in range(0, NUP, cwid):
            cs = slice(c0, c0 + cwid)
            dz = dz_ref[:, cs].astype(F32)
            zu = zu_ref[:, cs].astype(F32)
            pr = jnp.where(i > 0, dp_ref[HALO - 1:HALO, cs].astype(F32), 0.0)
            nx = jnp.where(i < ns - 1, dn_ref[0:1, cs].astype(F32), 0.0)
            ddn = jnp.where(row == 0, pr, pltpu.roll(dz, 1, 0))
            dup = jnp.where(row == tm - 1, nx, pltpu.roll(dz, tm - 1, 0))
            dzu = (dup * cw_ref[0:1, cs] + dz * cw_ref[1:2, cs] + ddn * cw_ref[2:3, cs]).astype(MX)
            dzu_ref[:, cs] = dzu
            dcw_ref[0:1, cs] += jnp.sum(zu * dup, axis=0, keepdims=True)
            dcw_ref[1:2, cs] += jnp.sum(zu * dz, axis=0, keepdims=True)
            dcw_ref[2:3, cs] += jnp.sum(zu * ddn, axis=0, keepdims=True)
            dcb_ref[:, cs] += jnp.sum(dz, axis=0, keepdims=True)
            dh = dh + _dg(dzu, w_ref[:, cs], 1, 1)
        _, vjp = jax.vjp(rmsnorm, x_ref[...], g_ref[...])
        dx, dg = vjp(dh)
        dx_ref[...] = dr_ref[...] + dx
        dg_ref[...] += dg

    return _call(body, "ffn_up_bwd", (ns,),
                 _halo_specs(T, tm, NUP) + [_rows(tm, NUP), _full(cw), _full(w_up), _rows(tm, D), _full(g), _rows(tm, D)],
                 [_rows(tm, NUP), _rows(tm, D), _full(g), _full(cw), pl.BlockSpec((1, NUP), lambda i: (0, 0))],
                 [_sds((T, NUP), MX), _sds((T, D)), _sds((1, D)), _sds((3, NUP)), _sds((1, NUP))])(
                     dz, dz, dz, zu, cw, w_up, x1, g, dres)


def nt_normbwd(dys, w, x, g, dres, tm, name):
    T = x.shape[0]
    n = len(dys)
    offs = [sum(d.shape[1] for d in dys[:i]) for i in range(n + 1)]

    def body(*refs):
        dy_refs, (w_ref, x_ref, g_ref, dr_ref, dx_ref, dg_ref) = refs[:n], refs[n:]

        @pl.when(pl.program_id(0) == 0)
        def _():
            dg_ref[...] = jnp.zeros_like(dg_ref)

        dh = _dg(dy_refs[0][...], w_ref[:, offs[0]:offs[1]], 1, 1)
        for i in range(1, n):
            dh = dh + _dg(dy_refs[i][...], w_ref[:, offs[i]:offs[i + 1]], 1, 1)
        _, vjp = jax.vjp(rmsnorm, x_ref[...], g_ref[...])
        dx, dg = vjp(dh)
        dx_ref[...] = dr_ref[...] + dx
        dg_ref[...] += dg

    return _call(body, name, (T // tm,),
                 [_rows(tm, d.shape[1]) for d in dys] + [_full(w), _rows(tm, D), _full(g), _rows(tm, D)],
                 [_rows(tm, D), _full(g)], [_sds((T, D)), _sds((1, D))])(*dys, w, x, g, dres)


def matmul_tn(a, b, tt, tn, name):
    T, k = a.shape
    n = b.shape[1]
    last = T // tt - 1

    def body(a_ref, b_ref, o_ref, acc_ref):
        @pl.when(pl.program_id(1) == 0)
        def _():
            acc_ref[...] = jnp.zeros_like(acc_ref)

        acc_ref[...] += _dg(a_ref[...], b_ref[...], 0, 0)

        @pl.when(pl.program_id(1) == last)
        def _():
            o_ref[...] = acc_ref[...].astype(MX)

    return _call(body, name, (n // tn, T // tt),
                 [pl.BlockSpec((tt, k), lambda j, i: (i, 0)), pl.BlockSpec((tt, tn), lambda j, i: (i, j))],
                 pl.BlockSpec((k, tn), lambda j, i: (0, j)), _sds((k, n), MX), scratch=[pltpu.VMEM((k, tn), F32)],
                 n_axes=2)(a, b)


def mix_out_bwd(dx1, w_out, of, ob, p, gg, tm):
    T = dx1.shape[0]

    def body(dx_ref, w_ref, of_ref, ob_ref, pg_ref, gg_ref, da_ref, do_ref, dpg_ref, dgg_ref):
        @pl.when(pl.program_id(0) == 0)
        def _():
            dgg_ref[...] = jnp.zeros_like(dgg_ref)

        dxb = dx_ref[...].astype(MX)
        da_ref[...] = _dg(dxb, w_ref[0:512, :], 1, 1)
        for h in range(4):
            sl = slice(h * 128, (h + 1) * 128)
            dm = _dg(dxb, w_ref[512 + h * 128:512 + (h + 1) * 128, :], 1, 1)
            _, vjp = jax.vjp(outb_head, of_ref[:, sl] + ob_ref[:, sl], pg_ref[:, sl], gg_ref[h])
            do, dpg, dg = vjp(dm)
            do_ref[:, sl] = do
            dpg_ref[:, sl] = dpg
            dgg_ref[h] += dg

    return _call(body, "mix_out_bwd", (T // tm,),
                 [_rows(tm, D), _full(w_out), _rows(tm, 512), _rows(tm, 512), _rows(tm, 512, 4), _full(gg)],
                 [_rows(tm, 512), _rows(tm, 512), _rows(tm, 512), _full(gg)],
                 [_sds((T, 512)), _sds((T, 512)), _sds((T, 512)), _sds(gg.shape)])(dx1, w_out, of, ob, p, gg)


def gla_bwd(p, wg, bg, ss, do, reverse, merge=None):
    T = p.shape[0]
    tm = CPB * CH
    n = T // tm
    rev = not reverse
    rn = n if rev else None

    def body(*refs):
        q_ref, k_ref, v_ref, r_ref, wg_ref, bg_ref, ss_ref, do_ref = refs[:8]
        if merge is None:
            dq_ref, dk_ref, dv_ref, dr_ref, dwg_ref, dbg_ref, dst_ref = refs[8:]
        else:
            mq_ref, mk_ref, mv_ref, mr_ref, mg_ref, out_ref, dwg_ref, dbg_ref, dst_ref, drs_ref = refs[8:]
            out_ref[:, 1024:1536] = mg_ref[...].astype(MX)

        @pl.when(pl.program_id(0) == 0)
        def _():
            dst_ref[...] = jnp.zeros_like(dst_ref)
            dwg_ref[...] = jnp.zeros_like(dwg_ref)
            dbg_ref[...] = jnp.zeros_like(dbg_ref)

        consts = _tri(reverse) + _lane_masks()
        order = list(reversed(range(CPB))) if reverse else list(range(CPB))
        for j in range(2):
            sl = slice(j * 128, (j + 1) * 128)
            v0s, v1s = slice(256 * j, 256 * j + 128), slice(256 * j + 128, 256 * j + 256)
            chunks, dout = [], []
            for c in order:
                rows = _chunk(c)
                chunks += [r_ref[rows, :], q_ref[rows, sl], k_ref[rows, sl], v_ref[rows, v0s], v_ref[rows, v1s]]
                dout += [do_ref[rows, v0s], do_ref[rows, v1s]]
            _, vjp = jax.vjp(functools.partial(gla_pair, consts), wg_ref[:, sl], bg_ref[:, sl],
                             ss_ref[0, 2 * j], ss_ref[0, 2 * j + 1], *chunks)
            g = vjp((*dout, dst_ref[2 * j], dst_ref[2 * j + 1]))
            dwg_ref[:, sl] += g[0]
            dbg_ref[:, sl] += g[1]
            dst_ref[2 * j] = g[2]
            dst_ref[2 * j + 1] = g[3]
            for i, c in enumerate(order):
                rows = _chunk(c)
                dr, dq, dk, dv0, dv1 = g[4 + 5 * i:9 + 5 * i]
                if merge is None:
                    if j == 0:
                        dr_ref[rows, :] = dr
                    else:
                        dr_ref[rows, :] += dr
                    dq_ref[rows, sl] = dq
                    dk_ref[rows, sl] = dk
                    dv_ref[rows, v0s] = dv0
                    dv_ref[rows, v1s] = dv1
                else:
                    if j == 0:
                        drs_ref[rows, :] = mr_ref[rows, :] + dr
                    else:
                        out_ref[rows, 1536:1664] = (drs_ref[rows, :] + dr).astype(MX)
                    out_ref[rows, sl] = (mq_ref[rows, sl] + dq).astype(MX)
                    out_ref[rows, 256 + 128 * j:384 + 128 * j] = (mk_ref[rows, sl] + dk).astype(MX)
                    out_ref[rows, 512 + 256 * j:640 + 256 * j] = (mv_ref[rows, v0s] + dv0).astype(MX)
                    out_ref[rows, 640 + 256 * j:768 + 256 * j] = (mv_ref[rows, v1s] + dv1).astype(MX)

    ss_spec = pl.BlockSpec((1, 4, 128, 128), (lambda i: (n - 1 - i, 0, 0, 0)) if rev else (lambda i: (i, 0, 0, 0)))
    ins = [p, p, p, p, wg, bg, ss, do]
    in_specs = _gla_in_specs(tm, n, rev) + [_full(wg), _full(bg), ss_spec, _rows(tm, 512, 0, rn)]
    scratch = [pltpu.VMEM((4, 128, 128), F32)]
    if merge is None:
        out_specs = [_rows(tm, 256, 0, rn), _rows(tm, 256, 0, rn), _rows(tm, 512, 0, rn), _rows(tm, 128, 0, rn)]
        out_shape = [_sds((T, 256)), _sds((T, 256)), _sds((T, 512)), _sds((T, 128))]
    else:
        ins += list(merge)
        in_specs += [_rows(tm, a.shape[1], 0, rn) for a in merge]
        out_specs, out_shape = [_rows(tm, 1664, 0, rn)], [_sds((T, 1664), MX)]
        scratch.append(pltpu.VMEM((tm, 128), F32))
    return _call(body, "gla_bwd_r" if reverse else "gla_bwd_f", (n,), in_specs, out_specs + [_full(wg), _full(bg)],
                 out_shape + [_sds(wg.shape), _sds(bg.shape)], scratch=scratch)(*ins)


def gmlp_bwd(p, douta, ws, bs, lg, lb):
    T = p.shape[0]
    cpb = 4
    tm = cpb * CH

    def body(pa_ref, do_ref, ws_ref, bs_ref, lg_ref, lb_ref, dpa_ref, dws_ref, dbs_ref, dlg_ref, dlb_ref):
        @pl.when(pl.program_id(0) == 0)
        def _():
            for r in (dws_ref, dbs_ref, dlg_ref, dlb_ref):
                r[...] = jnp.zeros_like(r)

        us = [slice(h * 128, (h + 1) * 128) for h in range(4)]
        vs = [slice(512 + h * 128, 512 + (h + 1) * 128) for h in range(4)]
        params = [(ws_ref[h], bs_ref[h], lg_ref[h], lb_ref[h]) for h in range(4)]
        pieces = [[(pa_ref[_chunk(c), us[h]], pa_ref[_chunk(c), vs[h]]) for h in range(4)] for c in range(cpb)]
        _, vjp = jax.vjp(gmlp_heads, params, pieces)
        dparams, dpieces = vjp([[do_ref[_chunk(c), us[h]] for h in range(4)] for c in range(cpb)])
        for h in range(4):
            for r, a in zip((dws_ref, dbs_ref, dlg_ref, dlb_ref), dparams[h]):
                r[h] += a
            for c in range(cpb):
                dpa_ref[_chunk(c), us[h]] = dpieces[c][h][0].astype(MX)
                dpa_ref[_chunk(c), vs[h]] = dpieces[c][h][1].astype(MX)

    return _call(body, "gmlp_bwd", (T // tm,),
                 [_rows(tm, 1024), _rows(tm, 512), _full(ws), _full(bs), _full(lg), _full(lb)],
                 [_rows(tm, 1024), _full(ws), _full(bs), _full(lg), _full(lb)],
                 [_sds((T, 1024), MX), _sds(ws.shape), _sds(bs.shape), _sds(lg.shape), _sds(lb.shape)])(p, douta, ws, bs, lg, lb)


def _gate_pad(w, row0):
    return jnp.zeros((128, 256), F32).at[row0:row0 + 16].set(w)


def local_step(x, tgt, W, get_big, emit, tm=256, tmm=512):
    saved = []
    for l in range(NL):
        s = {"x": x}
        s.update(get_big(l, "in", x))
        p, s["h"] = norm_matmul(x, W["g_mix"][l][None], s["w_in"], tmm, "mix_in")
        s["p"] = p
        ws, bs = W["w_s"][l], W["b_s"][l][:, :, None]
        lg, lb = W["ln_g"][l][:, None, :], W["ln_b"][l][:, None, :]
        outa = gmlp_fwd(p, ws, bs, lg, lb)
        wgf, wgb = _gate_pad(W["w_gate_f"][l], 0), _gate_pad(W["w_gate_b"][l], 16)
        bgf, bgb = W["b_gate_f"][l][None], W["b_gate_b"][l][None]
        s["of"], s["ssf"] = gla_fwd(p, wgf, bgf, False)
        s["ob"], s["ssb"] = gla_fwd(p, wgb, bgb, True)
        s.update(get_big(l, "rest", s["ob"]))
        gg = W["g_gla"][l][:, None, :]
        x1, s["mixed"] = mix_out(x, s["of"], s["ob"], p, outa, gg, s["w_out"], tmm)
        s["x1"] = x1
        s["zu"], s["h2"] = norm_matmul(x1, W["g_ffn"][l][None], s["w_up"], tmm, "ffn_up", MX)
        x, s["z"], s["a"] = ffn_down(x1, s["zu"], W["conv_w"][l], W["conv_b"][l][None], s["w_down"], tm)
        saved.append(s)

    lsum, dx, dgf = loss_head(x, W["g_final"][None], tgt, tmm)
    G = {k: [None] * NL for k in _SMALL if k != "g_final"}
    tok = jnp.zeros((1, 1), F32)
    for l in reversed(range(NL)):
        s = saved[l]
        g_down = matmul_tn(s["a"], dx, min(1024, tmm * 2), 512, "dw_down")
        dz = ffn_down_bwd(dx, s["z"], s["w_down"], tm)
        dzu, dx1, dg, G["conv_w"][l], dcb = ffn_up_bwd(dz, s["zu"], W["conv_w"][l] + tok, s["w_up"], s["x1"],
                                                       W["g_ffn"][l][None], dx, tm)
        G["conv_b"][l], G["g_ffn"][l] = dcb[0], dg[0]
        g_up = matmul_tn(s["h2"], dzu, min(1024, tmm * 2), 1408, "dw_up")
        tok = emit(l, "A", {"w_down": g_down, "w_up": g_up})
        g_out = matmul_tn(s["mixed"], dx1, min(1024, tmm * 2), 1024, "dw_out")
        gg = W["g_gla"][l][:, None, :] + tok
        douta, do, dpg, dgg = mix_out_bwd(dx1, s["w_out"], s["of"], s["ob"], s["p"], gg, tmm)
        G["g_gla"][l] = dgg[:, 0, :]
        wgf, wgb = _gate_pad(W["w_gate_f"][l], 0), _gate_pad(W["w_gate_b"][l], 16)
        bgf, bgb = W["b_gate_f"][l][None], W["b_gate_b"][l][None]
        dqf, dkf, dvf, drf, dwgf, dbgf = gla_bwd(s["p"], wgf, bgf, s["ssf"], do, False)
        dpb, dwgb, dbgb = gla_bwd(s["p"], wgb, bgb, s["ssb"], do, True, merge=(dqf, dkf, dvf, drf, dpg))
        G["w_gate_f"][l], G["b_gate_f"][l] = dwgf[0:16], dbgf[0]
        G["w_gate_b"][l], G["b_gate_b"][l] = dwgb[16:32], dbgb[0]
        ws, bs = W["w_s"][l], W["b_s"][l][:, :, None]
        lg, lb = W["ln_g"][l][:, None, :], W["ln_b"][l][:, None, :]
        dpa, G["w_s"][l], dbs, dlg, dlb = gmlp_bwd(s["p"], douta, ws, bs, lg, lb)
        G["b_s"][l], G["ln_g"][l], G["ln_b"][l] = dbs[:, :, 0], dlg[:, 0, :], dlb[:, 0, :]
        tt = min(1024, tmm * 2)
        g_in = jnp.concatenate([matmul_tn(s["h"], dpa, tt, 1024, "dw_in_a"),
                                matmul_tn(s["h"], dpb, tt, 1664, "dw_in_b")], axis=1)
        tok = emit(l, "B", {"w_out": g_out, "w_in": g_in})
        dx, dg = nt_normbwd([dpa, dpb], s["w_in"], s["x"], W["g_mix"][l][None] + tok, dx1, tmm, "mix_in_bwd")
        G["g_mix"][l] = dg[0]
    G = {k: jnp.stack(v) for k, v in G.items()}
    G["g_final"] = dgf[0]
    return lsum, dx, G


def cast_bf16(a, tr):
    r, c = a.shape

    def body(a_ref, o_ref):
        o_ref[...] = a_ref[...].astype(BF16)

    return _call(body, "cast_bf16", (r // tr,), [_rows(tr, c)], _rows(tr, c), _sds((r, c), BF16))(a)


def sum_lead(y, tr):
    n, rr, cc = y.shape

    def body(y_ref, o_ref):
        acc = y_ref[0].astype(F32)
        for k in range(1, n):
            acc = acc + y_ref[k].astype(F32)
        o_ref[...] = acc

    return _call(body, "sum_lead", (rr // tr,), [pl.BlockSpec((n, tr, cc), lambda i: (0, i, 0))],
                 _rows(tr, cc), _sds((rr, cc)))(y)


def sum_parts(land, grad, k, chipvec, tr):
    _, rr, cc = land.shape
    nb = rr // tr

    def body(c_ref, l_ref, g_ref, o_ref):
        mine = g_ref[...].astype(F32)
        acc = None
        for j in range(4):
            part = jnp.where(c_ref[0] == j, mine, l_ref[j].astype(F32))
            acc = part if acc is None else acc + part
        o_ref[...] = acc

    gs = pltpu.PrefetchScalarGridSpec(
        num_scalar_prefetch=1, grid=(nb,),
        in_specs=[pl.BlockSpec((4, tr, cc), lambda i, c: (0, i, 0)), _part_spec(k, tr, nb)],
        out_specs=pl.BlockSpec((tr, cc), lambda i, c: (i, 0)))
    return pl.pallas_call(body, name="sum_parts", grid_spec=gs, out_shape=_sds((rr, cc)),
                          compiler_params=_cparams(1))(chipvec, land, grad)


def adamw(w, ga, gb, m, v, tr):
    r, c = w.shape

    def body(w_ref, ga_ref, gb_ref, m_ref, v_ref, g_ref, d_ref, nm_ref, nv_ref):
        gr = ga_ref[...] + gb_ref[...]
        g_ref[...] = gr
        nm = ADAM_B1 * m_ref[...] + (1.0 - ADAM_B1) * gr
        nv = ADAM_B2 * v_ref[...] + (1.0 - ADAM_B2) * jnp.square(gr)
        m_hat = nm / (1.0 - ADAM_B1 ** ADAM_STEP)
        v_hat = nv / (1.0 - ADAM_B2 ** ADAM_STEP)
        d_ref[...] = -ADAM_LR * (m_hat / (jnp.sqrt(v_hat) + ADAM_EPS) + ADAM_WD * w_ref[...])
        nm_ref[...] = nm
        nv_ref[...] = nv

    sp = _rows(tr, c)
    return _call(body, "adamw", (r // tr,), [sp] * 5, [sp] * 4, [_sds((r, c))] * 4)(w, ga, gb, m, v)


MESH = pl.DeviceIdType.MESH
ANY = pl.BlockSpec(memory_space=pl.ANY)
N_BIG = 4


def _pos():
    return lax.axis_index("x"), lax.axis_index("y"), lax.axis_index("c")


def _other_chips(x, y):
    return [(1 - x, y), (x, 1 - y), (1 - x, 1 - y)]


def _rcopy(src, dst, send_sems, recv_sems, k, to):
    return pltpu.make_async_remote_copy(src_ref=src, dst_ref=dst, send_sem=send_sems.at[k], recv_sem=recv_sems.at[k],
                                        device_id=to, device_id_type=MESH)


def allgather8(xs):
    m, n = xs.shape

    def body(x_ref, out_ref, send_sems, recv_sems, local_sem):
        x, y, c = _pos()
        me, sibling = (x, y, c), (x, y, 1 - c)
        chips = _other_chips(x, y)

        def rows(px, py, pc):
            return out_ref.at[pl.ds((4 * px + 2 * py + pc) * m, m), :]

        def copy(k, block, to, src=None):
            return _rcopy(rows(*block) if src is None else src, rows(*block), send_sems, recv_sems, k, to)

        mine = pltpu.make_async_copy(x_ref, rows(*me), local_sem)
        mine.start()
        first = [copy(0, me, sibling, src=x_ref)]
        first += [copy(1 + j, me, (*chip, c), src=x_ref) for j, chip in enumerate(chips)]
        for cp in first:
            cp.start()
        passed = [copy(4 + j, (*chip, c), sibling) for j, chip in enumerate(chips)]
        for j, chip in enumerate(chips):
            copy(1 + j, (*chip, c), me).wait_recv()
            passed[j].start()
        copy(0, sibling, me).wait_recv()
        for j, chip in enumerate(chips):
            copy(4 + j, (*chip, 1 - c), me).wait_recv()
        for cp in first + passed:
            cp.wait_send()
        mine.wait()

    vm = pl.BlockSpec(memory_space=pltpu.VMEM)
    return pl.pallas_call(
        body, name="allgather8", out_shape=_sds((8 * m, n), xs.dtype), in_specs=[vm], out_specs=vm,
        scratch_shapes=[pltpu.SemaphoreType.DMA((7,)), pltpu.SemaphoreType.DMA((7,)), pltpu.SemaphoreType.DMA],
        compiler_params=pltpu.CompilerParams(vmem_limit_bytes=VMEM_LIMIT))(xs)


def _slab(k, ref, j):
    if k == 0:
        return ref.at[j]
    if k == 1:
        return ref.at[pl.ds(256 * j, 256), :]
    if k == 2:
        return ref.at[:, pl.ds(1408 * j, 1408)]
    return ref.at[pl.ds(704 * j, 704), :]


_LAYER_FULL = [(4, 1024, 648), (1024, 1024), (1024, NUP), (DFF, 1024)]
_LAYER_SHARD = [(1024, 648), (256, 1024), (1024, 1408), (704, 1024)]
_SHARD_SHAPES = [(NL,) + s for s in _LAYER_SHARD]

HBM = pl.BlockSpec(memory_space=pltpu.HBM)
SEM = pl.BlockSpec(memory_space=pltpu.SEMAPHORE)
VM = pl.BlockSpec(memory_space=pltpu.VMEM)
EFFECT = pltpu.SideEffectType.DATAFLOW_SIDE_EFFECTING
_GW_GROUPS = [[(0, 0)], [(0, 1), (0, 2), (0, 3)]] + [[(l, k) for k in range(N_BIG)] for l in range(1, NL)]
_GW_ORDER = [lk for g in _GW_GROUPS for lk in g]


def _hbm(a):
    return pltpu.with_memory_space_constraint(a, pltpu.HBM)


def _hbm_like(a):
    return pltpu.HBM(a.shape, a.dtype)


def _part_spec(k, tr, nb):
    cc = _LAYER_SHARD[k][1]
    if k == 0:
        return pl.BlockSpec((None, tr, cc), lambda i, c: (c[0], i, 0))
    if k == 2:
        return pl.BlockSpec((tr, cc), lambda i, c: (i, c[0]))
    return pl.BlockSpec((tr, cc), lambda i, c: (c[0] * nb + i, 0))


def place_own(shard, landing, l, k, chipvec, tr):
    rr, cc = _LAYER_SHARD[k]
    nb = rr // tr

    def body(c_ref, s_ref, l_ref, o_ref):
        o_ref[...] = s_ref[...]

    gs = pltpu.PrefetchScalarGridSpec(
        num_scalar_prefetch=1, grid=(nb,),
        in_specs=[pl.BlockSpec((None, tr, cc), lambda i, c: (l, i, 0)), ANY], out_specs=_part_spec(k, tr, nb))
    return pl.pallas_call(body, name="place_own", grid_spec=gs, out_shape=_sds(landing.shape, landing.dtype),
                          input_output_aliases={2: 0}, compiler_params=_cparams(1))(chipvec, shard, landing)


def gw_start(shards, landings):
    n = len(_GW_ORDER)

    def body(*refs):
        S, Ld = refs[:N_BIG], refs[N_BIG:N_BIG + n]
        outs = refs[N_BIG + n:]
        send_sems, recv, token = outs[0], outs[1:1 + len(_GW_GROUPS)], outs[-1]
        x, y, c = _pos()
        me = 2 * x + y
        ci = 0
        for gi, grp in enumerate(_GW_GROUPS):
            for t, (l, k) in enumerate(grp):
                land = Ld[_GW_ORDER.index((l, k))]
                for j, (px, py) in enumerate(_other_chips(x, y)):
                    pltpu.make_async_remote_copy(
                        src_ref=S[k].at[l], dst_ref=_slab(k, land, me), send_sem=send_sems.at[ci],
                        recv_sem=recv[gi].at[3 * t + j], device_id=(px, py, c), device_id_type=MESH).start()
                    ci += 1
        token[...] = jnp.zeros_like(token)

    ins = list(shards) + list(landings)
    sems = [pltpu.SemaphoreType.DMA((3 * n,))] + [pltpu.SemaphoreType.DMA((3 * len(g),)) for g in _GW_GROUPS]
    outs = pl.pallas_call(
        body, name="gw_start", out_shape=sems + [_hbm_like(a) for a in ins] + [_sds((8, 128))],
        in_specs=[HBM] * len(ins), out_specs=[SEM] * len(sems) + [HBM] * len(ins) + [VM],
        input_output_aliases={i: len(sems) + i for i in range(len(ins))},
        compiler_params=pltpu.CompilerParams(has_side_effects=EFFECT))(*[_hbm(a) for a in ins])
    ns = len(sems)
    return outs[0], outs[1:ns], outs[ns:ns + N_BIG], outs[ns + N_BIG:ns + len(ins)], outs[-1]


def gw_wait(gi, landings, recv_sems, after, shards=None, send_sems=None):
    grp = _GW_GROUPS[gi]
    n = len(grp)
    last = shards is not None

    def body(*refs):
        Ld, rs = refs[:n], refs[n]
        x, y, c = _pos()
        for t, (l, k) in enumerate(grp):
            for j, (px, py) in enumerate(_other_chips(x, y)):
                region = _slab(k, Ld[t], 2 * px + py)
                pltpu.make_async_remote_copy(src_ref=region, dst_ref=region, send_sem=rs.at[3 * t + j],
                                             recv_sem=rs.at[3 * t + j], device_id=(px, py, c),
                                             device_id_type=MESH).wait_recv()
        if last:
            S, ss = refs[n + 2:n + 2 + N_BIG], refs[n + 2 + N_BIG]
            me = 2 * x + y
            for ci, (l, k) in enumerate(lk for lk in _GW_ORDER for _ in range(3)):
                pltpu.make_async_remote_copy(src_ref=S[k].at[l], dst_ref=_slab(k, Ld[k], me), send_sem=ss.at[ci],
                                             recv_sem=ss.at[ci], device_id=(x, y, c), device_id_type=MESH).wait_send()

    ins = list(landings) + [recv_sems, after]
    specs = [HBM] * n + [SEM, pl.BlockSpec(memory_space=pl.ANY)]
    outs = [_hbm_like(a) for a in landings]
    alias = {i: i for i in range(n)}
    if last:
        ins += list(shards) + [send_sems]
        specs += [HBM] * N_BIG + [SEM]
        outs += [_hbm_like(a) for a in shards]
        alias.update({n + 2 + i: n + i for i in range(N_BIG)})
    res = pl.pallas_call(body, name="gw_wait_%d" % gi, out_shape=outs, in_specs=specs, out_specs=[HBM] * len(outs),
                         input_output_aliases=alias,
                         compiler_params=pltpu.CompilerParams(has_side_effects=EFFECT))(*ins)
    return res[:n]


def ga_start(tag, ks, grads, landings):
    n = len(ks)

    def body(*refs):
        G, Ld = refs[:n], refs[n:2 * n]
        send_sems, recv_sems, token = refs[2 * n], refs[2 * n + 1], refs[-1]
        x, y, c = _pos()
        me = 2 * x + y
        for t, k in enumerate(ks):
            for j, (px, py) in enumerate(_other_chips(x, y)):
                pltpu.make_async_remote_copy(
                    src_ref=_slab(k, G[t], 2 * px + py), dst_ref=Ld[t].at[me], send_sem=send_sems.at[3 * t + j],
                    recv_sem=recv_sems.at[3 * t + j], device_id=(px, py, c), device_id_type=MESH).start()
        token[...] = jnp.zeros_like(token)

    ins = list(grads) + list(landings)
    sems = [pltpu.SemaphoreType.DMA((3 * n,))] * 2
    outs = pl.pallas_call(
        body, name="ga_start_" + tag, out_shape=sems + [_hbm_like(a) for a in ins] + [_sds((8, 128))],
        in_specs=[HBM] * len(ins), out_specs=[SEM, SEM] + [HBM] * len(ins) + [VM],
        input_output_aliases={i: 2 + i for i in range(len(ins))},
        compiler_params=pltpu.CompilerParams(has_side_effects=EFFECT))(*[_hbm(a) for a in ins])
    return outs[0], outs[1], outs[2:2 + n], outs[2 + n:2 + 2 * n], outs[-1]


def ga_wait(tag, ks, send_sems, recv_sems, grads, landings, after):
    n = len(ks)

    def body(*refs):
        G, Ld, ss, rs = refs[:n], refs[n:2 * n], refs[2 * n], refs[2 * n + 1]
        x, y, c = _pos()
        me = 2 * x + y
        for t, k in enumerate(ks):
            for j, (px, py) in enumerate(_other_chips(x, y)):
                pj = 2 * px + py
                cp = pltpu.make_async_remote_copy(
                    src_ref=_slab(k, G[t], pj), dst_ref=Ld[t].at[pj], send_sem=ss.at[3 * t + j],
                    recv_sem=rs.at[3 * t + j], device_id=(px, py, c), device_id_type=MESH)
                cp.wait_send()
                cp.wait_recv()

    ins = list(grads) + list(landings) + [send_sems, recv_sems, after]
    res = pl.pallas_call(
        body, name="ga_wait_" + tag, out_shape=[_hbm_like(a) for a in list(grads) + list(landings)],
        in_specs=[HBM] * (2 * n) + [SEM, SEM, pl.BlockSpec(memory_space=pl.ANY)], out_specs=[HBM] * (2 * n),
        input_output_aliases={i: i for i in range(2 * n)},
        compiler_params=pltpu.CompilerParams(has_side_effects=EFFECT))(*ins)
    return res[:n], res[n:]


def swap4(parts):
    def body(*refs):
        Q, R = refs[:N_BIG], refs[N_BIG:2 * N_BIG]
        send_sems, recv_sems = refs[2 * N_BIG:]
        x, y, c = _pos()
        cps = [_rcopy(Q[k], R[k], send_sems, recv_sems, k, (x, y, 1 - c)) for k in range(N_BIG)]
        for cp in cps:
            cp.start()
        for cp in cps:
            cp.wait()

    return pl.pallas_call(
        body, name="swap4", out_shape=[_sds(s) for s in _SHARD_SHAPES],
        in_specs=[ANY] * N_BIG, out_specs=[ANY] * N_BIG,
        scratch_shapes=[pltpu.SemaphoreType.DMA((N_BIG,)), pltpu.SemaphoreType.DMA((N_BIG,))])(*parts)


_WEIGHTS = ['g_mix', 'w_in', 'w_s', 'b_s', 'ln_g', 'ln_b', 'w_gate_f', 'b_gate_f', 'w_gate_b', 'b_gate_b', 'g_gla',
            'w_out', 'g_ffn', 'w_up', 'conv_w', 'conv_b', 'w_down', 'g_final']
_BIG = ['w_in', 'w_out', 'w_up', 'w_down']
_SMALL = [n for n in _WEIGHTS if n not in _BIG]
_SMALL_SHARDED = {'w_gate_f': 64, 'w_gate_b': 64, 'conv_w': 1408}
_BIG_TR = {'w_in': 512, 'w_out': 256, 'w_up': 256, 'w_down': 352}


def _pack(arrs):
    flat = jnp.concatenate([a.reshape(-1) for a in arrs])
    pad = (-flat.shape[0]) % 1024
    return jnp.pad(flat, (0, pad)).reshape(-1, 128)


def _unpack(buf, shapes):
    flat = buf.reshape(-1)
    out, o = [], 0
    for s in shapes:
        n = 1
        for d in s:
            n *= d
        out.append(flat[o:o + n].reshape(s))
        o += n
    return out


def kernel(x, g_mix, w_in, w_s, b_s, ln_g, ln_b, w_gate_f, b_gate_f, w_gate_b, b_gate_b, g_gla, w_out, g_ffn, w_up, conv_w, conv_b, w_down, g_final, loss_target, m_g_mix, m_w_in, m_w_s, m_b_s, m_ln_g, m_ln_b, m_w_gate_f, m_b_gate_f, m_w_gate_b, m_b_gate_b, m_g_gla, m_w_out, m_g_ffn, m_w_up, m_conv_w, m_conv_b, m_w_down, m_g_final, v_g_mix, v_w_in, v_w_s, v_b_s, v_ln_g, v_ln_b, v_w_gate_f, v_b_gate_f, v_w_gate_b, v_b_gate_b, v_g_gla, v_w_out, v_g_ffn, v_w_up, v_conv_w, v_conv_b, v_w_down, v_g_final):
    loc = locals()
    w = {n: loc[n] for n in _WEIGHTS}
    m = {n: loc["m_" + n] for n in _WEIGHTS}
    v = {n: loc["v_" + n] for n in _WEIGHTS}
    xi, yi, _ = _pos()
    chip = 2 * xi + yi

    sh_names = list(_SMALL_SHARDED)
    g8 = allgather8(_pack([w[n] for n in sh_names]))
    rows = g8.shape[0] // 8
    per_chip = [_unpack(g8[2 * j * rows:(2 * j + 1) * rows], [w[n].shape for n in sh_names]) for j in range(4)]
    W = dict(w)
    for k, n in enumerate(sh_names):
        W[n] = jnp.concatenate([per_chip[j][k] for j in range(4)], axis=-1)

    shards = [cast_bf16(w[n].reshape(-1, w[n].shape[-1]), _BIG_TR[n]).reshape(w[n].shape) for n in _BIG]
    chipvec = jnp.reshape(chip, (1,)).astype(jnp.int32)
    landings = [place_own(shards[k], lax.empty(_LAYER_FULL[k], BF16), l, k, chipvec, _BIG_TR[_BIG[k]])
                for l, k in _GW_ORDER]
    send_sems, recv_sems, shards_fly, landings_fly, _ = gw_start(shards, landings)
    arrived = {}

    def get_big(l, stage, after):
        gi = {(0, "in"): 0, (0, "rest"): 1}.get((l, stage), l + 1 if stage == "in" else None)
        if gi is not None:
            lo = sum(len(g) for g in _GW_GROUPS[:gi])
            lands = landings_fly[lo:lo + len(_GW_GROUPS[gi])]
            if gi == len(_GW_GROUPS) - 1:
                full = gw_wait(gi, lands, recv_sems[gi], after, shards_fly, send_sems)
            else:
                full = gw_wait(gi, lands, recv_sems[gi], after)
            arrived.update(zip(_GW_GROUPS[gi], full))
        if stage == "in":
            f_in = jnp.transpose(arrived[(l, 0)], (1, 0, 2)).reshape(D, N_IN)
            return {"w_in": jnp.pad(f_in, ((0, 0), (0, N_INP - N_IN)))}
        return {"w_out": arrived[(l, 1)], "w_up": arrived[(l, 2)], "w_down": arrived[(l, 3)]}

    flying = []

    def emit(l, group, grads):
        ks = [3, 2] if group == "A" else [1, 0]
        gs = [grads[_BIG[k]] for k in ks]
        if group == "B":
            gs[1] = jnp.transpose(gs[1][:, :N_IN].reshape(D, 4, 648), (1, 0, 2))
        lands = [lax.empty((4,) + _LAYER_SHARD[k], BF16) for k in ks]
        tag = "%d%s" % (l, group)
        ss, rs, gs_fly, lands_fly, tok = ga_start(tag, ks, gs, lands)
        flying.append((tag, l, ks, ss, rs, gs_fly, lands_fly))
        return tok[0:1, 0:1]

    lsum, grad_x, G = local_step(x[0], loss_target[0], W, get_big, emit)

    plane = [[None] * NL for _ in range(N_BIG)]
    for tag, l, ks, ss, rs, gs_fly, lands_fly in flying:
        for k, g, a in zip(ks, *ga_wait(tag, ks, ss, rs, gs_fly, lands_fly, grad_x)):
            plane[k][l] = sum_parts(a, g, k, chipvec, _BIG_TR[_BIG[k]])
    plane = [jnp.stack(p) for p in plane]
    other = swap4(plane)

    small_shapes = [G[n].shape for n in _SMALL] + [(D,)]
    pk = _pack([G[n] for n in _SMALL] + [lsum])
    srows = pk.shape[0]
    red = sum_lead(allgather8(pk).reshape(8, srows, 128), srows)
    small = dict(zip(_SMALL + ["lsum"], _unpack(red, small_shapes)))
    loss = 0.5 * jnp.sum(small.pop("lsum")) / D
    for n, wd in _SMALL_SHARDED.items():
        small[n] = lax.dynamic_slice_in_dim(small[n], chip * wd, wd, axis=small[n].ndim - 1)

    grads, delta, new_m, new_v = dict(small), {}, {}, {}
    two = lambda a: a.reshape(-1, a.shape[-1])
    for k, n in enumerate(_BIG):
        res = adamw(two(w[n]), two(plane[k]), two(other[k]), two(m[n]), two(v[n]), _BIG_TR[n])
        grads[n], delta[n], new_m[n], new_v[n] = (r.reshape(w[n].shape) for r in res)
    shapes = [w[n].shape for n in _SMALL]
    pw, pg, pm, pv = (_pack([t[n] for n in _SMALL]) for t in (w, grads, m, v))
    _, d_, m_, v_ = adamw(pw, pg, jnp.zeros_like(pg), pm, pv, pw.shape[0])
    for t, buf in ((delta, d_), (new_m, m_), (new_v, v_)):
        t.update(zip(_SMALL, _unpack(buf, shapes)))

    return (loss, grad_x[None], *[grads[n] for n in _WEIGHTS], *[delta[n] for n in _WEIGHTS],
            *[new_m[n] for n in _WEIGHTS], *[new_v[n] for n in _WEIGHTS])
```

```python
import functools

import jax
import jax.numpy as jnp
from jax import lax
from jax.experimental import pallas as pl
from jax.experimental.pallas import tpu as pltpu

F32 = jnp.float32
BF16 = jnp.bfloat16
MX = BF16

D = 1024
CH = 128
NL = 4
N_IN = 2592
N_INP = 2688
NUP = 5632
DFF = 2816
EPS = 1e-6
VMEM_LIMIT = 56 * 1024 * 1024

ADAM_LR, ADAM_B1, ADAM_B2, ADAM_EPS, ADAM_WD, ADAM_STEP = 0.001, 0.9, 0.999, 1e-08, 0.01, 10


def _dg(a, b, ca, cb):
    return lax.dot_general(a.astype(MX), b.astype(MX), (((ca,), (cb,)), ((), ())), preferred_element_type=F32)


@jax.custom_vjp
def mm(a, b):
    return _dg(a, b, 1, 0)


mm.defvjp(lambda a, b: (_dg(a, b, 1, 0), (a, b)),
          lambda r, g: (_dg(g, r[1], 1, 1), _dg(r[0], g, 0, 0)))


@jax.custom_vjp
def mm_nt(a, b):
    return _dg(a, b, 1, 1)


mm_nt.defvjp(lambda a, b: (_dg(a, b, 1, 1), (a, b)),
             lambda r, g: (_dg(g, r[1], 1, 0), _dg(g, r[0], 0, 0)))


@jax.custom_vjp
def mm_tn(a, b):
    return _dg(a, b, 0, 0)


mm_tn.defvjp(lambda a, b: (_dg(a, b, 0, 0), (a, b)),
             lambda r, g: (_dg(r[1], g, 1, 1), _dg(r[0], g, 1, 0)))


def _split3(x):
    hi = x.astype(BF16)
    r1 = x - hi.astype(F32)
    mid = r1.astype(BF16)
    lo = (r1 - mid.astype(F32)).astype(BF16)
    return hi, mid, lo


def _dot3(m, x):
    hi, mid, lo = _split3(x)
    d = lambda p: lax.dot_general(m, p, (((1,), (0,)), ((), ())), preferred_element_type=F32)
    return d(hi) + d(mid) + d(lo)


@jax.custom_vjp
def cumdot(m, mt, x):
    return _dot3(m, x)


cumdot.defvjp(lambda m, mt, x: (_dot3(m, x), (m, mt)),
              lambda r, g: (jnp.zeros_like(r[0]), jnp.zeros_like(r[1]), _dot3(r[1], g)))


def rmsnorm(x, g):
    return x * lax.rsqrt(jnp.mean(x * x, axis=-1, keepdims=True) + EPS) * g


def gelu(x):
    return 0.5 * x * (1.0 + lax.erf(x * 0.7071067811865476))


def sigmoid(x):
    return 1.0 / (1.0 + jnp.exp(-x))


def log_sigmoid(x):
    return jnp.minimum(x, 0.0) - jnp.log(1.0 + jnp.exp(-jnp.abs(x)))


def gmlp_heads(params, pieces):
    u = [[gelu(p[0]) for p in ch] for ch in pieces]
    v = [[gelu(p[1]) for p in ch] for ch in pieces]
    mu = [[jnp.mean(x, axis=-1, keepdims=True) for x in ch] for ch in v]
    var = [[jnp.mean(jnp.square(x - m), axis=-1, keepdims=True) for x, m in zip(cv, cm)] for cv, cm in zip(v, mu)]
    vn = [[(x - m) * lax.rsqrt(s + EPS) * pr[2] + pr[3] for x, m, s, pr in zip(cv, cm, cs, params)]
          for cv, cm, cs in zip(v, mu, var)]
    mix = [[mm(pr[0], x) + pr[1] for x, pr in zip(ch, params)] for ch in vn]
    return [[a * b for a, b in zip(cu, cx)] for cu, cx in zip(u, mix)]


def outb_head(o, pg, g):
    return rmsnorm(o, g) * (pg * sigmoid(pg))


def ffn_act(zg, zv):
    return zg * sigmoid(zg) * zv


def _tri(reverse):
    r = lax.broadcasted_iota(jnp.int32, (CH, CH), 0)
    c = lax.broadcasted_iota(jnp.int32, (CH, CH), 1)
    if reverse:
        cm, sm = c >= r, c > r
    else:
        cm, sm = c <= r, c <= r
    one = jnp.ones((), BF16)
    zero = jnp.zeros((), BF16)
    return jnp.where(cm, one, zero), jnp.where(cm.T, one, zero), sm


def gla_pair(consts, wg, bg, st0, st1, *chunks):
    m, mt, smask, lm0, lm1 = consts
    ch = [chunks[5 * i:5 * i + 5] for i in range(len(chunks) // 5)]
    la = [log_sigmoid(mm(c[0], wg) + bg) * (1.0 / 16.0) for c in ch]
    cum = [cumdot(m, mt, x) for x in la]
    tot = [jnp.sum(x, axis=0, keepdims=True) for x in la]
    q_dec = [(c[1] * 0.125) * jnp.exp(cm) for c, cm in zip(ch, cum)]
    k_inv = [c[2] * jnp.exp(-cm) for c, cm in zip(ch, cum)]
    k_end = [c[2] * jnp.exp(t - cm) for c, t, cm in zip(ch, tot, cum)]
    s = [[jnp.where(smask, mm_nt(qd * lm, ki), 0.0) for lm in (lm0, lm1)] for qd, ki in zip(q_dec, k_inv)]
    o_in = [[mm(si[h], c[3 + h]) for h in (0, 1)] for si, c in zip(s, ch)]
    ds = [[mm_tn(c[3 + h], ke * lm) for h, lm in ((0, lm0), (1, lm1))] for c, ke in zip(ch, k_end)]
    sts = [(st0, st1)]
    for t, d in zip(tot, ds):
        dec = jnp.exp(t)
        sts.append((sts[-1][0] * dec + d[0], sts[-1][1] * dec + d[1]))
    outs = []
    for qd, oi, st in zip(q_dec, o_in, sts):
        outs += [oi[0] + mm_nt(qd, st[0]), oi[1] + mm_nt(qd, st[1])]
    return (*outs, sts[-1][0], sts[-1][1])


def _lane_masks():
    lane = lax.broadcasted_iota(jnp.int32, (1, 128), 1)
    return (lane < 64).astype(F32), (lane >= 64).astype(F32)


def _cparams(n_axes=1):
    return pltpu.CompilerParams(dimension_semantics=("arbitrary",) * n_axes, vmem_limit_bytes=VMEM_LIMIT)


def _full(a):
    nd = a.ndim
    return pl.BlockSpec(a.shape, lambda *_: (0,) * nd)


def _rows(tm, w, cb=0, rev_n=None):
    if rev_n is None:
        return pl.BlockSpec((tm, w), lambda i: (i, cb))
    return pl.BlockSpec((tm, w), lambda i: (rev_n - 1 - i, cb))


def _call(body, name, grid, in_specs, out_specs, out_shape, scratch=(), n_axes=1):
    return pl.pallas_call(body, name=name, grid=grid, in_specs=in_specs, out_specs=out_specs, out_shape=out_shape,
                          scratch_shapes=list(scratch), compiler_params=_cparams(n_axes))


def _sds(shape, dt=F32):
    return jax.ShapeDtypeStruct(shape, dt)


def norm_matmul(x, g, w, tm, name, ydt=F32):
    T, n = x.shape[0], w.shape[1]

    def body(x_ref, g_ref, w_ref, y_ref, h_ref):
        hb = rmsnorm(x_ref[...], g_ref[...]).astype(MX)
        h_ref[...] = hb
        y_ref[...] = jnp.dot(hb, w_ref[...], preferred_element_type=F32).astype(ydt)

    return _call(body, name, (T // tm,), [_rows(tm, D), _full(g), _full(w)],
                 [_rows(tm, n), _rows(tm, D)], [_sds((T, n), ydt), _sds((T, D), MX)])(x, g, w)


CPB = 8


def _chunk(c):
    return slice(c * CH, (c + 1) * CH)


def gmlp_fwd(p, ws, bs, lg, lb):
    T = p.shape[0]
    tm = CPB * CH

    def body(pa_ref, ws_ref, bs_ref, lg_ref, lb_ref, o_ref):
        params = [(ws_ref[h], bs_ref[h], lg_ref[h], lb_ref[h]) for h in range(4)]
        pieces = [[(pa_ref[_chunk(c), h * 128:(h + 1) * 128], pa_ref[_chunk(c), 512 + h * 128:512 + (h + 1) * 128])
                   for h in range(4)] for c in range(CPB)]
        out = gmlp_heads(params, pieces)
        for c in range(CPB):
            for h in range(4):
                o_ref[_chunk(c), h * 128:(h + 1) * 128] = out[c][h].astype(MX)

    return _call(body, "gmlp_fwd", (T // tm,), [_rows(tm, 1024), _full(ws), _full(bs), _full(lg), _full(lb)],
                 _rows(tm, 512), _sds((T, 512), MX))(p, ws, bs, lg, lb)


def _gla_in_specs(tm, n, rev):
    r = n if rev else None
    return [_rows(tm, 256, 4, r), _rows(tm, 256, 5, r), _rows(tm, 512, 3, r), _rows(tm, 128, 20, r)]


def gla_fwd(p, wg, bg, reverse):
    T = p.shape[0]
    tm = CPB * CH
    n = T // tm
    rev = n if reverse else None

    def body(q_ref, k_ref, v_ref, r_ref, wg_ref, bg_ref, o_ref, ss_ref, st_ref):
        @pl.when(pl.program_id(0) == 0)
        def _():
            st_ref[...] = jnp.zeros_like(st_ref)

        consts = _tri(reverse) + _lane_masks()
        order = list(reversed(range(CPB))) if reverse else list(range(CPB))
        ss_ref[0] = st_ref[...]
        for j in range(2):
            sl = slice(j * 128, (j + 1) * 128)
            v0s, v1s = slice(256 * j, 256 * j + 128), slice(256 * j + 128, 256 * j + 256)
            chunks = []
            for c in order:
                rows = _chunk(c)
                chunks += [r_ref[rows, :], q_ref[rows, sl], k_ref[rows, sl], v_ref[rows, v0s], v_ref[rows, v1s]]
            res = gla_pair(consts, wg_ref[:, sl], bg_ref[:, sl], st_ref[2 * j], st_ref[2 * j + 1], *chunks)
            for i, c in enumerate(order):
                o_ref[_chunk(c), v0s] = res[2 * i]
                o_ref[_chunk(c), v1s] = res[2 * i + 1]
            st_ref[2 * j] = res[-2]
            st_ref[2 * j + 1] = res[-1]

    ss_spec = pl.BlockSpec((1, 4, 128, 128), (lambda i: (n - 1 - i, 0, 0, 0)) if reverse else (lambda i: (i, 0, 0, 0)))
    return _call(body, "gla_fwd_r" if reverse else "gla_fwd_f", (n,),
                 _gla_in_specs(tm, n, reverse) + [_full(wg), _full(bg)],
                 [_rows(tm, 512, 0, rev), ss_spec], [_sds((T, 512)), _sds((n, 4, 128, 128))],
                 scratch=[pltpu.VMEM((4, 128, 128), F32)])(p, p, p, p, wg, bg)


def mix_out(x, of, ob, p, outa, gg, w_out, tm):
    T = x.shape[0]

    def body(x_ref, of_ref, ob_ref, pg_ref, oa_ref, gg_ref, w_ref, x1_ref, mx_ref):
        mx_ref[:, 0:512] = oa_ref[...]
        for h in range(4):
            sl = slice(h * 128, (h + 1) * 128)
            mx_ref[:, 512 + h * 128:512 + (h + 1) * 128] = outb_head(
                of_ref[:, sl] + ob_ref[:, sl], pg_ref[:, sl], gg_ref[h]).astype(MX)
        x1_ref[...] = x_ref[...] + jnp.dot(mx_ref[...], w_ref[...], preferred_element_type=F32)

    return _call(body, "mix_out", (T // tm,),
                 [_rows(tm, D), _rows(tm, 512), _rows(tm, 512), _rows(tm, 512, 4), _rows(tm, 512), _full(gg), _full(w_out)],
                 [_rows(tm, D), _rows(tm, 1024)], [_sds((T, D)), _sds((T, 1024), MX)])(x, of, ob, p, outa, gg, w_out)


HALO = 16


def _halo_specs(T, tm, w):
    nb = T // HALO
    r = tm // HALO
    return [pl.BlockSpec((tm, w), lambda i: (i, 0)),
            pl.BlockSpec((HALO, w), lambda i: (jnp.maximum(i * r - 1, 0), 0)),
            pl.BlockSpec((HALO, w), lambda i: (jnp.minimum((i + 1) * r, nb - 1), 0))]


def ffn_up_conv(x1, g, w_up, cw, cb, tm):
    T = x1.shape[0]
    ns = T // tm
    cwid = 256

    def body(x_ref, g_ref, w_ref, cw_ref, cb_ref, zu_ref, h_ref, z_ref, a_ref, prev_ref, tail_ref):
        i = pl.program_id(0)

        @pl.when(i == 0)
        def _():
            prev_ref[...] = jnp.zeros_like(prev_ref)
            tail_ref[...] = jnp.zeros_like(tail_ref)

        hb = rmsnorm(x_ref[...], g_ref[...]).astype(MX)
        h_ref[...] = hb
        row = lax.broadcasted_iota(jnp.int32, (tm, 1), 0)
        for c0 in range(0, DFF, cwid):
            z2 = []
            for cs in (slice(c0, c0 + cwid), slice(DFF + c0, DFF + c0 + cwid)):
                zub = jnp.dot(hb, w_ref[:, cs], preferred_element_type=F32).astype(MX)
                zu_ref[:, cs] = zub
                prev = prev_ref[:, cs].astype(F32)
                pr = tail_ref[HALO - 1:HALO, cs].astype(F32)
                nx = jnp.where(i < ns, zub[0:1, :].astype(F32), 0.0)
                dn = jnp.where(row == 0, pr, pltpu.roll(prev, 1, 0))
                up = jnp.where(row == tm - 1, nx, pltpu.roll(prev, tm - 1, 0))
                z = cb_ref[:, cs] + dn * cw_ref[0:1, cs] + prev * cw_ref[1:2, cs] + up * cw_ref[2:3, cs]
                z_ref[:, cs] = z.astype(MX)
                tail_ref[:, cs] = prev_ref[tm - HALO:tm, cs]
                prev_ref[:, cs] = zub
                z2.append(z)
            a_ref[:, c0:c0 + cwid] = ffn_act(z2[0], z2[1]).astype(MX)

    cur = lambda w: pl.BlockSpec((tm, w), lambda i: (jnp.minimum(i, ns - 1), 0))
    late = lambda w: pl.BlockSpec((tm, w), lambda i: (jnp.maximum(i - 1, 0), 0))
    return _call(body, "ffn_up", (ns + 1,), [cur(D), _full(g), _full(w_up), _full(cw), _full(cb)],
                 [cur(NUP), cur(D), late(NUP), late(DFF)],
                 [_sds((T, NUP), MX), _sds((T, D), MX), _sds((T, NUP), MX), _sds((T, DFF), MX)],
                 scratch=[pltpu.VMEM((tm, NUP), MX), pltpu.VMEM((HALO, NUP), MX)])(x1, g, w_up, cw, cb)


def matmul_res(a, w, res, tm, name):
    T, k = a.shape
    n = w.shape[1]

    def body(a_ref, w_ref, r_ref, o_ref):
        o_ref[...] = r_ref[...] + jnp.dot(a_ref[...], w_ref[...], preferred_element_type=F32)

    return _call(body, name, (T // tm,), [_rows(tm, k), _full(w), _rows(tm, n)], _rows(tm, n), _sds((T, n)))(a, w, res)


def loss_head(x, g, tgt, tm):
    T = x.shape[0]

    def body(x_ref, g_ref, t_ref, l_ref, dx_ref, dg_ref):
        @pl.when(pl.program_id(0) == 0)
        def _():
            l_ref[...] = jnp.zeros_like(l_ref)
            dg_ref[...] = jnp.zeros_like(dg_ref)

        y, vjp = jax.vjp(rmsnorm, x_ref[...], g_ref[...])
        err = y - t_ref[...]
        l_ref[...] += jnp.sum(err * err, axis=0, keepdims=True)
        dx, dg = vjp(err * (1.0 / D))
        dx_ref[...] = dx
        dg_ref[...] += dg

    return _call(body, "loss_head", (T // tm,), [_rows(tm, D), _full(g), _rows(tm, D)],
                 [_full(g), _rows(tm, D), _full(g)], [_sds((1, D)), _sds((T, D)), _sds((1, D))])(x, g, tgt)


def ffn_down_bwd(dx2, z, w_down, tm):
    T = dx2.shape[0]

    def body(dx_ref, z_ref, w_ref, dz_ref):
        da = _dg(dx_ref[...], w_ref[...], 1, 1)
        zg, zv = z_ref[:, :DFF].astype(F32), z_ref[:, DFF:].astype(F32)
        s = sigmoid(zg)
        sz = zg * s
        dz_ref[:, :DFF] = (da * zv * (s + sz * (1.0 - s))).astype(MX)
        dz_ref[:, DFF:] = (da * sz).astype(MX)

    return _call(body, "ffn_down_bwd", (T // tm,), [_rows(tm, D), _rows(tm, NUP), _full(w_down)],
                 _rows(tm, NUP), _sds((T, NUP), MX))(dx2, z, w_down)


def ffn_up_bwd(dz, zu, cw, w_up, x1, g, dres, tm):
    T = dz.shape[0]
    ns = T // tm
    cwid = 512

    def body(dz_ref, dp_ref, dn_ref, zu_ref, cw_ref, w_ref, x_ref, g_ref, dr_ref,
             dzu_ref, dx_ref, dg_ref, dcw_ref, dcb_ref):
        i = pl.program_id(0)

        @pl.when(i == 0)
        def _():
            for r in (dg_ref, dcw_ref, dcb_ref):
                r[...] = jnp.zeros_like(r)

        row = lax.broadcasted_iota(jnp.int32, (tm, 1), 0)
        dh = jnp.zeros((tm, D), F32)
        for c0 in range(0, NUP, cwid):
            cs = slice(c0, c0 + cwid)
            dz = dz_ref[:, cs].astype(F32)
            zu = zu_ref[:, cs].astype(F32)
            pr = jnp.where(i > 0, dp_ref[HALO - 1:HALO, cs].astype(F32), 0.0)
            nx = jnp.where(i < ns - 1, dn_ref[0:1, cs].astype(F32), 0.0)
            ddn = jnp.where(row == 0, pr, pltpu.roll(dz, 1, 0))
            dup = jnp.where(row == tm - 1, nx, pltpu.roll(dz, tm - 1, 0))
            dzu = (dup * cw_ref[0:1, cs] + dz * cw_ref[1:2, cs] + ddn * cw_ref[2:3, cs]).astype(MX)
            dzu_ref[:, cs] = dzu
            dcw_ref[0:1, cs] += jnp.sum(zu * dup, axis=0, keepdims=True)
            dcw_ref[1:2, cs] += jnp.sum(zu * dz, axis=0, keepdims=True)
            dcw_ref[2:3, cs] += jnp.sum(zu * ddn, axis=0, keepdims=True)
            dcb_ref[:, cs] += jnp.sum(dz, axis=0, keepdims=True)
            dh = dh + _dg(dzu, w_ref[:, cs], 1, 1)
        _, vjp = jax.vjp(rmsnorm, x_ref[...], g_ref[...])
        dx, dg = vjp(dh)
        dx_ref[...] = dr_ref[...] + dx
        dg_ref[...] += dg

    return _call(body, "ffn_up_bwd", (ns,),
                 _halo_specs(T, tm, NUP) + [_rows(tm, NUP), _full(cw), _full(w_up), _rows(tm, D), _full(g), _rows(tm, D)],
                 [_rows(tm, NUP), _rows(tm, D), _full(g), _full(cw), pl.BlockSpec((1, NUP), lambda i: (0, 0))],
                 [_sds((T, NUP), MX), _sds((T, D)), _sds((1, D)), _sds((3, NUP)), _sds((1, NUP))])(
                     dz, dz, dz, zu, cw, w_up, x1, g, dres)


def nt_normbwd(dys, w, x, g, dres, tm, name):
    T = x.shape[0]
    n = len(dys)
    offs = [sum(d.shape[1] for d in dys[:i]) for i in range(n + 1)]

    def body(*refs):
        dy_refs, (w_ref, x_ref, g_ref, dr_ref, dx_ref, dg_ref) = refs[:n], refs[n:]

        @pl.when(pl.program_id(0) == 0)
        def _():
            dg_ref[...] = jnp.zeros_like(dg_ref)

        dh = _dg(dy_refs[0][...], w_ref[:, offs[0]:offs[1]], 1, 1)
        for i in range(1, n):
            dh = dh + _dg(dy_refs[i][...], w_ref[:, offs[i]:offs[i + 1]], 1, 1)
        _, vjp = jax.vjp(rmsnorm, x_ref[...], g_ref[...])
        dx, dg = vjp(dh)
        dx_ref[...] = dr_ref[...] + dx
        dg_ref[...] += dg

    return _call(body, name, (T // tm,),
                 [_rows(tm, d.shape[1]) for d in dys] + [_full(w), _rows(tm, D), _full(g), _rows(tm, D)],
                 [_rows(tm, D), _full(g)], [_sds((T, D)), _sds((1, D))])(*dys, w, x, g, dres)


def matmul_tn(a, b, tt, tn, name):
    T, k = a.shape
    n = b.shape[1]
    last = T // tt - 1

    def body(a_ref, b_ref, o_ref, acc_ref):
        @pl.when(pl.program_id(1) == 0)
        def _():
            acc_ref[...] = jnp.zeros_like(acc_ref)

        acc_ref[...] += _dg(a_ref[...], b_ref[...], 0, 0)

        @pl.when(pl.program_id(1) == last)
        def _():
            o_ref[...] = acc_ref[...].astype(MX)

    return _call(body, name, (n // tn, T // tt),
                 [pl.BlockSpec((tt, k), lambda j, i: (i, 0)), pl.BlockSpec((tt, tn), lambda j, i: (i, j))],
                 pl.BlockSpec((k, tn), lambda j, i: (0, j)), _sds((k, n), MX), scratch=[pltpu.VMEM((k, tn), F32)],
                 n_axes=2)(a, b)


def mix_out_bwd(dx1, w_out, of, ob, p, gg, tm):
    T = dx1.shape[0]

    def body(dx_ref, w_ref, of_ref, ob_ref, pg_ref, gg_ref, da_ref, do_ref, dpg_ref, dgg_ref):
        @pl.when(pl.program_id(0) == 0)
        def _():
            dgg_ref[...] = jnp.zeros_like(dgg_ref)

        dxb = dx_ref[...].astype(MX)
        da_ref[...] = _dg(dxb, w_ref[0:512, :], 1, 1)
        for h in range(4):
            sl = slice(h * 128, (h + 1) * 128)
            dm = _dg(dxb, w_ref[512 + h * 128:512 + (h + 1) * 128, :], 1, 1)
            _, vjp = jax.vjp(outb_head, of_ref[:, sl] + ob_ref[:, sl], pg_ref[:, sl], gg_ref[h])
            do, dpg, dg = vjp(dm)
            do_ref[:, sl] = do
            dpg_ref[:, sl] = dpg
            dgg_ref[h] += dg

    return _call(body, "mix_out_bwd", (T // tm,),
                 [_rows(tm, D), _full(w_out), _rows(tm, 512), _rows(tm, 512), _rows(tm, 512, 4), _full(gg)],
                 [_rows(tm, 512), _rows(tm, 512), _rows(tm, 512), _full(gg)],
                 [_sds((T, 512)), _sds((T, 512)), _sds((T, 512)), _sds(gg.shape)])(dx1, w_out, of, ob, p, gg)


def gla_bwd(p, wg, bg, ss, do, reverse, merge=None):
    T = p.shape[0]
    tm = CPB * CH
    n = T // tm
    rev = not reverse
    rn = n if rev else None

    def body(*refs):
        q_ref, k_ref, v_ref, r_ref, wg_ref, bg_ref, ss_ref, do_ref = refs[:8]
        if merge is None:
            dq_ref, dk_ref, dv_ref, dr_ref, dwg_ref, dbg_ref, dst_ref = refs[8:]
        else:
            mq_ref, mk_ref, mv_ref, mr_ref, mg_ref, out_ref, dwg_ref, dbg_ref, dst_ref, drs_ref = refs[8:]
            out_ref[:, 1024:1536] = mg_ref[...].astype(MX)

        @pl.when(pl.program_id(0) == 0)
        def _():
            dst_ref[...] = jnp.zeros_like(dst_ref)
            dwg_ref[...] = jnp.zeros_like(dwg_ref)
            dbg_ref[...] = jnp.zeros_like(dbg_ref)

        consts = _tri(reverse) + _lane_masks()
        order = list(reversed(range(CPB))) if reverse else list(range(CPB))
        for j in range(2):
            sl = slice(j * 128, (j + 1) * 128)
            v0s, v1s = slice(256 * j, 256 * j + 128), slice(256 * j + 128, 256 * j + 256)
            chunks, dout = [], []
            for c in order:
                rows = _chunk(c)
                chunks += [r_ref[rows, :], q_ref[rows, sl], k_ref[rows, sl], v_ref[rows, v0s], v_ref[rows, v1s]]
                dout += [do_ref[rows, v0s], do_ref[rows, v1s]]
            _, vjp = jax.vjp(functools.partial(gla_pair, consts), wg_ref[:, sl], bg_ref[:, sl],
                             ss_ref[0, 2 * j], ss_ref[0, 2 * j + 1], *chunks)
            g = vjp((*dout, dst_ref[2 * j], dst_ref[2 * j + 1]))
            dwg_ref[:, sl] += g[0]
            dbg_ref[:, sl] += g[1]
            dst_ref[2 * j] = g[2]
            dst_ref[2 * j + 1] = g[3]
            for i, c in enumerate(order):
                rows = _chunk(c)
                dr, dq, dk, dv0, dv1 = g[4 + 5 * i:9 + 5 * i]
                if merge is None:
                    if j == 0:
                        dr_ref[rows, :] = dr
                    else:
                        dr_ref[rows, :] += dr
                    dq_ref[rows, sl] = dq
                    dk_ref[rows, sl] = dk
                    dv_ref[rows, v0s] = dv0
                    dv_ref[rows, v1s] = dv1
                else:
                    if j == 0:
                        drs_ref[rows, :] = mr_ref[rows, :] + dr
                    else:
                        out_ref[rows, 1536:1664] = (drs_ref[rows, :] + dr).astype(MX)
                    out_ref[rows, sl] = (mq_ref[rows, sl] + dq).astype(MX)
                    out_ref[rows, 256 + 128 * j:384 + 128 * j] = (mk_ref[rows, sl] + dk).astype(MX)
                    out_ref[rows, 512 + 256 * j:640 + 256 * j] = (mv_ref[rows, v0s] + dv0).astype(MX)
                    out_ref[rows, 640 + 256 * j:768 + 256 * j] = (mv_ref[rows, v1s] + dv1).astype(MX)

    ss_spec = pl.BlockSpec((1, 4, 128, 128), (lambda i: (n - 1 - i, 0, 0, 0)) if rev else (lambda i: (i, 0, 0, 0)))
    ins = [p, p, p, p, wg, bg, ss, do]
    in_specs = _gla_in_specs(tm, n, rev) + [_full(wg), _full(bg), ss_spec, _rows(tm, 512, 0, rn)]
    scratch = [pltpu.VMEM((4, 128, 128), F32)]
    if merge is None:
        out_specs = [_rows(tm, 256, 0, rn), _rows(tm, 256, 0, rn), _rows(tm, 512, 0, rn), _rows(tm, 128, 0, rn)]
        out_shape = [_sds((T, 256)), _sds((T, 256)), _sds((T, 512)), _sds((T, 128))]
    else:
        ins += list(merge)
        in_specs += [_rows(tm, a.shape[1], 0, rn) for a in merge]
        out_specs, out_shape = [_rows(tm, 1664, 0, rn)], [_sds((T, 1664), MX)]
        scratch.append(pltpu.VMEM((tm, 128), F32))
    return _call(body, "gla_bwd_r" if reverse else "gla_bwd_f", (n,), in_specs, out_specs + [_full(wg), _full(bg)],
                 out_shape + [_sds(wg.shape), _sds(bg.shape)], scratch=scratch)(*ins)


def gmlp_bwd(p, douta, ws, bs, lg, lb):
    T = p.shape[0]
    cpb = 4
    tm = cpb * CH

    def body(pa_ref, do_ref, ws_ref, bs_ref, lg_ref, lb_ref, dpa_ref, dws_ref, dbs_ref, dlg_ref, dlb_ref):
        @pl.when(pl.program_id(0) == 0)
        def _():
            for r in (dws_ref, dbs_ref, dlg_ref, dlb_ref):
                r[...] = jnp.zeros_like(r)

        us = [slice(h * 128, (h + 1) * 128) for h in range(4)]
        vs = [slice(512 + h * 128, 512 + (h + 1) * 128) for h in range(4)]
        params = [(ws_ref[h], bs_ref[h], lg_ref[h], lb_ref[h]) for h in range(4)]
        pieces = [[(pa_ref[_chunk(c), us[h]], pa_ref[_chunk(c), vs[h]]) for h in range(4)] for c in range(cpb)]
        _, vjp = jax.vjp(gmlp_heads, params, pieces)
        dparams, dpieces = vjp([[do_ref[_chunk(c), us[h]] for h in range(4)] for c in range(cpb)])
        for h in range(4):
            for r, a in zip((dws_ref, dbs_ref, dlg_ref, dlb_ref), dparams[h]):
                r[h] += a
            for c in range(cpb):
                dpa_ref[_chunk(c), us[h]] = dpieces[c][h][0].astype(MX)
                dpa_ref[_chunk(c), vs[h]] = dpieces[c][h][1].astype(MX)

    return _call(body, "gmlp_bwd", (T // tm,),
                 [_rows(tm, 1024), _rows(tm, 512), _full(ws), _full(bs), _full(lg), _full(lb)],
                 [_rows(tm, 1024), _full(ws), _full(bs), _full(lg), _full(lb)],
                 [_sds((T, 1024), MX), _sds(ws.shape), _sds(bs.shape), _sds(lg.shape), _sds(lb.shape)])(p, douta, ws, bs, lg, lb)


def _gate_pad(w, row0):
    return jnp.zeros((128, 256), F32).at[row0:row0 + 16].set(w)


def local_step(x, tgt, W, get_big, emit, tm=256, tmm=512):
    saved = []
    for l in range(NL):
        s = {"x": x}
        s.update(get_big(l, "in", x))
        p, s["h"] = norm_matmul(x, W["g_mix"][l][None], s["w_in"], tmm, "mix_in")
        s["p"] = p
        ws, bs = W["w_s"][l], W["b_s"][l][:, :, None]
        lg, lb = W["ln_g"][l][:, None, :], W["ln_b"][l][:, None, :]
        outa = gmlp_fwd(p, ws, bs, lg, lb)
        wgf, wgb = _gate_pad(W["w_gate_f"][l], 0), _gate_pad(W["w_gate_b"][l], 16)
        bgf, bgb = W["b_gate_f"][l][None], W["b_gate_b"][l][None]
        s["of"], s["ssf"] = gla_fwd(p, wgf, bgf, False)
        s["ob"], s["ssb"] = gla_fwd(p, wgb, bgb, True)
        s.update(get_big(l, "rest", s["ob"]))
        gg = W["g_gla"][l][:, None, :]
        x1, s["mixed"] = mix_out(x, s["of"], s["ob"], p, outa, gg, s["w_out"], tmm)
        s["x1"] = x1
        s["zu"], s["h2"], s["z"], s["a"] = ffn_up_conv(x1, W["g_ffn"][l][None], s["w_up"], W["conv_w"][l],
                                                       W["conv_b"][l][None], tm)
        x = matmul_res(s["a"], s["w_down"], x1, tmm, "ffn_down")
        saved.append(s)

    lsum, dx, dgf = loss_head(x, W["g_final"][None], tgt, tmm)
    G = {k: [None] * NL for k in _SMALL if k != "g_final"}
    tok = jnp.zeros((1, 1), F32)
    for l in reversed(range(NL)):
        s = saved[l]
        g_down = matmul_tn(s["a"], dx, min(1024, tmm * 2), 512, "dw_down")
        dz = ffn_down_bwd(dx, s["z"], s["w_down"], tm)
        dzu, dx1, dg, G["conv_w"][l], dcb = ffn_up_bwd(dz, s["zu"], W["conv_w"][l] + tok, s["w_up"], s["x1"],
                                                       W["g_ffn"][l][None], dx, tm)
        G["conv_b"][l], G["g_ffn"][l] = dcb[0], dg[0]
        g_up = matmul_tn(s["h2"], dzu, min(1024, tmm * 2), 1408, "dw_up")
        tok = emit(l, "A", {"w_down": g_down, "w_up": g_up})
        g_out = matmul_tn(s["mixed"], dx1, min(1024, tmm * 2), 1024, "dw_out")
        gg = W["g_gla"][l][:, None, :] + tok
        douta, do, dpg, dgg = mix_out_bwd(dx1, s["w_out"], s["of"], s["ob"], s["p"], gg, tmm)
        G["g_gla"][l] = dgg[:, 0, :]
        wgf, wgb = _gate_pad(W["w_gate_f"][l], 0), _gate_pad(W["w_gate_b"][l], 16)
        bgf, bgb = W["b_gate_f"][l][None], W["b_gate_b"][l][None]
        dqf, dkf, dvf, drf, dwgf, dbgf = gla_bwd(s["p"], wgf, bgf, s["ssf"], do, False)
        dpb, dwgb, dbgb = gla_bwd(s["p"], wgb, bgb, s["ssb"], do, True, merge=(dqf, dkf, dvf, drf, dpg))
        G["w_gate_f"][l], G["b_gate_f"][l] = dwgf[0:16], dbgf[0]
        G["w_gate_b"][l], G["b_gate_b"][l] = dwgb[16:32], dbgb[0]
        ws, bs = W["w_s"][l], W["b_s"][l][:, :, None]
        lg, lb = W["ln_g"][l][:, None, :], W["ln_b"][l][:, None, :]
        dpa, G["w_s"][l], dbs, dlg, dlb = gmlp_bwd(s["p"], douta, ws, bs, lg, lb)
        G["b_s"][l], G["ln_g"][l], G["ln_b"][l] = dbs[:, :, 0], dlg[:, 0, :], dlb[:, 0, :]
        tt = min(1024, tmm * 2)
        g_in = jnp.concatenate([matmul_tn(s["h"], dpa, tt, 1024, "dw_in_a"),
                                matmul_tn(s["h"], dpb, tt, 1664, "dw_in_b")], axis=1)
        tok = emit(l, "B", {"w_out": g_out, "w_in": g_in})
        dx, dg = nt_normbwd([dpa, dpb], s["w_in"], s["x"], W["g_mix"][l][None] + tok, dx1, tmm, "mix_in_bwd")
        G["g_mix"][l] = dg[0]
    G = {k: jnp.stack(v) for k, v in G.items()}
    G["g_final"] = dgf[0]
    return lsum, dx, G


def cast_bf16(a, tr):
    r, c = a.shape

    def body(a_ref, o_ref):
        o_ref[...] = a_ref[...].astype(BF16)

    return _call(body, "cast_bf16", (r // tr,), [_rows(tr, c)], _rows(tr, c), _sds((r, c), BF16))(a)


def sum_lead(y, tr):
    n, rr, cc = y.shape

    def body(y_ref, o_ref):
        acc = y_ref[0].astype(F32)
        for k in range(1, n):
            acc = acc + y_ref[k].astype(F32)
        o_ref[...] = acc

    return _call(body, "sum_lead", (rr // tr,), [pl.BlockSpec((n, tr, cc), lambda i: (0, i, 0))],
                 _rows(tr, cc), _sds((rr, cc)))(y)


def sum_parts(land, grad, k, chipvec, tr):
    _, rr, cc = land.shape
    nb = rr // tr

    def body(c_ref, l_ref, g_ref, o_ref):
        mine = g_ref[...].astype(F32)
        acc = None
        for j in range(4):
            part = jnp.where(c_ref[0] == j, mine, l_ref[j].astype(F32))
            acc = part if acc is None else acc + part
        o_ref[...] = acc

    gs = pltpu.PrefetchScalarGridSpec(
        num_scalar_prefetch=1, grid=(nb,),
        in_specs=[pl.BlockSpec((4, tr, cc), lambda i, c: (0, i, 0)), _part_spec(k, tr, nb)],
        out_specs=pl.BlockSpec((tr, cc), lambda i, c: (i, 0)))
    return pl.pallas_call(body, name="sum_parts", grid_spec=gs, out_shape=_sds((rr, cc)),
                          compiler_params=_cparams(1))(chipvec, land, grad)


def adamw(w, ga, gb, m, v, tr):
    r, c = w.shape

    def body(w_ref, ga_ref, gb_ref, m_ref, v_ref, g_ref, d_ref, nm_ref, nv_ref):
        gr = ga_ref[...] + gb_ref[...]
        g_ref[...] = gr
        nm = ADAM_B1 * m_ref[...] + (1.0 - ADAM_B1) * gr
        nv = ADAM_B2 * v_ref[...] + (1.0 - ADAM_B2) * jnp.square(gr)
        m_hat = nm / (1.0 - ADAM_B1 ** ADAM_STEP)
        v_hat = nv / (1.0 - ADAM_B2 ** ADAM_STEP)
        d_ref[...] = -ADAM_LR * (m_hat / (jnp.sqrt(v_hat) + ADAM_EPS) + ADAM_WD * w_ref[...])
        nm_ref[...] = nm
        nv_ref[...] = nv

    sp = _rows(tr, c)
    return _call(body, "adamw", (r // tr,), [sp] * 5, [sp] * 4, [_sds((r, c))] * 4)(w, ga, gb, m, v)


MESH = pl.DeviceIdType.MESH
ANY = pl.BlockSpec(memory_space=pl.ANY)
N_BIG = 4


def _pos():
    return lax.axis_index("x"), lax.axis_index("y"), lax.axis_index("c")


def _other_chips(x, y):
    return [(1 - x, y), (x, 1 - y), (1 - x, 1 - y)]


def _rcopy(src, dst, send_sems, recv_sems, k, to):
    return pltpu.make_async_remote_copy(src_ref=src, dst_ref=dst, send_sem=send_sems.at[k], recv_sem=recv_sems.at[k],
                                        device_id=to, device_id_type=MESH)


def allgather8(xs):
    m, n = xs.shape

    def body(x_ref, out_ref, send_sems, recv_sems, local_sem):
        x, y, c = _pos()
        me, sibling = (x, y, c), (x, y, 1 - c)
        chips = _other_chips(x, y)

        def rows(px, py, pc):
            return out_ref.at[pl.ds((4 * px + 2 * py + pc) * m, m), :]

        def copy(k, block, to, src=None):
            return _rcopy(rows(*block) if src is None else src, rows(*block), send_sems, recv_sems, k, to)

        mine = pltpu.make_async_copy(x_ref, rows(*me), local_sem)
        mine.start()
        first = [copy(0, me, sibling, src=x_ref)]
        first += [copy(1 + j, me, (*chip, c), src=x_ref) for j, chip in enumerate(chips)]
        for cp in first:
            cp.start()
        passed = [copy(4 + j, (*chip, c), sibling) for j, chip in enumerate(chips)]
        for j, chip in enumerate(chips):
            copy(1 + j, (*chip, c), me).wait_recv()
            passed[j].start()
        copy(0, sibling, me).wait_recv()
        for j, chip in enumerate(chips):
            copy(4 + j, (*chip, 1 - c), me).wait_recv()
        for cp in first + passed:
            cp.wait_send()
        mine.wait()

    vm = pl.BlockSpec(memory_space=pltpu.VMEM)
    return pl.pallas_call(
        body, name="allgather8", out_shape=_sds((8 * m, n), xs.dtype), in_specs=[vm], out_specs=vm,
        scratch_shapes=[pltpu.SemaphoreType.DMA((7,)), pltpu.SemaphoreType.DMA((7,)), pltpu.SemaphoreType.DMA],
        compiler_params=pltpu.CompilerParams(vmem_limit_bytes=VMEM_LIMIT))(xs)


def _slab(k, ref, j):
    if k == 0:
        return ref.at[j]
    if k == 1:
        return ref.at[pl.ds(256 * j, 256), :]
    if k == 2:
        return ref.at[:, pl.ds(1408 * j, 1408)]
    return ref.at[pl.ds(704 * j, 704), :]


_LAYER_FULL = [(4, 1024, 648), (1024, 1024), (1024, NUP), (DFF, 1024)]
_LAYER_SHARD = [(1024, 648), (256, 1024), (1024, 1408), (704, 1024)]
_SHARD_SHAPES = [(NL,) + s for s in _LAYER_SHARD]

HBM = pl.BlockSpec(memory_space=pltpu.HBM)
SEM = pl.BlockSpec(memory_space=pltpu.SEMAPHORE)
VM = pl.BlockSpec(memory_space=pltpu.VMEM)
EFFECT = pltpu.SideEffectType.DATAFLOW_SIDE_EFFECTING
_GW_GROUPS = [[(0, 0)], [(0, 1), (0, 2), (0, 3)]] + [[(l, k) for k in range(N_BIG)] for l in range(1, NL)]
_GW_ORDER = [lk for g in _GW_GROUPS for lk in g]


def _hbm(a):
    return pltpu.with_memory_space_constraint(a, pltpu.HBM)


def _hbm_like(a):
    return pltpu.HBM(a.shape, a.dtype)


def _part_spec(k, tr, nb):
    cc = _LAYER_SHARD[k][1]
    if k == 0:
        return pl.BlockSpec((None, tr, cc), lambda i, c: (c[0], i, 0))
    if k == 2:
        return pl.BlockSpec((tr, cc), lambda i, c: (i, c[0]))
    return pl.BlockSpec((tr, cc), lambda i, c: (c[0] * nb + i, 0))


def place_own(shard, landing, l, k, chipvec, tr):
    rr, cc = _LAYER_SHARD[k]
    nb = rr // tr

    def body(c_ref, s_ref, l_ref, o_ref):
        o_ref[...] = s_ref[...]

    gs = pltpu.PrefetchScalarGridSpec(
        num_scalar_prefetch=1, grid=(nb,),
        in_specs=[pl.BlockSpec((None, tr, cc), lambda i, c: (l, i, 0)), ANY], out_specs=_part_spec(k, tr, nb))
    return pl.pallas_call(body, name="place_own", grid_spec=gs, out_shape=_sds(landing.shape, landing.dtype),
                          input_output_aliases={2: 0}, compiler_params=_cparams(1))(chipvec, shard, landing)


def gw_start(shards, landings):
    n = len(_GW_ORDER)

    def body(*refs):
        S, Ld = refs[:N_BIG], refs[N_BIG:N_BIG + n]
        outs = refs[N_BIG + n:]
        send_sems, recv, token = outs[0], outs[1:1 + len(_GW_GROUPS)], outs[-1]
        x, y, c = _pos()
        me = 2 * x + y
        ci = 0
        for gi, grp in enumerate(_GW_GROUPS):
            for t, (l, k) in enumerate(grp):
                land = Ld[_GW_ORDER.index((l, k))]
                for j, (px, py) in enumerate(_other_chips(x, y)):
                    pltpu.make_async_remote_copy(
                        src_ref=S[k].at[l], dst_ref=_slab(k, land, me), send_sem=send_sems.at[ci],
                        recv_sem=recv[gi].at[3 * t + j], device_id=(px, py, c), device_id_type=MESH).start()
                    ci += 1
        token[...] = jnp.zeros_like(token)

    ins = list(shards) + list(landings)
    sems = [pltpu.SemaphoreType.DMA((3 * n,))] + [pltpu.SemaphoreType.DMA((3 * len(g),)) for g in _GW_GROUPS]
    outs = pl.pallas_call(
        body, name="gw_start", out_shape=sems + [_hbm_like(a) for a in ins] + [_sds((8, 128))],
        in_specs=[HBM] * len(ins), out_specs=[SEM] * len(sems) + [HBM] * len(ins) + [VM],
        input_output_aliases={i: len(sems) + i for i in range(len(ins))},
        compiler_params=pltpu.CompilerParams(has_side_effects=EFFECT))(*[_hbm(a) for a in ins])
    ns = len(sems)
    return outs[0], outs[1:ns], outs[ns:ns + N_BIG], outs[ns + N_BIG:ns + len(ins)], outs[-1]


def gw_wait(gi, landings, recv_sems, after, shards=None, send_sems=None):
    grp = _GW_GROUPS[gi]
    n = len(grp)
    last = shards is not None

    def body(*refs):
        Ld, rs = refs[:n], refs[n]
        x, y, c = _pos()
        for t, (l, k) in enumerate(grp):
            for j, (px, py) in enumerate(_other_chips(x, y)):
                region = _slab(k, Ld[t], 2 * px + py)
                pltpu.make_async_remote_copy(src_ref=region, dst_ref=region, send_sem=rs.at[3 * t + j],
                                             recv_sem=rs.at[3 * t + j], device_id=(px, py, c),
                                             device_id_type=MESH).wait_recv()
        if last:
            S, ss = refs[n + 2:n + 2 + N_BIG], refs[n + 2 + N_BIG]
            me = 2 * x + y
            for ci, (l, k) in enumerate(lk for lk in _GW_ORDER for _ in range(3)):
                pltpu.make_async_remote_copy(src_ref=S[k].at[l], dst_ref=_slab(k, Ld[k], me), send_sem=ss.at[ci],
                                             recv_sem=ss.at[ci], device_id=(x, y, c), device_id_type=MESH).wait_send()

    ins = list(landings) + [recv_sems, after]
    specs = [HBM] * n + [SEM, pl.BlockSpec(memory_space=pl.ANY)]
    outs = [_hbm_like(a) for a in landings]
    alias = {i: i for i in range(n)}
    if last:
        ins += list(shards) + [send_sems]
        specs += [HBM] * N_BIG + [SEM]
        outs += [_hbm_like(a) for a in shards]
        alias.update({n + 2 + i: n + i for i in range(N_BIG)})
    res = pl.pallas_call(body, name="gw_wait_%d" % gi, out_shape=outs, in_specs=specs, out_specs=[HBM] * len(outs),
                         input_output_aliases=alias,
                         compiler_params=pltpu.CompilerParams(has_side_effects=EFFECT))(*ins)
    return res[:n]


def ga_start(tag, ks, grads, landings):
    n = len(ks)

    def body(*refs):
        G, Ld = refs[:n], refs[n:2 * n]
        send_sems, recv_sems, token = refs[2 * n], refs[2 * n + 1], refs[-1]
        x, y, c = _pos()
        me = 2 * x + y
        for t, k in enumerate(ks):
            for j, (px, py) in enumerate(_other_chips(x, y)):
                pltpu.make_async_remote_copy(
                    src_ref=_slab(k, G[t], 2 * px + py), dst_ref=Ld[t].at[me], send_sem=send_sems.at[3 * t + j],
                    recv_sem=recv_sems.at[3 * t + j], device_id=(px, py, c), device_id_type=MESH).start()
        token[...] = jnp.zeros_like(token)

    ins = list(grads) + list(landings)
    sems = [pltpu.SemaphoreType.DMA((3 * n,))] * 2
    outs = pl.pallas_call(
        body, name="ga_start_" + tag, out_shape=sems + [_hbm_like(a) for a in ins] + [_sds((8, 128))],
        in_specs=[HBM] * len(ins), out_specs=[SEM, SEM] + [HBM] * len(ins) + [VM],
        input_output_aliases={i: 2 + i for i in range(len(ins))},
        compiler_params=pltpu.CompilerParams(has_side_effects=EFFECT))(*[_hbm(a) for a in ins])
    return outs[0], outs[1], outs[2:2 + n], outs[2 + n:2 + 2 * n], outs[-1]


def ga_wait(tag, ks, send_sems, recv_sems, grads, landings, after):
    n = len(ks)

    def body(*refs):
        G, Ld, ss, rs = refs[:n], refs[n:2 * n], refs[2 * n], refs[2 * n + 1]
        x, y, c = _pos()
        me = 2 * x + y
        for t, k in enumerate(ks):
            for j, (px, py) in enumerate(_other_chips(x, y)):
                pj = 2 * px + py
                cp = pltpu.make_async_remote_copy(
                    src_ref=_slab(k, G[t], pj), dst_ref=Ld[t].at[pj], send_sem=ss.at[3 * t + j],
                    recv_sem=rs.at[3 * t + j], device_id=(px, py, c), device_id_type=MESH)
                cp.wait_send()
                cp.wait_recv()

    ins = list(grads) + list(landings) + [send_sems, recv_sems, after]
    res = pl.pallas_call(
        body, name="ga_wait_" + tag, out_shape=[_hbm_like(a) for a in list(grads) + list(landings)],
        in_specs=[HBM] * (2 * n) + [SEM, SEM, pl.BlockSpec(memory_space=pl.ANY)], out_specs=[HBM] * (2 * n),
        input_output_aliases={i: i for i in range(2 * n)},
        compiler_params=pltpu.CompilerParams(has_side_effects=EFFECT))(*ins)
    return res[:n], res[n:]


def swap4(parts):
    def body(*refs):
        Q, R = refs[:N_BIG], refs[N_BIG:2 * N_BIG]
        send_sems, recv_sems = refs[2 * N_BIG:]
        x, y, c = _pos()
        cps = [_rcopy(Q[k], R[k], send_sems, recv_sems, k, (x, y, 1 - c)) for k in range(N_BIG)]
        for cp in cps:
            cp.start()
        for cp in cps:
            cp.wait()

    return pl.pallas_call(
        body, name="swap4", out_shape=[_sds(s) for s in _SHARD_SHAPES],
        in_specs=[ANY] * N_BIG, out_specs=[ANY] * N_BIG,
        scratch_shapes=[pltpu.SemaphoreType.DMA((N_BIG,)), pltpu.SemaphoreType.DMA((N_BIG,))])(*parts)


_WEIGHTS = ['g_mix', 'w_in', 'w_s', 'b_s', 'ln_g', 'ln_b', 'w_gate_f', 'b_gate_f', 'w_gate_b', 'b_gate_b', 'g_gla',
            'w_out', 'g_ffn', 'w_up', 'conv_w', 'conv_b', 'w_down', 'g_final']
_BIG = ['w_in', 'w_out', 'w_up', 'w_down']
_SMALL = [n for n in _WEIGHTS if n not in _BIG]
_SMALL_SHARDED = {'w_gate_f': 64, 'w_gate_b': 64, 'conv_w': 1408}
_BIG_TR = {'w_in': 512, 'w_out': 256, 'w_up': 256, 'w_down': 352}


def _pack(arrs):
    flat = jnp.concatenate([a.reshape(-1) for a in arrs])
    pad = (-flat.shape[0]) % 1024
    return jnp.pad(flat, (0, pad)).reshape(-1, 128)


def _unpack(buf, shapes):
    flat = buf.reshape(-1)
    out, o = [], 0
    for s in shapes:
        n = 1
        for d in s:
            n *= d
        out.append(flat[o:o + n].reshape(s))
        o += n
    return out


def kernel(x, g_mix, w_in, w_s, b_s, ln_g, ln_b, w_gate_f, b_gate_f, w_gate_b, b_gate_b, g_gla, w_out, g_ffn, w_up, conv_w, conv_b, w_down, g_final, loss_target, m_g_mix, m_w_in, m_w_s, m_b_s, m_ln_g, m_ln_b, m_w_gate_f, m_b_gate_f, m_w_gate_b, m_b_gate_b, m_g_gla, m_w_out, m_g_ffn, m_w_up, m_conv_w, m_conv_b, m_w_down, m_g_final, v_g_mix, v_w_in, v_w_s, v_b_s, v_ln_g, v_ln_b, v_w_gate_f, v_b_gate_f, v_w_gate_b, v_b_gate_b, v_g_gla, v_w_out, v_g_ffn, v_w_up, v_conv_w, v_conv_b, v_w_down, v_g_final):
    loc = locals()
    w = {n: loc[n] for n in _WEIGHTS}
    m = {n: loc["m_" + n] for n in _WEIGHTS}
    v = {n: loc["v_" + n] for n in _WEIGHTS}
    xi, yi, _ = _pos()
    chip = 2 * xi + yi

    sh_names = list(_SMALL_SHARDED)
    g8 = allgather8(_pack([w[n] for n in sh_names]))
    rows = g8.shape[0] // 8
    per_chip = [_unpack(g8[2 * j * rows:(2 * j + 1) * rows], [w[n].shape for n in sh_names]) for j in range(4)]
    W = dict(w)
    for k, n in enumerate(sh_names):
        W[n] = jnp.concatenate([per_chip[j][k] for j in range(4)], axis=-1)

    shards = [cast_bf16(w[n].reshape(-1, w[n].shape[-1]), _BIG_TR[n]).reshape(w[n].shape) for n in _BIG]
    chipvec = jnp.reshape(chip, (1,)).astype(jnp.int32)
    landings = [place_own(shards[k], lax.empty(_LAYER_FULL[k], BF16), l, k, chipvec, _BIG_TR[_BIG[k]])
                for l, k in _GW_ORDER]
    send_sems, recv_sems, shards_fly, landings_fly, _ = gw_start(shards, landings)
    arrived = {}

    def get_big(l, stage, after):
        gi = {(0, "in"): 0, (0, "rest"): 1}.get((l, stage), l + 1 if stage == "in" else None)
        if gi is not None:
            lo = sum(len(g) for g in _GW_GROUPS[:gi])
            lands = landings_fly[lo:lo + len(_GW_GROUPS[gi])]
            if gi == len(_GW_GROUPS) - 1:
                full = gw_wait(gi, lands, recv_sems[gi], after, shards_fly, send_sems)
            else:
                full = gw_wait(gi, lands, recv_sems[gi], after)
            arrived.update(zip(_GW_GROUPS[gi], full))
        if stage == "in":
            f_in = jnp.transpose(arrived[(l, 0)], (1, 0, 2)).reshape(D, N_IN)
            return {"w_in": jnp.pad(f_in, ((0, 0), (0, N_INP - N_IN)))}
        return {"w_out": arrived[(l, 1)], "w_up": arrived[(l, 2)], "w_down": arrived[(l, 3)]}

    flying = []

    def emit(l, group, grads):
        ks = [3, 2] if group == "A" else [1, 0]
        gs = [grads[_BIG[k]] for k in ks]
        if group == "B":
            gs[1] = jnp.transpose(gs[1][:, :N_IN].reshape(D, 4, 648), (1, 0, 2))
        lands = [lax.empty((4,) + _LAYER_SHARD[k], BF16) for k in ks]
        tag = "%d%s" % (l, group)
        ss, rs, gs_fly, lands_fly, tok = ga_start(tag, ks, gs, lands)
        flying.append((tag, l, ks, ss, rs, gs_fly, lands_fly))
        return tok[0:1, 0:1]

    lsum, grad_x, G = local_step(x[0], loss_target[0], W, get_big, emit)

    plane = [[None] * NL for _ in range(N_BIG)]
    for tag, l, ks, ss, rs, gs_fly, lands_fly in flying:
        for k, g, a in zip(ks, *ga_wait(tag, ks, ss, rs, gs_fly, lands_fly, grad_x)):
            plane[k][l] = sum_parts(a, g, k, chipvec, _BIG_TR[_BIG[k]])
    plane = [jnp.stack(p) for p in plane]
    other = swap4(plane)

    small_shapes = [G[n].shape for n in _SMALL] + [(D,)]
    pk = _pack([G[n] for n in _SMALL] + [lsum])
    srows = pk.shape[0]
    red = sum_lead(allgather8(pk).reshape(8, srows, 128), srows)
    small = dict(zip(_SMALL + ["lsum"], _unpack(red, small_shapes)))
    loss = 0.5 * jnp.sum(small.pop("lsum")) / D
    for n, wd in _SMALL_SHARDED.items():
        small[n] = lax.dynamic_slice_in_dim(small[n], chip * wd, wd, axis=small[n].ndim - 1)

    grads, delta, new_m, new_v = dict(small), {}, {}, {}
    two = lambda a: a.reshape(-1, a.shape[-1])
    for k, n in enumerate(_BIG):
        res = adamw(two(w[n]), two(plane[k]), two(other[k]), two(m[n]), two(v[n]), _BIG_TR[n])
        grads[n], delta[n], new_m[n], new_v[n] = (r.reshape(w[n].shape) for r in res)
    shapes = [w[n].shape for n in _SMALL]
    pw, pg, pm, pv = (_pack([t[n] for n in _SMALL]) for t in (w, grads, m, v))
    _, d_, m_, v_ = adamw(pw, pg, jnp.zeros_like(pg), pm, pv, pw.shape[0])
    for t, buf in ((delta, d_), (new_m, m_), (new_v, v_)):
        t.update(zip(_SMALL, _unpack(buf, shapes)))

    return (loss, grad_x[None], *[grads[n] for n in _WEIGHTS], *[delta[n] for n in _WEIGHTS],
            *[new_m[n] for n in _WEIGHTS], *[new_v[n] for n in _WEIGHTS])
```

```python
import functools

import jax
import jax.numpy as jnp
from jax import lax
from jax.experimental import pallas as pl
from jax.experimental.pallas import tpu as pltpu

F32 = jnp.float32
BF16 = jnp.bfloat16
MX = BF16

D = 1024
CH = 128
NL = 4
N_IN = 2592
N_INP = 2688
NUP = 5632
DFF = 2816
EPS = 1e-6
VMEM_LIMIT = 56 * 1024 * 1024

ADAM_LR, ADAM_B1, ADAM_B2, ADAM_EPS, ADAM_WD, ADAM_STEP = 0.001, 0.9, 0.999, 1e-08, 0.01, 10


def _dg(a, b, ca, cb):
    return lax.dot_general(a.astype(MX), b.astype(MX), (((ca,), (cb,)), ((), ())), preferred_element_type=F32)


@jax.custom_vjp
def mm(a, b):
    return _dg(a, b, 1, 0)


mm.defvjp(lambda a, b: (_dg(a, b, 1, 0), (a, b)),
          lambda r, g: (_dg(g, r[1], 1, 1), _dg(r[0], g, 0, 0)))


@jax.custom_vjp
def mm_nt(a, b):
    return _dg(a, b, 1, 1)


mm_nt.defvjp(lambda a, b: (_dg(a, b, 1, 1), (a, b)),
             lambda r, g: (_dg(g, r[1], 1, 0), _dg(g, r[0], 0, 0)))


@jax.custom_vjp
def mm_tn(a, b):
    return _dg(a, b, 0, 0)


mm_tn.defvjp(lambda a, b: (_dg(a, b, 0, 0), (a, b)),
             lambda r, g: (_dg(r[1], g, 1, 1), _dg(r[0], g, 1, 0)))


def _split3(x):
    hi = x.astype(BF16)
    r1 = x - hi.astype(F32)
    mid = r1.astype(BF16)
    lo = (r1 - mid.astype(F32)).astype(BF16)
    return hi, mid, lo


def _dot3(m, x):
    hi, mid, lo = _split3(x)
    d = lambda p: lax.dot_general(m, p, (((1,), (0,)), ((), ())), preferred_element_type=F32)
    return d(hi) + d(mid) + d(lo)


@jax.custom_vjp
def cumdot(m, mt, x):
    return _dot3(m, x)


cumdot.defvjp(lambda m, mt, x: (_dot3(m, x), (m, mt)),
              lambda r, g: (jnp.zeros_like(r[0]), jnp.zeros_like(r[1]), _dot3(r[1], g)))


def rmsnorm(x, g):
    return x * lax.rsqrt(jnp.mean(x * x, axis=-1, keepdims=True) + EPS) * g


def gelu(x):
    return 0.5 * x * (1.0 + lax.erf(x * 0.7071067811865476))


def sigmoid(x):
    return 1.0 / (1.0 + jnp.exp(-x))


def log_sigmoid(x):
    return jnp.minimum(x, 0.0) - jnp.log(1.0 + jnp.exp(-jnp.abs(x)))


def gmlp_heads(params, pieces):
    u = [[gelu(p[0]) for p in ch] for ch in pieces]
    v = [[gelu(p[1]) for p in ch] for ch in pieces]
    mu = [[jnp.mean(x, axis=-1, keepdims=True) for x in ch] for ch in v]
    var = [[jnp.mean(jnp.square(x - m), axis=-1, keepdims=True) for x, m in zip(cv, cm)] for cv, cm in zip(v, mu)]
    vn = [[(x - m) * lax.rsqrt(s + EPS) * pr[2] + pr[3] for x, m, s, pr in zip(cv, cm, cs, params)]
          for cv, cm, cs in zip(v, mu, var)]
    mix = [[mm(pr[0], x) + pr[1] for x, pr in zip(ch, params)] for ch in vn]
    return [[a * b for a, b in zip(cu, cx)] for cu, cx in zip(u, mix)]


def outb_head(o, pg, g):
    return rmsnorm(o, g) * (pg * sigmoid(pg))


def ffn_act(zg, zv):
    return zg * sigmoid(zg) * zv


def _tri(reverse):
    r = lax.broadcasted_iota(jnp.int32, (CH, CH), 0)
    c = lax.broadcasted_iota(jnp.int32, (CH, CH), 1)
    if reverse:
        cm, sm = c >= r, c > r
    else:
        cm, sm = c <= r, c <= r
    one = jnp.ones((), BF16)
    zero = jnp.zeros((), BF16)
    return jnp.where(cm, one, zero), jnp.where(cm.T, one, zero), sm


def gla_pair(consts, wg, bg, st0, st1, *chunks):
    m, mt, smask, lm0, lm1 = consts
    ch = [chunks[5 * i:5 * i + 5] for i in range(len(chunks) // 5)]
    la = [log_sigmoid(mm(c[0], wg) + bg) * (1.0 / 16.0) for c in ch]
    cum = [cumdot(m, mt, x) for x in la]
    tot = [jnp.sum(x, axis=0, keepdims=True) for x in la]
    q_dec = [(c[1] * 0.125) * jnp.exp(cm) for c, cm in zip(ch, cum)]
    k_inv = [c[2] * jnp.exp(-cm) for c, cm in zip(ch, cum)]
    k_end = [c[2] * jnp.exp(t - cm) for c, t, cm in zip(ch, tot, cum)]
    s = [[jnp.where(smask, mm_nt(qd * lm, ki), 0.0) for lm in (lm0, lm1)] for qd, ki in zip(q_dec, k_inv)]
    o_in = [[mm(si[h], c[3 + h]) for h in (0, 1)] for si, c in zip(s, ch)]
    ds = [[mm_tn(c[3 + h], ke * lm) for h, lm in ((0, lm0), (1, lm1))] for c, ke in zip(ch, k_end)]
    sts = [(st0, st1)]
    for t, d in zip(tot, ds):
        dec = jnp.exp(t)
        sts.append((sts[-1][0] * dec + d[0], sts[-1][1] * dec + d[1]))
    outs = []
    for qd, oi, st in zip(q_dec, o_in, sts):
        outs += [oi[0] + mm_nt(qd, st[0]), oi[1] + mm_nt(qd, st[1])]
    return (*outs, sts[-1][0], sts[-1][1])


def _lane_masks():
    lane = lax.broadcasted_iota(jnp.int32, (1, 128), 1)
    return (lane < 64).astype(F32), (lane >= 64).astype(F32)


def _cparams(n_axes=1):
    return pltpu.CompilerParams(dimension_semantics=("arbitrary",) * n_axes, vmem_limit_bytes=VMEM_LIMIT)


def _full(a):
    nd = a.ndim
    return pl.BlockSpec(a.shape, lambda *_: (0,) * nd)


def _rows(tm, w, cb=0, rev_n=None):
    if rev_n is None:
        return pl.BlockSpec((tm, w), lambda i: (i, cb))
    return pl.BlockSpec((tm, w), lambda i: (rev_n - 1 - i, cb))


def _call(body, name, grid, in_specs, out_specs, out_shape, scratch=(), n_axes=1):
    return pl.pallas_call(body, name=name, grid=grid, in_specs=in_specs, out_specs=out_specs, out_shape=out_shape,
                          scratch_shapes=list(scratch), compiler_params=_cparams(n_axes))


def _sds(shape, dt=F32):
    return jax.ShapeDtypeStruct(shape, dt)


def norm_matmul(x, g, w, tm, name, ydt=F32):
    T, n = x.shape[0], w.shape[1]

    def body(x_ref, g_ref, w_ref, y_ref, h_ref):
        hb = rmsnorm(x_ref[...], g_ref[...]).astype(MX)
        h_ref[...] = hb
        y_ref[...] = jnp.dot(hb, w_ref[...], preferred_element_type=F32).astype(ydt)

    return _call(body, name, (T // tm,), [_rows(tm, D), _full(g), _full(w)],
                 [_rows(tm, n), _rows(tm, D)], [_sds((T, n), ydt), _sds((T, D), MX)])(x, g, w)


CPB = 8


def _chunk(c):
    return slice(c * CH, (c + 1) * CH)


def gmlp_fwd(p, ws, bs, lg, lb):
    T = p.shape[0]
    tm = CPB * CH

    def body(pa_ref, ws_ref, bs_ref, lg_ref, lb_ref, o_ref):
        params = [(ws_ref[h], bs_ref[h], lg_ref[h], lb_ref[h]) for h in range(4)]
        pieces = [[(pa_ref[_chunk(c), h * 128:(h + 1) * 128], pa_ref[_chunk(c), 512 + h * 128:512 + (h + 1) * 128])
                   for h in range(4)] for c in range(CPB)]
        out = gmlp_heads(params, pieces)
        for c in range(CPB):
            for h in range(4):
                o_ref[_chunk(c), h * 128:(h + 1) * 128] = out[c][h].astype(MX)

    return _call(body, "gmlp_fwd", (T // tm,), [_rows(tm, 1024), _full(ws), _full(bs), _full(lg), _full(lb)],
                 _rows(tm, 512), _sds((T, 512), MX))(p, ws, bs, lg, lb)


def _gla_in_specs(tm, n, rev):
    r = n if rev else None
    return [_rows(tm, 256, 4, r), _rows(tm, 256, 5, r), _rows(tm, 512, 3, r), _rows(tm, 128, 20, r)]


def gla_fwd(p, wg, bg, reverse):
    T = p.shape[0]
    tm = CPB * CH
    n = T // tm
    rev = n if reverse else None

    def body(q_ref, k_ref, v_ref, r_ref, wg_ref, bg_ref, o_ref, ss_ref, st_ref):
        @pl.when(pl.program_id(0) == 0)
        def _():
            st_ref[...] = jnp.zeros_like(st_ref)

        consts = _tri(reverse) + _lane_masks()
        order = list(reversed(range(CPB))) if reverse else list(range(CPB))
        ss_ref[0] = st_ref[...]
        for j in range(2):
            sl = slice(j * 128, (j + 1) * 128)
            v0s, v1s = slice(256 * j, 256 * j + 128), slice(256 * j + 128, 256 * j + 256)
            chunks = []
            for c in order:
                rows = _chunk(c)
                chunks += [r_ref[rows, :], q_ref[rows, sl], k_ref[rows, sl], v_ref[rows, v0s], v_ref[rows, v1s]]
            res = gla_pair(consts, wg_ref[:, sl], bg_ref[:, sl], st_ref[2 * j], st_ref[2 * j + 1], *chunks)
            for i, c in enumerate(order):
                o_ref[_chunk(c), v0s] = res[2 * i]
                o_ref[_chunk(c), v1s] = res[2 * i + 1]
            st_ref[2 * j] = res[-2]
            st_ref[2 * j + 1] = res[-1]

    ss_spec = pl.BlockSpec((1, 4, 128, 128), (lambda i: (n - 1 - i, 0, 0, 0)) if reverse else (lambda i: (i, 0, 0, 0)))
    return _call(body, "gla_fwd_r" if reverse else "gla_fwd_f", (n,),
                 _gla_in_specs(tm, n, reverse) + [_full(wg), _full(bg)],
                 [_rows(tm, 512, 0, rev), ss_spec], [_sds((T, 512)), _sds((n, 4, 128, 128))],
                 scratch=[pltpu.VMEM((4, 128, 128), F32)])(p, p, p, p, wg, bg)


def mix_out(x, of, ob, p, outa, gg, w_out, tm):
    T = x.shape[0]

    def body(x_ref, of_ref, ob_ref, pg_ref, oa_ref, gg_ref, w_ref, x1_ref, mx_ref):
        mx_ref[:, 0:512] = oa_ref[...]
        for h in range(4):
            sl = slice(h * 128, (h + 1) * 128)
            mx_ref[:, 512 + h * 128:512 + (h + 1) * 128] = outb_head(
                of_ref[:, sl] + ob_ref[:, sl], pg_ref[:, sl], gg_ref[h]).astype(MX)
        x1_ref[...] = x_ref[...] + jnp.dot(mx_ref[...], w_ref[...], preferred_element_type=F32)

    return _call(body, "mix_out", (T // tm,),
                 [_rows(tm, D), _rows(tm, 512), _rows(tm, 512), _rows(tm, 512, 4), _rows(tm, 512), _full(gg), _full(w_out)],
                 [_rows(tm, D), _rows(tm, 1024)], [_sds((T, D)), _sds((T, 1024), MX)])(x, of, ob, p, outa, gg, w_out)


HALO = 16


def _halo_specs(T, tm, w):
    nb = T // HALO
    r = tm // HALO
    return [pl.BlockSpec((tm, w), lambda i: (i, 0)),
            pl.BlockSpec((HALO, w), lambda i: (jnp.maximum(i * r - 1, 0), 0)),
            pl.BlockSpec((HALO, w), lambda i: (jnp.minimum((i + 1) * r, nb - 1), 0))]


def ffn_up_conv(x1, g, w_up, cw, cb, tm):
    T = x1.shape[0]
    ns = T // tm
    cwid = 256

    def body(x_ref, g_ref, w_ref, cw_ref, cb_ref, zu_ref, h_ref, z_ref, a_ref, prev_ref, tail_ref):
        i = pl.program_id(0)

        @pl.when(i == 0)
        def _():
            prev_ref[...] = jnp.zeros_like(prev_ref)
            tail_ref[...] = jnp.zeros_like(tail_ref)

        hb = rmsnorm(x_ref[...], g_ref[...]).astype(MX)
        h_ref[...] = hb
        row = lax.broadcasted_iota(jnp.int32, (tm, 1), 0)
        for c0 in range(0, DFF, cwid):
            z2 = []
            for cs in (slice(c0, c0 + cwid), slice(DFF + c0, DFF + c0 + cwid)):
                zub = jnp.dot(hb, w_ref[:, cs], preferred_element_type=F32).astype(MX)
                zu_ref[:, cs] = zub
                prev = prev_ref[:, cs].astype(F32)
                pr = tail_ref[HALO - 1:HALO, cs].astype(F32)
                nx = jnp.where(i < ns, zub[0:1, :].astype(F32), 0.0)
                dn = jnp.where(row == 0, pr, pltpu.roll(prev, 1, 0))
                up = jnp.where(row == tm - 1, nx, pltpu.roll(prev, tm - 1, 0))
                z = cb_ref[:, cs] + dn * cw_ref[0:1, cs] + prev * cw_ref[1:2, cs] + up * cw_ref[2:3, cs]
                z_ref[:, cs] = z.astype(MX)
                tail_ref[:, cs] = prev_ref[tm - HALO:tm, cs]
                prev_ref[:, cs] = zub
                z2.append(z)
            a_ref[:, c0:c0 + cwid] = ffn_act(z2[0], z2[1]).astype(MX)

    cur = lambda w: pl.BlockSpec((tm, w), lambda i: (jnp.minimum(i, ns - 1), 0))
    late = lambda w: pl.BlockSpec((tm, w), lambda i: (jnp.maximum(i - 1, 0), 0))
    return _call(body, "ffn_up", (ns + 1,), [cur(D), _full(g), _full(w_up), _full(cw), _full(cb)],
                 [cur(NUP), cur(D), late(NUP), late(DFF)],
                 [_sds((T, NUP), MX), _sds((T, D), MX), _sds((T, NUP), MX), _sds((T, DFF), MX)],
                 scratch=[pltpu.VMEM((tm, NUP), MX), pltpu.VMEM((HALO, NUP), MX)])(x1, g, w_up, cw, cb)


def matmul_res(a, w, res, tm, name):
    T, k = a.shape
    n = w.shape[1]

    def body(a_ref, w_ref, r_ref, o_ref):
        o_ref[...] = r_ref[...] + jnp.dot(a_ref[...], w_ref[...], preferred_element_type=F32)

    return _call(body, name, (T // tm,), [_rows(tm, k), _full(w), _rows(tm, n)], _rows(tm, n), _sds((T, n)))(a, w, res)


def loss_head(x, g, tgt, tm):
    T = x.shape[0]

    def body(x_ref, g_ref, t_ref, l_ref, dx_ref, dg_ref):
        @pl.when(pl.program_id(0) == 0)
        def _():
            l_ref[...] = jnp.zeros_like(l_ref)
            dg_ref[...] = jnp.zeros_like(dg_ref)

        y, vjp = jax.vjp(rmsnorm, x_ref[...], g_ref[...])
        err = y - t_ref[...]
        l_ref[...] += jnp.sum(err * err, axis=0, keepdims=True)
        dx, dg = vjp(err * (1.0 / D))
        dx_ref[...] = dx
        dg_ref[...] += dg

    return _call(body, "loss_head", (T // tm,), [_rows(tm, D), _full(g), _rows(tm, D)],
                 [_full(g), _rows(tm, D), _full(g)], [_sds((1, D)), _sds((T, D)), _sds((1, D))])(x, g, tgt)


def ffn_down_bwd(dx2, z, w_down, tm):
    T = dx2.shape[0]

    def body(dx_ref, z_ref, w_ref, dz_ref):
        da = _dg(dx_ref[...], w_ref[...], 1, 1)
        zg, zv = z_ref[:, :DFF].astype(F32), z_ref[:, DFF:].astype(F32)
        s = sigmoid(zg)
        sz = zg * s
        dz_ref[:, :DFF] = (da * zv * (s + sz * (1.0 - s))).astype(MX)
        dz_ref[:, DFF:] = (da * sz).astype(MX)

    return _call(body, "ffn_down_bwd", (T // tm,), [_rows(tm, D), _rows(tm, NUP), _full(w_down)],
                 _rows(tm, NUP), _sds((T, NUP), MX))(dx2, z, w_down)


def ffn_up_bwd(dz, zu, cw, w_up, x1, g, dres, tm):
    T = dz.shape[0]
    ns = T // tm
    cwid = 512

    def body(dz_ref, dp_ref, dn_ref, zu_ref, cw_ref, w_ref, x_ref, g_ref, dr_ref,
             dzu_ref, dx_ref, dg_ref, dcw_ref, dcb_ref):
        i = pl.program_id(0)

        @pl.when(i == 0)
        def _():
            for r in (dg_ref, dcw_ref, dcb_ref):
                r[...] = jnp.zeros_like(r)

        row = lax.broadcasted_iota(jnp.int32, (tm, 1), 0)
        dh = jnp.zeros((tm, D), F32)
        for c0 in range(0, NUP, cwid):
            cs = slice(c0, c0 + cwid)
            dz = dz_ref[:, cs].astype(F32)
            zu = zu_ref[:, cs].astype(F32)
            pr = jnp.where(i > 0, dp_ref[HALO - 1:HALO, cs].astype(F32), 0.0)
            nx = jnp.where(i < ns - 1, dn_ref[0:1, cs].astype(F32), 0.0)
            ddn = jnp.where(row == 0, pr, pltpu.roll(dz, 1, 0))
            dup = jnp.where(row == tm - 1, nx, pltpu.roll(dz, tm - 1, 0))
            dzu = (dup * cw_ref[0:1, cs] + dz * cw_ref[1:2, cs] + ddn * cw_ref[2:3, cs]).astype(MX)
            dzu_ref[:, cs] = dzu
            dcw_ref[0:1, cs] += jnp.sum(zu * dup, axis=0, keepdims=True)
            dcw_ref[1:2, cs] += jnp.sum(zu * dz, axis=0, keepdims=True)
            dcw_ref[2:3, cs] += jnp.sum(zu * ddn, axis=0, keepdims=True)
            dcb_ref[:, cs] += jnp.sum(dz, axis=0, keepdims=True)
            dh = dh + _dg(dzu, w_ref[:, cs], 1, 1)
        _, vjp = jax.vjp(rmsnorm, x_ref[...], g_ref[...])
        dx, dg = vjp(dh)
        dx_ref[...] = dr_ref[...] + dx
        dg_ref[...] += dg

    return _call(body, "ffn_up_bwd", (ns,),
                 _halo_specs(T, tm, NUP) + [_rows(tm, NUP), _full(cw), _full(w_up), _rows(tm, D), _full(g), _rows(tm, D)],
                 [_rows(tm, NUP), _rows(tm, D), _full(g), _full(cw), pl.BlockSpec((1, NUP), lambda i: (0, 0))],
                 [_sds((T, NUP), MX), _sds((T, D)), _sds((1, D)), _sds((3, NUP)), _sds((1, NUP))])(
                     dz, dz, dz, zu, cw, w_up, x1, g, dres)


def nt_normbwd(dys, w, x, g, dres, tm, name):
    T = x.shape[0]
    n = len(dys)
    offs = [sum(d.shape[1] for d in dys[:i]) for i in range(n + 1)]

    def body(*refs):
        dy_refs, (w_ref, x_ref, g_ref, dr_ref, dx_ref, dg_ref) = refs[:n], refs[n:]

        @pl.when(pl.program_id(0) == 0)
        def _():
            dg_ref[...] = jnp.zeros_like(dg_ref)

        dh = _dg(dy_refs[0][...], w_ref[:, offs[0]:offs[1]], 1, 1)
        for i in range(1, n):
            dh = dh + _dg(dy_refs[i][...], w_ref[:, offs[i]:offs[i + 1]], 1, 1)
        _, vjp = jax.vjp(rmsnorm, x_ref[...], g_ref[...])
        dx, dg = vjp(dh)
        dx_ref[...] = dr_ref[...] + dx
        dg_ref[...] += dg

    return _call(body, name, (T // tm,),
                 [_rows(tm, d.shape[1]) for d in dys] + [_full(w), _rows(tm, D), _full(g), _rows(tm, D)],
                 [_rows(tm, D), _full(g)], [_sds((T, D)), _sds((1, D))])(*dys, w, x, g, dres)


def matmul_tn(a, b, tt, tn, name):
    T, k = a.shape
    n = b.shape[1]
    last = T // tt - 1

    def body(a_ref, b_ref, o_ref, acc_ref):
        @pl.when(pl.program_id(1) == 0)
        def _():
            acc_ref[...] = jnp.zeros_like(acc_ref)

        acc_ref[...] += _dg(a_ref[...], b_ref[...], 0, 0)

        @pl.when(pl.program_id(1) == last)
        def _():
            o_ref[...] = acc_ref[...].astype(MX)

    return _call(body, name, (n // tn, T // tt),
                 [pl.BlockSpec((tt, k), lambda j, i: (i, 0)), pl.BlockSpec((tt, tn), lambda j, i: (i, j))],
                 pl.BlockSpec((k, tn), lambda j, i: (0, j)), _sds((k, n), MX), scratch=[pltpu.VMEM((k, tn), F32)],
                 n_axes=2)(a, b)


def mix_out_bwd(dx1, w_out, of, ob, p, gg, tm):
    T = dx1.shape[0]

    def body(dx_ref, w_ref, of_ref, ob_ref, pg_ref, gg_ref, da_ref, do_ref, dpg_ref, dgg_ref):
        @pl.when(pl.program_id(0) == 0)
        def _():
            dgg_ref[...] = jnp.zeros_like(dgg_ref)

        dxb = dx_ref[...].astype(MX)
        da_ref[...] = _dg(dxb, w_ref[0:512, :], 1, 1)
        for h in range(4):
            sl = slice(h * 128, (h + 1) * 128)
            dm = _dg(dxb, w_ref[512 + h * 128:512 + (h + 1) * 128, :], 1, 1)
            _, vjp = jax.vjp(outb_head, of_ref[:, sl] + ob_ref[:, sl], pg_ref[:, sl], gg_ref[h])
            do, dpg, dg = vjp(dm)
            do_ref[:, sl] = do
            dpg_ref[:, sl] = dpg
            dgg_ref[h] += dg

    return _call(body, "mix_out_bwd", (T // tm,),
                 [_rows(tm, D), _full(w_out), _rows(tm, 512), _rows(tm, 512), _rows(tm, 512, 4), _full(gg)],
                 [_rows(tm, 512), _rows(tm, 512), _rows(tm, 512), _full(gg)],
                 [_sds((T, 512)), _sds((T, 512)), _sds((T, 512)), _sds(gg.shape)])(dx1, w_out, of, ob, p, gg)


def gla_bwd(p, wg, bg, ss, do, reverse, merge=None):
    T = p.shape[0]
    tm = CPB * CH
    n = T // tm
    rev = not reverse
    rn = n if rev else None

    def body(*refs):
        q_ref, k_ref, v_ref, r_ref, wg_ref, bg_ref, ss_ref, do_ref = refs[:8]
        if merge is None:
            dq_ref, dk_ref, dv_ref, dr_ref, dwg_ref, dbg_ref, dst_ref = refs[8:]
        else:
            mq_ref, mk_ref, mv_ref, mr_ref, mg_ref, out_ref, dwg_ref, dbg_ref, dst_ref, drs_ref = refs[8:]
            out_ref[:, 1024:1536] = mg_ref[...].astype(MX)

        @pl.when(pl.program_id(0) == 0)
        def _():
            dst_ref[...] = jnp.zeros_like(dst_ref)
            dwg_ref[...] = jnp.zeros_like(dwg_ref)
            dbg_ref[...] = jnp.zeros_like(dbg_ref)

        consts = _tri(reverse) + _lane_masks()
        order = list(reversed(range(CPB))) if reverse else list(range(CPB))
        for j in range(2):
            sl = slice(j * 128, (j + 1) * 128)
            v0s, v1s = slice(256 * j, 256 * j + 128), slice(256 * j + 128, 256 * j + 256)
            chunks, dout = [], []
            for c in order:
                rows = _chunk(c)
                chunks += [r_ref[rows, :], q_ref[rows, sl], k_ref[rows, sl], v_ref[rows, v0s], v_ref[rows, v1s]]
                dout += [do_ref[rows, v0s], do_ref[rows, v1s]]
            _, vjp = jax.vjp(functools.partial(gla_pair, consts), wg_ref[:, sl], bg_ref[:, sl],
                             ss_ref[0, 2 * j], ss_ref[0, 2 * j + 1], *chunks)
            g = vjp((*dout, dst_ref[2 * j], dst_ref[2 * j + 1]))
            dwg_ref[:, sl] += g[0]
            dbg_ref[:, sl] += g[1]
            dst_ref[2 * j] = g[2]
            dst_ref[2 * j + 1] = g[3]
            for i, c in enumerate(order):
                rows = _chunk(c)
                dr, dq, dk, dv0, dv1 = g[4 + 5 * i:9 + 5 * i]
                if merge is None:
                    if j == 0:
                        dr_ref[rows, :] = dr
                    else:
                        dr_ref[rows, :] += dr
                    dq_ref[rows, sl] = dq
                    dk_ref[rows, sl] = dk
                    dv_ref[rows, v0s] = dv0
                    dv_ref[rows, v1s] = dv1
                else:
                    if j == 0:
                        drs_ref[rows, :] = mr_ref[rows, :] + dr
                    else:
                        out_ref[rows, 1536:1664] = (drs_ref[rows, :] + dr).astype(MX)
                    out_ref[rows, sl] = (mq_ref[rows, sl] + dq).astype(MX)
                    out_ref[rows, 256 + 128 * j:384 + 128 * j] = (mk_ref[rows, sl] + dk).astype(MX)
                    out_ref[rows, 512 + 256 * j:640 + 256 * j] = (mv_ref[rows, v0s] + dv0).astype(MX)
                    out_ref[rows, 640 + 256 * j:768 + 256 * j] = (mv_ref[rows, v1s] + dv1).astype(MX)

    ss_spec = pl.BlockSpec((1, 4, 128, 128), (lambda i: (n - 1 - i, 0, 0, 0)) if rev else (lambda i: (i, 0, 0, 0)))
    ins = [p, p, p, p, wg, bg, ss, do]
    in_specs = _gla_in_specs(tm, n, rev) + [_full(wg), _full(bg), ss_spec, _rows(tm, 512, 0, rn)]
    scratch = [pltpu.VMEM((4, 128, 128), F32)]
    if merge is None:
        out_specs = [_rows(tm, 256, 0, rn), _rows(tm, 256, 0, rn), _rows(tm, 512, 0, rn), _rows(tm, 128, 0, rn)]
        out_shape = [_sds((T, 256)), _sds((T, 256)), _sds((T, 512)), _sds((T, 128))]
    else:
        ins += list(merge)
        in_specs += [_rows(tm, a.shape[1], 0, rn) for a in merge]
        out_specs, out_shape = [_rows(tm, 1664, 0, rn)], [_sds((T, 1664), MX)]
        scratch.append(pltpu.VMEM((tm, 128), F32))
    return _call(body, "gla_bwd_r" if reverse else "gla_bwd_f", (n,), in_specs, out_specs + [_full(wg), _full(bg)],
                 out_shape + [_sds(wg.shape), _sds(bg.shape)], scratch=scratch)(*ins)


def gmlp_bwd(p, douta, ws, bs, lg, lb):
    T = p.shape[0]
    cpb = 4
    tm = cpb * CH

    def body(pa_ref, do_ref, ws_ref, bs_ref, lg_ref, lb_ref, dpa_ref, dws_ref, dbs_ref, dlg_ref, dlb_ref):
        @pl.when(pl.program_id(0) == 0)
        def _():
            for r in (dws_ref, dbs_ref, dlg_ref, dlb_ref):
                r[...] = jnp.zeros_like(r)

        us = [slice(h * 128, (h + 1) * 128) for h in range(4)]
        vs = [slice(512 + h * 128, 512 + (h + 1) * 128) for h in range(4)]
        params = [(ws_ref[h], bs_ref[h], lg_ref[h], lb_ref[h]) for h in range(4)]
        pieces = [[(pa_ref[_chunk(c), us[h]], pa_ref[_chunk(c), vs[h]]) for h in range(4)] for c in range(cpb)]
        _, vjp = jax.vjp(gmlp_heads, params, pieces)
        dparams, dpieces = vjp([[do_ref[_chunk(c), us[h]] for h in range(4)] for c in range(cpb)])
        for h in range(4):
            for r, a in zip((dws_ref, dbs_ref, dlg_ref, dlb_ref), dparams[h]):
                r[h] += a
            for c in range(cpb):
                dpa_ref[_chunk(c), us[h]] = dpieces[c][h][0].astype(MX)
                dpa_ref[_chunk(c), vs[h]] = dpieces[c][h][1].astype(MX)

    return _call(body, "gmlp_bwd", (T // tm,),
                 [_rows(tm, 1024), _rows(tm, 512), _full(ws), _full(bs), _full(lg), _full(lb)],
                 [_rows(tm, 1024), _full(ws), _full(bs), _full(lg), _full(lb)],
                 [_sds((T, 1024), MX), _sds(ws.shape), _sds(bs.shape), _sds(lg.shape), _sds(lb.shape)])(p, douta, ws, bs, lg, lb)


def _gate_pad(w, row0):
    return jnp.zeros((128, 256), F32).at[row0:row0 + 16].set(w)


def local_step(x, tgt, W, get_big, emit, tm=256, tmm=512):
    saved = []
    for l in range(NL):
        s = {"x": x}
        s.update(get_big(l, "in", x))
        p, s["h"] = norm_matmul(x, W["g_mix"][l][None], s["w_in"], tmm, "mix_in")
        s["p"] = p
        ws, bs = W["w_s"][l], W["b_s"][l][:, :, None]
        lg, lb = W["ln_g"][l][:, None, :], W["ln_b"][l][:, None, :]
        outa = gmlp_fwd(p, ws, bs, lg, lb)
        wgf, wgb = _gate_pad(W["w_gate_f"][l], 0), _gate_pad(W["w_gate_b"][l], 16)
        bgf, bgb = W["b_gate_f"][l][None], W["b_gate_b"][l][None]
        s["of"], s["ssf"] = gla_fwd(p, wgf, bgf, False)
        s["ob"], s["ssb"] = gla_fwd(p, wgb, bgb, True)
        s.update(get_big(l, "rest", s["ob"]))
        gg = W["g_gla"][l][:, None, :]
        x1, s["mixed"] = mix_out(x, s["of"], s["ob"], p, outa, gg, s["w_out"], tmm)
        s["x1"] = x1
        s["zu"], s["h2"], s["z"], s["a"] = ffn_up_conv(x1, W["g_ffn"][l][None], s["w_up"], W["conv_w"][l],
                                                       W["conv_b"][l][None], tm)
        x = matmul_res(s["a"], s["w_down"], x1, tmm, "ffn_down")
        saved.append(s)

    lsum, dx, dgf = loss_head(x, W["g_final"][None], tgt, tmm)
    G = {k: [None] * NL for k in _SMALL if k != "g_final"}
    tok = jnp.zeros((1, 1), F32)
    for l in reversed(range(NL)):
        s = saved[l]
        g_down = matmul_tn(s["a"], dx, min(1024, tmm * 2), 512, "dw_down")
        dz = ffn_down_bwd(dx, s["z"], s["w_down"], tm)
        dzu, dx1, dg, G["conv_w"][l], dcb = ffn_up_bwd(dz, s["zu"], W["conv_w"][l] + tok, s["w_up"], s["x1"],
                                                       W["g_ffn"][l][None], dx, tm)
        G["conv_b"][l], G["g_ffn"][l] = dcb[0], dg[0]
        g_up = matmul_tn(s["h2"], dzu, min(1024, tmm * 2), 1408, "dw_up")
        tok = emit(l, "A", {"w_down": g_down, "w_up": g_up})
        g_out = matmul_tn(s["mixed"], dx1, min(1024, tmm * 2), 1024, "dw_out")
        gg = W["g_gla"][l][:, None, :] + tok
        douta, do, dpg, dgg = mix_out_bwd(dx1, s["w_out"], s["of"], s["ob"], s["p"], gg, tmm)
        G["g_gla"][l] = dgg[:, 0, :]
        wgf, wgb = _gate_pad(W["w_gate_f"][l], 0), _gate_pad(W["w_gate_b"][l], 16)
        bgf, bgb = W["b_gate_f"][l][None], W["b_gate_b"][l][None]
        dqf, dkf, dvf, drf, dwgf, dbgf = gla_bwd(s["p"], wgf, bgf, s["ssf"], do, False)
        dpb, dwgb, dbgb = gla_bwd(s["p"], wgb, bgb, s["ssb"], do, True, merge=(dqf, dkf, dvf, drf, dpg))
        G["w_gate_f"][l], G["b_gate_f"][l] = dwgf[0:16], dbgf[0]
        G["w_gate_b"][l], G["b_gate_b"][l] = dwgb[16:32], dbgb[0]
        ws, bs = W["w_s"][l], W["b_s"][l][:, :, None]
        lg, lb = W["ln_g"][l][:, None, :], W["ln_b"][l][:, None, :]
        dpa, G["w_s"][l], dbs, dlg, dlb = gmlp_bwd(s["p"], douta, ws, bs, lg, lb)
        G["b_s"][l], G["ln_g"][l], G["ln_b"][l] = dbs[:, :, 0], dlg[:, 0, :], dlb[:, 0, :]
        tt = min(1024, tmm * 2)
        g_in = jnp.concatenate([matmul_tn(s["h"], dpa, tt, 1024, "dw_in_a"),
                                matmul_tn(s["h"], dpb, tt, 1664, "dw_in_b")], axis=1)
        tok = emit(l, "B", {"w_out": g_out, "w_in": g_in})
        dx, dg = nt_normbwd([dpa, dpb], s["w_in"], s["x"], W["g_mix"][l][None] + tok, dx1, tmm, "mix_in_bwd")
        G["g_mix"][l] = dg[0]
    G = {k: jnp.stack(v) for k, v in G.items()}
    G["g_final"] = dgf[0]
    return lsum, dx, G


def _rows3(tr, c):
    return pl.BlockSpec((None, tr, c), lambda l, i: (l, i, 0))


def cast_bf16(a, tr):
    nl, r, c = a.shape

    def body(a_ref, o_ref):
        o_ref[...] = a_ref[...].astype(BF16)

    return _call(body, "cast_bf16", (nl, r // tr), [_rows3(tr, c)], _rows3(tr, c), _sds(a.shape, BF16), n_axes=2)(a)


def sum_parts(land, grad, k, chipvec, tr):
    _, rr, cc = land.shape
    nb = rr // tr

    def body(c_ref, l_ref, g_ref, o_ref):
        mine = g_ref[...].astype(F32)
        acc = None
        for j in range(4):
            part = jnp.where(c_ref[0] == j, mine, l_ref[j].astype(F32))
            acc = part if acc is None else acc + part
        o_ref[...] = acc

    gs = pltpu.PrefetchScalarGridSpec(
        num_scalar_prefetch=1, grid=(nb,),
        in_specs=[pl.BlockSpec((4, tr, cc), lambda i, c: (0, i, 0)), _part_spec(k, tr, nb)],
        out_specs=pl.BlockSpec((tr, cc), lambda i, c: (i, 0)))
    return pl.pallas_call(body, name="sum_parts", grid_spec=gs, out_shape=_sds((rr, cc)),
                          compiler_params=_cparams(1))(chipvec, land, grad)


def adamw(w, ga, gb, m, v, tr):
    nl, r, c = w.shape

    def body(w_ref, ga_ref, gb_ref, m_ref, v_ref, g_ref, d_ref, nm_ref, nv_ref):
        gr = ga_ref[...] + gb_ref[...]
        g_ref[...] = gr
        nm = ADAM_B1 * m_ref[...] + (1.0 - ADAM_B1) * gr
        nv = ADAM_B2 * v_ref[...] + (1.0 - ADAM_B2) * jnp.square(gr)
        m_hat = nm / (1.0 - ADAM_B1 ** ADAM_STEP)
        v_hat = nv / (1.0 - ADAM_B2 ** ADAM_STEP)
        d_ref[...] = -ADAM_LR * (m_hat / (jnp.sqrt(v_hat) + ADAM_EPS) + ADAM_WD * w_ref[...])
        nm_ref[...] = nm
        nv_ref[...] = nv

    sp = _rows3(tr, c)
    return _call(body, "adamw", (nl, r // tr), [sp] * 5, [sp] * 4, [_sds(w.shape)] * 4, n_axes=2)(w, ga, gb, m, v)


MESH = pl.DeviceIdType.MESH
ANY = pl.BlockSpec(memory_space=pl.ANY)
N_BIG = 4


def _pos():
    return lax.axis_index("x"), lax.axis_index("y"), lax.axis_index("c")


def _other_chips(x, y):
    return [(1 - x, y), (x, 1 - y), (1 - x, 1 - y)]


def _slab(k, ref, j):
    if k == 0:
        return ref.at[j]
    if k == 1:
        return ref.at[pl.ds(256 * j, 256), :]
    if k == 2:
        return ref.at[:, pl.ds(1408 * j, 1408)]
    return ref.at[pl.ds(704 * j, 704), :]


_LAYER_FULL = [(4, 1024, 648), (1024, 1024), (1024, NUP), (DFF, 1024)]
_LAYER_SHARD = [(1024, 648), (256, 1024), (1024, 1408), (704, 1024)]
_SHARD_SHAPES = [(NL,) + s for s in _LAYER_SHARD]

HBM = pl.BlockSpec(memory_space=pltpu.HBM)
SEM = pl.BlockSpec(memory_space=pltpu.SEMAPHORE)
VM = pl.BlockSpec(memory_space=pltpu.VMEM)
EFFECT = pltpu.SideEffectType.DATAFLOW_SIDE_EFFECTING
_GW_GROUPS = [[(0, 0)], [(0, 1), (0, 2), (0, 3)]] + [[(l, k) for k in range(N_BIG)] for l in range(1, NL)]
_GW_ORDER = [lk for g in _GW_GROUPS for lk in g]


def _hbm(a):
    return pltpu.with_memory_space_constraint(a, pltpu.HBM)


def _hbm_like(a):
    return pltpu.HBM(a.shape, a.dtype)


def _part_spec(k, tr, nb):
    cc = _LAYER_SHARD[k][1]
    if k == 0:
        return pl.BlockSpec((None, tr, cc), lambda i, c: (c[0], i, 0))
    if k == 2:
        return pl.BlockSpec((tr, cc), lambda i, c: (i, c[0]))
    return pl.BlockSpec((tr, cc), lambda i, c: (c[0] * nb + i, 0))


def place_own(shard, landing, l, k, chipvec, tr):
    rr, cc = _LAYER_SHARD[k]
    nb = rr // tr

    def body(c_ref, s_ref, l_ref, o_ref):
        o_ref[...] = s_ref[...]

    gs = pltpu.PrefetchScalarGridSpec(
        num_scalar_prefetch=1, grid=(nb,),
        in_specs=[pl.BlockSpec((None, tr, cc), lambda i, c: (l, i, 0)), ANY], out_specs=_part_spec(k, tr, nb))
    return pl.pallas_call(body, name="place_own", grid_spec=gs, out_shape=_sds(landing.shape, landing.dtype),
                          input_output_aliases={2: 0}, compiler_params=_cparams(1))(chipvec, shard, landing)


def gw_start(shards, landings):
    n = len(_GW_ORDER)

    def body(*refs):
        S, Ld = refs[:N_BIG], refs[N_BIG:N_BIG + n]
        outs = refs[N_BIG + n:]
        send_sems, recv, token = outs[0], outs[1:1 + len(_GW_GROUPS)], outs[-1]
        x, y, c = _pos()
        me = 2 * x + y
        ci = 0
        for gi, grp in enumerate(_GW_GROUPS):
            for t, (l, k) in enumerate(grp):
                land = Ld[_GW_ORDER.index((l, k))]
                for j, (px, py) in enumerate(_other_chips(x, y)):
                    pltpu.make_async_remote_copy(
                        src_ref=S[k].at[l], dst_ref=_slab(k, land, me), send_sem=send_sems.at[ci],
                        recv_sem=recv[gi].at[3 * t + j], device_id=(px, py, c), device_id_type=MESH).start()
                    ci += 1
        token[...] = jnp.zeros_like(token)

    ins = list(shards) + list(landings)
    sems = [pltpu.SemaphoreType.DMA((3 * n,))] + [pltpu.SemaphoreType.DMA((3 * len(g),)) for g in _GW_GROUPS]
    outs = pl.pallas_call(
        body, name="gw_start", out_shape=sems + [_hbm_like(a) for a in ins] + [_sds((8, 128))],
        in_specs=[HBM] * len(ins), out_specs=[SEM] * len(sems) + [HBM] * len(ins) + [VM],
        input_output_aliases={i: len(sems) + i for i in range(len(ins))},
        compiler_params=pltpu.CompilerParams(has_side_effects=EFFECT))(*[_hbm(a) for a in ins])
    ns = len(sems)
    return outs[0], outs[1:ns], outs[ns:ns + N_BIG], outs[ns + N_BIG:ns + len(ins)], outs[-1]


def gw_wait(gi, landings, recv_sems, after, shards=None, send_sems=None):
    grp = _GW_GROUPS[gi]
    n = len(grp)
    last = shards is not None

    def body(*refs):
        Ld, rs = refs[:n], refs[n]
        x, y, c = _pos()
        for t, (l, k) in enumerate(grp):
            for j, (px, py) in enumerate(_other_chips(x, y)):
                region = _slab(k, Ld[t], 2 * px + py)
                pltpu.make_async_remote_copy(src_ref=region, dst_ref=region, send_sem=rs.at[3 * t + j],
                                             recv_sem=rs.at[3 * t + j], device_id=(px, py, c),
                                             device_id_type=MESH).wait_recv()
        if last:
            S, ss = refs[n + 2:n + 2 + N_BIG], refs[n + 2 + N_BIG]
            me = 2 * x + y
            for ci, (l, k) in enumerate(lk for lk in _GW_ORDER for _ in range(3)):
                pltpu.make_async_remote_copy(src_ref=S[k].at[l], dst_ref=_slab(k, Ld[k], me), send_sem=ss.at[ci],
                                             recv_sem=ss.at[ci], device_id=(x, y, c), device_id_type=MESH).wait_send()

    ins = list(landings) + [recv_sems, after]
    specs = [HBM] * n + [SEM, pl.BlockSpec(memory_space=pl.ANY)]
    outs = [_hbm_like(a) for a in landings]
    alias = {i: i for i in range(n)}
    if last:
        ins += list(shards) + [send_sems]
        specs += [HBM] * N_BIG + [SEM]
        outs += [_hbm_like(a) for a in shards]
        alias.update({n + 2 + i: n + i for i in range(N_BIG)})
    res = pl.pallas_call(body, name="gw_wait_%d" % gi, out_shape=outs, in_specs=specs, out_specs=[HBM] * len(outs),
                         input_output_aliases=alias,
                         compiler_params=pltpu.CompilerParams(has_side_effects=EFFECT))(*ins)
    return res[:n]


def ga_start(tag, ks, grads, landings):
    n = len(ks)

    def body(*refs):
        G, Ld = refs[:n], refs[n:2 * n]
        send_sems, recv_sems, token = refs[2 * n], refs[2 * n + 1], refs[-1]
        x, y, c = _pos()
        me = 2 * x + y
        for t, k in enumerate(ks):
            for j, (px, py) in enumerate(_other_chips(x, y)):
                pltpu.make_async_remote_copy(
                    src_ref=_slab(k, G[t], 2 * px + py), dst_ref=Ld[t].at[me], send_sem=send_sems.at[3 * t + j],
                    recv_sem=recv_sems.at[3 * t + j], device_id=(px, py, c), device_id_type=MESH).start()
        token[...] = jnp.zeros_like(token)

    ins = list(grads) + list(landings)
    sems = [pltpu.SemaphoreType.DMA((3 * n,))] * 2
    outs = pl.pallas_call(
        body, name="ga_start_" + tag, out_shape=sems + [_hbm_like(a) for a in ins] + [_sds((8, 128))],
        in_specs=[HBM] * len(ins), out_specs=[SEM, SEM] + [HBM] * len(ins) + [VM],
        input_output_aliases={i: 2 + i for i in range(len(ins))},
        compiler_params=pltpu.CompilerParams(has_side_effects=EFFECT))(*[_hbm(a) for a in ins])
    return outs[0], outs[1], outs[2:2 + n], outs[2 + n:2 + 2 * n], outs[-1]


def ga_wait(tag, ks, send_sems, recv_sems, grads, landings, after):
    n = len(ks)

    def body(*refs):
        G, Ld, ss, rs = refs[:n], refs[n:2 * n], refs[2 * n], refs[2 * n + 1]
        x, y, c = _pos()
        me = 2 * x + y
        for t, k in enumerate(ks):
            for j, (px, py) in enumerate(_other_chips(x, y)):
                pj = 2 * px + py
                cp = pltpu.make_async_remote_copy(
                    src_ref=_slab(k, G[t], pj), dst_ref=Ld[t].at[pj], send_sem=ss.at[3 * t + j],
                    recv_sem=rs.at[3 * t + j], device_id=(px, py, c), device_id_type=MESH)
                cp.wait_send()
                cp.wait_recv()

    ins = list(grads) + list(landings) + [send_sems, recv_sems, after]
    res = pl.pallas_call(
        body, name="ga_wait_" + tag, out_shape=[_hbm_like(a) for a in list(grads) + list(landings)],
        in_specs=[HBM] * (2 * n) + [SEM, SEM, pl.BlockSpec(memory_space=pl.ANY)], out_specs=[HBM] * (2 * n),
        input_output_aliases={i: i for i in range(2 * n)},
        compiler_params=pltpu.CompilerParams(has_side_effects=EFFECT))(*ins)
    return res[:n], res[n:]


def swap_start(parts):
    n = len(parts)

    def body(*refs):
        Q, Ld, sems = refs[:n], refs[n:2 * n], refs[2 * n:4 * n]
        x, y, c = _pos()
        for k in range(n):
            pltpu.make_async_remote_copy(src_ref=Q[k], dst_ref=Ld[k], send_sem=sems[k].at[0], recv_sem=sems[n + k].at[0],
                                         device_id=(x, y, 1 - c), device_id_type=MESH).start()
        refs[-1][...] = jnp.zeros_like(refs[-1])

    ins = list(parts) + [lax.empty(p.shape, p.dtype) for p in parts]
    outs = pl.pallas_call(
        body, name="swap_start",
        out_shape=[pltpu.SemaphoreType.DMA((1,))] * (2 * n) + [_hbm_like(a) for a in ins] + [_sds((8, 128))],
        in_specs=[HBM] * (2 * n), out_specs=[SEM] * (2 * n) + [HBM] * (2 * n) + [VM],
        input_output_aliases={i: 2 * n + i for i in range(2 * n)},
        compiler_params=pltpu.CompilerParams(has_side_effects=EFFECT))(*[_hbm(a) for a in ins])
    return outs[:n], outs[n:2 * n], outs[2 * n:3 * n], outs[3 * n:4 * n]


def swap_wait(k, send_sem, recv_sem, part, landing, after):
    def body(q_ref, l_ref, ss, rs, after_ref, q_out, l_out):
        x, y, c = _pos()
        cp = pltpu.make_async_remote_copy(src_ref=q_ref, dst_ref=l_ref, send_sem=ss.at[0], recv_sem=rs.at[0],
                                          device_id=(x, y, 1 - c), device_id_type=MESH)
        cp.wait_send()
        cp.wait_recv()

    return pl.pallas_call(
        body, name="swap_wait_%d" % k, out_shape=[_hbm_like(part), _hbm_like(landing)],
        in_specs=[HBM, HBM, SEM, SEM, pl.BlockSpec(memory_space=pl.ANY)], out_specs=[HBM, HBM],
        input_output_aliases={0: 0, 1: 1},
        compiler_params=pltpu.CompilerParams(has_side_effects=EFFECT))(part, landing, send_sem, recv_sem, after)


def _peer(x, y, c, r):
    fx, fy, fc = (r >> 2) & 1, (r >> 1) & 1, r & 1
    return ((1 - x) if fx else x, (1 - y) if fy else y, (1 - c) if fc else c)


def ag_start(tag, pack):
    rr, cc = pack.shape

    def body(p_ref, l_ref, ss, rs, p_out, l_out, token):
        x, y, c = _pos()
        me = 4 * x + 2 * y + c
        for r in range(1, 8):
            pltpu.make_async_remote_copy(src_ref=p_ref, dst_ref=l_ref.at[me], send_sem=ss.at[r - 1], recv_sem=rs.at[r - 1],
                                         device_id=_peer(x, y, c, r), device_id_type=MESH).start()
        token[...] = jnp.zeros_like(token)

    outs = pl.pallas_call(
        body, name="ag_start_" + tag,
        out_shape=[pltpu.SemaphoreType.DMA((7,)), pltpu.SemaphoreType.DMA((7,)), _hbm_like(pack),
                   pltpu.HBM((8, rr, cc), pack.dtype), _sds((8, 128))],
        in_specs=[HBM, HBM], out_specs=[SEM, SEM, HBM, HBM, VM], input_output_aliases={0: 2, 1: 3},
        compiler_params=pltpu.CompilerParams(has_side_effects=EFFECT))(_hbm(pack), _hbm(lax.empty((8, rr, cc), pack.dtype)))
    return outs


def ag_wait(tag, send_sems, recv_sems, pack, landing, after):
    def body(p_ref, l_ref, ss, rs, after_ref, p_out, l_out):
        x, y, c = _pos()
        for r in range(1, 8):
            px, py, pc = _peer(x, y, c, r)
            cp = pltpu.make_async_remote_copy(src_ref=p_ref, dst_ref=l_ref.at[4 * px + 2 * py + pc], send_sem=ss.at[r - 1],
                                              recv_sem=rs.at[r - 1], device_id=(px, py, pc), device_id_type=MESH)
            cp.wait_send()
            cp.wait_recv()

    return pl.pallas_call(
        body, name="ag_wait_" + tag, out_shape=[_hbm_like(pack), _hbm_like(landing)],
        in_specs=[HBM, HBM, SEM, SEM, pl.BlockSpec(memory_space=pl.ANY)], out_specs=[HBM, HBM],
        input_output_aliases={0: 0, 1: 1},
        compiler_params=pltpu.CompilerParams(has_side_effects=EFFECT))(pack, landing, send_sems, recv_sems, after)


def sum_slots(landing, own, mevec):
    _, rr, cc = landing.shape

    def body(m_ref, l_ref, o_ref, out_ref):
        mine = o_ref[...]
        acc = None
        for j in range(8):
            part = jnp.where(m_ref[0] == j, mine, l_ref[j])
            acc = part if acc is None else acc + part
        out_ref[...] = acc

    gs = pltpu.PrefetchScalarGridSpec(
        num_scalar_prefetch=1, grid=(1,),
        in_specs=[pl.BlockSpec((8, rr, cc), lambda i, m: (0, 0, 0)), pl.BlockSpec((rr, cc), lambda i, m: (0, 0))],
        out_specs=pl.BlockSpec((rr, cc), lambda i, m: (0, 0)))
    return pl.pallas_call(body, name="sum_slots", grid_spec=gs, out_shape=_sds((rr, cc)),
                          compiler_params=_cparams(1))(mevec, landing, own)


_WEIGHTS = ['g_mix', 'w_in', 'w_s', 'b_s', 'ln_g', 'ln_b', 'w_gate_f', 'b_gate_f', 'w_gate_b', 'b_gate_b', 'g_gla',
            'w_out', 'g_ffn', 'w_up', 'conv_w', 'conv_b', 'w_down', 'g_final']
_BIG = ['w_in', 'w_out', 'w_up', 'w_down']
_SMALL = [n for n in _WEIGHTS if n not in _BIG]
_SMALL_SHARDED = {'w_gate_f': 64, 'w_gate_b': 64, 'conv_w': 1408}
_BIG_TR = {'w_in': 512, 'w_out': 256, 'w_up': 256, 'w_down': 352}


def _pack(arrs):
    flat = jnp.concatenate([a.reshape(-1) for a in arrs])
    pad = (-flat.shape[0]) % 1024
    return jnp.pad(flat, (0, pad)).reshape(-1, 128)


def _unpack(buf, shapes):
    flat = buf.reshape(-1)
    out, o = [], 0
    for s in shapes:
        n = 1
        for d in s:
            n *= d
        out.append(flat[o:o + n].reshape(s))
        o += n
    return out


def kernel(x, g_mix, w_in, w_s, b_s, ln_g, ln_b, w_gate_f, b_gate_f, w_gate_b, b_gate_b, g_gla, w_out, g_ffn, w_up, conv_w, conv_b, w_down, g_final, loss_target, m_g_mix, m_w_in, m_w_s, m_b_s, m_ln_g, m_ln_b, m_w_gate_f, m_b_gate_f, m_w_gate_b, m_b_gate_b, m_g_gla, m_w_out, m_g_ffn, m_w_up, m_conv_w, m_conv_b, m_w_down, m_g_final, v_g_mix, v_w_in, v_w_s, v_b_s, v_ln_g, v_ln_b, v_w_gate_f, v_b_gate_f, v_w_gate_b, v_b_gate_b, v_g_gla, v_w_out, v_g_ffn, v_w_up, v_conv_w, v_conv_b, v_w_down, v_g_final):
    loc = locals()
    w = {n: loc[n] for n in _WEIGHTS}
    m = {n: loc["m_" + n] for n in _WEIGHTS}
    v = {n: loc["v_" + n] for n in _WEIGHTS}
    xi, yi, ci = _pos()
    chip = 2 * xi + yi
    me = 2 * chip + ci
    mevec = jnp.reshape(me, (1,)).astype(jnp.int32)

    sh_names = list(_SMALL_SHARDED)
    ss_w, rs_w, pk_w, land_w, _ = ag_start("w", _pack([w[n] for n in sh_names]))

    shards = [cast_bf16(w[n], _BIG_TR[n]) for n in _BIG]
    chipvec = jnp.reshape(chip, (1,)).astype(jnp.int32)
    landings = [place_own(shards[k], lax.empty(_LAYER_FULL[k], BF16), l, k, chipvec, _BIG_TR[_BIG[k]])
                for l, k in _GW_ORDER]
    send_sems, recv_sems, shards_fly, landings_fly, _ = gw_start(shards, landings)

    pk_w, land_w = ag_wait("w", ss_w, rs_w, pk_w, land_w, landings[-1])
    per_chip = [_unpack(jnp.where(me == 2 * j, pk_w, land_w[2 * j]), [w[n].shape for n in sh_names]) for j in range(4)]
    W = dict(w)
    for k, n in enumerate(sh_names):
        W[n] = jnp.concatenate([per_chip[j][k] for j in range(4)], axis=-1)
    arrived = {}

    def get_big(l, stage, after):
        gi = {(0, "in"): 0, (0, "rest"): 1}.get((l, stage), l + 1 if stage == "in" else None)
        if gi is not None:
            lo = sum(len(g) for g in _GW_GROUPS[:gi])
            lands = landings_fly[lo:lo + len(_GW_GROUPS[gi])]
            if gi == len(_GW_GROUPS) - 1:
                full = gw_wait(gi, lands, recv_sems[gi], after, shards_fly, send_sems)
            else:
                full = gw_wait(gi, lands, recv_sems[gi], after)
            arrived.update(zip(_GW_GROUPS[gi], full))
        if stage == "in":
            f_in = jnp.transpose(arrived[(l, 0)], (1, 0, 2)).reshape(D, N_IN)
            return {"w_in": jnp.pad(f_in, ((0, 0), (0, N_INP - N_IN)))}
        return {"w_out": arrived[(l, 1)], "w_up": arrived[(l, 2)], "w_down": arrived[(l, 3)]}

    flying = []

    def emit(l, group, grads):
        ks = [3, 2] if group == "A" else [1, 0]
        gs = [grads[_BIG[k]] for k in ks]
        if group == "B":
            gs[1] = jnp.transpose(gs[1][:, :N_IN].reshape(D, 4, 648), (1, 0, 2))
        lands = [lax.empty((4,) + _LAYER_SHARD[k], BF16) for k in ks]
        tag = "%d%s" % (l, group)
        ss, rs, gs_fly, lands_fly, tok = ga_start(tag, ks, gs, lands)
        flying.append((tag, l, ks, ss, rs, gs_fly, lands_fly))
        return tok[0:1, 0:1]

    lsum, grad_x, G = local_step(x[0], loss_target[0], W, get_big, emit)

    small_shapes = [G[n].shape for n in _SMALL] + [(D,)]
    ss_g, rs_g, pk_g, land_g, _ = ag_start("g", _pack([G[n] for n in _SMALL] + [lsum]))

    plane = [[None] * NL for _ in range(N_BIG)]
    for tag, l, ks, ss, rs, gs_fly, lands_fly in flying:
        for k, g, a in zip(ks, *ga_wait(tag, ks, ss, rs, gs_fly, lands_fly, grad_x)):
            plane[k][l] = sum_parts(a, g, k, chipvec, _BIG_TR[_BIG[k]])
    ss_p, rs_p, plane_fly, other_fly = swap_start([jnp.stack(p) for p in plane])
    grads, delta, new_m, new_v = {}, {}, {}, {}
    after = grad_x
    for k in (1, 0, 3, 2):
        n = _BIG[k]
        mine, other = swap_wait(k, ss_p[k], rs_p[k], plane_fly[k], other_fly[k], after)
        grads[n], delta[n], new_m[n], new_v[n] = adamw(w[n], mine, other, m[n], v[n], _BIG_TR[n])
        after = delta[n]

    pk_g, land_g = ag_wait("g", ss_g, rs_g, pk_g, land_g, after)
    small = dict(zip(_SMALL + ["lsum"], _unpack(sum_slots(land_g, pk_g, mevec), small_shapes)))
    loss = 0.5 * jnp.sum(small.pop("lsum")) / D
    for n, wd in _SMALL_SHARDED.items():
        small[n] = lax.dynamic_slice_in_dim(small[n], chip * wd, wd, axis=small[n].ndim - 1)
    grads.update(small)
    shapes = [w[n].shape for n in _SMALL]
    pw, pg, pm, pv = (_pack([t[n] for n in _SMALL])[None] for t in (w, grads, m, v))
    _, d_, m_, v_ = adamw(pw, pg, jnp.zeros_like(pg), pm, pv, pw.shape[1])
    for t, buf in ((delta, d_), (new_m, m_), (new_v, v_)):
        t.update(zip(_SMALL, _unpack(buf, shapes)))

    return (loss, grad_x[None], *[grads[n] for n in _WEIGHTS], *[delta[n] for n in _WEIGHTS],
            *[new_m[n] for n in _WEIGHTS], *[new_v[n] for n in _WEIGHTS])
```

```python
import functools

import jax
import jax.numpy as jnp
from jax import lax
from jax.experimental import pallas as pl
from jax.experimental.pallas import tpu as pltpu

F32 = jnp.float32
BF16 = jnp.bfloat16
MX = BF16

D = 1024
CH = 128
NL = 4
N_IN = 2592
N_INP = 2688
NUP = 5632
DFF = 2816
EPS = 1e-6
VMEM_LIMIT = 56 * 1024 * 1024

ADAM_LR, ADAM_B1, ADAM_B2, ADAM_EPS, ADAM_WD, ADAM_STEP = 0.001, 0.9, 0.999, 1e-08, 0.01, 10


def _dg(a, b, ca, cb):
    return lax.dot_general(a.astype(MX), b.astype(MX), (((ca,), (cb,)), ((), ())), preferred_element_type=F32)


@jax.custom_vjp
def mm(a, b):
    return _dg(a, b, 1, 0)


mm.defvjp(lambda a, b: (_dg(a, b, 1, 0), (a, b)),
          lambda r, g: (_dg(g, r[1], 1, 1), _dg(r[0], g, 0, 0)))


@jax.custom_vjp
def mm_nt(a, b):
    return _dg(a, b, 1, 1)


mm_nt.defvjp(lambda a, b: (_dg(a, b, 1, 1), (a, b)),
             lambda r, g: (_dg(g, r[1], 1, 0), _dg(g, r[0], 0, 0)))


@jax.custom_vjp
def mm_tn(a, b):
    return _dg(a, b, 0, 0)


mm_tn.defvjp(lambda a, b: (_dg(a, b, 0, 0), (a, b)),
             lambda r, g: (_dg(r[1], g, 1, 1), _dg(r[0], g, 1, 0)))


def _split3(x):
    hi = x.astype(BF16)
    r1 = x - hi.astype(F32)
    mid = r1.astype(BF16)
    lo = (r1 - mid.astype(F32)).astype(BF16)
    return hi, mid, lo


def _dot3(m, x):
    hi, mid, lo = _split3(x)
    d = lambda p: lax.dot_general(m, p, (((1,), (0,)), ((), ())), preferred_element_type=F32)
    return d(hi) + d(mid) + d(lo)


@jax.custom_vjp
def cumdot(m, mt, x):
    return _dot3(m, x)


cumdot.defvjp(lambda m, mt, x: (_dot3(m, x), (m, mt)),
              lambda r, g: (jnp.zeros_like(r[0]), jnp.zeros_like(r[1]), _dot3(r[1], g)))


def rmsnorm(x, g):
    return x * lax.rsqrt(jnp.mean(x * x, axis=-1, keepdims=True) + EPS) * g


def gelu(x):
    return 0.5 * x * (1.0 + lax.erf(x * 0.7071067811865476))


def sigmoid(x):
    return 1.0 / (1.0 + jnp.exp(-x))


def log_sigmoid(x):
    return jnp.minimum(x, 0.0) - jnp.log(1.0 + jnp.exp(-jnp.abs(x)))


def gmlp_heads(params, pieces):
    u = [[gelu(p[0]) for p in ch] for ch in pieces]
    v = [[gelu(p[1]) for p in ch] for ch in pieces]
    mu = [[jnp.mean(x, axis=-1, keepdims=True) for x in ch] for ch in v]
    var = [[jnp.mean(jnp.square(x - m), axis=-1, keepdims=True) for x, m in zip(cv, cm)] for cv, cm in zip(v, mu)]
    vn = [[(x - m) * lax.rsqrt(s + EPS) * pr[2] + pr[3] for x, m, s, pr in zip(cv, cm, cs, params)]
          for cv, cm, cs in zip(v, mu, var)]
    mix = [[mm(pr[0], x) + pr[1] for x, pr in zip(ch, params)] for ch in vn]
    return [[a * b for a, b in zip(cu, cx)] for cu, cx in zip(u, mix)]


def outb_head(o, pg, g):
    return rmsnorm(o, g) * (pg * sigmoid(pg))


def ffn_act(zg, zv):
    return zg * sigmoid(zg) * zv


def _tri(reverse):
    r = lax.broadcasted_iota(jnp.int32, (CH, CH), 0)
    c = lax.broadcasted_iota(jnp.int32, (CH, CH), 1)
    if reverse:
        cm, sm = c >= r, c > r
    else:
        cm, sm = c <= r, c <= r
    one = jnp.ones((), BF16)
    zero = jnp.zeros((), BF16)
    return jnp.where(cm, one, zero), jnp.where(cm.T, one, zero), sm


def gla_pair(consts, wg, bg, st0, st1, *chunks):
    m, mt, smask, lm0, lm1 = consts
    ch = [chunks[5 * i:5 * i + 5] for i in range(len(chunks) // 5)]
    la = [log_sigmoid(mm(c[0], wg) + bg) * (1.0 / 16.0) for c in ch]
    cum = [cumdot(m, mt, x) for x in la]
    tot = [jnp.sum(x, axis=0, keepdims=True) for x in la]
    q_dec = [(c[1] * 0.125) * jnp.exp(cm) for c, cm in zip(ch, cum)]
    k_inv = [c[2] * jnp.exp(-cm) for c, cm in zip(ch, cum)]
    k_end = [c[2] * jnp.exp(t - cm) for c, t, cm in zip(ch, tot, cum)]
    s = [[jnp.where(smask, mm_nt(qd * lm, ki), 0.0) for lm in (lm0, lm1)] for qd, ki in zip(q_dec, k_inv)]
    o_in = [[mm(si[h], c[3 + h]) for h in (0, 1)] for si, c in zip(s, ch)]
    ds = [[mm_tn(c[3 + h], ke * lm) for h, lm in ((0, lm0), (1, lm1))] for c, ke in zip(ch, k_end)]
    sts = [(st0, st1)]
    for t, d in zip(tot, ds):
        dec = jnp.exp(t)
        sts.append((sts[-1][0] * dec + d[0], sts[-1][1] * dec + d[1]))
    outs = []
    for qd, oi, st in zip(q_dec, o_in, sts):
        outs += [oi[0] + mm_nt(qd, st[0]), oi[1] + mm_nt(qd, st[1])]
    return (*outs, sts[-1][0], sts[-1][1])


def _lane_masks():
    lane = lax.broadcasted_iota(jnp.int32, (1, 128), 1)
    return (lane < 64).astype(F32), (lane >= 64).astype(F32)


def _cparams(n_axes=1):
    return pltpu.CompilerParams(dimension_semantics=("arbitrary",) * n_axes, vmem_limit_bytes=VMEM_LIMIT)


def _full(a):
    nd = a.ndim
    return pl.BlockSpec(a.shape, lambda *_: (0,) * nd)


def _rows(tm, w, cb=0, rev_n=None):
    if rev_n is None:
        return pl.BlockSpec((tm, w), lambda i: (i, cb))
    return pl.BlockSpec((tm, w), lambda i: (rev_n - 1 - i, cb))


def _call(body, name, grid, in_specs, out_specs, out_shape, scratch=(), n_axes=1):
    return pl.pallas_call(body, name=name, grid=grid, in_specs=in_specs, out_specs=out_specs, out_shape=out_shape,
                          scratch_shapes=list(scratch), compiler_params=_cparams(n_axes))


def _sds(shape, dt=F32):
    return jax.ShapeDtypeStruct(shape, dt)


def norm_matmul(x, g, w, tm, name, ydt=F32):
    T, n = x.shape[0], w.shape[1]

    def body(x_ref, g_ref, w_ref, y_ref, h_ref):
        hb = rmsnorm(x_ref[...], g_ref[...]).astype(MX)
        h_ref[...] = hb
        y_ref[...] = jnp.dot(hb, w_ref[...], preferred_element_type=F32).astype(ydt)

    return _call(body, name, (T // tm,), [_rows(tm, D), _full(g), _full(w)],
                 [_rows(tm, n), _rows(tm, D)], [_sds((T, n), ydt), _sds((T, D), MX)])(x, g, w)


CPB = 8


def _chunk(c):
    return slice(c * CH, (c + 1) * CH)


def gmlp_fwd(p, ws, bs, lg, lb):
    T = p.shape[0]
    tm = CPB * CH

    def body(pa_ref, ws_ref, bs_ref, lg_ref, lb_ref, o_ref):
        params = [(ws_ref[h], bs_ref[h], lg_ref[h], lb_ref[h]) for h in range(4)]
        pieces = [[(pa_ref[_chunk(c), h * 128:(h + 1) * 128], pa_ref[_chunk(c), 512 + h * 128:512 + (h + 1) * 128])
                   for h in range(4)] for c in range(CPB)]
        out = gmlp_heads(params, pieces)
        for c in range(CPB):
            for h in range(4):
                o_ref[_chunk(c), h * 128:(h + 1) * 128] = out[c][h].astype(MX)

    return _call(body, "gmlp_fwd", (T // tm,), [_rows(tm, 1024), _full(ws), _full(bs), _full(lg), _full(lb)],
                 _rows(tm, 512), _sds((T, 512), MX))(p, ws, bs, lg, lb)


def _gla_in_specs(tm, n, rev):
    r = n if rev else None
    return [_rows(tm, 256, 4, r), _rows(tm, 256, 5, r), _rows(tm, 512, 3, r), _rows(tm, 128, 20, r)]


def gla_fwd(p, wg, bg, reverse):
    T = p.shape[0]
    tm = CPB * CH
    n = T // tm
    rev = n if reverse else None

    def body(q_ref, k_ref, v_ref, r_ref, wg_ref, bg_ref, o_ref, ss_ref, st_ref):
        @pl.when(pl.program_id(0) == 0)
        def _():
            st_ref[...] = jnp.zeros_like(st_ref)

        consts = _tri(reverse) + _lane_masks()
        order = list(reversed(range(CPB))) if reverse else list(range(CPB))
        ss_ref[0] = st_ref[...]
        for j in range(2):
            sl = slice(j * 128, (j + 1) * 128)
            v0s, v1s = slice(256 * j, 256 * j + 128), slice(256 * j + 128, 256 * j + 256)
            chunks = []
            for c in order:
                rows = _chunk(c)
                chunks += [r_ref[rows, :], q_ref[rows, sl], k_ref[rows, sl], v_ref[rows, v0s], v_ref[rows, v1s]]
            res = gla_pair(consts, wg_ref[:, sl], bg_ref[:, sl], st_ref[2 * j], st_ref[2 * j + 1], *chunks)
            for i, c in enumerate(order):
                o_ref[_chunk(c), v0s] = res[2 * i]
                o_ref[_chunk(c), v1s] = res[2 * i + 1]
            st_ref[2 * j] = res[-2]
            st_ref[2 * j + 1] = res[-1]

    ss_spec = pl.BlockSpec((1, 4, 128, 128), (lambda i: (n - 1 - i, 0, 0, 0)) if reverse else (lambda i: (i, 0, 0, 0)))
    return _call(body, "gla_fwd_r" if reverse else "gla_fwd_f", (n,),
                 _gla_in_specs(tm, n, reverse) + [_full(wg), _full(bg)],
                 [_rows(tm, 512, 0, rev), ss_spec], [_sds((T, 512)), _sds((n, 4, 128, 128))],
                 scratch=[pltpu.VMEM((4, 128, 128), F32)])(p, p, p, p, wg, bg)


def mix_out(x, of, ob, p, outa, gg, w_out, tm):
    T = x.shape[0]

    def body(x_ref, of_ref, ob_ref, pg_ref, oa_ref, gg_ref, w_ref, x1_ref, mx_ref):
        mx_ref[:, 0:512] = oa_ref[...]
        for h in range(4):
            sl = slice(h * 128, (h + 1) * 128)
            mx_ref[:, 512 + h * 128:512 + (h + 1) * 128] = outb_head(
                of_ref[:, sl] + ob_ref[:, sl], pg_ref[:, sl], gg_ref[h]).astype(MX)
        x1_ref[...] = x_ref[...] + jnp.dot(mx_ref[...], w_ref[...], preferred_element_type=F32)

    return _call(body, "mix_out", (T // tm,),
                 [_rows(tm, D), _rows(tm, 512), _rows(tm, 512), _rows(tm, 512, 4), _rows(tm, 512), _full(gg), _full(w_out)],
                 [_rows(tm, D), _rows(tm, 1024)], [_sds((T, D)), _sds((T, 1024), MX)])(x, of, ob, p, outa, gg, w_out)


HALO = 16


def _halo_specs(T, tm, w):
    nb = T // HALO
    r = tm // HALO
    return [pl.BlockSpec((tm, w), lambda i: (i, 0)),
            pl.BlockSpec((HALO, w), lambda i: (jnp.maximum(i * r - 1, 0), 0)),
            pl.BlockSpec((HALO, w), lambda i: (jnp.minimum((i + 1) * r, nb - 1), 0))]


def ffn_up_conv(x1, g, w_up, cw, cb, tm):
    T = x1.shape[0]
    ns = T // tm
    cwid = 256

    def body(x_ref, g_ref, w_ref, cw_ref, cb_ref, zu_ref, h_ref, z_ref, a_ref, prev_ref, tail_ref):
        i = pl.program_id(0)

        @pl.when(i == 0)
        def _():
            prev_ref[...] = jnp.zeros_like(prev_ref)
            tail_ref[...] = jnp.zeros_like(tail_ref)

        hb = rmsnorm(x_ref[...], g_ref[...]).astype(MX)
        h_ref[...] = hb
        row = lax.broadcasted_iota(jnp.int32, (tm, 1), 0)
        for c0 in range(0, DFF, cwid):
            z2 = []
            for cs in (slice(c0, c0 + cwid), slice(DFF + c0, DFF + c0 + cwid)):
                zub = jnp.dot(hb, w_ref[:, cs], preferred_element_type=F32).astype(MX)
                zu_ref[:, cs] = zub
                prev = prev_ref[:, cs].astype(F32)
                pr = tail_ref[HALO - 1:HALO, cs].astype(F32)
                nx = jnp.where(i < ns, zub[0:1, :].astype(F32), 0.0)
                dn = jnp.where(row == 0, pr, pltpu.roll(prev, 1, 0))
                up = jnp.where(row == tm - 1, nx, pltpu.roll(prev, tm - 1, 0))
                z = cb_ref[:, cs] + dn * cw_ref[0:1, cs] + prev * cw_ref[1:2, cs] + up * cw_ref[2:3, cs]
                z_ref[:, cs] = z.astype(MX)
                tail_ref[:, cs] = prev_ref[tm - HALO:tm, cs]
                prev_ref[:, cs] = zub
                z2.append(z)
            a_ref[:, c0:c0 + cwid] = ffn_act(z2[0], z2[1]).astype(MX)

    cur = lambda w: pl.BlockSpec((tm, w), lambda i: (jnp.minimum(i, ns - 1), 0))
    late = lambda w: pl.BlockSpec((tm, w), lambda i: (jnp.maximum(i - 1, 0), 0))
    return _call(body, "ffn_up", (ns + 1,), [cur(D), _full(g), _full(w_up), _full(cw), _full(cb)],
                 [cur(NUP), cur(D), late(NUP), late(DFF)],
                 [_sds((T, NUP), MX), _sds((T, D), MX), _sds((T, NUP), MX), _sds((T, DFF), MX)],
                 scratch=[pltpu.VMEM((tm, NUP), MX), pltpu.VMEM((HALO, NUP), MX)])(x1, g, w_up, cw, cb)


def matmul_res(a, w, res, tm, name):
    T, k = a.shape
    n = w.shape[1]

    def body(a_ref, w_ref, r_ref, o_ref):
        o_ref[...] = r_ref[...] + jnp.dot(a_ref[...], w_ref[...], preferred_element_type=F32)

    return _call(body, name, (T // tm,), [_rows(tm, k), _full(w), _rows(tm, n)], _rows(tm, n), _sds((T, n)))(a, w, res)


def loss_head(x, g, tgt, tm):
    T = x.shape[0]

    def body(x_ref, g_ref, t_ref, l_ref, dx_ref, dg_ref):
        @pl.when(pl.program_id(0) == 0)
        def _():
            l_ref[...] = jnp.zeros_like(l_ref)
            dg_ref[...] = jnp.zeros_like(dg_ref)

        y, vjp = jax.vjp(rmsnorm, x_ref[...], g_ref[...])
        err = y - t_ref[...]
        l_ref[...] += jnp.sum(err * err, axis=0, keepdims=True)
        dx, dg = vjp(err * (1.0 / D))
        dx_ref[...] = dx
        dg_ref[...] += dg

    return _call(body, "loss_head", (T // tm,), [_rows(tm, D), _full(g), _rows(tm, D)],
                 [_full(g), _rows(tm, D), _full(g)], [_sds((1, D)), _sds((T, D)), _sds((1, D))])(x, g, tgt)


def ffn_down_bwd(dx2, z, w_down, tm):
    T = dx2.shape[0]

    def body(dx_ref, z_ref, w_ref, dz_ref):
        da = _dg(dx_ref[...], w_ref[...], 1, 1)
        zg, zv = z_ref[:, :DFF].astype(F32), z_ref[:, DFF:].astype(F32)
        s = sigmoid(zg)
        sz = zg * s
        dz_ref[:, :DFF] = (da * zv * (s + sz * (1.0 - s))).astype(MX)
        dz_ref[:, DFF:] = (da * sz).astype(MX)

    return _call(body, "ffn_down_bwd", (T // tm,), [_rows(tm, D), _rows(tm, NUP), _full(w_down)],
                 _rows(tm, NUP), _sds((T, NUP), MX))(dx2, z, w_down)


def ffn_up_bwd(dz, zu, cw, w_up, x1, g, dres, tm):
    T = dz.shape[0]
    ns = T // tm
    cwid = 512

    def body(dz_ref, dp_ref, dn_ref, zu_ref, cw_ref, w_ref, x_ref, g_ref, dr_ref,
             dzu_ref, dx_ref, dg_ref, dcw_ref, dcb_ref):
        i = pl.program_id(0)

        @pl.when(i == 0)
        def _():
            for r in (dg_ref, dcw_ref, dcb_ref):
                r[...] = jnp.zeros_like(r)

        row = lax.broadcasted_iota(jnp.int32, (tm, 1), 0)
        dh = jnp.zeros((tm, D), F32)
        for c0 in range(0, NUP, cwid):
            cs = slice(c0, c0 + cwid)
            dz = dz_ref[:, cs].astype(F32)
            zu = zu_ref[:, cs].astype(F32)
            pr = jnp.where(i > 0, dp_ref[HALO - 1:HALO, cs].astype(F32), 0.0)
            nx = jnp.where(i < ns - 1, dn_ref[0:1, cs].astype(F32), 0.0)
            ddn = jnp.where(row == 0, pr, pltpu.roll(dz, 1, 0))
            dup = jnp.where(row == tm - 1, nx, pltpu.roll(dz, tm - 1, 0))
            dzu = (dup * cw_ref[0:1, cs] + dz * cw_ref[1:2, cs] + ddn * cw_ref[2:3, cs]).astype(MX)
            dzu_ref[:, cs] = dzu
            dcw_ref[0:1, cs] += jnp.sum(zu * dup, axis=0, keepdims=True)
            dcw_ref[1:2, cs] += jnp.sum(zu * dz, axis=0, keepdims=True)
            dcw_ref[2:3, cs] += jnp.sum(zu * ddn, axis=0, keepdims=True)
            dcb_ref[:, cs] += jnp.sum(dz, axis=0, keepdims=True)
            dh = dh + _dg(dzu, w_ref[:, cs], 1, 1)
        _, vjp = jax.vjp(rmsnorm, x_ref[...], g_ref[...])
        dx, dg = vjp(dh)
        dx_ref[...] = dr_ref[...] + dx
        dg_ref[...] += dg

    return _call(body, "ffn_up_bwd", (ns,),
                 _halo_specs(T, tm, NUP) + [_rows(tm, NUP), _full(cw), _full(w_up), _rows(tm, D), _full(g), _rows(tm, D)],
                 [_rows(tm, NUP), _rows(tm, D), _full(g), _full(cw), pl.BlockSpec((1, NUP), lambda i: (0, 0))],
                 [_sds((T, NUP), MX), _sds((T, D)), _sds((1, D)), _sds((3, NUP)), _sds((1, NUP))])(
                     dz, dz, dz, zu, cw, w_up, x1, g, dres)


def nt_normbwd(dys, w, x, g, dres, tm, name):
    T = x.shape[0]
    n = len(dys)
    offs = [sum(d.shape[1] for d in dys[:i]) for i in range(n + 1)]

    def body(*refs):
        dy_refs, (w_ref, x_ref, g_ref, dr_ref, dx_ref, dg_ref) = refs[:n], refs[n:]

        @pl.when(pl.program_id(0) == 0)
        def _():
            dg_ref[...] = jnp.zeros_like(dg_ref)

        dh = _dg(dy_refs[0][...], w_ref[:, offs[0]:offs[1]], 1, 1)
        for i in range(1, n):
            dh = dh + _dg(dy_refs[i][...], w_ref[:, offs[i]:offs[i + 1]], 1, 1)
        _, vjp = jax.vjp(rmsnorm, x_ref[...], g_ref[...])
        dx, dg = vjp(dh)
        dx_ref[...] = dr_ref[...] + dx
        dg_ref[...] += dg

    return _call(body, name, (T // tm,),
                 [_rows(tm, d.shape[1]) for d in dys] + [_full(w), _rows(tm, D), _full(g), _rows(tm, D)],
                 [_rows(tm, D), _full(g)], [_sds((T, D)), _sds((1, D))])(*dys, w, x, g, dres)


def matmul_tn(a, b, tt, tn, name):
    T, k = a.shape
    n = b.shape[1]
    last = T // tt - 1

    def body(a_ref, b_ref, o_ref, acc_ref):
        @pl.when(pl.program_id(1) == 0)
        def _():
            acc_ref[...] = jnp.zeros_like(acc_ref)

        acc_ref[...] += _dg(a_ref[...], b_ref[...], 0, 0)

        @pl.when(pl.program_id(1) == last)
        def _():
            o_ref[...] = acc_ref[...].astype(MX)

    return _call(body, name, (n // tn, T // tt),
                 [pl.BlockSpec((tt, k), lambda j, i: (i, 0)), pl.BlockSpec((tt, tn), lambda j, i: (i, j))],
                 pl.BlockSpec((k, tn), lambda j, i: (0, j)), _sds((k, n), MX), scratch=[pltpu.VMEM((k, tn), F32)],
                 n_axes=2)(a, b)


def mix_out_bwd(dx1, w_out, of, ob, p, gg, tm):
    T = dx1.shape[0]

    def body(dx_ref, w_ref, of_ref, ob_ref, pg_ref, gg_ref, da_ref, do_ref, dpg_ref, dgg_ref):
        @pl.when(pl.program_id(0) == 0)
        def _():
            dgg_ref[...] = jnp.zeros_like(dgg_ref)

        dxb = dx_ref[...].astype(MX)
        da_ref[...] = _dg(dxb, w_ref[0:512, :], 1, 1)
        for h in range(4):
            sl = slice(h * 128, (h + 1) * 128)
            dm = _dg(dxb, w_ref[512 + h * 128:512 + (h + 1) * 128, :], 1, 1)
            _, vjp = jax.vjp(outb_head, of_ref[:, sl] + ob_ref[:, sl], pg_ref[:, sl], gg_ref[h])
            do, dpg, dg = vjp(dm)
            do_ref[:, sl] = do
            dpg_ref[:, sl] = dpg
            dgg_ref[h] += dg

    return _call(body, "mix_out_bwd", (T // tm,),
                 [_rows(tm, D), _full(w_out), _rows(tm, 512), _rows(tm, 512), _rows(tm, 512, 4), _full(gg)],
                 [_rows(tm, 512), _rows(tm, 512), _rows(tm, 512), _full(gg)],
                 [_sds((T, 512)), _sds((T, 512)), _sds((T, 512)), _sds(gg.shape)])(dx1, w_out, of, ob, p, gg)


def gla_bwd(p, wg, bg, ss, do, reverse, merge=None):
    T = p.shape[0]
    tm = CPB * CH
    n = T // tm
    rev = not reverse
    rn = n if rev else None

    def body(*refs):
        q_ref, k_ref, v_ref, r_ref, wg_ref, bg_ref, ss_ref, do_ref = refs[:8]
        if merge is None:
            dq_ref, dk_ref, dv_ref, dr_ref, dwg_ref, dbg_ref, dst_ref = refs[8:]
        else:
            mq_ref, mk_ref, mv_ref, mr_ref, mg_ref, out_ref, dwg_ref, dbg_ref, dst_ref, drs_ref = refs[8:]
            out_ref[:, 1024:1536] = mg_ref[...].astype(MX)

        @pl.when(pl.program_id(0) == 0)
        def _():
            dst_ref[...] = jnp.zeros_like(dst_ref)
            dwg_ref[...] = jnp.zeros_like(dwg_ref)
            dbg_ref[...] = jnp.zeros_like(dbg_ref)

        consts = _tri(reverse) + _lane_masks()
        order = list(reversed(range(CPB))) if reverse else list(range(CPB))
        for j in range(2):
            sl = slice(j * 128, (j + 1) * 128)
            v0s, v1s = slice(256 * j, 256 * j + 128), slice(256 * j + 128, 256 * j + 256)
            chunks, dout = [], []
            for c in order:
                rows = _chunk(c)
                chunks += [r_ref[rows, :], q_ref[rows, sl], k_ref[rows, sl], v_ref[rows, v0s], v_ref[rows, v1s]]
                dout += [do_ref[rows, v0s], do_ref[rows, v1s]]
            _, vjp = jax.vjp(functools.partial(gla_pair, consts), wg_ref[:, sl], bg_ref[:, sl],
                             ss_ref[0, 2 * j], ss_ref[0, 2 * j + 1], *chunks)
            g = vjp((*dout, dst_ref[2 * j], dst_ref[2 * j + 1]))
            dwg_ref[:, sl] += g[0]
            dbg_ref[:, sl] += g[1]
            dst_ref[2 * j] = g[2]
            dst_ref[2 * j + 1] = g[3]
            for i, c in enumerate(order):
                rows = _chunk(c)
                dr, dq, dk, dv0, dv1 = g[4 + 5 * i:9 + 5 * i]
                if merge is None:
                    if j == 0:
                        dr_ref[rows, :] = dr
                    else:
                        dr_ref[rows, :] += dr
                    dq_ref[rows, sl] = dq
                    dk_ref[rows, sl] = dk
                    dv_ref[rows, v0s] = dv0
                    dv_ref[rows, v1s] = dv1
                else:
                    if j == 0:
                        drs_ref[rows, :] = mr_ref[rows, :] + dr
                    else:
                        out_ref[rows, 1536:1664] = (drs_ref[rows, :] + dr).astype(MX)
                    out_ref[rows, sl] = (mq_ref[rows, sl] + dq).astype(MX)
                    out_ref[rows, 256 + 128 * j:384 + 128 * j] = (mk_ref[rows, sl] + dk).astype(MX)
                    out_ref[rows, 512 + 256 * j:640 + 256 * j] = (mv_ref[rows, v0s] + dv0).astype(MX)
                    out_ref[rows, 640 + 256 * j:768 + 256 * j] = (mv_ref[rows, v1s] + dv1).astype(MX)

    ss_spec = pl.BlockSpec((1, 4, 128, 128), (lambda i: (n - 1 - i, 0, 0, 0)) if rev else (lambda i: (i, 0, 0, 0)))
    ins = [p, p, p, p, wg, bg, ss, do]
    in_specs = _gla_in_specs(tm, n, rev) + [_full(wg), _full(bg), ss_spec, _rows(tm, 512, 0, rn)]
    scratch = [pltpu.VMEM((4, 128, 128), F32)]
    if merge is None:
        out_specs = [_rows(tm, 256, 0, rn), _rows(tm, 256, 0, rn), _rows(tm, 512, 0, rn), _rows(tm, 128, 0, rn)]
        out_shape = [_sds((T, 256)), _sds((T, 256)), _sds((T, 512)), _sds((T, 128))]
    else:
        ins += list(merge)
        in_specs += [_rows(tm, a.shape[1], 0, rn) for a in merge]
        out_specs, out_shape = [_rows(tm, 1664, 0, rn)], [_sds((T, 1664), MX)]
        scratch.append(pltpu.VMEM((tm, 128), F32))
    return _call(body, "gla_bwd_r" if reverse else "gla_bwd_f", (n,), in_specs, out_specs + [_full(wg), _full(bg)],
                 out_shape + [_sds(wg.shape), _sds(bg.shape)], scratch=scratch)(*ins)


def gmlp_bwd(p, douta, ws, bs, lg, lb):
    T = p.shape[0]
    cpb = 4
    tm = cpb * CH

    def body(pa_ref, do_ref, ws_ref, bs_ref, lg_ref, lb_ref, dpa_ref, dws_ref, dbs_ref, dlg_ref, dlb_ref):
        @pl.when(pl.program_id(0) == 0)
        def _():
            for r in (dws_ref, dbs_ref, dlg_ref, dlb_ref):
                r[...] = jnp.zeros_like(r)

        us = [slice(h * 128, (h + 1) * 128) for h in range(4)]
        vs = [slice(512 + h * 128, 512 + (h + 1) * 128) for h in range(4)]
        params = [(ws_ref[h], bs_ref[h], lg_ref[h], lb_ref[h]) for h in range(4)]
        pieces = [[(pa_ref[_chunk(c), us[h]], pa_ref[_chunk(c), vs[h]]) for h in range(4)] for c in range(cpb)]
        _, vjp = jax.vjp(gmlp_heads, params, pieces)
        dparams, dpieces = vjp([[do_ref[_chunk(c), us[h]] for h in range(4)] for c in range(cpb)])
        for h in range(4):
            for r, a in zip((dws_ref, dbs_ref, dlg_ref, dlb_ref), dparams[h]):
                r[h] += a
            for c in range(cpb):
                dpa_ref[_chunk(c), us[h]] = dpieces[c][h][0].astype(MX)
                dpa_ref[_chunk(c), vs[h]] = dpieces[c][h][1].astype(MX)

    return _call(body, "gmlp_bwd", (T // tm,),
                 [_rows(tm, 1024), _rows(tm, 512), _full(ws), _full(bs), _full(lg), _full(lb)],
                 [_rows(tm, 1024), _full(ws), _full(bs), _full(lg), _full(lb)],
                 [_sds((T, 1024), MX), _sds(ws.shape), _sds(bs.shape), _sds(lg.shape), _sds(lb.shape)])(p, douta, ws, bs, lg, lb)


def _gate_pad(w, row0):
    return jnp.zeros((128, 256), F32).at[row0:row0 + 16].set(w)


def local_step(x, tgt, W, get_big, emit, tm=256, tmm=512):
    saved = []
    for l in range(NL):
        s = {"x": x}
        s.update(get_big(l, "in", x))
        p, s["h"] = norm_matmul(x, W["g_mix"][l][None], s["w_in"], tmm, "mix_in")
        s["p"] = p
        ws, bs = W["w_s"][l], W["b_s"][l][:, :, None]
        lg, lb = W["ln_g"][l][:, None, :], W["ln_b"][l][:, None, :]
        outa = gmlp_fwd(p, ws, bs, lg, lb)
        wgf, wgb = _gate_pad(W["w_gate_f"][l], 0), _gate_pad(W["w_gate_b"][l], 16)
        bgf, bgb = W["b_gate_f"][l][None], W["b_gate_b"][l][None]
        s["of"], s["ssf"] = gla_fwd(p, wgf, bgf, False)
        s["ob"], s["ssb"] = gla_fwd(p, wgb, bgb, True)
        s.update(get_big(l, "rest", s["ob"]))
        gg = W["g_gla"][l][:, None, :]
        x1, s["mixed"] = mix_out(x, s["of"], s["ob"], p, outa, gg, s["w_out"], tmm)
        s["x1"] = x1
        s["zu"], s["h2"], s["z"], s["a"] = ffn_up_conv(x1, W["g_ffn"][l][None], s["w_up"], W["conv_w"][l],
                                                       W["conv_b"][l][None], tm)
        x = matmul_res(s["a"], s["w_down"], x1, tmm, "ffn_down")
        saved.append(s)

    lsum, dx, dgf = loss_head(x, W["g_final"][None], tgt, tmm)
    G = {k: [None] * NL for k in _SMALL if k != "g_final"}
    tok = jnp.zeros((1, 1), F32)
    for l in reversed(range(NL)):
        s = saved[l]
        g_down = matmul_tn(s["a"], dx, min(1024, tmm * 2), 512, "dw_down")
        dz = ffn_down_bwd(dx, s["z"], s["w_down"], tm)
        dzu, dx1, dg, G["conv_w"][l], dcb = ffn_up_bwd(dz, s["zu"], W["conv_w"][l] + tok, s["w_up"], s["x1"],
                                                       W["g_ffn"][l][None], dx, tm)
        G["conv_b"][l], G["g_ffn"][l] = dcb[0], dg[0]
        g_up = matmul_tn(s["h2"], dzu, min(1024, tmm * 2), 1408, "dw_up")
        tok = emit(l, "A", {"w_down": g_down, "w_up": g_up})
        g_out = matmul_tn(s["mixed"], dx1, min(1024, tmm * 2), 1024, "dw_out")
        gg = W["g_gla"][l][:, None, :] + tok
        douta, do, dpg, dgg = mix_out_bwd(dx1, s["w_out"], s["of"], s["ob"], s["p"], gg, tmm)
        G["g_gla"][l] = dgg[:, 0, :]
        wgf, wgb = _gate_pad(W["w_gate_f"][l], 0), _gate_pad(W["w_gate_b"][l], 16)
        bgf, bgb = W["b_gate_f"][l][None], W["b_gate_b"][l][None]
        dqf, dkf, dvf, drf, dwgf, dbgf = gla_bwd(s["p"], wgf, bgf, s["ssf"], do, False)
        dpb, dwgb, dbgb = gla_bwd(s["p"], wgb, bgb, s["ssb"], do, True, merge=(dqf, dkf, dvf, drf, dpg))
        G["w_gate_f"][l], G["b_gate_f"][l] = dwgf[0:16], dbgf[0]
        G["w_gate_b"][l], G["b_gate_b"][l] = dwgb[16:32], dbgb[0]
        ws, bs = W["w_s"][l], W["b_s"][l][:, :, None]
        lg, lb = W["ln_g"][l][:, None, :], W["ln_b"][l][:, None, :]
        dpa, G["w_s"][l], dbs, dlg, dlb = gmlp_bwd(s["p"], douta, ws, bs, lg, lb)
        G["b_s"][l], G["ln_g"][l], G["ln_b"][l] = dbs[:, :, 0], dlg[:, 0, :], dlb[:, 0, :]
        tt = min(1024, tmm * 2)
        g_in = jnp.concatenate([matmul_tn(s["h"], dpa, tt, 1024, "dw_in_a"),
                                matmul_tn(s["h"], dpb, tt, 1664, "dw_in_b")], axis=1)
        tok = emit(l, "B", {"w_out": g_out, "w_in": g_in})
        dx, dg = nt_normbwd([dpa, dpb], s["w_in"], s["x"], W["g_mix"][l][None] + tok, dx1, tmm, "mix_in_bwd")
        G["g_mix"][l] = dg[0]
    G = {k: jnp.stack(v) for k, v in G.items()}
    G["g_final"] = dgf[0]
    return lsum, dx, G


def _rows3(tr, c):
    return pl.BlockSpec((None, tr, c), lambda l, i: (l, i, 0))


def cast_bf16(a, tr):
    nl, r, c = a.shape

    def body(a_ref, o_ref):
        o_ref[...] = a_ref[...].astype(BF16)

    return _call(body, "cast_bf16", (nl, r // tr), [_rows3(tr, c)], _rows3(tr, c), _sds(a.shape, BF16), n_axes=2)(a)


def sum_parts(land, grad, k, chipvec, tr):
    _, rr, cc = land.shape
    nb = rr // tr

    def body(c_ref, l_ref, g_ref, o_ref):
        mine = g_ref[...].astype(F32)
        acc = None
        for j in range(4):
            part = jnp.where(c_ref[0] == j, mine, l_ref[j].astype(F32))
            acc = part if acc is None else acc + part
        o_ref[...] = acc

    gs = pltpu.PrefetchScalarGridSpec(
        num_scalar_prefetch=1, grid=(nb,),
        in_specs=[pl.BlockSpec((4, tr, cc), lambda i, c: (0, i, 0)), _part_spec(k, tr, nb)],
        out_specs=pl.BlockSpec((tr, cc), lambda i, c: (i, 0)))
    return pl.pallas_call(body, name="sum_parts", grid_spec=gs, out_shape=_sds((rr, cc)),
                          compiler_params=_cparams(1))(chipvec, land, grad)


def adamw(w, ga, gb, m, v, tr):
    nl, r, c = w.shape

    def body(w_ref, ga_ref, gb_ref, m_ref, v_ref, g_ref, d_ref, nm_ref, nv_ref):
        gr = ga_ref[...] + gb_ref[...]
        g_ref[...] = gr
        nm = ADAM_B1 * m_ref[...] + (1.0 - ADAM_B1) * gr
        nv = ADAM_B2 * v_ref[...] + (1.0 - ADAM_B2) * jnp.square(gr)
        m_hat = nm / (1.0 - ADAM_B1 ** ADAM_STEP)
        v_hat = nv / (1.0 - ADAM_B2 ** ADAM_STEP)
        d_ref[...] = -ADAM_LR * (m_hat / (jnp.sqrt(v_hat) + ADAM_EPS) + ADAM_WD * w_ref[...])
        nm_ref[...] = nm
        nv_ref[...] = nv

    sp = _rows3(tr, c)
    return _call(body, "adamw", (nl, r // tr), [sp] * 5, [sp] * 4, [_sds(w.shape)] * 4, n_axes=2)(w, ga, gb, m, v)


MESH = pl.DeviceIdType.MESH
ANY = pl.BlockSpec(memory_space=pl.ANY)
N_BIG = 4


def _pos():
    return lax.axis_index("x"), lax.axis_index("y"), lax.axis_index("c")


def _other_chips(x, y):
    return [(1 - x, y), (x, 1 - y), (1 - x, 1 - y)]


def _slab(k, ref, j):
    if k == 0:
        return ref.at[j]
    if k == 1:
        return ref.at[pl.ds(256 * j, 256), :]
    if k == 2:
        return ref.at[:, pl.ds(1408 * j, 1408)]
    return ref.at[pl.ds(704 * j, 704), :]


_LAYER_FULL = [(4, 1024, 648), (1024, 1024), (1024, NUP), (DFF, 1024)]
_LAYER_SHARD = [(1024, 648), (256, 1024), (1024, 1408), (704, 1024)]
_SHARD_SHAPES = [(NL,) + s for s in _LAYER_SHARD]

HBM = pl.BlockSpec(memory_space=pltpu.HBM)
SEM = pl.BlockSpec(memory_space=pltpu.SEMAPHORE)
VM = pl.BlockSpec(memory_space=pltpu.VMEM)
EFFECT = pltpu.SideEffectType.DATAFLOW_SIDE_EFFECTING
_GW_GROUPS = [[(0, 0)], [(0, 1), (0, 2), (0, 3)]] + [[(l, k) for k in range(N_BIG)] for l in range(1, NL)]
_GW_ORDER = [lk for g in _GW_GROUPS for lk in g]


def _hbm(a):
    return pltpu.with_memory_space_constraint(a, pltpu.HBM)


def _hbm_like(a):
    return pltpu.HBM(a.shape, a.dtype)


def _part_spec(k, tr, nb):
    cc = _LAYER_SHARD[k][1]
    if k == 0:
        return pl.BlockSpec((None, tr, cc), lambda i, c: (c[0], i, 0))
    if k == 2:
        return pl.BlockSpec((tr, cc), lambda i, c: (i, c[0]))
    return pl.BlockSpec((tr, cc), lambda i, c: (c[0] * nb + i, 0))


def place_own(shard, landing, l, k, chipvec, tr):
    rr, cc = _LAYER_SHARD[k]
    nb = rr // tr

    def body(c_ref, s_ref, l_ref, o_ref):
        o_ref[...] = s_ref[...]

    gs = pltpu.PrefetchScalarGridSpec(
        num_scalar_prefetch=1, grid=(nb,),
        in_specs=[pl.BlockSpec((None, tr, cc), lambda i, c: (l, i, 0)), ANY], out_specs=_part_spec(k, tr, nb))
    return pl.pallas_call(body, name="place_own", grid_spec=gs, out_shape=_sds(landing.shape, landing.dtype),
                          input_output_aliases={2: 0}, compiler_params=_cparams(1))(chipvec, shard, landing)


def gw_start(shards, landings):
    n = len(_GW_ORDER)

    def body(*refs):
        S, Ld = refs[:N_BIG], refs[N_BIG:N_BIG + n]
        outs = refs[N_BIG + n:]
        send_sems, recv, token = outs[0], outs[1:1 + len(_GW_GROUPS)], outs[-1]
        x, y, c = _pos()
        me = 2 * x + y
        ci = 0
        for gi, grp in enumerate(_GW_GROUPS):
            for t, (l, k) in enumerate(grp):
                land = Ld[_GW_ORDER.index((l, k))]
                for j, (px, py) in enumerate(_other_chips(x, y)):
                    pltpu.make_async_remote_copy(
                        src_ref=S[k].at[l], dst_ref=_slab(k, land, me), send_sem=send_sems.at[ci],
                        recv_sem=recv[gi].at[3 * t + j], device_id=(px, py, c), device_id_type=MESH).start()
                    ci += 1
        token[...] = jnp.zeros_like(token)

    ins = list(shards) + list(landings)
    sems = [pltpu.SemaphoreType.DMA((3 * n,))] + [pltpu.SemaphoreType.DMA((3 * len(g),)) for g in _GW_GROUPS]
    outs = pl.pallas_call(
        body, name="gw_start", out_shape=sems + [_hbm_like(a) for a in ins] + [_sds((8, 128))],
        in_specs=[HBM] * len(ins), out_specs=[SEM] * len(sems) + [HBM] * len(ins) + [VM],
        input_output_aliases={i: len(sems) + i for i in range(len(ins))},
        compiler_params=pltpu.CompilerParams(has_side_effects=EFFECT))(*[_hbm(a) for a in ins])
    ns = len(sems)
    return outs[0], outs[1:ns], outs[ns:ns + N_BIG], outs[ns + N_BIG:ns + len(ins)], outs[-1]


def gw_wait(gi, landings, recv_sems, after, shards=None, send_sems=None):
    grp = _GW_GROUPS[gi]
    n = len(grp)
    last = shards is not None

    def body(*refs):
        Ld, rs = refs[:n], refs[n]
        x, y, c = _pos()
        for t, (l, k) in enumerate(grp):
            for j, (px, py) in enumerate(_other_chips(x, y)):
                region = _slab(k, Ld[t], 2 * px + py)
                pltpu.make_async_remote_copy(src_ref=region, dst_ref=region, send_sem=rs.at[3 * t + j],
                                             recv_sem=rs.at[3 * t + j], device_id=(px, py, c),
                                             device_id_type=MESH).wait_recv()
        if last:
            S, ss = refs[n + 2:n + 2 + N_BIG], refs[n + 2 + N_BIG]
            me = 2 * x + y
            for ci, (l, k) in enumerate(lk for lk in _GW_ORDER for _ in range(3)):
                pltpu.make_async_remote_copy(src_ref=S[k].at[l], dst_ref=_slab(k, Ld[k], me), send_sem=ss.at[ci],
                                             recv_sem=ss.at[ci], device_id=(x, y, c), device_id_type=MESH).wait_send()

    ins = list(landings) + [recv_sems, after]
    specs = [HBM] * n + [SEM, pl.BlockSpec(memory_space=pl.ANY)]
    outs = [_hbm_like(a) for a in landings]
    alias = {i: i for i in range(n)}
    if last:
        ins += list(shards) + [send_sems]
        specs += [HBM] * N_BIG + [SEM]
        outs += [_hbm_like(a) for a in shards]
        alias.update({n + 2 + i: n + i for i in range(N_BIG)})
    res = pl.pallas_call(body, name="gw_wait_%d" % gi, out_shape=outs, in_specs=specs, out_specs=[HBM] * len(outs),
                         input_output_aliases=alias,
                         compiler_params=pltpu.CompilerParams(has_side_effects=EFFECT))(*ins)
    return res[:n], (res[n:] if last else None)


def ga_start(tag, ks, grads, landings):
    n = len(ks)

    def body(*refs):
        G, Ld = refs[:n], refs[n:2 * n]
        send_sems, recv_sems, token = refs[2 * n], refs[2 * n + 1], refs[-1]
        x, y, c = _pos()
        me = 2 * x + y
        for t, k in enumerate(ks):
            for j, (px, py) in enumerate(_other_chips(x, y)):
                pltpu.make_async_remote_copy(
                    src_ref=_slab(k, G[t], 2 * px + py), dst_ref=Ld[t].at[me], send_sem=send_sems.at[3 * t + j],
                    recv_sem=recv_sems.at[3 * t + j], device_id=(px, py, c), device_id_type=MESH).start()
        token[...] = jnp.zeros_like(token)

    ins = list(grads) + list(landings)
    sems = [pltpu.SemaphoreType.DMA((3 * n,))] * 2
    outs = pl.pallas_call(
        body, name="ga_start_" + tag, out_shape=sems + [_hbm_like(a) for a in ins] + [_sds((8, 128))],
        in_specs=[HBM] * len(ins), out_specs=[SEM, SEM] + [HBM] * len(ins) + [VM],
        input_output_aliases={i: 2 + i for i in range(len(ins))},
        compiler_params=pltpu.CompilerParams(has_side_effects=EFFECT))(*[_hbm(a) for a in ins])
    return outs[0], outs[1], outs[2:2 + n], outs[2 + n:2 + 2 * n], outs[-1]


def ga_wait(tag, ks, send_sems, recv_sems, grads, landings, after):
    n = len(ks)

    def body(*refs):
        G, Ld, ss, rs = refs[:n], refs[n:2 * n], refs[2 * n], refs[2 * n + 1]
        x, y, c = _pos()
        me = 2 * x + y
        for t, k in enumerate(ks):
            for j, (px, py) in enumerate(_other_chips(x, y)):
                pj = 2 * px + py
                cp = pltpu.make_async_remote_copy(
                    src_ref=_slab(k, G[t], pj), dst_ref=Ld[t].at[pj], send_sem=ss.at[3 * t + j],
                    recv_sem=rs.at[3 * t + j], device_id=(px, py, c), device_id_type=MESH)
                cp.wait_send()
                cp.wait_recv()

    ins = list(grads) + list(landings) + [send_sems, recv_sems, after]
    res = pl.pallas_call(
        body, name="ga_wait_" + tag, out_shape=[_hbm_like(a) for a in list(grads) + list(landings)],
        in_specs=[HBM] * (2 * n) + [SEM, SEM, pl.BlockSpec(memory_space=pl.ANY)], out_specs=[HBM] * (2 * n),
        input_output_aliases={i: i for i in range(2 * n)},
        compiler_params=pltpu.CompilerParams(has_side_effects=EFFECT))(*ins)
    return res[:n], res[n:]


def swap_start(parts):
    n = len(parts)

    def body(*refs):
        Q, Ld, sems = refs[:n], refs[n:2 * n], refs[2 * n:4 * n]
        x, y, c = _pos()
        for k in range(n):
            pltpu.make_async_remote_copy(src_ref=Q[k], dst_ref=Ld[k], send_sem=sems[k].at[0], recv_sem=sems[n + k].at[0],
                                         device_id=(x, y, 1 - c), device_id_type=MESH).start()
        refs[-1][...] = jnp.zeros_like(refs[-1])

    ins = list(parts) + [lax.empty(p.shape, p.dtype) for p in parts]
    outs = pl.pallas_call(
        body, name="swap_start",
        out_shape=[pltpu.SemaphoreType.DMA((1,))] * (2 * n) + [_hbm_like(a) for a in ins] + [_sds((8, 128))],
        in_specs=[HBM] * (2 * n), out_specs=[SEM] * (2 * n) + [HBM] * (2 * n) + [VM],
        input_output_aliases={i: 2 * n + i for i in range(2 * n)},
        compiler_params=pltpu.CompilerParams(has_side_effects=EFFECT))(*[_hbm(a) for a in ins])
    return outs[:n], outs[n:2 * n], outs[2 * n:3 * n], outs[3 * n:4 * n]


def swap_wait(k, send_sem, recv_sem, part, landing, after):
    def body(q_ref, l_ref, ss, rs, after_ref, q_out, l_out):
        x, y, c = _pos()
        cp = pltpu.make_async_remote_copy(src_ref=q_ref, dst_ref=l_ref, send_sem=ss.at[0], recv_sem=rs.at[0],
                                          device_id=(x, y, 1 - c), device_id_type=MESH)
        cp.wait_send()
        cp.wait_recv()

    return pl.pallas_call(
        body, name="swap_wait_%d" % k, out_shape=[_hbm_like(part), _hbm_like(landing)],
        in_specs=[HBM, HBM, SEM, SEM, pl.BlockSpec(memory_space=pl.ANY)], out_specs=[HBM, HBM],
        input_output_aliases={0: 0, 1: 1},
        compiler_params=pltpu.CompilerParams(has_side_effects=EFFECT))(part, landing, send_sem, recv_sem, after)


def _peer(x, y, c, r):
    fx, fy, fc = (r >> 2) & 1, (r >> 1) & 1, r & 1
    return ((1 - x) if fx else x, (1 - y) if fy else y, (1 - c) if fc else c)


def ag_start(tag, pack):
    rr, cc = pack.shape

    def body(p_ref, l_ref, ss, rs, p_out, l_out, token):
        x, y, c = _pos()
        me = 4 * x + 2 * y + c
        for r in range(1, 8):
            pltpu.make_async_remote_copy(src_ref=p_ref, dst_ref=l_ref.at[me], send_sem=ss.at[r - 1], recv_sem=rs.at[r - 1],
                                         device_id=_peer(x, y, c, r), device_id_type=MESH).start()
        token[...] = jnp.zeros_like(token)

    outs = pl.pallas_call(
        body, name="ag_start_" + tag,
        out_shape=[pltpu.SemaphoreType.DMA((7,)), pltpu.SemaphoreType.DMA((7,)), _hbm_like(pack),
                   pltpu.HBM((8, rr, cc), pack.dtype), _sds((8, 128))],
        in_specs=[HBM, HBM], out_specs=[SEM, SEM, HBM, HBM, VM], input_output_aliases={0: 2, 1: 3},
        compiler_params=pltpu.CompilerParams(has_side_effects=EFFECT))(_hbm(pack), _hbm(lax.empty((8, rr, cc), pack.dtype)))
    return outs


def ag_wait(tag, send_sems, recv_sems, pack, landing, after):
    def body(p_ref, l_ref, ss, rs, after_ref, p_out, l_out):
        x, y, c = _pos()
        for r in range(1, 8):
            px, py, pc = _peer(x, y, c, r)
            cp = pltpu.make_async_remote_copy(src_ref=p_ref, dst_ref=l_ref.at[4 * px + 2 * py + pc], send_sem=ss.at[r - 1],
                                              recv_sem=rs.at[r - 1], device_id=(px, py, pc), device_id_type=MESH)
            cp.wait_send()
            cp.wait_recv()

    return pl.pallas_call(
        body, name="ag_wait_" + tag, out_shape=[_hbm_like(pack), _hbm_like(landing)],
        in_specs=[HBM, HBM, SEM, SEM, pl.BlockSpec(memory_space=pl.ANY)], out_specs=[HBM, HBM],
        input_output_aliases={0: 0, 1: 1},
        compiler_params=pltpu.CompilerParams(has_side_effects=EFFECT))(pack, landing, send_sems, recv_sems, after)


def sum_slots(landing, own, mevec):
    _, rr, cc = landing.shape

    def body(m_ref, l_ref, o_ref, out_ref):
        mine = o_ref[...]
        acc = None
        for j in range(8):
            part = jnp.where(m_ref[0] == j, mine, l_ref[j])
            acc = part if acc is None else acc + part
        out_ref[...] = acc

    gs = pltpu.PrefetchScalarGridSpec(
        num_scalar_prefetch=1, grid=(1,),
        in_specs=[pl.BlockSpec((8, rr, cc), lambda i, m: (0, 0, 0)), pl.BlockSpec((rr, cc), lambda i, m: (0, 0))],
        out_specs=pl.BlockSpec((rr, cc), lambda i, m: (0, 0)))
    return pl.pallas_call(body, name="sum_slots", grid_spec=gs, out_shape=_sds((rr, cc)),
                          compiler_params=_cparams(1))(mevec, landing, own)


_WEIGHTS = ['g_mix', 'w_in', 'w_s', 'b_s', 'ln_g', 'ln_b', 'w_gate_f', 'b_gate_f', 'w_gate_b', 'b_gate_b', 'g_gla',
            'w_out', 'g_ffn', 'w_up', 'conv_w', 'conv_b', 'w_down', 'g_final']
_BIG = ['w_in', 'w_out', 'w_up', 'w_down']
_SMALL = [n for n in _WEIGHTS if n not in _BIG]
_SMALL_SHARDED = {'w_gate_f': 64, 'w_gate_b': 64, 'conv_w': 1408}
_BIG_TR = {'w_in': 512, 'w_out': 256, 'w_up': 256, 'w_down': 352}


def _pack(arrs):
    flat = jnp.concatenate([a.reshape(-1) for a in arrs])
    pad = (-flat.shape[0]) % 1024
    return jnp.pad(flat, (0, pad)).reshape(-1, 128)


def _unpack(buf, shapes):
    flat = buf.reshape(-1)
    out, o = [], 0
    for s in shapes:
        n = 1
        for d in s:
            n *= d
        out.append(flat[o:o + n].reshape(s))
        o += n
    return out


def kernel(x, g_mix, w_in, w_s, b_s, ln_g, ln_b, w_gate_f, b_gate_f, w_gate_b, b_gate_b, g_gla, w_out, g_ffn, w_up, conv_w, conv_b, w_down, g_final, loss_target, m_g_mix, m_w_in, m_w_s, m_b_s, m_ln_g, m_ln_b, m_w_gate_f, m_b_gate_f, m_w_gate_b, m_b_gate_b, m_g_gla, m_w_out, m_g_ffn, m_w_up, m_conv_w, m_conv_b, m_w_down, m_g_final, v_g_mix, v_w_in, v_w_s, v_b_s, v_ln_g, v_ln_b, v_w_gate_f, v_b_gate_f, v_w_gate_b, v_b_gate_b, v_g_gla, v_w_out, v_g_ffn, v_w_up, v_conv_w, v_conv_b, v_w_down, v_g_final):
    loc = locals()
    w = {n: loc[n] for n in _WEIGHTS}
    m = {n: loc["m_" + n] for n in _WEIGHTS}
    v = {n: loc["v_" + n] for n in _WEIGHTS}
    xi, yi, ci = _pos()
    chip = 2 * xi + yi
    me = 2 * chip + ci
    mevec = jnp.reshape(me, (1,)).astype(jnp.int32)

    sh_names = list(_SMALL_SHARDED)
    ss_w, rs_w, pk_w, land_w, _ = ag_start("w", _pack([w[n] for n in sh_names]))

    shards = [cast_bf16(w[n], _BIG_TR[n]) for n in _BIG]
    chipvec = jnp.reshape(chip, (1,)).astype(jnp.int32)
    send_sems, recv_sems, shards_fly, landings_fly, started = gw_start(
        shards, [lax.empty(_LAYER_FULL[k], BF16) for _, k in _GW_ORDER])
    own = {"shards": shards_fly}

    pk_w, land_w = ag_wait("w", ss_w, rs_w, pk_w, land_w, started)
    per_chip = [_unpack(jnp.where(me == 2 * j, pk_w, land_w[2 * j]), [w[n].shape for n in sh_names]) for j in range(4)]
    W = dict(w)
    for k, n in enumerate(sh_names):
        W[n] = jnp.concatenate([per_chip[j][k] for j in range(4)], axis=-1)
    arrived = {}

    def get_big(l, stage, after):
        gi = {(0, "in"): 0, (0, "rest"): 1}.get((l, stage), l + 1 if stage == "in" else None)
        if gi is not None:
            lo = sum(len(g) for g in _GW_GROUPS[:gi])
            lands = landings_fly[lo:lo + len(_GW_GROUPS[gi])]
            if gi == len(_GW_GROUPS) - 1:
                full, own["shards"] = gw_wait(gi, lands, recv_sems[gi], after, shards_fly, send_sems)
            else:
                full, _ = gw_wait(gi, lands, recv_sems[gi], after)
            for (gl, gk), a in zip(_GW_GROUPS[gi], full):
                arrived[(gl, gk)] = place_own(own["shards"][gk], a, gl, gk, chipvec, _BIG_TR[_BIG[gk]])
        if stage == "in":
            f_in = jnp.transpose(arrived[(l, 0)], (1, 0, 2)).reshape(D, N_IN)
            return {"w_in": jnp.pad(f_in, ((0, 0), (0, N_INP - N_IN)))}
        return {"w_out": arrived[(l, 1)], "w_up": arrived[(l, 2)], "w_down": arrived[(l, 3)]}

    flying = []

    def emit(l, group, grads):
        ks = [3, 2] if group == "A" else [1, 0]
        gs = [grads[_BIG[k]] for k in ks]
        if group == "B":
            gs[1] = jnp.transpose(gs[1][:, :N_IN].reshape(D, 4, 648), (1, 0, 2))
        lands = [lax.empty((4,) + _LAYER_SHARD[k], BF16) for k in ks]
        tag = "%d%s" % (l, group)
        ss, rs, gs_fly, lands_fly, tok = ga_start(tag, ks, gs, lands)
        flying.append((tag, l, ks, ss, rs, gs_fly, lands_fly))
        return tok[0:1, 0:1]

    lsum, grad_x, G = local_step(x[0], loss_target[0], W, get_big, emit)

    small_shapes = [G[n].shape for n in _SMALL] + [(D,)]
    ss_g, rs_g, pk_g, land_g, started = ag_start("g", _pack([G[n] for n in _SMALL] + [lsum]))

    plane = [[None] * NL for _ in range(N_BIG)]
    for tag, l, ks, ss, rs, gs_fly, lands_fly in flying:
        for k, g, a in zip(ks, *ga_wait(tag, ks, ss, rs, gs_fly, lands_fly, started)):
            plane[k][l] = sum_parts(a, g, k, chipvec, _BIG_TR[_BIG[k]])
    ss_p, rs_p, plane_fly, other_fly = swap_start([jnp.stack(p) for p in plane])
    grads, delta, new_m, new_v = {}, {}, {}, {}
    after = grad_x
    for k in (1, 0, 3, 2):
        n = _BIG[k]
        mine, other = swap_wait(k, ss_p[k], rs_p[k], plane_fly[k], other_fly[k], after)
        grads[n], delta[n], new_m[n], new_v[n] = adamw(w[n], mine, other, m[n], v[n], _BIG_TR[n])
        after = delta[n]

    pk_g, land_g = ag_wait("g", ss_g, rs_g, pk_g, land_g, after)
    small = dict(zip(_SMALL + ["lsum"], _unpack(sum_slots(land_g, pk_g, mevec), small_shapes)))
    loss = 0.5 * jnp.sum(small.pop("lsum")) / D
    for n, wd in _SMALL_SHARDED.items():
        small[n] = lax.dynamic_slice_in_dim(small[n], chip * wd, wd, axis=small[n].ndim - 1)
    grads.update(small)
    shapes = [w[n].shape for n in _SMALL]
    pw, pg, pm, pv = (_pack([t[n] for n in _SMALL])[None] for t in (w, grads, m, v))
    _, d_, m_, v_ = adamw(pw, pg, jnp.zeros_like(pg), pm, pv, pw.shape[1])
    for t, buf in ((delta, d_), (new_m, m_), (new_v, v_)):
        t.update(zip(_SMALL, _unpack(buf, shapes)))

    return (loss, grad_x[None], *[grads[n] for n in _WEIGHTS], *[delta[n] for n in _WEIGHTS],
            *[new_m[n] for n in _WEIGHTS], *[new_v[n] for n in _WEIGHTS])
```

```python
import functools

import jax
import jax.numpy as jnp
from jax import lax
from jax.experimental import pallas as pl
from jax.experimental.pallas import tpu as pltpu

F32 = jnp.float32
BF16 = jnp.bfloat16
MX = BF16

D = 1024
CH = 128
NL = 4
N_IN = 2592
N_INP = 2688
NUP = 5632
DFF = 2816
EPS = 1e-6
VMEM_LIMIT = 56 * 1024 * 1024

ADAM_LR, ADAM_B1, ADAM_B2, ADAM_EPS, ADAM_WD, ADAM_STEP = 0.001, 0.9, 0.999, 1e-08, 0.01, 10


def _dg(a, b, ca, cb):
    return lax.dot_general(a.astype(MX), b.astype(MX), (((ca,), (cb,)), ((), ())), preferred_element_type=F32)


@jax.custom_vjp
def mm(a, b):
    return _dg(a, b, 1, 0)


mm.defvjp(lambda a, b: (_dg(a, b, 1, 0), (a, b)),
          lambda r, g: (_dg(g, r[1], 1, 1), _dg(r[0], g, 0, 0)))


@jax.custom_vjp
def mm_nt(a, b):
    return _dg(a, b, 1, 1)


mm_nt.defvjp(lambda a, b: (_dg(a, b, 1, 1), (a, b)),
             lambda r, g: (_dg(g, r[1], 1, 0), _dg(g, r[0], 0, 0)))


@jax.custom_vjp
def mm_tn(a, b):
    return _dg(a, b, 0, 0)


mm_tn.defvjp(lambda a, b: (_dg(a, b, 0, 0), (a, b)),
             lambda r, g: (_dg(r[1], g, 1, 1), _dg(r[0], g, 1, 0)))


def _split3(x):
    hi = x.astype(BF16)
    r1 = x - hi.astype(F32)
    mid = r1.astype(BF16)
    lo = (r1 - mid.astype(F32)).astype(BF16)
    return hi, mid, lo


def _dot3(m, x):
    hi, mid, lo = _split3(x)
    d = lambda p: lax.dot_general(m, p, (((1,), (0,)), ((), ())), preferred_element_type=F32)
    return d(hi) + d(mid) + d(lo)


@jax.custom_vjp
def cumdot(m, mt, x):
    return _dot3(m, x)


cumdot.defvjp(lambda m, mt, x: (_dot3(m, x), (m, mt)),
              lambda r, g: (jnp.zeros_like(r[0]), jnp.zeros_like(r[1]), _dot3(r[1], g)))


def rmsnorm(x, g):
    return x * lax.rsqrt(jnp.mean(x * x, axis=-1, keepdims=True) + EPS) * g


def gelu(x):
    return 0.5 * x * (1.0 + lax.erf(x * 0.7071067811865476))


def sigmoid(x):
    return 1.0 / (1.0 + jnp.exp(-x))


def log_sigmoid(x):
    return jnp.minimum(x, 0.0) - jnp.log(1.0 + jnp.exp(-jnp.abs(x)))


def gmlp_heads(params, pieces):
    u = [[gelu(p[0]) for p in ch] for ch in pieces]
    v = [[gelu(p[1]) for p in ch] for ch in pieces]
    mu = [[jnp.mean(x, axis=-1, keepdims=True) for x in ch] for ch in v]
    var = [[jnp.mean(jnp.square(x - m), axis=-1, keepdims=True) for x, m in zip(cv, cm)] for cv, cm in zip(v, mu)]
    vn = [[(x - m) * lax.rsqrt(s + EPS) * pr[2] + pr[3] for x, m, s, pr in zip(cv, cm, cs, params)]
          for cv, cm, cs in zip(v, mu, var)]
    mix = [[mm(pr[0], x) + pr[1] for x, pr in zip(ch, params)] for ch in vn]
    return [[a * b for a, b in zip(cu, cx)] for cu, cx in zip(u, mix)]


def outb_head(o, pg, g):
    return rmsnorm(o, g) * (pg * sigmoid(pg))


def ffn_act(zg, zv):
    return zg * sigmoid(zg) * zv


def _tri(reverse):
    r = lax.broadcasted_iota(jnp.int32, (CH, CH), 0)
    c = lax.broadcasted_iota(jnp.int32, (CH, CH), 1)
    if reverse:
        cm, sm = c >= r, c > r
    else:
        cm, sm = c <= r, c <= r
    one = jnp.ones((), BF16)
    zero = jnp.zeros((), BF16)
    return jnp.where(cm, one, zero), jnp.where(cm.T, one, zero), sm


def gla_pair(consts, wg, bg, st0, st1, *chunks):
    m, mt, smask, lm0, lm1 = consts
    ch = [chunks[5 * i:5 * i + 5] for i in range(len(chunks) // 5)]
    la = [log_sigmoid(mm(c[0], wg) + bg) * (1.0 / 16.0) for c in ch]
    cum = [cumdot(m, mt, x) for x in la]
    tot = [jnp.sum(x, axis=0, keepdims=True) for x in la]
    q_dec = [(c[1] * 0.125) * jnp.exp(cm) for c, cm in zip(ch, cum)]
    k_inv = [c[2] * jnp.exp(-cm) for c, cm in zip(ch, cum)]
    k_end = [c[2] * jnp.exp(t - cm) for c, t, cm in zip(ch, tot, cum)]
    s = [[jnp.where(smask, mm_nt(qd * lm, ki), 0.0) for lm in (lm0, lm1)] for qd, ki in zip(q_dec, k_inv)]
    o_in = [[mm(si[h], c[3 + h]) for h in (0, 1)] for si, c in zip(s, ch)]
    ds = [[mm_tn(c[3 + h], ke * lm) for h, lm in ((0, lm0), (1, lm1))] for c, ke in zip(ch, k_end)]
    sts = [(st0, st1)]
    for t, d in zip(tot, ds):
        dec = jnp.exp(t)
        sts.append((sts[-1][0] * dec + d[0], sts[-1][1] * dec + d[1]))
    outs = []
    for qd, oi, st in zip(q_dec, o_in, sts):
        outs += [oi[0] + mm_nt(qd, st[0]), oi[1] + mm_nt(qd, st[1])]
    return (*outs, sts[-1][0], sts[-1][1])


def _lane_masks():
    lane = lax.broadcasted_iota(jnp.int32, (1, 128), 1)
    return (lane < 64).astype(F32), (lane >= 64).astype(F32)


def _cparams(n_axes=1):
    return pltpu.CompilerParams(dimension_semantics=("arbitrary",) * n_axes, vmem_limit_bytes=VMEM_LIMIT)


def _full(a):
    nd = a.ndim
    return pl.BlockSpec(a.shape, lambda *_: (0,) * nd)


def _rows(tm, w, cb=0, rev_n=None):
    if rev_n is None:
        return pl.BlockSpec((tm, w), lambda i: (i, cb))
    return pl.BlockSpec((tm, w), lambda i: (rev_n - 1 - i, cb))


def _call(body, name, grid, in_specs, out_specs, out_shape, scratch=(), n_axes=1):
    return pl.pallas_call(body, name=name, grid=grid, in_specs=in_specs, out_specs=out_specs, out_shape=out_shape,
                          scratch_shapes=list(scratch), compiler_params=_cparams(n_axes))


def _sds(shape, dt=F32):
    return jax.ShapeDtypeStruct(shape, dt)


def norm_matmul(x, g, w, tm, name, ydt=F32):
    T, n = x.shape[0], w.shape[1]

    def body(x_ref, g_ref, w_ref, y_ref, h_ref):
        hb = rmsnorm(x_ref[...], g_ref[...]).astype(MX)
        h_ref[...] = hb
        y_ref[...] = jnp.dot(hb, w_ref[...], preferred_element_type=F32).astype(ydt)

    return _call(body, name, (T // tm,), [_rows(tm, D), _full(g), _full(w)],
                 [_rows(tm, n), _rows(tm, D)], [_sds((T, n), ydt), _sds((T, D), MX)])(x, g, w)


CPB = 8


def _chunk(c):
    return slice(c * CH, (c + 1) * CH)


def gmlp_fwd(p, ws, bs, lg, lb):
    T = p.shape[0]
    tm = CPB * CH

    def body(pa_ref, ws_ref, bs_ref, lg_ref, lb_ref, o_ref):
        params = [(ws_ref[h], bs_ref[h], lg_ref[h], lb_ref[h]) for h in range(4)]
        pieces = [[(pa_ref[_chunk(c), h * 128:(h + 1) * 128], pa_ref[_chunk(c), 512 + h * 128:512 + (h + 1) * 128])
                   for h in range(4)] for c in range(CPB)]
        out = gmlp_heads(params, pieces)
        for c in range(CPB):
            for h in range(4):
                o_ref[_chunk(c), h * 128:(h + 1) * 128] = out[c][h].astype(MX)

    return _call(body, "gmlp_fwd", (T // tm,), [_rows(tm, 1024), _full(ws), _full(bs), _full(lg), _full(lb)],
                 _rows(tm, 512), _sds((T, 512), MX))(p, ws, bs, lg, lb)


def _gla_in_specs(tm, n, rev):
    r = n if rev else None
    return [_rows(tm, 256, 4, r), _rows(tm, 256, 5, r), _rows(tm, 512, 3, r), _rows(tm, 128, 20, r)]


def gla_fwd(p, wg, bg, reverse):
    T = p.shape[0]
    tm = CPB * CH
    n = T // tm
    rev = n if reverse else None

    def body(q_ref, k_ref, v_ref, r_ref, wg_ref, bg_ref, o_ref, ss_ref, st_ref):
        @pl.when(pl.program_id(0) == 0)
        def _():
            st_ref[...] = jnp.zeros_like(st_ref)

        consts = _tri(reverse) + _lane_masks()
        order = list(reversed(range(CPB))) if reverse else list(range(CPB))
        ss_ref[0] = st_ref[...]
        for j in range(2):
            sl = slice(j * 128, (j + 1) * 128)
            v0s, v1s = slice(256 * j, 256 * j + 128), slice(256 * j + 128, 256 * j + 256)
            chunks = []
            for c in order:
                rows = _chunk(c)
                chunks += [r_ref[rows, :], q_ref[rows, sl], k_ref[rows, sl], v_ref[rows, v0s], v_ref[rows, v1s]]
            res = gla_pair(consts, wg_ref[:, sl], bg_ref[:, sl], st_ref[2 * j], st_ref[2 * j + 1], *chunks)
            for i, c in enumerate(order):
                o_ref[_chunk(c), v0s] = res[2 * i]
                o_ref[_chunk(c), v1s] = res[2 * i + 1]
            st_ref[2 * j] = res[-2]
            st_ref[2 * j + 1] = res[-1]

    ss_spec = pl.BlockSpec((1, 4, 128, 128), (lambda i: (n - 1 - i, 0, 0, 0)) if reverse else (lambda i: (i, 0, 0, 0)))
    return _call(body, "gla_fwd_r" if reverse else "gla_fwd_f", (n,),
                 _gla_in_specs(tm, n, reverse) + [_full(wg), _full(bg)],
                 [_rows(tm, 512, 0, rev), ss_spec], [_sds((T, 512)), _sds((n, 4, 128, 128))],
                 scratch=[pltpu.VMEM((4, 128, 128), F32)])(p, p, p, p, wg, bg)


def mix_out(x, of, ob, p, outa, gg, w_out, tm):
    T = x.shape[0]

    def body(x_ref, of_ref, ob_ref, pg_ref, oa_ref, gg_ref, w_ref, x1_ref, mx_ref):
        mx_ref[:, 0:512] = oa_ref[...]
        for h in range(4):
            sl = slice(h * 128, (h + 1) * 128)
            mx_ref[:, 512 + h * 128:512 + (h + 1) * 128] = outb_head(
                of_ref[:, sl] + ob_ref[:, sl], pg_ref[:, sl], gg_ref[h]).astype(MX)
        x1_ref[...] = x_ref[...] + jnp.dot(mx_ref[...], w_ref[...], preferred_element_type=F32)

    return _call(body, "mix_out", (T // tm,),
                 [_rows(tm, D), _rows(tm, 512), _rows(tm, 512), _rows(tm, 512, 4), _rows(tm, 512), _full(gg), _full(w_out)],
                 [_rows(tm, D), _rows(tm, 1024)], [_sds((T, D)), _sds((T, 1024), MX)])(x, of, ob, p, outa, gg, w_out)


HALO = 16


def _halo_specs(T, tm, w):
    nb = T // HALO
    r = tm // HALO
    return [pl.BlockSpec((tm, w), lambda i: (i, 0)),
            pl.BlockSpec((HALO, w), lambda i: (jnp.maximum(i * r - 1, 0), 0)),
            pl.BlockSpec((HALO, w), lambda i: (jnp.minimum((i + 1) * r, nb - 1), 0))]


def ffn_up_conv(x1, g, w_up, cw, cb, tm):
    T = x1.shape[0]
    ns = T // tm
    cwid = 256

    def body(x_ref, g_ref, w_ref, cw_ref, cb_ref, zu_ref, h_ref, z_ref, a_ref, prev_ref, tail_ref):
        i = pl.program_id(0)

        @pl.when(i == 0)
        def _():
            prev_ref[...] = jnp.zeros_like(prev_ref)
            tail_ref[...] = jnp.zeros_like(tail_ref)

        hb = rmsnorm(x_ref[...], g_ref[...]).astype(MX)
        h_ref[...] = hb
        row = lax.broadcasted_iota(jnp.int32, (tm, 1), 0)
        for c0 in range(0, DFF, cwid):
            z2 = []
            for cs in (slice(c0, c0 + cwid), slice(DFF + c0, DFF + c0 + cwid)):
                zub = jnp.dot(hb, w_ref[:, cs], preferred_element_type=F32).astype(MX)
                zu_ref[:, cs] = zub
                prev = prev_ref[:, cs].astype(F32)
                pr = tail_ref[HALO - 1:HALO, cs].astype(F32)
                nx = jnp.where(i < ns, zub[0:1, :].astype(F32), 0.0)
                dn = jnp.where(row == 0, pr, pltpu.roll(prev, 1, 0))
                up = jnp.where(row == tm - 1, nx, pltpu.roll(prev, tm - 1, 0))
                z = cb_ref[:, cs] + dn * cw_ref[0:1, cs] + prev * cw_ref[1:2, cs] + up * cw_ref[2:3, cs]
                z_ref[:, cs] = z.astype(MX)
                tail_ref[:, cs] = prev_ref[tm - HALO:tm, cs]
                prev_ref[:, cs] = zub
                z2.append(z)
            a_ref[:, c0:c0 + cwid] = ffn_act(z2[0], z2[1]).astype(MX)

    cur = lambda w: pl.BlockSpec((tm, w), lambda i: (jnp.minimum(i, ns - 1), 0))
    late = lambda w: pl.BlockSpec((tm, w), lambda i: (jnp.maximum(i - 1, 0), 0))
    return _call(body, "ffn_up", (ns + 1,), [cur(D), _full(g), _full(w_up), _full(cw), _full(cb)],
                 [cur(NUP), cur(D), late(NUP), late(DFF)],
                 [_sds((T, NUP), MX), _sds((T, D), MX), _sds((T, NUP), MX), _sds((T, DFF), MX)],
                 scratch=[pltpu.VMEM((tm, NUP), MX), pltpu.VMEM((HALO, NUP), MX)])(x1, g, w_up, cw, cb)


def matmul_res(a, w, res, tm, name):
    T, k = a.shape
    n = w.shape[1]

    def body(a_ref, w_ref, r_ref, o_ref):
        o_ref[...] = r_ref[...] + jnp.dot(a_ref[...], w_ref[...], preferred_element_type=F32)

    return _call(body, name, (T // tm,), [_rows(tm, k), _full(w), _rows(tm, n)], _rows(tm, n), _sds((T, n)))(a, w, res)


def loss_head(x, g, tgt, tm):
    T = x.shape[0]

    def body(x_ref, g_ref, t_ref, l_ref, dx_ref, dg_ref):
        @pl.when(pl.program_id(0) == 0)
        def _():
            l_ref[...] = jnp.zeros_like(l_ref)
            dg_ref[...] = jnp.zeros_like(dg_ref)

        y, vjp = jax.vjp(rmsnorm, x_ref[...], g_ref[...])
        err = y - t_ref[...]
        l_ref[...] += jnp.sum(err * err, axis=0, keepdims=True)
        dx, dg = vjp(err * (1.0 / D))
        dx_ref[...] = dx
        dg_ref[...] += dg

    return _call(body, "loss_head", (T // tm,), [_rows(tm, D), _full(g), _rows(tm, D)],
                 [_full(g), _rows(tm, D), _full(g)], [_sds((1, D)), _sds((T, D)), _sds((1, D))])(x, g, tgt)


def ffn_down_bwd(dx2, z, w_down, tm):
    T = dx2.shape[0]

    def body(dx_ref, z_ref, w_ref, dz_ref):
        da = _dg(dx_ref[...], w_ref[...], 1, 1)
        zg, zv = z_ref[:, :DFF].astype(F32), z_ref[:, DFF:].astype(F32)
        s = sigmoid(zg)
        sz = zg * s
        dz_ref[:, :DFF] = (da * zv * (s + sz * (1.0 - s))).astype(MX)
        dz_ref[:, DFF:] = (da * sz).astype(MX)

    return _call(body, "ffn_down_bwd", (T // tm,), [_rows(tm, D), _rows(tm, NUP), _full(w_down)],
                 _rows(tm, NUP), _sds((T, NUP), MX))(dx2, z, w_down)


def ffn_up_bwd(dz, zu, cw, w_up, x1, g, dres, tm):
    T = dz.shape[0]
    ns = T // tm
    cwid = 512

    def body(dz_ref, dp_ref, dn_ref, zu_ref, cw_ref, w_ref, x_ref, g_ref, dr_ref,
             dzu_ref, dx_ref, dg_ref, dcw_ref, dcb_ref):
        i = pl.program_id(0)

        @pl.when(i == 0)
        def _():
            for r in (dg_ref, dcw_ref, dcb_ref):
                r[...] = jnp.zeros_like(r)

        row = lax.broadcasted_iota(jnp.int32, (tm, 1), 0)
        dh = jnp.zeros((tm, D), F32)
        for c0 in range(0, NUP, cwid):
            cs = slice(c0, c0 + cwid)
            dz = dz_ref[:, cs].astype(F32)
            zu = zu_ref[:, cs].astype(F32)
            pr = jnp.where(i > 0, dp_ref[HALO - 1:HALO, cs].astype(F32), 0.0)
            nx = jnp.where(i < ns - 1, dn_ref[0:1, cs].astype(F32), 0.0)
            ddn = jnp.where(row == 0, pr, pltpu.roll(dz, 1, 0))
            dup = jnp.where(row == tm - 1, nx, pltpu.roll(dz, tm - 1, 0))
            dzu = (dup * cw_ref[0:1, cs] + dz * cw_ref[1:2, cs] + ddn * cw_ref[2:3, cs]).astype(MX)
            dzu_ref[:, cs] = dzu
            dcw_ref[0:1, cs] += jnp.sum(zu * dup, axis=0, keepdims=True)
            dcw_ref[1:2, cs] += jnp.sum(zu * dz, axis=0, keepdims=True)
            dcw_ref[2:3, cs] += jnp.sum(zu * ddn, axis=0, keepdims=True)
            dcb_ref[:, cs] += jnp.sum(dz, axis=0, keepdims=True)
            dh = dh + _dg(dzu, w_ref[:, cs], 1, 1)
        _, vjp = jax.vjp(rmsnorm, x_ref[...], g_ref[...])
        dx, dg = vjp(dh)
        dx_ref[...] = dr_ref[...] + dx
        dg_ref[...] += dg

    return _call(body, "ffn_up_bwd", (ns,),
                 _halo_specs(T, tm, NUP) + [_rows(tm, NUP), _full(cw), _full(w_up), _rows(tm, D), _full(g), _rows(tm, D)],
                 [_rows(tm, NUP), _rows(tm, D), _full(g), _full(cw), pl.BlockSpec((1, NUP), lambda i: (0, 0))],
                 [_sds((T, NUP), MX), _sds((T, D)), _sds((1, D)), _sds((3, NUP)), _sds((1, NUP))])(
                     dz, dz, dz, zu, cw, w_up, x1, g, dres)


def nt_normbwd(dys, w, x, g, dres, tm, name):
    T = x.shape[0]
    n = len(dys)
    offs = [sum(d.shape[1] for d in dys[:i]) for i in range(n + 1)]

    def body(*refs):
        dy_refs, (w_ref, x_ref, g_ref, dr_ref, dx_ref, dg_ref) = refs[:n], refs[n:]

        @pl.when(pl.program_id(0) == 0)
        def _():
            dg_ref[...] = jnp.zeros_like(dg_ref)

        dh = _dg(dy_refs[0][...], w_ref[:, offs[0]:offs[1]], 1, 1)
        for i in range(1, n):
            dh = dh + _dg(dy_refs[i][...], w_ref[:, offs[i]:offs[i + 1]], 1, 1)
        _, vjp = jax.vjp(rmsnorm, x_ref[...], g_ref[...])
        dx, dg = vjp(dh)
        dx_ref[...] = dr_ref[...] + dx
        dg_ref[...] += dg

    return _call(body, name, (T // tm,),
                 [_rows(tm, d.shape[1]) for d in dys] + [_full(w), _rows(tm, D), _full(g), _rows(tm, D)],
                 [_rows(tm, D), _full(g)], [_sds((T, D)), _sds((1, D))])(*dys, w, x, g, dres)


def matmul_tn(a, b, tt, tn, name):
    T, k = a.shape
    n = b.shape[1]
    last = T // tt - 1

    def body(a_ref, b_ref, o_ref, acc_ref):
        @pl.when(pl.program_id(1) == 0)
        def _():
            acc_ref[...] = jnp.zeros_like(acc_ref)

        acc_ref[...] += _dg(a_ref[...], b_ref[...], 0, 0)

        @pl.when(pl.program_id(1) == last)
        def _():
            o_ref[...] = acc_ref[...].astype(MX)

    return _call(body, name, (n // tn, T // tt),
                 [pl.BlockSpec((tt, k), lambda j, i: (i, 0)), pl.BlockSpec((tt, tn), lambda j, i: (i, j))],
                 pl.BlockSpec((k, tn), lambda j, i: (0, j)), _sds((k, n), MX), scratch=[pltpu.VMEM((k, tn), F32)],
                 n_axes=2)(a, b)


def mix_out_bwd(dx1, w_out, of, ob, p, gg, tm):
    T = dx1.shape[0]

    def body(dx_ref, w_ref, of_ref, ob_ref, pg_ref, gg_ref, da_ref, do_ref, dpg_ref, dgg_ref):
        @pl.when(pl.program_id(0) == 0)
        def _():
            dgg_ref[...] = jnp.zeros_like(dgg_ref)

        dxb = dx_ref[...].astype(MX)
        da_ref[...] = _dg(dxb, w_ref[0:512, :], 1, 1)
        for h in range(4):
            sl = slice(h * 128, (h + 1) * 128)
            dm = _dg(dxb, w_ref[512 + h * 128:512 + (h + 1) * 128, :], 1, 1)
            _, vjp = jax.vjp(outb_head, of_ref[:, sl] + ob_ref[:, sl], pg_ref[:, sl], gg_ref[h])
            do, dpg, dg = vjp(dm)
            do_ref[:, sl] = do
            dpg_ref[:, sl] = dpg
            dgg_ref[h] += dg

    return _call(body, "mix_out_bwd", (T // tm,),
                 [_rows(tm, D), _full(w_out), _rows(tm, 512), _rows(tm, 512), _rows(tm, 512, 4), _full(gg)],
                 [_rows(tm, 512), _rows(tm, 512), _rows(tm, 512), _full(gg)],
                 [_sds((T, 512)), _sds((T, 512)), _sds((T, 512)), _sds(gg.shape)])(dx1, w_out, of, ob, p, gg)


def gla_bwd(p, wg, bg, ss, do, reverse, merge=None):
    T = p.shape[0]
    tm = CPB * CH
    n = T // tm
    rev = not reverse
    rn = n if rev else None

    def body(*refs):
        q_ref, k_ref, v_ref, r_ref, wg_ref, bg_ref, ss_ref, do_ref = refs[:8]
        if merge is None:
            dq_ref, dk_ref, dv_ref, dr_ref, dwg_ref, dbg_ref, dst_ref = refs[8:]
        else:
            mq_ref, mk_ref, mv_ref, mr_ref, mg_ref, out_ref, dwg_ref, dbg_ref, dst_ref, drs_ref = refs[8:]
            out_ref[:, 1024:1536] = mg_ref[...].astype(MX)

        @pl.when(pl.program_id(0) == 0)
        def _():
            dst_ref[...] = jnp.zeros_like(dst_ref)
            dwg_ref[...] = jnp.zeros_like(dwg_ref)
            dbg_ref[...] = jnp.zeros_like(dbg_ref)

        consts = _tri(reverse) + _lane_masks()
        order = list(reversed(range(CPB))) if reverse else list(range(CPB))
        for j in range(2):
            sl = slice(j * 128, (j + 1) * 128)
            v0s, v1s = slice(256 * j, 256 * j + 128), slice(256 * j + 128, 256 * j + 256)
            chunks, dout = [], []
            for c in order:
                rows = _chunk(c)
                chunks += [r_ref[rows, :], q_ref[rows, sl], k_ref[rows, sl], v_ref[rows, v0s], v_ref[rows, v1s]]
                dout += [do_ref[rows, v0s], do_ref[rows, v1s]]
            _, vjp = jax.vjp(functools.partial(gla_pair, consts), wg_ref[:, sl], bg_ref[:, sl],
                             ss_ref[0, 2 * j], ss_ref[0, 2 * j + 1], *chunks)
            g = vjp((*dout, dst_ref[2 * j], dst_ref[2 * j + 1]))
            dwg_ref[:, sl] += g[0]
            dbg_ref[:, sl] += g[1]
            dst_ref[2 * j] = g[2]
            dst_ref[2 * j + 1] = g[3]
            for i, c in enumerate(order):
                rows = _chunk(c)
                dr, dq, dk, dv0, dv1 = g[4 + 5 * i:9 + 5 * i]
                if merge is None:
                    if j == 0:
                        dr_ref[rows, :] = dr
                    else:
                        dr_ref[rows, :] += dr
                    dq_ref[rows, sl] = dq
                    dk_ref[rows, sl] = dk
                    dv_ref[rows, v0s] = dv0
                    dv_ref[rows, v1s] = dv1
                else:
                    if j == 0:
                        drs_ref[rows, :] = mr_ref[rows, :] + dr
                    else:
                        out_ref[rows, 1536:1664] = (drs_ref[rows, :] + dr).astype(MX)
                    out_ref[rows, sl] = (mq_ref[rows, sl] + dq).astype(MX)
                    out_ref[rows, 256 + 128 * j:384 + 128 * j] = (mk_ref[rows, sl] + dk).astype(MX)
                    out_ref[rows, 512 + 256 * j:640 + 256 * j] = (mv_ref[rows, v0s] + dv0).astype(MX)
                    out_ref[rows, 640 + 256 * j:768 + 256 * j] = (mv_ref[rows, v1s] + dv1).astype(MX)

    ss_spec = pl.BlockSpec((1, 4, 128, 128), (lambda i: (n - 1 - i, 0, 0, 0)) if rev else (lambda i: (i, 0, 0, 0)))
    ins = [p, p, p, p, wg, bg, ss, do]
    in_specs = _gla_in_specs(tm, n, rev) + [_full(wg), _full(bg), ss_spec, _rows(tm, 512, 0, rn)]
    scratch = [pltpu.VMEM((4, 128, 128), F32)]
    if merge is None:
        out_specs = [_rows(tm, 256, 0, rn), _rows(tm, 256, 0, rn), _rows(tm, 512, 0, rn), _rows(tm, 128, 0, rn)]
        out_shape = [_sds((T, 256)), _sds((T, 256)), _sds((T, 512)), _sds((T, 128))]
    else:
        ins += list(merge)
        in_specs += [_rows(tm, a.shape[1], 0, rn) for a in merge]
        out_specs, out_shape = [_rows(tm, 1664, 0, rn)], [_sds((T, 1664), MX)]
        scratch.append(pltpu.VMEM((tm, 128), F32))
    return _call(body, "gla_bwd_r" if reverse else "gla_bwd_f", (n,), in_specs, out_specs + [_full(wg), _full(bg)],
                 out_shape + [_sds(wg.shape), _sds(bg.shape)], scratch=scratch)(*ins)


def gmlp_bwd(p, douta, ws, bs, lg, lb):
    T = p.shape[0]
    cpb = 4
    tm = cpb * CH

    def body(pa_ref, do_ref, ws_ref, bs_ref, lg_ref, lb_ref, dpa_ref, dws_ref, dbs_ref, dlg_ref, dlb_ref):
        @pl.when(pl.program_id(0) == 0)
        def _():
            for r in (dws_ref, dbs_ref, dlg_ref, dlb_ref):
                r[...] = jnp.zeros_like(r)

        us = [slice(h * 128, (h + 1) * 128) for h in range(4)]
        vs = [slice(512 + h * 128, 512 + (h + 1) * 128) for h in range(4)]
        params = [(ws_ref[h], bs_ref[h], lg_ref[h], lb_ref[h]) for h in range(4)]
        pieces = [[(pa_ref[_chunk(c), us[h]], pa_ref[_chunk(c), vs[h]]) for h in range(4)] for c in range(cpb)]
        _, vjp = jax.vjp(gmlp_heads, params, pieces)
        dparams, dpieces = vjp([[do_ref[_chunk(c), us[h]] for h in range(4)] for c in range(cpb)])
        for h in range(4):
            for r, a in zip((dws_ref, dbs_ref, dlg_ref, dlb_ref), dparams[h]):
                r[h] += a
            for c in range(cpb):
                dpa_ref[_chunk(c), us[h]] = dpieces[c][h][0].astype(MX)
                dpa_ref[_chunk(c), vs[h]] = dpieces[c][h][1].astype(MX)

    return _call(body, "gmlp_bwd", (T // tm,),
                 [_rows(tm, 1024), _rows(tm, 512), _full(ws), _full(bs), _full(lg), _full(lb)],
                 [_rows(tm, 1024), _full(ws), _full(bs), _full(lg), _full(lb)],
                 [_sds((T, 1024), MX), _sds(ws.shape), _sds(bs.shape), _sds(lg.shape), _sds(lb.shape)])(p, douta, ws, bs, lg, lb)


def _gate_pad(w, row0):
    return jnp.zeros((128, 256), F32).at[row0:row0 + 16].set(w)


def local_step(x, tgt, W, get_big, emit, tm=256, tmm=512):
    saved = []
    for l in range(NL):
        s = {"x": x}
        s.update(get_big(l, "in", x))
        p, s["h"] = norm_matmul(x, W["g_mix"][l][None], s["w_in"], tmm, "mix_in")
        s["p"] = p
        ws, bs = W["w_s"][l], W["b_s"][l][:, :, None]
        lg, lb = W["ln_g"][l][:, None, :], W["ln_b"][l][:, None, :]
        outa = gmlp_fwd(p, ws, bs, lg, lb)
        wgf, wgb = _gate_pad(W["w_gate_f"][l], 0), _gate_pad(W["w_gate_b"][l], 16)
        bgf, bgb = W["b_gate_f"][l][None], W["b_gate_b"][l][None]
        s["of"], s["ssf"] = gla_fwd(p, wgf, bgf, False)
        s["ob"], s["ssb"] = gla_fwd(p, wgb, bgb, True)
        s.update(get_big(l, "rest", s["ob"]))
        gg = W["g_gla"][l][:, None, :]
        x1, s["mixed"] = mix_out(x, s["of"], s["ob"], p, outa, gg, s["w_out"], tmm)
        s["x1"] = x1
        s["zu"], s["h2"], s["z"], s["a"] = ffn_up_conv(x1, W["g_ffn"][l][None], s["w_up"], W["conv_w"][l],
                                                       W["conv_b"][l][None], tm)
        x = matmul_res(s["a"], s["w_down"], x1, tmm, "ffn_down")
        saved.append(s)

    lsum, dx, dgf = loss_head(x, W["g_final"][None], tgt, tmm)
    G = {k: [None] * NL for k in _SMALL if k != "g_final"}
    tok = jnp.zeros((1, 1), F32)
    for l in reversed(range(NL)):
        s = saved[l]
        g_down = matmul_tn(s["a"], dx, min(1024, tmm * 2), 512, "dw_down")
        dz = ffn_down_bwd(dx, s["z"], s["w_down"], tm)
        dzu, dx1, dg, G["conv_w"][l], dcb = ffn_up_bwd(dz, s["zu"], W["conv_w"][l] + tok, s["w_up"], s["x1"],
                                                       W["g_ffn"][l][None], dx, tm)
        G["conv_b"][l], G["g_ffn"][l] = dcb[0], dg[0]
        g_up = matmul_tn(s["h2"], dzu, min(1024, tmm * 2), 1408, "dw_up")
        tok = emit(l, "A", {"w_down": g_down, "w_up": g_up})
        g_out = matmul_tn(s["mixed"], dx1, min(1024, tmm * 2), 1024, "dw_out")
        gg = W["g_gla"][l][:, None, :] + tok
        douta, do, dpg, dgg = mix_out_bwd(dx1, s["w_out"], s["of"], s["ob"], s["p"], gg, tmm)
        G["g_gla"][l] = dgg[:, 0, :]
        wgf, wgb = _gate_pad(W["w_gate_f"][l], 0), _gate_pad(W["w_gate_b"][l], 16)
        bgf, bgb = W["b_gate_f"][l][None], W["b_gate_b"][l][None]
        dqf, dkf, dvf, drf, dwgf, dbgf = gla_bwd(s["p"], wgf, bgf, s["ssf"], do, False)
        dpb, dwgb, dbgb = gla_bwd(s["p"], wgb, bgb, s["ssb"], do, True, merge=(dqf, dkf, dvf, drf, dpg))
        G["w_gate_f"][l], G["b_gate_f"][l] = dwgf[0:16], dbgf[0]
        G["w_gate_b"][l], G["b_gate_b"][l] = dwgb[16:32], dbgb[0]
        ws, bs = W["w_s"][l], W["b_s"][l][:, :, None]
        lg, lb = W["ln_g"][l][:, None, :], W["ln_b"][l][:, None, :]
        dpa, G["w_s"][l], dbs, dlg, dlb = gmlp_bwd(s["p"], douta, ws, bs, lg, lb)
        G["b_s"][l], G["ln_g"][l], G["ln_b"][l] = dbs[:, :, 0], dlg[:, 0, :], dlb[:, 0, :]
        tt = min(1024, tmm * 2)
        g_in = jnp.concatenate([matmul_tn(s["h"], dpa, tt, 1024, "dw_in_a"),
                                matmul_tn(s["h"], dpb, tt, 1664, "dw_in_b")], axis=1)
        tok = emit(l, "B", {"w_out": g_out, "w_in": g_in})
        dx, dg = nt_normbwd([dpa, dpb], s["w_in"], s["x"], W["g_mix"][l][None] + tok, dx1, tmm, "mix_in_bwd")
        G["g_mix"][l] = dg[0]
    G = {k: jnp.stack(v) for k, v in G.items()}
    G["g_final"] = dgf[0]
    return lsum, dx, G


def _rows3(tr, c):
    return pl.BlockSpec((None, tr, c), lambda l, i: (l, i, 0))


def cast_bf16(a, tr):
    nl, r, c = a.shape

    def body(a_ref, o_ref):
        o_ref[...] = a_ref[...].astype(BF16)

    return _call(body, "cast_bf16", (nl, r // tr), [_rows3(tr, c)], _rows3(tr, c), _sds(a.shape, BF16), n_axes=2)(a)


def sum_parts(land, grad, k, chipvec, tr):
    _, rr, cc = land.shape
    nb = rr // tr

    def body(c_ref, l_ref, g_ref, o_ref):
        mine = g_ref[...].astype(F32)
        acc = None
        for j in range(4):
            part = jnp.where(c_ref[0] == j, mine, l_ref[j].astype(F32))
            acc = part if acc is None else acc + part
        o_ref[...] = acc

    gs = pltpu.PrefetchScalarGridSpec(
        num_scalar_prefetch=1, grid=(nb,),
        in_specs=[pl.BlockSpec((4, tr, cc), lambda i, c: (0, i, 0)), _part_spec(k, tr, nb)],
        out_specs=pl.BlockSpec((tr, cc), lambda i, c: (i, 0)))
    return pl.pallas_call(body, name="sum_parts", grid_spec=gs, out_shape=_sds((rr, cc)),
                          compiler_params=_cparams(1))(chipvec, land, grad)


def adamw(w, ga, gb, m, v, tr):
    nl, r, c = w.shape

    def body(w_ref, ga_ref, gb_ref, m_ref, v_ref, g_ref, d_ref, nm_ref, nv_ref):
        gr = ga_ref[...] + gb_ref[...]
        g_ref[...] = gr
        nm = ADAM_B1 * m_ref[...] + (1.0 - ADAM_B1) * gr
        nv = ADAM_B2 * v_ref[...] + (1.0 - ADAM_B2) * jnp.square(gr)
        m_hat = nm / (1.0 - ADAM_B1 ** ADAM_STEP)
        v_hat = nv / (1.0 - ADAM_B2 ** ADAM_STEP)
        d_ref[...] = -ADAM_LR * (m_hat / (jnp.sqrt(v_hat) + ADAM_EPS) + ADAM_WD * w_ref[...])
        nm_ref[...] = nm
        nv_ref[...] = nv

    sp = _rows3(tr, c)
    return _call(body, "adamw", (nl, r // tr), [sp] * 5, [sp] * 4, [_sds(w.shape)] * 4, n_axes=2)(w, ga, gb, m, v)


MESH = pl.DeviceIdType.MESH
ANY = pl.BlockSpec(memory_space=pl.ANY)
N_BIG = 4


def _pos():
    return lax.axis_index("x"), lax.axis_index("y"), lax.axis_index("c")


def _other_chips(x, y):
    return [(1 - x, y), (x, 1 - y), (1 - x, 1 - y)]


def _slab(k, ref, j):
    if k == 0:
        return ref.at[j]
    if k == 1:
        return ref.at[pl.ds(256 * j, 256), :]
    if k == 2:
        return ref.at[:, pl.ds(1408 * j, 1408)]
    return ref.at[pl.ds(704 * j, 704), :]


_LAYER_FULL = [(4, 1024, 648), (1024, 1024), (1024, NUP), (DFF, 1024)]
_LAYER_SHARD = [(1024, 648), (256, 1024), (1024, 1408), (704, 1024)]
_SHARD_SHAPES = [(NL,) + s for s in _LAYER_SHARD]

HBM = pl.BlockSpec(memory_space=pltpu.HBM)
SEM = pl.BlockSpec(memory_space=pltpu.SEMAPHORE)
VM = pl.BlockSpec(memory_space=pltpu.VMEM)
EFFECT = pltpu.SideEffectType.DATAFLOW_SIDE_EFFECTING
_GW_GROUPS = [[(0, 0)], [(0, 1), (0, 2), (0, 3)]] + [[(l, k) for k in range(N_BIG)] for l in range(1, NL)]
_GW_ORDER = [lk for g in _GW_GROUPS for lk in g]


def _hbm(a):
    return pltpu.with_memory_space_constraint(a, pltpu.HBM)


def _hbm_like(a):
    return pltpu.HBM(a.shape, a.dtype)


def _part_spec(k, tr, nb):
    cc = _LAYER_SHARD[k][1]
    if k == 0:
        return pl.BlockSpec((None, tr, cc), lambda i, c: (c[0], i, 0))
    if k == 2:
        return pl.BlockSpec((tr, cc), lambda i, c: (i, c[0]))
    return pl.BlockSpec((tr, cc), lambda i, c: (c[0] * nb + i, 0))


def place_own(shard, landing, l, k, chipvec, tr):
    rr, cc = _LAYER_SHARD[k]
    nb = rr // tr

    def body(c_ref, s_ref, l_ref, o_ref):
        o_ref[...] = s_ref[...]

    gs = pltpu.PrefetchScalarGridSpec(
        num_scalar_prefetch=1, grid=(nb,),
        in_specs=[pl.BlockSpec((None, tr, cc), lambda i, c: (l, i, 0)), ANY], out_specs=_part_spec(k, tr, nb))
    return pl.pallas_call(body, name="place_own", grid_spec=gs, out_shape=_sds(landing.shape, landing.dtype),
                          input_output_aliases={2: 0}, compiler_params=_cparams(1))(chipvec, shard, landing)


def gw_start(shards, landings, after):
    n = len(_GW_ORDER)

    def body(*refs):
        S, Ld = refs[:N_BIG], refs[N_BIG:N_BIG + n]
        outs = refs[N_BIG + n + 1:]
        send_sems, recv, token = outs[0], outs[1:1 + len(_GW_GROUPS)], outs[-1]
        x, y, c = _pos()
        me = 2 * x + y
        ci = 0
        for gi, grp in enumerate(_GW_GROUPS):
            for t, (l, k) in enumerate(grp):
                land = Ld[_GW_ORDER.index((l, k))]
                for j, (px, py) in enumerate(_other_chips(x, y)):
                    pltpu.make_async_remote_copy(
                        src_ref=S[k].at[l], dst_ref=_slab(k, land, me), send_sem=send_sems.at[ci],
                        recv_sem=recv[gi].at[3 * t + j], device_id=(px, py, c), device_id_type=MESH).start()
                    ci += 1
        token[...] = jnp.zeros_like(token)

    ins = list(shards) + list(landings)
    sems = [pltpu.SemaphoreType.DMA((3 * n,))] + [pltpu.SemaphoreType.DMA((3 * len(g),)) for g in _GW_GROUPS]
    outs = pl.pallas_call(
        body, name="gw_start", out_shape=sems + [_hbm_like(a) for a in ins] + [_sds((8, 128))],
        in_specs=[HBM] * len(ins) + [pl.BlockSpec(memory_space=pl.ANY)],
        out_specs=[SEM] * len(sems) + [HBM] * len(ins) + [VM],
        input_output_aliases={i: len(sems) + i for i in range(len(ins))},
        compiler_params=pltpu.CompilerParams(has_side_effects=EFFECT))(*[_hbm(a) for a in ins], after)
    ns = len(sems)
    return outs[0], outs[1:ns], outs[ns:ns + N_BIG], outs[ns + N_BIG:ns + len(ins)], outs[-1]


def gw_wait(gi, landings, recv_sems, after, shards=None, send_sems=None):
    grp = _GW_GROUPS[gi]
    n = len(grp)
    last = shards is not None

    def body(*refs):
        Ld, rs = refs[:n], refs[n]
        x, y, c = _pos()
        for t, (l, k) in enumerate(grp):
            for j, (px, py) in enumerate(_other_chips(x, y)):
                region = _slab(k, Ld[t], 2 * px + py)
                pltpu.make_async_remote_copy(src_ref=region, dst_ref=region, send_sem=rs.at[3 * t + j],
                                             recv_sem=rs.at[3 * t + j], device_id=(px, py, c),
                                             device_id_type=MESH).wait_recv()
        if last:
            S, ss = refs[n + 2:n + 2 + N_BIG], refs[n + 2 + N_BIG]
            me = 2 * x + y
            for ci, (l, k) in enumerate(lk for lk in _GW_ORDER for _ in range(3)):
                pltpu.make_async_remote_copy(src_ref=S[k].at[l], dst_ref=_slab(k, Ld[k], me), send_sem=ss.at[ci],
                                             recv_sem=ss.at[ci], device_id=(x, y, c), device_id_type=MESH).wait_send()

    ins = list(landings) + [recv_sems, after]
    specs = [HBM] * n + [SEM, pl.BlockSpec(memory_space=pl.ANY)]
    outs = [_hbm_like(a) for a in landings]
    alias = {i: i for i in range(n)}
    if last:
        ins += list(shards) + [send_sems]
        specs += [HBM] * N_BIG + [SEM]
        outs += [_hbm_like(a) for a in shards]
        alias.update({n + 2 + i: n + i for i in range(N_BIG)})
    res = pl.pallas_call(body, name="gw_wait_%d" % gi, out_shape=outs, in_specs=specs, out_specs=[HBM] * len(outs),
                         input_output_aliases=alias,
                         compiler_params=pltpu.CompilerParams(has_side_effects=EFFECT))(*ins)
    return res[:n], (res[n:] if last else None)


def ga_start(tag, ks, grads, landings):
    n = len(ks)

    def body(*refs):
        G, Ld = refs[:n], refs[n:2 * n]
        send_sems, recv_sems, token = refs[2 * n], refs[2 * n + 1], refs[-1]
        x, y, c = _pos()
        me = 2 * x + y
        for t, k in enumerate(ks):
            for j, (px, py) in enumerate(_other_chips(x, y)):
                pltpu.make_async_remote_copy(
                    src_ref=_slab(k, G[t], 2 * px + py), dst_ref=Ld[t].at[me], send_sem=send_sems.at[3 * t + j],
                    recv_sem=recv_sems.at[3 * t + j], device_id=(px, py, c), device_id_type=MESH).start()
        token[...] = jnp.zeros_like(token)

    ins = list(grads) + list(landings)
    sems = [pltpu.SemaphoreType.DMA((3 * n,))] * 2
    outs = pl.pallas_call(
        body, name="ga_start_" + tag, out_shape=sems + [_hbm_like(a) for a in ins] + [_sds((8, 128))],
        in_specs=[HBM] * len(ins), out_specs=[SEM, SEM] + [HBM] * len(ins) + [VM],
        input_output_aliases={i: 2 + i for i in range(len(ins))},
        compiler_params=pltpu.CompilerParams(has_side_effects=EFFECT))(*[_hbm(a) for a in ins])
    return outs[0], outs[1], outs[2:2 + n], outs[2 + n:2 + 2 * n], outs[-1]


def ga_wait(tag, ks, send_sems, recv_sems, grads, landings, after):
    n = len(ks)

    def body(*refs):
        G, Ld, ss, rs = refs[:n], refs[n:2 * n], refs[2 * n], refs[2 * n + 1]
        x, y, c = _pos()
        me = 2 * x + y
        for t, k in enumerate(ks):
            for j, (px, py) in enumerate(_other_chips(x, y)):
                pj = 2 * px + py
                cp = pltpu.make_async_remote_copy(
                    src_ref=_slab(k, G[t], pj), dst_ref=Ld[t].at[pj], send_sem=ss.at[3 * t + j],
                    recv_sem=rs.at[3 * t + j], device_id=(px, py, c), device_id_type=MESH)
                cp.wait_send()
                cp.wait_recv()

    ins = list(grads) + list(landings) + [send_sems, recv_sems, after]
    res = pl.pallas_call(
        body, name="ga_wait_" + tag, out_shape=[_hbm_like(a) for a in list(grads) + list(landings)],
        in_specs=[HBM] * (2 * n) + [SEM, SEM, pl.BlockSpec(memory_space=pl.ANY)], out_specs=[HBM] * (2 * n),
        input_output_aliases={i: i for i in range(2 * n)},
        compiler_params=pltpu.CompilerParams(has_side_effects=EFFECT))(*ins)
    return res[:n], res[n:]


def swap_start(parts):
    n = len(parts)

    def body(*refs):
        Q, Ld, sems = refs[:n], refs[n:2 * n], refs[2 * n:4 * n]
        x, y, c = _pos()
        for k in range(n):
            pltpu.make_async_remote_copy(src_ref=Q[k], dst_ref=Ld[k], send_sem=sems[k].at[0], recv_sem=sems[n + k].at[0],
                                         device_id=(x, y, 1 - c), device_id_type=MESH).start()
        refs[-1][...] = jnp.zeros_like(refs[-1])

    ins = list(parts) + [lax.empty(p.shape, p.dtype) for p in parts]
    outs = pl.pallas_call(
        body, name="swap_start",
        out_shape=[pltpu.SemaphoreType.DMA((1,))] * (2 * n) + [_hbm_like(a) for a in ins] + [_sds((8, 128))],
        in_specs=[HBM] * (2 * n), out_specs=[SEM] * (2 * n) + [HBM] * (2 * n) + [VM],
        input_output_aliases={i: 2 * n + i for i in range(2 * n)},
        compiler_params=pltpu.CompilerParams(has_side_effects=EFFECT))(*[_hbm(a) for a in ins])
    return outs[:n], outs[n:2 * n], outs[2 * n:3 * n], outs[3 * n:4 * n]


def swap_wait(k, send_sem, recv_sem, part, landing, after):
    def body(q_ref, l_ref, ss, rs, after_ref, q_out, l_out):
        x, y, c = _pos()
        cp = pltpu.make_async_remote_copy(src_ref=q_ref, dst_ref=l_ref, send_sem=ss.at[0], recv_sem=rs.at[0],
                                          device_id=(x, y, 1 - c), device_id_type=MESH)
        cp.wait_send()
        cp.wait_recv()

    return pl.pallas_call(
        body, name="swap_wait_%d" % k, out_shape=[_hbm_like(part), _hbm_like(landing)],
        in_specs=[HBM, HBM, SEM, SEM, pl.BlockSpec(memory_space=pl.ANY)], out_specs=[HBM, HBM],
        input_output_aliases={0: 0, 1: 1},
        compiler_params=pltpu.CompilerParams(has_side_effects=EFFECT))(part, landing, send_sem, recv_sem, after)


def _peer(x, y, c, r):
    fx, fy, fc = (r >> 2) & 1, (r >> 1) & 1, r & 1
    return ((1 - x) if fx else x, (1 - y) if fy else y, (1 - c) if fc else c)


def ag_start(tag, pack):
    rr, cc = pack.shape

    def body(p_ref, l_ref, ss, rs, p_out, l_out, token):
        x, y, c = _pos()
        me = 4 * x + 2 * y + c
        for r in range(1, 8):
            pltpu.make_async_remote_copy(src_ref=p_ref, dst_ref=l_ref.at[me], send_sem=ss.at[r - 1], recv_sem=rs.at[r - 1],
                                         device_id=_peer(x, y, c, r), device_id_type=MESH).start()
        token[...] = jnp.zeros_like(token)

    outs = pl.pallas_call(
        body, name="ag_start_" + tag,
        out_shape=[pltpu.SemaphoreType.DMA((7,)), pltpu.SemaphoreType.DMA((7,)), _hbm_like(pack),
                   pltpu.HBM((8, rr, cc), pack.dtype), _sds((8, 128))],
        in_specs=[HBM, HBM], out_specs=[SEM, SEM, HBM, HBM, VM], input_output_aliases={0: 2, 1: 3},
        compiler_params=pltpu.CompilerParams(has_side_effects=EFFECT))(_hbm(pack), _hbm(lax.empty((8, rr, cc), pack.dtype)))
    return outs


def ag_wait(tag, send_sems, recv_sems, pack, landing, after):
    def body(p_ref, l_ref, ss, rs, after_ref, p_out, l_out):
        x, y, c = _pos()
        for r in range(1, 8):
            px, py, pc = _peer(x, y, c, r)
            cp = pltpu.make_async_remote_copy(src_ref=p_ref, dst_ref=l_ref.at[4 * px + 2 * py + pc], send_sem=ss.at[r - 1],
                                              recv_sem=rs.at[r - 1], device_id=(px, py, pc), device_id_type=MESH)
            cp.wait_send()
            cp.wait_recv()

    return pl.pallas_call(
        body, name="ag_wait_" + tag, out_shape=[_hbm_like(pack), _hbm_like(landing)],
        in_specs=[HBM, HBM, SEM, SEM, pl.BlockSpec(memory_space=pl.ANY)], out_specs=[HBM, HBM],
        input_output_aliases={0: 0, 1: 1},
        compiler_params=pltpu.CompilerParams(has_side_effects=EFFECT))(pack, landing, send_sems, recv_sems, after)


def sum_slots(landing, own, mevec):
    _, rr, cc = landing.shape

    def body(m_ref, l_ref, o_ref, out_ref):
        mine = o_ref[...]
        acc = None
        for j in range(8):
            part = jnp.where(m_ref[0] == j, mine, l_ref[j])
            acc = part if acc is None else acc + part
        out_ref[...] = acc

    gs = pltpu.PrefetchScalarGridSpec(
        num_scalar_prefetch=1, grid=(1,),
        in_specs=[pl.BlockSpec((8, rr, cc), lambda i, m: (0, 0, 0)), pl.BlockSpec((rr, cc), lambda i, m: (0, 0))],
        out_specs=pl.BlockSpec((rr, cc), lambda i, m: (0, 0)))
    return pl.pallas_call(body, name="sum_slots", grid_spec=gs, out_shape=_sds((rr, cc)),
                          compiler_params=_cparams(1))(mevec, landing, own)


_WEIGHTS = ['g_mix', 'w_in', 'w_s', 'b_s', 'ln_g', 'ln_b', 'w_gate_f', 'b_gate_f', 'w_gate_b', 'b_gate_b', 'g_gla',
            'w_out', 'g_ffn', 'w_up', 'conv_w', 'conv_b', 'w_down', 'g_final']
_BIG = ['w_in', 'w_out', 'w_up', 'w_down']
_SMALL = [n for n in _WEIGHTS if n not in _BIG]
_SMALL_SHARDED = {'w_gate_f': 64, 'w_gate_b': 64, 'conv_w': 1408}
_BIG_TR = {'w_in': 512, 'w_out': 256, 'w_up': 256, 'w_down': 352}


def _pack(arrs):
    flat = jnp.concatenate([a.reshape(-1) for a in arrs])
    pad = (-flat.shape[0]) % 1024
    return jnp.pad(flat, (0, pad)).reshape(-1, 128)


def _unpack(buf, shapes):
    flat = buf.reshape(-1)
    out, o = [], 0
    for s in shapes:
        n = 1
        for d in s:
            n *= d
        out.append(flat[o:o + n].reshape(s))
        o += n
    return out


def kernel(x, g_mix, w_in, w_s, b_s, ln_g, ln_b, w_gate_f, b_gate_f, w_gate_b, b_gate_b, g_gla, w_out, g_ffn, w_up, conv_w, conv_b, w_down, g_final, loss_target, m_g_mix, m_w_in, m_w_s, m_b_s, m_ln_g, m_ln_b, m_w_gate_f, m_b_gate_f, m_w_gate_b, m_b_gate_b, m_g_gla, m_w_out, m_g_ffn, m_w_up, m_conv_w, m_conv_b, m_w_down, m_g_final, v_g_mix, v_w_in, v_w_s, v_b_s, v_ln_g, v_ln_b, v_w_gate_f, v_b_gate_f, v_w_gate_b, v_b_gate_b, v_g_gla, v_w_out, v_g_ffn, v_w_up, v_conv_w, v_conv_b, v_w_down, v_g_final):
    loc = locals()
    w = {n: loc[n] for n in _WEIGHTS}
    m = {n: loc["m_" + n] for n in _WEIGHTS}
    v = {n: loc["v_" + n] for n in _WEIGHTS}
    xi, yi, ci = _pos()
    chip = 2 * xi + yi
    me = 2 * chip + ci
    mevec = jnp.reshape(me, (1,)).astype(jnp.int32)

    sh_names = list(_SMALL_SHARDED)
    ss_w, rs_w, pk_w, land_w, tok_w = ag_start("w", _pack([w[n] for n in sh_names]))

    shards = [cast_bf16(w[n], _BIG_TR[n]) for n in _BIG]
    chipvec = jnp.reshape(chip, (1,)).astype(jnp.int32)
    send_sems, recv_sems, shards_fly, landings_fly, started = gw_start(
        shards, [lax.empty(_LAYER_FULL[k], BF16) for _, k in _GW_ORDER], tok_w)
    own = {"shards": shards_fly}

    pk_w, land_w = ag_wait("w", ss_w, rs_w, pk_w, land_w, started)
    per_chip = [_unpack(jnp.where(me == 2 * j, pk_w, land_w[2 * j]), [w[n].shape for n in sh_names]) for j in range(4)]
    W = dict(w)
    for k, n in enumerate(sh_names):
        W[n] = jnp.concatenate([per_chip[j][k] for j in range(4)], axis=-1)
    arrived = {}

    def get_big(l, stage, after):
        gi = {(0, "in"): 0, (0, "rest"): 1}.get((l, stage), l + 1 if stage == "in" else None)
        if gi is not None:
            lo = sum(len(g) for g in _GW_GROUPS[:gi])
            lands = landings_fly[lo:lo + len(_GW_GROUPS[gi])]
            if gi == len(_GW_GROUPS) - 1:
                full, own["shards"] = gw_wait(gi, lands, recv_sems[gi], after, shards_fly, send_sems)
            else:
                full, _ = gw_wait(gi, lands, recv_sems[gi], after)
            for (gl, gk), a in zip(_GW_GROUPS[gi], full):
                arrived[(gl, gk)] = place_own(own["shards"][gk], a, gl, gk, chipvec, _BIG_TR[_BIG[gk]])
        if stage == "in":
            f_in = jnp.transpose(arrived[(l, 0)], (1, 0, 2)).reshape(D, N_IN)
            return {"w_in": jnp.pad(f_in, ((0, 0), (0, N_INP - N_IN)))}
        return {"w_out": arrived[(l, 1)], "w_up": arrived[(l, 2)], "w_down": arrived[(l, 3)]}

    flying = []

    def emit(l, group, grads):
        ks = [3, 2] if group == "A" else [1, 0]
        gs = [grads[_BIG[k]] for k in ks]
        if group == "B":
            gs[1] = jnp.transpose(gs[1][:, :N_IN].reshape(D, 4, 648), (1, 0, 2))
        lands = [lax.empty((4,) + _LAYER_SHARD[k], BF16) for k in ks]
        tag = "%d%s" % (l, group)
        ss, rs, gs_fly, lands_fly, tok = ga_start(tag, ks, gs, lands)
        flying.append((tag, l, ks, ss, rs, gs_fly, lands_fly))
        return tok[0:1, 0:1]

    lsum, grad_x, G = local_step(x[0], loss_target[0], W, get_big, emit)

    small_shapes = [G[n].shape for n in _SMALL] + [(D,)]
    ss_g, rs_g, pk_g, land_g, started = ag_start("g", _pack([G[n] for n in _SMALL] + [lsum]))

    plane = [[None] * NL for _ in range(N_BIG)]
    for tag, l, ks, ss, rs, gs_fly, lands_fly in flying:
        for k, g, a in zip(ks, *ga_wait(tag, ks, ss, rs, gs_fly, lands_fly, started)):
            plane[k][l] = sum_parts(a, g, k, chipvec, _BIG_TR[_BIG[k]])
    ss_p, rs_p, plane_fly, other_fly = swap_start([jnp.stack(p) for p in plane])
    grads, delta, new_m, new_v = {}, {}, {}, {}
    after = grad_x
    for k in (1, 0, 3, 2):
        n = _BIG[k]
        mine, other = swap_wait(k, ss_p[k], rs_p[k], plane_fly[k], other_fly[k], after)
        grads[n], delta[n], new_m[n], new_v[n] = adamw(w[n], mine, other, m[n], v[n], _BIG_TR[n])
        after = delta[n]

    pk_g, land_g = ag_wait("g", ss_g, rs_g, pk_g, land_g, after)
    small = dict(zip(_SMALL + ["lsum"], _unpack(sum_slots(land_g, pk_g, mevec), small_shapes)))
    loss = 0.5 * jnp.sum(small.pop("lsum")) / D
    for n, wd in _SMALL_SHARDED.items():
        small[n] = lax.dynamic_slice_in_dim(small[n], chip * wd, wd, axis=small[n].ndim - 1)
    grads.update(small)
    shapes = [w[n].shape for n in _SMALL]
    pw, pg, pm, pv = (_pack([t[n] for n in _SMALL])[None] for t in (w, grads, m, v))
    _, d_, m_, v_ = adamw(pw, pg, jnp.zeros_like(pg), pm, pv, pw.shape[1])
    for t, buf in ((delta, d_), (new_m, m_), (new_v, v_)):
        t.update(zip(_SMALL, _unpack(buf, shapes)))

    return (loss, grad_x[None], *[grads[n] for n in _WEIGHTS], *[delta[n] for n in _WEIGHTS],
            *[new_m[n] for n in _WEIGHTS], *[new_v[n] for n in _WEIGHTS])
```

```python
import functools

import jax
import jax.numpy as jnp
from jax import lax
from jax.experimental import pallas as pl
from jax.experimental.pallas import tpu as pltpu

F32 = jnp.float32
BF16 = jnp.bfloat16
MX = BF16

D = 1024
CH = 128
NL = 4
N_IN = 2592
N_INP = 2688
NUP = 5632
DFF = 2816
EPS = 1e-6
VMEM_LIMIT = 56 * 1024 * 1024

ADAM_LR, ADAM_B1, ADAM_B2, ADAM_EPS, ADAM_WD, ADAM_STEP = 0.001, 0.9, 0.999, 1e-08, 0.01, 10


def _dg(a, b, ca, cb):
    return lax.dot_general(a.astype(MX), b.astype(MX), (((ca,), (cb,)), ((), ())), preferred_element_type=F32)


@jax.custom_vjp
def mm(a, b):
    return _dg(a, b, 1, 0)


mm.defvjp(lambda a, b: (_dg(a, b, 1, 0), (a, b)),
          lambda r, g: (_dg(g, r[1], 1, 1), _dg(r[0], g, 0, 0)))


@jax.custom_vjp
def mm_nt(a, b):
    return _dg(a, b, 1, 1)


mm_nt.defvjp(lambda a, b: (_dg(a, b, 1, 1), (a, b)),
             lambda r, g: (_dg(g, r[1], 1, 0), _dg(g, r[0], 0, 0)))


@jax.custom_vjp
def mm_tn(a, b):
    return _dg(a, b, 0, 0)


mm_tn.defvjp(lambda a, b: (_dg(a, b, 0, 0), (a, b)),
             lambda r, g: (_dg(r[1], g, 1, 1), _dg(r[0], g, 1, 0)))


def _split3(x):
    hi = x.astype(BF16)
    r1 = x - hi.astype(F32)
    mid = r1.astype(BF16)
    lo = (r1 - mid.astype(F32)).astype(BF16)
    return hi, mid, lo


def _dot3(m, x):
    hi, mid, lo = _split3(x)
    d = lambda p: lax.dot_general(m, p, (((1,), (0,)), ((), ())), preferred_element_type=F32)
    return d(hi) + d(mid) + d(lo)


@jax.custom_vjp
def cumdot(m, mt, x):
    return _dot3(m, x)


cumdot.defvjp(lambda m, mt, x: (_dot3(m, x), (m, mt)),
              lambda r, g: (jnp.zeros_like(r[0]), jnp.zeros_like(r[1]), _dot3(r[1], g)))


def rmsnorm(x, g):
    return x * lax.rsqrt(jnp.mean(x * x, axis=-1, keepdims=True) + EPS) * g


def gelu(x):
    return 0.5 * x * (1.0 + lax.erf(x * 0.7071067811865476))


def sigmoid(x):
    return 1.0 / (1.0 + jnp.exp(-x))


def log_sigmoid(x):
    return jnp.minimum(x, 0.0) - jnp.log(1.0 + jnp.exp(-jnp.abs(x)))


def gmlp_heads(params, pieces):
    u = [[gelu(p[0]) for p in ch] for ch in pieces]
    v = [[gelu(p[1]) for p in ch] for ch in pieces]
    mu = [[jnp.mean(x, axis=-1, keepdims=True) for x in ch] for ch in v]
    var = [[jnp.mean(jnp.square(x - m), axis=-1, keepdims=True) for x, m in zip(cv, cm)] for cv, cm in zip(v, mu)]
    vn = [[(x - m) * lax.rsqrt(s + EPS) * pr[2] + pr[3] for x, m, s, pr in zip(cv, cm, cs, params)]
          for cv, cm, cs in zip(v, mu, var)]
    mix = [[mm(pr[0], x) + pr[1] for x, pr in zip(ch, params)] for ch in vn]
    return [[a * b for a, b in zip(cu, cx)] for cu, cx in zip(u, mix)]


def outb_head(o, pg, g):
    return rmsnorm(o, g) * (pg * sigmoid(pg))


def ffn_act(zg, zv):
    return zg * sigmoid(zg) * zv


def _tri(reverse):
    r = lax.broadcasted_iota(jnp.int32, (CH, CH), 0)
    c = lax.broadcasted_iota(jnp.int32, (CH, CH), 1)
    if reverse:
        cm, sm = c >= r, c > r
    else:
        cm, sm = c <= r, c <= r
    one = jnp.ones((), BF16)
    zero = jnp.zeros((), BF16)
    return jnp.where(cm, one, zero), jnp.where(cm.T, one, zero), sm


def gla_pair(consts, wg, bg, st0, st1, *chunks):
    m, mt, smask, lm0, lm1 = consts
    ch = [chunks[5 * i:5 * i + 5] for i in range(len(chunks) // 5)]
    la = [log_sigmoid(mm(c[0], wg) + bg) * (1.0 / 16.0) for c in ch]
    cum = [cumdot(m, mt, x) for x in la]
    tot = [jnp.sum(x, axis=0, keepdims=True) for x in la]
    q_dec = [(c[1] * 0.125) * jnp.exp(cm) for c, cm in zip(ch, cum)]
    k_inv = [c[2] * jnp.exp(-cm) for c, cm in zip(ch, cum)]
    k_end = [c[2] * jnp.exp(t - cm) for c, t, cm in zip(ch, tot, cum)]
    s = [[jnp.where(smask, mm_nt(qd * lm, ki), 0.0) for lm in (lm0, lm1)] for qd, ki in zip(q_dec, k_inv)]
    o_in = [[mm(si[h], c[3 + h]) for h in (0, 1)] for si, c in zip(s, ch)]
    ds = [[mm_tn(c[3 + h], ke * lm) for h, lm in ((0, lm0), (1, lm1))] for c, ke in zip(ch, k_end)]
    sts = [(st0, st1)]
    for t, d in zip(tot, ds):
        dec = jnp.exp(t)
        sts.append((sts[-1][0] * dec + d[0], sts[-1][1] * dec + d[1]))
    outs = []
    for qd, oi, st in zip(q_dec, o_in, sts):
        outs += [oi[0] + mm_nt(qd, st[0]), oi[1] + mm_nt(qd, st[1])]
    return (*outs, sts[-1][0], sts[-1][1])


def _lane_masks():
    lane = lax.broadcasted_iota(jnp.int32, (1, 128), 1)
    return (lane < 64).astype(F32), (lane >= 64).astype(F32)


def _cparams(n_axes=1):
    return pltpu.CompilerParams(dimension_semantics=("arbitrary",) * n_axes, vmem_limit_bytes=VMEM_LIMIT)


def _full(a):
    nd = a.ndim
    return pl.BlockSpec(a.shape, lambda *_: (0,) * nd)


def _rows(tm, w, cb=0, rev_n=None):
    if rev_n is None:
        return pl.BlockSpec((tm, w), lambda i: (i, cb))
    return pl.BlockSpec((tm, w), lambda i: (rev_n - 1 - i, cb))


def _call(body, name, grid, in_specs, out_specs, out_shape, scratch=(), n_axes=1):
    return pl.pallas_call(body, name=name, grid=grid, in_specs=in_specs, out_specs=out_specs, out_shape=out_shape,
                          scratch_shapes=list(scratch), compiler_params=_cparams(n_axes))


def _sds(shape, dt=F32):
    return jax.ShapeDtypeStruct(shape, dt)


def norm_matmul(x, g, w, tm, name, ydt=F32):
    T, n = x.shape[0], w.shape[1]

    def body(x_ref, g_ref, w_ref, y_ref, h_ref):
        hb = rmsnorm(x_ref[...], g_ref[...]).astype(MX)
        h_ref[...] = hb
        y_ref[...] = jnp.dot(hb, w_ref[...], preferred_element_type=F32).astype(ydt)

    return _call(body, name, (T // tm,), [_rows(tm, D), _full(g), _full(w)],
                 [_rows(tm, n), _rows(tm, D)], [_sds((T, n), ydt), _sds((T, D), MX)])(x, g, w)


CPB = 8


def _chunk(c):
    return slice(c * CH, (c + 1) * CH)


def gmlp_fwd(p, ws, bs, lg, lb):
    T = p.shape[0]
    tm = CPB * CH

    def body(pa_ref, ws_ref, bs_ref, lg_ref, lb_ref, o_ref):
        params = [(ws_ref[h], bs_ref[h], lg_ref[h], lb_ref[h]) for h in range(4)]
        pieces = [[(pa_ref[_chunk(c), h * 128:(h + 1) * 128], pa_ref[_chunk(c), 512 + h * 128:512 + (h + 1) * 128])
                   for h in range(4)] for c in range(CPB)]
        out = gmlp_heads(params, pieces)
        for c in range(CPB):
            for h in range(4):
                o_ref[_chunk(c), h * 128:(h + 1) * 128] = out[c][h].astype(MX)

    return _call(body, "gmlp_fwd", (T // tm,), [_rows(tm, 1024), _full(ws), _full(bs), _full(lg), _full(lb)],
                 _rows(tm, 512), _sds((T, 512), MX))(p, ws, bs, lg, lb)


def _gla_in_specs(tm, n, rev):
    r = n if rev else None
    return [_rows(tm, 256, 4, r), _rows(tm, 256, 5, r), _rows(tm, 512, 3, r), _rows(tm, 128, 20, r)]


def gla_fwd(p, wg, bg, reverse):
    T = p.shape[0]
    tm = CPB * CH
    n = T // tm
    rev = n if reverse else None

    def body(q_ref, k_ref, v_ref, r_ref, wg_ref, bg_ref, o_ref, ss_ref, st_ref):
        @pl.when(pl.program_id(0) == 0)
        def _():
            st_ref[...] = jnp.zeros_like(st_ref)

        consts = _tri(reverse) + _lane_masks()
        order = list(reversed(range(CPB))) if reverse else list(range(CPB))
        ss_ref[0] = st_ref[...]
        for j in range(2):
            sl = slice(j * 128, (j + 1) * 128)
            v0s, v1s = slice(256 * j, 256 * j + 128), slice(256 * j + 128, 256 * j + 256)
            chunks = []
            for c in order:
                rows = _chunk(c)
                chunks += [r_ref[rows, :], q_ref[rows, sl], k_ref[rows, sl], v_ref[rows, v0s], v_ref[rows, v1s]]
            res = gla_pair(consts, wg_ref[:, sl], bg_ref[:, sl], st_ref[2 * j], st_ref[2 * j + 1], *chunks)
            for i, c in enumerate(order):
                o_ref[_chunk(c), v0s] = res[2 * i]
                o_ref[_chunk(c), v1s] = res[2 * i + 1]
            st_ref[2 * j] = res[-2]
            st_ref[2 * j + 1] = res[-1]

    ss_spec = pl.BlockSpec((1, 4, 128, 128), (lambda i: (n - 1 - i, 0, 0, 0)) if reverse else (lambda i: (i, 0, 0, 0)))
    return _call(body, "gla_fwd_r" if reverse else "gla_fwd_f", (n,),
                 _gla_in_specs(tm, n, reverse) + [_full(wg), _full(bg)],
                 [_rows(tm, 512, 0, rev), ss_spec], [_sds((T, 512)), _sds((n, 4, 128, 128))],
                 scratch=[pltpu.VMEM((4, 128, 128), F32)])(p, p, p, p, wg, bg)


def mix_out(x, of, ob, p, outa, gg, w_out, tm):
    T = x.shape[0]

    def body(x_ref, of_ref, ob_ref, pg_ref, oa_ref, gg_ref, w_ref, x1_ref, mx_ref):
        mx_ref[:, 0:512] = oa_ref[...]
        for h in range(4):
            sl = slice(h * 128, (h + 1) * 128)
            mx_ref[:, 512 + h * 128:512 + (h + 1) * 128] = outb_head(
                of_ref[:, sl] + ob_ref[:, sl], pg_ref[:, sl], gg_ref[h]).astype(MX)
        x1_ref[...] = x_ref[...] + jnp.dot(mx_ref[...], w_ref[...], preferred_element_type=F32)

    return _call(body, "mix_out", (T // tm,),
                 [_rows(tm, D), _rows(tm, 512), _rows(tm, 512), _rows(tm, 512, 4), _rows(tm, 512), _full(gg), _full(w_out)],
                 [_rows(tm, D), _rows(tm, 1024)], [_sds((T, D)), _sds((T, 1024), MX)])(x, of, ob, p, outa, gg, w_out)


HALO = 16


def _halo_specs(T, tm, w):
    nb = T // HALO
    r = tm // HALO
    return [pl.BlockSpec((tm, w), lambda i: (i, 0)),
            pl.BlockSpec((HALO, w), lambda i: (jnp.maximum(i * r - 1, 0), 0)),
            pl.BlockSpec((HALO, w), lambda i: (jnp.minimum((i + 1) * r, nb - 1), 0))]


def ffn_up_conv(x1, g, w_up, cw, cb, tm):
    T = x1.shape[0]
    ns = T // tm
    cwid = 256

    def body(x_ref, g_ref, w_ref, cw_ref, cb_ref, zu_ref, h_ref, z_ref, a_ref, prev_ref, tail_ref):
        i = pl.program_id(0)

        @pl.when(i == 0)
        def _():
            prev_ref[...] = jnp.zeros_like(prev_ref)
            tail_ref[...] = jnp.zeros_like(tail_ref)

        hb = rmsnorm(x_ref[...], g_ref[...]).astype(MX)
        h_ref[...] = hb
        row = lax.broadcasted_iota(jnp.int32, (tm, 1), 0)
        for c0 in range(0, DFF, cwid):
            z2 = []
            for cs in (slice(c0, c0 + cwid), slice(DFF + c0, DFF + c0 + cwid)):
                zub = jnp.dot(hb, w_ref[:, cs], preferred_element_type=F32).astype(MX)
                zu_ref[:, cs] = zub
                prev = prev_ref[:, cs].astype(F32)
                pr = tail_ref[HALO - 1:HALO, cs].astype(F32)
                nx = jnp.where(i < ns, zub[0:1, :].astype(F32), 0.0)
                dn = jnp.where(row == 0, pr, pltpu.roll(prev, 1, 0))
                up = jnp.where(row == tm - 1, nx, pltpu.roll(prev, tm - 1, 0))
                z = cb_ref[:, cs] + dn * cw_ref[0:1, cs] + prev * cw_ref[1:2, cs] + up * cw_ref[2:3, cs]
                z_ref[:, cs] = z.astype(MX)
                tail_ref[:, cs] = prev_ref[tm - HALO:tm, cs]
                prev_ref[:, cs] = zub
                z2.append(z)
            a_ref[:, c0:c0 + cwid] = ffn_act(z2[0], z2[1]).astype(MX)

    cur = lambda w: pl.BlockSpec((tm, w), lambda i: (jnp.minimum(i, ns - 1), 0))
    late = lambda w: pl.BlockSpec((tm, w), lambda i: (jnp.maximum(i - 1, 0), 0))
    return _call(body, "ffn_up", (ns + 1,), [cur(D), _full(g), _full(w_up), _full(cw), _full(cb)],
                 [cur(NUP), cur(D), late(NUP), late(DFF)],
                 [_sds((T, NUP), MX), _sds((T, D), MX), _sds((T, NUP), MX), _sds((T, DFF), MX)],
                 scratch=[pltpu.VMEM((tm, NUP), MX), pltpu.VMEM((HALO, NUP), MX)])(x1, g, w_up, cw, cb)


def matmul_res(a, w, res, tm, name):
    T, k = a.shape
    n = w.shape[1]

    def body(a_ref, w_ref, r_ref, o_ref):
        o_ref[...] = r_ref[...] + jnp.dot(a_ref[...], w_ref[...], preferred_element_type=F32)

    return _call(body, name, (T // tm,), [_rows(tm, k), _full(w), _rows(tm, n)], _rows(tm, n), _sds((T, n)))(a, w, res)


def loss_head(x, g, tgt, tm):
    T = x.shape[0]

    def body(x_ref, g_ref, t_ref, l_ref, dx_ref, dg_ref):
        @pl.when(pl.program_id(0) == 0)
        def _():
            l_ref[...] = jnp.zeros_like(l_ref)
            dg_ref[...] = jnp.zeros_like(dg_ref)

        y, vjp = jax.vjp(rmsnorm, x_ref[...], g_ref[...])
        err = y - t_ref[...]
        l_ref[...] += jnp.sum(err * err, axis=0, keepdims=True)
        dx, dg = vjp(err * (1.0 / D))
        dx_ref[...] = dx
        dg_ref[...] += dg

    return _call(body, "loss_head", (T // tm,), [_rows(tm, D), _full(g), _rows(tm, D)],
                 [_full(g), _rows(tm, D), _full(g)], [_sds((1, D)), _sds((T, D)), _sds((1, D))])(x, g, tgt)


def ffn_down_bwd(dx2, z, w_down, tm):
    T = dx2.shape[0]

    def body(dx_ref, z_ref, w_ref, dz_ref):
        da = _dg(dx_ref[...], w_ref[...], 1, 1)
        zg, zv = z_ref[:, :DFF].astype(F32), z_ref[:, DFF:].astype(F32)
        s = sigmoid(zg)
        sz = zg * s
        dz_ref[:, :DFF] = (da * zv * (s + sz * (1.0 - s))).astype(MX)
        dz_ref[:, DFF:] = (da * sz).astype(MX)

    return _call(body, "ffn_down_bwd", (T // tm,), [_rows(tm, D), _rows(tm, NUP), _full(w_down)],
                 _rows(tm, NUP), _sds((T, NUP), MX))(dx2, z, w_down)


def ffn_up_bwd(dz, zu, cw, w_up, x1, g, dres, tm):
    T = dz.shape[0]
    ns = T // tm
    cwid = 512

    def body(dz_ref, dp_ref, dn_ref, zu_ref, cw_ref, w_ref, x_ref, g_ref, dr_ref,
             dzu_ref, dx_ref, dg_ref, dcw_ref, dcb_ref):
        i = pl.program_id(0)

        @pl.when(i == 0)
        def _():
            for r in (dg_ref, dcw_ref, dcb_ref):
                r[...] = jnp.zeros_like(r)

        row = lax.broadcasted_iota(jnp.int32, (tm, 1), 0)
        dh = jnp.zeros((tm, D), F32)
        for c0 in range(0, NUP, cwid):
            cs = slice(c0, c0 + cwid)
            dz = dz_ref[:, cs].astype(F32)
            zu = zu_ref[:, cs].astype(F32)
            pr = jnp.where(i > 0, dp_ref[HALO - 1:HALO, cs].astype(F32), 0.0)
            nx = jnp.where(i < ns - 1, dn_ref[0:1, cs].astype(F32), 0.0)
            ddn = jnp.where(row == 0, pr, pltpu.roll(dz, 1, 0))
            dup = jnp.where(row == tm - 1, nx, pltpu.roll(dz, tm - 1, 0))
            dzu = (dup * cw_ref[0:1, cs] + dz * cw_ref[1:2, cs] + ddn * cw_ref[2:3, cs]).astype(MX)
            dzu_ref[:, cs] = dzu
            dcw_ref[0:1, cs] += jnp.sum(zu * dup, axis=0, keepdims=True)
            dcw_ref[1:2, cs] += jnp.sum(zu * dz, axis=0, keepdims=True)
            dcw_ref[2:3, cs] += jnp.sum(zu * ddn, axis=0, keepdims=True)
            dcb_ref[:, cs] += jnp.sum(dz, axis=0, keepdims=True)
            dh = dh + _dg(dzu, w_ref[:, cs], 1, 1)
        _, vjp = jax.vjp(rmsnorm, x_ref[...], g_ref[...])
        dx, dg = vjp(dh)
        dx_ref[...] = dr_ref[...] + dx
        dg_ref[...] += dg

    return _call(body, "ffn_up_bwd", (ns,),
                 _halo_specs(T, tm, NUP) + [_rows(tm, NUP), _full(cw), _full(w_up), _rows(tm, D), _full(g), _rows(tm, D)],
                 [_rows(tm, NUP), _rows(tm, D), _full(g), _full(cw), pl.BlockSpec((1, NUP), lambda i: (0, 0))],
                 [_sds((T, NUP), MX), _sds((T, D)), _sds((1, D)), _sds((3, NUP)), _sds((1, NUP))])(
                     dz, dz, dz, zu, cw, w_up, x1, g, dres)


def nt_normbwd(dys, w, x, g, dres, tm, name):
    T = x.shape[0]
    n = len(dys)
    offs = [sum(d.shape[1] for d in dys[:i]) for i in range(n + 1)]

    def body(*refs):
        dy_refs, (w_ref, x_ref, g_ref, dr_ref, dx_ref, dg_ref) = refs[:n], refs[n:]

        @pl.when(pl.program_id(0) == 0)
        def _():
            dg_ref[...] = jnp.zeros_like(dg_ref)

        dh = _dg(dy_refs[0][...], w_ref[:, offs[0]:offs[1]], 1, 1)
        for i in range(1, n):
            dh = dh + _dg(dy_refs[i][...], w_ref[:, offs[i]:offs[i + 1]], 1, 1)
        _, vjp = jax.vjp(rmsnorm, x_ref[...], g_ref[...])
        dx, dg = vjp(dh)
        dx_ref[...] = dr_ref[...] + dx
        dg_ref[...] += dg

    return _call(body, name, (T // tm,),
                 [_rows(tm, d.shape[1]) for d in dys] + [_full(w), _rows(tm, D), _full(g), _rows(tm, D)],
                 [_rows(tm, D), _full(g)], [_sds((T, D)), _sds((1, D))])(*dys, w, x, g, dres)


def matmul_tn(a, b, tt, tn, name):
    T, k = a.shape
    n = b.shape[1]
    last = T // tt - 1

    def body(a_ref, b_ref, o_ref, acc_ref):
        @pl.when(pl.program_id(1) == 0)
        def _():
            acc_ref[...] = jnp.zeros_like(acc_ref)

        acc_ref[...] += _dg(a_ref[...], b_ref[...], 0, 0)

        @pl.when(pl.program_id(1) == last)
        def _():
            o_ref[...] = acc_ref[...].astype(MX)

    return _call(body, name, (n // tn, T // tt),
                 [pl.BlockSpec((tt, k), lambda j, i: (i, 0)), pl.BlockSpec((tt, tn), lambda j, i: (i, j))],
                 pl.BlockSpec((k, tn), lambda j, i: (0, j)), _sds((k, n), MX), scratch=[pltpu.VMEM((k, tn), F32)],
                 n_axes=2)(a, b)


def mix_out_bwd(dx1, w_out, of, ob, p, gg, tm):
    T = dx1.shape[0]

    def body(dx_ref, w_ref, of_ref, ob_ref, pg_ref, gg_ref, da_ref, do_ref, dpg_ref, dgg_ref):
        @pl.when(pl.program_id(0) == 0)
        def _():
            dgg_ref[...] = jnp.zeros_like(dgg_ref)

        dxb = dx_ref[...].astype(MX)
        da_ref[...] = _dg(dxb, w_ref[0:512, :], 1, 1)
        for h in range(4):
            sl = slice(h * 128, (h + 1) * 128)
            dm = _dg(dxb, w_ref[512 + h * 128:512 + (h + 1) * 128, :], 1, 1)
            _, vjp = jax.vjp(outb_head, of_ref[:, sl] + ob_ref[:, sl], pg_ref[:, sl], gg_ref[h])
            do, dpg, dg = vjp(dm)
            do_ref[:, sl] = do
            dpg_ref[:, sl] = dpg
            dgg_ref[h] += dg

    return _call(body, "mix_out_bwd", (T // tm,),
                 [_rows(tm, D), _full(w_out), _rows(tm, 512), _rows(tm, 512), _rows(tm, 512, 4), _full(gg)],
                 [_rows(tm, 512), _rows(tm, 512), _rows(tm, 512), _full(gg)],
                 [_sds((T, 512)), _sds((T, 512)), _sds((T, 512)), _sds(gg.shape)])(dx1, w_out, of, ob, p, gg)


def gla_bwd(p, wg, bg, ss, do, reverse, merge=None):
    T = p.shape[0]
    tm = CPB * CH
    n = T // tm
    rev = not reverse
    rn = n if rev else None

    def body(*refs):
        q_ref, k_ref, v_ref, r_ref, wg_ref, bg_ref, ss_ref, do_ref = refs[:8]
        if merge is None:
            dq_ref, dk_ref, dv_ref, dr_ref, dwg_ref, dbg_ref, dst_ref = refs[8:]
        else:
            mq_ref, mk_ref, mv_ref, mr_ref, mg_ref, out_ref, dwg_ref, dbg_ref, dst_ref, drs_ref = refs[8:]
            out_ref[:, 1024:1536] = mg_ref[...].astype(MX)

        @pl.when(pl.program_id(0) == 0)
        def _():
            dst_ref[...] = jnp.zeros_like(dst_ref)
            dwg_ref[...] = jnp.zeros_like(dwg_ref)
            dbg_ref[...] = jnp.zeros_like(dbg_ref)

        consts = _tri(reverse) + _lane_masks()
        order = list(reversed(range(CPB))) if reverse else list(range(CPB))
        for j in range(2):
            sl = slice(j * 128, (j + 1) * 128)
            v0s, v1s = slice(256 * j, 256 * j + 128), slice(256 * j + 128, 256 * j + 256)
            chunks, dout = [], []
            for c in order:
                rows = _chunk(c)
                chunks += [r_ref[rows, :], q_ref[rows, sl], k_ref[rows, sl], v_ref[rows, v0s], v_ref[rows, v1s]]
                dout += [do_ref[rows, v0s], do_ref[rows, v1s]]
            _, vjp = jax.vjp(functools.partial(gla_pair, consts), wg_ref[:, sl], bg_ref[:, sl],
                             ss_ref[0, 2 * j], ss_ref[0, 2 * j + 1], *chunks)
            g = vjp((*dout, dst_ref[2 * j], dst_ref[2 * j + 1]))
            dwg_ref[:, sl] += g[0]
            dbg_ref[:, sl] += g[1]
            dst_ref[2 * j] = g[2]
            dst_ref[2 * j + 1] = g[3]
            for i, c in enumerate(order):
                rows = _chunk(c)
                dr, dq, dk, dv0, dv1 = g[4 + 5 * i:9 + 5 * i]
                if merge is None:
                    if j == 0:
                        dr_ref[rows, :] = dr
                    else:
                        dr_ref[rows, :] += dr
                    dq_ref[rows, sl] = dq
                    dk_ref[rows, sl] = dk
                    dv_ref[rows, v0s] = dv0
                    dv_ref[rows, v1s] = dv1
                else:
                    if j == 0:
                        drs_ref[rows, :] = mr_ref[rows, :] + dr
                    else:
                        out_ref[rows, 1536:1664] = (drs_ref[rows, :] + dr).astype(MX)
                    out_ref[rows, sl] = (mq_ref[rows, sl] + dq).astype(MX)
                    out_ref[rows, 256 + 128 * j:384 + 128 * j] = (mk_ref[rows, sl] + dk).astype(MX)
                    out_ref[rows, 512 + 256 * j:640 + 256 * j] = (mv_ref[rows, v0s] + dv0).astype(MX)
                    out_ref[rows, 640 + 256 * j:768 + 256 * j] = (mv_ref[rows, v1s] + dv1).astype(MX)

    ss_spec = pl.BlockSpec((1, 4, 128, 128), (lambda i: (n - 1 - i, 0, 0, 0)) if rev else (lambda i: (i, 0, 0, 0)))
    ins = [p, p, p, p, wg, bg, ss, do]
    in_specs = _gla_in_specs(tm, n, rev) + [_full(wg), _full(bg), ss_spec, _rows(tm, 512, 0, rn)]
    scratch = [pltpu.VMEM((4, 128, 128), F32)]
    if merge is None:
        out_specs = [_rows(tm, 256, 0, rn), _rows(tm, 256, 0, rn), _rows(tm, 512, 0, rn), _rows(tm, 128, 0, rn)]
        out_shape = [_sds((T, 256)), _sds((T, 256)), _sds((T, 512)), _sds((T, 128))]
    else:
        ins += list(merge)
        in_specs += [_rows(tm, a.shape[1], 0, rn) for a in merge]
        out_specs, out_shape = [_rows(tm, 1664, 0, rn)], [_sds((T, 1664), MX)]
        scratch.append(pltpu.VMEM((tm, 128), F32))
    return _call(body, "gla_bwd_r" if reverse else "gla_bwd_f", (n,), in_specs, out_specs + [_full(wg), _full(bg)],
                 out_shape + [_sds(wg.shape), _sds(bg.shape)], scratch=scratch)(*ins)


def gmlp_bwd(p, douta, ws, bs, lg, lb):
    T = p.shape[0]
    cpb = 4
    tm = cpb * CH

    def body(pa_ref, do_ref, ws_ref, bs_ref, lg_ref, lb_ref, dpa_ref, dws_ref, dbs_ref, dlg_ref, dlb_ref):
        @pl.when(pl.program_id(0) == 0)
        def _():
            for r in (dws_ref, dbs_ref, dlg_ref, dlb_ref):
                r[...] = jnp.zeros_like(r)

        us = [slice(h * 128, (h + 1) * 128) for h in range(4)]
        vs = [slice(512 + h * 128, 512 + (h + 1) * 128) for h in range(4)]
        params = [(ws_ref[h], bs_ref[h], lg_ref[h], lb_ref[h]) for h in range(4)]
        pieces = [[(pa_ref[_chunk(c), us[h]], pa_ref[_chunk(c), vs[h]]) for h in range(4)] for c in range(cpb)]
        _, vjp = jax.vjp(gmlp_heads, params, pieces)
        dparams, dpieces = vjp([[do_ref[_chunk(c), us[h]] for h in range(4)] for c in range(cpb)])
        for h in range(4):
            for r, a in zip((dws_ref, dbs_ref, dlg_ref, dlb_ref), dparams[h]):
                r[h] += a
            for c in range(cpb):
                dpa_ref[_chunk(c), us[h]] = dpieces[c][h][0].astype(MX)
                dpa_ref[_chunk(c), vs[h]] = dpieces[c][h][1].astype(MX)

    return _call(body, "gmlp_bwd", (T // tm,),
                 [_rows(tm, 1024), _rows(tm, 512), _full(ws), _full(bs), _full(lg), _full(lb)],
                 [_rows(tm, 1024), _full(ws), _full(bs), _full(lg), _full(lb)],
                 [_sds((T, 1024), MX), _sds(ws.shape), _sds(bs.shape), _sds(lg.shape), _sds(lb.shape)])(p, douta, ws, bs, lg, lb)


def _gate_pad(w, row0):
    return jnp.zeros((128, 256), F32).at[row0:row0 + 16].set(w)


def local_step(x, tgt, W, get_big, emit, tm=256, tmm=512):
    saved = []
    for l in range(NL):
        s = {"x": x}
        s["w_in"] = get_big(l, 0, x)
        p, s["h"] = norm_matmul(x, W["g_mix"][l][None], s["w_in"], tmm, "mix_in")
        s["p"] = p
        ws, bs = W["w_s"][l], W["b_s"][l][:, :, None]
        lg, lb = W["ln_g"][l][:, None, :], W["ln_b"][l][:, None, :]
        outa = gmlp_fwd(p, ws, bs, lg, lb)
        wgf, wgb = _gate_pad(W["w_gate_f"][l], 0), _gate_pad(W["w_gate_b"][l], 16)
        bgf, bgb = W["b_gate_f"][l][None], W["b_gate_b"][l][None]
        s["of"], s["ssf"] = gla_fwd(p, wgf, bgf, False)
        s["ob"], s["ssb"] = gla_fwd(p, wgb, bgb, True)
        s["w_out"] = get_big(l, 1, s["ob"])
        gg = W["g_gla"][l][:, None, :]
        x1, s["mixed"] = mix_out(x, s["of"], s["ob"], p, outa, gg, s["w_out"], tmm)
        s["x1"] = x1
        s["w_up"] = get_big(l, 2, x1)
        s["zu"], s["h2"], s["z"], s["a"] = ffn_up_conv(x1, W["g_ffn"][l][None], s["w_up"], W["conv_w"][l],
                                                       W["conv_b"][l][None], tm)
        s["w_down"] = get_big(l, 3, s["a"])
        x = matmul_res(s["a"], s["w_down"], x1, tmm, "ffn_down")
        saved.append(s)

    lsum, dx, dgf = loss_head(x, W["g_final"][None], tgt, tmm)
    G = {k: [None] * NL for k in _SMALL if k != "g_final"}
    tok = jnp.zeros((1, 1), F32)
    for l in reversed(range(NL)):
        s = saved[l]
        g_down = matmul_tn(s["a"], dx, min(1024, tmm * 2), 512, "dw_down")
        dz = ffn_down_bwd(dx, s["z"], s["w_down"], tm)
        dzu, dx1, dg, G["conv_w"][l], dcb = ffn_up_bwd(dz, s["zu"], W["conv_w"][l] + tok, s["w_up"], s["x1"],
                                                       W["g_ffn"][l][None], dx, tm)
        G["conv_b"][l], G["g_ffn"][l] = dcb[0], dg[0]
        g_up = matmul_tn(s["h2"], dzu, min(1024, tmm * 2), 1408, "dw_up")
        tok = emit(l, "A", {"w_down": g_down, "w_up": g_up})
        g_out = matmul_tn(s["mixed"], dx1, min(1024, tmm * 2), 1024, "dw_out")
        gg = W["g_gla"][l][:, None, :] + tok
        douta, do, dpg, dgg = mix_out_bwd(dx1, s["w_out"], s["of"], s["ob"], s["p"], gg, tmm)
        G["g_gla"][l] = dgg[:, 0, :]
        wgf, wgb = _gate_pad(W["w_gate_f"][l], 0), _gate_pad(W["w_gate_b"][l], 16)
        bgf, bgb = W["b_gate_f"][l][None], W["b_gate_b"][l][None]
        dqf, dkf, dvf, drf, dwgf, dbgf = gla_bwd(s["p"], wgf, bgf, s["ssf"], do, False)
        dpb, dwgb, dbgb = gla_bwd(s["p"], wgb, bgb, s["ssb"], do, True, merge=(dqf, dkf, dvf, drf, dpg))
        G["w_gate_f"][l], G["b_gate_f"][l] = dwgf[0:16], dbgf[0]
        G["w_gate_b"][l], G["b_gate_b"][l] = dwgb[16:32], dbgb[0]
        ws, bs = W["w_s"][l], W["b_s"][l][:, :, None]
        lg, lb = W["ln_g"][l][:, None, :], W["ln_b"][l][:, None, :]
        dpa, G["w_s"][l], dbs, dlg, dlb = gmlp_bwd(s["p"], douta, ws, bs, lg, lb)
        G["b_s"][l], G["ln_g"][l], G["ln_b"][l] = dbs[:, :, 0], dlg[:, 0, :], dlb[:, 0, :]
        tt = min(1024, tmm * 2)
        g_in = jnp.concatenate([matmul_tn(s["h"], dpa, tt, 1024, "dw_in_a"),
                                matmul_tn(s["h"], dpb, tt, 1664, "dw_in_b")], axis=1)
        tok = emit(l, "B", {"w_out": g_out, "w_in": g_in})
        dx, dg = nt_normbwd([dpa, dpb], s["w_in"], s["x"], W["g_mix"][l][None] + tok, dx1, tmm, "mix_in_bwd")
        G["g_mix"][l] = dg[0]
    G = {k: jnp.stack(v) for k, v in G.items()}
    G["g_final"] = dgf[0]
    return lsum, dx, G


def _rows3(tr, c):
    return pl.BlockSpec((None, tr, c), lambda l, i: (l, i, 0))


def cast_bf16(a, tr):
    nl, r, c = a.shape

    def body(a_ref, o_ref):
        o_ref[...] = a_ref[...].astype(BF16)

    return _call(body, "cast_bf16", (nl, r // tr), [_rows3(tr, c)], _rows3(tr, c), _sds(a.shape, BF16), n_axes=2)(a)


def sum_parts(land, grad, k, chipvec, tr):
    _, rr, cc = land.shape
    nb = rr // tr

    def body(c_ref, l_ref, g_ref, o_ref):
        mine = g_ref[...].astype(F32)
        acc = None
        for j in range(4):
            part = jnp.where(c_ref[0] == j, mine, l_ref[j].astype(F32))
            acc = part if acc is None else acc + part
        o_ref[...] = acc

    gs = pltpu.PrefetchScalarGridSpec(
        num_scalar_prefetch=1, grid=(nb,),
        in_specs=[pl.BlockSpec((4, tr, cc), lambda i, c: (0, i, 0)), _part_spec(k, tr, nb)],
        out_specs=pl.BlockSpec((tr, cc), lambda i, c: (i, 0)))
    return pl.pallas_call(body, name="sum_parts", grid_spec=gs, out_shape=_sds((rr, cc)),
                          compiler_params=_cparams(1))(chipvec, land, grad)


def adamw(w, ga, gb, m, v, tr):
    nl, r, c = w.shape

    def body(w_ref, ga_ref, gb_ref, m_ref, v_ref, g_ref, d_ref, nm_ref, nv_ref):
        gr = ga_ref[...] + gb_ref[...]
        g_ref[...] = gr
        nm = ADAM_B1 * m_ref[...] + (1.0 - ADAM_B1) * gr
        nv = ADAM_B2 * v_ref[...] + (1.0 - ADAM_B2) * jnp.square(gr)
        m_hat = nm / (1.0 - ADAM_B1 ** ADAM_STEP)
        v_hat = nv / (1.0 - ADAM_B2 ** ADAM_STEP)
        d_ref[...] = -ADAM_LR * (m_hat / (jnp.sqrt(v_hat) + ADAM_EPS) + ADAM_WD * w_ref[...])
        nm_ref[...] = nm
        nv_ref[...] = nv

    sp = _rows3(tr, c)
    return _call(body, "adamw", (nl, r // tr), [sp] * 5, [sp] * 4, [_sds(w.shape)] * 4, n_axes=2)(w, ga, gb, m, v)


MESH = pl.DeviceIdType.MESH
ANY = pl.BlockSpec(memory_space=pl.ANY)
N_BIG = 4


def _pos():
    return lax.axis_index("x"), lax.axis_index("y"), lax.axis_index("c")


def _other_chips(x, y):
    return [(1 - x, y), (x, 1 - y), (1 - x, 1 - y)]


def _slab(k, ref, j):
    if k == 0:
        return ref.at[j]
    if k == 1:
        return ref.at[pl.ds(256 * j, 256), :]
    if k == 2:
        return ref.at[:, pl.ds(1408 * j, 1408)]
    return ref.at[pl.ds(704 * j, 704), :]


_LAYER_FULL = [(4, 1024, 648), (1024, 1024), (1024, NUP), (DFF, 1024)]
_LAYER_SHARD = [(1024, 648), (256, 1024), (1024, 1408), (704, 1024)]
_SHARD_SHAPES = [(NL,) + s for s in _LAYER_SHARD]

HBM = pl.BlockSpec(memory_space=pltpu.HBM)
SEM = pl.BlockSpec(memory_space=pltpu.SEMAPHORE)
VM = pl.BlockSpec(memory_space=pltpu.VMEM)
EFFECT = pltpu.SideEffectType.DATAFLOW_SIDE_EFFECTING
_GW_GROUPS = [[(0, k)] for k in range(N_BIG)] + [[(l, k) for k in range(N_BIG)] for l in range(1, NL)]
_GW_ORDER = [lk for g in _GW_GROUPS for lk in g]


def _hbm(a):
    return pltpu.with_memory_space_constraint(a, pltpu.HBM)


def _hbm_like(a):
    return pltpu.HBM(a.shape, a.dtype)


def _part_spec(k, tr, nb):
    cc = _LAYER_SHARD[k][1]
    if k == 0:
        return pl.BlockSpec((None, tr, cc), lambda i, c: (c[0], i, 0))
    if k == 2:
        return pl.BlockSpec((tr, cc), lambda i, c: (i, c[0]))
    return pl.BlockSpec((tr, cc), lambda i, c: (c[0] * nb + i, 0))


def place_own(shard, landing, l, k, chipvec, tr):
    rr, cc = _LAYER_SHARD[k]
    nb = rr // tr

    def body(c_ref, s_ref, l_ref, o_ref):
        o_ref[...] = s_ref[...]

    gs = pltpu.PrefetchScalarGridSpec(
        num_scalar_prefetch=1, grid=(nb,),
        in_specs=[pl.BlockSpec((None, tr, cc), lambda i, c: (l, i, 0)), ANY], out_specs=_part_spec(k, tr, nb))
    return pl.pallas_call(body, name="place_own", grid_spec=gs, out_shape=_sds(landing.shape, landing.dtype),
                          input_output_aliases={2: 0}, compiler_params=_cparams(1))(chipvec, shard, landing)


def gw_start(shards, landings, after):
    n = len(_GW_ORDER)

    def body(*refs):
        S, Ld = refs[:N_BIG], refs[N_BIG:N_BIG + n]
        outs = refs[N_BIG + n + 1:]
        send_sems, recv, token = outs[0], outs[1:1 + len(_GW_GROUPS)], outs[-1]
        x, y, c = _pos()
        me = 2 * x + y
        ci = 0
        for gi, grp in enumerate(_GW_GROUPS):
            for t, (l, k) in enumerate(grp):
                land = Ld[_GW_ORDER.index((l, k))]
                for j, (px, py) in enumerate(_other_chips(x, y)):
                    pltpu.make_async_remote_copy(
                        src_ref=S[k].at[l], dst_ref=_slab(k, land, me), send_sem=send_sems.at[ci],
                        recv_sem=recv[gi].at[3 * t + j], device_id=(px, py, c), device_id_type=MESH).start()
                    ci += 1
        token[...] = jnp.zeros_like(token)

    ins = list(shards) + list(landings)
    sems = [pltpu.SemaphoreType.DMA((3 * n,))] + [pltpu.SemaphoreType.DMA((3 * len(g),)) for g in _GW_GROUPS]
    outs = pl.pallas_call(
        body, name="gw_start", out_shape=sems + [_hbm_like(a) for a in ins] + [_sds((8, 128))],
        in_specs=[HBM] * len(ins) + [pl.BlockSpec(memory_space=pl.ANY)],
        out_specs=[SEM] * len(sems) + [HBM] * len(ins) + [VM],
        input_output_aliases={i: len(sems) + i for i in range(len(ins))},
        compiler_params=pltpu.CompilerParams(has_side_effects=EFFECT))(*[_hbm(a) for a in ins], after)
    ns = len(sems)
    return outs[0], outs[1:ns], outs[ns:ns + N_BIG], outs[ns + N_BIG:ns + len(ins)], outs[-1]


def gw_wait(gi, landings, recv_sems, after, shards=None, send_sems=None):
    grp = _GW_GROUPS[gi]
    n = len(grp)
    last = shards is not None

    def body(*refs):
        Ld, rs = refs[:n], refs[n]
        x, y, c = _pos()
        for t, (l, k) in enumerate(grp):
            for j, (px, py) in enumerate(_other_chips(x, y)):
                region = _slab(k, Ld[t], 2 * px + py)
                pltpu.make_async_remote_copy(src_ref=region, dst_ref=region, send_sem=rs.at[3 * t + j],
                                             recv_sem=rs.at[3 * t + j], device_id=(px, py, c),
                                             device_id_type=MESH).wait_recv()
        if last:
            S, ss = refs[n + 2:n + 2 + N_BIG], refs[n + 2 + N_BIG]
            me = 2 * x + y
            for ci, (l, k) in enumerate(lk for lk in _GW_ORDER for _ in range(3)):
                pltpu.make_async_remote_copy(src_ref=S[k].at[l], dst_ref=_slab(k, Ld[k], me), send_sem=ss.at[ci],
                                             recv_sem=ss.at[ci], device_id=(x, y, c), device_id_type=MESH).wait_send()

    ins = list(landings) + [recv_sems, after]
    specs = [HBM] * n + [SEM, pl.BlockSpec(memory_space=pl.ANY)]
    outs = [_hbm_like(a) for a in landings]
    alias = {i: i for i in range(n)}
    if last:
        ins += list(shards) + [send_sems]
        specs += [HBM] * N_BIG + [SEM]
        outs += [_hbm_like(a) for a in shards]
        alias.update({n + 2 + i: n + i for i in range(N_BIG)})
    res = pl.pallas_call(body, name="gw_wait_%d" % gi, out_shape=outs, in_specs=specs, out_specs=[HBM] * len(outs),
                         input_output_aliases=alias,
                         compiler_params=pltpu.CompilerParams(has_side_effects=EFFECT))(*ins)
    return res[:n], (res[n:] if last else None)


def ga_start(tag, ks, grads, landings):
    n = len(ks)

    def body(*refs):
        G, Ld = refs[:n], refs[n:2 * n]
        send_sems, recv_sems, token = refs[2 * n], refs[2 * n + 1], refs[-1]
        x, y, c = _pos()
        me = 2 * x + y
        for t, k in enumerate(ks):
            for j, (px, py) in enumerate(_other_chips(x, y)):
                pltpu.make_async_remote_copy(
                    src_ref=_slab(k, G[t], 2 * px + py), dst_ref=Ld[t].at[me], send_sem=send_sems.at[3 * t + j],
                    recv_sem=recv_sems.at[3 * t + j], device_id=(px, py, c), device_id_type=MESH).start()
        token[...] = jnp.zeros_like(token)

    ins = list(grads) + list(landings)
    sems = [pltpu.SemaphoreType.DMA((3 * n,))] * 2
    outs = pl.pallas_call(
        body, name="ga_start_" + tag, out_shape=sems + [_hbm_like(a) for a in ins] + [_sds((8, 128))],
        in_specs=[HBM] * len(ins), out_specs=[SEM, SEM] + [HBM] * len(ins) + [VM],
        input_output_aliases={i: 2 + i for i in range(len(ins))},
        compiler_params=pltpu.CompilerParams(has_side_effects=EFFECT))(*[_hbm(a) for a in ins])
    return outs[0], outs[1], outs[2:2 + n], outs[2 + n:2 + 2 * n], outs[-1]


def ga_wait(tag, ks, send_sems, recv_sems, grads, landings, after):
    n = len(ks)

    def body(*refs):
        G, Ld, ss, rs = refs[:n], refs[n:2 * n], refs[2 * n], refs[2 * n + 1]
        x, y, c = _pos()
        me = 2 * x + y
        for t, k in enumerate(ks):
            for j, (px, py) in enumerate(_other_chips(x, y)):
                pj = 2 * px + py
                cp = pltpu.make_async_remote_copy(
                    src_ref=_slab(k, G[t], pj), dst_ref=Ld[t].at[pj], send_sem=ss.at[3 * t + j],
                    recv_sem=rs.at[3 * t + j], device_id=(px, py, c), device_id_type=MESH)
                cp.wait_send()
                cp.wait_recv()

    ins = list(grads) + list(landings) + [send_sems, recv_sems, after]
    res = pl.pallas_call(
        body, name="ga_wait_" + tag, out_shape=[_hbm_like(a) for a in list(grads) + list(landings)],
        in_specs=[HBM] * (2 * n) + [SEM, SEM, pl.BlockSpec(memory_space=pl.ANY)], out_specs=[HBM] * (2 * n),
        input_output_aliases={i: i for i in range(2 * n)},
        compiler_params=pltpu.CompilerParams(has_side_effects=EFFECT))(*ins)
    return res[:n], res[n:]


def swap_start(parts):
    n = len(parts)

    def body(*refs):
        Q, Ld, sems = refs[:n], refs[n:2 * n], refs[2 * n:4 * n]
        x, y, c = _pos()
        for k in range(n):
            pltpu.make_async_remote_copy(src_ref=Q[k], dst_ref=Ld[k], send_sem=sems[k].at[0], recv_sem=sems[n + k].at[0],
                                         device_id=(x, y, 1 - c), device_id_type=MESH).start()
        refs[-1][...] = jnp.zeros_like(refs[-1])

    ins = list(parts) + [lax.empty(p.shape, p.dtype) for p in parts]
    outs = pl.pallas_call(
        body, name="swap_start",
        out_shape=[pltpu.SemaphoreType.DMA((1,))] * (2 * n) + [_hbm_like(a) for a in ins] + [_sds((8, 128))],
        in_specs=[HBM] * (2 * n), out_specs=[SEM] * (2 * n) + [HBM] * (2 * n) + [VM],
        input_output_aliases={i: 2 * n + i for i in range(2 * n)},
        compiler_params=pltpu.CompilerParams(has_side_effects=EFFECT))(*[_hbm(a) for a in ins])
    return outs[:n], outs[n:2 * n], outs[2 * n:3 * n], outs[3 * n:4 * n]


def swap_wait(k, send_sem, recv_sem, part, landing, after):
    def body(q_ref, l_ref, ss, rs, after_ref, q_out, l_out):
        x, y, c = _pos()
        cp = pltpu.make_async_remote_copy(src_ref=q_ref, dst_ref=l_ref, send_sem=ss.at[0], recv_sem=rs.at[0],
                                          device_id=(x, y, 1 - c), device_id_type=MESH)
        cp.wait_send()
        cp.wait_recv()

    return pl.pallas_call(
        body, name="swap_wait_%d" % k, out_shape=[_hbm_like(part), _hbm_like(landing)],
        in_specs=[HBM, HBM, SEM, SEM, pl.BlockSpec(memory_space=pl.ANY)], out_specs=[HBM, HBM],
        input_output_aliases={0: 0, 1: 1},
        compiler_params=pltpu.CompilerParams(has_side_effects=EFFECT))(part, landing, send_sem, recv_sem, after)


def _peer(x, y, c, r):
    fx, fy, fc = (r >> 2) & 1, (r >> 1) & 1, r & 1
    return ((1 - x) if fx else x, (1 - y) if fy else y, (1 - c) if fc else c)


def ag_start(tag, pack):
    rr, cc = pack.shape

    def body(p_ref, l_ref, ss, rs, p_out, l_out, token):
        x, y, c = _pos()
        me = 4 * x + 2 * y + c
        for r in range(1, 8):
            pltpu.make_async_remote_copy(src_ref=p_ref, dst_ref=l_ref.at[me], send_sem=ss.at[r - 1], recv_sem=rs.at[r - 1],
                                         device_id=_peer(x, y, c, r), device_id_type=MESH).start()
        token[...] = jnp.zeros_like(token)

    outs = pl.pallas_call(
        body, name="ag_start_" + tag,
        out_shape=[pltpu.SemaphoreType.DMA((7,)), pltpu.SemaphoreType.DMA((7,)), _hbm_like(pack),
                   pltpu.HBM((8, rr, cc), pack.dtype), _sds((8, 128))],
        in_specs=[HBM, HBM], out_specs=[SEM, SEM, HBM, HBM, VM], input_output_aliases={0: 2, 1: 3},
        compiler_params=pltpu.CompilerParams(has_side_effects=EFFECT))(_hbm(pack), _hbm(lax.empty((8, rr, cc), pack.dtype)))
    return outs


def ag_wait(tag, send_sems, recv_sems, pack, landing, after):
    def body(p_ref, l_ref, ss, rs, after_ref, p_out, l_out):
        x, y, c = _pos()
        for r in range(1, 8):
            px, py, pc = _peer(x, y, c, r)
            cp = pltpu.make_async_remote_copy(src_ref=p_ref, dst_ref=l_ref.at[4 * px + 2 * py + pc], send_sem=ss.at[r - 1],
                                              recv_sem=rs.at[r - 1], device_id=(px, py, pc), device_id_type=MESH)
            cp.wait_send()
            cp.wait_recv()

    return pl.pallas_call(
        body, name="ag_wait_" + tag, out_shape=[_hbm_like(pack), _hbm_like(landing)],
        in_specs=[HBM, HBM, SEM, SEM, pl.BlockSpec(memory_space=pl.ANY)], out_specs=[HBM, HBM],
        input_output_aliases={0: 0, 1: 1},
        compiler_params=pltpu.CompilerParams(has_side_effects=EFFECT))(pack, landing, send_sems, recv_sems, after)


def sum_slots(landing, own, mevec):
    _, rr, cc = landing.shape

    def body(m_ref, l_ref, o_ref, out_ref):
        mine = o_ref[...]
        acc = None
        for j in range(8):
            part = jnp.where(m_ref[0] == j, mine, l_ref[j])
            acc = part if acc is None else acc + part
        out_ref[...] = acc

    gs = pltpu.PrefetchScalarGridSpec(
        num_scalar_prefetch=1, grid=(1,),
        in_specs=[pl.BlockSpec((8, rr, cc), lambda i, m: (0, 0, 0)), pl.BlockSpec((rr, cc), lambda i, m: (0, 0))],
        out_specs=pl.BlockSpec((rr, cc), lambda i, m: (0, 0)))
    return pl.pallas_call(body, name="sum_slots", grid_spec=gs, out_shape=_sds((rr, cc)),
                          compiler_params=_cparams(1))(mevec, landing, own)


_WEIGHTS = ['g_mix', 'w_in', 'w_s', 'b_s', 'ln_g', 'ln_b', 'w_gate_f', 'b_gate_f', 'w_gate_b', 'b_gate_b', 'g_gla',
            'w_out', 'g_ffn', 'w_up', 'conv_w', 'conv_b', 'w_down', 'g_final']
_BIG = ['w_in', 'w_out', 'w_up', 'w_down']
_SMALL = [n for n in _WEIGHTS if n not in _BIG]
_SMALL_SHARDED = {'w_gate_f': 64, 'w_gate_b': 64, 'conv_w': 1408}
_BIG_TR = {'w_in': 512, 'w_out': 256, 'w_up': 256, 'w_down': 352}


def _pack(arrs):
    flat = jnp.concatenate([a.reshape(-1) for a in arrs])
    pad = (-flat.shape[0]) % 1024
    return jnp.pad(flat, (0, pad)).reshape(-1, 128)


def _unpack(buf, shapes):
    flat = buf.reshape(-1)
    out, o = [], 0
    for s in shapes:
        n = 1
        for d in s:
            n *= d
        out.append(flat[o:o + n].reshape(s))
        o += n
    return out


def kernel(x, g_mix, w_in, w_s, b_s, ln_g, ln_b, w_gate_f, b_gate_f, w_gate_b, b_gate_b, g_gla, w_out, g_ffn, w_up, conv_w, conv_b, w_down, g_final, loss_target, m_g_mix, m_w_in, m_w_s, m_b_s, m_ln_g, m_ln_b, m_w_gate_f, m_b_gate_f, m_w_gate_b, m_b_gate_b, m_g_gla, m_w_out, m_g_ffn, m_w_up, m_conv_w, m_conv_b, m_w_down, m_g_final, v_g_mix, v_w_in, v_w_s, v_b_s, v_ln_g, v_ln_b, v_w_gate_f, v_b_gate_f, v_w_gate_b, v_b_gate_b, v_g_gla, v_w_out, v_g_ffn, v_w_up, v_conv_w, v_conv_b, v_w_down, v_g_final):
    loc = locals()
    w = {n: loc[n] for n in _WEIGHTS}
    m = {n: loc["m_" + n] for n in _WEIGHTS}
    v = {n: loc["v_" + n] for n in _WEIGHTS}
    xi, yi, ci = _pos()
    chip = 2 * xi + yi
    me = 2 * chip + ci
    mevec = jnp.reshape(me, (1,)).astype(jnp.int32)

    sh_names = list(_SMALL_SHARDED)
    ss_w, rs_w, pk_w, land_w, tok_w = ag_start("w", _pack([w[n] for n in sh_names]))

    shards = [cast_bf16(w[n], _BIG_TR[n]) for n in _BIG]
    chipvec = jnp.reshape(chip, (1,)).astype(jnp.int32)
    send_sems, recv_sems, shards_fly, landings_fly, started = gw_start(
        shards, [lax.empty(_LAYER_FULL[k], BF16) for _, k in _GW_ORDER], tok_w)
    own = {"shards": shards_fly}

    pk_w, land_w = ag_wait("w", ss_w, rs_w, pk_w, land_w, started)
    per_chip = [_unpack(jnp.where(me == 2 * j, pk_w, land_w[2 * j]), [w[n].shape for n in sh_names]) for j in range(4)]
    W = dict(w)
    for k, n in enumerate(sh_names):
        W[n] = jnp.concatenate([per_chip[j][k] for j in range(4)], axis=-1)
    arrived = {}

    def get_big(l, k, after):
        if (l, k) not in arrived:
            gi = next(i for i, g in enumerate(_GW_GROUPS) if (l, k) in g)
            lo = sum(len(g) for g in _GW_GROUPS[:gi])
            lands = landings_fly[lo:lo + len(_GW_GROUPS[gi])]
            if gi == len(_GW_GROUPS) - 1:
                full, own["shards"] = gw_wait(gi, lands, recv_sems[gi], after, shards_fly, send_sems)
            else:
                full, _ = gw_wait(gi, lands, recv_sems[gi], after)
            for (gl, gk), a in zip(_GW_GROUPS[gi], full):
                arrived[(gl, gk)] = place_own(own["shards"][gk], a, gl, gk, chipvec, _BIG_TR[_BIG[gk]])
        if k == 0:
            f_in = jnp.transpose(arrived[(l, 0)], (1, 0, 2)).reshape(D, N_IN)
            return jnp.pad(f_in, ((0, 0), (0, N_INP - N_IN)))
        return arrived[(l, k)]

    flying = []

    def emit(l, group, grads):
        ks = [3, 2] if group == "A" else [1, 0]
        gs = [grads[_BIG[k]] for k in ks]
        if group == "B":
            gs[1] = jnp.transpose(gs[1][:, :N_IN].reshape(D, 4, 648), (1, 0, 2))
        lands = [lax.empty((4,) + _LAYER_SHARD[k], BF16) for k in ks]
        tag = "%d%s" % (l, group)
        ss, rs, gs_fly, lands_fly, tok = ga_start(tag, ks, gs, lands)
        flying.append((tag, l, ks, ss, rs, gs_fly, lands_fly))
        return tok[0:1, 0:1]

    lsum, grad_x, G = local_step(x[0], loss_target[0], W, get_big, emit)

    small_shapes = [G[n].shape for n in _SMALL] + [(D,)]
    ss_g, rs_g, pk_g, land_g, started = ag_start("g", _pack([G[n] for n in _SMALL] + [lsum]))

    plane = [[None] * NL for _ in range(N_BIG)]
    for tag, l, ks, ss, rs, gs_fly, lands_fly in flying:
        for k, g, a in zip(ks, *ga_wait(tag, ks, ss, rs, gs_fly, lands_fly, started)):
            plane[k][l] = sum_parts(a, g, k, chipvec, _BIG_TR[_BIG[k]])
    ss_p, rs_p, plane_fly, other_fly = swap_start([jnp.stack(p) for p in plane])
    grads, delta, new_m, new_v = {}, {}, {}, {}
    after = grad_x
    for k in (1, 0, 3, 2):
        n = _BIG[k]
        mine, other = swap_wait(k, ss_p[k], rs_p[k], plane_fly[k], other_fly[k], after)
        grads[n], delta[n], new_m[n], new_v[n] = adamw(w[n], mine, other, m[n], v[n], _BIG_TR[n])
        after = delta[n]

    pk_g, land_g = ag_wait("g", ss_g, rs_g, pk_g, land_g, after)
    small = dict(zip(_SMALL + ["lsum"], _unpack(sum_slots(land_g, pk_g, mevec), small_shapes)))
    loss = 0.5 * jnp.sum(small.pop("lsum")) / D
    for n, wd in _SMALL_SHARDED.items():
        small[n] = lax.dynamic_slice_in_dim(small[n], chip * wd, wd, axis=small[n].ndim - 1)
    grads.update(small)
    shapes = [w[n].shape for n in _SMALL]
    pw, pg, pm, pv = (_pack([t[n] for n in _SMALL])[None] for t in (w, grads, m, v))
    _, d_, m_, v_ = adamw(pw, pg, jnp.zeros_like(pg), pm, pv, pw.shape[1])
    for t, buf in ((delta, d_), (new_m, m_), (new_v, v_)):
        t.update(zip(_SMALL, _unpack(buf, shapes)))

    return (loss, grad_x[None], *[grads[n] for n in _WEIGHTS], *[delta[n] for n in _WEIGHTS],
            *[new_m[n] for n in _WEIGHTS], *[new_v[n] for n in _WEIGHTS])
```

```python
import functools

import jax
import jax.numpy as jnp
from jax import lax
from jax.experimental import pallas as pl
from jax.experimental.pallas import tpu as pltpu

F32 = jnp.float32
BF16 = jnp.bfloat16
MX = BF16

D = 1024
CH = 128
NL = 4
N_IN = 2592
N_INP = 2688
NUP = 5632
DFF = 2816
EPS = 1e-6
VMEM_LIMIT = 56 * 1024 * 1024

ADAM_LR, ADAM_B1, ADAM_B2, ADAM_EPS, ADAM_WD, ADAM_STEP = 0.001, 0.9, 0.999, 1e-08, 0.01, 10


def _dg(a, b, ca, cb):
    return lax.dot_general(a.astype(MX), b.astype(MX), (((ca,), (cb,)), ((), ())), preferred_element_type=F32)


@jax.custom_vjp
def mm(a, b):
    return _dg(a, b, 1, 0)


mm.defvjp(lambda a, b: (_dg(a, b, 1, 0), (a, b)),
          lambda r, g: (_dg(g, r[1], 1, 1), _dg(r[0], g, 0, 0)))


@jax.custom_vjp
def mm_nt(a, b):
    return _dg(a, b, 1, 1)


mm_nt.defvjp(lambda a, b: (_dg(a, b, 1, 1), (a, b)),
             lambda r, g: (_dg(g, r[1], 1, 0), _dg(g, r[0], 0, 0)))


@jax.custom_vjp
def mm_tn(a, b):
    return _dg(a, b, 0, 0)


mm_tn.defvjp(lambda a, b: (_dg(a, b, 0, 0), (a, b)),
             lambda r, g: (_dg(r[1], g, 1, 1), _dg(r[0], g, 1, 0)))


def _split3(x):
    hi = x.astype(BF16)
    r1 = x - hi.astype(F32)
    mid = r1.astype(BF16)
    lo = (r1 - mid.astype(F32)).astype(BF16)
    return hi, mid, lo


def _dot3(m, x):
    hi, mid, lo = _split3(x)
    d = lambda p: lax.dot_general(m, p, (((1,), (0,)), ((), ())), preferred_element_type=F32)
    return d(hi) + d(mid) + d(lo)


@jax.custom_vjp
def cumdot(m, mt, x):
    return _dot3(m, x)


cumdot.defvjp(lambda m, mt, x: (_dot3(m, x), (m, mt)),
              lambda r, g: (jnp.zeros_like(r[0]), jnp.zeros_like(r[1]), _dot3(r[1], g)))


def rmsnorm(x, g):
    return x * lax.rsqrt(jnp.mean(x * x, axis=-1, keepdims=True) + EPS) * g


def gelu(x):
    return 0.5 * x * (1.0 + lax.erf(x * 0.7071067811865476))


def sigmoid(x):
    return 1.0 / (1.0 + jnp.exp(-x))


def log_sigmoid(x):
    return jnp.minimum(x, 0.0) - jnp.log(1.0 + jnp.exp(-jnp.abs(x)))


def gmlp_heads(params, pieces):
    u = [[gelu(p[0]) for p in ch] for ch in pieces]
    v = [[gelu(p[1]) for p in ch] for ch in pieces]
    mu = [[jnp.mean(x, axis=-1, keepdims=True) for x in ch] for ch in v]
    var = [[jnp.mean(jnp.square(x - m), axis=-1, keepdims=True) for x, m in zip(cv, cm)] for cv, cm in zip(v, mu)]
    vn = [[(x - m) * lax.rsqrt(s + EPS) * pr[2] + pr[3] for x, m, s, pr in zip(cv, cm, cs, params)]
          for cv, cm, cs in zip(v, mu, var)]
    mix = [[mm(pr[0], x) + pr[1] for x, pr in zip(ch, params)] for ch in vn]
    return [[a * b for a, b in zip(cu, cx)] for cu, cx in zip(u, mix)]


def outb_head(o, pg, g):
    return rmsnorm(o, g) * (pg * sigmoid(pg))


def ffn_act(zg, zv):
    return zg * sigmoid(zg) * zv


def _tri(reverse):
    r = lax.broadcasted_iota(jnp.int32, (CH, CH), 0)
    c = lax.broadcasted_iota(jnp.int32, (CH, CH), 1)
    if reverse:
        cm, sm = c >= r, c > r
    else:
        cm, sm = c <= r, c <= r
    one = jnp.ones((), BF16)
    zero = jnp.zeros((), BF16)
    return jnp.where(cm, one, zero), jnp.where(cm.T, one, zero), sm


def gla_pair(consts, wg, bg, st0, st1, *chunks):
    m, mt, smask, lm0, lm1 = consts
    ch = [chunks[5 * i:5 * i + 5] for i in range(len(chunks) // 5)]
    la = [log_sigmoid(mm(c[0], wg) + bg) * (1.0 / 16.0) for c in ch]
    cum = [cumdot(m, mt, x) for x in la]
    tot = [jnp.sum(x, axis=0, keepdims=True) for x in la]
    q_dec = [(c[1] * 0.125) * jnp.exp(cm) for c, cm in zip(ch, cum)]
    k_inv = [c[2] * jnp.exp(-cm) for c, cm in zip(ch, cum)]
    k_end = [c[2] * jnp.exp(t - cm) for c, t, cm in zip(ch, tot, cum)]
    s = [[jnp.where(smask, mm_nt(qd * lm, ki), 0.0) for lm in (lm0, lm1)] for qd, ki in zip(q_dec, k_inv)]
    o_in = [[mm(si[h], c[3 + h]) for h in (0, 1)] for si, c in zip(s, ch)]
    ds = [[mm_tn(c[3 + h], ke * lm) for h, lm in ((0, lm0), (1, lm1))] for c, ke in zip(ch, k_end)]
    sts = [(st0, st1)]
    for t, d in zip(tot, ds):
        dec = jnp.exp(t)
        sts.append((sts[-1][0] * dec + d[0], sts[-1][1] * dec + d[1]))
    outs = []
    for qd, oi, st in zip(q_dec, o_in, sts):
        outs += [oi[0] + mm_nt(qd, st[0]), oi[1] + mm_nt(qd, st[1])]
    return (*outs, sts[-1][0], sts[-1][1])


def _lane_masks():
    lane = lax.broadcasted_iota(jnp.int32, (1, 128), 1)
    return (lane < 64).astype(F32), (lane >= 64).astype(F32)


def _cparams(n_axes=1):
    return pltpu.CompilerParams(dimension_semantics=("arbitrary",) * n_axes, vmem_limit_bytes=VMEM_LIMIT)


def _full(a):
    nd = a.ndim
    return pl.BlockSpec(a.shape, lambda *_: (0,) * nd)


def _rows(tm, w, cb=0, rev_n=None):
    if rev_n is None:
        return pl.BlockSpec((tm, w), lambda i: (i, cb))
    return pl.BlockSpec((tm, w), lambda i: (rev_n - 1 - i, cb))


def _call(body, name, grid, in_specs, out_specs, out_shape, scratch=(), n_axes=1):
    return pl.pallas_call(body, name=name, grid=grid, in_specs=in_specs, out_specs=out_specs, out_shape=out_shape,
                          scratch_shapes=list(scratch), compiler_params=_cparams(n_axes))


def _sds(shape, dt=F32):
    return jax.ShapeDtypeStruct(shape, dt)


def norm_matmul(x, g, w, tm, name, ydt=F32):
    T, n = x.shape[0], w.shape[1]

    def body(x_ref, g_ref, w_ref, y_ref, h_ref):
        hb = rmsnorm(x_ref[...], g_ref[...]).astype(MX)
        h_ref[...] = hb
        y_ref[...] = jnp.dot(hb, w_ref[...], preferred_element_type=F32).astype(ydt)

    return _call(body, name, (T // tm,), [_rows(tm, D), _full(g), _full(w)],
                 [_rows(tm, n), _rows(tm, D)], [_sds((T, n), ydt), _sds((T, D), MX)])(x, g, w)


CPB = 8


def _chunk(c):
    return slice(c * CH, (c + 1) * CH)


def gmlp_fwd(p, ws, bs, lg, lb):
    T = p.shape[0]
    tm = CPB * CH

    def body(pa_ref, ws_ref, bs_ref, lg_ref, lb_ref, o_ref):
        params = [(ws_ref[h], bs_ref[h], lg_ref[h], lb_ref[h]) for h in range(4)]
        pieces = [[(pa_ref[_chunk(c), h * 128:(h + 1) * 128], pa_ref[_chunk(c), 512 + h * 128:512 + (h + 1) * 128])
                   for h in range(4)] for c in range(CPB)]
        out = gmlp_heads(params, pieces)
        for c in range(CPB):
            for h in range(4):
                o_ref[_chunk(c), h * 128:(h + 1) * 128] = out[c][h].astype(MX)

    return _call(body, "gmlp_fwd", (T // tm,), [_rows(tm, 1024), _full(ws), _full(bs), _full(lg), _full(lb)],
                 _rows(tm, 512), _sds((T, 512), MX))(p, ws, bs, lg, lb)


def _gla_in_specs(tm, n, rev):
    r = n if rev else None
    return [_rows(tm, 256, 4, r), _rows(tm, 256, 5, r), _rows(tm, 512, 3, r), _rows(tm, 128, 20, r)]


def gla_fwd(p, wg, bg, reverse):
    T = p.shape[0]
    tm = CPB * CH
    n = T // tm
    rev = n if reverse else None

    def body(q_ref, k_ref, v_ref, r_ref, wg_ref, bg_ref, o_ref, ss_ref, st_ref):
        @pl.when(pl.program_id(0) == 0)
        def _():
            st_ref[...] = jnp.zeros_like(st_ref)

        consts = _tri(reverse) + _lane_masks()
        order = list(reversed(range(CPB))) if reverse else list(range(CPB))
        ss_ref[0] = st_ref[...]
        for j in range(2):
            sl = slice(j * 128, (j + 1) * 128)
            v0s, v1s = slice(256 * j, 256 * j + 128), slice(256 * j + 128, 256 * j + 256)
            chunks = []
            for c in order:
                rows = _chunk(c)
                chunks += [r_ref[rows, :], q_ref[rows, sl], k_ref[rows, sl], v_ref[rows, v0s], v_ref[rows, v1s]]
            res = gla_pair(consts, wg_ref[:, sl], bg_ref[:, sl], st_ref[2 * j], st_ref[2 * j + 1], *chunks)
            for i, c in enumerate(order):
                o_ref[_chunk(c), v0s] = res[2 * i]
                o_ref[_chunk(c), v1s] = res[2 * i + 1]
            st_ref[2 * j] = res[-2]
            st_ref[2 * j + 1] = res[-1]

    ss_spec = pl.BlockSpec((1, 4, 128, 128), (lambda i: (n - 1 - i, 0, 0, 0)) if reverse else (lambda i: (i, 0, 0, 0)))
    return _call(body, "gla_fwd_r" if reverse else "gla_fwd_f", (n,),
                 _gla_in_specs(tm, n, reverse) + [_full(wg), _full(bg)],
                 [_rows(tm, 512, 0, rev), ss_spec], [_sds((T, 512)), _sds((n, 4, 128, 128))],
                 scratch=[pltpu.VMEM((4, 128, 128), F32)])(p, p, p, p, wg, bg)


def mix_out(x, of, ob, p, outa, gg, w_out, tm):
    T = x.shape[0]

    def body(x_ref, of_ref, ob_ref, pg_ref, oa_ref, gg_ref, w_ref, x1_ref, mx_ref):
        mx_ref[:, 0:512] = oa_ref[...]
        for h in range(4):
            sl = slice(h * 128, (h + 1) * 128)
            mx_ref[:, 512 + h * 128:512 + (h + 1) * 128] = outb_head(
                of_ref[:, sl] + ob_ref[:, sl], pg_ref[:, sl], gg_ref[h]).astype(MX)
        x1_ref[...] = x_ref[...] + jnp.dot(mx_ref[...], w_ref[...], preferred_element_type=F32)

    return _call(body, "mix_out", (T // tm,),
                 [_rows(tm, D), _rows(tm, 512), _rows(tm, 512), _rows(tm, 512, 4), _rows(tm, 512), _full(gg), _full(w_out)],
                 [_rows(tm, D), _rows(tm, 1024)], [_sds((T, D)), _sds((T, 1024), MX)])(x, of, ob, p, outa, gg, w_out)


HALO = 16


def _halo_specs(T, tm, w):
    nb = T // HALO
    r = tm // HALO
    return [pl.BlockSpec((tm, w), lambda i: (i, 0)),
            pl.BlockSpec((HALO, w), lambda i: (jnp.maximum(i * r - 1, 0), 0)),
            pl.BlockSpec((HALO, w), lambda i: (jnp.minimum((i + 1) * r, nb - 1), 0))]


def ffn_up_conv(x1, g, w_up, cw, cb, tm):
    T = x1.shape[0]
    ns = T // tm
    cwid = 256

    def body(x_ref, g_ref, w_ref, cw_ref, cb_ref, zu_ref, h_ref, z_ref, a_ref, prev_ref, tail_ref):
        i = pl.program_id(0)

        @pl.when(i == 0)
        def _():
            prev_ref[...] = jnp.zeros_like(prev_ref)
            tail_ref[...] = jnp.zeros_like(tail_ref)

        hb = rmsnorm(x_ref[...], g_ref[...]).astype(MX)
        h_ref[...] = hb
        row = lax.broadcasted_iota(jnp.int32, (tm, 1), 0)
        for c0 in range(0, DFF, cwid):
            z2 = []
            for cs in (slice(c0, c0 + cwid), slice(DFF + c0, DFF + c0 + cwid)):
                zub = jnp.dot(hb, w_ref[:, cs], preferred_element_type=F32).astype(MX)
                zu_ref[:, cs] = zub
                prev = prev_ref[:, cs].astype(F32)
                pr = tail_ref[HALO - 1:HALO, cs].astype(F32)
                nx = jnp.where(i < ns, zub[0:1, :].astype(F32), 0.0)
                dn = jnp.where(row == 0, pr, pltpu.roll(prev, 1, 0))
                up = jnp.where(row == tm - 1, nx, pltpu.roll(prev, tm - 1, 0))
                z = cb_ref[:, cs] + dn * cw_ref[0:1, cs] + prev * cw_ref[1:2, cs] + up * cw_ref[2:3, cs]
                z_ref[:, cs] = z.astype(MX)
                tail_ref[:, cs] = prev_ref[tm - HALO:tm, cs]
                prev_ref[:, cs] = zub
                z2.append(z)
            a_ref[:, c0:c0 + cwid] = ffn_act(z2[0], z2[1]).astype(MX)

    cur = lambda w: pl.BlockSpec((tm, w), lambda i: (jnp.minimum(i, ns - 1), 0))
    late = lambda w: pl.BlockSpec((tm, w), lambda i: (jnp.maximum(i - 1, 0), 0))
    return _call(body, "ffn_up", (ns + 1,), [cur(D), _full(g), _full(w_up), _full(cw), _full(cb)],
                 [cur(NUP), cur(D), late(NUP), late(DFF)],
                 [_sds((T, NUP), MX), _sds((T, D), MX), _sds((T, NUP), MX), _sds((T, DFF), MX)],
                 scratch=[pltpu.VMEM((tm, NUP), MX), pltpu.VMEM((HALO, NUP), MX)])(x1, g, w_up, cw, cb)


def matmul_res(a, w, res, tm, name):
    T, k = a.shape
    n = w.shape[1]

    def body(a_ref, w_ref, r_ref, o_ref):
        o_ref[...] = r_ref[...] + jnp.dot(a_ref[...], w_ref[...], preferred_element_type=F32)

    return _call(body, name, (T // tm,), [_rows(tm, k), _full(w), _rows(tm, n)], _rows(tm, n), _sds((T, n)))(a, w, res)


def loss_head(x, g, tgt, tm):
    T = x.shape[0]

    def body(x_ref, g_ref, t_ref, l_ref, dx_ref, dg_ref):
        @pl.when(pl.program_id(0) == 0)
        def _():
            l_ref[...] = jnp.zeros_like(l_ref)
            dg_ref[...] = jnp.zeros_like(dg_ref)

        y, vjp = jax.vjp(rmsnorm, x_ref[...], g_ref[...])
        err = y - t_ref[...]
        l_ref[...] += jnp.sum(err * err, axis=0, keepdims=True)
        dx, dg = vjp(err * (1.0 / D))
        dx_ref[...] = dx
        dg_ref[...] += dg

    return _call(body, "loss_head", (T // tm,), [_rows(tm, D), _full(g), _rows(tm, D)],
                 [_full(g), _rows(tm, D), _full(g)], [_sds((1, D)), _sds((T, D)), _sds((1, D))])(x, g, tgt)


def ffn_down_bwd(dx2, z, w_down, tm):
    T = dx2.shape[0]

    def body(dx_ref, z_ref, w_ref, dz_ref):
        da = _dg(dx_ref[...], w_ref[...], 1, 1)
        zg, zv = z_ref[:, :DFF].astype(F32), z_ref[:, DFF:].astype(F32)
        s = sigmoid(zg)
        sz = zg * s
        dz_ref[:, :DFF] = (da * zv * (s + sz * (1.0 - s))).astype(MX)
        dz_ref[:, DFF:] = (da * sz).astype(MX)

    return _call(body, "ffn_down_bwd", (T // tm,), [_rows(tm, D), _rows(tm, NUP), _full(w_down)],
                 _rows(tm, NUP), _sds((T, NUP), MX))(dx2, z, w_down)


def ffn_up_bwd(dz, zu, cw, w_up, x1, g, dres, tm):
    T = dz.shape[0]
    ns = T // tm
    cwid = 512

    def body(dz_ref, dp_ref, dn_ref, zu_ref, cw_ref, w_ref, x_ref, g_ref, dr_ref,
             dzu_ref, dx_ref, dg_ref, dcw_ref, dcb_ref):
        i = pl.program_id(0)

        @pl.when(i == 0)
        def _():
            for r in (dg_ref, dcw_ref, dcb_ref):
                r[...] = jnp.zeros_like(r)

        row = lax.broadcasted_iota(jnp.int32, (tm, 1), 0)
        dh = jnp.zeros((tm, D), F32)
        for c0 in range(0, NUP, cwid):
            cs = slice(c0, c0 + cwid)
            dz = dz_ref[:, cs].astype(F32)
            zu = zu_ref[:, cs].astype(F32)
            pr = jnp.where(i > 0, dp_ref[HALO - 1:HALO, cs].astype(F32), 0.0)
            nx = jnp.where(i < ns - 1, dn_ref[0:1, cs].astype(F32), 0.0)
            ddn = jnp.where(row == 0, pr, pltpu.roll(dz, 1, 0))
            dup = jnp.where(row == tm - 1, nx, pltpu.roll(dz, tm - 1, 0))
            dzu = (dup * cw_ref[0:1, cs] + dz * cw_ref[1:2, cs] + ddn * cw_ref[2:3, cs]).astype(MX)
            dzu_ref[:, cs] = dzu
            dcw_ref[0:1, cs] += jnp.sum(zu * dup, axis=0, keepdims=True)
            dcw_ref[1:2, cs] += jnp.sum(zu * dz, axis=0, keepdims=True)
            dcw_ref[2:3, cs] += jnp.sum(zu * ddn, axis=0, keepdims=True)
            dcb_ref[:, cs] += jnp.sum(dz, axis=0, keepdims=True)
            dh = dh + _dg(dzu, w_ref[:, cs], 1, 1)
        _, vjp = jax.vjp(rmsnorm, x_ref[...], g_ref[...])
        dx, dg = vjp(dh)
        dx_ref[...] = dr_ref[...] + dx
        dg_ref[...] += dg

    return _call(body, "ffn_up_bwd", (ns,),
                 _halo_specs(T, tm, NUP) + [_rows(tm, NUP), _full(cw), _full(w_up), _rows(tm, D), _full(g), _rows(tm, D)],
                 [_rows(tm, NUP), _rows(tm, D), _full(g), _full(cw), pl.BlockSpec((1, NUP), lambda i: (0, 0))],
                 [_sds((T, NUP), MX), _sds((T, D)), _sds((1, D)), _sds((3, NUP)), _sds((1, NUP))])(
                     dz, dz, dz, zu, cw, w_up, x1, g, dres)


def nt_normbwd(dys, w, x, g, dres, tm, name):
    T = x.shape[0]
    n = len(dys)
    offs = [sum(d.shape[1] for d in dys[:i]) for i in range(n + 1)]

    def body(*refs):
        dy_refs, (w_ref, x_ref, g_ref, dr_ref, dx_ref, dg_ref) = refs[:n], refs[n:]

        @pl.when(pl.program_id(0) == 0)
        def _():
            dg_ref[...] = jnp.zeros_like(dg_ref)

        dh = _dg(dy_refs[0][...], w_ref[:, offs[0]:offs[1]], 1, 1)
        for i in range(1, n):
            dh = dh + _dg(dy_refs[i][...], w_ref[:, offs[i]:offs[i + 1]], 1, 1)
        _, vjp = jax.vjp(rmsnorm, x_ref[...], g_ref[...])
        dx, dg = vjp(dh)
        dx_ref[...] = dr_ref[...] + dx
        dg_ref[...] += dg

    return _call(body, name, (T // tm,),
                 [_rows(tm, d.shape[1]) for d in dys] + [_full(w), _rows(tm, D), _full(g), _rows(tm, D)],
                 [_rows(tm, D), _full(g)], [_sds((T, D)), _sds((1, D))])(*dys, w, x, g, dres)


def matmul_tn(a, b, tt, tn, name):
    T, k = a.shape
    n = b.shape[1]
    last = T // tt - 1

    def body(a_ref, b_ref, o_ref, acc_ref):
        @pl.when(pl.program_id(1) == 0)
        def _():
            acc_ref[...] = jnp.zeros_like(acc_ref)

        acc_ref[...] += _dg(a_ref[...], b_ref[...], 0, 0)

        @pl.when(pl.program_id(1) == last)
        def _():
            o_ref[...] = acc_ref[...].astype(MX)

    return _call(body, name, (n // tn, T // tt),
                 [pl.BlockSpec((tt, k), lambda j, i: (i, 0)), pl.BlockSpec((tt, tn), lambda j, i: (i, j))],
                 pl.BlockSpec((k, tn), lambda j, i: (0, j)), _sds((k, n), MX), scratch=[pltpu.VMEM((k, tn), F32)],
                 n_axes=2)(a, b)


def mix_out_bwd(dx1, w_out, of, ob, p, gg, tm):
    T = dx1.shape[0]

    def body(dx_ref, w_ref, of_ref, ob_ref, pg_ref, gg_ref, da_ref, do_ref, dpg_ref, dgg_ref):
        @pl.when(pl.program_id(0) == 0)
        def _():
            dgg_ref[...] = jnp.zeros_like(dgg_ref)

        dxb = dx_ref[...].astype(MX)
        da_ref[...] = _dg(dxb, w_ref[0:512, :], 1, 1)
        for h in range(4):
            sl = slice(h * 128, (h + 1) * 128)
            dm = _dg(dxb, w_ref[512 + h * 128:512 + (h + 1) * 128, :], 1, 1)
            _, vjp = jax.vjp(outb_head, of_ref[:, sl] + ob_ref[:, sl], pg_ref[:, sl], gg_ref[h])
            do, dpg, dg = vjp(dm)
            do_ref[:, sl] = do
            dpg_ref[:, sl] = dpg
            dgg_ref[h] += dg

    return _call(body, "mix_out_bwd", (T // tm,),
                 [_rows(tm, D), _full(w_out), _rows(tm, 512), _rows(tm, 512), _rows(tm, 512, 4), _full(gg)],
                 [_rows(tm, 512), _rows(tm, 512), _rows(tm, 512), _full(gg)],
                 [_sds((T, 512)), _sds((T, 512)), _sds((T, 512)), _sds(gg.shape)])(dx1, w_out, of, ob, p, gg)


def gla_bwd(p, wg, bg, ss, do, reverse, merge=None):
    T = p.shape[0]
    tm = CPB * CH
    n = T // tm
    rev = not reverse
    rn = n if rev else None

    def body(*refs):
        q_ref, k_ref, v_ref, r_ref, wg_ref, bg_ref, ss_ref, do_ref = refs[:8]
        if merge is None:
            dq_ref, dk_ref, dv_ref, dr_ref, dwg_ref, dbg_ref, dst_ref = refs[8:]
        else:
            mq_ref, mk_ref, mv_ref, mr_ref, mg_ref, out_ref, dwg_ref, dbg_ref, dst_ref, drs_ref = refs[8:]
            out_ref[:, 1024:1536] = mg_ref[...].astype(MX)

        @pl.when(pl.program_id(0) == 0)
        def _():
            dst_ref[...] = jnp.zeros_like(dst_ref)
            dwg_ref[...] = jnp.zeros_like(dwg_ref)
            dbg_ref[...] = jnp.zeros_like(dbg_ref)

        consts = _tri(reverse) + _lane_masks()
        order = list(reversed(range(CPB))) if reverse else list(range(CPB))
        for j in range(2):
            sl = slice(j * 128, (j + 1) * 128)
            v0s, v1s = slice(256 * j, 256 * j + 128), slice(256 * j + 128, 256 * j + 256)
            chunks, dout = [], []
            for c in order:
                rows = _chunk(c)
                chunks += [r_ref[rows, :], q_ref[rows, sl], k_ref[rows, sl], v_ref[rows, v0s], v_ref[rows, v1s]]
                dout += [do_ref[rows, v0s], do_ref[rows, v1s]]
            _, vjp = jax.vjp(functools.partial(gla_pair, consts), wg_ref[:, sl], bg_ref[:, sl],
                             ss_ref[0, 2 * j], ss_ref[0, 2 * j + 1], *chunks)
            g = vjp((*dout, dst_ref[2 * j], dst_ref[2 * j + 1]))
            dwg_ref[:, sl] += g[0]
            dbg_ref[:, sl] += g[1]
            dst_ref[2 * j] = g[2]
            dst_ref[2 * j + 1] = g[3]
            for i, c in enumerate(order):
                rows = _chunk(c)
                dr, dq, dk, dv0, dv1 = g[4 + 5 * i:9 + 5 * i]
                if merge is None:
                    if j == 0:
                        dr_ref[rows, :] = dr
                    else:
                        dr_ref[rows, :] += dr
                    dq_ref[rows, sl] = dq
                    dk_ref[rows, sl] = dk
                    dv_ref[rows, v0s] = dv0
                    dv_ref[rows, v1s] = dv1
                else:
                    if j == 0:
                        drs_ref[rows, :] = mr_ref[rows, :] + dr
                    else:
                        out_ref[rows, 1536:1664] = (drs_ref[rows, :] + dr).astype(MX)
                    out_ref[rows, sl] = (mq_ref[rows, sl] + dq).astype(MX)
                    out_ref[rows, 256 + 128 * j:384 + 128 * j] = (mk_ref[rows, sl] + dk).astype(MX)
                    out_ref[rows, 512 + 256 * j:640 + 256 * j] = (mv_ref[rows, v0s] + dv0).astype(MX)
                    out_ref[rows, 640 + 256 * j:768 + 256 * j] = (mv_ref[rows, v1s] + dv1).astype(MX)

    ss_spec = pl.BlockSpec((1, 4, 128, 128), (lambda i: (n - 1 - i, 0, 0, 0)) if rev else (lambda i: (i, 0, 0, 0)))
    ins = [p, p, p, p, wg, bg, ss, do]
    in_specs = _gla_in_specs(tm, n, rev) + [_full(wg), _full(bg), ss_spec, _rows(tm, 512, 0, rn)]
    scratch = [pltpu.VMEM((4, 128, 128), F32)]
    if merge is None:
        out_specs = [_rows(tm, 256, 0, rn), _rows(tm, 256, 0, rn), _rows(tm, 512, 0, rn), _rows(tm, 128, 0, rn)]
        out_shape = [_sds((T, 256)), _sds((T, 256)), _sds((T, 512)), _sds((T, 128))]
    else:
        ins += list(merge)
        in_specs += [_rows(tm, a.shape[1], 0, rn) for a in merge]
        out_specs, out_shape = [_rows(tm, 1664, 0, rn)], [_sds((T, 1664), MX)]
        scratch.append(pltpu.VMEM((tm, 128), F32))
    return _call(body, "gla_bwd_r" if reverse else "gla_bwd_f", (n,), in_specs, out_specs + [_full(wg), _full(bg)],
                 out_shape + [_sds(wg.shape), _sds(bg.shape)], scratch=scratch)(*ins)


def gmlp_bwd(p, douta, ws, bs, lg, lb):
    T = p.shape[0]
    cpb = 4
    tm = cpb * CH

    def body(pa_ref, do_ref, ws_ref, bs_ref, lg_ref, lb_ref, dpa_ref, dws_ref, dbs_ref, dlg_ref, dlb_ref):
        @pl.when(pl.program_id(0) == 0)
        def _():
            for r in (dws_ref, dbs_ref, dlg_ref, dlb_ref):
                r[...] = jnp.zeros_like(r)

        us = [slice(h * 128, (h + 1) * 128) for h in range(4)]
        vs = [slice(512 + h * 128, 512 + (h + 1) * 128) for h in range(4)]
        params = [(ws_ref[h], bs_ref[h], lg_ref[h], lb_ref[h]) for h in range(4)]
        pieces = [[(pa_ref[_chunk(c), us[h]], pa_ref[_chunk(c), vs[h]]) for h in range(4)] for c in range(cpb)]
        _, vjp = jax.vjp(gmlp_heads, params, pieces)
        dparams, dpieces = vjp([[do_ref[_chunk(c), us[h]] for h in range(4)] for c in range(cpb)])
        for h in range(4):
            for r, a in zip((dws_ref, dbs_ref, dlg_ref, dlb_ref), dparams[h]):
                r[h] += a
            for c in range(cpb):
                dpa_ref[_chunk(c), us[h]] = dpieces[c][h][0].astype(MX)
                dpa_ref[_chunk(c), vs[h]] = dpieces[c][h][1].astype(MX)

    return _call(body, "gmlp_bwd", (T // tm,),
                 [_rows(tm, 1024), _rows(tm, 512), _full(ws), _full(bs), _full(lg), _full(lb)],
                 [_rows(tm, 1024), _full(ws), _full(bs), _full(lg), _full(lb)],
                 [_sds((T, 1024), MX), _sds(ws.shape), _sds(bs.shape), _sds(lg.shape), _sds(lb.shape)])(p, douta, ws, bs, lg, lb)


def _gate_pad(w, row0):
    return jnp.zeros((128, 256), F32).at[row0:row0 + 16].set(w)


def local_step(x, tgt, W, get_big, emit, tm=256, tmm=512):
    saved = []
    for l in range(NL):
        s = {"x": x}
        s["w_in"] = get_big(l, 0, x)
        p, s["h"] = norm_matmul(x, W["g_mix"][l][None], s["w_in"], tmm, "mix_in")
        s["p"] = p
        ws, bs = W["w_s"][l], W["b_s"][l][:, :, None]
        lg, lb = W["ln_g"][l][:, None, :], W["ln_b"][l][:, None, :]
        outa = gmlp_fwd(p, ws, bs, lg, lb)
        wgf, wgb = _gate_pad(W["w_gate_f"][l], 0), _gate_pad(W["w_gate_b"][l], 16)
        bgf, bgb = W["b_gate_f"][l][None], W["b_gate_b"][l][None]
        s["of"], s["ssf"] = gla_fwd(p, wgf, bgf, False)
        s["ob"], s["ssb"] = gla_fwd(p, wgb, bgb, True)
        s["w_out"] = get_big(l, 1, s["ob"])
        gg = W["g_gla"][l][:, None, :]
        x1, s["mixed"] = mix_out(x, s["of"], s["ob"], p, outa, gg, s["w_out"], tmm)
        s["x1"] = x1
        s["w_up"] = get_big(l, 2, x1)
        s["zu"], s["h2"], s["z"], s["a"] = ffn_up_conv(x1, W["g_ffn"][l][None], s["w_up"], W["conv_w"][l],
                                                       W["conv_b"][l][None], tm // 2)
        s["w_down"] = get_big(l, 3, s["a"])
        x = matmul_res(s["a"], s["w_down"], x1, tmm, "ffn_down")
        saved.append(s)

    lsum, dx, dgf = loss_head(x, W["g_final"][None], tgt, tmm)
    G = {k: [None] * NL for k in _SMALL if k != "g_final"}
    tok = jnp.zeros((1, 1), F32)
    for l in reversed(range(NL)):
        s = saved[l]
        g_down = matmul_tn(s["a"], dx, min(1024, tmm * 2), 512, "dw_down")
        dz = ffn_down_bwd(dx, s["z"], s["w_down"], tm)
        dzu, dx1, dg, G["conv_w"][l], dcb = ffn_up_bwd(dz, s["zu"], W["conv_w"][l] + tok, s["w_up"], s["x1"],
                                                       W["g_ffn"][l][None], dx, tm)
        G["conv_b"][l], G["g_ffn"][l] = dcb[0], dg[0]
        g_up = matmul_tn(s["h2"], dzu, min(1024, tmm * 2), 1408, "dw_up")
        tok = emit(l, "A", {"w_down": g_down, "w_up": g_up})
        g_out = matmul_tn(s["mixed"], dx1, min(1024, tmm * 2), 1024, "dw_out")
        gg = W["g_gla"][l][:, None, :] + tok
        douta, do, dpg, dgg = mix_out_bwd(dx1, s["w_out"], s["of"], s["ob"], s["p"], gg, tmm)
        G["g_gla"][l] = dgg[:, 0, :]
        wgf, wgb = _gate_pad(W["w_gate_f"][l], 0), _gate_pad(W["w_gate_b"][l], 16)
        bgf, bgb = W["b_gate_f"][l][None], W["b_gate_b"][l][None]
        dqf, dkf, dvf, drf, dwgf, dbgf = gla_bwd(s["p"], wgf, bgf, s["ssf"], do, False)
        dpb, dwgb, dbgb = gla_bwd(s["p"], wgb, bgb, s["ssb"], do, True, merge=(dqf, dkf, dvf, drf, dpg))
        G["w_gate_f"][l], G["b_gate_f"][l] = dwgf[0:16], dbgf[0]
        G["w_gate_b"][l], G["b_gate_b"][l] = dwgb[16:32], dbgb[0]
        ws, bs = W["w_s"][l], W["b_s"][l][:, :, None]
        lg, lb = W["ln_g"][l][:, None, :], W["ln_b"][l][:, None, :]
        dpa, G["w_s"][l], dbs, dlg, dlb = gmlp_bwd(s["p"], douta, ws, bs, lg, lb)
        G["b_s"][l], G["ln_g"][l], G["ln_b"][l] = dbs[:, :, 0], dlg[:, 0, :], dlb[:, 0, :]
        tt = min(1024, tmm * 2)
        g_in = jnp.concatenate([matmul_tn(s["h"], dpa, tt, 1024, "dw_in_a"),
                                matmul_tn(s["h"], dpb, tt, 1664, "dw_in_b")], axis=1)
        tok = emit(l, "B", {"w_out": g_out, "w_in": g_in})
        dx, dg = nt_normbwd([dpa, dpb], s["w_in"], s["x"], W["g_mix"][l][None] + tok, dx1, tmm, "mix_in_bwd")
        G["g_mix"][l] = dg[0]
    G = {k: jnp.stack(v) for k, v in G.items()}
    G["g_final"] = dgf[0]
    return lsum, dx, G


def _rows3(tr, c):
    return pl.BlockSpec((None, tr, c), lambda l, i: (l, i, 0))


def cast_bf16(a, tr):
    nl, r, c = a.shape

    def body(a_ref, o_ref):
        o_ref[...] = a_ref[...].astype(BF16)

    return _call(body, "cast_bf16", (nl, r // tr), [_rows3(tr, c)], _rows3(tr, c), _sds(a.shape, BF16), n_axes=2)(a)


def sum_parts(land, grad, k, chipvec, tr):
    _, rr, cc = land.shape
    nb = rr // tr

    def body(c_ref, l_ref, g_ref, o_ref):
        mine = g_ref[...].astype(F32)
        acc = None
        for j in range(4):
            part = jnp.where(c_ref[0] == j, mine, l_ref[j].astype(F32))
            acc = part if acc is None else acc + part
        o_ref[...] = acc

    gs = pltpu.PrefetchScalarGridSpec(
        num_scalar_prefetch=1, grid=(nb,),
        in_specs=[pl.BlockSpec((4, tr, cc), lambda i, c: (0, i, 0)), _part_spec(k, tr, nb)],
        out_specs=pl.BlockSpec((tr, cc), lambda i, c: (i, 0)))
    return pl.pallas_call(body, name="sum_parts", grid_spec=gs, out_shape=_sds((rr, cc)),
                          compiler_params=_cparams(1))(chipvec, land, grad)


def adamw(w, ga, gb, m, v, tr):
    nl, r, c = w.shape

    def body(w_ref, ga_ref, gb_ref, m_ref, v_ref, g_ref, d_ref, nm_ref, nv_ref):
        gr = ga_ref[...] + gb_ref[...]
        g_ref[...] = gr
        nm = ADAM_B1 * m_ref[...] + (1.0 - ADAM_B1) * gr
        nv = ADAM_B2 * v_ref[...] + (1.0 - ADAM_B2) * jnp.square(gr)
        m_hat = nm * (1.0 / (1.0 - ADAM_B1 ** ADAM_STEP))
        v_hat = nv * (1.0 / (1.0 - ADAM_B2 ** ADAM_STEP))
        d_ref[...] = -ADAM_LR * (m_hat / (jnp.sqrt(v_hat) + ADAM_EPS) + ADAM_WD * w_ref[...])
        nm_ref[...] = nm
        nv_ref[...] = nv

    sp = _rows3(tr, c)
    return _call(body, "adamw", (nl, r // tr), [sp] * 5, [sp] * 4, [_sds(w.shape)] * 4, n_axes=2)(w, ga, gb, m, v)


MESH = pl.DeviceIdType.MESH
ANY = pl.BlockSpec(memory_space=pl.ANY)
N_BIG = 4


def _pos():
    return lax.axis_index("x"), lax.axis_index("y"), lax.axis_index("c")


def _other_chips(x, y):
    return [(1 - x, y), (x, 1 - y), (1 - x, 1 - y)]


def _slab(k, ref, j):
    if k == 0:
        return ref.at[j]
    if k == 1:
        return ref.at[pl.ds(256 * j, 256), :]
    if k == 2:
        return ref.at[:, pl.ds(1408 * j, 1408)]
    return ref.at[pl.ds(704 * j, 704), :]


_LAYER_FULL = [(4, 1024, 648), (1024, 1024), (1024, NUP), (DFF, 1024)]
_LAYER_SHARD = [(1024, 648), (256, 1024), (1024, 1408), (704, 1024)]

HBM = pl.BlockSpec(memory_space=pltpu.HBM)
SEM = pl.BlockSpec(memory_space=pltpu.SEMAPHORE)
VM = pl.BlockSpec(memory_space=pltpu.VMEM)
EFFECT = pltpu.SideEffectType.DATAFLOW_SIDE_EFFECTING
_GW_GROUPS = [[(0, k)] for k in range(N_BIG)] + [[(l, k) for k in range(N_BIG)] for l in range(1, NL)]
_GW_ORDER = [lk for g in _GW_GROUPS for lk in g]


def _hbm(a):
    return pltpu.with_memory_space_constraint(a, pltpu.HBM)


def _hbm_like(a):
    return pltpu.HBM(a.shape, a.dtype)


def _part_spec(k, tr, nb):
    cc = _LAYER_SHARD[k][1]
    if k == 0:
        return pl.BlockSpec((None, tr, cc), lambda i, c: (c[0], i, 0))
    if k == 2:
        return pl.BlockSpec((tr, cc), lambda i, c: (i, c[0]))
    return pl.BlockSpec((tr, cc), lambda i, c: (c[0] * nb + i, 0))


def place_own(shard, landing, l, k, chipvec, tr):
    rr, cc = _LAYER_SHARD[k]
    nb = rr // tr

    def body(c_ref, s_ref, l_ref, o_ref):
        o_ref[...] = s_ref[...]

    gs = pltpu.PrefetchScalarGridSpec(
        num_scalar_prefetch=1, grid=(nb,),
        in_specs=[pl.BlockSpec((None, tr, cc), lambda i, c: (l, i, 0)), ANY], out_specs=_part_spec(k, tr, nb))
    return pl.pallas_call(body, name="place_own", grid_spec=gs, out_shape=_sds(landing.shape, landing.dtype),
                          input_output_aliases={2: 0}, compiler_params=_cparams(1))(chipvec, shard, landing)


def gw_start(shards, landings, after):
    n = len(_GW_ORDER)

    def body(*refs):
        S, Ld = refs[:N_BIG], refs[N_BIG:N_BIG + n]
        outs = refs[N_BIG + n + 1:]
        send_sems, recv, token = outs[0], outs[1:1 + len(_GW_GROUPS)], outs[-1]
        x, y, c = _pos()
        me = 2 * x + y
        ci = 0
        for gi, grp in enumerate(_GW_GROUPS):
            for t, (l, k) in enumerate(grp):
                land = Ld[_GW_ORDER.index((l, k))]
                for j, (px, py) in enumerate(_other_chips(x, y)):
                    pltpu.make_async_remote_copy(
                        src_ref=S[k].at[l], dst_ref=_slab(k, land, me), send_sem=send_sems.at[ci],
                        recv_sem=recv[gi].at[3 * t + j], device_id=(px, py, c), device_id_type=MESH).start()
                    ci += 1
        token[...] = jnp.zeros_like(token)

    ins = list(shards) + list(landings)
    sems = [pltpu.SemaphoreType.DMA((3 * n,))] + [pltpu.SemaphoreType.DMA((3 * len(g),)) for g in _GW_GROUPS]
    outs = pl.pallas_call(
        body, name="gw_start", out_shape=sems + [_hbm_like(a) for a in ins] + [_sds((8, 128))],
        in_specs=[HBM] * len(ins) + [pl.BlockSpec(memory_space=pl.ANY)],
        out_specs=[SEM] * len(sems) + [HBM] * len(ins) + [VM],
        input_output_aliases={i: len(sems) + i for i in range(len(ins))},
        compiler_params=pltpu.CompilerParams(has_side_effects=EFFECT))(*[_hbm(a) for a in ins], after)
    ns = len(sems)
    return outs[0], outs[1:ns], outs[ns:ns + N_BIG], outs[ns + N_BIG:ns + len(ins)], outs[-1]


def gw_wait(gi, landings, recv_sems, after, shards=None, send_sems=None):
    grp = _GW_GROUPS[gi]
    n = len(grp)
    last = shards is not None

    def body(*refs):
        Ld, rs = refs[:n], refs[n]
        x, y, c = _pos()
        for t, (l, k) in enumerate(grp):
            for j, (px, py) in enumerate(_other_chips(x, y)):
                region = _slab(k, Ld[t], 2 * px + py)
                pltpu.make_async_remote_copy(src_ref=region, dst_ref=region, send_sem=rs.at[3 * t + j],
                                             recv_sem=rs.at[3 * t + j], device_id=(px, py, c),
                                             device_id_type=MESH).wait_recv()
        if last:
            S, ss = refs[n + 2:n + 2 + N_BIG], refs[n + 2 + N_BIG]
            me = 2 * x + y
            for ci, (l, k) in enumerate(lk for lk in _GW_ORDER for _ in range(3)):
                pltpu.make_async_remote_copy(src_ref=S[k].at[l], dst_ref=_slab(k, Ld[k], me), send_sem=ss.at[ci],
                                             recv_sem=ss.at[ci], device_id=(x, y, c), device_id_type=MESH).wait_send()

    ins = list(landings) + [recv_sems, after]
    specs = [HBM] * n + [SEM, pl.BlockSpec(memory_space=pl.ANY)]
    outs = [_hbm_like(a) for a in landings]
    alias = {i: i for i in range(n)}
    if last:
        ins += list(shards) + [send_sems]
        specs += [HBM] * N_BIG + [SEM]
        outs += [_hbm_like(a) for a in shards]
        alias.update({n + 2 + i: n + i for i in range(N_BIG)})
    res = pl.pallas_call(body, name="gw_wait_%d" % gi, out_shape=outs, in_specs=specs, out_specs=[HBM] * len(outs),
                         input_output_aliases=alias,
                         compiler_params=pltpu.CompilerParams(has_side_effects=EFFECT))(*ins)
    return res[:n], (res[n:] if last else None)


def ga_start(tag, ks, grads, landings):
    n = len(ks)

    def body(*refs):
        G, Ld = refs[:n], refs[n:2 * n]
        send_sems, recv_sems, token = refs[2 * n], refs[2 * n + 1], refs[-1]
        x, y, c = _pos()
        me = 2 * x + y
        for t, k in enumerate(ks):
            for j, (px, py) in enumerate(_other_chips(x, y)):
                pltpu.make_async_remote_copy(
                    src_ref=_slab(k, G[t], 2 * px + py), dst_ref=Ld[t].at[me], send_sem=send_sems.at[3 * t + j],
                    recv_sem=recv_sems.at[3 * t + j], device_id=(px, py, c), device_id_type=MESH).start()
        token[...] = jnp.zeros_like(token)

    ins = list(grads) + list(landings)
    sems = [pltpu.SemaphoreType.DMA((3 * n,))] * 2
    outs = pl.pallas_call(
        body, name="ga_start_" + tag, out_shape=sems + [_hbm_like(a) for a in ins] + [_sds((8, 128))],
        in_specs=[HBM] * len(ins), out_specs=[SEM, SEM] + [HBM] * len(ins) + [VM],
        input_output_aliases={i: 2 + i for i in range(len(ins))},
        compiler_params=pltpu.CompilerParams(has_side_effects=EFFECT))(*[_hbm(a) for a in ins])
    return outs[0], outs[1], outs[2:2 + n], outs[2 + n:2 + 2 * n], outs[-1]


def ga_wait(tag, ks, send_sems, recv_sems, grads, landings, after):
    n = len(ks)

    def body(*refs):
        G, Ld, ss, rs = refs[:n], refs[n:2 * n], refs[2 * n], refs[2 * n + 1]
        x, y, c = _pos()
        me = 2 * x + y
        for t, k in enumerate(ks):
            for j, (px, py) in enumerate(_other_chips(x, y)):
                pj = 2 * px + py
                cp = pltpu.make_async_remote_copy(
                    src_ref=_slab(k, G[t], pj), dst_ref=Ld[t].at[pj], send_sem=ss.at[3 * t + j],
                    recv_sem=rs.at[3 * t + j], device_id=(px, py, c), device_id_type=MESH)
                cp.wait_send()
                cp.wait_recv()

    ins = list(grads) + list(landings) + [send_sems, recv_sems, after]
    res = pl.pallas_call(
        body, name="ga_wait_" + tag, out_shape=[_hbm_like(a) for a in list(grads) + list(landings)],
        in_specs=[HBM] * (2 * n) + [SEM, SEM, pl.BlockSpec(memory_space=pl.ANY)], out_specs=[HBM] * (2 * n),
        input_output_aliases={i: i for i in range(2 * n)},
        compiler_params=pltpu.CompilerParams(has_side_effects=EFFECT))(*ins)
    return res[:n], res[n:]


def swap_start(parts):
    n = len(parts)

    def body(*refs):
        Q, Ld, sems = refs[:n], refs[n:2 * n], refs[2 * n:4 * n]
        x, y, c = _pos()
        for k in range(n):
            pltpu.make_async_remote_copy(src_ref=Q[k], dst_ref=Ld[k], send_sem=sems[k].at[0], recv_sem=sems[n + k].at[0],
                                         device_id=(x, y, 1 - c), device_id_type=MESH).start()
        refs[-1][...] = jnp.zeros_like(refs[-1])

    ins = list(parts) + [lax.empty(p.shape, p.dtype) for p in parts]
    outs = pl.pallas_call(
        body, name="swap_start",
        out_shape=[pltpu.SemaphoreType.DMA((1,))] * (2 * n) + [_hbm_like(a) for a in ins] + [_sds((8, 128))],
        in_specs=[HBM] * (2 * n), out_specs=[SEM] * (2 * n) + [HBM] * (2 * n) + [VM],
        input_output_aliases={i: 2 * n + i for i in range(2 * n)},
        compiler_params=pltpu.CompilerParams(has_side_effects=EFFECT))(*[_hbm(a) for a in ins])
    return outs[:n], outs[n:2 * n], outs[2 * n:3 * n], outs[3 * n:4 * n]


def swap_wait(k, send_sem, recv_sem, part, landing, after):
    def body(q_ref, l_ref, ss, rs, after_ref, q_out, l_out):
        x, y, c = _pos()
        cp = pltpu.make_async_remote_copy(src_ref=q_ref, dst_ref=l_ref, send_sem=ss.at[0], recv_sem=rs.at[0],
                                          device_id=(x, y, 1 - c), device_id_type=MESH)
        cp.wait_send()
        cp.wait_recv()

    return pl.pallas_call(
        body, name="swap_wait_%d" % k, out_shape=[_hbm_like(part), _hbm_like(landing)],
        in_specs=[HBM, HBM, SEM, SEM, pl.BlockSpec(memory_space=pl.ANY)], out_specs=[HBM, HBM],
        input_output_aliases={0: 0, 1: 1},
        compiler_params=pltpu.CompilerParams(has_side_effects=EFFECT))(part, landing, send_sem, recv_sem, after)


def _peer(x, y, c, r):
    fx, fy, fc = (r >> 2) & 1, (r >> 1) & 1, r & 1
    return ((1 - x) if fx else x, (1 - y) if fy else y, (1 - c) if fc else c)


def ag_start(tag, pack):
    rr, cc = pack.shape

    def body(p_ref, l_ref, ss, rs, p_out, l_out, token):
        x, y, c = _pos()
        me = 4 * x + 2 * y + c
        for r in range(1, 8):
            pltpu.make_async_remote_copy(src_ref=p_ref, dst_ref=l_ref.at[me], send_sem=ss.at[r - 1], recv_sem=rs.at[r - 1],
                                         device_id=_peer(x, y, c, r), device_id_type=MESH).start()
        token[...] = jnp.zeros_like(token)

    outs = pl.pallas_call(
        body, name="ag_start_" + tag,
        out_shape=[pltpu.SemaphoreType.DMA((7,)), pltpu.SemaphoreType.DMA((7,)), _hbm_like(pack),
                   pltpu.HBM((8, rr, cc), pack.dtype), _sds((8, 128))],
        in_specs=[HBM, HBM], out_specs=[SEM, SEM, HBM, HBM, VM], input_output_aliases={0: 2, 1: 3},
        compiler_params=pltpu.CompilerParams(has_side_effects=EFFECT))(_hbm(pack), _hbm(lax.empty((8, rr, cc), pack.dtype)))
    return outs


def ag_wait(tag, send_sems, recv_sems, pack, landing, after):
    def body(p_ref, l_ref, ss, rs, after_ref, p_out, l_out):
        x, y, c = _pos()
        for r in range(1, 8):
            px, py, pc = _peer(x, y, c, r)
            cp = pltpu.make_async_remote_copy(src_ref=p_ref, dst_ref=l_ref.at[4 * px + 2 * py + pc], send_sem=ss.at[r - 1],
                                              recv_sem=rs.at[r - 1], device_id=(px, py, pc), device_id_type=MESH)
            cp.wait_send()
            cp.wait_recv()

    return pl.pallas_call(
        body, name="ag_wait_" + tag, out_shape=[_hbm_like(pack), _hbm_like(landing)],
        in_specs=[HBM, HBM, SEM, SEM, pl.BlockSpec(memory_space=pl.ANY)], out_specs=[HBM, HBM],
        input_output_aliases={0: 0, 1: 1},
        compiler_params=pltpu.CompilerParams(has_side_effects=EFFECT))(pack, landing, send_sems, recv_sems, after)


def sum_slots(landing, own, mevec):
    _, rr, cc = landing.shape

    def body(m_ref, l_ref, o_ref, out_ref):
        mine = o_ref[...]
        acc = None
        for j in range(8):
            part = jnp.where(m_ref[0] == j, mine, l_ref[j])
            acc = part if acc is None else acc + part
        out_ref[...] = acc

    gs = pltpu.PrefetchScalarGridSpec(
        num_scalar_prefetch=1, grid=(1,),
        in_specs=[pl.BlockSpec((8, rr, cc), lambda i, m: (0, 0, 0)), pl.BlockSpec((rr, cc), lambda i, m: (0, 0))],
        out_specs=pl.BlockSpec((rr, cc), lambda i, m: (0, 0)))
    return pl.pallas_call(body, name="sum_slots", grid_spec=gs, out_shape=_sds((rr, cc)),
                          compiler_params=_cparams(1))(mevec, landing, own)


_WEIGHTS = ['g_mix', 'w_in', 'w_s', 'b_s', 'ln_g', 'ln_b', 'w_gate_f', 'b_gate_f', 'w_gate_b', 'b_gate_b', 'g_gla',
            'w_out', 'g_ffn', 'w_up', 'conv_w', 'conv_b', 'w_down', 'g_final']
_BIG = ['w_in', 'w_out', 'w_up', 'w_down']
_SMALL = [n for n in _WEIGHTS if n not in _BIG]
_SMALL_SHARDED = {'w_gate_f': 64, 'w_gate_b': 64, 'conv_w': 1408}
_BIG_TR = {'w_in': 512, 'w_out': 256, 'w_up': 256, 'w_down': 352}


def _pack(arrs):
    flat = jnp.concatenate([a.reshape(-1) for a in arrs])
    pad = (-flat.shape[0]) % 1024
    return jnp.pad(flat, (0, pad)).reshape(-1, 128)


def _unpack(buf, shapes):
    flat = buf.reshape(-1)
    out, o = [], 0
    for s in shapes:
        n = 1
        for d in s:
            n *= d
        out.append(flat[o:o + n].reshape(s))
        o += n
    return out


def kernel(x, g_mix, w_in, w_s, b_s, ln_g, ln_b, w_gate_f, b_gate_f, w_gate_b, b_gate_b, g_gla, w_out, g_ffn, w_up, conv_w, conv_b, w_down, g_final, loss_target, m_g_mix, m_w_in, m_w_s, m_b_s, m_ln_g, m_ln_b, m_w_gate_f, m_b_gate_f, m_w_gate_b, m_b_gate_b, m_g_gla, m_w_out, m_g_ffn, m_w_up, m_conv_w, m_conv_b, m_w_down, m_g_final, v_g_mix, v_w_in, v_w_s, v_b_s, v_ln_g, v_ln_b, v_w_gate_f, v_b_gate_f, v_w_gate_b, v_b_gate_b, v_g_gla, v_w_out, v_g_ffn, v_w_up, v_conv_w, v_conv_b, v_w_down, v_g_final):
    loc = locals()
    w = {n: loc[n] for n in _WEIGHTS}
    m = {n: loc["m_" + n] for n in _WEIGHTS}
    v = {n: loc["v_" + n] for n in _WEIGHTS}
    xi, yi, ci = _pos()
    chip = 2 * xi + yi
    me = 2 * chip + ci
    mevec = jnp.reshape(me, (1,)).astype(jnp.int32)

    sh_names = list(_SMALL_SHARDED)
    ss_w, rs_w, pk_w, land_w, tok_w = ag_start("w", _pack([w[n] for n in sh_names]))

    shards = [cast_bf16(w[n], _BIG_TR[n]) for n in _BIG]
    chipvec = jnp.reshape(chip, (1,)).astype(jnp.int32)
    send_sems, recv_sems, shards_fly, landings_fly, started = gw_start(
        shards, [lax.empty(_LAYER_FULL[k], BF16) for _, k in _GW_ORDER], tok_w)
    own = {"shards": shards_fly}

    pk_w, land_w = ag_wait("w", ss_w, rs_w, pk_w, land_w, started)
    per_chip = [_unpack(jnp.where(me == 2 * j, pk_w, land_w[2 * j]), [w[n].shape for n in sh_names]) for j in range(4)]
    W = dict(w)
    for k, n in enumerate(sh_names):
        W[n] = jnp.concatenate([per_chip[j][k] for j in range(4)], axis=-1)
    arrived = {}

    def get_big(l, k, after):
        if (l, k) not in arrived:
            gi = next(i for i, g in enumerate(_GW_GROUPS) if (l, k) in g)
            lo = sum(len(g) for g in _GW_GROUPS[:gi])
            lands = landings_fly[lo:lo + len(_GW_GROUPS[gi])]
            if gi == len(_GW_GROUPS) - 1:
                full, own["shards"] = gw_wait(gi, lands, recv_sems[gi], after, shards_fly, send_sems)
            else:
                full, _ = gw_wait(gi, lands, recv_sems[gi], after)
            for (gl, gk), a in zip(_GW_GROUPS[gi], full):
                arrived[(gl, gk)] = place_own(own["shards"][gk], a, gl, gk, chipvec, _BIG_TR[_BIG[gk]])
        if k == 0:
            f_in = jnp.transpose(arrived[(l, 0)], (1, 0, 2)).reshape(D, N_IN)
            return jnp.pad(f_in, ((0, 0), (0, N_INP - N_IN)))
        return arrived[(l, k)]

    flying = []

    def emit(l, group, grads):
        ks = [3, 2] if group == "A" else [1, 0]
        gs = [grads[_BIG[k]] for k in ks]
        if group == "B":
            gs[1] = jnp.transpose(gs[1][:, :N_IN].reshape(D, 4, 648), (1, 0, 2))
        lands = [lax.empty((4,) + _LAYER_SHARD[k], BF16) for k in ks]
        tag = "%d%s" % (l, group)
        ss, rs, gs_fly, lands_fly, tok = ga_start(tag, ks, gs, lands)
        flying.append((tag, l, ks, ss, rs, gs_fly, lands_fly))
        return tok[0:1, 0:1]

    lsum, grad_x, G = local_step(x[0], loss_target[0], W, get_big, emit)

    small_shapes = [G[n].shape for n in _SMALL] + [(D,)]
    ss_g, rs_g, pk_g, land_g, started = ag_start("g", _pack([G[n] for n in _SMALL] + [lsum]))

    plane = [[None] * NL for _ in range(N_BIG)]
    for tag, l, ks, ss, rs, gs_fly, lands_fly in flying:
        for k, g, a in zip(ks, *ga_wait(tag, ks, ss, rs, gs_fly, lands_fly, started)):
            plane[k][l] = sum_parts(a, g, k, chipvec, _BIG_TR[_BIG[k]])
    ss_p, rs_p, plane_fly, other_fly = swap_start([jnp.stack(p) for p in plane])
    grads, delta, new_m, new_v = {}, {}, {}, {}
    after = grad_x
    for k in (1, 0, 3, 2):
        n = _BIG[k]
        mine, other = swap_wait(k, ss_p[k], rs_p[k], plane_fly[k], other_fly[k], after)
        grads[n], delta[n], new_m[n], new_v[n] = adamw(w[n], mine, other, m[n], v[n], _BIG_TR[n])
        after = delta[n]

    pk_g, land_g = ag_wait("g", ss_g, rs_g, pk_g, land_g, after)
    small = dict(zip(_SMALL + ["lsum"], _unpack(sum_slots(land_g, pk_g, mevec), small_shapes)))
    loss = 0.5 * jnp.sum(small.pop("lsum")) / D
    for n, wd in _SMALL_SHARDED.items():
        small[n] = lax.dynamic_slice_in_dim(small[n], chip * wd, wd, axis=small[n].ndim - 1)
    grads.update(small)
    shapes = [w[n].shape for n in _SMALL]
    pw, pg, pm, pv = (_pack([t[n] for n in _SMALL])[None] for t in (w, grads, m, v))
    _, d_, m_, v_ = adamw(pw, pg, jnp.zeros_like(pg), pm, pv, pw.shape[1])
    for t, buf in ((delta, d_), (new_m, m_), (new_v, v_)):
        t.update(zip(_SMALL, _unpack(buf, shapes)))

    return (loss, grad_x[None], *[grads[n] for n in _WEIGHTS], *[delta[n] for n in _WEIGHTS],
            *[new_m[n] for n in _WEIGHTS], *[new_v[n] for n in _WEIGHTS])
```

```python
import functools

import jax
import jax.numpy as jnp
from jax import lax
from jax.experimental import pallas as pl
from jax.experimental.pallas import tpu as pltpu

F32 = jnp.float32
BF16 = jnp.bfloat16
MX = BF16

D = 1024
CH = 128
NL = 4
N_IN = 2592
N_INP = 2688
NUP = 5632
DFF = 2816
EPS = 1e-6
VMEM_LIMIT = 56 * 1024 * 1024

ADAM_LR, ADAM_B1, ADAM_B2, ADAM_EPS, ADAM_WD, ADAM_STEP = 0.001, 0.9, 0.999, 1e-08, 0.01, 10


def _dg(a, b, ca, cb):
    return lax.dot_general(a.astype(MX), b.astype(MX), (((ca,), (cb,)), ((), ())), preferred_element_type=F32)


@jax.custom_vjp
def mm(a, b):
    return _dg(a, b, 1, 0)


mm.defvjp(lambda a, b: (_dg(a, b, 1, 0), (a, b)),
          lambda r, g: (_dg(g, r[1], 1, 1), _dg(r[0], g, 0, 0)))


@jax.custom_vjp
def mm_nt(a, b):
    return _dg(a, b, 1, 1)


mm_nt.defvjp(lambda a, b: (_dg(a, b, 1, 1), (a, b)),
             lambda r, g: (_dg(g, r[1], 1, 0), _dg(g, r[0], 0, 0)))


@jax.custom_vjp
def mm_tn(a, b):
    return _dg(a, b, 0, 0)


mm_tn.defvjp(lambda a, b: (_dg(a, b, 0, 0), (a, b)),
             lambda r, g: (_dg(r[1], g, 1, 1), _dg(r[0], g, 1, 0)))


def _split3(x):
    hi = x.astype(BF16)
    r1 = x - hi.astype(F32)
    mid = r1.astype(BF16)
    lo = (r1 - mid.astype(F32)).astype(BF16)
    return hi, mid, lo


def _dot3(m, x):
    hi, mid, lo = _split3(x)
    d = lambda p: lax.dot_general(m, p, (((1,), (0,)), ((), ())), preferred_element_type=F32)
    return d(hi) + d(mid) + d(lo)


@jax.custom_vjp
def cumdot(m, mt, x):
    return _dot3(m, x)


cumdot.defvjp(lambda m, mt, x: (_dot3(m, x), (m, mt)),
              lambda r, g: (jnp.zeros_like(r[0]), jnp.zeros_like(r[1]), _dot3(r[1], g)))


def rmsnorm(x, g):
    return x * lax.rsqrt(jnp.mean(x * x, axis=-1, keepdims=True) + EPS) * g


def gelu(x):
    return 0.5 * x * (1.0 + lax.erf(x * 0.7071067811865476))


def sigmoid(x):
    return 1.0 / (1.0 + jnp.exp(-x))


def log_sigmoid(x):
    return jnp.minimum(x, 0.0) - jnp.log(1.0 + jnp.exp(-jnp.abs(x)))


def gmlp_heads(params, pieces):
    u = [[gelu(p[0]) for p in ch] for ch in pieces]
    v = [[gelu(p[1]) for p in ch] for ch in pieces]
    mu = [[jnp.mean(x, axis=-1, keepdims=True) for x in ch] for ch in v]
    var = [[jnp.mean(jnp.square(x - m), axis=-1, keepdims=True) for x, m in zip(cv, cm)] for cv, cm in zip(v, mu)]
    vn = [[(x - m) * lax.rsqrt(s + EPS) * pr[2] + pr[3] for x, m, s, pr in zip(cv, cm, cs, params)]
          for cv, cm, cs in zip(v, mu, var)]
    mix = [[mm(pr[0], x) + pr[1] for x, pr in zip(ch, params)] for ch in vn]
    return [[a * b for a, b in zip(cu, cx)] for cu, cx in zip(u, mix)]


def outb_head(o, pg, g):
    return rmsnorm(o, g) * (pg * sigmoid(pg))


def ffn_act(zg, zv):
    return zg * sigmoid(zg) * zv


def _tri(reverse):
    r = lax.broadcasted_iota(jnp.int32, (CH, CH), 0)
    c = lax.broadcasted_iota(jnp.int32, (CH, CH), 1)
    if reverse:
        cm, sm = c >= r, c > r
    else:
        cm, sm = c <= r, c <= r
    one = jnp.ones((), BF16)
    zero = jnp.zeros((), BF16)
    return jnp.where(cm, one, zero), jnp.where(cm.T, one, zero), sm


def gla_pair(consts, wg, bg, st0, st1, *chunks):
    m, mt, smask, lm0, lm1 = consts
    ch = [chunks[5 * i:5 * i + 5] for i in range(len(chunks) // 5)]
    la = [log_sigmoid(mm(c[0], wg) + bg) * (1.0 / 16.0) for c in ch]
    cum = [cumdot(m, mt, x) for x in la]
    tot = [jnp.sum(x, axis=0, keepdims=True) for x in la]
    q_dec = [(c[1] * 0.125) * jnp.exp(cm) for c, cm in zip(ch, cum)]
    k_inv = [c[2] * jnp.exp(-cm) for c, cm in zip(ch, cum)]
    k_end = [c[2] * jnp.exp(t - cm) for c, t, cm in zip(ch, tot, cum)]
    s = [[jnp.where(smask, mm_nt(qd * lm, ki), 0.0) for lm in (lm0, lm1)] for qd, ki in zip(q_dec, k_inv)]
    o_in = [[mm(si[h], c[3 + h]) for h in (0, 1)] for si, c in zip(s, ch)]
    ds = [[mm_tn(c[3 + h], ke * lm) for h, lm in ((0, lm0), (1, lm1))] for c, ke in zip(ch, k_end)]
    sts = [(st0, st1)]
    for t, d in zip(tot, ds):
        dec = jnp.exp(t)
        sts.append((sts[-1][0] * dec + d[0], sts[-1][1] * dec + d[1]))
    outs = []
    for qd, oi, st in zip(q_dec, o_in, sts):
        outs += [oi[0] + mm_nt(qd, st[0]), oi[1] + mm_nt(qd, st[1])]
    return (*outs, sts[-1][0], sts[-1][1])


def _lane_masks():
    lane = lax.broadcasted_iota(jnp.int32, (1, 128), 1)
    return (lane < 64).astype(F32), (lane >= 64).astype(F32)


def _cparams(n_axes=1):
    return pltpu.CompilerParams(dimension_semantics=("arbitrary",) * n_axes, vmem_limit_bytes=VMEM_LIMIT)


def _full(a):
    nd = a.ndim
    return pl.BlockSpec(a.shape, lambda *_: (0,) * nd)


def _rows(tm, w, cb=0, rev_n=None):
    if rev_n is None:
        return pl.BlockSpec((tm, w), lambda i: (i, cb))
    return pl.BlockSpec((tm, w), lambda i: (rev_n - 1 - i, cb))


def _call(body, name, grid, in_specs, out_specs, out_shape, scratch=(), n_axes=1):
    return pl.pallas_call(body, name=name, grid=grid, in_specs=in_specs, out_specs=out_specs, out_shape=out_shape,
                          scratch_shapes=list(scratch), compiler_params=_cparams(n_axes))


def _sds(shape, dt=F32):
    return jax.ShapeDtypeStruct(shape, dt)


def norm_matmul(x, g, w, tm, name, ydt=F32):
    T, n = x.shape[0], w.shape[1]

    def body(x_ref, g_ref, w_ref, y_ref, h_ref):
        hb = rmsnorm(x_ref[...], g_ref[...]).astype(MX)
        h_ref[...] = hb
        y_ref[...] = jnp.dot(hb, w_ref[...], preferred_element_type=F32).astype(ydt)

    return _call(body, name, (T // tm,), [_rows(tm, D), _full(g), _full(w)],
                 [_rows(tm, n), _rows(tm, D)], [_sds((T, n), ydt), _sds((T, D), MX)])(x, g, w)


CPB = 8


def _chunk(c):
    return slice(c * CH, (c + 1) * CH)


def gmlp_fwd(p, ws, bs, lg, lb):
    T = p.shape[0]
    tm = CPB * CH

    def body(pa_ref, ws_ref, bs_ref, lg_ref, lb_ref, o_ref):
        params = [(ws_ref[h], bs_ref[h], lg_ref[h], lb_ref[h]) for h in range(4)]
        pieces = [[(pa_ref[_chunk(c), h * 128:(h + 1) * 128], pa_ref[_chunk(c), 512 + h * 128:512 + (h + 1) * 128])
                   for h in range(4)] for c in range(CPB)]
        out = gmlp_heads(params, pieces)
        for c in range(CPB):
            for h in range(4):
                o_ref[_chunk(c), h * 128:(h + 1) * 128] = out[c][h].astype(MX)

    return _call(body, "gmlp_fwd", (T // tm,), [_rows(tm, 1024), _full(ws), _full(bs), _full(lg), _full(lb)],
                 _rows(tm, 512), _sds((T, 512), MX))(p, ws, bs, lg, lb)


def _gla_in_specs(tm, n, rev):
    r = n if rev else None
    return [_rows(tm, 256, 4, r), _rows(tm, 256, 5, r), _rows(tm, 512, 3, r), _rows(tm, 128, 20, r)]


def gla_fwd(p, wg, bg, reverse):
    T = p.shape[0]
    tm = CPB * CH
    n = T // tm
    rev = n if reverse else None

    def body(q_ref, k_ref, v_ref, r_ref, wg_ref, bg_ref, o_ref, ss_ref, st_ref):
        @pl.when(pl.program_id(0) == 0)
        def _():
            st_ref[...] = jnp.zeros_like(st_ref)

        consts = _tri(reverse) + _lane_masks()
        order = list(reversed(range(CPB))) if reverse else list(range(CPB))
        ss_ref[0] = st_ref[...]
        for j in range(2):
            sl = slice(j * 128, (j + 1) * 128)
            v0s, v1s = slice(256 * j, 256 * j + 128), slice(256 * j + 128, 256 * j + 256)
            chunks = []
            for c in order:
                rows = _chunk(c)
                chunks += [r_ref[rows, :], q_ref[rows, sl], k_ref[rows, sl], v_ref[rows, v0s], v_ref[rows, v1s]]
            res = gla_pair(consts, wg_ref[:, sl], bg_ref[:, sl], st_ref[2 * j], st_ref[2 * j + 1], *chunks)
            for i, c in enumerate(order):
                o_ref[_chunk(c), v0s] = res[2 * i]
                o_ref[_chunk(c), v1s] = res[2 * i + 1]
            st_ref[2 * j] = res[-2]
            st_ref[2 * j + 1] = res[-1]

    ss_spec = pl.BlockSpec((1, 4, 128, 128), (lambda i: (n - 1 - i, 0, 0, 0)) if reverse else (lambda i: (i, 0, 0, 0)))
    return _call(body, "gla_fwd_r" if reverse else "gla_fwd_f", (n,),
                 _gla_in_specs(tm, n, reverse) + [_full(wg), _full(bg)],
                 [_rows(tm, 512, 0, rev), ss_spec], [_sds((T, 512)), _sds((n, 4, 128, 128))],
                 scratch=[pltpu.VMEM((4, 128, 128), F32)])(p, p, p, p, wg, bg)


def mix_out(x, of, ob, p, outa, gg, w_out, tm):
    T = x.shape[0]

    def body(x_ref, of_ref, ob_ref, pg_ref, oa_ref, gg_ref, w_ref, x1_ref, mx_ref):
        mx_ref[:, 0:512] = oa_ref[...]
        for h in range(4):
            sl = slice(h * 128, (h + 1) * 128)
            mx_ref[:, 512 + h * 128:512 + (h + 1) * 128] = outb_head(
                of_ref[:, sl] + ob_ref[:, sl], pg_ref[:, sl], gg_ref[h]).astype(MX)
        x1_ref[...] = x_ref[...] + jnp.dot(mx_ref[...], w_ref[...], preferred_element_type=F32)

    return _call(body, "mix_out", (T // tm,),
                 [_rows(tm, D), _rows(tm, 512), _rows(tm, 512), _rows(tm, 512, 4), _rows(tm, 512), _full(gg), _full(w_out)],
                 [_rows(tm, D), _rows(tm, 1024)], [_sds((T, D)), _sds((T, 1024), MX)])(x, of, ob, p, outa, gg, w_out)


HALO = 16


def _halo_specs(T, tm, w):
    nb = T // HALO
    r = tm // HALO
    return [pl.BlockSpec((tm, w), lambda i: (i, 0)),
            pl.BlockSpec((HALO, w), lambda i: (jnp.maximum(i * r - 1, 0), 0)),
            pl.BlockSpec((HALO, w), lambda i: (jnp.minimum((i + 1) * r, nb - 1), 0))]


def ffn_up_conv(x1, g, w_up, cw, cb, tm):
    T = x1.shape[0]
    ns = T // tm
    cwid = 256

    def body(x_ref, g_ref, w_ref, cw_ref, cb_ref, zu_ref, h_ref, z_ref, a_ref, prev_ref, tail_ref):
        i = pl.program_id(0)

        @pl.when(i == 0)
        def _():
            prev_ref[...] = jnp.zeros_like(prev_ref)
            tail_ref[...] = jnp.zeros_like(tail_ref)

        hb = rmsnorm(x_ref[...], g_ref[...]).astype(MX)
        h_ref[...] = hb
        row = lax.broadcasted_iota(jnp.int32, (tm, 1), 0)
        for c0 in range(0, DFF, cwid):
            z2 = []
            for cs in (slice(c0, c0 + cwid), slice(DFF + c0, DFF + c0 + cwid)):
                zub = jnp.dot(hb, w_ref[:, cs], preferred_element_type=F32).astype(MX)
                zu_ref[:, cs] = zub
                prev = prev_ref[:, cs].astype(F32)
                pr = tail_ref[HALO - 1:HALO, cs].astype(F32)
                nx = jnp.where(i < ns, zub[0:1, :].astype(F32), 0.0)
                dn = jnp.where(row == 0, pr, pltpu.roll(prev, 1, 0))
                up = jnp.where(row == tm - 1, nx, pltpu.roll(prev, tm - 1, 0))
                z = cb_ref[:, cs] + dn * cw_ref[0:1, cs] + prev * cw_ref[1:2, cs] + up * cw_ref[2:3, cs]
                z_ref[:, cs] = z.astype(MX)
                tail_ref[:, cs] = prev_ref[tm - HALO:tm, cs]
                prev_ref[:, cs] = zub
                z2.append(z)
            a_ref[:, c0:c0 + cwid] = ffn_act(z2[0], z2[1]).astype(MX)

    cur = lambda w: pl.BlockSpec((tm, w), lambda i: (jnp.minimum(i, ns - 1), 0))
    late = lambda w: pl.BlockSpec((tm, w), lambda i: (jnp.maximum(i - 1, 0), 0))
    return _call(body, "ffn_up", (ns + 1,), [cur(D), _full(g), _full(w_up), _full(cw), _full(cb)],
                 [cur(NUP), cur(D), late(NUP), late(DFF)],
                 [_sds((T, NUP), MX), _sds((T, D), MX), _sds((T, NUP), MX), _sds((T, DFF), MX)],
                 scratch=[pltpu.VMEM((tm, NUP), MX), pltpu.VMEM((HALO, NUP), MX)])(x1, g, w_up, cw, cb)


def matmul_res(a, w, res, tm, name):
    T, k = a.shape
    n = w.shape[1]

    def body(a_ref, w_ref, r_ref, o_ref):
        o_ref[...] = r_ref[...] + jnp.dot(a_ref[...], w_ref[...], preferred_element_type=F32)

    return _call(body, name, (T // tm,), [_rows(tm, k), _full(w), _rows(tm, n)], _rows(tm, n), _sds((T, n)))(a, w, res)


def loss_head(x, g, tgt, tm):
    T = x.shape[0]

    def body(x_ref, g_ref, t_ref, l_ref, dx_ref, dg_ref):
        @pl.when(pl.program_id(0) == 0)
        def _():
            l_ref[...] = jnp.zeros_like(l_ref)
            dg_ref[...] = jnp.zeros_like(dg_ref)

        y, vjp = jax.vjp(rmsnorm, x_ref[...], g_ref[...])
        err = y - t_ref[...]
        l_ref[...] += jnp.sum(err * err, axis=0, keepdims=True)
        dx, dg = vjp(err * (1.0 / D))
        dx_ref[...] = dx
        dg_ref[...] += dg

    return _call(body, "loss_head", (T // tm,), [_rows(tm, D), _full(g), _rows(tm, D)],
                 [_full(g), _rows(tm, D), _full(g)], [_sds((1, D)), _sds((T, D)), _sds((1, D))])(x, g, tgt)


def ffn_down_bwd(dx2, z, w_down, tm):
    T = dx2.shape[0]

    def body(dx_ref, z_ref, w_ref, dz_ref):
        da = _dg(dx_ref[...], w_ref[...], 1, 1)
        zg, zv = z_ref[:, :DFF].astype(F32), z_ref[:, DFF:].astype(F32)
        s = sigmoid(zg)
        sz = zg * s
        dz_ref[:, :DFF] = (da * zv * (s + sz * (1.0 - s))).astype(MX)
        dz_ref[:, DFF:] = (da * sz).astype(MX)

    return _call(body, "ffn_down_bwd", (T // tm,), [_rows(tm, D), _rows(tm, NUP), _full(w_down)],
                 _rows(tm, NUP), _sds((T, NUP), MX))(dx2, z, w_down)


def ffn_up_bwd(dz, zu, cw, w_up, x1, g, dres, tm):
    T = dz.shape[0]
    ns = T // tm
    cwid = 256

    def body(dz_ref, dp_ref, dn_ref, zu_ref, cw_ref, w_ref, x_ref, g_ref, dr_ref,
             dzu_ref, dx_ref, dg_ref, dcw_ref, dcb_ref):
        i = pl.program_id(0)

        @pl.when(i == 0)
        def _():
            for r in (dg_ref, dcw_ref, dcb_ref):
                r[...] = jnp.zeros_like(r)

        row = lax.broadcasted_iota(jnp.int32, (tm, 1), 0)
        dh = jnp.zeros((tm, D), F32)
        for c0 in range(0, NUP, cwid):
            cs = slice(c0, c0 + cwid)
            dz = dz_ref[:, cs].astype(F32)
            zu = zu_ref[:, cs].astype(F32)
            pr = jnp.where(i > 0, dp_ref[HALO - 1:HALO, cs].astype(F32), 0.0)
            nx = jnp.where(i < ns - 1, dn_ref[0:1, cs].astype(F32), 0.0)
            ddn = jnp.where(row == 0, pr, pltpu.roll(dz, 1, 0))
            dup = jnp.where(row == tm - 1, nx, pltpu.roll(dz, tm - 1, 0))
            dzu = (dup * cw_ref[0:1, cs] + dz * cw_ref[1:2, cs] + ddn * cw_ref[2:3, cs]).astype(MX)
            dzu_ref[:, cs] = dzu
            dcw_ref[0:1, cs] += jnp.sum(zu * dup, axis=0, keepdims=True)
            dcw_ref[1:2, cs] += jnp.sum(zu * dz, axis=0, keepdims=True)
            dcw_ref[2:3, cs] += jnp.sum(zu * ddn, axis=0, keepdims=True)
            dcb_ref[:, cs] += jnp.sum(dz, axis=0, keepdims=True)
            dh = dh + _dg(dzu, w_ref[:, cs], 1, 1)
        _, vjp = jax.vjp(rmsnorm, x_ref[...], g_ref[...])
        dx, dg = vjp(dh)
        dx_ref[...] = dr_ref[...] + dx
        dg_ref[...] += dg

    return _call(body, "ffn_up_bwd", (ns,),
                 _halo_specs(T, tm, NUP) + [_rows(tm, NUP), _full(cw), _full(w_up), _rows(tm, D), _full(g), _rows(tm, D)],
                 [_rows(tm, NUP), _rows(tm, D), _full(g), _full(cw), pl.BlockSpec((1, NUP), lambda i: (0, 0))],
                 [_sds((T, NUP), MX), _sds((T, D)), _sds((1, D)), _sds((3, NUP)), _sds((1, NUP))])(
                     dz, dz, dz, zu, cw, w_up, x1, g, dres)


def nt_normbwd(dys, w, x, g, dres, tm, name):
    T = x.shape[0]
    n = len(dys)
    offs = [sum(d.shape[1] for d in dys[:i]) for i in range(n + 1)]

    def body(*refs):
        dy_refs, (w_ref, x_ref, g_ref, dr_ref, dx_ref, dg_ref) = refs[:n], refs[n:]

        @pl.when(pl.program_id(0) == 0)
        def _():
            dg_ref[...] = jnp.zeros_like(dg_ref)

        dh = _dg(dy_refs[0][...], w_ref[:, offs[0]:offs[1]], 1, 1)
        for i in range(1, n):
            dh = dh + _dg(dy_refs[i][...], w_ref[:, offs[i]:offs[i + 1]], 1, 1)
        _, vjp = jax.vjp(rmsnorm, x_ref[...], g_ref[...])
        dx, dg = vjp(dh)
        dx_ref[...] = dr_ref[...] + dx
        dg_ref[...] += dg

    return _call(body, name, (T // tm,),
                 [_rows(tm, d.shape[1]) for d in dys] + [_full(w), _rows(tm, D), _full(g), _rows(tm, D)],
                 [_rows(tm, D), _full(g)], [_sds((T, D)), _sds((1, D))])(*dys, w, x, g, dres)


def matmul_tn(a, b, tt, tn, name):
    T, k = a.shape
    n = b.shape[1]
    last = T // tt - 1

    def body(a_ref, b_ref, o_ref, acc_ref):
        @pl.when(pl.program_id(1) == 0)
        def _():
            acc_ref[...] = jnp.zeros_like(acc_ref)

        acc_ref[...] += _dg(a_ref[...], b_ref[...], 0, 0)

        @pl.when(pl.program_id(1) == last)
        def _():
            o_ref[...] = acc_ref[...].astype(MX)

    return _call(body, name, (n // tn, T // tt),
                 [pl.BlockSpec((tt, k), lambda j, i: (i, 0)), pl.BlockSpec((tt, tn), lambda j, i: (i, j))],
                 pl.BlockSpec((k, tn), lambda j, i: (0, j)), _sds((k, n), MX), scratch=[pltpu.VMEM((k, tn), F32)],
                 n_axes=2)(a, b)


def mix_out_bwd(dx1, w_out, of, ob, p, gg, tm):
    T = dx1.shape[0]

    def body(dx_ref, w_ref, of_ref, ob_ref, pg_ref, gg_ref, da_ref, do_ref, dpg_ref, dgg_ref):
        @pl.when(pl.program_id(0) == 0)
        def _():
            dgg_ref[...] = jnp.zeros_like(dgg_ref)

        dxb = dx_ref[...].astype(MX)
        da_ref[...] = _dg(dxb, w_ref[0:512, :], 1, 1)
        for h in range(4):
            sl = slice(h * 128, (h + 1) * 128)
            dm = _dg(dxb, w_ref[512 + h * 128:512 + (h + 1) * 128, :], 1, 1)
            _, vjp = jax.vjp(outb_head, of_ref[:, sl] + ob_ref[:, sl], pg_ref[:, sl], gg_ref[h])
            do, dpg, dg = vjp(dm)
            do_ref[:, sl] = do
            dpg_ref[:, sl] = dpg
            dgg_ref[h] += dg

    return _call(body, "mix_out_bwd", (T // tm,),
                 [_rows(tm, D), _full(w_out), _rows(tm, 512), _rows(tm, 512), _rows(tm, 512, 4), _full(gg)],
                 [_rows(tm, 512), _rows(tm, 512), _rows(tm, 512), _full(gg)],
                 [_sds((T, 512)), _sds((T, 512)), _sds((T, 512)), _sds(gg.shape)])(dx1, w_out, of, ob, p, gg)


def gla_bwd(p, wg, bg, ss, do, reverse, merge=None):
    T = p.shape[0]
    tm = CPB * CH
    n = T // tm
    rev = not reverse
    rn = n if rev else None

    def body(*refs):
        q_ref, k_ref, v_ref, r_ref, wg_ref, bg_ref, ss_ref, do_ref = refs[:8]
        if merge is None:
            dq_ref, dk_ref, dv_ref, dr_ref, dwg_ref, dbg_ref, dst_ref = refs[8:]
        else:
            mq_ref, mk_ref, mv_ref, mr_ref, mg_ref, out_ref, dwg_ref, dbg_ref, dst_ref, drs_ref = refs[8:]
            out_ref[:, 1024:1536] = mg_ref[...].astype(MX)

        @pl.when(pl.program_id(0) == 0)
        def _():
            dst_ref[...] = jnp.zeros_like(dst_ref)
            dwg_ref[...] = jnp.zeros_like(dwg_ref)
            dbg_ref[...] = jnp.zeros_like(dbg_ref)

        consts = _tri(reverse) + _lane_masks()
        order = list(reversed(range(CPB))) if reverse else list(range(CPB))
        for j in range(2):
            sl = slice(j * 128, (j + 1) * 128)
            v0s, v1s = slice(256 * j, 256 * j + 128), slice(256 * j + 128, 256 * j + 256)
            chunks, dout = [], []
            for c in order:
                rows = _chunk(c)
                chunks += [r_ref[rows, :], q_ref[rows, sl], k_ref[rows, sl], v_ref[rows, v0s], v_ref[rows, v1s]]
                dout += [do_ref[rows, v0s], do_ref[rows, v1s]]
            _, vjp = jax.vjp(functools.partial(gla_pair, consts), wg_ref[:, sl], bg_ref[:, sl],
                             ss_ref[0, 2 * j], ss_ref[0, 2 * j + 1], *chunks)
            g = vjp((*dout, dst_ref[2 * j], dst_ref[2 * j + 1]))
            dwg_ref[:, sl] += g[0]
            dbg_ref[:, sl] += g[1]
            dst_ref[2 * j] = g[2]
            dst_ref[2 * j + 1] = g[3]
            for i, c in enumerate(order):
                rows = _chunk(c)
                dr, dq, dk, dv0, dv1 = g[4 + 5 * i:9 + 5 * i]
                if merge is None:
                    if j == 0:
                        dr_ref[rows, :] = dr
                    else:
                        dr_ref[rows, :] += dr
                    dq_ref[rows, sl] = dq
                    dk_ref[rows, sl] = dk
                    dv_ref[rows, v0s] = dv0
                    dv_ref[rows, v1s] = dv1
                else:
                    if j == 0:
                        drs_ref[rows, :] = mr_ref[rows, :] + dr
                    else:
                        out_ref[rows, 1536:1664] = (drs_ref[rows, :] + dr).astype(MX)
                    out_ref[rows, sl] = (mq_ref[rows, sl] + dq).astype(MX)
                    out_ref[rows, 256 + 128 * j:384 + 128 * j] = (mk_ref[rows, sl] + dk).astype(MX)
                    out_ref[rows, 512 + 256 * j:640 + 256 * j] = (mv_ref[rows, v0s] + dv0).astype(MX)
                    out_ref[rows, 640 + 256 * j:768 + 256 * j] = (mv_ref[rows, v1s] + dv1).astype(MX)

    ss_spec = pl.BlockSpec((1, 4, 128, 128), (lambda i: (n - 1 - i, 0, 0, 0)) if rev else (lambda i: (i, 0, 0, 0)))
    ins = [p, p, p, p, wg, bg, ss, do]
    in_specs = _gla_in_specs(tm, n, rev) + [_full(wg), _full(bg), ss_spec, _rows(tm, 512, 0, rn)]
    scratch = [pltpu.VMEM((4, 128, 128), F32)]
    if merge is None:
        out_specs = [_rows(tm, 256, 0, rn), _rows(tm, 256, 0, rn), _rows(tm, 512, 0, rn), _rows(tm, 128, 0, rn)]
        out_shape = [_sds((T, 256)), _sds((T, 256)), _sds((T, 512)), _sds((T, 128))]
    else:
        ins += list(merge)
        in_specs += [_rows(tm, a.shape[1], 0, rn) for a in merge]
        out_specs, out_shape = [_rows(tm, 1664, 0, rn)], [_sds((T, 1664), MX)]
        scratch.append(pltpu.VMEM((tm, 128), F32))
    return _call(body, "gla_bwd_r" if reverse else "gla_bwd_f", (n,), in_specs, out_specs + [_full(wg), _full(bg)],
                 out_shape + [_sds(wg.shape), _sds(bg.shape)], scratch=scratch)(*ins)


def gmlp_bwd(p, douta, ws, bs, lg, lb):
    T = p.shape[0]
    cpb = 4
    tm = cpb * CH

    def body(pa_ref, do_ref, ws_ref, bs_ref, lg_ref, lb_ref, dpa_ref, dws_ref, dbs_ref, dlg_ref, dlb_ref):
        @pl.when(pl.program_id(0) == 0)
        def _():
            for r in (dws_ref, dbs_ref, dlg_ref, dlb_ref):
                r[...] = jnp.zeros_like(r)

        us = [slice(h * 128, (h + 1) * 128) for h in range(4)]
        vs = [slice(512 + h * 128, 512 + (h + 1) * 128) for h in range(4)]
        params = [(ws_ref[h], bs_ref[h], lg_ref[h], lb_ref[h]) for h in range(4)]
        pieces = [[(pa_ref[_chunk(c), us[h]], pa_ref[_chunk(c), vs[h]]) for h in range(4)] for c in range(cpb)]
        _, vjp = jax.vjp(gmlp_heads, params, pieces)
        dparams, dpieces = vjp([[do_ref[_chunk(c), us[h]] for h in range(4)] for c in range(cpb)])
        for h in range(4):
            for r, a in zip((dws_ref, dbs_ref, dlg_ref, dlb_ref), dparams[h]):
                r[h] += a
            for c in range(cpb):
                dpa_ref[_chunk(c), us[h]] = dpieces[c][h][0].astype(MX)
                dpa_ref[_chunk(c), vs[h]] = dpieces[c][h][1].astype(MX)

    return _call(body, "gmlp_bwd", (T // tm,),
                 [_rows(tm, 1024), _rows(tm, 512), _full(ws), _full(bs), _full(lg), _full(lb)],
                 [_rows(tm, 1024), _full(ws), _full(bs), _full(lg), _full(lb)],
                 [_sds((T, 1024), MX), _sds(ws.shape), _sds(bs.shape), _sds(lg.shape), _sds(lb.shape)])(p, douta, ws, bs, lg, lb)


def _gate_pad(w, row0):
    return jnp.zeros((128, 256), F32).at[row0:row0 + 16].set(w)


def local_step(x, tgt, W, get_big, emit, tm=256, tmm=512):
    saved = []
    for l in range(NL):
        s = {"x": x}
        s["w_in"] = get_big(l, 0, x)
        p, s["h"] = norm_matmul(x, W["g_mix"][l][None], s["w_in"], tmm, "mix_in")
        s["p"] = p
        ws, bs = W["w_s"][l], W["b_s"][l][:, :, None]
        lg, lb = W["ln_g"][l][:, None, :], W["ln_b"][l][:, None, :]
        outa = gmlp_fwd(p, ws, bs, lg, lb)
        wgf, wgb = _gate_pad(W["w_gate_f"][l], 0), _gate_pad(W["w_gate_b"][l], 16)
        bgf, bgb = W["b_gate_f"][l][None], W["b_gate_b"][l][None]
        s["of"], s["ssf"] = gla_fwd(p, wgf, bgf, False)
        s["ob"], s["ssb"] = gla_fwd(p, wgb, bgb, True)
        s["w_out"] = get_big(l, 1, s["ob"])
        gg = W["g_gla"][l][:, None, :]
        x1, s["mixed"] = mix_out(x, s["of"], s["ob"], p, outa, gg, s["w_out"], tmm)
        s["x1"] = x1
        s["w_up"] = get_big(l, 2, x1)
        s["zu"], s["h2"], s["z"], s["a"] = ffn_up_conv(x1, W["g_ffn"][l][None], s["w_up"], W["conv_w"][l],
                                                       W["conv_b"][l][None], tm // 2)
        s["w_down"] = get_big(l, 3, s["a"])
        x = matmul_res(s["a"], s["w_down"], x1, tmm, "ffn_down")
        saved.append(s)

    lsum, dx, dgf = loss_head(x, W["g_final"][None], tgt, tmm)
    G = {k: [None] * NL for k in _SMALL if k != "g_final"}
    tok = jnp.zeros((1, 1), F32)
    for l in reversed(range(NL)):
        s = saved[l]
        g_down = matmul_tn(s["a"], dx, min(1024, tmm * 2), 512, "dw_down")
        dz = ffn_down_bwd(dx, s["z"], s["w_down"], tm)
        dzu, dx1, dg, G["conv_w"][l], dcb = ffn_up_bwd(dz, s["zu"], W["conv_w"][l] + tok, s["w_up"], s["x1"],
                                                       W["g_ffn"][l][None], dx, tm)
        G["conv_b"][l], G["g_ffn"][l] = dcb[0], dg[0]
        g_up = matmul_tn(s["h2"], dzu, min(1024, tmm * 2), 1408, "dw_up")
        tok = emit(l, "A", {"w_down": g_down, "w_up": g_up})
        g_out = matmul_tn(s["mixed"], dx1, min(1024, tmm * 2), 1024, "dw_out")
        gg = W["g_gla"][l][:, None, :] + tok
        douta, do, dpg, dgg = mix_out_bwd(dx1, s["w_out"], s["of"], s["ob"], s["p"], gg, tmm)
        G["g_gla"][l] = dgg[:, 0, :]
        wgf, wgb = _gate_pad(W["w_gate_f"][l], 0), _gate_pad(W["w_gate_b"][l], 16)
        bgf, bgb = W["b_gate_f"][l][None], W["b_gate_b"][l][None]
        dqf, dkf, dvf, drf, dwgf, dbgf = gla_bwd(s["p"], wgf, bgf, s["ssf"], do, False)
        dpb, dwgb, dbgb = gla_bwd(s["p"], wgb, bgb, s["ssb"], do, True, merge=(dqf, dkf, dvf, drf, dpg))
        G["w_gate_f"][l], G["b_gate_f"][l] = dwgf[0:16], dbgf[0]
        G["w_gate_b"][l], G["b_gate_b"][l] = dwgb[16:32], dbgb[0]
        ws, bs = W["w_s"][l], W["b_s"][l][:, :, None]
        lg, lb = W["ln_g"][l][:, None, :], W["ln_b"][l][:, None, :]
        dpa, G["w_s"][l], dbs, dlg, dlb = gmlp_bwd(s["p"], douta, ws, bs, lg, lb)
        G["b_s"][l], G["ln_g"][l], G["ln_b"][l] = dbs[:, :, 0], dlg[:, 0, :], dlb[:, 0, :]
        tt = min(1024, tmm * 2)
        g_in = jnp.concatenate([matmul_tn(s["h"], dpa, tt, 1024, "dw_in_a"),
                                matmul_tn(s["h"], dpb, tt, 1664, "dw_in_b")], axis=1)
        tok = emit(l, "B", {"w_out": g_out, "w_in": g_in})
        dx, dg = nt_normbwd([dpa, dpb], s["w_in"], s["x"], W["g_mix"][l][None] + tok, dx1, tmm, "mix_in_bwd")
        G["g_mix"][l] = dg[0]
    G = {k: jnp.stack(v) for k, v in G.items()}
    G["g_final"] = dgf[0]
    return lsum, dx, G


def _rows3(tr, c):
    return pl.BlockSpec((None, tr, c), lambda l, i: (l, i, 0))


def cast_bf16(a, tr):
    nl, r, c = a.shape

    def body(a_ref, o_ref):
        o_ref[...] = a_ref[...].astype(BF16)

    return _call(body, "cast_bf16", (nl, r // tr), [_rows3(tr, c)], _rows3(tr, c), _sds(a.shape, BF16), n_axes=2)(a)


def sum_parts(land, grad, k, chipvec, tr):
    _, rr, cc = land.shape
    nb = rr // tr

    def body(c_ref, l_ref, g_ref, o_ref):
        mine = g_ref[...].astype(F32)
        acc = None
        for j in range(4):
            part = jnp.where(c_ref[0] == j, mine, l_ref[j].astype(F32))
            acc = part if acc is None else acc + part
        o_ref[...] = acc

    gs = pltpu.PrefetchScalarGridSpec(
        num_scalar_prefetch=1, grid=(nb,),
        in_specs=[pl.BlockSpec((4, tr, cc), lambda i, c: (0, i, 0)), _part_spec(k, tr, nb)],
        out_specs=pl.BlockSpec((tr, cc), lambda i, c: (i, 0)))
    return pl.pallas_call(body, name="sum_parts", grid_spec=gs, out_shape=_sds((rr, cc)),
                          compiler_params=_cparams(1))(chipvec, land, grad)


def adamw(w, ga, gb, m, v, tr):
    nl, r, c = w.shape

    def body(w_ref, ga_ref, gb_ref, m_ref, v_ref, g_ref, d_ref, nm_ref, nv_ref):
        gr = ga_ref[...] + gb_ref[...]
        g_ref[...] = gr
        nm = ADAM_B1 * m_ref[...] + (1.0 - ADAM_B1) * gr
        nv = ADAM_B2 * v_ref[...] + (1.0 - ADAM_B2) * jnp.square(gr)
        m_hat = nm * (1.0 / (1.0 - ADAM_B1 ** ADAM_STEP))
        v_hat = nv * (1.0 / (1.0 - ADAM_B2 ** ADAM_STEP))
        d_ref[...] = -ADAM_LR * (m_hat / (jnp.sqrt(v_hat) + ADAM_EPS) + ADAM_WD * w_ref[...])
        nm_ref[...] = nm
        nv_ref[...] = nv

    sp = _rows3(tr, c)
    return _call(body, "adamw", (nl, r // tr), [sp] * 5, [sp] * 4, [_sds(w.shape)] * 4, n_axes=2)(w, ga, gb, m, v)


MESH = pl.DeviceIdType.MESH
ANY = pl.BlockSpec(memory_space=pl.ANY)
N_BIG = 4


def _pos():
    return lax.axis_index("x"), lax.axis_index("y"), lax.axis_index("c")


def _other_chips(x, y):
    return [(1 - x, y), (x, 1 - y), (1 - x, 1 - y)]


def _slab(k, ref, j):
    if k == 0:
        return ref.at[j]
    if k == 1:
        return ref.at[pl.ds(256 * j, 256), :]
    if k == 2:
        return ref.at[:, pl.ds(1408 * j, 1408)]
    return ref.at[pl.ds(704 * j, 704), :]


_LAYER_FULL = [(4, 1024, 648), (1024, 1024), (1024, NUP), (DFF, 1024)]
_LAYER_SHARD = [(1024, 648), (256, 1024), (1024, 1408), (704, 1024)]

HBM = pl.BlockSpec(memory_space=pltpu.HBM)
SEM = pl.BlockSpec(memory_space=pltpu.SEMAPHORE)
VM = pl.BlockSpec(memory_space=pltpu.VMEM)
EFFECT = pltpu.SideEffectType.DATAFLOW_SIDE_EFFECTING
_GW_GROUPS = [[(0, k)] for k in range(N_BIG)] + [[(l, k) for k in range(N_BIG)] for l in range(1, NL)]
_GW_ORDER = [lk for g in _GW_GROUPS for lk in g]


def _hbm(a):
    return pltpu.with_memory_space_constraint(a, pltpu.HBM)


def _hbm_like(a):
    return pltpu.HBM(a.shape, a.dtype)


def _part_spec(k, tr, nb):
    cc = _LAYER_SHARD[k][1]
    if k == 0:
        return pl.BlockSpec((None, tr, cc), lambda i, c: (c[0], i, 0))
    if k == 2:
        return pl.BlockSpec((tr, cc), lambda i, c: (i, c[0]))
    return pl.BlockSpec((tr, cc), lambda i, c: (c[0] * nb + i, 0))


def place_own(shard, landing, l, k, chipvec, tr):
    rr, cc = _LAYER_SHARD[k]
    nb = rr // tr

    def body(c_ref, s_ref, l_ref, o_ref):
        o_ref[...] = s_ref[...]

    gs = pltpu.PrefetchScalarGridSpec(
        num_scalar_prefetch=1, grid=(nb,),
        in_specs=[pl.BlockSpec((None, tr, cc), lambda i, c: (l, i, 0)), ANY], out_specs=_part_spec(k, tr, nb))
    return pl.pallas_call(body, name="place_own", grid_spec=gs, out_shape=_sds(landing.shape, landing.dtype),
                          input_output_aliases={2: 0}, compiler_params=_cparams(1))(chipvec, shard, landing)


def gw_start(shards, landings, after):
    n = len(_GW_ORDER)

    def body(*refs):
        S, Ld = refs[:N_BIG], refs[N_BIG:N_BIG + n]
        outs = refs[N_BIG + n + 1:]
        send_sems, recv, token = outs[0], outs[1:1 + len(_GW_GROUPS)], outs[-1]
        x, y, c = _pos()
        me = 2 * x + y
        ci = 0
        for gi, grp in enumerate(_GW_GROUPS):
            for t, (l, k) in enumerate(grp):
                land = Ld[_GW_ORDER.index((l, k))]
                for j, (px, py) in enumerate(_other_chips(x, y)):
                    pltpu.make_async_remote_copy(
                        src_ref=S[k].at[l], dst_ref=_slab(k, land, me), send_sem=send_sems.at[ci],
                        recv_sem=recv[gi].at[3 * t + j], device_id=(px, py, c), device_id_type=MESH).start()
                    ci += 1
        token[...] = jnp.zeros_like(token)

    ins = list(shards) + list(landings)
    sems = [pltpu.SemaphoreType.DMA((3 * n,))] + [pltpu.SemaphoreType.DMA((3 * len(g),)) for g in _GW_GROUPS]
    outs = pl.pallas_call(
        body, name="gw_start", out_shape=sems + [_hbm_like(a) for a in ins] + [_sds((8, 128))],
        in_specs=[HBM] * len(ins) + [pl.BlockSpec(memory_space=pl.ANY)],
        out_specs=[SEM] * len(sems) + [HBM] * len(ins) + [VM],
        input_output_aliases={i: len(sems) + i for i in range(len(ins))},
        compiler_params=pltpu.CompilerParams(has_side_effects=EFFECT))(*[_hbm(a) for a in ins], after)
    ns = len(sems)
    return outs[0], outs[1:ns], outs[ns:ns + N_BIG], outs[ns + N_BIG:ns + len(ins)], outs[-1]


def gw_wait(gi, landings, recv_sems, after, shards=None, send_sems=None):
    grp = _GW_GROUPS[gi]
    n = len(grp)
    last = shards is not None

    def body(*refs):
        Ld, rs = refs[:n], refs[n]
        x, y, c = _pos()
        for t, (l, k) in enumerate(grp):
            for j, (px, py) in enumerate(_other_chips(x, y)):
                region = _slab(k, Ld[t], 2 * px + py)
                pltpu.make_async_remote_copy(src_ref=region, dst_ref=region, send_sem=rs.at[3 * t + j],
                                             recv_sem=rs.at[3 * t + j], device_id=(px, py, c),
                                             device_id_type=MESH).wait_recv()
        if last:
            S, ss = refs[n + 2:n + 2 + N_BIG], refs[n + 2 + N_BIG]
            me = 2 * x + y
            for ci, (l, k) in enumerate(lk for lk in _GW_ORDER for _ in range(3)):
                pltpu.make_async_remote_copy(src_ref=S[k].at[l], dst_ref=_slab(k, Ld[k], me), send_sem=ss.at[ci],
                                             recv_sem=ss.at[ci], device_id=(x, y, c), device_id_type=MESH).wait_send()

    ins = list(landings) + [recv_sems, after]
    specs = [HBM] * n + [SEM, pl.BlockSpec(memory_space=pl.ANY)]
    outs = [_hbm_like(a) for a in landings]
    alias = {i: i for i in range(n)}
    if last:
        ins += list(shards) + [send_sems]
        specs += [HBM] * N_BIG + [SEM]
        outs += [_hbm_like(a) for a in shards]
        alias.update({n + 2 + i: n + i for i in range(N_BIG)})
    res = pl.pallas_call(body, name="gw_wait_%d" % gi, out_shape=outs, in_specs=specs, out_specs=[HBM] * len(outs),
                         input_output_aliases=alias,
                         compiler_params=pltpu.CompilerParams(has_side_effects=EFFECT))(*ins)
    return res[:n], (res[n:] if last else None)


def ga_start(tag, ks, grads, landings):
    n = len(ks)

    def body(*refs):
        G, Ld = refs[:n], refs[n:2 * n]
        send_sems, recv_sems, token = refs[2 * n], refs[2 * n + 1], refs[-1]
        x, y, c = _pos()
        me = 2 * x + y
        for t, k in enumerate(ks):
            for j, (px, py) in enumerate(_other_chips(x, y)):
                pltpu.make_async_remote_copy(
                    src_ref=_slab(k, G[t], 2 * px + py), dst_ref=Ld[t].at[me], send_sem=send_sems.at[3 * t + j],
                    recv_sem=recv_sems.at[3 * t + j], device_id=(px, py, c), device_id_type=MESH).start()
        token[...] = jnp.zeros_like(token)

    ins = list(grads) + list(landings)
    sems = [pltpu.SemaphoreType.DMA((3 * n,))] * 2
    outs = pl.pallas_call(
        body, name="ga_start_" + tag, out_shape=sems + [_hbm_like(a) for a in ins] + [_sds((8, 128))],
        in_specs=[HBM] * len(ins), out_specs=[SEM, SEM] + [HBM] * len(ins) + [VM],
        input_output_aliases={i: 2 + i for i in range(len(ins))},
        compiler_params=pltpu.CompilerParams(has_side_effects=EFFECT))(*[_hbm(a) for a in ins])
    return outs[0], outs[1], outs[2:2 + n], outs[2 + n:2 + 2 * n], outs[-1]


def ga_wait(tag, ks, send_sems, recv_sems, grads, landings, after):
    n = len(ks)

    def body(*refs):
        G, Ld, ss, rs = refs[:n], refs[n:2 * n], refs[2 * n], refs[2 * n + 1]
        x, y, c = _pos()
        me = 2 * x + y
        for t, k in enumerate(ks):
            for j, (px, py) in enumerate(_other_chips(x, y)):
                pj = 2 * px + py
                cp = pltpu.make_async_remote_copy(
                    src_ref=_slab(k, G[t], pj), dst_ref=Ld[t].at[pj], send_sem=ss.at[3 * t + j],
                    recv_sem=rs.at[3 * t + j], device_id=(px, py, c), device_id_type=MESH)
                cp.wait_send()
                cp.wait_recv()

    ins = list(grads) + list(landings) + [send_sems, recv_sems, after]
    res = pl.pallas_call(
        body, name="ga_wait_" + tag, out_shape=[_hbm_like(a) for a in list(grads) + list(landings)],
        in_specs=[HBM] * (2 * n) + [SEM, SEM, pl.BlockSpec(memory_space=pl.ANY)], out_specs=[HBM] * (2 * n),
        input_output_aliases={i: i for i in range(2 * n)},
        compiler_params=pltpu.CompilerParams(has_side_effects=EFFECT))(*ins)
    return res[:n], res[n:]


def swap_start(parts):
    n = len(parts)

    def body(*refs):
        Q, Ld, sems = refs[:n], refs[n:2 * n], refs[2 * n:4 * n]
        x, y, c = _pos()
        for k in range(n):
            pltpu.make_async_remote_copy(src_ref=Q[k], dst_ref=Ld[k], send_sem=sems[k].at[0], recv_sem=sems[n + k].at[0],
                                         device_id=(x, y, 1 - c), device_id_type=MESH).start()
        refs[-1][...] = jnp.zeros_like(refs[-1])

    ins = list(parts) + [lax.empty(p.shape, p.dtype) for p in parts]
    outs = pl.pallas_call(
        body, name="swap_start",
        out_shape=[pltpu.SemaphoreType.DMA((1,))] * (2 * n) + [_hbm_like(a) for a in ins] + [_sds((8, 128))],
        in_specs=[HBM] * (2 * n), out_specs=[SEM] * (2 * n) + [HBM] * (2 * n) + [VM],
        input_output_aliases={i: 2 * n + i for i in range(2 * n)},
        compiler_params=pltpu.CompilerParams(has_side_effects=EFFECT))(*[_hbm(a) for a in ins])
    return outs[:n], outs[n:2 * n], outs[2 * n:3 * n], outs[3 * n:4 * n]


def swap_wait(k, send_sem, recv_sem, part, landing, after):
    def body(q_ref, l_ref, ss, rs, after_ref, q_out, l_out):
        x, y, c = _pos()
        cp = pltpu.make_async_remote_copy(src_ref=q_ref, dst_ref=l_ref, send_sem=ss.at[0], recv_sem=rs.at[0],
                                          device_id=(x, y, 1 - c), device_id_type=MESH)
        cp.wait_send()
        cp.wait_recv()

    return pl.pallas_call(
        body, name="swap_wait_%d" % k, out_shape=[_hbm_like(part), _hbm_like(landing)],
        in_specs=[HBM, HBM, SEM, SEM, pl.BlockSpec(memory_space=pl.ANY)], out_specs=[HBM, HBM],
        input_output_aliases={0: 0, 1: 1},
        compiler_params=pltpu.CompilerParams(has_side_effects=EFFECT))(part, landing, send_sem, recv_sem, after)


def _peer(x, y, c, r):
    fx, fy, fc = (r >> 2) & 1, (r >> 1) & 1, r & 1
    return ((1 - x) if fx else x, (1 - y) if fy else y, (1 - c) if fc else c)


def ag_start(tag, pack):
    rr, cc = pack.shape

    def body(p_ref, l_ref, ss, rs, p_out, l_out, token):
        x, y, c = _pos()
        me = 4 * x + 2 * y + c
        for r in range(1, 8):
            pltpu.make_async_remote_copy(src_ref=p_ref, dst_ref=l_ref.at[me], send_sem=ss.at[r - 1], recv_sem=rs.at[r - 1],
                                         device_id=_peer(x, y, c, r), device_id_type=MESH).start()
        token[...] = jnp.zeros_like(token)

    outs = pl.pallas_call(
        body, name="ag_start_" + tag,
        out_shape=[pltpu.SemaphoreType.DMA((7,)), pltpu.SemaphoreType.DMA((7,)), _hbm_like(pack),
                   pltpu.HBM((8, rr, cc), pack.dtype), _sds((8, 128))],
        in_specs=[HBM, HBM], out_specs=[SEM, SEM, HBM, HBM, VM], input_output_aliases={0: 2, 1: 3},
        compiler_params=pltpu.CompilerParams(has_side_effects=EFFECT))(_hbm(pack), _hbm(lax.empty((8, rr, cc), pack.dtype)))
    return outs


def ag_wait(tag, send_sems, recv_sems, pack, landing, after):
    def body(p_ref, l_ref, ss, rs, after_ref, p_out, l_out):
        x, y, c = _pos()
        for r in range(1, 8):
            px, py, pc = _peer(x, y, c, r)
            cp = pltpu.make_async_remote_copy(src_ref=p_ref, dst_ref=l_ref.at[4 * px + 2 * py + pc], send_sem=ss.at[r - 1],
                                              recv_sem=rs.at[r - 1], device_id=(px, py, pc), device_id_type=MESH)
            cp.wait_send()
            cp.wait_recv()

    return pl.pallas_call(
        body, name="ag_wait_" + tag, out_shape=[_hbm_like(pack), _hbm_like(landing)],
        in_specs=[HBM, HBM, SEM, SEM, pl.BlockSpec(memory_space=pl.ANY)], out_specs=[HBM, HBM],
        input_output_aliases={0: 0, 1: 1},
        compiler_params=pltpu.CompilerParams(has_side_effects=EFFECT))(pack, landing, send_sems, recv_sems, after)


def sum_slots(landing, own, mevec):
    _, rr, cc = landing.shape

    def body(m_ref, l_ref, o_ref, out_ref):
        mine = o_ref[...]
        acc = None
        for j in range(8):
            part = jnp.where(m_ref[0] == j, mine, l_ref[j])
            acc = part if acc is None else acc + part
        out_ref[...] = acc

    gs = pltpu.PrefetchScalarGridSpec(
        num_scalar_prefetch=1, grid=(1,),
        in_specs=[pl.BlockSpec((8, rr, cc), lambda i, m: (0, 0, 0)), pl.BlockSpec((rr, cc), lambda i, m: (0, 0))],
        out_specs=pl.BlockSpec((rr, cc), lambda i, m: (0, 0)))
    return pl.pallas_call(body, name="sum_slots", grid_spec=gs, out_shape=_sds((rr, cc)),
                          compiler_params=_cparams(1))(mevec, landing, own)


_WEIGHTS = ['g_mix', 'w_in', 'w_s', 'b_s', 'ln_g', 'ln_b', 'w_gate_f', 'b_gate_f', 'w_gate_b', 'b_gate_b', 'g_gla',
            'w_out', 'g_ffn', 'w_up', 'conv_w', 'conv_b', 'w_down', 'g_final']
_BIG = ['w_in', 'w_out', 'w_up', 'w_down']
_SMALL = [n for n in _WEIGHTS if n not in _BIG]
_SMALL_SHARDED = {'w_gate_f': 64, 'w_gate_b': 64, 'conv_w': 1408}
_BIG_TR = {'w_in': 512, 'w_out': 256, 'w_up': 256, 'w_down': 352}


def _pack(arrs):
    flat = jnp.concatenate([a.reshape(-1) for a in arrs])
    pad = (-flat.shape[0]) % 1024
    return jnp.pad(flat, (0, pad)).reshape(-1, 128)


def _unpack(buf, shapes):
    flat = buf.reshape(-1)
    out, o = [], 0
    for s in shapes:
        n = 1
        for d in s:
            n *= d
        out.append(flat[o:o + n].reshape(s))
        o += n
    return out


def kernel(x, g_mix, w_in, w_s, b_s, ln_g, ln_b, w_gate_f, b_gate_f, w_gate_b, b_gate_b, g_gla, w_out, g_ffn, w_up, conv_w, conv_b, w_down, g_final, loss_target, m_g_mix, m_w_in, m_w_s, m_b_s, m_ln_g, m_ln_b, m_w_gate_f, m_b_gate_f, m_w_gate_b, m_b_gate_b, m_g_gla, m_w_out, m_g_ffn, m_w_up, m_conv_w, m_conv_b, m_w_down, m_g_final, v_g_mix, v_w_in, v_w_s, v_b_s, v_ln_g, v_ln_b, v_w_gate_f, v_b_gate_f, v_w_gate_b, v_b_gate_b, v_g_gla, v_w_out, v_g_ffn, v_w_up, v_conv_w, v_conv_b, v_w_down, v_g_final):
    loc = locals()
    w = {n: loc[n] for n in _WEIGHTS}
    m = {n: loc["m_" + n] for n in _WEIGHTS}
    v = {n: loc["v_" + n] for n in _WEIGHTS}
    xi, yi, ci = _pos()
    chip = 2 * xi + yi
    me = 2 * chip + ci
    mevec = jnp.reshape(me, (1,)).astype(jnp.int32)

    sh_names = list(_SMALL_SHARDED)
    ss_w, rs_w, pk_w, land_w, tok_w = ag_start("w", _pack([w[n] for n in sh_names]))

    shards = [cast_bf16(w[n], _BIG_TR[n]) for n in _BIG]
    chipvec = jnp.reshape(chip, (1,)).astype(jnp.int32)
    send_sems, recv_sems, shards_fly, landings_fly, started = gw_start(
        shards, [lax.empty(_LAYER_FULL[k], BF16) for _, k in _GW_ORDER], tok_w)
    own = {"shards": shards_fly}

    pk_w, land_w = ag_wait("w", ss_w, rs_w, pk_w, land_w, started)
    per_chip = [_unpack(jnp.where(me == 2 * j, pk_w, land_w[2 * j]), [w[n].shape for n in sh_names]) for j in range(4)]
    W = dict(w)
    for k, n in enumerate(sh_names):
        W[n] = jnp.concatenate([per_chip[j][k] for j in range(4)], axis=-1)
    arrived = {}

    def get_big(l, k, after):
        if (l, k) not in arrived:
            gi = next(i for i, g in enumerate(_GW_GROUPS) if (l, k) in g)
            lo = sum(len(g) for g in _GW_GROUPS[:gi])
            lands = landings_fly[lo:lo + len(_GW_GROUPS[gi])]
            if gi == len(_GW_GROUPS) - 1:
                full, own["shards"] = gw_wait(gi, lands, recv_sems[gi], after, shards_fly, send_sems)
            else:
                full, _ = gw_wait(gi, lands, recv_sems[gi], after)
            for (gl, gk), a in zip(_GW_GROUPS[gi], full):
                arrived[(gl, gk)] = place_own(own["shards"][gk], a, gl, gk, chipvec, _BIG_TR[_BIG[gk]])
        if k == 0:
            f_in = jnp.transpose(arrived[(l, 0)], (1, 0, 2)).reshape(D, N_IN)
            return jnp.pad(f_in, ((0, 0), (0, N_INP - N_IN)))
        return arrived[(l, k)]

    flying = []

    def emit(l, group, grads):
        ks = [3, 2] if group == "A" else [1, 0]
        gs = [grads[_BIG[k]] for k in ks]
        if group == "B":
            gs[1] = jnp.transpose(gs[1][:, :N_IN].reshape(D, 4, 648), (1, 0, 2))
        lands = [lax.empty((4,) + _LAYER_SHARD[k], BF16) for k in ks]
        tag = "%d%s" % (l, group)
        ss, rs, gs_fly, lands_fly, tok = ga_start(tag, ks, gs, lands)
        flying.append((tag, l, ks, ss, rs, gs_fly, lands_fly))
        return tok[0:1, 0:1]

    lsum, grad_x, G = local_step(x[0], loss_target[0], W, get_big, emit)

    small_shapes = [G[n].shape for n in _SMALL] + [(D,)]
    ss_g, rs_g, pk_g, land_g, started = ag_start("g", _pack([G[n] for n in _SMALL] + [lsum]))

    plane = [[None] * NL for _ in range(N_BIG)]
    for tag, l, ks, ss, rs, gs_fly, lands_fly in flying:
        for k, g, a in zip(ks, *ga_wait(tag, ks, ss, rs, gs_fly, lands_fly, started)):
            plane[k][l] = sum_parts(a, g, k, chipvec, _BIG_TR[_BIG[k]])
    ss_p, rs_p, plane_fly, other_fly = swap_start([jnp.stack(p) for p in plane])
    grads, delta, new_m, new_v = {}, {}, {}, {}
    after = grad_x
    for k in (1, 0, 3, 2):
        n = _BIG[k]
        mine, other = swap_wait(k, ss_p[k], rs_p[k], plane_fly[k], other_fly[k], after)
        grads[n], delta[n], new_m[n], new_v[n] = adamw(w[n], mine, other, m[n], v[n], _BIG_TR[n])
        after = delta[n]

    pk_g, land_g = ag_wait("g", ss_g, rs_g, pk_g, land_g, after)
    small = dict(zip(_SMALL + ["lsum"], _unpack(sum_slots(land_g, pk_g, mevec), small_shapes)))
    loss = 0.5 * jnp.sum(small.pop("lsum")) / D
    for n, wd in _SMALL_SHARDED.items():
        small[n] = lax.dynamic_slice_in_dim(small[n], chip * wd, wd, axis=small[n].ndim - 1)
    grads.update(small)
    shapes = [w[n].shape for n in _SMALL]
    pw, pg, pm, pv = (_pack([t[n] for n in _SMALL])[None] for t in (w, grads, m, v))
    _, d_, m_, v_ = adamw(pw, pg, jnp.zeros_like(pg), pm, pv, pw.shape[1])
    for t, buf in ((delta, d_), (new_m, m_), (new_v, v_)):
        t.update(zip(_SMALL, _unpack(buf, shapes)))

    return (loss, grad_x[None], *[grads[n] for n in _WEIGHTS], *[delta[n] for n in _WEIGHTS],
            *[new_m[n] for n in _WEIGHTS], *[new_v[n] for n in _WEIGHTS])
```

```python
import functools

import jax
import jax.numpy as jnp
from jax import lax
from jax.experimental import pallas as pl
from jax.experimental.pallas import tpu as pltpu

F32 = jnp.float32
BF16 = jnp.bfloat16
MX = BF16

D = 1024
CH = 128
NL = 4
N_IN = 2592
N_INP = 2688
NUP = 5632
DFF = 2816
EPS = 1e-6
VMEM_LIMIT = 56 * 1024 * 1024

ADAM_LR, ADAM_B1, ADAM_B2, ADAM_EPS, ADAM_WD, ADAM_STEP = 0.001, 0.9, 0.999, 1e-08, 0.01, 10


def _dg(a, b, ca, cb):
    return lax.dot_general(a.astype(MX), b.astype(MX), (((ca,), (cb,)), ((), ())), preferred_element_type=F32)


@jax.custom_vjp
def mm(a, b):
    return _dg(a, b, 1, 0)


mm.defvjp(lambda a, b: (_dg(a, b, 1, 0), (a, b)),
          lambda r, g: (_dg(g, r[1], 1, 1), _dg(r[0], g, 0, 0)))


@jax.custom_vjp
def mm_nt(a, b):
    return _dg(a, b, 1, 1)


mm_nt.defvjp(lambda a, b: (_dg(a, b, 1, 1), (a, b)),
             lambda r, g: (_dg(g, r[1], 1, 0), _dg(g, r[0], 0, 0)))


@jax.custom_vjp
def mm_tn(a, b):
    return _dg(a, b, 0, 0)


mm_tn.defvjp(lambda a, b: (_dg(a, b, 0, 0), (a, b)),
             lambda r, g: (_dg(r[1], g, 1, 1), _dg(r[0], g, 1, 0)))


def _split3(x):
    hi = x.astype(BF16)
    r1 = x - hi.astype(F32)
    mid = r1.astype(BF16)
    lo = (r1 - mid.astype(F32)).astype(BF16)
    return hi, mid, lo


def _dot3(m, x):
    hi, mid, lo = _split3(x)
    d = lambda p: lax.dot_general(m, p, (((1,), (0,)), ((), ())), preferred_element_type=F32)
    return d(hi) + d(mid) + d(lo)


@jax.custom_vjp
def cumdot(m, mt, x):
    return _dot3(m, x)


cumdot.defvjp(lambda m, mt, x: (_dot3(m, x), (m, mt)),
              lambda r, g: (jnp.zeros_like(r[0]), jnp.zeros_like(r[1]), _dot3(r[1], g)))


def rmsnorm(x, g):
    return x * lax.rsqrt(jnp.mean(x * x, axis=-1, keepdims=True) + EPS) * g


def gelu(x):
    return 0.5 * x * (1.0 + lax.erf(x * 0.7071067811865476))


def sigmoid(x):
    return 1.0 / (1.0 + jnp.exp(-x))


def log_sigmoid(x):
    return jnp.minimum(x, 0.0) - jnp.log(1.0 + jnp.exp(-jnp.abs(x)))


def gmlp_heads(params, pieces):
    u = [[gelu(p[0]) for p in ch] for ch in pieces]
    v = [[gelu(p[1]) for p in ch] for ch in pieces]
    mu = [[jnp.mean(x, axis=-1, keepdims=True) for x in ch] for ch in v]
    var = [[jnp.mean(jnp.square(x - m), axis=-1, keepdims=True) for x, m in zip(cv, cm)] for cv, cm in zip(v, mu)]
    vn = [[(x - m) * lax.rsqrt(s + EPS) * pr[2] + pr[3] for x, m, s, pr in zip(cv, cm, cs, params)]
          for cv, cm, cs in zip(v, mu, var)]
    mix = [[mm(pr[0], x) + pr[1] for x, pr in zip(ch, params)] for ch in vn]
    return [[a * b for a, b in zip(cu, cx)] for cu, cx in zip(u, mix)]


def outb_head(o, pg, g):
    return rmsnorm(o, g) * (pg * sigmoid(pg))


def ffn_act(zg, zv):
    return zg * sigmoid(zg) * zv


def _tri(reverse):
    r = lax.broadcasted_iota(jnp.int32, (CH, CH), 0)
    c = lax.broadcasted_iota(jnp.int32, (CH, CH), 1)
    if reverse:
        cm, sm = c >= r, c > r
    else:
        cm, sm = c <= r, c <= r
    one = jnp.ones((), BF16)
    zero = jnp.zeros((), BF16)
    return jnp.where(cm, one, zero), jnp.where(cm.T, one, zero), sm


def gla_pair(consts, wg, bg, st0, st1, *chunks):
    m, mt, smask, lm0, lm1 = consts
    ch = [chunks[5 * i:5 * i + 5] for i in range(len(chunks) // 5)]
    la = [log_sigmoid(mm(c[0], wg) + bg) * (1.0 / 16.0) for c in ch]
    cum = [cumdot(m, mt, x) for x in la]
    tot = [jnp.sum(x, axis=0, keepdims=True) for x in la]
    q_dec = [(c[1] * 0.125) * jnp.exp(cm) for c, cm in zip(ch, cum)]
    k_inv = [c[2] * jnp.exp(-cm) for c, cm in zip(ch, cum)]
    k_end = [c[2] * jnp.exp(t - cm) for c, t, cm in zip(ch, tot, cum)]
    s = [[jnp.where(smask, mm_nt(qd * lm, ki), 0.0) for lm in (lm0, lm1)] for qd, ki in zip(q_dec, k_inv)]
    o_in = [[mm(si[h], c[3 + h]) for h in (0, 1)] for si, c in zip(s, ch)]
    ds = [[mm_tn(c[3 + h], ke * lm) for h, lm in ((0, lm0), (1, lm1))] for c, ke in zip(ch, k_end)]
    sts = [(st0, st1)]
    for t, d in zip(tot, ds):
        dec = jnp.exp(t)
        sts.append((sts[-1][0] * dec + d[0], sts[-1][1] * dec + d[1]))
    outs = []
    for qd, oi, st in zip(q_dec, o_in, sts):
        outs += [oi[0] + mm_nt(qd, st[0]), oi[1] + mm_nt(qd, st[1])]
    return (*outs, sts[-1][0], sts[-1][1])


def _lane_masks():
    lane = lax.broadcasted_iota(jnp.int32, (1, 128), 1)
    return (lane < 64).astype(F32), (lane >= 64).astype(F32)


def _cparams(n_axes=1):
    return pltpu.CompilerParams(dimension_semantics=("arbitrary",) * n_axes, vmem_limit_bytes=VMEM_LIMIT)


def _full(a):
    nd = a.ndim
    return pl.BlockSpec(a.shape, lambda *_: (0,) * nd)


def _rows(tm, w, cb=0, rev_n=None):
    if rev_n is None:
        return pl.BlockSpec((tm, w), lambda i: (i, cb))
    return pl.BlockSpec((tm, w), lambda i: (rev_n - 1 - i, cb))


def _call(body, name, grid, in_specs, out_specs, out_shape, scratch=(), n_axes=1):
    return pl.pallas_call(body, name=name, grid=grid, in_specs=in_specs, out_specs=out_specs, out_shape=out_shape,
                          scratch_shapes=list(scratch), compiler_params=_cparams(n_axes))


def _sds(shape, dt=F32):
    return jax.ShapeDtypeStruct(shape, dt)


def norm_matmul(x, g, w, tm, name, ydt=F32):
    T, n = x.shape[0], w.shape[1]

    def body(x_ref, g_ref, w_ref, y_ref, h_ref):
        hb = rmsnorm(x_ref[...], g_ref[...]).astype(MX)
        h_ref[...] = hb
        y_ref[...] = jnp.dot(hb, w_ref[...], preferred_element_type=F32).astype(ydt)

    return _call(body, name, (T // tm,), [_rows(tm, D), _full(g), _full(w)],
                 [_rows(tm, n), _rows(tm, D)], [_sds((T, n), ydt), _sds((T, D), MX)])(x, g, w)


CPB = 8


def _chunk(c):
    return slice(c * CH, (c + 1) * CH)


def gmlp_fwd(p, ws, bs, lg, lb):
    T = p.shape[0]
    tm = CPB * CH

    def body(pa_ref, ws_ref, bs_ref, lg_ref, lb_ref, o_ref):
        params = [(ws_ref[h], bs_ref[h], lg_ref[h], lb_ref[h]) for h in range(4)]
        pieces = [[(pa_ref[_chunk(c), h * 128:(h + 1) * 128], pa_ref[_chunk(c), 512 + h * 128:512 + (h + 1) * 128])
                   for h in range(4)] for c in range(CPB)]
        out = gmlp_heads(params, pieces)
        for c in range(CPB):
            for h in range(4):
                o_ref[_chunk(c), h * 128:(h + 1) * 128] = out[c][h].astype(MX)

    return _call(body, "gmlp_fwd", (T // tm,), [_rows(tm, 1024), _full(ws), _full(bs), _full(lg), _full(lb)],
                 _rows(tm, 512), _sds((T, 512), MX))(p, ws, bs, lg, lb)


def _gla_in_specs(tm, n, rev):
    r = n if rev else None
    return [_rows(tm, 256, 4, r), _rows(tm, 256, 5, r), _rows(tm, 512, 3, r), _rows(tm, 128, 20, r)]


def gla_fwd(p, wg, bg, reverse):
    T = p.shape[0]
    tm = CPB * CH
    n = T // tm
    rev = n if reverse else None

    def body(q_ref, k_ref, v_ref, r_ref, wg_ref, bg_ref, o_ref, ss_ref, st_ref):
        @pl.when(pl.program_id(0) == 0)
        def _():
            st_ref[...] = jnp.zeros_like(st_ref)

        consts = _tri(reverse) + _lane_masks()
        order = list(reversed(range(CPB))) if reverse else list(range(CPB))
        ss_ref[0] = st_ref[...]
        for j in range(2):
            sl = slice(j * 128, (j + 1) * 128)
            v0s, v1s = slice(256 * j, 256 * j + 128), slice(256 * j + 128, 256 * j + 256)
            chunks = []
            for c in order:
                rows = _chunk(c)
                chunks += [r_ref[rows, :], q_ref[rows, sl], k_ref[rows, sl], v_ref[rows, v0s], v_ref[rows, v1s]]
            res = gla_pair(consts, wg_ref[:, sl], bg_ref[:, sl], st_ref[2 * j], st_ref[2 * j + 1], *chunks)
            for i, c in enumerate(order):
                o_ref[_chunk(c), v0s] = res[2 * i]
                o_ref[_chunk(c), v1s] = res[2 * i + 1]
            st_ref[2 * j] = res[-2]
            st_ref[2 * j + 1] = res[-1]

    ss_spec = pl.BlockSpec((1, 4, 128, 128), (lambda i: (n - 1 - i, 0, 0, 0)) if reverse else (lambda i: (i, 0, 0, 0)))
    return _call(body, "gla_fwd_r" if reverse else "gla_fwd_f", (n,),
                 _gla_in_specs(tm, n, reverse) + [_full(wg), _full(bg)],
                 [_rows(tm, 512, 0, rev), ss_spec], [_sds((T, 512)), _sds((n, 4, 128, 128))],
                 scratch=[pltpu.VMEM((4, 128, 128), F32)])(p, p, p, p, wg, bg)


def mix_out(x, of, ob, p, outa, gg, w_out, tm):
    T = x.shape[0]

    def body(x_ref, of_ref, ob_ref, pg_ref, oa_ref, gg_ref, w_ref, x1_ref, mx_ref):
        mx_ref[:, 0:512] = oa_ref[...]
        for h in range(4):
            sl = slice(h * 128, (h + 1) * 128)
            mx_ref[:, 512 + h * 128:512 + (h + 1) * 128] = outb_head(
                of_ref[:, sl] + ob_ref[:, sl], pg_ref[:, sl], gg_ref[h]).astype(MX)
        x1_ref[...] = x_ref[...] + jnp.dot(mx_ref[...], w_ref[...], preferred_element_type=F32)

    return _call(body, "mix_out", (T // tm,),
                 [_rows(tm, D), _rows(tm, 512), _rows(tm, 512), _rows(tm, 512, 4), _rows(tm, 512), _full(gg), _full(w_out)],
                 [_rows(tm, D), _rows(tm, 1024)], [_sds((T, D)), _sds((T, 1024), MX)])(x, of, ob, p, outa, gg, w_out)


HALO = 16


def _halo_specs(T, tm, w):
    nb = T // HALO
    r = tm // HALO
    return [pl.BlockSpec((tm, w), lambda i: (i, 0)),
            pl.BlockSpec((HALO, w), lambda i: (jnp.maximum(i * r - 1, 0), 0)),
            pl.BlockSpec((HALO, w), lambda i: (jnp.minimum((i + 1) * r, nb - 1), 0))]


def ffn_up_conv(x1, g, w_up, cw, cb, tm):
    T = x1.shape[0]
    ns = T // tm
    cwid = 256

    def body(x_ref, g_ref, w_ref, cw_ref, cb_ref, zu_ref, h_ref, z_ref, a_ref, prev_ref, tail_ref):
        i = pl.program_id(0)

        @pl.when(i == 0)
        def _():
            prev_ref[...] = jnp.zeros_like(prev_ref)
            tail_ref[...] = jnp.zeros_like(tail_ref)

        hb = rmsnorm(x_ref[...], g_ref[...]).astype(MX)
        h_ref[...] = hb
        row = lax.broadcasted_iota(jnp.int32, (tm, 1), 0)
        for c0 in range(0, DFF, cwid):
            z2 = []
            for cs in (slice(c0, c0 + cwid), slice(DFF + c0, DFF + c0 + cwid)):
                zub = jnp.dot(hb, w_ref[:, cs], preferred_element_type=F32).astype(MX)
                zu_ref[:, cs] = zub
                prev = prev_ref[:, cs].astype(F32)
                pr = tail_ref[HALO - 1:HALO, cs].astype(F32)
                nx = jnp.where(i < ns, zub[0:1, :].astype(F32), 0.0)
                dn = jnp.where(row == 0, pr, pltpu.roll(prev, 1, 0))
                up = jnp.where(row == tm - 1, nx, pltpu.roll(prev, tm - 1, 0))
                z = cb_ref[:, cs] + dn * cw_ref[0:1, cs] + prev * cw_ref[1:2, cs] + up * cw_ref[2:3, cs]
                z_ref[:, cs] = z.astype(MX)
                tail_ref[:, cs] = prev_ref[tm - HALO:tm, cs]
                prev_ref[:, cs] = zub
                z2.append(z)
            a_ref[:, c0:c0 + cwid] = ffn_act(z2[0], z2[1]).astype(MX)

    cur = lambda w: pl.BlockSpec((tm, w), lambda i: (jnp.minimum(i, ns - 1), 0))
    late = lambda w: pl.BlockSpec((tm, w), lambda i: (jnp.maximum(i - 1, 0), 0))
    return _call(body, "ffn_up", (ns + 1,), [cur(D), _full(g), _full(w_up), _full(cw), _full(cb)],
                 [cur(NUP), cur(D), late(NUP), late(DFF)],
                 [_sds((T, NUP), MX), _sds((T, D), MX), _sds((T, NUP), MX), _sds((T, DFF), MX)],
                 scratch=[pltpu.VMEM((tm, NUP), MX), pltpu.VMEM((HALO, NUP), MX)])(x1, g, w_up, cw, cb)


def matmul_res(a, w, res, tm, name):
    T, k = a.shape
    n = w.shape[1]

    def body(a_ref, w_ref, r_ref, o_ref):
        o_ref[...] = r_ref[...] + jnp.dot(a_ref[...], w_ref[...], preferred_element_type=F32)

    return _call(body, name, (T // tm,), [_rows(tm, k), _full(w), _rows(tm, n)], _rows(tm, n), _sds((T, n)))(a, w, res)


def loss_head(x, g, tgt, tm):
    T = x.shape[0]

    def body(x_ref, g_ref, t_ref, l_ref, dx_ref, dg_ref):
        @pl.when(pl.program_id(0) == 0)
        def _():
            l_ref[...] = jnp.zeros_like(l_ref)
            dg_ref[...] = jnp.zeros_like(dg_ref)

        y, vjp = jax.vjp(rmsnorm, x_ref[...], g_ref[...])
        err = y - t_ref[...]
        l_ref[...] += jnp.sum(err * err, axis=0, keepdims=True)
        dx, dg = vjp(err * (1.0 / D))
        dx_ref[...] = dx
        dg_ref[...] += dg

    return _call(body, "loss_head", (T // tm,), [_rows(tm, D), _full(g), _rows(tm, D)],
                 [_full(g), _rows(tm, D), _full(g)], [_sds((1, D)), _sds((T, D)), _sds((1, D))])(x, g, tgt)


def ffn_down_bwd(dx2, z, w_down, tm):
    T = dx2.shape[0]

    def body(dx_ref, z_ref, w_ref, dz_ref):
        dxb = dx_ref[...].astype(MX)
        for c0 in range(0, DFF, 256):
            gs, vs = slice(c0, c0 + 256), slice(DFF + c0, DFF + c0 + 256)
            da = _dg(dxb, w_ref[gs, :], 1, 1)
            zg, zv = z_ref[:, gs].astype(F32), z_ref[:, vs].astype(F32)
            s = sigmoid(zg)
            sz = zg * s
            dz_ref[:, gs] = (da * zv * (s + sz * (1.0 - s))).astype(MX)
            dz_ref[:, vs] = (da * sz).astype(MX)

    return _call(body, "ffn_down_bwd", (T // tm,), [_rows(tm, D), _rows(tm, NUP), _full(w_down)],
                 _rows(tm, NUP), _sds((T, NUP), MX))(dx2, z, w_down)


def ffn_up_bwd(dz, zu, cw, w_up, x1, g, dres, tm):
    T = dz.shape[0]
    ns = T // tm
    cwid = 256

    def body(dz_ref, dp_ref, dn_ref, zu_ref, cw_ref, w_ref, x_ref, g_ref, dr_ref,
             dzu_ref, dx_ref, dg_ref, dcw_ref, dcb_ref):
        i = pl.program_id(0)

        @pl.when(i == 0)
        def _():
            for r in (dg_ref, dcw_ref, dcb_ref):
                r[...] = jnp.zeros_like(r)

        row = lax.broadcasted_iota(jnp.int32, (tm, 1), 0)
        dh = jnp.zeros((tm, D), F32)
        for c0 in range(0, NUP, cwid):
            cs = slice(c0, c0 + cwid)
            dz = dz_ref[:, cs].astype(F32)
            zu = zu_ref[:, cs].astype(F32)
            pr = jnp.where(i > 0, dp_ref[HALO - 1:HALO, cs].astype(F32), 0.0)
            nx = jnp.where(i < ns - 1, dn_ref[0:1, cs].astype(F32), 0.0)
            ddn = jnp.where(row == 0, pr, pltpu.roll(dz, 1, 0))
            dup = jnp.where(row == tm - 1, nx, pltpu.roll(dz, tm - 1, 0))
            dzu = (dup * cw_ref[0:1, cs] + dz * cw_ref[1:2, cs] + ddn * cw_ref[2:3, cs]).astype(MX)
            dzu_ref[:, cs] = dzu
            dcw_ref[0:1, cs] += jnp.sum(zu * dup, axis=0, keepdims=True)
            dcw_ref[1:2, cs] += jnp.sum(zu * dz, axis=0, keepdims=True)
            dcw_ref[2:3, cs] += jnp.sum(zu * ddn, axis=0, keepdims=True)
            dcb_ref[:, cs] += jnp.sum(dz, axis=0, keepdims=True)
            dh = dh + _dg(dzu, w_ref[:, cs], 1, 1)
        _, vjp = jax.vjp(rmsnorm, x_ref[...], g_ref[...])
        dx, dg = vjp(dh)
        dx_ref[...] = dr_ref[...] + dx
        dg_ref[...] += dg

    return _call(body, "ffn_up_bwd", (ns,),
                 _halo_specs(T, tm, NUP) + [_rows(tm, NUP), _full(cw), _full(w_up), _rows(tm, D), _full(g), _rows(tm, D)],
                 [_rows(tm, NUP), _rows(tm, D), _full(g), _full(cw), pl.BlockSpec((1, NUP), lambda i: (0, 0))],
                 [_sds((T, NUP), MX), _sds((T, D)), _sds((1, D)), _sds((3, NUP)), _sds((1, NUP))])(
                     dz, dz, dz, zu, cw, w_up, x1, g, dres)


def nt_normbwd(dys, w, x, g, dres, tm, name):
    T = x.shape[0]
    n = len(dys)
    offs = [sum(d.shape[1] for d in dys[:i]) for i in range(n + 1)]

    def body(*refs):
        dy_refs, (w_ref, x_ref, g_ref, dr_ref, dx_ref, dg_ref) = refs[:n], refs[n:]

        @pl.when(pl.program_id(0) == 0)
        def _():
            dg_ref[...] = jnp.zeros_like(dg_ref)

        dh = _dg(dy_refs[0][...], w_ref[:, offs[0]:offs[1]], 1, 1)
        for i in range(1, n):
            dh = dh + _dg(dy_refs[i][...], w_ref[:, offs[i]:offs[i + 1]], 1, 1)
        _, vjp = jax.vjp(rmsnorm, x_ref[...], g_ref[...])
        dx, dg = vjp(dh)
        dx_ref[...] = dr_ref[...] + dx
        dg_ref[...] += dg

    return _call(body, name, (T // tm,),
                 [_rows(tm, d.shape[1]) for d in dys] + [_full(w), _rows(tm, D), _full(g), _rows(tm, D)],
                 [_rows(tm, D), _full(g)], [_sds((T, D)), _sds((1, D))])(*dys, w, x, g, dres)


def matmul_tn(a, b, tt, tn, name):
    T, k = a.shape
    n = b.shape[1]
    last = T // tt - 1

    def body(a_ref, b_ref, o_ref, acc_ref):
        @pl.when(pl.program_id(1) == 0)
        def _():
            acc_ref[...] = jnp.zeros_like(acc_ref)

        acc_ref[...] += _dg(a_ref[...], b_ref[...], 0, 0)

        @pl.when(pl.program_id(1) == last)
        def _():
            o_ref[...] = acc_ref[...].astype(MX)

    return _call(body, name, (n // tn, T // tt),
                 [pl.BlockSpec((tt, k), lambda j, i: (i, 0)), pl.BlockSpec((tt, tn), lambda j, i: (i, j))],
                 pl.BlockSpec((k, tn), lambda j, i: (0, j)), _sds((k, n), MX), scratch=[pltpu.VMEM((k, tn), F32)],
                 n_axes=2)(a, b)


def mix_out_bwd(dx1, w_out, of, ob, p, gg, tm):
    T = dx1.shape[0]

    def body(dx_ref, w_ref, of_ref, ob_ref, pg_ref, gg_ref, da_ref, do_ref, dpg_ref, dgg_ref):
        @pl.when(pl.program_id(0) == 0)
        def _():
            dgg_ref[...] = jnp.zeros_like(dgg_ref)

        dxb = dx_ref[...].astype(MX)
        da_ref[...] = _dg(dxb, w_ref[0:512, :], 1, 1)
        for h in range(4):
            sl = slice(h * 128, (h + 1) * 128)
            dm = _dg(dxb, w_ref[512 + h * 128:512 + (h + 1) * 128, :], 1, 1)
            _, vjp = jax.vjp(outb_head, of_ref[:, sl] + ob_ref[:, sl], pg_ref[:, sl], gg_ref[h])
            do, dpg, dg = vjp(dm)
            do_ref[:, sl] = do
            dpg_ref[:, sl] = dpg
            dgg_ref[h] += dg

    return _call(body, "mix_out_bwd", (T // tm,),
                 [_rows(tm, D), _full(w_out), _rows(tm, 512), _rows(tm, 512), _rows(tm, 512, 4), _full(gg)],
                 [_rows(tm, 512), _rows(tm, 512), _rows(tm, 512), _full(gg)],
                 [_sds((T, 512)), _sds((T, 512)), _sds((T, 512)), _sds(gg.shape)])(dx1, w_out, of, ob, p, gg)


def gla_bwd(p, wg, bg, ss, do, reverse, merge=None):
    T = p.shape[0]
    tm = CPB * CH
    n = T // tm
    rev = not reverse
    rn = n if rev else None

    def body(*refs):
        q_ref, k_ref, v_ref, r_ref, wg_ref, bg_ref, ss_ref, do_ref = refs[:8]
        if merge is None:
            dq_ref, dk_ref, dv_ref, dr_ref, dwg_ref, dbg_ref, dst_ref = refs[8:]
        else:
            mq_ref, mk_ref, mv_ref, mr_ref, mg_ref, out_ref, dwg_ref, dbg_ref, dst_ref, drs_ref = refs[8:]
            out_ref[:, 1024:1536] = mg_ref[...].astype(MX)

        @pl.when(pl.program_id(0) == 0)
        def _():
            dst_ref[...] = jnp.zeros_like(dst_ref)
            dwg_ref[...] = jnp.zeros_like(dwg_ref)
            dbg_ref[...] = jnp.zeros_like(dbg_ref)

        consts = _tri(reverse) + _lane_masks()
        order = list(reversed(range(CPB))) if reverse else list(range(CPB))
        for j in range(2):
            sl = slice(j * 128, (j + 1) * 128)
            v0s, v1s = slice(256 * j, 256 * j + 128), slice(256 * j + 128, 256 * j + 256)
            chunks, dout = [], []
            for c in order:
                rows = _chunk(c)
                chunks += [r_ref[rows, :], q_ref[rows, sl], k_ref[rows, sl], v_ref[rows, v0s], v_ref[rows, v1s]]
                dout += [do_ref[rows, v0s], do_ref[rows, v1s]]
            _, vjp = jax.vjp(functools.partial(gla_pair, consts), wg_ref[:, sl], bg_ref[:, sl],
                             ss_ref[0, 2 * j], ss_ref[0, 2 * j + 1], *chunks)
            g = vjp((*dout, dst_ref[2 * j], dst_ref[2 * j + 1]))
            dwg_ref[:, sl] += g[0]
            dbg_ref[:, sl] += g[1]
            dst_ref[2 * j] = g[2]
            dst_ref[2 * j + 1] = g[3]
            for i, c in enumerate(order):
                rows = _chunk(c)
                dr, dq, dk, dv0, dv1 = g[4 + 5 * i:9 + 5 * i]
                if merge is None:
                    if j == 0:
                        dr_ref[rows, :] = dr
                    else:
                        dr_ref[rows, :] += dr
                    dq_ref[rows, sl] = dq
                    dk_ref[rows, sl] = dk
                    dv_ref[rows, v0s] = dv0
                    dv_ref[rows, v1s] = dv1
                else:
                    if j == 0:
                        drs_ref[rows, :] = mr_ref[rows, :] + dr
                    else:
                        out_ref[rows, 1536:1664] = (drs_ref[rows, :] + dr).astype(MX)
                    out_ref[rows, sl] = (mq_ref[rows, sl] + dq).astype(MX)
                    out_ref[rows, 256 + 128 * j:384 + 128 * j] = (mk_ref[rows, sl] + dk).astype(MX)
                    out_ref[rows, 512 + 256 * j:640 + 256 * j] = (mv_ref[rows, v0s] + dv0).astype(MX)
                    out_ref[rows, 640 + 256 * j:768 + 256 * j] = (mv_ref[rows, v1s] + dv1).astype(MX)

    ss_spec = pl.BlockSpec((1, 4, 128, 128), (lambda i: (n - 1 - i, 0, 0, 0)) if rev else (lambda i: (i, 0, 0, 0)))
    ins = [p, p, p, p, wg, bg, ss, do]
    in_specs = _gla_in_specs(tm, n, rev) + [_full(wg), _full(bg), ss_spec, _rows(tm, 512, 0, rn)]
    scratch = [pltpu.VMEM((4, 128, 128), F32)]
    if merge is None:
        out_specs = [_rows(tm, 256, 0, rn), _rows(tm, 256, 0, rn), _rows(tm, 512, 0, rn), _rows(tm, 128, 0, rn)]
        out_shape = [_sds((T, 256)), _sds((T, 256)), _sds((T, 512)), _sds((T, 128))]
    else:
        ins += list(merge)
        in_specs += [_rows(tm, a.shape[1], 0, rn) for a in merge]
        out_specs, out_shape = [_rows(tm, 1664, 0, rn)], [_sds((T, 1664), MX)]
        scratch.append(pltpu.VMEM((tm, 128), F32))
    return _call(body, "gla_bwd_r" if reverse else "gla_bwd_f", (n,), in_specs, out_specs + [_full(wg), _full(bg)],
                 out_shape + [_sds(wg.shape), _sds(bg.shape)], scratch=scratch)(*ins)


def gmlp_bwd(p, douta, ws, bs, lg, lb):
    T = p.shape[0]
    cpb = 4
    tm = cpb * CH

    def body(pa_ref, do_ref, ws_ref, bs_ref, lg_ref, lb_ref, dpa_ref, dws_ref, dbs_ref, dlg_ref, dlb_ref):
        @pl.when(pl.program_id(0) == 0)
        def _():
            for r in (dws_ref, dbs_ref, dlg_ref, dlb_ref):
                r[...] = jnp.zeros_like(r)

        us = [slice(h * 128, (h + 1) * 128) for h in range(4)]
        vs = [slice(512 + h * 128, 512 + (h + 1) * 128) for h in range(4)]
        params = [(ws_ref[h], bs_ref[h], lg_ref[h], lb_ref[h]) for h in range(4)]
        pieces = [[(pa_ref[_chunk(c), us[h]], pa_ref[_chunk(c), vs[h]]) for h in range(4)] for c in range(cpb)]
        _, vjp = jax.vjp(gmlp_heads, params, pieces)
        dparams, dpieces = vjp([[do_ref[_chunk(c), us[h]] for h in range(4)] for c in range(cpb)])
        for h in range(4):
            for r, a in zip((dws_ref, dbs_ref, dlg_ref, dlb_ref), dparams[h]):
                r[h] += a
            for c in range(cpb):
                dpa_ref[_chunk(c), us[h]] = dpieces[c][h][0].astype(MX)
                dpa_ref[_chunk(c), vs[h]] = dpieces[c][h][1].astype(MX)

    return _call(body, "gmlp_bwd", (T // tm,),
                 [_rows(tm, 1024), _rows(tm, 512), _full(ws), _full(bs), _full(lg), _full(lb)],
                 [_rows(tm, 1024), _full(ws), _full(bs), _full(lg), _full(lb)],
                 [_sds((T, 1024), MX), _sds(ws.shape), _sds(bs.shape), _sds(lg.shape), _sds(lb.shape)])(p, douta, ws, bs, lg, lb)


def _gate_pad(w, row0):
    return jnp.zeros((128, 256), F32).at[row0:row0 + 16].set(w)


def local_step(x, tgt, W, get_big, emit, tm=256, tmm=512):
    saved = []
    for l in range(NL):
        s = {"x": x}
        s["w_in"] = get_big(l, 0, x)
        p, s["h"] = norm_matmul(x, W["g_mix"][l][None], s["w_in"], tmm, "mix_in")
        s["p"] = p
        ws, bs = W["w_s"][l], W["b_s"][l][:, :, None]
        lg, lb = W["ln_g"][l][:, None, :], W["ln_b"][l][:, None, :]
        outa = gmlp_fwd(p, ws, bs, lg, lb)
        wgf, wgb = _gate_pad(W["w_gate_f"][l], 0), _gate_pad(W["w_gate_b"][l], 16)
        bgf, bgb = W["b_gate_f"][l][None], W["b_gate_b"][l][None]
        s["of"], s["ssf"] = gla_fwd(p, wgf, bgf, False)
        s["ob"], s["ssb"] = gla_fwd(p, wgb, bgb, True)
        s["w_out"] = get_big(l, 1, s["ob"])
        gg = W["g_gla"][l][:, None, :]
        x1, s["mixed"] = mix_out(x, s["of"], s["ob"], p, outa, gg, s["w_out"], tmm)
        s["x1"] = x1
        s["w_up"] = get_big(l, 2, x1)
        s["zu"], s["h2"], s["z"], s["a"] = ffn_up_conv(x1, W["g_ffn"][l][None], s["w_up"], W["conv_w"][l],
                                                       W["conv_b"][l][None], tm // 2)
        s["w_down"] = get_big(l, 3, s["a"])
        x = matmul_res(s["a"], s["w_down"], x1, tmm, "ffn_down")
        saved.append(s)

    lsum, dx, dgf = loss_head(x, W["g_final"][None], tgt, tmm)
    G = {k: [None] * NL for k in _SMALL if k != "g_final"}
    tok = jnp.zeros((1, 1), F32)
    for l in reversed(range(NL)):
        s = saved[l]
        g_down = matmul_tn(s["a"], dx, min(1024, tmm * 2), 512, "dw_down")
        dz = ffn_down_bwd(dx, s["z"], s["w_down"], tmm)
        dzu, dx1, dg, G["conv_w"][l], dcb = ffn_up_bwd(dz, s["zu"], W["conv_w"][l] + tok, s["w_up"], s["x1"],
                                                       W["g_ffn"][l][None], dx, tm)
        G["conv_b"][l], G["g_ffn"][l] = dcb[0], dg[0]
        g_up = matmul_tn(s["h2"], dzu, min(1024, tmm * 2), 1408, "dw_up")
        tok = emit(l, "A", {"w_down": g_down, "w_up": g_up})
        g_out = matmul_tn(s["mixed"], dx1, min(1024, tmm * 2), 1024, "dw_out")
        gg = W["g_gla"][l][:, None, :] + tok
        douta, do, dpg, dgg = mix_out_bwd(dx1, s["w_out"], s["of"], s["ob"], s["p"], gg, tmm)
        G["g_gla"][l] = dgg[:, 0, :]
        wgf, wgb = _gate_pad(W["w_gate_f"][l], 0), _gate_pad(W["w_gate_b"][l], 16)
        bgf, bgb = W["b_gate_f"][l][None], W["b_gate_b"][l][None]
        dqf, dkf, dvf, drf, dwgf, dbgf = gla_bwd(s["p"], wgf, bgf, s["ssf"], do, False)
        dpb, dwgb, dbgb = gla_bwd(s["p"], wgb, bgb, s["ssb"], do, True, merge=(dqf, dkf, dvf, drf, dpg))
        G["w_gate_f"][l], G["b_gate_f"][l] = dwgf[0:16], dbgf[0]
        G["w_gate_b"][l], G["b_gate_b"][l] = dwgb[16:32], dbgb[0]
        ws, bs = W["w_s"][l], W["b_s"][l][:, :, None]
        lg, lb = W["ln_g"][l][:, None, :], W["ln_b"][l][:, None, :]
        dpa, G["w_s"][l], dbs, dlg, dlb = gmlp_bwd(s["p"], douta, ws, bs, lg, lb)
        G["b_s"][l], G["ln_g"][l], G["ln_b"][l] = dbs[:, :, 0], dlg[:, 0, :], dlb[:, 0, :]
        tt = min(1024, tmm * 2)
        g_in = jnp.concatenate([matmul_tn(s["h"], dpa, tt, 1024, "dw_in_a"),
                                matmul_tn(s["h"], dpb, tt, 1664, "dw_in_b")], axis=1)
        tok = emit(l, "B", {"w_out": g_out, "w_in": g_in})
        dx, dg = nt_normbwd([dpa, dpb], s["w_in"], s["x"], W["g_mix"][l][None] + tok, dx1, tmm, "mix_in_bwd")
        G["g_mix"][l] = dg[0]
    G = {k: jnp.stack(v) for k, v in G.items()}
    G["g_final"] = dgf[0]
    return lsum, dx, G


def _rows3(tr, c):
    return pl.BlockSpec((None, tr, c), lambda l, i: (l, i, 0))


def cast_bf16(a, tr):
    nl, r, c = a.shape

    def body(a_ref, o_ref):
        o_ref[...] = a_ref[...].astype(BF16)

    return _call(body, "cast_bf16", (nl, r // tr), [_rows3(tr, c)], _rows3(tr, c), _sds(a.shape, BF16), n_axes=2)(a)


def sum_parts(land, grad, k, chipvec, tr):
    _, rr, cc = land.shape
    nb = rr // tr

    def body(c_ref, l_ref, g_ref, o_ref):
        mine = g_ref[...].astype(F32)
        acc = None
        for j in range(4):
            part = jnp.where(c_ref[0] == j, mine, l_ref[j].astype(F32))
            acc = part if acc is None else acc + part
        o_ref[...] = acc

    gs = pltpu.PrefetchScalarGridSpec(
        num_scalar_prefetch=1, grid=(nb,),
        in_specs=[pl.BlockSpec((4, tr, cc), lambda i, c: (0, i, 0)), _part_spec(k, tr, nb)],
        out_specs=pl.BlockSpec((tr, cc), lambda i, c: (i, 0)))
    return pl.pallas_call(body, name="sum_parts", grid_spec=gs, out_shape=_sds((rr, cc)),
                          compiler_params=_cparams(1))(chipvec, land, grad)


def adamw(w, ga, gb, m, v, tr):
    nl, r, c = w.shape

    def body(w_ref, ga_ref, gb_ref, m_ref, v_ref, g_ref, d_ref, nm_ref, nv_ref):
        gr = ga_ref[...] + gb_ref[...]
        g_ref[...] = gr
        nm = ADAM_B1 * m_ref[...] + (1.0 - ADAM_B1) * gr
        nv = ADAM_B2 * v_ref[...] + (1.0 - ADAM_B2) * jnp.square(gr)
        m_hat = nm * (1.0 / (1.0 - ADAM_B1 ** ADAM_STEP))
        v_hat = nv * (1.0 / (1.0 - ADAM_B2 ** ADAM_STEP))
        d_ref[...] = -ADAM_LR * (m_hat / (jnp.sqrt(v_hat) + ADAM_EPS) + ADAM_WD * w_ref[...])
        nm_ref[...] = nm
        nv_ref[...] = nv

    sp = _rows3(tr, c)
    return _call(body, "adamw", (nl, r // tr), [sp] * 5, [sp] * 4, [_sds(w.shape)] * 4, n_axes=2)(w, ga, gb, m, v)


MESH = pl.DeviceIdType.MESH
ANY = pl.BlockSpec(memory_space=pl.ANY)
N_BIG = 4


def _pos():
    return lax.axis_index("x"), lax.axis_index("y"), lax.axis_index("c")


def _other_chips(x, y):
    return [(1 - x, y), (x, 1 - y), (1 - x, 1 - y)]


def _slab(k, ref, j):
    if k == 0:
        return ref.at[j]
    if k == 1:
        return ref.at[pl.ds(256 * j, 256), :]
    if k == 2:
        return ref.at[:, pl.ds(1408 * j, 1408)]
    return ref.at[pl.ds(704 * j, 704), :]


_LAYER_FULL = [(4, 1024, 648), (1024, 1024), (1024, NUP), (DFF, 1024)]
_LAYER_SHARD = [(1024, 648), (256, 1024), (1024, 1408), (704, 1024)]

HBM = pl.BlockSpec(memory_space=pltpu.HBM)
SEM = pl.BlockSpec(memory_space=pltpu.SEMAPHORE)
VM = pl.BlockSpec(memory_space=pltpu.VMEM)
EFFECT = pltpu.SideEffectType.DATAFLOW_SIDE_EFFECTING
_GW_GROUPS = [[(0, k)] for k in range(N_BIG)] + [[(l, k) for k in range(N_BIG)] for l in range(1, NL)]
_GW_ORDER = [lk for g in _GW_GROUPS for lk in g]


def _hbm(a):
    return pltpu.with_memory_space_constraint(a, pltpu.HBM)


def _hbm_like(a):
    return pltpu.HBM(a.shape, a.dtype)


def _part_spec(k, tr, nb):
    cc = _LAYER_SHARD[k][1]
    if k == 0:
        return pl.BlockSpec((None, tr, cc), lambda i, c: (c[0], i, 0))
    if k == 2:
        return pl.BlockSpec((tr, cc), lambda i, c: (i, c[0]))
    return pl.BlockSpec((tr, cc), lambda i, c: (c[0] * nb + i, 0))


def place_own(shard, landing, l, k, chipvec, tr):
    rr, cc = _LAYER_SHARD[k]
    nb = rr // tr

    def body(c_ref, s_ref, l_ref, o_ref):
        o_ref[...] = s_ref[...]

    gs = pltpu.PrefetchScalarGridSpec(
        num_scalar_prefetch=1, grid=(nb,),
        in_specs=[pl.BlockSpec((None, tr, cc), lambda i, c: (l, i, 0)), ANY], out_specs=_part_spec(k, tr, nb))
    return pl.pallas_call(body, name="place_own", grid_spec=gs, out_shape=_sds(landing.shape, landing.dtype),
                          input_output_aliases={2: 0}, compiler_params=_cparams(1))(chipvec, shard, landing)


def gw_start(shards, landings, after):
    n = len(_GW_ORDER)

    def body(*refs):
        S, Ld = refs[:N_BIG], refs[N_BIG:N_BIG + n]
        outs = refs[N_BIG + n + 1:]
        send_sems, recv, token = outs[0], outs[1:1 + len(_GW_GROUPS)], outs[-1]
        x, y, c = _pos()
        me = 2 * x + y
        ci = 0
        for gi, grp in enumerate(_GW_GROUPS):
            for t, (l, k) in enumerate(grp):
                land = Ld[_GW_ORDER.index((l, k))]
                for j, (px, py) in enumerate(_other_chips(x, y)):
                    pltpu.make_async_remote_copy(
                        src_ref=S[k].at[l], dst_ref=_slab(k, land, me), send_sem=send_sems.at[ci],
                        recv_sem=recv[gi].at[3 * t + j], device_id=(px, py, c), device_id_type=MESH).start()
                    ci += 1
        token[...] = jnp.zeros_like(token)

    ins = list(shards) + list(landings)
    sems = [pltpu.SemaphoreType.DMA((3 * n,))] + [pltpu.SemaphoreType.DMA((3 * len(g),)) for g in _GW_GROUPS]
    outs = pl.pallas_call(
        body, name="gw_start", out_shape=sems + [_hbm_like(a) for a in ins] + [_sds((8, 128))],
        in_specs=[HBM] * len(ins) + [pl.BlockSpec(memory_space=pl.ANY)],
        out_specs=[SEM] * len(sems) + [HBM] * len(ins) + [VM],
        input_output_aliases={i: len(sems) + i for i in range(len(ins))},
        compiler_params=pltpu.CompilerParams(has_side_effects=EFFECT))(*[_hbm(a) for a in ins], after)
    ns = len(sems)
    return outs[0], outs[1:ns], outs[ns:ns + N_BIG], outs[ns + N_BIG:ns + len(ins)], outs[-1]


def gw_wait(gi, landings, recv_sems, after, shards=None, send_sems=None):
    grp = _GW_GROUPS[gi]
    n = len(grp)
    last = shards is not None

    def body(*refs):
        Ld, rs = refs[:n], refs[n]
        x, y, c = _pos()
        for t, (l, k) in enumerate(grp):
            for j, (px, py) in enumerate(_other_chips(x, y)):
                region = _slab(k, Ld[t], 2 * px + py)
                pltpu.make_async_remote_copy(src_ref=region, dst_ref=region, send_sem=rs.at[3 * t + j],
                                             recv_sem=rs.at[3 * t + j], device_id=(px, py, c),
                                             device_id_type=MESH).wait_recv()
        if last:
            S, ss = refs[n + 2:n + 2 + N_BIG], refs[n + 2 + N_BIG]
            me = 2 * x + y
            for ci, (l, k) in enumerate(lk for lk in _GW_ORDER for _ in range(3)):
                pltpu.make_async_remote_copy(src_ref=S[k].at[l], dst_ref=_slab(k, Ld[k], me), send_sem=ss.at[ci],
                                             recv_sem=ss.at[ci], device_id=(x, y, c), device_id_type=MESH).wait_send()

    ins = list(landings) + [recv_sems, after]
    specs = [HBM] * n + [SEM, pl.BlockSpec(memory_space=pl.ANY)]
    outs = [_hbm_like(a) for a in landings]
    alias = {i: i for i in range(n)}
    if last:
        ins += list(shards) + [send_sems]
        specs += [HBM] * N_BIG + [SEM]
        outs += [_hbm_like(a) for a in shards]
        alias.update({n + 2 + i: n + i for i in range(N_BIG)})
    res = pl.pallas_call(body, name="gw_wait_%d" % gi, out_shape=outs, in_specs=specs, out_specs=[HBM] * len(outs),
                         input_output_aliases=alias,
                         compiler_params=pltpu.CompilerParams(has_side_effects=EFFECT))(*ins)
    return res[:n], (res[n:] if last else None)


def ga_start(tag, ks, grads, landings):
    n = len(ks)

    def body(*refs):
        G, Ld = refs[:n], refs[n:2 * n]
        send_sems, recv_sems, token = refs[2 * n], refs[2 * n + 1], refs[-1]
        x, y, c = _pos()
        me = 2 * x + y
        for t, k in enumerate(ks):
            for j, (px, py) in enumerate(_other_chips(x, y)):
                pltpu.make_async_remote_copy(
                    src_ref=_slab(k, G[t], 2 * px + py), dst_ref=Ld[t].at[me], send_sem=send_sems.at[3 * t + j],
                    recv_sem=recv_sems.at[3 * t + j], device_id=(px, py, c), device_id_type=MESH).start()
        token[...] = jnp.zeros_like(token)

    ins = list(grads) + list(landings)
    sems = [pltpu.SemaphoreType.DMA((3 * n,))] * 2
    outs = pl.pallas_call(
        body, name="ga_start_" + tag, out_shape=sems + [_hbm_like(a) for a in ins] + [_sds((8, 128))],
        in_specs=[HBM] * len(ins), out_specs=[SEM, SEM] + [HBM] * len(ins) + [VM],
        input_output_aliases={i: 2 + i for i in range(len(ins))},
        compiler_params=pltpu.CompilerParams(has_side_effects=EFFECT))(*[_hbm(a) for a in ins])
    return outs[0], outs[1], outs[2:2 + n], outs[2 + n:2 + 2 * n], outs[-1]


def ga_wait(tag, ks, send_sems, recv_sems, grads, landings, after):
    n = len(ks)

    def body(*refs):
        G, Ld, ss, rs = refs[:n], refs[n:2 * n], refs[2 * n], refs[2 * n + 1]
        x, y, c = _pos()
        me = 2 * x + y
        for t, k in enumerate(ks):
            for j, (px, py) in enumerate(_other_chips(x, y)):
                pj = 2 * px + py
                cp = pltpu.make_async_remote_copy(
                    src_ref=_slab(k, G[t], pj), dst_ref=Ld[t].at[pj], send_sem=ss.at[3 * t + j],
                    recv_sem=rs.at[3 * t + j], device_id=(px, py, c), device_id_type=MESH)
                cp.wait_send()
                cp.wait_recv()

    ins = list(grads) + list(landings) + [send_sems, recv_sems, after]
    res = pl.pallas_call(
        body, name="ga_wait_" + tag, out_shape=[_hbm_like(a) for a in list(grads) + list(landings)],
        in_specs=[HBM] * (2 * n) + [SEM, SEM, pl.BlockSpec(memory_space=pl.ANY)], out_specs=[HBM] * (2 * n),
        input_output_aliases={i: i for i in range(2 * n)},
        compiler_params=pltpu.CompilerParams(has_side_effects=EFFECT))(*ins)
    return res[:n], res[n:]


def swap_start(parts):
    n = len(parts)

    def body(*refs):
        Q, Ld, sems = refs[:n], refs[n:2 * n], refs[2 * n:4 * n]
        x, y, c = _pos()
        for k in range(n):
            pltpu.make_async_remote_copy(src_ref=Q[k], dst_ref=Ld[k], send_sem=sems[k].at[0], recv_sem=sems[n + k].at[0],
                                         device_id=(x, y, 1 - c), device_id_type=MESH).start()
        refs[-1][...] = jnp.zeros_like(refs[-1])

    ins = list(parts) + [lax.empty(p.shape, p.dtype) for p in parts]
    outs = pl.pallas_call(
        body, name="swap_start",
        out_shape=[pltpu.SemaphoreType.DMA((1,))] * (2 * n) + [_hbm_like(a) for a in ins] + [_sds((8, 128))],
        in_specs=[HBM] * (2 * n), out_specs=[SEM] * (2 * n) + [HBM] * (2 * n) + [VM],
        input_output_aliases={i: 2 * n + i for i in range(2 * n)},
        compiler_params=pltpu.CompilerParams(has_side_effects=EFFECT))(*[_hbm(a) for a in ins])
    return outs[:n], outs[n:2 * n], outs[2 * n:3 * n], outs[3 * n:4 * n]


def swap_wait(k, send_sem, recv_sem, part, landing, after):
    def body(q_ref, l_ref, ss, rs, after_ref, q_out, l_out):
        x, y, c = _pos()
        cp = pltpu.make_async_remote_copy(src_ref=q_ref, dst_ref=l_ref, send_sem=ss.at[0], recv_sem=rs.at[0],
                                          device_id=(x, y, 1 - c), device_id_type=MESH)
        cp.wait_send()
        cp.wait_recv()

    return pl.pallas_call(
        body, name="swap_wait_%d" % k, out_shape=[_hbm_like(part), _hbm_like(landing)],
        in_specs=[HBM, HBM, SEM, SEM, pl.BlockSpec(memory_space=pl.ANY)], out_specs=[HBM, HBM],
        input_output_aliases={0: 0, 1: 1},
        compiler_params=pltpu.CompilerParams(has_side_effects=EFFECT))(part, landing, send_sem, recv_sem, after)


def _peer(x, y, c, r):
    fx, fy, fc = (r >> 2) & 1, (r >> 1) & 1, r & 1
    return ((1 - x) if fx else x, (1 - y) if fy else y, (1 - c) if fc else c)


def ag_start(tag, pack):
    rr, cc = pack.shape

    def body(p_ref, l_ref, ss, rs, p_out, l_out, token):
        x, y, c = _pos()
        me = 4 * x + 2 * y + c
        for r in range(1, 8):
            pltpu.make_async_remote_copy(src_ref=p_ref, dst_ref=l_ref.at[me], send_sem=ss.at[r - 1], recv_sem=rs.at[r - 1],
                                         device_id=_peer(x, y, c, r), device_id_type=MESH).start()
        token[...] = jnp.zeros_like(token)

    outs = pl.pallas_call(
        body, name="ag_start_" + tag,
        out_shape=[pltpu.SemaphoreType.DMA((7,)), pltpu.SemaphoreType.DMA((7,)), _hbm_like(pack),
                   pltpu.HBM((8, rr, cc), pack.dtype), _sds((8, 128))],
        in_specs=[HBM, HBM], out_specs=[SEM, SEM, HBM, HBM, VM], input_output_aliases={0: 2, 1: 3},
        compiler_params=pltpu.CompilerParams(has_side_effects=EFFECT))(_hbm(pack), _hbm(lax.empty((8, rr, cc), pack.dtype)))
    return outs


def ag_wait(tag, send_sems, recv_sems, pack, landing, after):
    def body(p_ref, l_ref, ss, rs, after_ref, p_out, l_out):
        x, y, c = _pos()
        for r in range(1, 8):
            px, py, pc = _peer(x, y, c, r)
            cp = pltpu.make_async_remote_copy(src_ref=p_ref, dst_ref=l_ref.at[4 * px + 2 * py + pc], send_sem=ss.at[r - 1],
                                              recv_sem=rs.at[r - 1], device_id=(px, py, pc), device_id_type=MESH)
            cp.wait_send()
            cp.wait_recv()

    return pl.pallas_call(
        body, name="ag_wait_" + tag, out_shape=[_hbm_like(pack), _hbm_like(landing)],
        in_specs=[HBM, HBM, SEM, SEM, pl.BlockSpec(memory_space=pl.ANY)], out_specs=[HBM, HBM],
        input_output_aliases={0: 0, 1: 1},
        compiler_params=pltpu.CompilerParams(has_side_effects=EFFECT))(pack, landing, send_sems, recv_sems, after)


def sum_slots(landing, own, mevec):
    _, rr, cc = landing.shape

    def body(m_ref, l_ref, o_ref, out_ref):
        mine = o_ref[...]
        acc = None
        for j in range(8):
            part = jnp.where(m_ref[0] == j, mine, l_ref[j])
            acc = part if acc is None else acc + part
        out_ref[...] = acc

    gs = pltpu.PrefetchScalarGridSpec(
        num_scalar_prefetch=1, grid=(1,),
        in_specs=[pl.BlockSpec((8, rr, cc), lambda i, m: (0, 0, 0)), pl.BlockSpec((rr, cc), lambda i, m: (0, 0))],
        out_specs=pl.BlockSpec((rr, cc), lambda i, m: (0, 0)))
    return pl.pallas_call(body, name="sum_slots", grid_spec=gs, out_shape=_sds((rr, cc)),
                          compiler_params=_cparams(1))(mevec, landing, own)


_WEIGHTS = ['g_mix', 'w_in', 'w_s', 'b_s', 'ln_g', 'ln_b', 'w_gate_f', 'b_gate_f', 'w_gate_b', 'b_gate_b', 'g_gla',
            'w_out', 'g_ffn', 'w_up', 'conv_w', 'conv_b', 'w_down', 'g_final']
_BIG = ['w_in', 'w_out', 'w_up', 'w_down']
_SMALL = [n for n in _WEIGHTS if n not in _BIG]
_SMALL_SHARDED = {'w_gate_f': 64, 'w_gate_b': 64, 'conv_w': 1408}
_BIG_TR = {'w_in': 512, 'w_out': 256, 'w_up': 256, 'w_down': 352}


def _pack(arrs):
    flat = jnp.concatenate([a.reshape(-1) for a in arrs])
    pad = (-flat.shape[0]) % 1024
    return jnp.pad(flat, (0, pad)).reshape(-1, 128)


def _unpack(buf, shapes):
    flat = buf.reshape(-1)
    out, o = [], 0
    for s in shapes:
        n = 1
        for d in s:
            n *= d
        out.append(flat[o:o + n].reshape(s))
        o += n
    return out


def kernel(x, g_mix, w_in, w_s, b_s, ln_g, ln_b, w_gate_f, b_gate_f, w_gate_b, b_gate_b, g_gla, w_out, g_ffn, w_up, conv_w, conv_b, w_down, g_final, loss_target, m_g_mix, m_w_in, m_w_s, m_b_s, m_ln_g, m_ln_b, m_w_gate_f, m_b_gate_f, m_w_gate_b, m_b_gate_b, m_g_gla, m_w_out, m_g_ffn, m_w_up, m_conv_w, m_conv_b, m_w_down, m_g_final, v_g_mix, v_w_in, v_w_s, v_b_s, v_ln_g, v_ln_b, v_w_gate_f, v_b_gate_f, v_w_gate_b, v_b_gate_b, v_g_gla, v_w_out, v_g_ffn, v_w_up, v_conv_w, v_conv_b, v_w_down, v_g_final):
    loc = locals()
    w = {n: loc[n] for n in _WEIGHTS}
    m = {n: loc["m_" + n] for n in _WEIGHTS}
    v = {n: loc["v_" + n] for n in _WEIGHTS}
    xi, yi, ci = _pos()
    chip = 2 * xi + yi
    me = 2 * chip + ci
    mevec = jnp.reshape(me, (1,)).astype(jnp.int32)

    sh_names = list(_SMALL_SHARDED)
    ss_w, rs_w, pk_w, land_w, tok_w = ag_start("w", _pack([w[n] for n in sh_names]))

    shards = [cast_bf16(w[n], _BIG_TR[n]) for n in _BIG]
    chipvec = jnp.reshape(chip, (1,)).astype(jnp.int32)
    send_sems, recv_sems, shards_fly, landings_fly, started = gw_start(
        shards, [lax.empty(_LAYER_FULL[k], BF16) for _, k in _GW_ORDER], tok_w)
    own = {"shards": shards_fly}

    pk_w, land_w = ag_wait("w", ss_w, rs_w, pk_w, land_w, started)
    per_chip = [_unpack(jnp.where(me == 2 * j, pk_w, land_w[2 * j]), [w[n].shape for n in sh_names]) for j in range(4)]
    W = dict(w)
    for k, n in enumerate(sh_names):
        W[n] = jnp.concatenate([per_chip[j][k] for j in range(4)], axis=-1)
    arrived = {}

    def get_big(l, k, after):
        if (l, k) not in arrived:
            gi = next(i for i, g in enumerate(_GW_GROUPS) if (l, k) in g)
            lo = sum(len(g) for g in _GW_GROUPS[:gi])
            lands = landings_fly[lo:lo + len(_GW_GROUPS[gi])]
            if gi == len(_GW_GROUPS) - 1:
                full, own["shards"] = gw_wait(gi, lands, recv_sems[gi], after, shards_fly, send_sems)
            else:
                full, _ = gw_wait(gi, lands, recv_sems[gi], after)
            for (gl, gk), a in zip(_GW_GROUPS[gi], full):
                arrived[(gl, gk)] = place_own(own["shards"][gk], a, gl, gk, chipvec, _BIG_TR[_BIG[gk]])
        if k == 0:
            f_in = jnp.transpose(arrived[(l, 0)], (1, 0, 2)).reshape(D, N_IN)
            return jnp.pad(f_in, ((0, 0), (0, N_INP - N_IN)))
        return arrived[(l, k)]

    flying = []

    def emit(l, group, grads):
        ks = [3, 2] if group == "A" else [1, 0]
        gs = [grads[_BIG[k]] for k in ks]
        if group == "B":
            gs[1] = jnp.transpose(gs[1][:, :N_IN].reshape(D, 4, 648), (1, 0, 2))
        lands = [lax.empty((4,) + _LAYER_SHARD[k], BF16) for k in ks]
        tag = "%d%s" % (l, group)
        ss, rs, gs_fly, lands_fly, tok = ga_start(tag, ks, gs, lands)
        flying.append((tag, l, ks, ss, rs, gs_fly, lands_fly))
        return tok[0:1, 0:1]

    lsum, grad_x, G = local_step(x[0], loss_target[0], W, get_big, emit)

    small_shapes = [G[n].shape for n in _SMALL] + [(D,)]
    ss_g, rs_g, pk_g, land_g, started = ag_start("g", _pack([G[n] for n in _SMALL] + [lsum]))

    plane = [[None] * NL for _ in range(N_BIG)]
    for tag, l, ks, ss, rs, gs_fly, lands_fly in flying:
        for k, g, a in zip(ks, *ga_wait(tag, ks, ss, rs, gs_fly, lands_fly, started)):
            plane[k][l] = sum_parts(a, g, k, chipvec, _BIG_TR[_BIG[k]])
    ss_p, rs_p, plane_fly, other_fly = swap_start([jnp.stack(p) for p in plane])
    grads, delta, new_m, new_v = {}, {}, {}, {}
    after = grad_x
    for k in (1, 0, 3, 2):
        n = _BIG[k]
        mine, other = swap_wait(k, ss_p[k], rs_p[k], plane_fly[k], other_fly[k], after)
        grads[n], delta[n], new_m[n], new_v[n] = adamw(w[n], mine, other, m[n], v[n], _BIG_TR[n])
        after = delta[n]

    pk_g, land_g = ag_wait("g", ss_g, rs_g, pk_g, land_g, after)
    small = dict(zip(_SMALL + ["lsum"], _unpack(sum_slots(land_g, pk_g, mevec), small_shapes)))
    loss = 0.5 * jnp.sum(small.pop("lsum")) / D
    for n, wd in _SMALL_SHARDED.items():
        small[n] = lax.dynamic_slice_in_dim(small[n], chip * wd, wd, axis=small[n].ndim - 1)
    grads.update(small)
    shapes = [w[n].shape for n in _SMALL]
    pw, pg, pm, pv = (_pack([t[n] for n in _SMALL])[None] for t in (w, grads, m, v))
    _, d_, m_, v_ = adamw(pw, pg, jnp.zeros_like(pg), pm, pv, pw.shape[1])
    for t, buf in ((delta, d_), (new_m, m_), (new_v, v_)):
        t.update(zip(_SMALL, _unpack(buf, shapes)))

    return (loss, grad_x[None], *[grads[n] for n in _WEIGHTS], *[delta[n] for n in _WEIGHTS],
            *[new_m[n] for n in _WEIGHTS], *[new_v[n] for n in _WEIGHTS])
```

```python
import functools

import jax
import jax.numpy as jnp
from jax import lax
from jax.experimental import pallas as pl
from jax.experimental.pallas import tpu as pltpu

F32 = jnp.float32
BF16 = jnp.bfloat16
MX = BF16

D = 1024
CH = 128
NL = 4
N_IN = 2592
N_INP = 2688
NUP = 5632
DFF = 2816
EPS = 1e-6
VMEM_LIMIT = 56 * 1024 * 1024

ADAM_LR, ADAM_B1, ADAM_B2, ADAM_EPS, ADAM_WD, ADAM_STEP = 0.001, 0.9, 0.999, 1e-08, 0.01, 10


def _dg(a, b, ca, cb):
    return lax.dot_general(a.astype(MX), b.astype(MX), (((ca,), (cb,)), ((), ())), preferred_element_type=F32)


@jax.custom_vjp
def mm(a, b):
    return _dg(a, b, 1, 0)


mm.defvjp(lambda a, b: (_dg(a, b, 1, 0), (a, b)),
          lambda r, g: (_dg(g, r[1], 1, 1), _dg(r[0], g, 0, 0)))


@jax.custom_vjp
def mm_nt(a, b):
    return _dg(a, b, 1, 1)


mm_nt.defvjp(lambda a, b: (_dg(a, b, 1, 1), (a, b)),
             lambda r, g: (_dg(g, r[1], 1, 0), _dg(g, r[0], 0, 0)))


@jax.custom_vjp
def mm_tn(a, b):
    return _dg(a, b, 0, 0)


mm_tn.defvjp(lambda a, b: (_dg(a, b, 0, 0), (a, b)),
             lambda r, g: (_dg(r[1], g, 1, 1), _dg(r[0], g, 1, 0)))


def _split3(x):
    hi = x.astype(BF16)
    r1 = x - hi.astype(F32)
    mid = r1.astype(BF16)
    lo = (r1 - mid.astype(F32)).astype(BF16)
    return hi, mid, lo


def _dot3(m, x):
    hi, mid, lo = _split3(x)
    d = lambda p: lax.dot_general(m, p, (((1,), (0,)), ((), ())), preferred_element_type=F32)
    return d(hi) + d(mid) + d(lo)


@jax.custom_vjp
def cumdot(m, mt, x):
    return _dot3(m, x)


cumdot.defvjp(lambda m, mt, x: (_dot3(m, x), (m, mt)),
              lambda r, g: (jnp.zeros_like(r[0]), jnp.zeros_like(r[1]), _dot3(r[1], g)))


def rmsnorm(x, g):
    return x * lax.rsqrt(jnp.mean(x * x, axis=-1, keepdims=True) + EPS) * g


def gelu(x):
    return 0.5 * x * (1.0 + lax.erf(x * 0.7071067811865476))


def sigmoid(x):
    return 1.0 / (1.0 + jnp.exp(-x))


def log_sigmoid(x):
    return jnp.minimum(x, 0.0) - jnp.log(1.0 + jnp.exp(-jnp.abs(x)))


def gmlp_heads(params, pieces):
    u = [[gelu(p[0]) for p in ch] for ch in pieces]
    v = [[gelu(p[1]) for p in ch] for ch in pieces]
    mu = [[jnp.mean(x, axis=-1, keepdims=True) for x in ch] for ch in v]
    var = [[jnp.mean(jnp.square(x - m), axis=-1, keepdims=True) for x, m in zip(cv, cm)] for cv, cm in zip(v, mu)]
    vn = [[(x - m) * lax.rsqrt(s + EPS) * pr[2] + pr[3] for x, m, s, pr in zip(cv, cm, cs, params)]
          for cv, cm, cs in zip(v, mu, var)]
    mix = [[mm(pr[0], x) + pr[1] for x, pr in zip(ch, params)] for ch in vn]
    return [[a * b for a, b in zip(cu, cx)] for cu, cx in zip(u, mix)]


def outb_head(o, pg, g):
    return rmsnorm(o, g) * (pg * sigmoid(pg))


def ffn_act(zg, zv):
    return zg * sigmoid(zg) * zv


def _tri(reverse):
    r = lax.broadcasted_iota(jnp.int32, (CH, CH), 0)
    c = lax.broadcasted_iota(jnp.int32, (CH, CH), 1)
    if reverse:
        cm, sm = c >= r, c > r
    else:
        cm, sm = c <= r, c <= r
    one = jnp.ones((), BF16)
    zero = jnp.zeros((), BF16)
    return jnp.where(cm, one, zero), jnp.where(cm.T, one, zero), sm


def gla_pair(consts, wg, bg, st0, st1, *chunks):
    m, mt, smask, lm0, lm1 = consts
    ch = [chunks[5 * i:5 * i + 5] for i in range(len(chunks) // 5)]
    la = [log_sigmoid(mm(c[0], wg) + bg) * (1.0 / 16.0) for c in ch]
    cum = [cumdot(m, mt, x) for x in la]
    tot = [jnp.sum(x, axis=0, keepdims=True) for x in la]
    q_dec = [(c[1] * 0.125) * jnp.exp(cm) for c, cm in zip(ch, cum)]
    k_inv = [c[2] * jnp.exp(-cm) for c, cm in zip(ch, cum)]
    k_end = [c[2] * jnp.exp(t - cm) for c, t, cm in zip(ch, tot, cum)]
    s = [[jnp.where(smask, mm_nt(qd * lm, ki), 0.0) for lm in (lm0, lm1)] for qd, ki in zip(q_dec, k_inv)]
    o_in = [[mm(si[h], c[3 + h]) for h in (0, 1)] for si, c in zip(s, ch)]
    ds = [[mm_tn(c[3 + h], ke * lm) for h, lm in ((0, lm0), (1, lm1))] for c, ke in zip(ch, k_end)]
    sts = [(st0, st1)]
    for t, d in zip(tot, ds):
        dec = jnp.exp(t)
        sts.append((sts[-1][0] * dec + d[0], sts[-1][1] * dec + d[1]))
    outs = []
    for qd, oi, st in zip(q_dec, o_in, sts):
        outs += [oi[0] + mm_nt(qd, st[0]), oi[1] + mm_nt(qd, st[1])]
    return (*outs, sts[-1][0], sts[-1][1])


def _lane_masks():
    lane = lax.broadcasted_iota(jnp.int32, (1, 128), 1)
    return (lane < 64).astype(F32), (lane >= 64).astype(F32)


def _cparams(n_axes=1):
    return pltpu.CompilerParams(dimension_semantics=("arbitrary",) * n_axes, vmem_limit_bytes=VMEM_LIMIT)


def _full(a):
    nd = a.ndim
    return pl.BlockSpec(a.shape, lambda *_: (0,) * nd)


def _rows(tm, w, cb=0, rev_n=None):
    if rev_n is None:
        return pl.BlockSpec((tm, w), lambda i: (i, cb))
    return pl.BlockSpec((tm, w), lambda i: (rev_n - 1 - i, cb))


def _call(body, name, grid, in_specs, out_specs, out_shape, scratch=(), n_axes=1):
    return pl.pallas_call(body, name=name, grid=grid, in_specs=in_specs, out_specs=out_specs, out_shape=out_shape,
                          scratch_shapes=list(scratch), compiler_params=_cparams(n_axes))


def _sds(shape, dt=F32):
    return jax.ShapeDtypeStruct(shape, dt)


def norm_matmul(x, g, w, tm, name, ydt=F32):
    T, n = x.shape[0], w.shape[1]

    def body(x_ref, g_ref, w_ref, y_ref, h_ref):
        hb = rmsnorm(x_ref[...], g_ref[...]).astype(MX)
        h_ref[...] = hb
        y_ref[...] = jnp.dot(hb, w_ref[...], preferred_element_type=F32).astype(ydt)

    return _call(body, name, (T // tm,), [_rows(tm, D), _full(g), _full(w)],
                 [_rows(tm, n), _rows(tm, D)], [_sds((T, n), ydt), _sds((T, D), MX)])(x, g, w)


CPB = 8


def _chunk(c):
    return slice(c * CH, (c + 1) * CH)


def gmlp_fwd(p, ws, bs, lg, lb):
    T = p.shape[0]
    tm = CPB * CH

    def body(pa_ref, ws_ref, bs_ref, lg_ref, lb_ref, o_ref):
        params = [(ws_ref[h], bs_ref[h], lg_ref[h], lb_ref[h]) for h in range(4)]
        pieces = [[(pa_ref[_chunk(c), h * 128:(h + 1) * 128], pa_ref[_chunk(c), 512 + h * 128:512 + (h + 1) * 128])
                   for h in range(4)] for c in range(CPB)]
        out = gmlp_heads(params, pieces)
        for c in range(CPB):
            for h in range(4):
                o_ref[_chunk(c), h * 128:(h + 1) * 128] = out[c][h].astype(MX)

    return _call(body, "gmlp_fwd", (T // tm,), [_rows(tm, 1024), _full(ws), _full(bs), _full(lg), _full(lb)],
                 _rows(tm, 512), _sds((T, 512), MX))(p, ws, bs, lg, lb)


def _gla_in_specs(tm, n, rev):
    r = n if rev else None
    return [_rows(tm, 256, 4, r), _rows(tm, 256, 5, r), _rows(tm, 512, 3, r), _rows(tm, 128, 20, r)]


def gla_fwd(p, wg, bg, reverse):
    T = p.shape[0]
    tm = CPB * CH
    n = T // tm
    rev = n if reverse else None

    def body(q_ref, k_ref, v_ref, r_ref, wg_ref, bg_ref, o_ref, ss_ref, st_ref):
        @pl.when(pl.program_id(0) == 0)
        def _():
            st_ref[...] = jnp.zeros_like(st_ref)

        consts = _tri(reverse) + _lane_masks()
        order = list(reversed(range(CPB))) if reverse else list(range(CPB))
        ss_ref[0] = st_ref[...]
        for j in range(2):
            sl = slice(j * 128, (j + 1) * 128)
            v0s, v1s = slice(256 * j, 256 * j + 128), slice(256 * j + 128, 256 * j + 256)
            chunks = []
            for c in order:
                rows = _chunk(c)
                chunks += [r_ref[rows, :], q_ref[rows, sl], k_ref[rows, sl], v_ref[rows, v0s], v_ref[rows, v1s]]
            res = gla_pair(consts, wg_ref[:, sl], bg_ref[:, sl], st_ref[2 * j], st_ref[2 * j + 1], *chunks)
            for i, c in enumerate(order):
                o_ref[_chunk(c), v0s] = res[2 * i]
                o_ref[_chunk(c), v1s] = res[2 * i + 1]
            st_ref[2 * j] = res[-2]
            st_ref[2 * j + 1] = res[-1]

    ss_spec = pl.BlockSpec((1, 4, 128, 128), (lambda i: (n - 1 - i, 0, 0, 0)) if reverse else (lambda i: (i, 0, 0, 0)))
    return _call(body, "gla_fwd_r" if reverse else "gla_fwd_f", (n,),
                 _gla_in_specs(tm, n, reverse) + [_full(wg), _full(bg)],
                 [_rows(tm, 512, 0, rev), ss_spec], [_sds((T, 512)), _sds((n, 4, 128, 128))],
                 scratch=[pltpu.VMEM((4, 128, 128), F32)])(p, p, p, p, wg, bg)


def mix_out(x, of, ob, p, outa, gg, w_out, tm):
    T = x.shape[0]

    def body(x_ref, of_ref, ob_ref, pg_ref, oa_ref, gg_ref, w_ref, x1_ref, mx_ref):
        mx_ref[:, 0:512] = oa_ref[...]
        for h in range(4):
            sl = slice(h * 128, (h + 1) * 128)
            mx_ref[:, 512 + h * 128:512 + (h + 1) * 128] = outb_head(
                of_ref[:, sl] + ob_ref[:, sl], pg_ref[:, sl], gg_ref[h]).astype(MX)
        x1_ref[...] = x_ref[...] + jnp.dot(mx_ref[...], w_ref[...], preferred_element_type=F32)

    return _call(body, "mix_out", (T // tm,),
                 [_rows(tm, D), _rows(tm, 512), _rows(tm, 512), _rows(tm, 512, 4), _rows(tm, 512), _full(gg), _full(w_out)],
                 [_rows(tm, D), _rows(tm, 1024)], [_sds((T, D)), _sds((T, 1024), MX)])(x, of, ob, p, outa, gg, w_out)


HALO = 16


def _halo_specs(T, tm, w):
    nb = T // HALO
    r = tm // HALO
    return [pl.BlockSpec((tm, w), lambda i: (i, 0)),
            pl.BlockSpec((HALO, w), lambda i: (jnp.maximum(i * r - 1, 0), 0)),
            pl.BlockSpec((HALO, w), lambda i: (jnp.minimum((i + 1) * r, nb - 1), 0))]


def ffn_up_conv(x1, g, w_up, cw, cb, tm):
    T = x1.shape[0]
    ns = T // tm
    cwid = 256

    def body(x_ref, g_ref, w_ref, cw_ref, cb_ref, zu_ref, h_ref, z_ref, a_ref, prev_ref, tail_ref):
        i = pl.program_id(0)

        @pl.when(i == 0)
        def _():
            prev_ref[...] = jnp.zeros_like(prev_ref)
            tail_ref[...] = jnp.zeros_like(tail_ref)

        hb = rmsnorm(x_ref[...], g_ref[...]).astype(MX)
        h_ref[...] = hb
        row = lax.broadcasted_iota(jnp.int32, (tm, 1), 0)
        for c0 in range(0, DFF, cwid):
            z2 = []
            for cs in (slice(c0, c0 + cwid), slice(DFF + c0, DFF + c0 + cwid)):
                zub = jnp.dot(hb, w_ref[:, cs], preferred_element_type=F32).astype(MX)
                zu_ref[:, cs] = zub
                prev = prev_ref[:, cs].astype(F32)
                pr = tail_ref[HALO - 1:HALO, cs].astype(F32)
                nx = jnp.where(i < ns, zub[0:1, :].astype(F32), 0.0)
                dn = jnp.where(row == 0, pr, pltpu.roll(prev, 1, 0))
                up = jnp.where(row == tm - 1, nx, pltpu.roll(prev, tm - 1, 0))
                z = cb_ref[:, cs] + dn * cw_ref[0:1, cs] + prev * cw_ref[1:2, cs] + up * cw_ref[2:3, cs]
                z_ref[:, cs] = z.astype(MX)
                tail_ref[:, cs] = prev_ref[tm - HALO:tm, cs]
                prev_ref[:, cs] = zub
                z2.append(z)
            a_ref[:, c0:c0 + cwid] = ffn_act(z2[0], z2[1]).astype(MX)

    cur = lambda w: pl.BlockSpec((tm, w), lambda i: (jnp.minimum(i, ns - 1), 0))
    late = lambda w: pl.BlockSpec((tm, w), lambda i: (jnp.maximum(i - 1, 0), 0))
    return _call(body, "ffn_up", (ns + 1,), [cur(D), _full(g), _full(w_up), _full(cw), _full(cb)],
                 [cur(NUP), cur(D), late(NUP), late(DFF)],
                 [_sds((T, NUP), MX), _sds((T, D), MX), _sds((T, NUP), MX), _sds((T, DFF), MX)],
                 scratch=[pltpu.VMEM((tm, NUP), MX), pltpu.VMEM((HALO, NUP), MX)])(x1, g, w_up, cw, cb)


def matmul_res(a, w, res, tm, name):
    T, k = a.shape
    n = w.shape[1]

    def body(a_ref, w_ref, r_ref, o_ref):
        o_ref[...] = r_ref[...] + jnp.dot(a_ref[...], w_ref[...], preferred_element_type=F32)

    return _call(body, name, (T // tm,), [_rows(tm, k), _full(w), _rows(tm, n)], _rows(tm, n), _sds((T, n)))(a, w, res)


def loss_head(x, g, tgt, tm):
    T = x.shape[0]

    def body(x_ref, g_ref, t_ref, l_ref, dx_ref, dg_ref):
        @pl.when(pl.program_id(0) == 0)
        def _():
            l_ref[...] = jnp.zeros_like(l_ref)
            dg_ref[...] = jnp.zeros_like(dg_ref)

        y, vjp = jax.vjp(rmsnorm, x_ref[...], g_ref[...])
        err = y - t_ref[...]
        l_ref[...] += jnp.sum(err * err, axis=0, keepdims=True)
        dx, dg = vjp(err * (1.0 / D))
        dx_ref[...] = dx
        dg_ref[...] += dg

    return _call(body, "loss_head", (T // tm,), [_rows(tm, D), _full(g), _rows(tm, D)],
                 [_full(g), _rows(tm, D), _full(g)], [_sds((1, D)), _sds((T, D)), _sds((1, D))])(x, g, tgt)


def ffn_down_bwd(dx2, z, w_down, tm):
    T = dx2.shape[0]

    def body(dx_ref, z_ref, w_ref, dz_ref):
        dxb = dx_ref[...].astype(MX)
        for c0 in range(0, DFF, 256):
            gs, vs = slice(c0, c0 + 256), slice(DFF + c0, DFF + c0 + 256)
            da = _dg(dxb, w_ref[gs, :], 1, 1)
            zg, zv = z_ref[:, gs].astype(F32), z_ref[:, vs].astype(F32)
            s = sigmoid(zg)
            sz = zg * s
            dz_ref[:, gs] = (da * zv * (s + sz * (1.0 - s))).astype(MX)
            dz_ref[:, vs] = (da * sz).astype(MX)

    return _call(body, "ffn_down_bwd", (T // tm,), [_rows(tm, D), _rows(tm, NUP), _full(w_down)],
                 _rows(tm, NUP), _sds((T, NUP), MX))(dx2, z, w_down)


def ffn_up_bwd(dz, zu, cw, w_up, x1, g, dres, tm):
    T = dz.shape[0]
    ns = T // tm
    cwid = 256

    def body(dz_ref, dp_ref, dn_ref, zu_ref, cw_ref, w_ref, x_ref, g_ref, dr_ref,
             dzu_ref, dx_ref, dg_ref, dcw_ref, dcb_ref):
        i = pl.program_id(0)

        @pl.when(i == 0)
        def _():
            for r in (dg_ref, dcw_ref, dcb_ref):
                r[...] = jnp.zeros_like(r)

        row = lax.broadcasted_iota(jnp.int32, (tm, 1), 0)
        dh = jnp.zeros((tm, D), F32)
        for c0 in range(0, NUP, cwid):
            cs = slice(c0, c0 + cwid)
            dz = dz_ref[:, cs].astype(F32)
            zu = zu_ref[:, cs].astype(F32)
            pr = jnp.where(i > 0, dp_ref[HALO - 1:HALO, cs].astype(F32), 0.0)
            nx = jnp.where(i < ns - 1, dn_ref[0:1, cs].astype(F32), 0.0)
            ddn = jnp.where(row == 0, pr, pltpu.roll(dz, 1, 0))
            dup = jnp.where(row == tm - 1, nx, pltpu.roll(dz, tm - 1, 0))
            dzu = (dup * cw_ref[0:1, cs] + dz * cw_ref[1:2, cs] + ddn * cw_ref[2:3, cs]).astype(MX)
            dzu_ref[:, cs] = dzu
            dcw_ref[0:1, cs] += jnp.sum(zu * dup, axis=0, keepdims=True)
            dcw_ref[1:2, cs] += jnp.sum(zu * dz, axis=0, keepdims=True)
            dcw_ref[2:3, cs] += jnp.sum(zu * ddn, axis=0, keepdims=True)
            dcb_ref[:, cs] += jnp.sum(dz, axis=0, keepdims=True)
            dh = dh + _dg(dzu, w_ref[:, cs], 1, 1)
        _, vjp = jax.vjp(rmsnorm, x_ref[...], g_ref[...])
        dx, dg = vjp(dh)
        dx_ref[...] = dr_ref[...] + dx
        dg_ref[...] += dg

    return _call(body, "ffn_up_bwd", (ns,),
                 _halo_specs(T, tm, NUP) + [_rows(tm, NUP), _full(cw), _full(w_up), _rows(tm, D), _full(g), _rows(tm, D)],
                 [_rows(tm, NUP), _rows(tm, D), _full(g), _full(cw), pl.BlockSpec((1, NUP), lambda i: (0, 0))],
                 [_sds((T, NUP), MX), _sds((T, D)), _sds((1, D)), _sds((3, NUP)), _sds((1, NUP))])(
                     dz, dz, dz, zu, cw, w_up, x1, g, dres)


def nt_normbwd(dys, w, x, g, dres, tm, name):
    T = x.shape[0]
    n = len(dys)
    offs = [sum(d.shape[1] for d in dys[:i]) for i in range(n + 1)]

    def body(*refs):
        dy_refs, (w_ref, x_ref, g_ref, dr_ref, dx_ref, dg_ref) = refs[:n], refs[n:]

        @pl.when(pl.program_id(0) == 0)
        def _():
            dg_ref[...] = jnp.zeros_like(dg_ref)

        dh = _dg(dy_refs[0][...], w_ref[:, offs[0]:offs[1]], 1, 1)
        for i in range(1, n):
            dh = dh + _dg(dy_refs[i][...], w_ref[:, offs[i]:offs[i + 1]], 1, 1)
        _, vjp = jax.vjp(rmsnorm, x_ref[...], g_ref[...])
        dx, dg = vjp(dh)
        dx_ref[...] = dr_ref[...] + dx
        dg_ref[...] += dg

    return _call(body, name, (T // tm,),
                 [_rows(tm, d.shape[1]) for d in dys] + [_full(w), _rows(tm, D), _full(g), _rows(tm, D)],
                 [_rows(tm, D), _full(g)], [_sds((T, D)), _sds((1, D))])(*dys, w, x, g, dres)


def matmul_tn(a, b, tt, tn, name):
    T, k = a.shape
    n = b.shape[1]
    last = T // tt - 1

    def body(a_ref, b_ref, o_ref, acc_ref):
        @pl.when(pl.program_id(1) == 0)
        def _():
            acc_ref[...] = jnp.zeros_like(acc_ref)

        acc_ref[...] += _dg(a_ref[...], b_ref[...], 0, 0)

        @pl.when(pl.program_id(1) == last)
        def _():
            o_ref[...] = acc_ref[...].astype(MX)

    return _call(body, name, (n // tn, T // tt),
                 [pl.BlockSpec((tt, k), lambda j, i: (i, 0)), pl.BlockSpec((tt, tn), lambda j, i: (i, j))],
                 pl.BlockSpec((k, tn), lambda j, i: (0, j)), _sds((k, n), MX), scratch=[pltpu.VMEM((k, tn), F32)],
                 n_axes=2)(a, b)


def mix_out_bwd(dx1, w_out, of, ob, p, gg, tm):
    T = dx1.shape[0]

    def body(dx_ref, w_ref, of_ref, ob_ref, pg_ref, gg_ref, da_ref, do_ref, dpg_ref, dgg_ref):
        @pl.when(pl.program_id(0) == 0)
        def _():
            dgg_ref[...] = jnp.zeros_like(dgg_ref)

        dxb = dx_ref[...].astype(MX)
        da_ref[...] = _dg(dxb, w_ref[0:512, :], 1, 1)
        for h in range(4):
            sl = slice(h * 128, (h + 1) * 128)
            dm = _dg(dxb, w_ref[512 + h * 128:512 + (h + 1) * 128, :], 1, 1)
            _, vjp = jax.vjp(outb_head, of_ref[:, sl] + ob_ref[:, sl], pg_ref[:, sl], gg_ref[h])
            do, dpg, dg = vjp(dm)
            do_ref[:, sl] = do
            dpg_ref[:, sl] = dpg
            dgg_ref[h] += dg

    return _call(body, "mix_out_bwd", (T // tm,),
                 [_rows(tm, D), _full(w_out), _rows(tm, 512), _rows(tm, 512), _rows(tm, 512, 4), _full(gg)],
                 [_rows(tm, 512), _rows(tm, 512), _rows(tm, 512), _full(gg)],
                 [_sds((T, 512)), _sds((T, 512)), _sds((T, 512)), _sds(gg.shape)])(dx1, w_out, of, ob, p, gg)


def gla_bwd(p, wg, bg, ss, do, reverse, merge=None):
    T = p.shape[0]
    tm = CPB * CH
    n = T // tm
    rev = not reverse
    rn = n if rev else None

    def body(*refs):
        q_ref, k_ref, v_ref, r_ref, wg_ref, bg_ref, ss_ref, do_ref = refs[:8]
        if merge is None:
            dq_ref, dk_ref, dv_ref, dr_ref, dwg_ref, dbg_ref, dst_ref = refs[8:]
        else:
            mq_ref, mk_ref, mv_ref, mr_ref, mg_ref, out_ref, dwg_ref, dbg_ref, dst_ref, drs_ref = refs[8:]
            out_ref[:, 1024:1536] = mg_ref[...].astype(MX)

        @pl.when(pl.program_id(0) == 0)
        def _():
            dst_ref[...] = jnp.zeros_like(dst_ref)
            dwg_ref[...] = jnp.zeros_like(dwg_ref)
            dbg_ref[...] = jnp.zeros_like(dbg_ref)

        consts = _tri(reverse) + _lane_masks()
        order = list(reversed(range(CPB))) if reverse else list(range(CPB))
        for j in range(2):
            sl = slice(j * 128, (j + 1) * 128)
            v0s, v1s = slice(256 * j, 256 * j + 128), slice(256 * j + 128, 256 * j + 256)
            chunks, dout = [], []
            for c in order:
                rows = _chunk(c)
                chunks += [r_ref[rows, :], q_ref[rows, sl], k_ref[rows, sl], v_ref[rows, v0s], v_ref[rows, v1s]]
                dout += [do_ref[rows, v0s], do_ref[rows, v1s]]
            _, vjp = jax.vjp(functools.partial(gla_pair, consts), wg_ref[:, sl], bg_ref[:, sl],
                             ss_ref[0, 2 * j], ss_ref[0, 2 * j + 1], *chunks)
            g = vjp((*dout, dst_ref[2 * j], dst_ref[2 * j + 1]))
            dwg_ref[:, sl] += g[0]
            dbg_ref[:, sl] += g[1]
            dst_ref[2 * j] = g[2]
            dst_ref[2 * j + 1] = g[3]
            for i, c in enumerate(order):
                rows = _chunk(c)
                dr, dq, dk, dv0, dv1 = g[4 + 5 * i:9 + 5 * i]
                if merge is None:
                    if j == 0:
                        dr_ref[rows, :] = dr
                    else:
                        dr_ref[rows, :] += dr
                    dq_ref[rows, sl] = dq
                    dk_ref[rows, sl] = dk
                    dv_ref[rows, v0s] = dv0
                    dv_ref[rows, v1s] = dv1
                else:
                    if j == 0:
                        drs_ref[rows, :] = mr_ref[rows, :] + dr
                    else:
                        out_ref[rows, 1536:1664] = (drs_ref[rows, :] + dr).astype(MX)
                    out_ref[rows, sl] = (mq_ref[rows, sl] + dq).astype(MX)
                    out_ref[rows, 256 + 128 * j:384 + 128 * j] = (mk_ref[rows, sl] + dk).astype(MX)
                    out_ref[rows, 512 + 256 * j:640 + 256 * j] = (mv_ref[rows, v0s] + dv0).astype(MX)
                    out_ref[rows, 640 + 256 * j:768 + 256 * j] = (mv_ref[rows, v1s] + dv1).astype(MX)

    ss_spec = pl.BlockSpec((1, 4, 128, 128), (lambda i: (n - 1 - i, 0, 0, 0)) if rev else (lambda i: (i, 0, 0, 0)))
    ins = [p, p, p, p, wg, bg, ss, do]
    in_specs = _gla_in_specs(tm, n, rev) + [_full(wg), _full(bg), ss_spec, _rows(tm, 512, 0, rn)]
    scratch = [pltpu.VMEM((4, 128, 128), F32)]
    if merge is None:
        out_specs = [_rows(tm, 256, 0, rn), _rows(tm, 256, 0, rn), _rows(tm, 512, 0, rn), _rows(tm, 128, 0, rn)]
        out_shape = [_sds((T, 256)), _sds((T, 256)), _sds((T, 512)), _sds((T, 128))]
    else:
        ins += list(merge)
        in_specs += [_rows(tm, a.shape[1], 0, rn) for a in merge]
        out_specs, out_shape = [_rows(tm, 1664, 0, rn)], [_sds((T, 1664), MX)]
        scratch.append(pltpu.VMEM((tm, 128), F32))
    return _call(body, "gla_bwd_r" if reverse else "gla_bwd_f", (n,), in_specs, out_specs + [_full(wg), _full(bg)],
                 out_shape + [_sds(wg.shape), _sds(bg.shape)], scratch=scratch)(*ins)


def gmlp_bwd(p, douta, ws, bs, lg, lb):
    T = p.shape[0]
    cpb = 4
    tm = cpb * CH

    def body(pa_ref, do_ref, ws_ref, bs_ref, lg_ref, lb_ref, dpa_ref, dws_ref, dbs_ref, dlg_ref, dlb_ref):
        @pl.when(pl.program_id(0) == 0)
        def _():
            for r in (dws_ref, dbs_ref, dlg_ref, dlb_ref):
                r[...] = jnp.zeros_like(r)

        us = [slice(h * 128, (h + 1) * 128) for h in range(4)]
        vs = [slice(512 + h * 128, 512 + (h + 1) * 128) for h in range(4)]
        params = [(ws_ref[h], bs_ref[h], lg_ref[h], lb_ref[h]) for h in range(4)]
        pieces = [[(pa_ref[_chunk(c), us[h]], pa_ref[_chunk(c), vs[h]]) for h in range(4)] for c in range(cpb)]
        _, vjp = jax.vjp(gmlp_heads, params, pieces)
        dparams, dpieces = vjp([[do_ref[_chunk(c), us[h]] for h in range(4)] for c in range(cpb)])
        for h in range(4):
            for r, a in zip((dws_ref, dbs_ref, dlg_ref, dlb_ref), dparams[h]):
                r[h] += a
            for c in range(cpb):
                dpa_ref[_chunk(c), us[h]] = dpieces[c][h][0].astype(MX)
                dpa_ref[_chunk(c), vs[h]] = dpieces[c][h][1].astype(MX)

    return _call(body, "gmlp_bwd", (T // tm,),
                 [_rows(tm, 1024), _rows(tm, 512), _full(ws), _full(bs), _full(lg), _full(lb)],
                 [_rows(tm, 1024), _full(ws), _full(bs), _full(lg), _full(lb)],
                 [_sds((T, 1024), MX), _sds(ws.shape), _sds(bs.shape), _sds(lg.shape), _sds(lb.shape)])(p, douta, ws, bs, lg, lb)


def _gate_pad(w, row0):
    return jnp.zeros((128, 256), F32).at[row0:row0 + 16].set(w)


def local_step(x, tgt, W, get_big, emit, tm=256, tmm=512):
    saved = []
    for l in range(NL):
        s = {"x": x}
        s["w_in"] = get_big(l, 0, x)
        p, s["h"] = norm_matmul(x, W["g_mix"][l][None], s["w_in"], tmm, "mix_in")
        s["p"] = p
        ws, bs = W["w_s"][l], W["b_s"][l][:, :, None]
        lg, lb = W["ln_g"][l][:, None, :], W["ln_b"][l][:, None, :]
        outa = gmlp_fwd(p, ws, bs, lg, lb)
        wgf, wgb = _gate_pad(W["w_gate_f"][l], 0), _gate_pad(W["w_gate_b"][l], 16)
        bgf, bgb = W["b_gate_f"][l][None], W["b_gate_b"][l][None]
        s["of"], s["ssf"] = gla_fwd(p, wgf, bgf, False)
        s["ob"], s["ssb"] = gla_fwd(p, wgb, bgb, True)
        s["w_out"] = get_big(l, 1, s["ob"])
        gg = W["g_gla"][l][:, None, :]
        x1, s["mixed"] = mix_out(x, s["of"], s["ob"], p, outa, gg, s["w_out"], min(1024, tmm * 2))
        s["x1"] = x1
        s["w_up"] = get_big(l, 2, x1)
        s["zu"], s["h2"], s["z"], s["a"] = ffn_up_conv(x1, W["g_ffn"][l][None], s["w_up"], W["conv_w"][l],
                                                       W["conv_b"][l][None], tm // 2)
        s["w_down"] = get_big(l, 3, s["a"])
        x = matmul_res(s["a"], s["w_down"], x1, min(1024, tmm * 2), "ffn_down")
        saved.append(s)

    lsum, dx, dgf = loss_head(x, W["g_final"][None], tgt, tmm)
    G = {k: [None] * NL for k in _SMALL if k != "g_final"}
    tok = jnp.zeros((1, 1), F32)
    for l in reversed(range(NL)):
        s = saved[l]
        g_down = matmul_tn(s["a"], dx, min(1024, tmm * 2), 512, "dw_down")
        dz = ffn_down_bwd(dx, s["z"], s["w_down"], tmm)
        dzu, dx1, dg, G["conv_w"][l], dcb = ffn_up_bwd(dz, s["zu"], W["conv_w"][l] + tok, s["w_up"], s["x1"],
                                                       W["g_ffn"][l][None], dx, tm)
        G["conv_b"][l], G["g_ffn"][l] = dcb[0], dg[0]
        g_up = matmul_tn(s["h2"], dzu, min(1024, tmm * 2), 1408, "dw_up")
        tok = emit(l, "A", {"w_down": g_down, "w_up": g_up})
        g_out = matmul_tn(s["mixed"], dx1, min(1024, tmm * 2), 1024, "dw_out")
        gg = W["g_gla"][l][:, None, :] + tok
        douta, do, dpg, dgg = mix_out_bwd(dx1, s["w_out"], s["of"], s["ob"], s["p"], gg, min(1024, tmm * 2))
        G["g_gla"][l] = dgg[:, 0, :]
        wgf, wgb = _gate_pad(W["w_gate_f"][l], 0), _gate_pad(W["w_gate_b"][l], 16)
        bgf, bgb = W["b_gate_f"][l][None], W["b_gate_b"][l][None]
        dqf, dkf, dvf, drf, dwgf, dbgf = gla_bwd(s["p"], wgf, bgf, s["ssf"], do, False)
        dpb, dwgb, dbgb = gla_bwd(s["p"], wgb, bgb, s["ssb"], do, True, merge=(dqf, dkf, dvf, drf, dpg))
        G["w_gate_f"][l], G["b_gate_f"][l] = dwgf[0:16], dbgf[0]
        G["w_gate_b"][l], G["b_gate_b"][l] = dwgb[16:32], dbgb[0]
        ws, bs = W["w_s"][l], W["b_s"][l][:, :, None]
        lg, lb = W["ln_g"][l][:, None, :], W["ln_b"][l][:, None, :]
        dpa, G["w_s"][l], dbs, dlg, dlb = gmlp_bwd(s["p"], douta, ws, bs, lg, lb)
        G["b_s"][l], G["ln_g"][l], G["ln_b"][l] = dbs[:, :, 0], dlg[:, 0, :], dlb[:, 0, :]
        tt = min(1024, tmm * 2)
        g_in = jnp.concatenate([matmul_tn(s["h"], dpa, tt, 1024, "dw_in_a"),
                                matmul_tn(s["h"], dpb, tt, 1664, "dw_in_b")], axis=1)
        tok = emit(l, "B", {"w_out": g_out, "w_in": g_in})
        dx, dg = nt_normbwd([dpa, dpb], s["w_in"], s["x"], W["g_mix"][l][None] + tok, dx1, tmm, "mix_in_bwd")
        G["g_mix"][l] = dg[0]
    G = {k: jnp.stack(v) for k, v in G.items()}
    G["g_final"] = dgf[0]
    return lsum, dx, G


def _rows3(tr, c):
    return pl.BlockSpec((None, tr, c), lambda l, i: (l, i, 0))


def cast_bf16(a, tr):
    nl, r, c = a.shape

    def body(a_ref, o_ref):
        o_ref[...] = a_ref[...].astype(BF16)

    return _call(body, "cast_bf16", (nl, r // tr), [_rows3(tr, c)], _rows3(tr, c), _sds(a.shape, BF16), n_axes=2)(a)


def sum_parts(land, grad, k, chipvec, tr):
    _, rr, cc = land.shape
    nb = rr // tr

    def body(c_ref, l_ref, g_ref, o_ref):
        mine = g_ref[...].astype(F32)
        acc = None
        for j in range(4):
            part = jnp.where(c_ref[0] == j, mine, l_ref[j].astype(F32))
            acc = part if acc is None else acc + part
        o_ref[...] = acc

    gs = pltpu.PrefetchScalarGridSpec(
        num_scalar_prefetch=1, grid=(nb,),
        in_specs=[pl.BlockSpec((4, tr, cc), lambda i, c: (0, i, 0)), _part_spec(k, tr, nb)],
        out_specs=pl.BlockSpec((tr, cc), lambda i, c: (i, 0)))
    return pl.pallas_call(body, name="sum_parts", grid_spec=gs, out_shape=_sds((rr, cc)),
                          compiler_params=_cparams(1))(chipvec, land, grad)


def adamw(w, ga, gb, m, v, tr):
    nl, r, c = w.shape

    def body(w_ref, ga_ref, gb_ref, m_ref, v_ref, g_ref, d_ref, nm_ref, nv_ref):
        gr = ga_ref[...] + gb_ref[...]
        g_ref[...] = gr
        nm = ADAM_B1 * m_ref[...] + (1.0 - ADAM_B1) * gr
        nv = ADAM_B2 * v_ref[...] + (1.0 - ADAM_B2) * jnp.square(gr)
        m_hat = nm * (1.0 / (1.0 - ADAM_B1 ** ADAM_STEP))
        v_hat = nv * (1.0 / (1.0 - ADAM_B2 ** ADAM_STEP))
        d_ref[...] = -ADAM_LR * (m_hat / (jnp.sqrt(v_hat) + ADAM_EPS) + ADAM_WD * w_ref[...])
        nm_ref[...] = nm
        nv_ref[...] = nv

    sp = _rows3(tr, c)
    return _call(body, "adamw", (nl, r // tr), [sp] * 5, [sp] * 4, [_sds(w.shape)] * 4, n_axes=2)(w, ga, gb, m, v)


MESH = pl.DeviceIdType.MESH
ANY = pl.BlockSpec(memory_space=pl.ANY)
N_BIG = 4


def _pos():
    return lax.axis_index("x"), lax.axis_index("y"), lax.axis_index("c")


def _other_chips(x, y):
    return [(1 - x, y), (x, 1 - y), (1 - x, 1 - y)]


def _slab(k, ref, j):
    if k == 0:
        return ref.at[j]
    if k == 1:
        return ref.at[pl.ds(256 * j, 256), :]
    if k == 2:
        return ref.at[:, pl.ds(1408 * j, 1408)]
    return ref.at[pl.ds(704 * j, 704), :]


_LAYER_FULL = [(4, 1024, 648), (1024, 1024), (1024, NUP), (DFF, 1024)]
_LAYER_SHARD = [(1024, 648), (256, 1024), (1024, 1408), (704, 1024)]

HBM = pl.BlockSpec(memory_space=pltpu.HBM)
SEM = pl.BlockSpec(memory_space=pltpu.SEMAPHORE)
VM = pl.BlockSpec(memory_space=pltpu.VMEM)
EFFECT = pltpu.SideEffectType.DATAFLOW_SIDE_EFFECTING
_GW_GROUPS = [[(0, k)] for k in range(N_BIG)] + [[(l, k) for k in range(N_BIG)] for l in range(1, NL)]
_GW_ORDER = [lk for g in _GW_GROUPS for lk in g]


def _hbm(a):
    return pltpu.with_memory_space_constraint(a, pltpu.HBM)


def _hbm_like(a):
    return pltpu.HBM(a.shape, a.dtype)


def _part_spec(k, tr, nb):
    cc = _LAYER_SHARD[k][1]
    if k == 0:
        return pl.BlockSpec((None, tr, cc), lambda i, c: (c[0], i, 0))
    if k == 2:
        return pl.BlockSpec((tr, cc), lambda i, c: (i, c[0]))
    return pl.BlockSpec((tr, cc), lambda i, c: (c[0] * nb + i, 0))


def place_own(shard, landing, l, k, chipvec, tr):
    rr, cc = _LAYER_SHARD[k]
    nb = rr // tr

    def body(c_ref, s_ref, l_ref, o_ref):
        o_ref[...] = s_ref[...]

    gs = pltpu.PrefetchScalarGridSpec(
        num_scalar_prefetch=1, grid=(nb,),
        in_specs=[pl.BlockSpec((None, tr, cc), lambda i, c: (l, i, 0)), ANY], out_specs=_part_spec(k, tr, nb))
    return pl.pallas_call(body, name="place_own", grid_spec=gs, out_shape=_sds(landing.shape, landing.dtype),
                          input_output_aliases={2: 0}, compiler_params=_cparams(1))(chipvec, shard, landing)


def gw_start(shards, landings, after):
    n = len(_GW_ORDER)

    def body(*refs):
        S, Ld = refs[:N_BIG], refs[N_BIG:N_BIG + n]
        outs = refs[N_BIG + n + 1:]
        send_sems, recv, token = outs[0], outs[1:1 + len(_GW_GROUPS)], outs[-1]
        x, y, c = _pos()
        me = 2 * x + y
        ci = 0
        for gi, grp in enumerate(_GW_GROUPS):
            for t, (l, k) in enumerate(grp):
                land = Ld[_GW_ORDER.index((l, k))]
                for j, (px, py) in enumerate(_other_chips(x, y)):
                    pltpu.make_async_remote_copy(
                        src_ref=S[k].at[l], dst_ref=_slab(k, land, me), send_sem=send_sems.at[ci],
                        recv_sem=recv[gi].at[3 * t + j], device_id=(px, py, c), device_id_type=MESH).start()
                    ci += 1
        token[...] = jnp.zeros_like(token)

    ins = list(shards) + list(landings)
    sems = [pltpu.SemaphoreType.DMA((3 * n,))] + [pltpu.SemaphoreType.DMA((3 * len(g),)) for g in _GW_GROUPS]
    outs = pl.pallas_call(
        body, name="gw_start", out_shape=sems + [_hbm_like(a) for a in ins] + [_sds((8, 128))],
        in_specs=[HBM] * len(ins) + [pl.BlockSpec(memory_space=pl.ANY)],
        out_specs=[SEM] * len(sems) + [HBM] * len(ins) + [VM],
        input_output_aliases={i: len(sems) + i for i in range(len(ins))},
        compiler_params=pltpu.CompilerParams(has_side_effects=EFFECT))(*[_hbm(a) for a in ins], after)
    ns = len(sems)
    return outs[0], outs[1:ns], outs[ns:ns + N_BIG], outs[ns + N_BIG:ns + len(ins)], outs[-1]


def gw_wait(gi, landings, recv_sems, after, shards=None, send_sems=None):
    grp = _GW_GROUPS[gi]
    n = len(grp)
    last = shards is not None

    def body(*refs):
        Ld, rs = refs[:n], refs[n]
        x, y, c = _pos()
        for t, (l, k) in enumerate(grp):
            for j, (px, py) in enumerate(_other_chips(x, y)):
                region = _slab(k, Ld[t], 2 * px + py)
                pltpu.make_async_remote_copy(src_ref=region, dst_ref=region, send_sem=rs.at[3 * t + j],
                                             recv_sem=rs.at[3 * t + j], device_id=(px, py, c),
                                             device_id_type=MESH).wait_recv()
        if last:
            S, ss = refs[n + 2:n + 2 + N_BIG], refs[n + 2 + N_BIG]
            me = 2 * x + y
            for ci, (l, k) in enumerate(lk for lk in _GW_ORDER for _ in range(3)):
                pltpu.make_async_remote_copy(src_ref=S[k].at[l], dst_ref=_slab(k, Ld[k], me), send_sem=ss.at[ci],
                                             recv_sem=ss.at[ci], device_id=(x, y, c), device_id_type=MESH).wait_send()

    ins = list(landings) + [recv_sems, after]
    specs = [HBM] * n + [SEM, pl.BlockSpec(memory_space=pl.ANY)]
    outs = [_hbm_like(a) for a in landings]
    alias = {i: i for i in range(n)}
    if last:
        ins += list(shards) + [send_sems]
        specs += [HBM] * N_BIG + [SEM]
        outs += [_hbm_like(a) for a in shards]
        alias.update({n + 2 + i: n + i for i in range(N_BIG)})
    res = pl.pallas_call(body, name="gw_wait_%d" % gi, out_shape=outs, in_specs=specs, out_specs=[HBM] * len(outs),
                         input_output_aliases=alias,
                         compiler_params=pltpu.CompilerParams(has_side_effects=EFFECT))(*ins)
    return res[:n], (res[n:] if last else None)


def ga_start(tag, ks, grads, landings):
    n = len(ks)

    def body(*refs):
        G, Ld = refs[:n], refs[n:2 * n]
        send_sems, recv_sems, token = refs[2 * n], refs[2 * n + 1], refs[-1]
        x, y, c = _pos()
        me = 2 * x + y
        for t, k in enumerate(ks):
            for j, (px, py) in enumerate(_other_chips(x, y)):
                pltpu.make_async_remote_copy(
                    src_ref=_slab(k, G[t], 2 * px + py), dst_ref=Ld[t].at[me], send_sem=send_sems.at[3 * t + j],
                    recv_sem=recv_sems.at[3 * t + j], device_id=(px, py, c), device_id_type=MESH).start()
        token[...] = jnp.zeros_like(token)

    ins = list(grads) + list(landings)
    sems = [pltpu.SemaphoreType.DMA((3 * n,))] * 2
    outs = pl.pallas_call(
        body, name="ga_start_" + tag, out_shape=sems + [_hbm_like(a) for a in ins] + [_sds((8, 128))],
        in_specs=[HBM] * len(ins), out_specs=[SEM, SEM] + [HBM] * len(ins) + [VM],
        input_output_aliases={i: 2 + i for i in range(len(ins))},
        compiler_params=pltpu.CompilerParams(has_side_effects=EFFECT))(*[_hbm(a) for a in ins])
    return outs[0], outs[1], outs[2:2 + n], outs[2 + n:2 + 2 * n], outs[-1]


def ga_wait(tag, ks, send_sems, recv_sems, grads, landings, after):
    n = len(ks)

    def body(*refs):
        G, Ld, ss, rs = refs[:n], refs[n:2 * n], refs[2 * n], refs[2 * n + 1]
        x, y, c = _pos()
        me = 2 * x + y
        for t, k in enumerate(ks):
            for j, (px, py) in enumerate(_other_chips(x, y)):
                pj = 2 * px + py
                cp = pltpu.make_async_remote_copy(
                    src_ref=_slab(k, G[t], pj), dst_ref=Ld[t].at[pj], send_sem=ss.at[3 * t + j],
                    recv_sem=rs.at[3 * t + j], device_id=(px, py, c), device_id_type=MESH)
                cp.wait_send()
                cp.wait_recv()

    ins = list(grads) + list(landings) + [send_sems, recv_sems, after]
    res = pl.pallas_call(
        body, name="ga_wait_" + tag, out_shape=[_hbm_like(a) for a in list(grads) + list(landings)],
        in_specs=[HBM] * (2 * n) + [SEM, SEM, pl.BlockSpec(memory_space=pl.ANY)], out_specs=[HBM] * (2 * n),
        input_output_aliases={i: i for i in range(2 * n)},
        compiler_params=pltpu.CompilerParams(has_side_effects=EFFECT))(*ins)
    return res[:n], res[n:]


def swap_start(parts):
    n = len(parts)

    def body(*refs):
        Q, Ld, sems = refs[:n], refs[n:2 * n], refs[2 * n:4 * n]
        x, y, c = _pos()
        for k in range(n):
            pltpu.make_async_remote_copy(src_ref=Q[k], dst_ref=Ld[k], send_sem=sems[k].at[0], recv_sem=sems[n + k].at[0],
                                         device_id=(x, y, 1 - c), device_id_type=MESH).start()
        refs[-1][...] = jnp.zeros_like(refs[-1])

    ins = list(parts) + [lax.empty(p.shape, p.dtype) for p in parts]
    outs = pl.pallas_call(
        body, name="swap_start",
        out_shape=[pltpu.SemaphoreType.DMA((1,))] * (2 * n) + [_hbm_like(a) for a in ins] + [_sds((8, 128))],
        in_specs=[HBM] * (2 * n), out_specs=[SEM] * (2 * n) + [HBM] * (2 * n) + [VM],
        input_output_aliases={i: 2 * n + i for i in range(2 * n)},
        compiler_params=pltpu.CompilerParams(has_side_effects=EFFECT))(*[_hbm(a) for a in ins])
    return outs[:n], outs[n:2 * n], outs[2 * n:3 * n], outs[3 * n:4 * n]


def swap_wait(k, send_sem, recv_sem, part, landing, after):
    def body(q_ref, l_ref, ss, rs, after_ref, q_out, l_out):
        x, y, c = _pos()
        cp = pltpu.make_async_remote_copy(src_ref=q_ref, dst_ref=l_ref, send_sem=ss.at[0], recv_sem=rs.at[0],
                                          device_id=(x, y, 1 - c), device_id_type=MESH)
        cp.wait_send()
        cp.wait_recv()

    return pl.pallas_call(
        body, name="swap_wait_%d" % k, out_shape=[_hbm_like(part), _hbm_like(landing)],
        in_specs=[HBM, HBM, SEM, SEM, pl.BlockSpec(memory_space=pl.ANY)], out_specs=[HBM, HBM],
        input_output_aliases={0: 0, 1: 1},
        compiler_params=pltpu.CompilerParams(has_side_effects=EFFECT))(part, landing, send_sem, recv_sem, after)


def _peer(x, y, c, r):
    fx, fy, fc = (r >> 2) & 1, (r >> 1) & 1, r & 1
    return ((1 - x) if fx else x, (1 - y) if fy else y, (1 - c) if fc else c)


def ag_start(tag, pack):
    rr, cc = pack.shape

    def body(p_ref, l_ref, ss, rs, p_out, l_out, token):
        x, y, c = _pos()
        me = 4 * x + 2 * y + c
        for r in range(1, 8):
            pltpu.make_async_remote_copy(src_ref=p_ref, dst_ref=l_ref.at[me], send_sem=ss.at[r - 1], recv_sem=rs.at[r - 1],
                                         device_id=_peer(x, y, c, r), device_id_type=MESH).start()
        token[...] = jnp.zeros_like(token)

    outs = pl.pallas_call(
        body, name="ag_start_" + tag,
        out_shape=[pltpu.SemaphoreType.DMA((7,)), pltpu.SemaphoreType.DMA((7,)), _hbm_like(pack),
                   pltpu.HBM((8, rr, cc), pack.dtype), _sds((8, 128))],
        in_specs=[HBM, HBM], out_specs=[SEM, SEM, HBM, HBM, VM], input_output_aliases={0: 2, 1: 3},
        compiler_params=pltpu.CompilerParams(has_side_effects=EFFECT))(_hbm(pack), _hbm(lax.empty((8, rr, cc), pack.dtype)))
    return outs


def ag_wait(tag, send_sems, recv_sems, pack, landing, after):
    def body(p_ref, l_ref, ss, rs, after_ref, p_out, l_out):
        x, y, c = _pos()
        for r in range(1, 8):
            px, py, pc = _peer(x, y, c, r)
            cp = pltpu.make_async_remote_copy(src_ref=p_ref, dst_ref=l_ref.at[4 * px + 2 * py + pc], send_sem=ss.at[r - 1],
                                              recv_sem=rs.at[r - 1], device_id=(px, py, pc), device_id_type=MESH)
            cp.wait_send()
            cp.wait_recv()

    return pl.pallas_call(
        body, name="ag_wait_" + tag, out_shape=[_hbm_like(pack), _hbm_like(landing)],
        in_specs=[HBM, HBM, SEM, SEM, pl.BlockSpec(memory_space=pl.ANY)], out_specs=[HBM, HBM],
        input_output_aliases={0: 0, 1: 1},
        compiler_params=pltpu.CompilerParams(has_side_effects=EFFECT))(pack, landing, send_sems, recv_sems, after)


def sum_slots(landing, own, mevec):
    _, rr, cc = landing.shape

    def body(m_ref, l_ref, o_ref, out_ref):
        mine = o_ref[...]
        acc = None
        for j in range(8):
            part = jnp.where(m_ref[0] == j, mine, l_ref[j])
            acc = part if acc is None else acc + part
        out_ref[...] = acc

    gs = pltpu.PrefetchScalarGridSpec(
        num_scalar_prefetch=1, grid=(1,),
        in_specs=[pl.BlockSpec((8, rr, cc), lambda i, m: (0, 0, 0)), pl.BlockSpec((rr, cc), lambda i, m: (0, 0))],
        out_specs=pl.BlockSpec((rr, cc), lambda i, m: (0, 0)))
    return pl.pallas_call(body, name="sum_slots", grid_spec=gs, out_shape=_sds((rr, cc)),
                          compiler_params=_cparams(1))(mevec, landing, own)


_WEIGHTS = ['g_mix', 'w_in', 'w_s', 'b_s', 'ln_g', 'ln_b', 'w_gate_f', 'b_gate_f', 'w_gate_b', 'b_gate_b', 'g_gla',
            'w_out', 'g_ffn', 'w_up', 'conv_w', 'conv_b', 'w_down', 'g_final']
_BIG = ['w_in', 'w_out', 'w_up', 'w_down']
_SMALL = [n for n in _WEIGHTS if n not in _BIG]
_SMALL_SHARDED = {'w_gate_f': 64, 'w_gate_b': 64, 'conv_w': 1408}
_BIG_TR = {'w_in': 512, 'w_out': 256, 'w_up': 256, 'w_down': 352}


def _pack(arrs):
    flat = jnp.concatenate([a.reshape(-1) for a in arrs])
    pad = (-flat.shape[0]) % 1024
    return jnp.pad(flat, (0, pad)).reshape(-1, 128)


def _unpack(buf, shapes):
    flat = buf.reshape(-1)
    out, o = [], 0
    for s in shapes:
        n = 1
        for d in s:
            n *= d
        out.append(flat[o:o + n].reshape(s))
        o += n
    return out


def kernel(x, g_mix, w_in, w_s, b_s, ln_g, ln_b, w_gate_f, b_gate_f, w_gate_b, b_gate_b, g_gla, w_out, g_ffn, w_up, conv_w, conv_b, w_down, g_final, loss_target, m_g_mix, m_w_in, m_w_s, m_b_s, m_ln_g, m_ln_b, m_w_gate_f, m_b_gate_f, m_w_gate_b, m_b_gate_b, m_g_gla, m_w_out, m_g_ffn, m_w_up, m_conv_w, m_conv_b, m_w_down, m_g_final, v_g_mix, v_w_in, v_w_s, v_b_s, v_ln_g, v_ln_b, v_w_gate_f, v_b_gate_f, v_w_gate_b, v_b_gate_b, v_g_gla, v_w_out, v_g_ffn, v_w_up, v_conv_w, v_conv_b, v_w_down, v_g_final):
    loc = locals()
    w = {n: loc[n] for n in _WEIGHTS}
    m = {n: loc["m_" + n] for n in _WEIGHTS}
    v = {n: loc["v_" + n] for n in _WEIGHTS}
    xi, yi, ci = _pos()
    chip = 2 * xi + yi
    me = 2 * chip + ci
    mevec = jnp.reshape(me, (1,)).astype(jnp.int32)

    sh_names = list(_SMALL_SHARDED)
    ss_w, rs_w, pk_w, land_w, tok_w = ag_start("w", _pack([w[n] for n in sh_names]))

    shards = [cast_bf16(w[n], _BIG_TR[n]) for n in _BIG]
    chipvec = jnp.reshape(chip, (1,)).astype(jnp.int32)
    send_sems, recv_sems, shards_fly, landings_fly, started = gw_start(
        shards, [lax.empty(_LAYER_FULL[k], BF16) for _, k in _GW_ORDER], tok_w)
    own = {"shards": shards_fly}

    pk_w, land_w = ag_wait("w", ss_w, rs_w, pk_w, land_w, started)
    per_chip = [_unpack(jnp.where(me == 2 * j, pk_w, land_w[2 * j]), [w[n].shape for n in sh_names]) for j in range(4)]
    W = dict(w)
    for k, n in enumerate(sh_names):
        W[n] = jnp.concatenate([per_chip[j][k] for j in range(4)], axis=-1)
    arrived = {}

    def get_big(l, k, after):
        if (l, k) not in arrived:
            gi = next(i for i, g in enumerate(_GW_GROUPS) if (l, k) in g)
            lo = sum(len(g) for g in _GW_GROUPS[:gi])
            lands = landings_fly[lo:lo + len(_GW_GROUPS[gi])]
            if gi == len(_GW_GROUPS) - 1:
                full, own["shards"] = gw_wait(gi, lands, recv_sems[gi], after, shards_fly, send_sems)
            else:
                full, _ = gw_wait(gi, lands, recv_sems[gi], after)
            for (gl, gk), a in zip(_GW_GROUPS[gi], full):
                arrived[(gl, gk)] = place_own(own["shards"][gk], a, gl, gk, chipvec, _BIG_TR[_BIG[gk]])
        if k == 0:
            f_in = jnp.transpose(arrived[(l, 0)], (1, 0, 2)).reshape(D, N_IN)
            return jnp.pad(f_in, ((0, 0), (0, N_INP - N_IN)))
        return arrived[(l, k)]

    flying = []

    def emit(l, group, grads):
        ks = [3, 2] if group == "A" else [1, 0]
        gs = [grads[_BIG[k]] for k in ks]
        if group == "B":
            gs[1] = jnp.transpose(gs[1][:, :N_IN].reshape(D, 4, 648), (1, 0, 2))
        lands = [lax.empty((4,) + _LAYER_SHARD[k], BF16) for k in ks]
        tag = "%d%s" % (l, group)
        ss, rs, gs_fly, lands_fly, tok = ga_start(tag, ks, gs, lands)
        flying.append((tag, l, ks, ss, rs, gs_fly, lands_fly))
        return tok[0:1, 0:1]

    lsum, grad_x, G = local_step(x[0], loss_target[0], W, get_big, emit)

    small_shapes = [G[n].shape for n in _SMALL] + [(D,)]
    ss_g, rs_g, pk_g, land_g, started = ag_start("g", _pack([G[n] for n in _SMALL] + [lsum]))

    plane = [[None] * NL for _ in range(N_BIG)]
    for tag, l, ks, ss, rs, gs_fly, lands_fly in flying:
        for k, g, a in zip(ks, *ga_wait(tag, ks, ss, rs, gs_fly, lands_fly, started)):
            plane[k][l] = sum_parts(a, g, k, chipvec, _BIG_TR[_BIG[k]])
    ss_p, rs_p, plane_fly, other_fly = swap_start([jnp.stack(p) for p in plane])
    grads, delta, new_m, new_v = {}, {}, {}, {}
    after = grad_x
    for k in (1, 0, 3, 2):
        n = _BIG[k]
        mine, other = swap_wait(k, ss_p[k], rs_p[k], plane_fly[k], other_fly[k], after)
        grads[n], delta[n], new_m[n], new_v[n] = adamw(w[n], mine, other, m[n], v[n], _BIG_TR[n])
        after = delta[n]

    pk_g, land_g = ag_wait("g", ss_g, rs_g, pk_g, land_g, after)
    small = dict(zip(_SMALL + ["lsum"], _unpack(sum_slots(land_g, pk_g, mevec), small_shapes)))
    loss = 0.5 * jnp.sum(small.pop("lsum")) / D
    for n, wd in _SMALL_SHARDED.items():
        small[n] = lax.dynamic_slice_in_dim(small[n], chip * wd, wd, axis=small[n].ndim - 1)
    grads.update(small)
    shapes = [w[n].shape for n in _SMALL]
    pw, pg, pm, pv = (_pack([t[n] for n in _SMALL])[None] for t in (w, grads, m, v))
    _, d_, m_, v_ = adamw(pw, pg, jnp.zeros_like(pg), pm, pv, pw.shape[1])
    for t, buf in ((delta, d_), (new_m, m_), (new_v, v_)):
        t.update(zip(_SMALL, _unpack(buf, shapes)))

    return (loss, grad_x[None], *[grads[n] for n in _WEIGHTS], *[delta[n] for n in _WEIGHTS],
            *[new_m[n] for n in _WEIGHTS], *[new_v[n] for n in _WEIGHTS])
```

```python
import functools

import jax
import jax.numpy as jnp
from jax import lax
from jax.experimental import pallas as pl
from jax.experimental.pallas import tpu as pltpu

F32 = jnp.float32
BF16 = jnp.bfloat16
MX = BF16

D = 1024
CH = 128
NL = 4
N_IN = 2592
N_INP = 2688
NUP = 5632
DFF = 2816
EPS = 1e-6
VMEM_LIMIT = 56 * 1024 * 1024

ADAM_LR, ADAM_B1, ADAM_B2, ADAM_EPS, ADAM_WD, ADAM_STEP = 0.001, 0.9, 0.999, 1e-08, 0.01, 10


def _dg(a, b, ca, cb):
    return lax.dot_general(a.astype(MX), b.astype(MX), (((ca,), (cb,)), ((), ())), preferred_element_type=F32)


@jax.custom_vjp
def mm(a, b):
    return _dg(a, b, 1, 0)


mm.defvjp(lambda a, b: (_dg(a, b, 1, 0), (a, b)),
          lambda r, g: (_dg(g, r[1], 1, 1), _dg(r[0], g, 0, 0)))


@jax.custom_vjp
def mm_nt(a, b):
    return _dg(a, b, 1, 1)


mm_nt.defvjp(lambda a, b: (_dg(a, b, 1, 1), (a, b)),
             lambda r, g: (_dg(g, r[1], 1, 0), _dg(g, r[0], 0, 0)))


@jax.custom_vjp
def mm_tn(a, b):
    return _dg(a, b, 0, 0)


mm_tn.defvjp(lambda a, b: (_dg(a, b, 0, 0), (a, b)),
             lambda r, g: (_dg(r[1], g, 1, 1), _dg(r[0], g, 1, 0)))


def _split3(x):
    hi = x.astype(BF16)
    r1 = x - hi.astype(F32)
    mid = r1.astype(BF16)
    lo = (r1 - mid.astype(F32)).astype(BF16)
    return hi, mid, lo


def _dot3(m, x):
    hi, mid, lo = _split3(x)
    d = lambda p: lax.dot_general(m, p, (((1,), (0,)), ((), ())), preferred_element_type=F32)
    return d(hi) + d(mid) + d(lo)


@jax.custom_vjp
def cumdot(m, mt, x):
    return _dot3(m, x)


cumdot.defvjp(lambda m, mt, x: (_dot3(m, x), (m, mt)),
              lambda r, g: (jnp.zeros_like(r[0]), jnp.zeros_like(r[1]), _dot3(r[1], g)))


def rmsnorm(x, g):
    return x * lax.rsqrt(jnp.mean(x * x, axis=-1, keepdims=True) + EPS) * g


def gelu(x):
    return 0.5 * x * (1.0 + lax.erf(x * 0.7071067811865476))


def sigmoid(x):
    return 1.0 / (1.0 + jnp.exp(-x))


def log_sigmoid(x):
    return jnp.minimum(x, 0.0) - jnp.log(1.0 + jnp.exp(-jnp.abs(x)))


def gmlp_heads(params, pieces):
    u = [[gelu(p[0]) for p in ch] for ch in pieces]
    v = [[gelu(p[1]) for p in ch] for ch in pieces]
    mu = [[jnp.mean(x, axis=-1, keepdims=True) for x in ch] for ch in v]
    var = [[jnp.mean(jnp.square(x - m), axis=-1, keepdims=True) for x, m in zip(cv, cm)] for cv, cm in zip(v, mu)]
    vn = [[(x - m) * lax.rsqrt(s + EPS) * pr[2] + pr[3] for x, m, s, pr in zip(cv, cm, cs, params)]
          for cv, cm, cs in zip(v, mu, var)]
    mix = [[mm(pr[0], x) + pr[1] for x, pr in zip(ch, params)] for ch in vn]
    return [[a * b for a, b in zip(cu, cx)] for cu, cx in zip(u, mix)]


def outb_head(o, pg, g):
    return rmsnorm(o, g) * (pg * sigmoid(pg))


def ffn_act(zg, zv):
    return zg * sigmoid(zg) * zv


def _tri(reverse):
    r = lax.broadcasted_iota(jnp.int32, (CH, CH), 0)
    c = lax.broadcasted_iota(jnp.int32, (CH, CH), 1)
    if reverse:
        cm, sm = c >= r, c > r
    else:
        cm, sm = c <= r, c <= r
    one = jnp.ones((), BF16)
    zero = jnp.zeros((), BF16)
    return jnp.where(cm, one, zero), jnp.where(cm.T, one, zero), sm


def gla_pair(consts, wg, bg, st0, st1, *chunks):
    m, mt, smask, lm0, lm1 = consts
    ch = [chunks[5 * i:5 * i + 5] for i in range(len(chunks) // 5)]
    la = [log_sigmoid(mm(c[0], wg) + bg) * (1.0 / 16.0) for c in ch]
    cum = [cumdot(m, mt, x) for x in la]
    tot = [jnp.sum(x, axis=0, keepdims=True) for x in la]
    q_dec = [(c[1] * 0.125) * jnp.exp(cm) for c, cm in zip(ch, cum)]
    k_inv = [c[2] * jnp.exp(-cm) for c, cm in zip(ch, cum)]
    k_end = [c[2] * jnp.exp(t - cm) for c, t, cm in zip(ch, tot, cum)]
    s = [[jnp.where(smask, mm_nt(qd * lm, ki), 0.0) for lm in (lm0, lm1)] for qd, ki in zip(q_dec, k_inv)]
    o_in = [[mm(si[h], c[3 + h]) for h in (0, 1)] for si, c in zip(s, ch)]
    ds = [[mm_tn(c[3 + h], ke * lm) for h, lm in ((0, lm0), (1, lm1))] for c, ke in zip(ch, k_end)]
    sts = [(st0, st1)]
    for t, d in zip(tot, ds):
        dec = jnp.exp(t)
        sts.append((sts[-1][0] * dec + d[0], sts[-1][1] * dec + d[1]))
    outs = []
    for qd, oi, st in zip(q_dec, o_in, sts):
        outs += [oi[0] + mm_nt(qd, st[0]), oi[1] + mm_nt(qd, st[1])]
    return (*outs, sts[-1][0], sts[-1][1])


def _lane_masks():
    lane = lax.broadcasted_iota(jnp.int32, (1, 128), 1)
    return (lane < 64).astype(F32), (lane >= 64).astype(F32)


def _cparams(n_axes=1):
    return pltpu.CompilerParams(dimension_semantics=("arbitrary",) * n_axes, vmem_limit_bytes=VMEM_LIMIT)


def _full(a):
    nd = a.ndim
    return pl.BlockSpec(a.shape, lambda *_: (0,) * nd)


def _rows(tm, w, cb=0, rev_n=None):
    if rev_n is None:
        return pl.BlockSpec((tm, w), lambda i: (i, cb))
    return pl.BlockSpec((tm, w), lambda i: (rev_n - 1 - i, cb))


def _call(body, name, grid, in_specs, out_specs, out_shape, scratch=(), n_axes=1):
    return pl.pallas_call(body, name=name, grid=grid, in_specs=in_specs, out_specs=out_specs, out_shape=out_shape,
                          scratch_shapes=list(scratch), compiler_params=_cparams(n_axes))


def _sds(shape, dt=F32):
    return jax.ShapeDtypeStruct(shape, dt)


def norm_matmul(x, g, w, tm, name, ydt=F32):
    T, n = x.shape[0], w.shape[1]

    def body(x_ref, g_ref, w_ref, y_ref, h_ref):
        hb = rmsnorm(x_ref[...], g_ref[...]).astype(MX)
        h_ref[...] = hb
        y_ref[...] = jnp.dot(hb, w_ref[...], preferred_element_type=F32).astype(ydt)

    return _call(body, name, (T // tm,), [_rows(tm, D), _full(g), _full(w)],
                 [_rows(tm, n), _rows(tm, D)], [_sds((T, n), ydt), _sds((T, D), MX)])(x, g, w)


CPB = 8


def _chunk(c):
    return slice(c * CH, (c + 1) * CH)


def gmlp_fwd(p, ws, bs, lg, lb):
    T = p.shape[0]
    tm = CPB * CH

    def body(pa_ref, ws_ref, bs_ref, lg_ref, lb_ref, o_ref):
        params = [(ws_ref[h], bs_ref[h], lg_ref[h], lb_ref[h]) for h in range(4)]
        pieces = [[(pa_ref[_chunk(c), h * 128:(h + 1) * 128], pa_ref[_chunk(c), 512 + h * 128:512 + (h + 1) * 128])
                   for h in range(4)] for c in range(CPB)]
        out = gmlp_heads(params, pieces)
        for c in range(CPB):
            for h in range(4):
                o_ref[_chunk(c), h * 128:(h + 1) * 128] = out[c][h].astype(MX)

    return _call(body, "gmlp_fwd", (T // tm,), [_rows(tm, 1024), _full(ws), _full(bs), _full(lg), _full(lb)],
                 _rows(tm, 512), _sds((T, 512), MX))(p, ws, bs, lg, lb)


def _gla_in_specs(tm, n, rev):
    r = n if rev else None
    return [_rows(tm, 256, 4, r), _rows(tm, 256, 5, r), _rows(tm, 512, 3, r), _rows(tm, 128, 20, r)]


def gla_fwd(p, wg, bg, reverse):
    T = p.shape[0]
    tm = CPB * CH
    n = T // tm
    rev = n if reverse else None

    def body(q_ref, k_ref, v_ref, r_ref, wg_ref, bg_ref, o_ref, ss_ref, st_ref):
        @pl.when(pl.program_id(0) == 0)
        def _():
            st_ref[...] = jnp.zeros_like(st_ref)

        consts = _tri(reverse) + _lane_masks()
        order = list(reversed(range(CPB))) if reverse else list(range(CPB))
        ss_ref[0] = st_ref[...]
        for j in range(2):
            sl = slice(j * 128, (j + 1) * 128)
            v0s, v1s = slice(256 * j, 256 * j + 128), slice(256 * j + 128, 256 * j + 256)
            chunks = []
            for c in order:
                rows = _chunk(c)
                chunks += [r_ref[rows, :], q_ref[rows, sl], k_ref[rows, sl], v_ref[rows, v0s], v_ref[rows, v1s]]
            res = gla_pair(consts, wg_ref[:, sl], bg_ref[:, sl], st_ref[2 * j], st_ref[2 * j + 1], *chunks)
            for i, c in enumerate(order):
                o_ref[_chunk(c), v0s] = res[2 * i]
                o_ref[_chunk(c), v1s] = res[2 * i + 1]
            st_ref[2 * j] = res[-2]
            st_ref[2 * j + 1] = res[-1]

    ss_spec = pl.BlockSpec((1, 4, 128, 128), (lambda i: (n - 1 - i, 0, 0, 0)) if reverse else (lambda i: (i, 0, 0, 0)))
    return _call(body, "gla_fwd_r" if reverse else "gla_fwd_f", (n,),
                 _gla_in_specs(tm, n, reverse) + [_full(wg), _full(bg)],
                 [_rows(tm, 512, 0, rev), ss_spec], [_sds((T, 512)), _sds((n, 4, 128, 128))],
                 scratch=[pltpu.VMEM((4, 128, 128), F32)])(p, p, p, p, wg, bg)


def mix_out(x, of, ob, p, outa, gg, w_out, tm):
    T = x.shape[0]

    def body(x_ref, of_ref, ob_ref, pg_ref, oa_ref, gg_ref, w_ref, x1_ref, mx_ref):
        mx_ref[:, 0:512] = oa_ref[...]
        for h in range(4):
            sl = slice(h * 128, (h + 1) * 128)
            mx_ref[:, 512 + h * 128:512 + (h + 1) * 128] = outb_head(
                of_ref[:, sl] + ob_ref[:, sl], pg_ref[:, sl], gg_ref[h]).astype(MX)
        x1_ref[...] = x_ref[...] + jnp.dot(mx_ref[...], w_ref[...], preferred_element_type=F32)

    return _call(body, "mix_out", (T // tm,),
                 [_rows(tm, D), _rows(tm, 512), _rows(tm, 512), _rows(tm, 512, 4), _rows(tm, 512), _full(gg), _full(w_out)],
                 [_rows(tm, D), _rows(tm, 1024)], [_sds((T, D)), _sds((T, 1024), MX)])(x, of, ob, p, outa, gg, w_out)


HALO = 16


def _halo_specs(T, tm, w):
    nb = T // HALO
    r = tm // HALO
    return [pl.BlockSpec((tm, w), lambda i: (i, 0)),
            pl.BlockSpec((HALO, w), lambda i: (jnp.maximum(i * r - 1, 0), 0)),
            pl.BlockSpec((HALO, w), lambda i: (jnp.minimum((i + 1) * r, nb - 1), 0))]


def ffn_up_conv(x1, g, w_up, cw, cb, tm):
    T = x1.shape[0]
    ns = T // tm
    cwid = 256

    def body(x_ref, g_ref, w_ref, cw_ref, cb_ref, zu_ref, h_ref, z_ref, a_ref, prev_ref, tail_ref):
        i = pl.program_id(0)

        @pl.when(i == 0)
        def _():
            prev_ref[...] = jnp.zeros_like(prev_ref)
            tail_ref[...] = jnp.zeros_like(tail_ref)

        hb = rmsnorm(x_ref[...], g_ref[...]).astype(MX)
        h_ref[...] = hb
        row = lax.broadcasted_iota(jnp.int32, (tm, 1), 0)
        for c0 in range(0, DFF, cwid):
            z2 = []
            for cs in (slice(c0, c0 + cwid), slice(DFF + c0, DFF + c0 + cwid)):
                zub = jnp.dot(hb, w_ref[:, cs], preferred_element_type=F32).astype(MX)
                zu_ref[:, cs] = zub
                prev = prev_ref[:, cs].astype(F32)
                pr = tail_ref[HALO - 1:HALO, cs].astype(F32)
                nx = jnp.where(i < ns, zub[0:1, :].astype(F32), 0.0)
                dn = jnp.where(row == 0, pr, pltpu.roll(prev, 1, 0))
                up = jnp.where(row == tm - 1, nx, pltpu.roll(prev, tm - 1, 0))
                z = cb_ref[:, cs] + dn * cw_ref[0:1, cs] + prev * cw_ref[1:2, cs] + up * cw_ref[2:3, cs]
                z_ref[:, cs] = z.astype(MX)
                tail_ref[:, cs] = prev_ref[tm - HALO:tm, cs]
                prev_ref[:, cs] = zub
                z2.append(z)
            a_ref[:, c0:c0 + cwid] = ffn_act(z2[0], z2[1]).astype(MX)

    cur = lambda w: pl.BlockSpec((tm, w), lambda i: (jnp.minimum(i, ns - 1), 0))
    late = lambda w: pl.BlockSpec((tm, w), lambda i: (jnp.maximum(i - 1, 0), 0))
    return _call(body, "ffn_up", (ns + 1,), [cur(D), _full(g), _full(w_up), _full(cw), _full(cb)],
                 [cur(NUP), cur(D), late(NUP), late(DFF)],
                 [_sds((T, NUP), MX), _sds((T, D), MX), _sds((T, NUP), MX), _sds((T, DFF), MX)],
                 scratch=[pltpu.VMEM((tm, NUP), MX), pltpu.VMEM((HALO, NUP), MX)])(x1, g, w_up, cw, cb)


def matmul_res(a, w, res, tm, name):
    T, k = a.shape
    n = w.shape[1]

    def body(a_ref, w_ref, r_ref, o_ref):
        o_ref[...] = r_ref[...] + jnp.dot(a_ref[...], w_ref[...], preferred_element_type=F32)

    return _call(body, name, (T // tm,), [_rows(tm, k), _full(w), _rows(tm, n)], _rows(tm, n), _sds((T, n)))(a, w, res)


def loss_head(x, g, tgt, tm):
    T = x.shape[0]

    def body(x_ref, g_ref, t_ref, l_ref, dx_ref, dg_ref):
        @pl.when(pl.program_id(0) == 0)
        def _():
            l_ref[...] = jnp.zeros_like(l_ref)
            dg_ref[...] = jnp.zeros_like(dg_ref)

        y, vjp = jax.vjp(rmsnorm, x_ref[...], g_ref[...])
        err = y - t_ref[...]
        l_ref[...] += jnp.sum(err * err, axis=0, keepdims=True)
        dx, dg = vjp(err * (1.0 / D))
        dx_ref[...] = dx
        dg_ref[...] += dg

    return _call(body, "loss_head", (T // tm,), [_rows(tm, D), _full(g), _rows(tm, D)],
                 [_full(g), _rows(tm, D), _full(g)], [_sds((1, D)), _sds((T, D)), _sds((1, D))])(x, g, tgt)


def ffn_down_bwd(dx2, z, w_down, tm):
    T = dx2.shape[0]

    def body(dx_ref, z_ref, w_ref, dz_ref):
        dxb = dx_ref[...].astype(MX)
        for c0 in range(0, DFF, 256):
            gs, vs = slice(c0, c0 + 256), slice(DFF + c0, DFF + c0 + 256)
            da = _dg(dxb, w_ref[gs, :], 1, 1)
            zg, zv = z_ref[:, gs].astype(F32), z_ref[:, vs].astype(F32)
            s = sigmoid(zg)
            sz = zg * s
            dz_ref[:, gs] = (da * zv * (s + sz * (1.0 - s))).astype(MX)
            dz_ref[:, vs] = (da * sz).astype(MX)

    return _call(body, "ffn_down_bwd", (T // tm,), [_rows(tm, D), _rows(tm, NUP), _full(w_down)],
                 _rows(tm, NUP), _sds((T, NUP), MX))(dx2, z, w_down)


def ffn_up_bwd(dz, zu, cw, w_up, x1, g, dres, tm):
    T = dz.shape[0]
    ns = T // tm
    cwid = 256

    def body(dz_ref, dp_ref, dn_ref, zu_ref, cw_ref, w_ref, x_ref, g_ref, dr_ref,
             dzu_ref, dx_ref, dg_ref, dcw_ref, dcb_ref):
        i = pl.program_id(0)

        @pl.when(i == 0)
        def _():
            for r in (dg_ref, dcw_ref, dcb_ref):
                r[...] = jnp.zeros_like(r)

        row = lax.broadcasted_iota(jnp.int32, (tm, 1), 0)
        dh = jnp.zeros((tm, D), F32)
        for c0 in range(0, NUP, cwid):
            cs = slice(c0, c0 + cwid)
            dz = dz_ref[:, cs].astype(F32)
            zu = zu_ref[:, cs].astype(F32)
            pr = jnp.where(i > 0, dp_ref[HALO - 1:HALO, cs].astype(F32), 0.0)
            nx = jnp.where(i < ns - 1, dn_ref[0:1, cs].astype(F32), 0.0)
            ddn = jnp.where(row == 0, pr, pltpu.roll(dz, 1, 0))
            dup = jnp.where(row == tm - 1, nx, pltpu.roll(dz, tm - 1, 0))
            dzu = (dup * cw_ref[0:1, cs] + dz * cw_ref[1:2, cs] + ddn * cw_ref[2:3, cs]).astype(MX)
            dzu_ref[:, cs] = dzu
            dcw_ref[0:1, cs] += jnp.sum(zu * dup, axis=0, keepdims=True)
            dcw_ref[1:2, cs] += jnp.sum(zu * dz, axis=0, keepdims=True)
            dcw_ref[2:3, cs] += jnp.sum(zu * ddn, axis=0, keepdims=True)
            dcb_ref[:, cs] += jnp.sum(dz, axis=0, keepdims=True)
            dh = dh + _dg(dzu, w_ref[:, cs], 1, 1)
        _, vjp = jax.vjp(rmsnorm, x_ref[...], g_ref[...])
        dx, dg = vjp(dh)
        dx_ref[...] = dr_ref[...] + dx
        dg_ref[...] += dg

    return _call(body, "ffn_up_bwd", (ns,),
                 _halo_specs(T, tm, NUP) + [_rows(tm, NUP), _full(cw), _full(w_up), _rows(tm, D), _full(g), _rows(tm, D)],
                 [_rows(tm, NUP), _rows(tm, D), _full(g), _full(cw), pl.BlockSpec((1, NUP), lambda i: (0, 0))],
                 [_sds((T, NUP), MX), _sds((T, D)), _sds((1, D)), _sds((3, NUP)), _sds((1, NUP))])(
                     dz, dz, dz, zu, cw, w_up, x1, g, dres)


def nt_normbwd(dys, w, x, g, dres, tm, name):
    T = x.shape[0]
    n = len(dys)
    offs = [sum(d.shape[1] for d in dys[:i]) for i in range(n + 1)]

    def body(*refs):
        dy_refs, (w_ref, x_ref, g_ref, dr_ref, dx_ref, dg_ref) = refs[:n], refs[n:]

        @pl.when(pl.program_id(0) == 0)
        def _():
            dg_ref[...] = jnp.zeros_like(dg_ref)

        dh = _dg(dy_refs[0][...], w_ref[:, offs[0]:offs[1]], 1, 1)
        for i in range(1, n):
            dh = dh + _dg(dy_refs[i][...], w_ref[:, offs[i]:offs[i + 1]], 1, 1)
        _, vjp = jax.vjp(rmsnorm, x_ref[...], g_ref[...])
        dx, dg = vjp(dh)
        dx_ref[...] = dr_ref[...] + dx
        dg_ref[...] += dg

    return _call(body, name, (T // tm,),
                 [_rows(tm, d.shape[1]) for d in dys] + [_full(w), _rows(tm, D), _full(g), _rows(tm, D)],
                 [_rows(tm, D), _full(g)], [_sds((T, D)), _sds((1, D))])(*dys, w, x, g, dres)


def matmul_tn(a, b, tt, tn, name):
    T, k = a.shape
    n = b.shape[1]
    last = T // tt - 1

    def body(a_ref, b_ref, o_ref, acc_ref):
        @pl.when(pl.program_id(1) == 0)
        def _():
            acc_ref[...] = jnp.zeros_like(acc_ref)

        acc_ref[...] += _dg(a_ref[...], b_ref[...], 0, 0)

        @pl.when(pl.program_id(1) == last)
        def _():
            o_ref[...] = acc_ref[...].astype(MX)

    return _call(body, name, (n // tn, T // tt),
                 [pl.BlockSpec((tt, k), lambda j, i: (i, 0)), pl.BlockSpec((tt, tn), lambda j, i: (i, j))],
                 pl.BlockSpec((k, tn), lambda j, i: (0, j)), _sds((k, n), MX), scratch=[pltpu.VMEM((k, tn), F32)],
                 n_axes=2)(a, b)


def mix_out_bwd(dx1, w_out, of, ob, p, gg, tm):
    T = dx1.shape[0]

    def body(dx_ref, w_ref, of_ref, ob_ref, pg_ref, gg_ref, da_ref, do_ref, dpg_ref, dgg_ref):
        @pl.when(pl.program_id(0) == 0)
        def _():
            dgg_ref[...] = jnp.zeros_like(dgg_ref)

        dxb = dx_ref[...].astype(MX)
        da_ref[...] = _dg(dxb, w_ref[0:512, :], 1, 1)
        for h in range(4):
            sl = slice(h * 128, (h + 1) * 128)
            dm = _dg(dxb, w_ref[512 + h * 128:512 + (h + 1) * 128, :], 1, 1)
            _, vjp = jax.vjp(outb_head, of_ref[:, sl] + ob_ref[:, sl], pg_ref[:, sl], gg_ref[h])
            do, dpg, dg = vjp(dm)
            do_ref[:, sl] = do
            dpg_ref[:, sl] = dpg
            dgg_ref[h] += dg

    return _call(body, "mix_out_bwd", (T // tm,),
                 [_rows(tm, D), _full(w_out), _rows(tm, 512), _rows(tm, 512), _rows(tm, 512, 4), _full(gg)],
                 [_rows(tm, 512), _rows(tm, 512), _rows(tm, 512), _full(gg)],
                 [_sds((T, 512)), _sds((T, 512)), _sds((T, 512)), _sds(gg.shape)])(dx1, w_out, of, ob, p, gg)


def gla_bwd(p, wg, bg, ss, do, reverse, merge=None):
    T = p.shape[0]
    tm = CPB * CH
    n = T // tm
    rev = not reverse
    rn = n if rev else None

    def body(*refs):
        q_ref, k_ref, v_ref, r_ref, wg_ref, bg_ref, ss_ref, do_ref = refs[:8]
        if merge is None:
            dq_ref, dk_ref, dv_ref, dr_ref, dwg_ref, dbg_ref, dst_ref = refs[8:]
        else:
            mq_ref, mk_ref, mv_ref, mr_ref, mg_ref, out_ref, dwg_ref, dbg_ref, dst_ref, drs_ref = refs[8:]
            out_ref[:, 1024:1536] = mg_ref[...].astype(MX)

        @pl.when(pl.program_id(0) == 0)
        def _():
            dst_ref[...] = jnp.zeros_like(dst_ref)
            dwg_ref[...] = jnp.zeros_like(dwg_ref)
            dbg_ref[...] = jnp.zeros_like(dbg_ref)

        consts = _tri(reverse) + _lane_masks()
        order = list(reversed(range(CPB))) if reverse else list(range(CPB))
        for j in range(2):
            sl = slice(j * 128, (j + 1) * 128)
            v0s, v1s = slice(256 * j, 256 * j + 128), slice(256 * j + 128, 256 * j + 256)
            chunks, dout = [], []
            for c in order:
                rows = _chunk(c)
                chunks += [r_ref[rows, :], q_ref[rows, sl], k_ref[rows, sl], v_ref[rows, v0s], v_ref[rows, v1s]]
                dout += [do_ref[rows, v0s], do_ref[rows, v1s]]
            _, vjp = jax.vjp(functools.partial(gla_pair, consts), wg_ref[:, sl], bg_ref[:, sl],
                             ss_ref[0, 2 * j], ss_ref[0, 2 * j + 1], *chunks)
            g = vjp((*dout, dst_ref[2 * j], dst_ref[2 * j + 1]))
            dwg_ref[:, sl] += g[0]
            dbg_ref[:, sl] += g[1]
            dst_ref[2 * j] = g[2]
            dst_ref[2 * j + 1] = g[3]
            for i, c in enumerate(order):
                rows = _chunk(c)
                dr, dq, dk, dv0, dv1 = g[4 + 5 * i:9 + 5 * i]
                if merge is None:
                    if j == 0:
                        dr_ref[rows, :] = dr
                    else:
                        dr_ref[rows, :] += dr
                    dq_ref[rows, sl] = dq
                    dk_ref[rows, sl] = dk
                    dv_ref[rows, v0s] = dv0
                    dv_ref[rows, v1s] = dv1
                else:
                    if j == 0:
                        drs_ref[rows, :] = mr_ref[rows, :] + dr
                    else:
                        out_ref[rows, 1536:1664] = (drs_ref[rows, :] + dr).astype(MX)
                    out_ref[rows, sl] = (mq_ref[rows, sl] + dq).astype(MX)
                    out_ref[rows, 256 + 128 * j:384 + 128 * j] = (mk_ref[rows, sl] + dk).astype(MX)
                    out_ref[rows, 512 + 256 * j:640 + 256 * j] = (mv_ref[rows, v0s] + dv0).astype(MX)
                    out_ref[rows, 640 + 256 * j:768 + 256 * j] = (mv_ref[rows, v1s] + dv1).astype(MX)

    ss_spec = pl.BlockSpec((1, 4, 128, 128), (lambda i: (n - 1 - i, 0, 0, 0)) if rev else (lambda i: (i, 0, 0, 0)))
    ins = [p, p, p, p, wg, bg, ss, do]
    in_specs = _gla_in_specs(tm, n, rev) + [_full(wg), _full(bg), ss_spec, _rows(tm, 512, 0, rn)]
    scratch = [pltpu.VMEM((4, 128, 128), F32)]
    if merge is None:
        out_specs = [_rows(tm, 256, 0, rn), _rows(tm, 256, 0, rn), _rows(tm, 512, 0, rn), _rows(tm, 128, 0, rn)]
        out_shape = [_sds((T, 256)), _sds((T, 256)), _sds((T, 512)), _sds((T, 128))]
    else:
        ins += list(merge)
        in_specs += [_rows(tm, a.shape[1], 0, rn) for a in merge]
        out_specs, out_shape = [_rows(tm, 1664, 0, rn)], [_sds((T, 1664), MX)]
        scratch.append(pltpu.VMEM((tm, 128), F32))
    return _call(body, "gla_bwd_r" if reverse else "gla_bwd_f", (n,), in_specs, out_specs + [_full(wg), _full(bg)],
                 out_shape + [_sds(wg.shape), _sds(bg.shape)], scratch=scratch)(*ins)


def gmlp_bwd(p, douta, ws, bs, lg, lb):
    T = p.shape[0]
    cpb = 4
    tm = cpb * CH

    def body(pa_ref, do_ref, ws_ref, bs_ref, lg_ref, lb_ref, dpa_ref, dws_ref, dbs_ref, dlg_ref, dlb_ref):
        @pl.when(pl.program_id(0) == 0)
        def _():
            for r in (dws_ref, dbs_ref, dlg_ref, dlb_ref):
                r[...] = jnp.zeros_like(r)

        us = [slice(h * 128, (h + 1) * 128) for h in range(4)]
        vs = [slice(512 + h * 128, 512 + (h + 1) * 128) for h in range(4)]
        params = [(ws_ref[h], bs_ref[h], lg_ref[h], lb_ref[h]) for h in range(4)]
        pieces = [[(pa_ref[_chunk(c), us[h]], pa_ref[_chunk(c), vs[h]]) for h in range(4)] for c in range(cpb)]
        _, vjp = jax.vjp(gmlp_heads, params, pieces)
        dparams, dpieces = vjp([[do_ref[_chunk(c), us[h]] for h in range(4)] for c in range(cpb)])
        for h in range(4):
            for r, a in zip((dws_ref, dbs_ref, dlg_ref, dlb_ref), dparams[h]):
                r[h] += a
            for c in range(cpb):
                dpa_ref[_chunk(c), us[h]] = dpieces[c][h][0].astype(MX)
                dpa_ref[_chunk(c), vs[h]] = dpieces[c][h][1].astype(MX)

    return _call(body, "gmlp_bwd", (T // tm,),
                 [_rows(tm, 1024), _rows(tm, 512), _full(ws), _full(bs), _full(lg), _full(lb)],
                 [_rows(tm, 1024), _full(ws), _full(bs), _full(lg), _full(lb)],
                 [_sds((T, 1024), MX), _sds(ws.shape), _sds(bs.shape), _sds(lg.shape), _sds(lb.shape)])(p, douta, ws, bs, lg, lb)


def _gate_pad(w, row0):
    return jnp.zeros((128, 256), F32).at[row0:row0 + 16].set(w)


def local_step(x, tgt, W, get_big, emit, tm=256, tmm=512):
    saved = []
    for l in range(NL):
        s = {"x": x}
        s["w_in"] = get_big(l, 0, x)
        p, s["h"] = norm_matmul(x, W["g_mix"][l][None], s["w_in"], tmm, "mix_in")
        s["p"] = p
        ws, bs = W["w_s"][l], W["b_s"][l][:, :, None]
        lg, lb = W["ln_g"][l][:, None, :], W["ln_b"][l][:, None, :]
        outa = gmlp_fwd(p, ws, bs, lg, lb)
        wgf, wgb = _gate_pad(W["w_gate_f"][l], 0), _gate_pad(W["w_gate_b"][l], 16)
        bgf, bgb = W["b_gate_f"][l][None], W["b_gate_b"][l][None]
        s["of"], s["ssf"] = gla_fwd(p, wgf, bgf, False)
        s["ob"], s["ssb"] = gla_fwd(p, wgb, bgb, True)
        s["w_out"] = get_big(l, 1, s["ob"])
        gg = W["g_gla"][l][:, None, :]
        x1, s["mixed"] = mix_out(x, s["of"], s["ob"], p, outa, gg, s["w_out"], min(1024, tmm * 2))
        s["x1"] = x1
        s["w_up"] = get_big(l, 2, x1)
        s["zu"], s["h2"], s["z"], s["a"] = ffn_up_conv(x1, W["g_ffn"][l][None], s["w_up"], W["conv_w"][l],
                                                       W["conv_b"][l][None], tm // 2)
        s["w_down"] = get_big(l, 3, s["a"])
        x = matmul_res(s["a"], s["w_down"], x1, min(1024, tmm * 2), "ffn_down")
        saved.append(s)

    lsum, dx, dgf = loss_head(x, W["g_final"][None], tgt, tmm)
    G = {k: [None] * NL for k in _SMALL if k != "g_final"}
    tok = jnp.zeros((1, 1), F32)
    for l in reversed(range(NL)):
        s = saved[l]
        g_down = matmul_tn(s["a"], dx, min(1024, tmm * 2), 1024, "dw_down")
        dz = ffn_down_bwd(dx, s["z"], s["w_down"], tmm)
        dzu, dx1, dg, G["conv_w"][l], dcb = ffn_up_bwd(dz, s["zu"], W["conv_w"][l] + tok, s["w_up"], s["x1"],
                                                       W["g_ffn"][l][None], dx, tm)
        G["conv_b"][l], G["g_ffn"][l] = dcb[0], dg[0]
        g_up = matmul_tn(s["h2"], dzu, min(1024, tmm * 2), 1408, "dw_up")
        tok = emit(l, "A", {"w_down": g_down, "w_up": g_up})
        g_out = matmul_tn(s["mixed"], dx1, min(1024, tmm * 2), 1024, "dw_out")
        gg = W["g_gla"][l][:, None, :] + tok
        douta, do, dpg, dgg = mix_out_bwd(dx1, s["w_out"], s["of"], s["ob"], s["p"], gg, min(1024, tmm * 2))
        G["g_gla"][l] = dgg[:, 0, :]
        wgf, wgb = _gate_pad(W["w_gate_f"][l], 0), _gate_pad(W["w_gate_b"][l], 16)
        bgf, bgb = W["b_gate_f"][l][None], W["b_gate_b"][l][None]
        dqf, dkf, dvf, drf, dwgf, dbgf = gla_bwd(s["p"], wgf, bgf, s["ssf"], do, False)
        dpb, dwgb, dbgb = gla_bwd(s["p"], wgb, bgb, s["ssb"], do, True, merge=(dqf, dkf, dvf, drf, dpg))
        G["w_gate_f"][l], G["b_gate_f"][l] = dwgf[0:16], dbgf[0]
        G["w_gate_b"][l], G["b_gate_b"][l] = dwgb[16:32], dbgb[0]
        ws, bs = W["w_s"][l], W["b_s"][l][:, :, None]
        lg, lb = W["ln_g"][l][:, None, :], W["ln_b"][l][:, None, :]
        dpa, G["w_s"][l], dbs, dlg, dlb = gmlp_bwd(s["p"], douta, ws, bs, lg, lb)
        G["b_s"][l], G["ln_g"][l], G["ln_b"][l] = dbs[:, :, 0], dlg[:, 0, :], dlb[:, 0, :]
        tt = min(1024, tmm * 2)
        g_in = jnp.concatenate([matmul_tn(s["h"], dpa, tt, 1024, "dw_in_a"),
                                matmul_tn(s["h"], dpb, tt, 1664, "dw_in_b")], axis=1)
        tok = emit(l, "B", {"w_out": g_out, "w_in": g_in})
        dx, dg = nt_normbwd([dpa, dpb], s["w_in"], s["x"], W["g_mix"][l][None] + tok, dx1, tmm, "mix_in_bwd")
        G["g_mix"][l] = dg[0]
    G = {k: jnp.stack(v) for k, v in G.items()}
    G["g_final"] = dgf[0]
    return lsum, dx, G


def _rows3(tr, c):
    return pl.BlockSpec((None, tr, c), lambda l, i: (l, i, 0))


def cast_bf16(a, tr):
    nl, r, c = a.shape

    def body(a_ref, o_ref):
        o_ref[...] = a_ref[...].astype(BF16)

    return _call(body, "cast_bf16", (nl, r // tr), [_rows3(tr, c)], _rows3(tr, c), _sds(a.shape, BF16), n_axes=2)(a)


def sum_parts(land, grad, k, chipvec, tr):
    _, rr, cc = land.shape
    nb = rr // tr

    def body(c_ref, l_ref, g_ref, o_ref):
        mine = g_ref[...].astype(F32)
        acc = None
        for j in range(4):
            part = jnp.where(c_ref[0] == j, mine, l_ref[j].astype(F32))
            acc = part if acc is None else acc + part
        o_ref[...] = acc

    gs = pltpu.PrefetchScalarGridSpec(
        num_scalar_prefetch=1, grid=(nb,),
        in_specs=[pl.BlockSpec((4, tr, cc), lambda i, c: (0, i, 0)), _part_spec(k, tr, nb)],
        out_specs=pl.BlockSpec((tr, cc), lambda i, c: (i, 0)))
    return pl.pallas_call(body, name="sum_parts", grid_spec=gs, out_shape=_sds((rr, cc)),
                          compiler_params=_cparams(1))(chipvec, land, grad)


def adamw(w, ga, gb, m, v, tr):
    nl, r, c = w.shape

    def body(w_ref, ga_ref, gb_ref, m_ref, v_ref, g_ref, d_ref, nm_ref, nv_ref):
        gr = ga_ref[...] + gb_ref[...]
        g_ref[...] = gr
        nm = ADAM_B1 * m_ref[...] + (1.0 - ADAM_B1) * gr
        nv = ADAM_B2 * v_ref[...] + (1.0 - ADAM_B2) * jnp.square(gr)
        m_hat = nm * (1.0 / (1.0 - ADAM_B1 ** ADAM_STEP))
        v_hat = nv * (1.0 / (1.0 - ADAM_B2 ** ADAM_STEP))
        d_ref[...] = -ADAM_LR * (m_hat / (jnp.sqrt(v_hat) + ADAM_EPS) + ADAM_WD * w_ref[...])
        nm_ref[...] = nm
        nv_ref[...] = nv

    sp = _rows3(tr, c)
    return _call(body, "adamw", (nl, r // tr), [sp] * 5, [sp] * 4, [_sds(w.shape)] * 4, n_axes=2)(w, ga, gb, m, v)


MESH = pl.DeviceIdType.MESH
ANY = pl.BlockSpec(memory_space=pl.ANY)
N_BIG = 4


def _pos():
    return lax.axis_index("x"), lax.axis_index("y"), lax.axis_index("c")


def _other_chips(x, y):
    return [(1 - x, y), (x, 1 - y), (1 - x, 1 - y)]


def _slab(k, ref, j):
    if k == 0:
        return ref.at[j]
    if k == 1:
        return ref.at[pl.ds(256 * j, 256), :]
    if k == 2:
        return ref.at[:, pl.ds(1408 * j, 1408)]
    return ref.at[pl.ds(704 * j, 704), :]


_LAYER_FULL = [(4, 1024, 648), (1024, 1024), (1024, NUP), (DFF, 1024)]
_LAYER_SHARD = [(1024, 648), (256, 1024), (1024, 1408), (704, 1024)]

HBM = pl.BlockSpec(memory_space=pltpu.HBM)
SEM = pl.BlockSpec(memory_space=pltpu.SEMAPHORE)
VM = pl.BlockSpec(memory_space=pltpu.VMEM)
EFFECT = pltpu.SideEffectType.DATAFLOW_SIDE_EFFECTING
_GW_GROUPS = [[(0, k)] for k in range(N_BIG)] + [[(l, k) for k in range(N_BIG)] for l in range(1, NL)]
_GW_ORDER = [lk for g in _GW_GROUPS for lk in g]


def _hbm(a):
    return pltpu.with_memory_space_constraint(a, pltpu.HBM)


def _hbm_like(a):
    return pltpu.HBM(a.shape, a.dtype)


def _part_spec(k, tr, nb):
    cc = _LAYER_SHARD[k][1]
    if k == 0:
        return pl.BlockSpec((None, tr, cc), lambda i, c: (c[0], i, 0))
    if k == 2:
        return pl.BlockSpec((tr, cc), lambda i, c: (i, c[0]))
    return pl.BlockSpec((tr, cc), lambda i, c: (c[0] * nb + i, 0))


def place_own(shard, landing, l, k, chipvec, tr):
    rr, cc = _LAYER_SHARD[k]
    nb = rr // tr

    def body(c_ref, s_ref, l_ref, o_ref):
        o_ref[...] = s_ref[...]

    gs = pltpu.PrefetchScalarGridSpec(
        num_scalar_prefetch=1, grid=(nb,),
        in_specs=[pl.BlockSpec((None, tr, cc), lambda i, c: (l, i, 0)), ANY], out_specs=_part_spec(k, tr, nb))
    return pl.pallas_call(body, name="place_own", grid_spec=gs, out_shape=_sds(landing.shape, landing.dtype),
                          input_output_aliases={2: 0}, compiler_params=_cparams(1))(chipvec, shard, landing)


def gw_start(shards, landings, after):
    n = len(_GW_ORDER)

    def body(*refs):
        S, Ld = refs[:N_BIG], refs[N_BIG:N_BIG + n]
        outs = refs[N_BIG + n + 1:]
        send_sems, recv, token = outs[0], outs[1:1 + len(_GW_GROUPS)], outs[-1]
        x, y, c = _pos()
        me = 2 * x + y
        ci = 0
        for gi, grp in enumerate(_GW_GROUPS):
            for t, (l, k) in enumerate(grp):
                land = Ld[_GW_ORDER.index((l, k))]
                for j, (px, py) in enumerate(_other_chips(x, y)):
                    pltpu.make_async_remote_copy(
                        src_ref=S[k].at[l], dst_ref=_slab(k, land, me), send_sem=send_sems.at[ci],
                        recv_sem=recv[gi].at[3 * t + j], device_id=(px, py, c), device_id_type=MESH).start()
                    ci += 1
        token[...] = jnp.zeros_like(token)

    ins = list(shards) + list(landings)
    sems = [pltpu.SemaphoreType.DMA((3 * n,))] + [pltpu.SemaphoreType.DMA((3 * len(g),)) for g in _GW_GROUPS]
    outs = pl.pallas_call(
        body, name="gw_start", out_shape=sems + [_hbm_like(a) for a in ins] + [_sds((8, 128))],
        in_specs=[HBM] * len(ins) + [pl.BlockSpec(memory_space=pl.ANY)],
        out_specs=[SEM] * len(sems) + [HBM] * len(ins) + [VM],
        input_output_aliases={i: len(sems) + i for i in range(len(ins))},
        compiler_params=pltpu.CompilerParams(has_side_effects=EFFECT))(*[_hbm(a) for a in ins], after)
    ns = len(sems)
    return outs[0], outs[1:ns], outs[ns:ns + N_BIG], outs[ns + N_BIG:ns + len(ins)], outs[-1]


def gw_wait(gi, landings, recv_sems, after, shards=None, send_sems=None):
    grp = _GW_GROUPS[gi]
    n = len(grp)
    last = shards is not None

    def body(*refs):
        Ld, rs = refs[:n], refs[n]
        x, y, c = _pos()
        for t, (l, k) in enumerate(grp):
            for j, (px, py) in enumerate(_other_chips(x, y)):
                region = _slab(k, Ld[t], 2 * px + py)
                pltpu.make_async_remote_copy(src_ref=region, dst_ref=region, send_sem=rs.at[3 * t + j],
                                             recv_sem=rs.at[3 * t + j], device_id=(px, py, c),
                                             device_id_type=MESH).wait_recv()
        if last:
            S, ss = refs[n + 2:n + 2 + N_BIG], refs[n + 2 + N_BIG]
            me = 2 * x + y
            for ci, (l, k) in enumerate(lk for lk in _GW_ORDER for _ in range(3)):
                pltpu.make_async_remote_copy(src_ref=S[k].at[l], dst_ref=_slab(k, Ld[k], me), send_sem=ss.at[ci],
                                             recv_sem=ss.at[ci], device_id=(x, y, c), device_id_type=MESH).wait_send()

    ins = list(landings) + [recv_sems, after]
    specs = [HBM] * n + [SEM, pl.BlockSpec(memory_space=pl.ANY)]
    outs = [_hbm_like(a) for a in landings]
    alias = {i: i for i in range(n)}
    if last:
        ins += list(shards) + [send_sems]
        specs += [HBM] * N_BIG + [SEM]
        outs += [_hbm_like(a) for a in shards]
        alias.update({n + 2 + i: n + i for i in range(N_BIG)})
    res = pl.pallas_call(body, name="gw_wait_%d" % gi, out_shape=outs, in_specs=specs, out_specs=[HBM] * len(outs),
                         input_output_aliases=alias,
                         compiler_params=pltpu.CompilerParams(has_side_effects=EFFECT))(*ins)
    return res[:n], (res[n:] if last else None)


def ga_start(tag, ks, grads, landings):
    n = len(ks)

    def body(*refs):
        G, Ld = refs[:n], refs[n:2 * n]
        send_sems, recv_sems, token = refs[2 * n], refs[2 * n + 1], refs[-1]
        x, y, c = _pos()
        me = 2 * x + y
        for t, k in enumerate(ks):
            for j, (px, py) in enumerate(_other_chips(x, y)):
                pltpu.make_async_remote_copy(
                    src_ref=_slab(k, G[t], 2 * px + py), dst_ref=Ld[t].at[me], send_sem=send_sems.at[3 * t + j],
                    recv_sem=recv_sems.at[3 * t + j], device_id=(px, py, c), device_id_type=MESH).start()
        token[...] = jnp.zeros_like(token)

    ins = list(grads) + list(landings)
    sems = [pltpu.SemaphoreType.DMA((3 * n,))] * 2
    outs = pl.pallas_call(
        body, name="ga_start_" + tag, out_shape=sems + [_hbm_like(a) for a in ins] + [_sds((8, 128))],
        in_specs=[HBM] * len(ins), out_specs=[SEM, SEM] + [HBM] * len(ins) + [VM],
        input_output_aliases={i: 2 + i for i in range(len(ins))},
        compiler_params=pltpu.CompilerParams(has_side_effects=EFFECT))(*[_hbm(a) for a in ins])
    return outs[0], outs[1], outs[2:2 + n], outs[2 + n:2 + 2 * n], outs[-1]


def ga_wait(tag, ks, send_sems, recv_sems, grads, landings, after):
    n = len(ks)

    def body(*refs):
        G, Ld, ss, rs = refs[:n], refs[n:2 * n], refs[2 * n], refs[2 * n + 1]
        x, y, c = _pos()
        me = 2 * x + y
        for t, k in enumerate(ks):
            for j, (px, py) in enumerate(_other_chips(x, y)):
                pj = 2 * px + py
                cp = pltpu.make_async_remote_copy(
                    src_ref=_slab(k, G[t], pj), dst_ref=Ld[t].at[pj], send_sem=ss.at[3 * t + j],
                    recv_sem=rs.at[3 * t + j], device_id=(px, py, c), device_id_type=MESH)
                cp.wait_send()
                cp.wait_recv()

    ins = list(grads) + list(landings) + [send_sems, recv_sems, after]
    res = pl.pallas_call(
        body, name="ga_wait_" + tag, out_shape=[_hbm_like(a) for a in list(grads) + list(landings)],
        in_specs=[HBM] * (2 * n) + [SEM, SEM, pl.BlockSpec(memory_space=pl.ANY)], out_specs=[HBM] * (2 * n),
        input_output_aliases={i: i for i in range(2 * n)},
        compiler_params=pltpu.CompilerParams(has_side_effects=EFFECT))(*ins)
    return res[:n], res[n:]


def swap_start(parts):
    n = len(parts)

    def body(*refs):
        Q, Ld, sems = refs[:n], refs[n:2 * n], refs[2 * n:4 * n]
        x, y, c = _pos()
        for k in range(n):
            pltpu.make_async_remote_copy(src_ref=Q[k], dst_ref=Ld[k], send_sem=sems[k].at[0], recv_sem=sems[n + k].at[0],
                                         device_id=(x, y, 1 - c), device_id_type=MESH).start()
        refs[-1][...] = jnp.zeros_like(refs[-1])

    ins = list(parts) + [lax.empty(p.shape, p.dtype) for p in parts]
    outs = pl.pallas_call(
        body, name="swap_start",
        out_shape=[pltpu.SemaphoreType.DMA((1,))] * (2 * n) + [_hbm_like(a) for a in ins] + [_sds((8, 128))],
        in_specs=[HBM] * (2 * n), out_specs=[SEM] * (2 * n) + [HBM] * (2 * n) + [VM],
        input_output_aliases={i: 2 * n + i for i in range(2 * n)},
        compiler_params=pltpu.CompilerParams(has_side_effects=EFFECT))(*[_hbm(a) for a in ins])
    return outs[:n], outs[n:2 * n], outs[2 * n:3 * n], outs[3 * n:4 * n]


def swap_wait(k, send_sem, recv_sem, part, landing, after):
    def body(q_ref, l_ref, ss, rs, after_ref, q_out, l_out):
        x, y, c = _pos()
        cp = pltpu.make_async_remote_copy(src_ref=q_ref, dst_ref=l_ref, send_sem=ss.at[0], recv_sem=rs.at[0],
                                          device_id=(x, y, 1 - c), device_id_type=MESH)
        cp.wait_send()
        cp.wait_recv()

    return pl.pallas_call(
        body, name="swap_wait_%d" % k, out_shape=[_hbm_like(part), _hbm_like(landing)],
        in_specs=[HBM, HBM, SEM, SEM, pl.BlockSpec(memory_space=pl.ANY)], out_specs=[HBM, HBM],
        input_output_aliases={0: 0, 1: 1},
        compiler_params=pltpu.CompilerParams(has_side_effects=EFFECT))(part, landing, send_sem, recv_sem, after)


def _peer(x, y, c, r):
    fx, fy, fc = (r >> 2) & 1, (r >> 1) & 1, r & 1
    return ((1 - x) if fx else x, (1 - y) if fy else y, (1 - c) if fc else c)


def ag_start(tag, pack):
    rr, cc = pack.shape

    def body(p_ref, l_ref, ss, rs, p_out, l_out, token):
        x, y, c = _pos()
        me = 4 * x + 2 * y + c
        for r in range(1, 8):
            pltpu.make_async_remote_copy(src_ref=p_ref, dst_ref=l_ref.at[me], send_sem=ss.at[r - 1], recv_sem=rs.at[r - 1],
                                         device_id=_peer(x, y, c, r), device_id_type=MESH).start()
        token[...] = jnp.zeros_like(token)

    outs = pl.pallas_call(
        body, name="ag_start_" + tag,
        out_shape=[pltpu.SemaphoreType.DMA((7,)), pltpu.SemaphoreType.DMA((7,)), _hbm_like(pack),
                   pltpu.HBM((8, rr, cc), pack.dtype), _sds((8, 128))],
        in_specs=[HBM, HBM], out_specs=[SEM, SEM, HBM, HBM, VM], input_output_aliases={0: 2, 1: 3},
        compiler_params=pltpu.CompilerParams(has_side_effects=EFFECT))(_hbm(pack), _hbm(lax.empty((8, rr, cc), pack.dtype)))
    return outs


def ag_wait(tag, send_sems, recv_sems, pack, landing, after):
    def body(p_ref, l_ref, ss, rs, after_ref, p_out, l_out):
        x, y, c = _pos()
        for r in range(1, 8):
            px, py, pc = _peer(x, y, c, r)
            cp = pltpu.make_async_remote_copy(src_ref=p_ref, dst_ref=l_ref.at[4 * px + 2 * py + pc], send_sem=ss.at[r - 1],
                                              recv_sem=rs.at[r - 1], device_id=(px, py, pc), device_id_type=MESH)
            cp.wait_send()
            cp.wait_recv()

    return pl.pallas_call(
        body, name="ag_wait_" + tag, out_shape=[_hbm_like(pack), _hbm_like(landing)],
        in_specs=[HBM, HBM, SEM, SEM, pl.BlockSpec(memory_space=pl.ANY)], out_specs=[HBM, HBM],
        input_output_aliases={0: 0, 1: 1},
        compiler_params=pltpu.CompilerParams(has_side_effects=EFFECT))(pack, landing, send_sems, recv_sems, after)


def sum_slots(landing, own, mevec):
    _, rr, cc = landing.shape

    def body(m_ref, l_ref, o_ref, out_ref):
        mine = o_ref[...]
        acc = None
        for j in range(8):
            part = jnp.where(m_ref[0] == j, mine, l_ref[j])
            acc = part if acc is None else acc + part
        out_ref[...] = acc

    gs = pltpu.PrefetchScalarGridSpec(
        num_scalar_prefetch=1, grid=(1,),
        in_specs=[pl.BlockSpec((8, rr, cc), lambda i, m: (0, 0, 0)), pl.BlockSpec((rr, cc), lambda i, m: (0, 0))],
        out_specs=pl.BlockSpec((rr, cc), lambda i, m: (0, 0)))
    return pl.pallas_call(body, name="sum_slots", grid_spec=gs, out_shape=_sds((rr, cc)),
                          compiler_params=_cparams(1))(mevec, landing, own)


_WEIGHTS = ['g_mix', 'w_in', 'w_s', 'b_s', 'ln_g', 'ln_b', 'w_gate_f', 'b_gate_f', 'w_gate_b', 'b_gate_b', 'g_gla',
            'w_out', 'g_ffn', 'w_up', 'conv_w', 'conv_b', 'w_down', 'g_final']
_BIG = ['w_in', 'w_out', 'w_up', 'w_down']
_SMALL = [n for n in _WEIGHTS if n not in _BIG]
_SMALL_SHARDED = {'w_gate_f': 64, 'w_gate_b': 64, 'conv_w': 1408}
_BIG_TR = {'w_in': 512, 'w_out': 256, 'w_up': 256, 'w_down': 352}


def _pack(arrs):
    flat = jnp.concatenate([a.reshape(-1) for a in arrs])
    pad = (-flat.shape[0]) % 1024
    return jnp.pad(flat, (0, pad)).reshape(-1, 128)


def _unpack(buf, shapes):
    flat = buf.reshape(-1)
    out, o = [], 0
    for s in shapes:
        n = 1
        for d in s:
            n *= d
        out.append(flat[o:o + n].reshape(s))
        o += n
    return out


def kernel(x, g_mix, w_in, w_s, b_s, ln_g, ln_b, w_gate_f, b_gate_f, w_gate_b, b_gate_b, g_gla, w_out, g_ffn, w_up, conv_w, conv_b, w_down, g_final, loss_target, m_g_mix, m_w_in, m_w_s, m_b_s, m_ln_g, m_ln_b, m_w_gate_f, m_b_gate_f, m_w_gate_b, m_b_gate_b, m_g_gla, m_w_out, m_g_ffn, m_w_up, m_conv_w, m_conv_b, m_w_down, m_g_final, v_g_mix, v_w_in, v_w_s, v_b_s, v_ln_g, v_ln_b, v_w_gate_f, v_b_gate_f, v_w_gate_b, v_b_gate_b, v_g_gla, v_w_out, v_g_ffn, v_w_up, v_conv_w, v_conv_b, v_w_down, v_g_final):
    loc = locals()
    w = {n: loc[n] for n in _WEIGHTS}
    m = {n: loc["m_" + n] for n in _WEIGHTS}
    v = {n: loc["v_" + n] for n in _WEIGHTS}
    xi, yi, ci = _pos()
    chip = 2 * xi + yi
    me = 2 * chip + ci
    mevec = jnp.reshape(me, (1,)).astype(jnp.int32)

    sh_names = list(_SMALL_SHARDED)
    ss_w, rs_w, pk_w, land_w, tok_w = ag_start("w", _pack([w[n] for n in sh_names]))

    shards = [cast_bf16(w[n], _BIG_TR[n]) for n in _BIG]
    chipvec = jnp.reshape(chip, (1,)).astype(jnp.int32)
    send_sems, recv_sems, shards_fly, landings_fly, started = gw_start(
        shards, [lax.empty(_LAYER_FULL[k], BF16) for _, k in _GW_ORDER], tok_w)
    own = {"shards": shards_fly}

    pk_w, land_w = ag_wait("w", ss_w, rs_w, pk_w, land_w, started)
    per_chip = [_unpack(jnp.where(me == 2 * j, pk_w, land_w[2 * j]), [w[n].shape for n in sh_names]) for j in range(4)]
    W = dict(w)
    for k, n in enumerate(sh_names):
        W[n] = jnp.concatenate([per_chip[j][k] for j in range(4)], axis=-1)
    arrived = {}

    def get_big(l, k, after):
        if (l, k) not in arrived:
            gi = next(i for i, g in enumerate(_GW_GROUPS) if (l, k) in g)
            lo = sum(len(g) for g in _GW_GROUPS[:gi])
            lands = landings_fly[lo:lo + len(_GW_GROUPS[gi])]
            if gi == len(_GW_GROUPS) - 1:
                full, own["shards"] = gw_wait(gi, lands, recv_sems[gi], after, shards_fly, send_sems)
            else:
                full, _ = gw_wait(gi, lands, recv_sems[gi], after)
            for (gl, gk), a in zip(_GW_GROUPS[gi], full):
                arrived[(gl, gk)] = place_own(own["shards"][gk], a, gl, gk, chipvec, _BIG_TR[_BIG[gk]])
        if k == 0:
            f_in = jnp.transpose(arrived[(l, 0)], (1, 0, 2)).reshape(D, N_IN)
            return jnp.pad(f_in, ((0, 0), (0, N_INP - N_IN)))
        return arrived[(l, k)]

    flying = []

    def emit(l, group, grads):
        ks = [3, 2] if group == "A" else [1, 0]
        gs = [grads[_BIG[k]] for k in ks]
        if group == "B":
            gs[1] = jnp.transpose(gs[1][:, :N_IN].reshape(D, 4, 648), (1, 0, 2))
        lands = [lax.empty((4,) + _LAYER_SHARD[k], BF16) for k in ks]
        tag = "%d%s" % (l, group)
        ss, rs, gs_fly, lands_fly, tok = ga_start(tag, ks, gs, lands)
        flying.append((tag, l, ks, ss, rs, gs_fly, lands_fly))
        return tok[0:1, 0:1]

    lsum, grad_x, G = local_step(x[0], loss_target[0], W, get_big, emit)

    small_shapes = [G[n].shape for n in _SMALL] + [(D,)]
    ss_g, rs_g, pk_g, land_g, started = ag_start("g", _pack([G[n] for n in _SMALL] + [lsum]))

    plane = [[None] * NL for _ in range(N_BIG)]
    for tag, l, ks, ss, rs, gs_fly, lands_fly in flying:
        for k, g, a in zip(ks, *ga_wait(tag, ks, ss, rs, gs_fly, lands_fly, started)):
            plane[k][l] = sum_parts(a, g, k, chipvec, _BIG_TR[_BIG[k]])
    ss_p, rs_p, plane_fly, other_fly = swap_start([jnp.stack(p) for p in plane])
    grads, delta, new_m, new_v = {}, {}, {}, {}
    after = grad_x
    for k in (1, 0, 3, 2):
        n = _BIG[k]
        mine, other = swap_wait(k, ss_p[k], rs_p[k], plane_fly[k], other_fly[k], after)
        grads[n], delta[n], new_m[n], new_v[n] = adamw(w[n], mine, other, m[n], v[n], _BIG_TR[n])
        after = delta[n]

    pk_g, land_g = ag_wait("g", ss_g, rs_g, pk_g, land_g, after)
    small = dict(zip(_SMALL + ["lsum"], _unpack(sum_slots(land_g, pk_g, mevec), small_shapes)))
    loss = 0.5 * jnp.sum(small.pop("lsum")) / D
    for n, wd in _SMALL_SHARDED.items():
        small[n] = lax.dynamic_slice_in_dim(small[n], chip * wd, wd, axis=small[n].ndim - 1)
    grads.update(small)
    shapes = [w[n].shape for n in _SMALL]
    pw, pg, pm, pv = (_pack([t[n] for n in _SMALL])[None] for t in (w, grads, m, v))
    _, d_, m_, v_ = adamw(pw, pg, jnp.zeros_like(pg), pm, pv, pw.shape[1])
    for t, buf in ((delta, d_), (new_m, m_), (new_v, v_)):
        t.update(zip(_SMALL, _unpack(buf, shapes)))

    return (loss, grad_x[None], *[grads[n] for n in _WEIGHTS], *[delta[n] for n in _WEIGHTS],
            *[new_m[n] for n in _WEIGHTS], *[new_v[n] for n in _WEIGHTS])
```
